```python
import math
import jax, jax.numpy as jnp
from jax import lax
import numpy as np

D_MODEL = 1024
BATCH = 8
SEQ = 16384
DEPTH = 2

CHUNK = 64
N_MIXERS = 2
EPS = 1e-6
MLA_HEADS = 8
QK_NOPE = 128
QK_ROPE = 64
V_HEAD = 128
Q_LORA = 384
KV_LORA = 256
ROPE_THETA = 10000.0
Q_BLOCK = 128
SSM_WIDTH = D_MODEL
SSM_GROUP = 16
SSM_GROUPS = SSM_WIDTH // SSM_GROUP
SSM_STATE = 64
DT_MIN = 1e-3
DT_MAX = 1e-1
D_FF = 2816
CONV_W = 3

N_MLA_LAYERS = (DEPTH + 1) // 2
N_SSM_LAYERS = DEPTH // 2

kernel_name = "hybrid_mla_s5_convffn_stream_encoder"


def rmsnorm(x, g):
    xf = x.astype(jnp.float32)
    y = xf * lax.rsqrt(jnp.mean(xf * xf, axis=-1, keepdims=True) + EPS)
    return (y * g.astype(jnp.float32)).astype(x.dtype)


def rope_tables(positions):
    inv = 1.0 / (ROPE_THETA ** (jnp.arange(0, QK_ROPE, 2, dtype=jnp.float32) / QK_ROPE))
    ang = positions.astype(jnp.float32)[..., None] * inv
    return jnp.cos(ang), jnp.sin(ang)


def apply_rope(x, cos, sin):
    xf = x.astype(jnp.float32)
    x1, x2 = jnp.split(xf, 2, axis=-1)
    return jnp.concatenate([x1 * cos - x2 * sin, x1 * sin + x2 * cos], axis=-1).astype(x.dtype)


def mla_mixer(h, cos, sin, w_a, g_q, g_kv, w_uq, w_ukv, w_o):
    B, S, _ = h.shape
    H = MLA_HEADS
    a = h @ w_a
    c_q, c_kv, k_rope = jnp.split(a, [Q_LORA, Q_LORA + KV_LORA], axis=-1)
    q = (rmsnorm(c_q, g_q) @ w_uq).reshape(B, S, H, QK_NOPE + QK_ROPE)
    q_nope = q[..., :QK_NOPE]
    q_rope = apply_rope(q[..., QK_NOPE:], cos[:, :, None, :], sin[:, :, None, :])
    k_rope = apply_rope(k_rope, cos, sin)
    kv = (rmsnorm(c_kv, g_kv) @ w_ukv).reshape(B, S, H, QK_NOPE + V_HEAD)
    k_nope = kv[..., :QK_NOPE]
    v = kv[..., QK_NOPE:]
    scale = (QK_NOPE + QK_ROPE) ** -0.5
    nb = S // Q_BLOCK
    key_chunk = jnp.arange(S) // CHUNK

    def block(args):
        qn, qr, i = args
        s = (jnp.einsum('bqhd,bkhd->bhqk', qn, k_nope)
             + jnp.einsum('bqhr,bkr->bhqk', qr, k_rope)).astype(jnp.float32) * scale
        q_chunk = (i * Q_BLOCK + jnp.arange(Q_BLOCK)) // CHUNK
        mask = key_chunk[None, :] <= q_chunk[:, None]
        s = jnp.where(mask[None, None], s, -jnp.inf)
        p = jax.nn.softmax(s, axis=-1).astype(v.dtype)
        return jnp.einsum('bhqk,bkhd->bqhd', p, v)

    qn_b = q_nope.reshape(B, nb, Q_BLOCK, H, QK_NOPE).transpose(1, 0, 2, 3, 4)
    qr_b = q_rope.reshape(B, nb, Q_BLOCK, H, QK_ROPE).transpose(1, 0, 2, 3, 4)
    o = lax.map(block, (qn_b, qr_b, jnp.arange(nb)))
    o = o.transpose(1, 0, 2, 3, 4).reshape(B, S, H * V_HEAD)
    return o @ w_o


def s5_mixer(h, w_in, lam_re, lam_im, log_dt, b_re, b_im, c_re, c_im, d_skip, w_glu):
    B, S, _ = h.shape
    G, P, C = SSM_GROUPS, SSM_STATE, SSM_GROUP
    u = (h @ w_in).astype(jnp.float32).reshape(B, S, G, C)
    dt = jnp.exp(log_dt.astype(jnp.float32))[:, None]
    lr = lam_re.astype(jnp.float32)
    li = lam_im.astype(jnp.float32)
    mag = jnp.exp(lr * dt)
    ar = mag * jnp.cos(li * dt)
    ai = mag * jnp.sin(li * dt)
    den = lr * lr + li * li
    nr = ar - 1.0
    coef_r = (nr * lr + ai * li) / den
    coef_i = (ai * lr - nr * li) / den
    br = b_re.astype(jnp.float32)
    bi = b_im.astype(jnp.float32)
    bbar_r = coef_r[..., None] * br - coef_i[..., None] * bi
    bbar_i = coef_r[..., None] * bi + coef_i[..., None] * br
    bu_r = jnp.einsum('bsgc,gpc->bsgp', u, bbar_r)
    bu_i = jnp.einsum('bsgc,gpc->bsgp', u, bbar_i)
    a_r = jnp.broadcast_to(ar, bu_r.shape)
    a_i = jnp.broadcast_to(ai, bu_i.shape)

    def combine(e1, e2):
        a1r, a1i, b1r, b1i = e1
        a2r, a2i, b2r, b2i = e2
        return (a2r * a1r - a2i * a1i,
                a2r * a1i + a2i * a1r,
                a2r * b1r - a2i * b1i + b2r,
                a2r * b1i + a2i * b1r + b2i)

    _, _, xr, xi = lax.associative_scan(combine, (a_r, a_i, bu_r, bu_i), axis=1)
    y = (jnp.einsum('bsgp,gcp->bsgc', xr, c_re.astype(jnp.float32))
         - jnp.einsum('bsgp,gcp->bsgc', xi, c_im.astype(jnp.float32)))
    y = y.reshape(B, S, SSM_WIDTH) + d_skip.astype(jnp.float32) * u.reshape(B, S, SSM_WIDTH)
    y = jax.nn.gelu(y).astype(h.dtype)
    val, gate = jnp.split(y @ w_glu, 2, axis=-1)
    return val * jax.nn.sigmoid(gate)


def conv_ffn(h, w_up, conv_w, conv_b, w_down):
    up = h @ w_up
    up = lax.conv_general_dilated(up, conv_w[:, None, :], window_strides=(1,),
                                  padding=[(CONV_W - 1, 0)],
                                  dimension_numbers=('NWC', 'WIO', 'NWC'),
                                  feature_group_count=2 * D_FF) + conv_b
    val, gate = jnp.split(up, 2, axis=-1)
    return (jax.nn.silu(gate) * val) @ w_down


def _fwd_setup_inputs(seed: int = 0) -> dict:
    key = jax.random.key(seed)
    ks = iter(jax.random.split(key, 32))
    f32 = jnp.float32

    def nrm(shape, fan_in):
        return jax.random.normal(next(ks), shape, f32) * (fan_in ** -0.5)

    def gain(shape):
        return 1.0 + 0.02 * jax.random.normal(next(ks), shape, f32)

    Lm, Ls = N_MLA_LAYERS, N_SSM_LAYERS
    G, P, C = SSM_GROUPS, SSM_STATE, SSM_GROUP
    x = jax.random.normal(next(ks), (BATCH, SEQ, D_MODEL), f32)
    offset = jax.random.randint(next(ks), (BATCH, 1), 0, 4096, dtype=jnp.int32)
    positions = (offset + jnp.arange(SEQ, dtype=jnp.int32)[None, :]).astype(jnp.int32)

    mla_w_a = nrm((Lm, D_MODEL, Q_LORA + KV_LORA + QK_ROPE), D_MODEL)
    mla_g_q = gain((Lm, Q_LORA))
    mla_g_kv = gain((Lm, KV_LORA))
    mla_w_uq = nrm((Lm, Q_LORA, MLA_HEADS * (QK_NOPE + QK_ROPE)), Q_LORA)
    mla_w_ukv = nrm((Lm, KV_LORA, MLA_HEADS * (QK_NOPE + V_HEAD)), KV_LORA)
    mla_w_o = nrm((Lm, MLA_HEADS * V_HEAD, D_MODEL), MLA_HEADS * V_HEAD)

    ssm_w_in = nrm((Ls, D_MODEL, SSM_WIDTH), D_MODEL)
    ssm_lambda_re = -0.5 + 0.02 * jax.random.normal(next(ks), (Ls, G, P), f32)
    ssm_lambda_im = math.pi * jnp.arange(P, dtype=f32)[None, None, :] + 0.02 * jax.random.normal(next(ks), (Ls, G, P), f32)
    ssm_log_dt = jax.random.uniform(next(ks), (Ls, G), f32, math.log(DT_MIN), math.log(DT_MAX))
    ssm_b_re = nrm((Ls, G, P, C), 2 * C)
    ssm_b_im = nrm((Ls, G, P, C), 2 * C)
    ssm_c_re = nrm((Ls, G, C, P), 2 * P)
    ssm_c_im = nrm((Ls, G, C, P), 2 * P)
    ssm_d = jax.random.normal(next(ks), (Ls, SSM_WIDTH), f32)
    ssm_w_glu = nrm((Ls, SSM_WIDTH, 2 * D_MODEL), SSM_WIDTH)

    ffn_w_up = nrm((DEPTH, D_MODEL, 2 * D_FF), D_MODEL)
    ffn_conv_w = nrm((DEPTH, CONV_W, 2 * D_FF), CONV_W)
    ffn_conv_b = 0.02 * jax.random.normal(next(ks), (DEPTH, 2 * D_FF), f32)
    ffn_w_down = nrm((DEPTH, D_FF, D_MODEL), D_FF)

    g_mix = gain((DEPTH, D_MODEL))
    g_ffn = gain((DEPTH, D_MODEL))
    g_final = gain((D_MODEL,))
    return {"x": x, "positions": positions,
            "mla_w_a": mla_w_a, "mla_g_q": mla_g_q, "mla_g_kv": mla_g_kv,
            "mla_w_uq": mla_w_uq, "mla_w_ukv": mla_w_ukv, "mla_w_o": mla_w_o,
            "ssm_w_in": ssm_w_in, "ssm_lambda_re": ssm_lambda_re, "ssm_lambda_im": ssm_lambda_im,
            "ssm_log_dt": ssm_log_dt, "ssm_b_re": ssm_b_re, "ssm_b_im": ssm_b_im,
            "ssm_c_re": ssm_c_re, "ssm_c_im": ssm_c_im, "ssm_d": ssm_d, "ssm_w_glu": ssm_w_glu,
            "ffn_w_up": ffn_w_up, "ffn_conv_w": ffn_conv_w, "ffn_conv_b": ffn_conv_b,
            "ffn_w_down": ffn_w_down, "g_mix": g_mix, "g_ffn": g_ffn, "g_final": g_final}


def _fwd_reference(x, positions, mla_w_a, mla_g_q, mla_g_kv, mla_w_uq, mla_w_ukv, mla_w_o,
              ssm_w_in, ssm_lambda_re, ssm_lambda_im, ssm_log_dt, ssm_b_re, ssm_b_im,
              ssm_c_re, ssm_c_im, ssm_d, ssm_w_glu, ffn_w_up, ffn_conv_w, ffn_conv_b,
              ffn_w_down, g_mix, g_ffn, g_final):
    cos, sin = rope_tables(positions)
    h = x
    for i in range(DEPTH):
        hn = rmsnorm(h, g_mix[i])
        j = i // N_MIXERS
        if i % N_MIXERS == 0:
            mix = mla_mixer(hn, cos, sin, mla_w_a[j], mla_g_q[j], mla_g_kv[j],
                            mla_w_uq[j], mla_w_ukv[j], mla_w_o[j])
        else:
            mix = s5_mixer(hn, ssm_w_in[j], ssm_lambda_re[j], ssm_lambda_im[j], ssm_log_dt[j],
                           ssm_b_re[j], ssm_b_im[j], ssm_c_re[j], ssm_c_im[j], ssm_d[j], ssm_w_glu[j])
        h = h + mix
        h = h + conv_ffn(rmsnorm(h, g_ffn[i]), ffn_w_up[i], ffn_conv_w[i], ffn_conv_b[i], ffn_w_down[i])
    return rmsnorm(h, g_final)


import jax as _jax
import jax.numpy as _jnp

TWIN_FORMAT = 'train_step'
FWD_PARAMS = ['x', 'positions', 'mla_w_a', 'mla_g_q', 'mla_g_kv', 'mla_w_uq', 'mla_w_ukv', 'mla_w_o', 'ssm_w_in', 'ssm_lambda_re', 'ssm_lambda_im', 'ssm_log_dt', 'ssm_b_re', 'ssm_b_im', 'ssm_c_re', 'ssm_c_im', 'ssm_d', 'ssm_w_glu', 'ffn_w_up', 'ffn_conv_w', 'ffn_conv_b', 'ffn_w_down', 'g_mix', 'g_ffn', 'g_final']
TWIN_WEIGHTS = ['mla_w_a', 'mla_g_q', 'mla_g_kv', 'mla_w_uq', 'mla_w_ukv', 'mla_w_o', 'ssm_w_in', 'ssm_lambda_re', 'ssm_lambda_im', 'ssm_log_dt', 'ssm_b_re', 'ssm_b_im', 'ssm_c_re', 'ssm_c_im', 'ssm_d', 'ssm_w_glu', 'ffn_w_up', 'ffn_conv_w', 'ffn_conv_b', 'ffn_w_down', 'g_mix', 'g_ffn', 'g_final']
TWIN_DIFF_INPUT = 'x'
TWIN_INPUTS = ['x', 'positions', 'mla_w_a', 'mla_g_q', 'mla_g_kv', 'mla_w_uq', 'mla_w_ukv', 'mla_w_o', 'ssm_w_in', 'ssm_lambda_re', 'ssm_lambda_im', 'ssm_log_dt', 'ssm_b_re', 'ssm_b_im', 'ssm_c_re', 'ssm_c_im', 'ssm_d', 'ssm_w_glu', 'ffn_w_up', 'ffn_conv_w', 'ffn_conv_b', 'ffn_w_down', 'g_mix', 'g_ffn', 'g_final', 'loss_target', 'm_mla_w_a', 'm_mla_g_q', 'm_mla_g_kv', 'm_mla_w_uq', 'm_mla_w_ukv', 'm_mla_w_o', 'm_ssm_w_in', 'm_ssm_lambda_re', 'm_ssm_lambda_im', 'm_ssm_log_dt', 'm_ssm_b_re', 'm_ssm_b_im', 'm_ssm_c_re', 'm_ssm_c_im', 'm_ssm_d', 'm_ssm_w_glu', 'm_ffn_w_up', 'm_ffn_conv_w', 'm_ffn_conv_b', 'm_ffn_w_down', 'm_g_mix', 'm_g_ffn', 'm_g_final', 'v_mla_w_a', 'v_mla_g_q', 'v_mla_g_kv', 'v_mla_w_uq', 'v_mla_w_ukv', 'v_mla_w_o', 'v_ssm_w_in', 'v_ssm_lambda_re', 'v_ssm_lambda_im', 'v_ssm_log_dt', 'v_ssm_b_re', 'v_ssm_b_im', 'v_ssm_c_re', 'v_ssm_c_im', 'v_ssm_d', 'v_ssm_w_glu', 'v_ffn_w_up', 'v_ffn_conv_w', 'v_ffn_conv_b', 'v_ffn_w_down', 'v_g_mix', 'v_g_ffn', 'v_g_final']
TWIN_OUTPUTS = ['loss', 'grad_x', 'grad_mla_w_a', 'grad_mla_g_q', 'grad_mla_g_kv', 'grad_mla_w_uq', 'grad_mla_w_ukv', 'grad_mla_w_o', 'grad_ssm_w_in', 'grad_ssm_lambda_re', 'grad_ssm_lambda_im', 'grad_ssm_log_dt', 'grad_ssm_b_re', 'grad_ssm_b_im', 'grad_ssm_c_re', 'grad_ssm_c_im', 'grad_ssm_d', 'grad_ssm_w_glu', 'grad_ffn_w_up', 'grad_ffn_conv_w', 'grad_ffn_conv_b', 'grad_ffn_w_down', 'grad_g_mix', 'grad_g_ffn', 'grad_g_final', 'delta_mla_w_a', 'delta_mla_g_q', 'delta_mla_g_kv', 'delta_mla_w_uq', 'delta_mla_w_ukv', 'delta_mla_w_o', 'delta_ssm_w_in', 'delta_ssm_lambda_re', 'delta_ssm_lambda_im', 'delta_ssm_log_dt', 'delta_ssm_b_re', 'delta_ssm_b_im', 'delta_ssm_c_re', 'delta_ssm_c_im', 'delta_ssm_d', 'delta_ssm_w_glu', 'delta_ffn_w_up', 'delta_ffn_conv_w', 'delta_ffn_conv_b', 'delta_ffn_w_down', 'delta_g_mix', 'delta_g_ffn', 'delta_g_final', 'new_m_mla_w_a', 'new_m_mla_g_q', 'new_m_mla_g_kv', 'new_m_mla_w_uq', 'new_m_mla_w_ukv', 'new_m_mla_w_o', 'new_m_ssm_w_in', 'new_m_ssm_lambda_re', 'new_m_ssm_lambda_im', 'new_m_ssm_log_dt', 'new_m_ssm_b_re', 'new_m_ssm_b_im', 'new_m_ssm_c_re', 'new_m_ssm_c_im', 'new_m_ssm_d', 'new_m_ssm_w_glu', 'new_m_ffn_w_up', 'new_m_ffn_conv_w', 'new_m_ffn_conv_b', 'new_m_ffn_w_down', 'new_m_g_mix', 'new_m_g_ffn', 'new_m_g_final', 'new_v_mla_w_a', 'new_v_mla_g_q', 'new_v_mla_g_kv', 'new_v_mla_w_uq', 'new_v_mla_w_ukv', 'new_v_mla_w_o', 'new_v_ssm_w_in', 'new_v_ssm_lambda_re', 'new_v_ssm_lambda_im', 'new_v_ssm_log_dt', 'new_v_ssm_b_re', 'new_v_ssm_b_im', 'new_v_ssm_c_re', 'new_v_ssm_c_im', 'new_v_ssm_d', 'new_v_ssm_w_glu', 'new_v_ffn_w_up', 'new_v_ffn_conv_w', 'new_v_ffn_conv_b', 'new_v_ffn_w_down', 'new_v_g_mix', 'new_v_g_ffn', 'new_v_g_final']
TWIN_LEAF_KINDS = {'loss': 'loss', 'grad_x': 'grad_x', 'grad_mla_w_a': 'grad_w', 'grad_mla_g_q': 'grad_w', 'grad_mla_g_kv': 'grad_w', 'grad_mla_w_uq': 'grad_w', 'grad_mla_w_ukv': 'grad_w', 'grad_mla_w_o': 'grad_w', 'grad_ssm_w_in': 'grad_w', 'grad_ssm_lambda_re': 'grad_w', 'grad_ssm_lambda_im': 'grad_w', 'grad_ssm_log_dt': 'grad_w', 'grad_ssm_b_re': 'grad_w', 'grad_ssm_b_im': 'grad_w', 'grad_ssm_c_re': 'grad_w', 'grad_ssm_c_im': 'grad_w', 'grad_ssm_d': 'grad_w', 'grad_ssm_w_glu': 'grad_w', 'grad_ffn_w_up': 'grad_w', 'grad_ffn_conv_w': 'grad_w', 'grad_ffn_conv_b': 'grad_w', 'grad_ffn_w_down': 'grad_w', 'grad_g_mix': 'grad_w', 'grad_g_ffn': 'grad_w', 'grad_g_final': 'grad_w', 'delta_mla_w_a': 'delta_w', 'delta_mla_g_q': 'delta_w', 'delta_mla_g_kv': 'delta_w', 'delta_mla_w_uq': 'delta_w', 'delta_mla_w_ukv': 'delta_w', 'delta_mla_w_o': 'delta_w', 'delta_ssm_w_in': 'delta_w', 'delta_ssm_lambda_re': 'delta_w', 'delta_ssm_lambda_im': 'delta_w', 'delta_ssm_log_dt': 'delta_w', 'delta_ssm_b_re': 'delta_w', 'delta_ssm_b_im': 'delta_w', 'delta_ssm_c_re': 'delta_w', 'delta_ssm_c_im': 'delta_w', 'delta_ssm_d': 'delta_w', 'delta_ssm_w_glu': 'delta_w', 'delta_ffn_w_up': 'delta_w', 'delta_ffn_conv_w': 'delta_w', 'delta_ffn_conv_b': 'delta_w', 'delta_ffn_w_down': 'delta_w', 'delta_g_mix': 'delta_w', 'delta_g_ffn': 'delta_w', 'delta_g_final': 'delta_w', 'new_m_mla_w_a': 'new_m', 'new_m_mla_g_q': 'new_m', 'new_m_mla_g_kv': 'new_m', 'new_m_mla_w_uq': 'new_m', 'new_m_mla_w_ukv': 'new_m', 'new_m_mla_w_o': 'new_m', 'new_m_ssm_w_in': 'new_m', 'new_m_ssm_lambda_re': 'new_m', 'new_m_ssm_lambda_im': 'new_m', 'new_m_ssm_log_dt': 'new_m', 'new_m_ssm_b_re': 'new_m', 'new_m_ssm_b_im': 'new_m', 'new_m_ssm_c_re': 'new_m', 'new_m_ssm_c_im': 'new_m', 'new_m_ssm_d': 'new_m', 'new_m_ssm_w_glu': 'new_m', 'new_m_ffn_w_up': 'new_m', 'new_m_ffn_conv_w': 'new_m', 'new_m_ffn_conv_b': 'new_m', 'new_m_ffn_w_down': 'new_m', 'new_m_g_mix': 'new_m', 'new_m_g_ffn': 'new_m', 'new_m_g_final': 'new_m', 'new_v_mla_w_a': 'new_v', 'new_v_mla_g_q': 'new_v', 'new_v_mla_g_kv': 'new_v', 'new_v_mla_w_uq': 'new_v', 'new_v_mla_w_ukv': 'new_v', 'new_v_mla_w_o': 'new_v', 'new_v_ssm_w_in': 'new_v', 'new_v_ssm_lambda_re': 'new_v', 'new_v_ssm_lambda_im': 'new_v', 'new_v_ssm_log_dt': 'new_v', 'new_v_ssm_b_re': 'new_v', 'new_v_ssm_b_im': 'new_v', 'new_v_ssm_c_re': 'new_v', 'new_v_ssm_c_im': 'new_v', 'new_v_ssm_d': 'new_v', 'new_v_ssm_w_glu': 'new_v', 'new_v_ffn_w_up': 'new_v', 'new_v_ffn_conv_w': 'new_v', 'new_v_ffn_conv_b': 'new_v', 'new_v_ffn_w_down': 'new_v', 'new_v_g_mix': 'new_v', 'new_v_g_ffn': 'new_v', 'new_v_g_final': 'new_v'}


def _forward(args):
    return _fwd_reference(*[args[k] for k in FWD_PARAMS])


def _output_shape():
    def fwd():
        inp = _fwd_setup_inputs(0)
        return _fwd_reference(*[inp[k] for k in FWD_PARAMS])
    out = _jax.eval_shape(fwd)
    return out.shape, out.dtype

N_MICROBATCH = 1
ADAM_LR = 0.001
ADAM_B1 = 0.9
ADAM_B2 = 0.999
ADAM_EPS = 1e-08
ADAM_WD = 0.01
ADAM_STEP = 10
PER_EXAMPLE_BATCH_AXIS = {'x': 0, 'positions': 0, 'loss_target': 0}
SHARED_INPUTS = []
_WEIGHT_DTYPES = {'mla_w_a': _jnp.float32, 'mla_g_q': _jnp.float32, 'mla_g_kv': _jnp.float32, 'mla_w_uq': _jnp.float32, 'mla_w_ukv': _jnp.float32, 'mla_w_o': _jnp.float32, 'ssm_w_in': _jnp.float32, 'ssm_lambda_re': _jnp.float32, 'ssm_lambda_im': _jnp.float32, 'ssm_log_dt': _jnp.float32, 'ssm_b_re': _jnp.float32, 'ssm_b_im': _jnp.float32, 'ssm_c_re': _jnp.float32, 'ssm_c_im': _jnp.float32, 'ssm_d': _jnp.float32, 'ssm_w_glu': _jnp.float32, 'ffn_w_up': _jnp.float32, 'ffn_conv_w': _jnp.float32, 'ffn_conv_b': _jnp.float32, 'ffn_w_down': _jnp.float32, 'g_mix': _jnp.float32, 'g_ffn': _jnp.float32, 'g_final': _jnp.float32}
MOMENT_SCALE = {'mla_w_a': 1.314905e-01, 'mla_g_q': 1.014452e-01, 'mla_g_kv': 1.762059e-01, 'mla_w_uq': 5.251986e-02, 'mla_w_ukv': 6.109367e-02, 'mla_w_o': 6.733961e-02, 'ssm_w_in': 1.190573e-01, 'ssm_lambda_re': 8.427036e-03, 'ssm_lambda_im': 7.029107e-03, 'ssm_log_dt': 1.025204e+01, 'ssm_b_re': 4.092480e-03, 'ssm_b_im': 4.052586e-03, 'ssm_c_re': 8.354982e-03, 'ssm_c_im': 7.990508e-03, 'ssm_d': 1.249440e-01, 'ssm_w_glu': 8.677381e-02, 'ffn_w_up': 1.126579e-01, 'ffn_conv_w': 1.126997e-01, 'ffn_conv_b': 1.080008e-01, 'ffn_w_down': 1.845941e-01, 'g_mix': 1.103401e-01, 'g_ffn': 2.605712e-01, 'g_final': 1.281594e+02}


def _to_microbatches(a, axis):
    t = _jnp.moveaxis(a, axis, 0)
    t = t.reshape((N_MICROBATCH, t.shape[0] // N_MICROBATCH) + t.shape[1:])
    return _jnp.moveaxis(t, 1, axis + 1)


def setup_inputs(seed: int = 0) -> dict:
    inp = _fwd_setup_inputs(seed)
    key = _jax.random.fold_in(_jax.random.key(seed), 7919)
    shape, _ = _output_shape()
    out = dict(inp)
    out["loss_target"] = _jax.random.normal(_jax.random.fold_in(key, 0), shape, _jnp.float32)
    for i, name in enumerate(TWIN_WEIGHTS):
        w = inp[name].astype(_jnp.float32)
        if MOMENT_SCALE is None:
            s = _jnp.sqrt(_jnp.mean(_jnp.square(w)) + 1e-30)
        else:
            s = MOMENT_SCALE[name]
        km, kv = _jax.random.split(_jax.random.fold_in(key, i + 1))
        out[name] = w
        out["m_" + name] = s * _jax.random.normal(km, w.shape, _jnp.float32)
        out["v_" + name] = (s * s) * _jax.random.uniform(kv, w.shape, _jnp.float32, 0.5, 1.5)
    if N_MICROBATCH > 1:
        for name, axis in PER_EXAMPLE_BATCH_AXIS.items():
            out[name] = _to_microbatches(out[name], axis)
    return {'x': out['x'], 'positions': out['positions'], 'mla_w_a': out['mla_w_a'], 'mla_g_q': out['mla_g_q'], 'mla_g_kv': out['mla_g_kv'], 'mla_w_uq': out['mla_w_uq'], 'mla_w_ukv': out['mla_w_ukv'], 'mla_w_o': out['mla_w_o'], 'ssm_w_in': out['ssm_w_in'], 'ssm_lambda_re': out['ssm_lambda_re'], 'ssm_lambda_im': out['ssm_lambda_im'], 'ssm_log_dt': out['ssm_log_dt'], 'ssm_b_re': out['ssm_b_re'], 'ssm_b_im': out['ssm_b_im'], 'ssm_c_re': out['ssm_c_re'], 'ssm_c_im': out['ssm_c_im'], 'ssm_d': out['ssm_d'], 'ssm_w_glu': out['ssm_w_glu'], 'ffn_w_up': out['ffn_w_up'], 'ffn_conv_w': out['ffn_conv_w'], 'ffn_conv_b': out['ffn_conv_b'], 'ffn_w_down': out['ffn_w_down'], 'g_mix': out['g_mix'], 'g_ffn': out['g_ffn'], 'g_final': out['g_final'], 'loss_target': out['loss_target'], 'm_mla_w_a': out['m_mla_w_a'], 'm_mla_g_q': out['m_mla_g_q'], 'm_mla_g_kv': out['m_mla_g_kv'], 'm_mla_w_uq': out['m_mla_w_uq'], 'm_mla_w_ukv': out['m_mla_w_ukv'], 'm_mla_w_o': out['m_mla_w_o'], 'm_ssm_w_in': out['m_ssm_w_in'], 'm_ssm_lambda_re': out['m_ssm_lambda_re'], 'm_ssm_lambda_im': out['m_ssm_lambda_im'], 'm_ssm_log_dt': out['m_ssm_log_dt'], 'm_ssm_b_re': out['m_ssm_b_re'], 'm_ssm_b_im': out['m_ssm_b_im'], 'm_ssm_c_re': out['m_ssm_c_re'], 'm_ssm_c_im': out['m_ssm_c_im'], 'm_ssm_d': out['m_ssm_d'], 'm_ssm_w_glu': out['m_ssm_w_glu'], 'm_ffn_w_up': out['m_ffn_w_up'], 'm_ffn_conv_w': out['m_ffn_conv_w'], 'm_ffn_conv_b': out['m_ffn_conv_b'], 'm_ffn_w_down': out['m_ffn_w_down'], 'm_g_mix': out['m_g_mix'], 'm_g_ffn': out['m_g_ffn'], 'm_g_final': out['m_g_final'], 'v_mla_w_a': out['v_mla_w_a'], 'v_mla_g_q': out['v_mla_g_q'], 'v_mla_g_kv': out['v_mla_g_kv'], 'v_mla_w_uq': out['v_mla_w_uq'], 'v_mla_w_ukv': out['v_mla_w_ukv'], 'v_mla_w_o': out['v_mla_w_o'], 'v_ssm_w_in': out['v_ssm_w_in'], 'v_ssm_lambda_re': out['v_ssm_lambda_re'], 'v_ssm_lambda_im': out['v_ssm_lambda_im'], 'v_ssm_log_dt': out['v_ssm_log_dt'], 'v_ssm_b_re': out['v_ssm_b_re'], 'v_ssm_b_im': out['v_ssm_b_im'], 'v_ssm_c_re': out['v_ssm_c_re'], 'v_ssm_c_im': out['v_ssm_c_im'], 'v_ssm_d': out['v_ssm_d'], 'v_ssm_w_glu': out['v_ssm_w_glu'], 'v_ffn_w_up': out['v_ffn_w_up'], 'v_ffn_conv_w': out['v_ffn_conv_w'], 'v_ffn_conv_b': out['v_ffn_conv_b'], 'v_ffn_w_down': out['v_ffn_w_down'], 'v_g_mix': out['v_g_mix'], 'v_g_ffn': out['v_g_ffn'], 'v_g_final': out['v_g_final']}


def _loss(weights, diff, rest, loss_target):
    with _jax.named_scope("forward"):
        args = {**rest, TWIN_DIFF_INPUT: diff, **{k: w.astype(_WEIGHT_DTYPES[k]) for k, w in weights.items()}}
        y = _forward(args)
    with _jax.named_scope("loss_head"):
        err = _jnp.square(y.astype(_jnp.float32) - loss_target)
        return 0.5 * _jnp.sum(_jnp.mean(err, axis=-1)) if err.ndim else 0.5 * err


def _adamw(w, g, m, v):
    m = ADAM_B1 * m + (1.0 - ADAM_B1) * g
    v = ADAM_B2 * v + (1.0 - ADAM_B2) * _jnp.square(g)
    m_hat = m / (1.0 - ADAM_B1 ** ADAM_STEP)
    v_hat = v / (1.0 - ADAM_B2 ** ADAM_STEP)
    delta = -ADAM_LR * (m_hat / (_jnp.sqrt(v_hat) + ADAM_EPS) + ADAM_WD * w)
    return delta, m, v


def reference(x, positions, mla_w_a, mla_g_q, mla_g_kv, mla_w_uq, mla_w_ukv, mla_w_o, ssm_w_in, ssm_lambda_re, ssm_lambda_im, ssm_log_dt, ssm_b_re, ssm_b_im, ssm_c_re, ssm_c_im, ssm_d, ssm_w_glu, ffn_w_up, ffn_conv_w, ffn_conv_b, ffn_w_down, g_mix, g_ffn, g_final, loss_target, m_mla_w_a, m_mla_g_q, m_mla_g_kv, m_mla_w_uq, m_mla_w_ukv, m_mla_w_o, m_ssm_w_in, m_ssm_lambda_re, m_ssm_lambda_im, m_ssm_log_dt, m_ssm_b_re, m_ssm_b_im, m_ssm_c_re, m_ssm_c_im, m_ssm_d, m_ssm_w_glu, m_ffn_w_up, m_ffn_conv_w, m_ffn_conv_b, m_ffn_w_down, m_g_mix, m_g_ffn, m_g_final, v_mla_w_a, v_mla_g_q, v_mla_g_kv, v_mla_w_uq, v_mla_w_ukv, v_mla_w_o, v_ssm_w_in, v_ssm_lambda_re, v_ssm_lambda_im, v_ssm_log_dt, v_ssm_b_re, v_ssm_b_im, v_ssm_c_re, v_ssm_c_im, v_ssm_d, v_ssm_w_glu, v_ffn_w_up, v_ffn_conv_w, v_ffn_conv_b, v_ffn_w_down, v_g_mix, v_g_ffn, v_g_final):
    given = dict(x=x, positions=positions, mla_w_a=mla_w_a, mla_g_q=mla_g_q, mla_g_kv=mla_g_kv, mla_w_uq=mla_w_uq, mla_w_ukv=mla_w_ukv, mla_w_o=mla_w_o, ssm_w_in=ssm_w_in, ssm_lambda_re=ssm_lambda_re, ssm_lambda_im=ssm_lambda_im, ssm_log_dt=ssm_log_dt, ssm_b_re=ssm_b_re, ssm_b_im=ssm_b_im, ssm_c_re=ssm_c_re, ssm_c_im=ssm_c_im, ssm_d=ssm_d, ssm_w_glu=ssm_w_glu, ffn_w_up=ffn_w_up, ffn_conv_w=ffn_conv_w, ffn_conv_b=ffn_conv_b, ffn_w_down=ffn_w_down, g_mix=g_mix, g_ffn=g_ffn, g_final=g_final, loss_target=loss_target, m_mla_w_a=m_mla_w_a, m_mla_g_q=m_mla_g_q, m_mla_g_kv=m_mla_g_kv, m_mla_w_uq=m_mla_w_uq, m_mla_w_ukv=m_mla_w_ukv, m_mla_w_o=m_mla_w_o, m_ssm_w_in=m_ssm_w_in, m_ssm_lambda_re=m_ssm_lambda_re, m_ssm_lambda_im=m_ssm_lambda_im, m_ssm_log_dt=m_ssm_log_dt, m_ssm_b_re=m_ssm_b_re, m_ssm_b_im=m_ssm_b_im, m_ssm_c_re=m_ssm_c_re, m_ssm_c_im=m_ssm_c_im, m_ssm_d=m_ssm_d, m_ssm_w_glu=m_ssm_w_glu, m_ffn_w_up=m_ffn_w_up, m_ffn_conv_w=m_ffn_conv_w, m_ffn_conv_b=m_ffn_conv_b, m_ffn_w_down=m_ffn_w_down, m_g_mix=m_g_mix, m_g_ffn=m_g_ffn, m_g_final=m_g_final, v_mla_w_a=v_mla_w_a, v_mla_g_q=v_mla_g_q, v_mla_g_kv=v_mla_g_kv, v_mla_w_uq=v_mla_w_uq, v_mla_w_ukv=v_mla_w_ukv, v_mla_w_o=v_mla_w_o, v_ssm_w_in=v_ssm_w_in, v_ssm_lambda_re=v_ssm_lambda_re, v_ssm_lambda_im=v_ssm_lambda_im, v_ssm_log_dt=v_ssm_log_dt, v_ssm_b_re=v_ssm_b_re, v_ssm_b_im=v_ssm_b_im, v_ssm_c_re=v_ssm_c_re, v_ssm_c_im=v_ssm_c_im, v_ssm_d=v_ssm_d, v_ssm_w_glu=v_ssm_w_glu, v_ffn_w_up=v_ffn_w_up, v_ffn_conv_w=v_ffn_conv_w, v_ffn_conv_b=v_ffn_conv_b, v_ffn_w_down=v_ffn_w_down, v_g_mix=v_g_mix, v_g_ffn=v_g_ffn, v_g_final=v_g_final)
    weights = {n: given[n] for n in TWIN_WEIGHTS}
    shared = {n: given[n] for n in SHARED_INPUTS}
    per_example = {n: given[n] for n in ['x', 'positions']}
    grad_fn = _jax.value_and_grad(_loss, argnums=(0, 1))

    def one_microbatch(ex, loss_target):
        ex = dict(ex)
        diff = ex.pop(TWIN_DIFF_INPUT)
        return grad_fn(weights, diff, {**shared, **ex}, loss_target)

    if N_MICROBATCH == 1:
        loss, (grad_w, grad_x) = one_microbatch(per_example, given["loss_target"])
    else:
        def body(carry, xs):
            loss_sum, grad_sum = carry
            l_k, (gw_k, gx_k) = one_microbatch(xs[0], xs[1])
            with _jax.named_scope("update"):
                return (loss_sum + l_k, _jax.tree.map(_jnp.add, grad_sum, gw_k)), gx_k

        init = (_jnp.zeros((), _jnp.float32), _jax.tree.map(_jnp.zeros_like, weights))
        (loss, grad_w), grad_x = _jax.lax.scan(body, init, (per_example, given["loss_target"]))
    with _jax.named_scope("update"):
        delta_w, new_m, new_v = {}, {}, {}
        for n in TWIN_WEIGHTS:
            delta_w[n], new_m[n], new_v[n] = _adamw(weights[n], grad_w[n], given["m_" + n], given["v_" + n])
    return (loss, grad_x, *[grad_w[n] for n in TWIN_WEIGHTS], *[delta_w[n] for n in TWIN_WEIGHTS],
            *[new_m[n] for n in TWIN_WEIGHTS], *[new_v[n] for n in TWIN_WEIGHTS])
```

```python
import functools
import math

import jax
import jax.numpy as jnp
from jax import lax
from jax.experimental import pallas as pl
from jax.experimental.pallas import tpu as pltpu

F32 = jnp.float32
MXU = jnp.bfloat16

D_MODEL = 1024
CHUNK = 64
EPS = 1e-6
HEADS = 8
QK_NOPE = 128
QK_ROPE = 64
V_HEAD = 128
Q_LORA = 384
KV_LORA = 256
ROPE_THETA = 10000.0
QK_DIM = QK_NOPE + QK_ROPE
SSM_GROUP = 16
SSM_GROUPS = D_MODEL // SSM_GROUP
SSM_STATE = 64
NSTATE = SSM_GROUPS * SSM_STATE
D_FF = 2816
ATT_SCALE = QK_DIM ** -0.5
NEG = -1e30
NSEG = 8
SBLK = 8

ADAM_LR = 0.001
ADAM_B1 = 0.9
ADAM_B2 = 0.999
ADAM_EPS = 1e-08
ADAM_WD = 0.01
ADAM_STEP = 10

LANE = 128
VMEM_BIG = 56 * 1024 * 1024

WNAMES = ['mla_w_a', 'mla_g_q', 'mla_g_kv', 'mla_w_uq', 'mla_w_ukv', 'mla_w_o', 'ssm_w_in', 'ssm_lambda_re',
          'ssm_lambda_im', 'ssm_log_dt', 'ssm_b_re', 'ssm_b_im', 'ssm_c_re', 'ssm_c_im', 'ssm_d', 'ssm_w_glu',
          'ffn_w_up', 'ffn_conv_w', 'ffn_conv_b', 'ffn_w_down', 'g_mix', 'g_ffn', 'g_final']
FWD_NAMES = ['x', 'positions'] + WNAMES
SHARD_AXIS = {'mla_w_a': 1, 'mla_w_uq': 2, 'mla_w_ukv': 2, 'mla_w_o': 1, 'ssm_w_in': 1, 'ssm_d': 1,
              'ssm_w_glu': 2, 'ffn_w_up': 2, 'ffn_conv_w': 2, 'ffn_w_down': 1}
GATHER_MXU = ['mla_w_a', 'mla_w_uq', 'mla_w_ukv', 'mla_w_o', 'ssm_w_in', 'ssm_w_glu', 'ffn_w_up', 'ffn_w_down']
GATHER_F32 = ['ssm_d', 'ffn_conv_w']
NCHIP = 4
MESH = pl.DeviceIdType.MESH


def _tile(d, pref):
    t = min(pref, d) // LANE * LANE
    while t >= LANE:
        if d % t == 0:
            return t
        t -= LANE
    return d


def _rows(s, pref):
    t = min(s, pref)
    assert s % t == 0 and t % 8 == 0
    return t


def _params(big=False):
    if big:
        return pltpu.CompilerParams(vmem_limit_bytes=VMEM_BIG)
    return pltpu.CompilerParams(vmem_limit_bytes=40 * 1024 * 1024)


def _mm(a, b, *, name, mode="nn", out_dtype=F32, res=None, tm=None, tn=None, tk=None):
    if mode == "nn":
        M, K = a.shape
        tm = tm or _rows(M, 1024)
    else:
        K, M = a.shape
        tm = tm or _tile(M, 1024)
    N = b.shape[1]
    assert b.shape[0] == K
    tn = tn or _tile(N, 512)
    tk = tk or (_tile(K, 1408) if mode == "nn" else _rows(K, 512))
    nk = K // tk
    has_res = res is not None

    def body(a_ref, b_ref, *rest):
        if has_res:
            r_ref, o_ref, acc = rest
        else:
            o_ref, acc = rest
        k = pl.program_id(2)

        @pl.when(k == 0)
        def _():
            acc[...] = jnp.zeros_like(acc)

        av = a_ref[...].astype(MXU)
        bv = b_ref[...].astype(MXU)
        if mode == "nn":
            acc[...] += jnp.dot(av, bv, preferred_element_type=F32)
        else:
            acc[...] += lax.dot_general(av, bv, (((0,), (0,)), ((), ())), preferred_element_type=F32)

        @pl.when(k == nk - 1)
        def _():
            o = acc[...]
            if has_res:
                o = o + r_ref[...]
            o_ref[...] = o.astype(o_ref.dtype)

    if mode == "nn":
        a_spec = pl.BlockSpec((tm, tk), lambda i, j, k: (i, k))
    else:
        a_spec = pl.BlockSpec((tk, tm), lambda i, j, k: (k, i))
    in_specs = [a_spec, pl.BlockSpec((tk, tn), lambda i, j, k: (k, j))]
    ops = [a, b]
    if has_res:
        in_specs.append(pl.BlockSpec((tm, tn), lambda i, j, k: (i, j)))
        ops.append(res)
    return pl.pallas_call(
        body, name=name, grid=(M // tm, N // tn, nk), in_specs=in_specs,
        out_specs=pl.BlockSpec((tm, tn), lambda i, j, k: (i, j)),
        out_shape=jax.ShapeDtypeStruct((M, N), out_dtype),
        scratch_shapes=[pltpu.VMEM((tm, tn), F32)], compiler_params=_params(),
    )(*ops)


def _row_spec(tm, c):
    return pl.BlockSpec((tm, c), lambda i: (i, 0))


def _const_spec(r, c):
    return pl.BlockSpec((r, c), lambda i: (0, 0))


def _rms_parts(xv):
    r = lax.rsqrt(jnp.mean(xv * xv, axis=-1, keepdims=True) + EPS)
    return r, xv * r


def _rms_vjp(xv, gv, dyv):
    r, xhat = _rms_parts(xv)
    gy = dyv * gv
    dx = r * (gy - xhat * jnp.mean(gy * xhat, axis=-1, keepdims=True))
    return dx, dyv * xhat


def _rmsnorm_fwd(x, g, *, name):
    S, D = x.shape
    tm = _rows(S, 512)

    def body(x_ref, g_ref, o_ref):
        _, xhat = _rms_parts(x_ref[...])
        o_ref[...] = (xhat * g_ref[...]).astype(o_ref.dtype)

    return pl.pallas_call(
        body, name=name, grid=(S // tm,), in_specs=[_row_spec(tm, D), _const_spec(1, D)],
        out_specs=_row_spec(tm, D), out_shape=jax.ShapeDtypeStruct((S, D), MXU), compiler_params=_params(),
    )(x, g)


def _rmsnorm_bwd(x, g, dy, dres, *, name):
    S, D = x.shape
    tm = _rows(S, 512)

    def body(x_ref, g_ref, dy_ref, dr_ref, dx_ref, dg_ref):
        @pl.when(pl.program_id(0) == 0)
        def _():
            dg_ref[...] = jnp.zeros_like(dg_ref)

        dx, dgp = _rms_vjp(x_ref[...], g_ref[...], dy_ref[...])
        dx_ref[...] = dr_ref[...] + dx
        dg_ref[...] += jnp.sum(dgp, axis=0, keepdims=True)

    return pl.pallas_call(
        body, name=name, grid=(S // tm,),
        in_specs=[_row_spec(tm, D), _const_spec(1, D), _row_spec(tm, D), _row_spec(tm, D)],
        out_specs=(_row_spec(tm, D), _const_spec(1, D)),
        out_shape=(jax.ShapeDtypeStruct((S, D), F32), jax.ShapeDtypeStruct((1, D), F32)),
        compiler_params=_params(),
    )(x, g, dy, dres)


def _loss_head(h, g, tgt, *, name):
    S, D = h.shape
    tm = _rows(S, 512)

    def body(h_ref, g_ref, t_ref, l_ref, dh_ref, dg_ref):
        @pl.when(pl.program_id(0) == 0)
        def _():
            l_ref[...] = jnp.zeros_like(l_ref)
            dg_ref[...] = jnp.zeros_like(dg_ref)

        hv = h_ref[...]
        gv = g_ref[...]
        _, xhat = _rms_parts(hv)
        e = xhat * gv - t_ref[...]
        l_ref[...] += 0.5 * jnp.sum(jnp.mean(e * e, axis=-1, keepdims=True), axis=0, keepdims=True)
        dx, dgp = _rms_vjp(hv, gv, e * (1.0 / D))
        dh_ref[...] = dx
        dg_ref[...] += jnp.sum(dgp, axis=0, keepdims=True)

    return pl.pallas_call(
        body, name=name, grid=(S // tm,),
        in_specs=[_row_spec(tm, D), _const_spec(1, D), _row_spec(tm, D)],
        out_specs=(_const_spec(1, 1), _row_spec(tm, D), _const_spec(1, D)),
        out_shape=(jax.ShapeDtypeStruct((1, 1), F32), jax.ShapeDtypeStruct((S, D), F32),
                   jax.ShapeDtypeStruct((1, D), F32)),
        compiler_params=_params(),
    )(h, g, tgt)


def _swap_halves(g):
    lane = lax.broadcasted_iota(jnp.int32, g.shape, 1)
    return jnp.where(lane < QK_ROPE // 2, pltpu.roll(g, LANE - QK_ROPE // 2, axis=1),
                     pltpu.roll(g, QK_ROPE // 2, axis=1))


def _rope128(g, c128, s128):
    return g * c128 + _swap_halves(g) * s128


def _rope128_vjp(dy, c128, s128):
    lane = lax.broadcasted_iota(jnp.int32, dy.shape, 1)
    return jnp.where(lane < QK_ROPE, dy * c128 + _swap_halves(dy * s128), 0.0)


A_PAD = 768
KR0 = Q_LORA + KV_LORA


def _mla_mid_fwd(a, g_q, g_kv, c128, s128, *, name):
    S = a.shape[0]
    tm = _rows(S, 512)

    def body(a_ref, gq_ref, gkv_ref, c_ref, s_ref, cq_ref, ckv_ref, kr_ref):
        av = a_ref[...]
        _, qh = _rms_parts(av[:, :Q_LORA])
        cq_ref[...] = (qh * gq_ref[...]).astype(cq_ref.dtype)
        _, kh = _rms_parts(av[:, Q_LORA:KR0])
        ckv_ref[...] = (kh * gkv_ref[...]).astype(ckv_ref.dtype)
        kr = _rope128(av[:, KR0:A_PAD], c_ref[...], s_ref[...])
        kr_ref[...] = kr[:, :QK_ROPE].astype(kr_ref.dtype)

    return pl.pallas_call(
        body, name=name, grid=(S // tm,),
        in_specs=[_row_spec(tm, A_PAD), _const_spec(1, Q_LORA), _const_spec(1, KV_LORA), _row_spec(tm, LANE),
                  _row_spec(tm, LANE)],
        out_specs=(_row_spec(tm, Q_LORA), _row_spec(tm, KV_LORA), _row_spec(tm, QK_ROPE)),
        out_shape=(jax.ShapeDtypeStruct((S, Q_LORA), MXU), jax.ShapeDtypeStruct((S, KV_LORA), MXU),
                   jax.ShapeDtypeStruct((S, QK_ROPE), MXU)),
        compiler_params=_params(),
    )(a, g_q, g_kv, c128, s128)


def _mla_mid_bwd(a, dcq, dckv, dkr, g_q, g_kv, c128, s128, *, name):
    S = a.shape[0]
    tm = _rows(S, 512)

    def body(a_ref, dcq_ref, dckv_ref, dkr_ref, gq_ref, gkv_ref, c_ref, s_ref, da_ref, dgq_ref, dgkv_ref):
        @pl.when(pl.program_id(0) == 0)
        def _():
            dgq_ref[...] = jnp.zeros_like(dgq_ref)
            dgkv_ref[...] = jnp.zeros_like(dgkv_ref)

        av = a_ref[...]
        dx, dgp = _rms_vjp(av[:, :Q_LORA], gq_ref[...], dcq_ref[...])
        da_ref[:, :Q_LORA] = dx.astype(da_ref.dtype)
        dgq_ref[...] += jnp.sum(dgp, axis=0, keepdims=True)
        dx, dgp = _rms_vjp(av[:, Q_LORA:KR0], gkv_ref[...], dckv_ref[...])
        da_ref[:, Q_LORA:KR0] = dx.astype(da_ref.dtype)
        dgkv_ref[...] += jnp.sum(dgp, axis=0, keepdims=True)
        da_ref[:, KR0:A_PAD] = _rope128_vjp(dkr_ref[...], c_ref[...], s_ref[...]).astype(da_ref.dtype)

    return pl.pallas_call(
        body, name=name, grid=(S // tm,),
        in_specs=[_row_spec(tm, A_PAD), _row_spec(tm, Q_LORA), _row_spec(tm, KV_LORA), _row_spec(tm, LANE),
                  _const_spec(1, Q_LORA), _const_spec(1, KV_LORA), _row_spec(tm, LANE), _row_spec(tm, LANE)],
        out_specs=(_row_spec(tm, A_PAD), _const_spec(1, Q_LORA), _const_spec(1, KV_LORA)),
        out_shape=(jax.ShapeDtypeStruct((S, A_PAD), MXU), jax.ShapeDtypeStruct((1, Q_LORA), F32),
                   jax.ShapeDtypeStruct((1, KV_LORA), F32)),
        compiler_params=_params(),
    )(a, dcq, dckv, dkr, g_q, g_kv, c128, s128)


QF = 2 * HEADS * LANE
KVF = HEADS * (QK_NOPE + V_HEAD)


def _qk_prep(qfull, kv, kr, c128, s128, *, name):
    S = qfull.shape[0]
    tm = _rows(S, 256)

    def body(q_ref, kv_ref, kr_ref, c_ref, s_ref, *outs):
        qo, ko, vo = outs[:HEADS], outs[HEADS:2 * HEADS], outs[2 * HEADS:]
        cv, sv = c_ref[...], s_ref[...]
        krv = kr_ref[...]
        for h in range(HEADS):
            qo[h][:, :QK_NOPE] = q_ref[:, h * LANE:(h + 1) * LANE].astype(MXU)
            g = q_ref[:, (HEADS + h) * LANE:(HEADS + h + 1) * LANE]
            qo[h][:, QK_NOPE:] = _rope128(g, cv, sv)[:, :QK_ROPE].astype(MXU)
            ko[h][:, :QK_NOPE] = kv_ref[:, 2 * h * LANE:(2 * h + 1) * LANE]
            ko[h][:, QK_NOPE:] = krv
            vo[h][...] = kv_ref[:, (2 * h + 1) * LANE:(2 * h + 2) * LANE]

    shapes = ([jax.ShapeDtypeStruct((S, QK_DIM), MXU)] * (2 * HEADS)
              + [jax.ShapeDtypeStruct((S, V_HEAD), MXU)] * HEADS)
    specs = [_row_spec(tm, QK_DIM)] * (2 * HEADS) + [_row_spec(tm, V_HEAD)] * HEADS
    outs = pl.pallas_call(
        body, name=name, grid=(S // tm,),
        in_specs=[_row_spec(tm, QF), _row_spec(tm, KVF), _row_spec(tm, QK_ROPE), _row_spec(tm, LANE),
                  _row_spec(tm, LANE)],
        out_specs=tuple(specs), out_shape=tuple(shapes), compiler_params=_params(),
    )(qfull, kv, kr, c128, s128)
    return outs[:HEADS], outs[HEADS:2 * HEADS], outs[2 * HEADS:]


def _qk_prep_bwd(dqs, dks, dvs, c128, s128, *, name):
    S = dqs[0].shape[0]
    tm = _rows(S, 256)

    def body(*refs):
        dq = refs[:HEADS]
        dk = refs[HEADS:2 * HEADS]
        dv = refs[2 * HEADS:3 * HEADS]
        c_ref, s_ref, dqf_ref, dkv_ref, dkr_ref, tmp = refs[3 * HEADS:]
        cv, sv = c_ref[...], s_ref[...]
        tmp[...] = jnp.zeros_like(tmp)
        dkr_ref[...] = jnp.zeros_like(dkr_ref)
        for h in range(HEADS):
            dqf_ref[:, h * LANE:(h + 1) * LANE] = dq[h][:, :QK_NOPE].astype(MXU)
            tmp[:, :QK_ROPE] = dq[h][:, QK_NOPE:]
            dqf_ref[:, (HEADS + h) * LANE:(HEADS + h + 1) * LANE] = _rope128_vjp(tmp[...], cv, sv).astype(MXU)
            dkv_ref[:, 2 * h * LANE:(2 * h + 1) * LANE] = dk[h][:, :QK_NOPE].astype(MXU)
            dkv_ref[:, (2 * h + 1) * LANE:(2 * h + 2) * LANE] = dv[h][...].astype(MXU)
            dkr_ref[:, :QK_ROPE] += dk[h][:, QK_NOPE:]

    return pl.pallas_call(
        body, name=name, grid=(S // tm,),
        in_specs=[_row_spec(tm, QK_DIM)] * (2 * HEADS) + [_row_spec(tm, V_HEAD)] * HEADS
        + [_row_spec(tm, LANE), _row_spec(tm, LANE)],
        out_specs=(_row_spec(tm, QF), _row_spec(tm, KVF), _row_spec(tm, LANE)),
        out_shape=(jax.ShapeDtypeStruct((S, QF), MXU), jax.ShapeDtypeStruct((S, KVF), MXU),
                   jax.ShapeDtypeStruct((S, LANE), F32)),
        scratch_shapes=[pltpu.VMEM((tm, LANE), F32)], compiler_params=_params(),
    )(*dqs, *dks, *dvs, c128, s128)


def _dot_nt(a, b):
    return lax.dot_general(a, b, (((1,), (1,)), ((), ())), preferred_element_type=F32)


def _dot_tn(a, b):
    return lax.dot_general(a, b, (((0,), (0,)), ((), ())), preferred_element_type=F32)


def _chunk_mask(t, q_axis):
    qc = lax.broadcasted_iota(jnp.int32, (t, t), q_axis) // CHUNK
    kc = lax.broadcasted_iota(jnp.int32, (t, t), 1 - q_axis) // CHUNK
    return kc <= qc


def _attn_fwd(q, k, v, *, name):
    S = q.shape[0]
    T = _rows(S, 512)
    n = S // T

    def body(q_ref, k_ref, v_ref, o_ref, lse_ref, m_s, l_s, acc_s):
        i = pl.program_id(0)
        j = pl.program_id(1)

        @pl.when(j == 0)
        def _():
            m_s[...] = jnp.full_like(m_s, NEG)
            l_s[...] = jnp.zeros_like(l_s)
            acc_s[...] = jnp.zeros_like(acc_s)

        def update(masked):
            s = _dot_nt(q_ref[...], k_ref[...]) * ATT_SCALE
            if masked:
                s = jnp.where(_chunk_mask(T, 0), s, NEG)
            m_prev = m_s[...]
            m_new = jnp.maximum(m_prev, jnp.max(s, axis=1, keepdims=True))
            alpha = jnp.exp(m_prev - m_new)
            p = jnp.exp(s - m_new)
            l_s[...] = alpha * l_s[...] + jnp.sum(p, axis=1, keepdims=True)
            acc_s[...] = alpha * acc_s[...] + jnp.dot(p.astype(MXU), v_ref[...], preferred_element_type=F32)
            m_s[...] = m_new

        @pl.when(j < i)
        def _():
            update(False)

        @pl.when(j == i)
        def _():
            update(True)
            l = l_s[...]
            o_ref[...] = (acc_s[...] / l).astype(o_ref.dtype)
            lse_ref[...] = m_s[...] + jnp.log(l)

    kv_map = lambda i, j: (jnp.minimum(j, i), 0)
    return pl.pallas_call(
        body, name=name, grid=(n, n),
        in_specs=[pl.BlockSpec((T, QK_DIM), lambda i, j: (i, 0)), pl.BlockSpec((T, QK_DIM), kv_map),
                  pl.BlockSpec((T, V_HEAD), kv_map)],
        out_specs=(pl.BlockSpec((T, V_HEAD), lambda i, j: (i, 0)), pl.BlockSpec((T, 1), lambda i, j: (i, 0))),
        out_shape=(jax.ShapeDtypeStruct((S, V_HEAD), MXU), jax.ShapeDtypeStruct((S, 1), F32)),
        scratch_shapes=[pltpu.VMEM((T, 1), F32), pltpu.VMEM((T, 1), F32), pltpu.VMEM((T, V_HEAD), F32)],
        compiler_params=_params(),
    )(q, k, v)


def _attn_delta(do, o, *, name):
    S = do.shape[0]
    tm = _rows(S, 1024)

    def body(do_ref, o_ref, d_ref):
        d_ref[...] = jnp.sum(do_ref[...].astype(F32) * o_ref[...].astype(F32), axis=1, keepdims=True)

    return pl.pallas_call(
        body, name=name, grid=(S // tm,), in_specs=[_row_spec(tm, V_HEAD), _row_spec(tm, V_HEAD)],
        out_specs=_row_spec(tm, 1), out_shape=jax.ShapeDtypeStruct((S, 1), F32), compiler_params=_params(),
    )(do, o)


def _attn_bwd(q, k, v, do, lse_row, delta_row, *, name):
    S = q.shape[0]
    T = _rows(S, 512)
    n = S // T

    def body(q_ref, k_ref, v_ref, do_ref, lse_ref, dl_ref, dq_ref, dk_ref, dv_ref, dk_s, dv_s):
        j = pl.program_id(0)
        i = pl.program_id(1)

        @pl.when((j == 0) & (i == 0))
        def _():
            dq_ref[...] = jnp.zeros_like(dq_ref)

        def update(masked):
            qv, kv_, dov = q_ref[...], k_ref[...], do_ref[...]
            st = _dot_nt(kv_, qv) * ATT_SCALE
            if masked:
                st = jnp.where(_chunk_mask(T, 1), st, NEG)
            pt = jnp.exp(st - lse_ref[...])
            dv_s[...] += jnp.dot(pt.astype(MXU), dov, preferred_element_type=F32)
            dpt = _dot_nt(v_ref[...], dov)
            ds = (pt * (dpt - dl_ref[...]) * ATT_SCALE).astype(MXU)
            dk_s[...] += jnp.dot(ds, qv, preferred_element_type=F32)
            rows = pl.ds(pl.multiple_of(i * T, T), T)
            dq_ref[rows, :] += _dot_tn(ds, kv_)

        @pl.when(i == j)
        def _():
            dk_s[...] = jnp.zeros_like(dk_s)
            dv_s[...] = jnp.zeros_like(dv_s)
            update(True)

        @pl.when(i > j)
        def _():
            update(False)

        @pl.when(i == n - 1)
        def _():
            dk_ref[...] = dk_s[...]
            dv_ref[...] = dv_s[...]

    q_map = lambda j, i: (jnp.maximum(i, j), 0)
    r_map = lambda j, i: (0, jnp.maximum(i, j))
    k_map = lambda j, i: (j, 0)
    return pl.pallas_call(
        body, name=name, grid=(n, n),
        in_specs=[pl.BlockSpec((T, QK_DIM), q_map), pl.BlockSpec((T, QK_DIM), k_map),
                  pl.BlockSpec((T, V_HEAD), k_map), pl.BlockSpec((T, V_HEAD), q_map),
                  pl.BlockSpec((1, T), r_map), pl.BlockSpec((1, T), r_map)],
        out_specs=(pl.BlockSpec((S, QK_DIM), lambda j, i: (0, 0)), pl.BlockSpec((T, QK_DIM), k_map),
                   pl.BlockSpec((T, V_HEAD), k_map)),
        out_shape=(jax.ShapeDtypeStruct((S, QK_DIM), F32), jax.ShapeDtypeStruct((S, QK_DIM), F32),
                   jax.ShapeDtypeStruct((S, V_HEAD), F32)),
        scratch_shapes=[pltpu.VMEM((T, QK_DIM), F32), pltpu.VMEM((T, V_HEAD), F32)],
        compiler_params=_params(big=True),
    )(q, k, v, do, lse_row, delta_row)


HALO = 8


def _conv_tiles(S):
    tm = _rows(S, 256)
    tc = D_FF // 2
    return tm, tc, D_FF // tc


def _silu_parts(gate):
    sg = jax.nn.sigmoid(gate)
    return sg, gate * sg


def _convgate_fwd(up, cw, cb, *, name):
    S = up.shape[0]
    tm, tc, nc = _conv_tiles(S)
    hb = tm // HALO

    def body(v_ref, g_ref, hv_ref, hg_ref, wv_ref, wg_ref, bv_ref, bg_ref, o_ref, extv, extg):
        keep = (pl.program_id(0) > 0).astype(F32)

        def conv(ext, t_ref, h_ref, w_ref, b_ref):
            ext[0:HALO, :] = h_ref[...] * keep
            ext[HALO:HALO + tm, :] = t_ref[...]
            w = w_ref[...]
            return (w[0:1] * ext[HALO - 2:HALO - 2 + tm, :] + w[1:2] * ext[HALO - 1:HALO - 1 + tm, :]
                    + w[2:3] * ext[HALO:HALO + tm, :] + b_ref[...])

        val = conv(extv, v_ref, hv_ref, wv_ref, bv_ref)
        gate = conv(extg, g_ref, hg_ref, wg_ref, bg_ref)
        _, silu = _silu_parts(gate)
        o_ref[...] = (silu * val).astype(o_ref.dtype)

    prev = lambda i: jnp.maximum(i * hb - 1, 0)
    return pl.pallas_call(
        body, name=name, grid=(S // tm, nc),
        in_specs=[pl.BlockSpec((tm, tc), lambda i, j: (i, j)), pl.BlockSpec((tm, tc), lambda i, j: (i, j + nc)),
                  pl.BlockSpec((HALO, tc), lambda i, j: (prev(i), j)),
                  pl.BlockSpec((HALO, tc), lambda i, j: (prev(i), j + nc)),
                  pl.BlockSpec((3, tc), lambda i, j: (0, j)), pl.BlockSpec((3, tc), lambda i, j: (0, j + nc)),
                  pl.BlockSpec((1, tc), lambda i, j: (0, j)), pl.BlockSpec((1, tc), lambda i, j: (0, j + nc))],
        out_specs=pl.BlockSpec((tm, tc), lambda i, j: (i, j)),
        out_shape=jax.ShapeDtypeStruct((S, D_FF), MXU),
        scratch_shapes=[pltpu.VMEM((tm + HALO, tc), F32), pltpu.VMEM((tm + HALO, tc), F32)],
        compiler_params=_params(),
    )(up, up, up, up, cw, cw, cb, cb)


def _convgate_bwd(up, dact, cw, cb, *, name):
    S = up.shape[0]
    tm, tc, nc = _conv_tiles(S)
    hb = tm // HALO
    nr = S // tm
    R = tm + HALO

    def body(v_ref, g_ref, pv_ref, pg_ref, nv_ref, ng_ref, da_ref, dan_ref, wv_ref, wg_ref, bv_ref, bg_ref,
             duv_ref, dug_ref, dwv_ref, dwg_ref, dbv_ref, dbg_ref, extv, extg, dsv, dsg):
        i = pl.program_id(1)

        @pl.when(i == 0)
        def _():
            for r in (dwv_ref, dwg_ref, dbv_ref, dbg_ref):
                r[...] = jnp.zeros_like(r)

        keep_prev = (i > 0).astype(F32)
        keep_next = (i < nr - 1).astype(F32)

        def conv(ext, t_ref, p_ref, n_ref, w_ref, b_ref):
            ext[0:HALO, :] = p_ref[...] * keep_prev
            ext[HALO:HALO + tm, :] = t_ref[...]
            ext[HALO + tm:2 * HALO + tm, :] = n_ref[...]
            w = w_ref[...]
            return (w[0:1] * ext[HALO - 2:HALO - 2 + R, :] + w[1:2] * ext[HALO - 1:HALO - 1 + R, :]
                    + w[2:3] * ext[HALO:HALO + R, :] + b_ref[...])

        val = conv(extv, v_ref, pv_ref, nv_ref, wv_ref, bv_ref)
        gate = conv(extg, g_ref, pg_ref, ng_ref, wg_ref, bg_ref)
        dact_e = jnp.concatenate([da_ref[...], dan_ref[...] * keep_next], axis=0)
        sg, silu = _silu_parts(gate)
        dsv[...] = dact_e * silu
        dsg[...] = dact_e * val * (sg * (1.0 + gate * (1.0 - sg)))

        def back(ds, ext, w_ref, du_ref, dw_ref, db_ref):
            w = w_ref[...]
            du_ref[...] = (w[2:3] * ds[0:tm, :] + w[1:2] * ds[1:1 + tm, :] + w[0:1] * ds[2:2 + tm, :]).astype(du_ref.dtype)
            d0 = ds[0:tm, :]
            for kk in range(3):
                dw_ref[kk:kk + 1, :] += jnp.sum(d0 * ext[HALO - 2 + kk:HALO - 2 + kk + tm, :], axis=0, keepdims=True)
            db_ref[...] += jnp.sum(d0, axis=0, keepdims=True)

        back(dsv, extv, wv_ref, duv_ref, dwv_ref, dbv_ref)
        back(dsg, extg, wg_ref, dug_ref, dwg_ref, dbg_ref)

    prev = lambda i: jnp.maximum(i * hb - 1, 0)
    nxt = lambda i: jnp.minimum((i + 1) * hb, S // HALO - 1)
    tile_v = pl.BlockSpec((tm, tc), lambda j, i: (i, j))
    tile_g = pl.BlockSpec((tm, tc), lambda j, i: (i, j + nc))
    w_v = pl.BlockSpec((3, tc), lambda j, i: (0, j))
    w_g = pl.BlockSpec((3, tc), lambda j, i: (0, j + nc))
    b_v = pl.BlockSpec((1, tc), lambda j, i: (0, j))
    b_g = pl.BlockSpec((1, tc), lambda j, i: (0, j + nc))
    return pl.pallas_call(
        body, name=name, grid=(nc, nr),
        in_specs=[tile_v, tile_g,
                  pl.BlockSpec((HALO, tc), lambda j, i: (prev(i), j)), pl.BlockSpec((HALO, tc), lambda j, i: (prev(i), j + nc)),
                  pl.BlockSpec((HALO, tc), lambda j, i: (nxt(i), j)), pl.BlockSpec((HALO, tc), lambda j, i: (nxt(i), j + nc)),
                  tile_v, pl.BlockSpec((HALO, tc), lambda j, i: (nxt(i), j)), w_v, w_g, b_v, b_g],
        out_specs=(tile_v, tile_v, w_v, w_v, b_v, b_v),
        out_shape=(jax.ShapeDtypeStruct((S, D_FF), MXU), jax.ShapeDtypeStruct((S, D_FF), MXU),
                   jax.ShapeDtypeStruct((3, D_FF), F32), jax.ShapeDtypeStruct((3, D_FF), F32),
                   jax.ShapeDtypeStruct((1, D_FF), F32), jax.ShapeDtypeStruct((1, D_FF), F32)),
        scratch_shapes=[pltpu.VMEM((tm + 2 * HALO, tc), F32), pltpu.VMEM((tm + 2 * HALO, tc), F32),
                        pltpu.VMEM((R, tc), F32), pltpu.VMEM((R, tc), F32)],
        compiler_params=_params(),
    )(up, up, up, up, up, up, dact, dact, cw, cw, cb, cb)


def _glu_fwd(z, h, *, name):
    S = z.shape[0]
    tm = _rows(S, 512)

    def body(z_ref, h_ref, o_ref):
        o_ref[...] = h_ref[...] + z_ref[:, :D_MODEL] * jax.nn.sigmoid(z_ref[:, D_MODEL:])

    return pl.pallas_call(
        body, name=name, grid=(S // tm,), in_specs=[_row_spec(tm, 2 * D_MODEL), _row_spec(tm, D_MODEL)],
        out_specs=_row_spec(tm, D_MODEL), out_shape=jax.ShapeDtypeStruct((S, D_MODEL), F32),
        compiler_params=_params(),
    )(z, h)


def _glu_bwd(z, dm, *, name):
    S = z.shape[0]
    tm = _rows(S, 512)

    def body(z_ref, dm_ref, o_ref):
        sg = jax.nn.sigmoid(z_ref[:, D_MODEL:])
        dmv = dm_ref[...]
        o_ref[:, :D_MODEL] = (dmv * sg).astype(o_ref.dtype)
        o_ref[:, D_MODEL:] = (dmv * z_ref[:, :D_MODEL] * sg * (1.0 - sg)).astype(o_ref.dtype)

    return pl.pallas_call(
        body, name=name, grid=(S // tm,), in_specs=[_row_spec(tm, 2 * D_MODEL), _row_spec(tm, D_MODEL)],
        out_specs=_row_spec(tm, 2 * D_MODEL), out_shape=jax.ShapeDtypeStruct((S, 2 * D_MODEL), MXU),
        compiler_params=_params(),
    )(z, dm)


GELU_C = math.sqrt(2.0 / math.pi)
GELU_A = 0.044715


def _gelu(y):
    return 0.5 * y * (1.0 + jnp.tanh(GELU_C * (y + GELU_A * (y * y * y))))


def _gelu_bwd(y, dg, *, name):
    S = y.shape[0]
    tm = _rows(S, 512)

    def body(y_ref, dg_ref, o_ref):
        yv = y_ref[...]
        t = jnp.tanh(GELU_C * (yv + GELU_A * (yv * yv * yv)))
        d = 0.5 * (1.0 + t) + 0.5 * yv * (1.0 - t * t) * (GELU_C * (1.0 + 3.0 * GELU_A * (yv * yv)))
        o_ref[...] = dg_ref[...] * d

    return pl.pallas_call(
        body, name=name, grid=(S // tm,), in_specs=[_row_spec(tm, D_MODEL), _row_spec(tm, D_MODEL)],
        out_specs=_row_spec(tm, D_MODEL), out_shape=jax.ShapeDtypeStruct((S, D_MODEL), F32),
        compiler_params=_params(),
    )(y, dg)


SW = NSTATE // SBLK
ST2 = 2 * NSTATE


def _s5_fwd(u, wb, wc, abc, dskip, x0, *, full, name):
    S = u.shape[0]
    T = _rows(S, 256)
    nb = S // T
    nj = T // NSEG

    def body(u_ref, wb_ref, wc_ref, a_ref, d_ref, x0_ref, *rest):
        if full:
            xs_ref, y_ref, yg_ref, st = rest
        else:
            e_ref, xs_ref, st = rest
        i = pl.program_id(0)

        @pl.when(i == 0)
        def _():
            st[...] = x0_ref[...]

        uv = u_ref[...]
        ub = uv.astype(MXU)
        for kb in range(SBLK):
            r = jnp.dot(ub[:, kb * LANE:(kb + 1) * LANE], wb_ref[kb], preferred_element_type=F32)
            xs_ref[:, kb * SW:(kb + 1) * SW] = r[:, :SW]
            xs_ref[:, NSTATE + kb * SW:NSTATE + (kb + 1) * SW] = r[:, SW:]
        ar = a_ref[:, :NSTATE]
        ai = a_ref[:, NSTATE:]

        def step(j, c):
            xr, xi = c
            rows = pl.ds(pl.multiple_of(j * NSEG, NSEG), NSEG)
            br = xs_ref[rows, pl.ds(0, NSTATE)]
            bi = xs_ref[rows, pl.ds(NSTATE, NSTATE)]
            nr = ar * xr - ai * xi + br
            ni = ar * xi + ai * xr + bi
            xs_ref[rows, pl.ds(0, NSTATE)] = nr
            xs_ref[rows, pl.ds(NSTATE, NSTATE)] = ni
            return nr, ni

        xr, xi = lax.fori_loop(0, nj, step, (st[:, :NSTATE], st[:, NSTATE:]))
        st[:, :NSTATE] = xr
        st[:, NSTATE:] = xi
        if full:
            for kb in range(SBLK):
                yk = (jnp.dot(xs_ref[:, kb * SW:(kb + 1) * SW].astype(MXU), wc_ref[kb, :SW, :], preferred_element_type=F32)
                      + jnp.dot(xs_ref[:, NSTATE + kb * SW:NSTATE + (kb + 1) * SW].astype(MXU), wc_ref[kb, SW:, :],
                                preferred_element_type=F32))
                cols = slice(kb * LANE, (kb + 1) * LANE)
                yk = yk + d_ref[:, cols] * uv[:, cols]
                y_ref[:, cols] = yk
                yg_ref[:, cols] = _gelu(yk).astype(yg_ref.dtype)
        else:
            @pl.when(i == nb - 1)
            def _():
                e_ref[...] = st[...]

    in_specs = [_row_spec(T, D_MODEL), pl.BlockSpec((SBLK, LANE, 2 * SW), lambda i: (0, 0, 0)),
                pl.BlockSpec((SBLK, 2 * SW, LANE), lambda i: (0, 0, 0)), _const_spec(NSEG, ST2),
                _const_spec(1, D_MODEL), _const_spec(NSEG, ST2)]
    if full:
        out_specs = (_row_spec(T, ST2), _row_spec(T, D_MODEL), _row_spec(T, D_MODEL))
        out_shape = (jax.ShapeDtypeStruct((S, ST2), F32), jax.ShapeDtypeStruct((S, D_MODEL), F32),
                     jax.ShapeDtypeStruct((S, D_MODEL), MXU))
        scratch = [pltpu.VMEM((NSEG, ST2), F32)]
    else:
        out_specs = _const_spec(NSEG, ST2)
        out_shape = jax.ShapeDtypeStruct((NSEG, ST2), F32)
        scratch = [pltpu.VMEM((T, ST2), F32), pltpu.VMEM((NSEG, ST2), F32)]
    return pl.pallas_call(
        body, name=name, grid=(nb,), in_specs=in_specs, out_specs=out_specs, out_shape=out_shape,
        scratch_shapes=scratch, compiler_params=_params(big=True),
    )(u, wb, wc, abc, dskip, x0)


def _s5_bwd(dy, xs, u, wct, wbt, abc, dskip, x0, l0, *, full, name):
    S = dy.shape[0]
    T = _rows(S, 128)
    nb = S // T
    nj = T // NSEG
    blk = lambda i: nb - 1 - i

    def body(dy_ref, *rest):
        if full:
            (xs_ref, xh_ref, u_ref, wct_ref, wbt_ref, a_ref, d_ref, x0_ref, l0_ref,
             du_ref, da_ref, dwb_ref, dwc_ref, dd_ref, g_s, lam_s) = rest
        else:
            wct_ref, a_ref, l0_ref, f_ref, g_s, lam_s = rest
        i = pl.program_id(0)

        @pl.when(i == 0)
        def _():
            lam_s[...] = l0_ref[...]
            if full:
                for r in (da_ref, dwb_ref, dwc_ref, dd_ref):
                    r[...] = jnp.zeros_like(r)

        dyv = dy_ref[...]
        dyb = dyv.astype(MXU)
        for kb in range(SBLK):
            r = jnp.dot(dyb[:, kb * LANE:(kb + 1) * LANE], wct_ref[kb], preferred_element_type=F32)
            g_s[:, kb * SW:(kb + 1) * SW] = r[:, :SW]
            g_s[:, NSTATE + kb * SW:NSTATE + (kb + 1) * SW] = r[:, SW:]
        ar = a_ref[:, :NSTATE]
        ai = a_ref[:, NSTATE:]

        def advance(row, lam):
            lr, li = lam
            rows = pl.ds(row, NSEG)
            nr = g_s[rows, pl.ds(0, NSTATE)] + ar * lr + ai * li
            ni = g_s[rows, pl.ds(NSTATE, NSTATE)] - ai * lr + ar * li
            g_s[rows, pl.ds(0, NSTATE)] = nr
            g_s[rows, pl.ds(NSTATE, NSTATE)] = ni
            return nr, ni

        def accumulate(lam, xpr, xpi):
            nr, ni = lam
            da_ref[:, :NSTATE] += nr * xpr + ni * xpi
            da_ref[:, NSTATE:] += ni * xpr - nr * xpi

        def step(jj, lam):
            row = pl.multiple_of((nj - 1 - jj) * NSEG, NSEG)
            lam = advance(row, lam)
            if full:
                prow = pl.ds(pl.multiple_of(row - NSEG, NSEG), NSEG)
                accumulate(lam, xs_ref[prow, pl.ds(0, NSTATE)], xs_ref[prow, pl.ds(NSTATE, NSTATE)])
            return lam

        lam = lax.fori_loop(0, nj - 1, step, (lam_s[:, :NSTATE], lam_s[:, NSTATE:]))
        lam = advance(0, lam)
        if full:
            first = (blk(i) == 0)
            xp = jnp.where(first, x0_ref[...], xh_ref[...])
            accumulate(lam, xp[:, :NSTATE], xp[:, NSTATE:])
        lam_s[:, :NSTATE] = lam[0]
        lam_s[:, NSTATE:] = lam[1]
        if full:
            uv = u_ref[...]
            ub = uv.astype(MXU)
            dd_ref[...] += jnp.sum(dyv * uv, axis=0, keepdims=True)
            for kb in range(SBLK):
                cols = slice(kb * LANE, (kb + 1) * LANE)
                re = slice(kb * SW, (kb + 1) * SW)
                im = slice(NSTATE + kb * SW, NSTATE + (kb + 1) * SW)
                lr_b = g_s[:, re].astype(MXU)
                li_b = g_s[:, im].astype(MXU)
                duk = (jnp.dot(lr_b, wbt_ref[kb, :SW, :], preferred_element_type=F32)
                       + jnp.dot(li_b, wbt_ref[kb, SW:, :], preferred_element_type=F32))
                du_ref[:, cols] = duk + d_ref[:, cols] * dyv[:, cols]
                dwb_ref[kb, :, :SW] += _dot_tn(ub[:, cols], lr_b)
                dwb_ref[kb, :, SW:] += _dot_tn(ub[:, cols], li_b)
                dwc_ref[kb, :SW, :] += _dot_tn(xs_ref[:, re].astype(MXU), dyb[:, cols])
                dwc_ref[kb, SW:, :] += _dot_tn(xs_ref[:, im].astype(MXU), dyb[:, cols])
        else:
            @pl.when(i == nb - 1)
            def _():
                f_ref[...] = lam_s[...]

    rev = lambda c: pl.BlockSpec((T, c), lambda i: (blk(i), 0))
    w3 = lambda a, b: pl.BlockSpec((SBLK, a, b), lambda i: (0, 0, 0))
    if full:
        hb = T // NSEG
        in_specs = [rev(D_MODEL), rev(ST2),
                    pl.BlockSpec((NSEG, ST2), lambda i: (jnp.maximum(blk(i) * hb - 1, 0), 0)),
                    rev(D_MODEL), w3(LANE, 2 * SW), w3(2 * SW, LANE), _const_spec(NSEG, ST2),
                    _const_spec(1, D_MODEL), _const_spec(NSEG, ST2), _const_spec(NSEG, ST2)]
        ops = [dy, xs, xs, u, wct, wbt, abc, dskip, x0, l0]
        out_specs = (rev(D_MODEL), _const_spec(NSEG, ST2), w3(LANE, 2 * SW), w3(2 * SW, LANE),
                     _const_spec(1, D_MODEL))
        out_shape = (jax.ShapeDtypeStruct((S, D_MODEL), F32), jax.ShapeDtypeStruct((NSEG, ST2), F32),
                     jax.ShapeDtypeStruct((SBLK, LANE, 2 * SW), F32), jax.ShapeDtypeStruct((SBLK, 2 * SW, LANE), F32),
                     jax.ShapeDtypeStruct((1, D_MODEL), F32))
    else:
        in_specs = [rev(D_MODEL), w3(LANE, 2 * SW), _const_spec(NSEG, ST2), _const_spec(NSEG, ST2)]
        ops = [dy, wct, abc, l0]
        out_specs = _const_spec(NSEG, ST2)
        out_shape = jax.ShapeDtypeStruct((NSEG, ST2), F32)
    return pl.pallas_call(
        body, name=name, grid=(nb,), in_specs=in_specs, out_specs=out_specs, out_shape=out_shape,
        scratch_shapes=[pltpu.VMEM((T, ST2), F32), pltpu.VMEM((NSEG, ST2), F32)],
        compiler_params=_params(big=True),
    )(*ops)


def _s5_discretize(lr, li, log_dt, br, bi):
    dt = jnp.exp(log_dt)[:, None]
    mag = jnp.exp(lr * dt)
    ar = mag * jnp.cos(li * dt)
    ai = mag * jnp.sin(li * dt)
    den = lr * lr + li * li
    nr = ar - 1.0
    coef_r = (nr * lr + ai * li) / den
    coef_i = (ai * lr - nr * li) / den
    bbar_r = coef_r[..., None] * br - coef_i[..., None] * bi
    bbar_i = coef_r[..., None] * bi + coef_i[..., None] * br
    return ar, ai, bbar_r, bbar_i


def _blockdiag(m):
    gpb = SSM_GROUPS // SBLK
    a, b = m.shape[1:]
    mb = m.reshape(SBLK, gpb, a, b)
    eye = jnp.eye(gpb, dtype=m.dtype)
    return jnp.einsum('kgab,gh->kgahb', mb, eye).reshape(SBLK, gpb * a, gpb * b)


def _blockdiag_extract(w, a, b):
    gpb = SSM_GROUPS // SBLK
    w5 = w.reshape(SBLK, gpb, a, gpb, b)
    return jnp.einsum('kgahb,gh->kgab', w5, jnp.eye(gpb, dtype=w.dtype)).reshape(SSM_GROUPS, a, b)


def _cpow(ar, ai, n):
    rr, ri = jnp.ones_like(ar), jnp.zeros_like(ai)
    br, bi = ar, ai
    while n:
        if n & 1:
            rr, ri = rr * br - ri * bi, rr * bi + ri * br
        br, bi = br * br - bi * bi, 2.0 * br * bi
        n >>= 1
    return rr, ri


def _perm(a):
    s, c = a.shape
    return a.reshape(NSEG, s // NSEG, c).transpose(1, 0, 2).reshape(s, c)


def _unperm(a):
    s, c = a.shape
    return a.reshape(s // NSEG, NSEG, c).transpose(1, 0, 2).reshape(s, c)


PACKW = 1024
PACK_ROWS = 512


def _other_chips(x, y):
    return [(1 - x, y), (x, 1 - y), (1 - x, 1 - y)]


def _chip_exchange(src, *, gather, name):
    shape = src.shape[-2:]

    def body(s_ref, o_ref, send_sems, recv_sems, local_sem):
        x, y, c = lax.axis_index("x"), lax.axis_index("y"), lax.axis_index("c")
        me = 2 * x + y

        def part(chip):
            return s_ref if gather else s_ref.at[chip]

        def remote(k, tx, ty):
            return pltpu.make_async_remote_copy(
                src_ref=part(2 * tx + ty), dst_ref=o_ref.at[me], send_sem=send_sems.at[k],
                recv_sem=recv_sems.at[k], device_id=(tx, ty, c), device_id_type=MESH)

        def arrival(k, tx, ty):
            return pltpu.make_async_remote_copy(
                src_ref=part(me), dst_ref=o_ref.at[2 * tx + ty], send_sem=send_sems.at[k],
                recv_sem=recv_sems.at[k], device_id=(tx, ty, c), device_id_type=MESH)

        mine = pltpu.make_async_copy(part(me), o_ref.at[me], local_sem)
        mine.start()
        sends = [remote(k, tx, ty) for k, (tx, ty) in enumerate(_other_chips(x, y))]
        for cp in sends:
            cp.start()
        for k, (tx, ty) in enumerate(_other_chips(x, y)):
            arrival(k, tx, ty).wait_recv()
        for cp in sends:
            cp.wait_send()
        mine.wait()

    return pl.pallas_call(
        body, name=name, in_specs=[pl.BlockSpec(memory_space=pl.ANY)], out_specs=pl.BlockSpec(memory_space=pl.ANY),
        out_shape=jax.ShapeDtypeStruct((NCHIP,) + shape, src.dtype),
        scratch_shapes=[pltpu.SemaphoreType.DMA((NCHIP - 1,)), pltpu.SemaphoreType.DMA((NCHIP - 1,)),
                        pltpu.SemaphoreType.DMA],
    )(src)


def _sibling_exchange(src, *, name):
    def body(s_ref, o_ref, send_sem, recv_sem):
        x, y, c = lax.axis_index("x"), lax.axis_index("y"), lax.axis_index("c")
        cp = pltpu.make_async_remote_copy(src_ref=s_ref, dst_ref=o_ref, send_sem=send_sem, recv_sem=recv_sem,
                                          device_id=(x, y, 1 - c), device_id_type=MESH)
        cp.start()
        cp.wait()

    return pl.pallas_call(
        body, name=name, in_specs=[pl.BlockSpec(memory_space=pl.ANY)], out_specs=pl.BlockSpec(memory_space=pl.ANY),
        out_shape=jax.ShapeDtypeStruct(src.shape, src.dtype),
        scratch_shapes=[pltpu.SemaphoreType.DMA, pltpu.SemaphoreType.DMA],
    )(src)


def _sum_chips(r, *, name):
    _, R, W = r.shape
    tm = _rows(R, 512)

    def body(r_ref, o_ref):
        o_ref[...] = ((r_ref[0] + r_ref[1]) + r_ref[2]) + r_ref[3]

    return pl.pallas_call(
        body, name=name, grid=(R // tm,), in_specs=[pl.BlockSpec((NCHIP, tm, W), lambda i: (0, i, 0))],
        out_specs=_row_spec(tm, W), out_shape=jax.ShapeDtypeStruct((R, W), F32), compiler_params=_params(),
    )(r)


def _adamw(p_mine, p_sib, w, m, v, *, name):
    R, W = w.shape
    tm = _rows(R, 256)

    def body(a_ref, b_ref, w_ref, m_ref, v_ref, g_ref, d_ref, nm_ref, nv_ref):
        g = a_ref[...] + b_ref[...]
        mm = ADAM_B1 * m_ref[...] + (1.0 - ADAM_B1) * g
        vv = ADAM_B2 * v_ref[...] + (1.0 - ADAM_B2) * (g * g)
        m_hat = mm / (1.0 - ADAM_B1 ** ADAM_STEP)
        v_hat = vv / (1.0 - ADAM_B2 ** ADAM_STEP)
        g_ref[...] = g
        d_ref[...] = -ADAM_LR * (m_hat / (jnp.sqrt(v_hat) + ADAM_EPS) + ADAM_WD * w_ref[...])
        nm_ref[...] = mm
        nv_ref[...] = vv

    spec = _row_spec(tm, W)
    shp = jax.ShapeDtypeStruct((R, W), F32)
    return pl.pallas_call(
        body, name=name, grid=(R // tm,), in_specs=[spec] * 5, out_specs=(spec,) * 4, out_shape=(shp,) * 4,
        compiler_params=_params(),
    )(p_mine, p_sib, w, m, v)


def _pack_rows(flat, mult):
    n = flat.shape[-1]
    unit = mult * PACKW
    pad = (-n) % unit
    if pad:
        flat = jnp.pad(flat, [(0, 0)] * (flat.ndim - 1) + [(0, pad)])
    return flat.reshape(flat.shape[:-1] + ((n + pad) // PACKW, PACKW))


def _as_payload(a):
    if jnp.dtype(MXU).itemsize == 2:
        return lax.bitcast_convert_type(a, MXU)
    return a


def _from_payload(a):
    if jnp.dtype(MXU).itemsize == 2:
        return lax.bitcast_convert_type(a.reshape(a.shape[:-1] + (a.shape[-1] // 2, 2)), F32)
    return a


def _gather_weights(w):
    parts, meta = [], []
    for n in GATHER_MXU:
        parts.append(w[n].astype(MXU).reshape(-1))
    for n in GATHER_F32:
        parts.append(_as_payload(w[n]).reshape(-1))
    for n, p in zip(GATHER_MXU + GATHER_F32, parts):
        meta.append((n, p.shape[0]))
    flat = jnp.concatenate(parts)
    total = flat.shape[0]
    got = _chip_exchange(_pack_rows(flat, 16), gather=True, name="gather_weights")
    got = got.reshape(NCHIP, -1)[:, :total]
    full, off = {}, 0
    for n, ln in meta:
        seg = got[:, off:off + ln]
        off += ln
        if n in GATHER_F32:
            seg = _from_payload(seg)
        local = w[n].shape
        ax = SHARD_AXIS[n]
        seg = seg.reshape((NCHIP,) + local)
        seg = jnp.moveaxis(seg, 0, ax)
        full[n] = seg.reshape(local[:ax] + (NCHIP * local[ax],) + local[ax + 1:])
    return full


def _pack_for_chips(g):
    rows = []
    for t in range(NCHIP):
        parts = []
        for n in WNAMES:
            a = g[n]
            if n in SHARD_AXIS:
                ax = SHARD_AXIS[n]
                sz = a.shape[ax] // NCHIP
                a = lax.slice_in_dim(a, t * sz, (t + 1) * sz, axis=ax)
            parts.append(a.reshape(-1))
        rows.append(jnp.concatenate(parts))
    return _pack_rows(jnp.stack(rows), PACK_ROWS)


def _pack_local(a_by_name):
    return _pack_rows(jnp.concatenate([a_by_name[n].reshape(-1) for n in WNAMES]), PACK_ROWS)


def _unpack_local(buf, like):
    flat = buf.reshape(-1)
    out, off = {}, 0
    for n in WNAMES:
        sz = math.prod(like[n].shape)
        out[n] = flat[off:off + sz].reshape(like[n].shape)
        off += sz
    return out


def kernel(x, positions, mla_w_a, mla_g_q, mla_g_kv, mla_w_uq, mla_w_ukv, mla_w_o, ssm_w_in, ssm_lambda_re, ssm_lambda_im, ssm_log_dt, ssm_b_re, ssm_b_im, ssm_c_re, ssm_c_im, ssm_d, ssm_w_glu, ffn_w_up, ffn_conv_w, ffn_conv_b, ffn_w_down, g_mix, g_ffn, g_final, loss_target, m_mla_w_a, m_mla_g_q, m_mla_g_kv, m_mla_w_uq, m_mla_w_ukv, m_mla_w_o, m_ssm_w_in, m_ssm_lambda_re, m_ssm_lambda_im, m_ssm_log_dt, m_ssm_b_re, m_ssm_b_im, m_ssm_c_re, m_ssm_c_im, m_ssm_d, m_ssm_w_glu, m_ffn_w_up, m_ffn_conv_w, m_ffn_conv_b, m_ffn_w_down, m_g_mix, m_g_ffn, m_g_final, v_mla_w_a, v_mla_g_q, v_mla_g_kv, v_mla_w_uq, v_mla_w_ukv, v_mla_w_o, v_ssm_w_in, v_ssm_lambda_re, v_ssm_lambda_im, v_ssm_log_dt, v_ssm_b_re, v_ssm_b_im, v_ssm_c_re, v_ssm_c_im, v_ssm_d, v_ssm_w_glu, v_ffn_w_up, v_ffn_conv_w, v_ffn_conv_b, v_ffn_w_down, v_g_mix, v_g_ffn, v_g_final):
    w = dict(zip(WNAMES, (mla_w_a, mla_g_q, mla_g_kv, mla_w_uq, mla_w_ukv, mla_w_o, ssm_w_in, ssm_lambda_re,
                          ssm_lambda_im, ssm_log_dt, ssm_b_re, ssm_b_im, ssm_c_re, ssm_c_im, ssm_d, ssm_w_glu,
                          ffn_w_up, ffn_conv_w, ffn_conv_b, ffn_w_down, g_mix, g_ffn, g_final)))
    mom = dict(zip(WNAMES, (m_mla_w_a, m_mla_g_q, m_mla_g_kv, m_mla_w_uq, m_mla_w_ukv, m_mla_w_o, m_ssm_w_in,
                            m_ssm_lambda_re, m_ssm_lambda_im, m_ssm_log_dt, m_ssm_b_re, m_ssm_b_im, m_ssm_c_re,
                            m_ssm_c_im, m_ssm_d, m_ssm_w_glu, m_ffn_w_up, m_ffn_conv_w, m_ffn_conv_b,
                            m_ffn_w_down, m_g_mix, m_g_ffn, m_g_final)))
    var = dict(zip(WNAMES, (v_mla_w_a, v_mla_g_q, v_mla_g_kv, v_mla_w_uq, v_mla_w_ukv, v_mla_w_o, v_ssm_w_in,
                            v_ssm_lambda_re, v_ssm_lambda_im, v_ssm_log_dt, v_ssm_b_re, v_ssm_b_im, v_ssm_c_re,
                            v_ssm_c_im, v_ssm_d, v_ssm_w_glu, v_ffn_w_up, v_ffn_conv_w, v_ffn_conv_b,
                            v_ffn_w_down, v_g_mix, v_g_ffn, v_g_final)))
    S = x.shape[1]
    D = D_MODEL
    x2 = x.reshape(S, D)
    tgt = loss_target.reshape(S, D)

    fw = _gather_weights(w)
    w_a = jnp.pad(fw['mla_w_a'][0], ((0, 0), (0, A_PAD - KR0 - QK_ROPE)))
    uq = fw['mla_w_uq'][0].reshape(Q_LORA, HEADS, QK_DIM)
    w_uq = jnp.concatenate([uq[:, :, :QK_NOPE].reshape(Q_LORA, HEADS * QK_NOPE),
                            jnp.pad(uq[:, :, QK_NOPE:], ((0, 0), (0, 0), (0, LANE - QK_ROPE))).reshape(Q_LORA, HEADS * LANE)],
                           axis=1)
    w_ukv = fw['mla_w_ukv'][0]
    w_o = fw['mla_w_o'][0]
    w_in = fw['ssm_w_in'][0]
    w_glu = fw['ssm_w_glu'][0]
    w_up = fw['ffn_w_up']
    w_down = fw['ffn_w_down']
    conv_w = fw['ffn_conv_w']
    dskip = fw['ssm_d']
    conv_b = w['ffn_conv_b']
    g_q, g_kv = w['mla_g_q'], w['mla_g_kv']
    gm, gf = w['g_mix'], w['g_ffn']
    gfin = w['g_final'].reshape(1, D)

    inv = 1.0 / (ROPE_THETA ** (jnp.arange(0, QK_ROPE, 2, dtype=F32) / QK_ROPE))
    ang = positions.reshape(S).astype(F32)[:, None] * inv
    cos, sin = jnp.cos(ang), jnp.sin(ang)
    zpad = jnp.zeros((S, LANE - QK_ROPE), F32)
    c128 = jnp.concatenate([cos, cos, zpad], axis=1)
    s128 = jnp.concatenate([-sin, sin, zpad], axis=1)

    hn0 = _rmsnorm_fwd(x2, gm[0:1], name="rms_mix0")
    a = _mm(hn0, w_a, name="mla_a")
    cqn, ckvn, kr = _mla_mid_fwd(a, g_q, g_kv, c128, s128, name="mla_mid_fwd")
    qfull = _mm(cqn, w_uq, name="mla_q")
    kv = _mm(ckvn, w_ukv, out_dtype=MXU, name="mla_kv")
    qs, ks, vs = _qk_prep(qfull, kv, kr, c128, s128, name="qk_prep")
    os_, lses = [], []
    for h in range(HEADS):
        o_h, lse_h = _attn_fwd(qs[h], ks[h], vs[h], name="attn_fwd")
        os_.append(o_h)
        lses.append(lse_h)
    o_cat = jnp.concatenate(os_, axis=1)
    h1 = _mm(o_cat, w_o, res=x2, name="mla_o")

    def ffn_fwd(h, l):
        hn = _rmsnorm_fwd(h, gf[l:l + 1], name="rms_ffn")
        up = _mm(hn, w_up[l], name="ffn_up")
        act = _convgate_fwd(up, conv_w[l], conv_b[l:l + 1], name="convgate_fwd")
        return _mm(act, w_down[l], res=h, name="ffn_down"), (hn, up, act)

    h2, saved0 = ffn_fwd(h1, 0)

    lam_re, lam_im, log_dt = w['ssm_lambda_re'][0], w['ssm_lambda_im'][0], w['ssm_log_dt'][0]
    (a_re, a_im, bbar_r, bbar_i), disc_vjp = jax.vjp(_s5_discretize, lam_re, lam_im, log_dt, w['ssm_b_re'][0],
                                                    w['ssm_b_im'][0])
    c_re, c_im = w['ssm_c_re'][0], w['ssm_c_im'][0]
    bt_r, bt_i = jnp.swapaxes(bbar_r, 1, 2), jnp.swapaxes(bbar_i, 1, 2)
    wb = jnp.concatenate([_blockdiag(bt_r), _blockdiag(bt_i)], axis=2).astype(MXU)
    wbt = jnp.concatenate([_blockdiag(bbar_r), _blockdiag(bbar_i)], axis=1).astype(MXU)
    ct_r, ct_i = jnp.swapaxes(c_re, 1, 2), jnp.swapaxes(c_im, 1, 2)
    wc = jnp.concatenate([_blockdiag(ct_r), _blockdiag(-ct_i)], axis=1).astype(MXU)
    wct = jnp.concatenate([_blockdiag(c_re), _blockdiag(-c_im)], axis=2).astype(MXU)
    af_r, af_i = a_re.reshape(NSTATE), a_im.reshape(NSTATE)
    abc = jnp.broadcast_to(jnp.concatenate([af_r, af_i])[None], (NSEG, ST2))
    seg = S // NSEG
    ap_r, ap_i = _cpow(af_r, af_i, seg)

    hn1 = _rmsnorm_fwd(h2, gm[1:2], name="rms_mix1")
    u = _mm(hn1, w_in, name="s5_in")
    u_p = _perm(u)
    zero_state = jnp.zeros((NSEG, ST2), F32)
    ends = _s5_fwd(u_p, wb, wc, abc, dskip, zero_state, full=False, name="s5_fwd_ends")
    inits, cr, ci = [], jnp.zeros((NSTATE,), F32), jnp.zeros((NSTATE,), F32)
    for r in range(NSEG):
        inits.append(jnp.concatenate([cr, ci]))
        er, ei = ends[r, :NSTATE], ends[r, NSTATE:]
        cr, ci = er + ap_r * cr - ap_i * ci, ei + ap_r * ci + ap_i * cr
    x0 = jnp.stack(inits)
    xs, y_p, yg_p = _s5_fwd(u_p, wb, wc, abc, dskip, x0, full=True, name="s5_fwd")
    yg = _unperm(yg_p)
    z = _mm(yg, w_glu, name="s5_glu")
    h3 = _glu_fwd(z, h2, name="glu_fwd")
    h4, saved1 = ffn_fwd(h3, 1)

    loss_l, dh4, dg_final = _loss_head(h4, gfin, tgt, name="loss_head")

    grads = {}

    def ffn_bwd(h_in, g, saved, l):
        hn, up, act = saved
        w_up_t = w_up[l].T
        dact = _mm(g, w_down[l].T, name="ffn_down_dx")
        dw_down = _mm(act, g, mode="tn", name="ffn_down_dw")
        duv, dug, dwv, dwg, dbv, dbg = _convgate_bwd(up, dact, conv_w[l], conv_b[l:l + 1], name="convgate_bwd")
        dw_up = jnp.concatenate([_mm(hn, duv, mode="tn", name="ffn_up_dw"), _mm(hn, dug, mode="tn", name="ffn_up_dw")],
                                axis=1)
        dhn = _mm(duv, w_up_t[:D_FF], name="ffn_up_dx")
        dhn = _mm(dug, w_up_t[D_FF:], res=dhn, name="ffn_up_dx_acc")
        dh, dg = _rmsnorm_bwd(h_in, gf[l:l + 1], dhn, g, name="rms_ffn_bwd")
        return dh, dict(w_up=dw_up, w_down=dw_down, conv_w=jnp.concatenate([dwv, dwg], axis=1),
                        conv_b=jnp.concatenate([dbv, dbg], axis=1)[0], g_ffn=dg[0])

    dh3, fg1 = ffn_bwd(h3, dh4, saved1, 1)

    dz = _glu_bwd(z, dh3, name="glu_bwd")
    grads['ssm_w_glu'] = _mm(yg, dz, mode="tn", name="s5_glu_dw")[None]
    dyg = _mm(dz, w_glu.T, name="s5_glu_dx")
    dy_p = _gelu_bwd(y_p, _perm(dyg), name="gelu_bwd")
    firsts = _s5_bwd(dy_p, None, None, wct, None, abc, None, None, zero_state, full=False, name="s5_bwd_firsts")
    linits, cr, ci = [None] * NSEG, jnp.zeros((NSTATE,), F32), jnp.zeros((NSTATE,), F32)
    for r in reversed(range(NSEG)):
        linits[r] = jnp.concatenate([cr, ci])
        fr, fi = firsts[r, :NSTATE], firsts[r, NSTATE:]
        cr, ci = fr + ap_r * cr + ap_i * ci, fi + ap_r * ci - ap_i * cr
    l0 = jnp.stack(linits)
    du_p, dab, dwb, dwc, dd = _s5_bwd(dy_p, xs, u_p, wct, wbt, abc, dskip, x0, l0, full=True, name="s5_bwd")
    du = _unperm(du_p)
    grads['ssm_w_in'] = _mm(hn1, du, mode="tn", name="s5_in_dw")[None]
    dhn1 = _mm(du, w_in.T, name="s5_in_dx")
    dh2, dg_mix1 = _rmsnorm_bwd(h2, gm[1:2], dhn1, dh3, name="rms_mix_bwd")
    da_sum = jnp.sum(dab, axis=0)
    dbt_r = _blockdiag_extract(dwb[:, :, :SW], SSM_GROUP, SSM_STATE)
    dbt_i = _blockdiag_extract(dwb[:, :, SW:], SSM_GROUP, SSM_STATE)
    dlr, dli, dlog_dt, dbr, dbi = disc_vjp((da_sum[:NSTATE].reshape(SSM_GROUPS, SSM_STATE),
                                            da_sum[NSTATE:].reshape(SSM_GROUPS, SSM_STATE),
                                            jnp.swapaxes(dbt_r, 1, 2), jnp.swapaxes(dbt_i, 1, 2)))
    dct_r = _blockdiag_extract(dwc[:, :SW, :], SSM_STATE, SSM_GROUP)
    dct_i = _blockdiag_extract(dwc[:, SW:, :], SSM_STATE, SSM_GROUP)
    grads['ssm_lambda_re'], grads['ssm_lambda_im'], grads['ssm_log_dt'] = dlr[None], dli[None], dlog_dt[None]
    grads['ssm_b_re'], grads['ssm_b_im'] = dbr[None], dbi[None]
    grads['ssm_c_re'] = jnp.swapaxes(dct_r, 1, 2)[None]
    grads['ssm_c_im'] = -jnp.swapaxes(dct_i, 1, 2)[None]
    grads['ssm_d'] = dd

    dh1, fg0 = ffn_bwd(h1, dh2, saved0, 0)
    grads['ffn_w_up'] = jnp.stack([fg0['w_up'], fg1['w_up']])
    grads['ffn_w_down'] = jnp.stack([fg0['w_down'], fg1['w_down']])
    grads['ffn_conv_w'] = jnp.stack([fg0['conv_w'], fg1['conv_w']])
    grads['ffn_conv_b'] = jnp.stack([fg0['conv_b'], fg1['conv_b']])
    grads['g_ffn'] = jnp.stack([fg0['g_ffn'], fg1['g_ffn']])

    do_cat = _mm(dh1, w_o.T, out_dtype=MXU, name="mla_o_dx")
    grads['mla_w_o'] = _mm(o_cat, dh1, mode="tn", name="mla_o_dw")[None]
    dqs, dks, dvs = [], [], []
    for h in range(HEADS):
        do_h = do_cat[:, h * V_HEAD:(h + 1) * V_HEAD]
        delta = _attn_delta(do_h, os_[h], name="attn_delta")
        dq_h, dk_h, dv_h = _attn_bwd(qs[h], ks[h], vs[h], do_h, lses[h].reshape(1, S), delta.reshape(1, S),
                                     name="attn_bwd")
        dqs.append(dq_h)
        dks.append(dk_h)
        dvs.append(dv_h)
    dqfull, dkv, dkr = _qk_prep_bwd(dqs, dks, dvs, c128, s128, name="qk_prep_bwd")
    dw_uq_p = _mm(cqn, dqfull, mode="tn", name="mla_q_dw")
    dcqn = _mm(dqfull, w_uq.T, name="mla_q_dx")
    grads['mla_w_ukv'] = _mm(ckvn, dkv, mode="tn", name="mla_kv_dw")[None]
    dckvn = _mm(dkv, w_ukv.T, name="mla_kv_dx")
    da, dgq, dgkv = _mla_mid_bwd(a, dcqn, dckvn, dkr, g_q, g_kv, c128, s128, name="mla_mid_bwd")
    grads['mla_w_a'] = _mm(hn0, da, mode="tn", name="mla_a_dw")[None, :, :KR0 + QK_ROPE]
    dhn0 = _mm(da, w_a.T, name="mla_a_dx")
    dx, dg_mix0 = _rmsnorm_bwd(x2, gm[0:1], dhn0, dh1, name="rms_mix_bwd")
    grads['mla_w_uq'] = jnp.concatenate(
        [dw_uq_p[:, :HEADS * QK_NOPE].reshape(Q_LORA, HEADS, QK_NOPE),
         dw_uq_p[:, HEADS * QK_NOPE:].reshape(Q_LORA, HEADS, LANE)[:, :, :QK_ROPE]], axis=2).reshape(1, Q_LORA, HEADS * QK_DIM)
    grads['mla_g_q'], grads['mla_g_kv'] = dgq, dgkv
    grads['g_mix'] = jnp.concatenate([dg_mix0, dg_mix1], axis=0)
    grads['g_final'] = dg_final[0]

    landed = _chip_exchange(_pack_for_chips(grads), gather=False, name="grad_exchange")
    partial = _sum_chips(landed, name="grad_sum_chips")
    sibling = _sibling_exchange(partial, name="grad_sibling")
    g_p, d_p, m_p, v_p = _adamw(partial, sibling, _pack_local(w), _pack_local(mom), _pack_local(var), name="adamw")
    g_out, d_out = _unpack_local(g_p, w), _unpack_local(d_p, w)
    m_out, v_out = _unpack_local(m_p, w), _unpack_local(v_p, w)

    loss = lax.psum(loss_l[0, 0], ("x", "y", "c"))
    return (loss, dx.reshape(1, S, D), *[g_out[n] for n in WNAMES], *[d_out[n] for n in WNAMES],
            *[m_out[n] for n in WNAMES], *[v_out[n] for n in WNAMES])
```

```python
import math

import jax
import jax.numpy as jnp
from jax import lax
from jax.experimental import pallas as pl
from jax.experimental.pallas import tpu as pltpu

F32 = jnp.float32
MXU = jnp.bfloat16

D_MODEL = 1024
CHUNK = 64
EPS = 1e-6
HEADS = 8
QK_NOPE = 128
QK_ROPE = 64
V_HEAD = 128
Q_LORA = 384
KV_LORA = 256
ROPE_THETA = 10000.0
QK_DIM = QK_NOPE + QK_ROPE
SSM_GROUP = 16
SSM_GROUPS = D_MODEL // SSM_GROUP
SSM_STATE = 64
NSTATE = SSM_GROUPS * SSM_STATE
D_FF = 2816
ATT_SCALE = QK_DIM ** -0.5
EXP2_SCALE = ATT_SCALE * math.log2(math.e)
NEG = -1e30
NSEG = 8
SBLK = 8

ADAM_LR = 0.001
ADAM_B1 = 0.9
ADAM_B2 = 0.999
ADAM_EPS = 1e-08
ADAM_WD = 0.01
ADAM_STEP = 10

LANE = 128
TN_MAX_COLS = 2816
TN_ACC_ELEMS = 1536 * 1024
VMEM_BIG = 56 * 1024 * 1024

WNAMES = ['mla_w_a', 'mla_g_q', 'mla_g_kv', 'mla_w_uq', 'mla_w_ukv', 'mla_w_o', 'ssm_w_in', 'ssm_lambda_re',
          'ssm_lambda_im', 'ssm_log_dt', 'ssm_b_re', 'ssm_b_im', 'ssm_c_re', 'ssm_c_im', 'ssm_d', 'ssm_w_glu',
          'ffn_w_up', 'ffn_conv_w', 'ffn_conv_b', 'ffn_w_down', 'g_mix', 'g_ffn', 'g_final']
FWD_NAMES = ['x', 'positions'] + WNAMES
SHARD_AXIS = {'mla_w_a': 1, 'mla_w_uq': 2, 'mla_w_ukv': 2, 'mla_w_o': 1, 'ssm_w_in': 1, 'ssm_d': 1,
              'ssm_w_glu': 2, 'ffn_w_up': 2, 'ffn_conv_w': 2, 'ffn_w_down': 1}
GATHER_MXU = ['mla_w_a', 'mla_w_uq', 'mla_w_ukv', 'mla_w_o', 'ssm_w_in', 'ssm_w_glu', 'ffn_w_up', 'ffn_w_down']
GATHER_F32 = ['ssm_d', 'ffn_conv_w']
NCHIP = 4
PACKW = 1024
SMALL_ROWS = 64
MESH = pl.DeviceIdType.MESH


def _tile(d, pref):
    t = min(pref, d) // LANE * LANE
    while t >= LANE:
        if d % t == 0:
            return t
        t -= LANE
    return d


def _rows(s, pref):
    t = min(s, pref)
    assert s % t == 0 and t % 8 == 0
    return t


def _params(big=False):
    if big:
        return pltpu.CompilerParams(vmem_limit_bytes=VMEM_BIG)
    return pltpu.CompilerParams(vmem_limit_bytes=40 * 1024 * 1024)


def _mm(a, b, *, name, mode="nn", out_dtype=F32, res=None, tm=None, tn=None, tk=None):
    if mode == "nn":
        M, K = a.shape
        tm = tm or _rows(M, 1024)
    else:
        K, M = a.shape
        tn = tn or _tile(b.shape[1], TN_MAX_COLS)
        tm = tm or _tile(M, max(LANE, TN_ACC_ELEMS // tn))
    N = b.shape[1]
    assert b.shape[0] == K
    tn = tn or _tile(N, 512)
    tk = tk or (_tile(K, 1408) if mode == "nn" else _rows(K, 512))
    nk = K // tk
    has_res = res is not None

    def body(a_ref, b_ref, *rest):
        if has_res:
            r_ref, o_ref, acc = rest
        else:
            o_ref, acc = rest
        k = pl.program_id(2)

        @pl.when(k == 0)
        def _():
            acc[...] = jnp.zeros_like(acc)

        av = a_ref[...].astype(MXU)
        bv = b_ref[...].astype(MXU)
        if mode == "nn":
            acc[...] += jnp.dot(av, bv, preferred_element_type=F32)
        else:
            acc[...] += lax.dot_general(av, bv, (((0,), (0,)), ((), ())), preferred_element_type=F32)

        @pl.when(k == nk - 1)
        def _():
            o = acc[...]
            if has_res:
                o = o + r_ref[...]
            o_ref[...] = o.astype(o_ref.dtype)

    if mode == "nn":
        a_spec = pl.BlockSpec((tm, tk), lambda i, j, k: (i, k))
    else:
        a_spec = pl.BlockSpec((tk, tm), lambda i, j, k: (k, i))
    in_specs = [a_spec, pl.BlockSpec((tk, tn), lambda i, j, k: (k, j))]
    ops = [a, b]
    if has_res:
        in_specs.append(pl.BlockSpec((tm, tn), lambda i, j, k: (i, j)))
        ops.append(res)
    return pl.pallas_call(
        body, name=name, grid=(M // tm, N // tn, nk), in_specs=in_specs,
        out_specs=pl.BlockSpec((tm, tn), lambda i, j, k: (i, j)),
        out_shape=jax.ShapeDtypeStruct((M, N), out_dtype),
        scratch_shapes=[pltpu.VMEM((tm, tn), F32)], compiler_params=_params(),
    )(*ops)


def _row_spec(tm, c):
    return pl.BlockSpec((tm, c), lambda i: (i, 0))


def _const_spec(r, c):
    return pl.BlockSpec((r, c), lambda i: (0, 0))


def _rms_parts(xv):
    r = lax.rsqrt(jnp.mean(xv * xv, axis=-1, keepdims=True) + EPS)
    return r, xv * r


def _rms_vjp(xv, gv, dyv):
    r, xhat = _rms_parts(xv)
    gy = dyv * gv
    dx = r * (gy - xhat * jnp.mean(gy * xhat, axis=-1, keepdims=True))
    return dx, dyv * xhat


def _rmsnorm_fwd(x, g, *, name):
    S, D = x.shape
    tm = _rows(S, 512)

    def body(x_ref, g_ref, o_ref):
        _, xhat = _rms_parts(x_ref[...])
        o_ref[...] = (xhat * g_ref[...]).astype(o_ref.dtype)

    return pl.pallas_call(
        body, name=name, grid=(S // tm,), in_specs=[_row_spec(tm, D), _const_spec(1, D)],
        out_specs=_row_spec(tm, D), out_shape=jax.ShapeDtypeStruct((S, D), MXU), compiler_params=_params(),
    )(x, g)


def _rmsnorm_bwd(x, g, dy, dres, *, name):
    S, D = x.shape
    tm = _rows(S, 512)

    def body(x_ref, g_ref, dy_ref, dr_ref, dx_ref, dg_ref):
        @pl.when(pl.program_id(0) == 0)
        def _():
            dg_ref[...] = jnp.zeros_like(dg_ref)

        dx, dgp = _rms_vjp(x_ref[...], g_ref[...], dy_ref[...])
        dx_ref[...] = dr_ref[...] + dx
        dg_ref[...] += jnp.sum(dgp, axis=0, keepdims=True)

    return pl.pallas_call(
        body, name=name, grid=(S // tm,),
        in_specs=[_row_spec(tm, D), _const_spec(1, D), _row_spec(tm, D), _row_spec(tm, D)],
        out_specs=(_row_spec(tm, D), _const_spec(1, D)),
        out_shape=(jax.ShapeDtypeStruct((S, D), F32), jax.ShapeDtypeStruct((1, D), F32)),
        compiler_params=_params(),
    )(x, g, dy, dres)


def _loss_head(h, g, tgt, *, name):
    S, D = h.shape
    tm = _rows(S, 512)

    def body(h_ref, g_ref, t_ref, l_ref, dh_ref, dg_ref):
        @pl.when(pl.program_id(0) == 0)
        def _():
            l_ref[...] = jnp.zeros_like(l_ref)
            dg_ref[...] = jnp.zeros_like(dg_ref)

        hv = h_ref[...]
        gv = g_ref[...]
        _, xhat = _rms_parts(hv)
        e = xhat * gv - t_ref[...]
        l_ref[...] += 0.5 * jnp.sum(jnp.mean(e * e, axis=-1, keepdims=True), axis=0, keepdims=True)
        dx, dgp = _rms_vjp(hv, gv, e * (1.0 / D))
        dh_ref[...] = dx
        dg_ref[...] += jnp.sum(dgp, axis=0, keepdims=True)

    return pl.pallas_call(
        body, name=name, grid=(S // tm,),
        in_specs=[_row_spec(tm, D), _const_spec(1, D), _row_spec(tm, D)],
        out_specs=(_const_spec(1, 1), _row_spec(tm, D), _const_spec(1, D)),
        out_shape=(jax.ShapeDtypeStruct((1, 1), F32), jax.ShapeDtypeStruct((S, D), F32),
                   jax.ShapeDtypeStruct((1, D), F32)),
        compiler_params=_params(),
    )(h, g, tgt)


def _swap_halves(g):
    lane = lax.broadcasted_iota(jnp.int32, g.shape, 1)
    return jnp.where(lane < QK_ROPE // 2, pltpu.roll(g, LANE - QK_ROPE // 2, axis=1),
                     pltpu.roll(g, QK_ROPE // 2, axis=1))


def _rope128(g, c128, s128):
    return g * c128 + _swap_halves(g) * s128


def _rope128_vjp(dy, c128, s128):
    lane = lax.broadcasted_iota(jnp.int32, dy.shape, 1)
    return jnp.where(lane < QK_ROPE, dy * c128 + _swap_halves(dy * s128), 0.0)


A_PAD = 768
KR0 = Q_LORA + KV_LORA


def _mla_mid_fwd(a, g_q, g_kv, c128, s128, *, name):
    S = a.shape[0]
    tm = _rows(S, 512)

    def body(a_ref, gq_ref, gkv_ref, c_ref, s_ref, cq_ref, ckv_ref, kr_ref):
        av = a_ref[...]
        _, qh = _rms_parts(av[:, :Q_LORA])
        cq_ref[...] = (qh * gq_ref[...]).astype(cq_ref.dtype)
        _, kh = _rms_parts(av[:, Q_LORA:KR0])
        ckv_ref[...] = (kh * gkv_ref[...]).astype(ckv_ref.dtype)
        kr = _rope128(av[:, KR0:A_PAD], c_ref[...], s_ref[...])
        kr_ref[...] = kr[:, :QK_ROPE].astype(kr_ref.dtype)

    return pl.pallas_call(
        body, name=name, grid=(S // tm,),
        in_specs=[_row_spec(tm, A_PAD), _const_spec(1, Q_LORA), _const_spec(1, KV_LORA), _row_spec(tm, LANE),
                  _row_spec(tm, LANE)],
        out_specs=(_row_spec(tm, Q_LORA), _row_spec(tm, KV_LORA), _row_spec(tm, QK_ROPE)),
        out_shape=(jax.ShapeDtypeStruct((S, Q_LORA), MXU), jax.ShapeDtypeStruct((S, KV_LORA), MXU),
                   jax.ShapeDtypeStruct((S, QK_ROPE), MXU)),
        compiler_params=_params(),
    )(a, g_q, g_kv, c128, s128)


def _mla_mid_bwd(a, dcq, dckv, dkr, g_q, g_kv, c128, s128, *, name):
    S = a.shape[0]
    tm = _rows(S, 512)

    def body(a_ref, dcq_ref, dckv_ref, dkr_ref, gq_ref, gkv_ref, c_ref, s_ref, da_ref, dgq_ref, dgkv_ref):
        @pl.when(pl.program_id(0) == 0)
        def _():
            dgq_ref[...] = jnp.zeros_like(dgq_ref)
            dgkv_ref[...] = jnp.zeros_like(dgkv_ref)

        av = a_ref[...]
        dx, dgp = _rms_vjp(av[:, :Q_LORA], gq_ref[...], dcq_ref[...])
        da_ref[:, :Q_LORA] = dx.astype(da_ref.dtype)
        dgq_ref[...] += jnp.sum(dgp, axis=0, keepdims=True)
        dx, dgp = _rms_vjp(av[:, Q_LORA:KR0], gkv_ref[...], dckv_ref[...])
        da_ref[:, Q_LORA:KR0] = dx.astype(da_ref.dtype)
        dgkv_ref[...] += jnp.sum(dgp, axis=0, keepdims=True)
        da_ref[:, KR0:A_PAD] = _rope128_vjp(dkr_ref[...], c_ref[...], s_ref[...]).astype(da_ref.dtype)

    return pl.pallas_call(
        body, name=name, grid=(S // tm,),
        in_specs=[_row_spec(tm, A_PAD), _row_spec(tm, Q_LORA), _row_spec(tm, KV_LORA), _row_spec(tm, LANE),
                  _const_spec(1, Q_LORA), _const_spec(1, KV_LORA), _row_spec(tm, LANE), _row_spec(tm, LANE)],
        out_specs=(_row_spec(tm, A_PAD), _const_spec(1, Q_LORA), _const_spec(1, KV_LORA)),
        out_shape=(jax.ShapeDtypeStruct((S, A_PAD), MXU), jax.ShapeDtypeStruct((1, Q_LORA), F32),
                   jax.ShapeDtypeStruct((1, KV_LORA), F32)),
        compiler_params=_params(),
    )(a, dcq, dckv, dkr, g_q, g_kv, c128, s128)


QF = 2 * HEADS * LANE
KVF = HEADS * (QK_NOPE + V_HEAD)
VX = 2 * V_HEAD


def _qk_prep(qfull, kv, kr, c128, s128, *, name):
    S = qfull.shape[0]
    tm = _rows(S, 256)

    def body(q_ref, kv_ref, kr_ref, c_ref, s_ref, *outs):
        qo, ko, vo = outs[:HEADS], outs[HEADS:2 * HEADS], outs[2 * HEADS:]
        cv, sv = c_ref[...], s_ref[...]
        krv = kr_ref[...]
        for h in range(HEADS):
            qo[h][:, :QK_NOPE] = q_ref[:, h * LANE:(h + 1) * LANE].astype(MXU)
            g = q_ref[:, (HEADS + h) * LANE:(HEADS + h + 1) * LANE]
            qo[h][:, QK_NOPE:] = _rope128(g, cv, sv)[:, :QK_ROPE].astype(MXU)
            ko[h][:, :QK_NOPE] = kv_ref[:, 2 * h * LANE:(2 * h + 1) * LANE]
            ko[h][:, QK_NOPE:] = krv
            vo[h][:, :V_HEAD] = kv_ref[:, (2 * h + 1) * LANE:(2 * h + 2) * LANE]
            vo[h][:, V_HEAD:] = jnp.ones((tm, VX - V_HEAD), MXU)

    shapes = ([jax.ShapeDtypeStruct((S, QK_DIM), MXU)] * (2 * HEADS)
              + [jax.ShapeDtypeStruct((S, VX), MXU)] * HEADS)
    specs = [_row_spec(tm, QK_DIM)] * (2 * HEADS) + [_row_spec(tm, VX)] * HEADS
    outs = pl.pallas_call(
        body, name=name, grid=(S // tm,),
        in_specs=[_row_spec(tm, QF), _row_spec(tm, KVF), _row_spec(tm, QK_ROPE), _row_spec(tm, LANE),
                  _row_spec(tm, LANE)],
        out_specs=tuple(specs), out_shape=tuple(shapes), compiler_params=_params(),
    )(qfull, kv, kr, c128, s128)
    return outs[:HEADS], outs[HEADS:2 * HEADS], outs[2 * HEADS:]


def _qk_prep_bwd(dqs, dks, dvs, c128, s128, *, name):
    S = dqs[0].shape[0]
    tm = _rows(S, 256)

    def body(*refs):
        dq = refs[:HEADS]
        dk = refs[HEADS:2 * HEADS]
        dv = refs[2 * HEADS:3 * HEADS]
        c_ref, s_ref, dqf_ref, dkv_ref, dkr_ref, tmp = refs[3 * HEADS:]
        cv, sv = c_ref[...], s_ref[...]
        tmp[...] = jnp.zeros_like(tmp)
        dkr_ref[...] = jnp.zeros_like(dkr_ref)
        for h in range(HEADS):
            dqf_ref[:, h * LANE:(h + 1) * LANE] = dq[h][:, :QK_NOPE].astype(MXU)
            tmp[:, :QK_ROPE] = dq[h][:, QK_NOPE:]
            dqf_ref[:, (HEADS + h) * LANE:(HEADS + h + 1) * LANE] = _rope128_vjp(tmp[...], cv, sv).astype(MXU)
            dkv_ref[:, 2 * h * LANE:(2 * h + 1) * LANE] = dk[h][:, :QK_NOPE].astype(MXU)
            dkv_ref[:, (2 * h + 1) * LANE:(2 * h + 2) * LANE] = dv[h][...].astype(MXU)
            dkr_ref[:, :QK_ROPE] += dk[h][:, QK_NOPE:]

    return pl.pallas_call(
        body, name=name, grid=(S // tm,),
        in_specs=[_row_spec(tm, QK_DIM)] * (2 * HEADS) + [_row_spec(tm, V_HEAD)] * HEADS
        + [_row_spec(tm, LANE), _row_spec(tm, LANE)],
        out_specs=(_row_spec(tm, QF), _row_spec(tm, KVF), _row_spec(tm, LANE)),
        out_shape=(jax.ShapeDtypeStruct((S, QF), MXU), jax.ShapeDtypeStruct((S, KVF), MXU),
                   jax.ShapeDtypeStruct((S, LANE), F32)),
        scratch_shapes=[pltpu.VMEM((tm, LANE), F32)], compiler_params=_params(),
    )(*dqs, *dks, *dvs, c128, s128)


def _dot_nt(a, b):
    return lax.dot_general(a, b, (((1,), (1,)), ((), ())), preferred_element_type=F32)


def _dot_tn(a, b):
    return lax.dot_general(a, b, (((0,), (0,)), ((), ())), preferred_element_type=F32)


def _chunk_mask(t, q_axis):
    qc = lax.broadcasted_iota(jnp.int32, (t, t), q_axis) // CHUNK
    kc = lax.broadcasted_iota(jnp.int32, (t, t), 1 - q_axis) // CHUNK
    return kc <= qc


def _attn_fwd(q, k, vx, *, name):
    S = q.shape[0]
    T = _rows(S, 1024)
    n = S // T
    cpt = T // CHUNK

    def body(q_ref, k_ref, v_ref, o_ref, lse_ref, s_buf, p_buf, a_buf, m_s, acc_s):
        i = pl.program_id(0)
        qc = lax.broadcasted_iota(jnp.int32, (T, T), 0) // CHUNK
        kc = lax.broadcasted_iota(jnp.int32, (T, T), 1) // CHUNK
        dchunk = kc - qc

        def tile_rows(b):
            return pl.ds(pl.multiple_of(jnp.clip(b, 0, n - 1) * T, T), T)

        def scores(b, slot):
            s = _dot_nt(q_ref[...], k_ref[tile_rows(b), :])
            s_buf[slot] = jnp.where(dchunk <= (i - b) * cpt, s, NEG)

        def softmax(slot):
            s = s_buf[slot]
            m_prev = m_s[...]
            m_new = jnp.maximum(m_prev, jnp.max(s, axis=1, keepdims=True))
            a_buf[slot] = jnp.exp2((m_prev - m_new) * EXP2_SCALE)
            p_buf[slot] = jnp.exp2((s - m_new) * EXP2_SCALE).astype(MXU)
            m_s[...] = m_new

        def pv(b, slot):
            acc_s[...] = a_buf[slot] * acc_s[...] + jnp.dot(p_buf[slot], v_ref[tile_rows(b), :],
                                                              preferred_element_type=F32)

        m_s[...] = jnp.full_like(m_s, NEG)
        acc_s[...] = jnp.zeros_like(acc_s)
        p_buf[1] = jnp.zeros((T, T), MXU)
        a_buf[1] = jnp.ones((T, 1), F32)
        scores(0, 0)

        def pair(u, carry):
            t = 2 * u
            scores(t + 1, 1)
            softmax(0)
            pv(t - 1, 1)
            scores(t + 2, 0)
            softmax(1)
            pv(t, 0)
            return carry

        npairs = (i + 2) // 2
        lax.fori_loop(0, npairs, pair, 0)
        pv(2 * npairs - 1, 1)
        acc = acc_s[...]
        l = acc[:, V_HEAD:V_HEAD + 1]
        o_ref[...] = (acc[:, :V_HEAD] / l).astype(o_ref.dtype)
        lse_ref[...] = m_s[...] * ATT_SCALE + jnp.log(l)

    return pl.pallas_call(
        body, name=name, grid=(n,),
        in_specs=[pl.BlockSpec((T, QK_DIM), lambda i: (i, 0)), pl.BlockSpec((S, QK_DIM), lambda i: (0, 0)),
                  pl.BlockSpec((S, VX), lambda i: (0, 0))],
        out_specs=(pl.BlockSpec((T, V_HEAD), lambda i: (i, 0)), pl.BlockSpec((T, 1), lambda i: (i, 0))),
        out_shape=(jax.ShapeDtypeStruct((S, V_HEAD), MXU), jax.ShapeDtypeStruct((S, 1), F32)),
        scratch_shapes=[pltpu.VMEM((2, T, T), F32), pltpu.VMEM((2, T, T), MXU), pltpu.VMEM((2, T, 1), F32),
                        pltpu.VMEM((T, 1), F32), pltpu.VMEM((T, VX), F32)],
        compiler_params=_params(big=True),
    )(q, k, vx)


def _attn_delta(do, o, *, name):
    S = do.shape[0]
    tm = _rows(S, 1024)

    def body(do_ref, o_ref, d_ref):
        d_ref[...] = jnp.sum(do_ref[...].astype(F32) * o_ref[...].astype(F32), axis=1, keepdims=True)

    return pl.pallas_call(
        body, name=name, grid=(S // tm,), in_specs=[_row_spec(tm, V_HEAD), _row_spec(tm, V_HEAD)],
        out_specs=_row_spec(tm, 1), out_shape=jax.ShapeDtypeStruct((S, 1), F32), compiler_params=_params(),
    )(do, o)


def _attn_bwd(q, k, v, do, lse_row, delta_row, *, name):
    S = q.shape[0]
    T = _rows(S, 512)
    n = S // T

    def body(q_ref, k_ref, v_ref, do_ref, lse_ref, dl_ref, dq_ref, dk_ref, dv_ref, dk_s, dv_s):
        j = pl.program_id(0)
        i = pl.program_id(1)

        @pl.when((j == 0) & (i == 0))
        def _():
            dq_ref[...] = jnp.zeros_like(dq_ref)

        def update(masked):
            qv, kv_, dov = q_ref[...], k_ref[...], do_ref[...]
            st = _dot_nt(kv_, qv) * ATT_SCALE
            if masked:
                st = jnp.where(_chunk_mask(T, 1), st, NEG)
            pt = jnp.exp(st - lse_ref[...])
            dv_s[...] += jnp.dot(pt.astype(MXU), dov, preferred_element_type=F32)
            dpt = _dot_nt(v_ref[...], dov)
            ds = (pt * (dpt - dl_ref[...]) * ATT_SCALE).astype(MXU)
            dk_s[...] += jnp.dot(ds, qv, preferred_element_type=F32)
            rows = pl.ds(pl.multiple_of(i * T, T), T)
            dq_ref[rows, :] += _dot_tn(ds, kv_)

        @pl.when(i == j)
        def _():
            dk_s[...] = jnp.zeros_like(dk_s)
            dv_s[...] = jnp.zeros_like(dv_s)
            update(True)

        @pl.when(i > j)
        def _():
            update(False)

        @pl.when(i == n - 1)
        def _():
            dk_ref[...] = dk_s[...]
            dv_ref[...] = dv_s[...]

    q_map = lambda j, i: (jnp.maximum(i, j), 0)
    r_map = lambda j, i: (0, jnp.maximum(i, j))
    k_map = lambda j, i: (j, 0)
    return pl.pallas_call(
        body, name=name, grid=(n, n),
        in_specs=[pl.BlockSpec((T, QK_DIM), q_map), pl.BlockSpec((T, QK_DIM), k_map),
                  pl.BlockSpec((T, V_HEAD), k_map), pl.BlockSpec((T, V_HEAD), q_map),
                  pl.BlockSpec((1, T), r_map), pl.BlockSpec((1, T), r_map)],
        out_specs=(pl.BlockSpec((S, QK_DIM), lambda j, i: (0, 0)), pl.BlockSpec((T, QK_DIM), k_map),
                   pl.BlockSpec((T, V_HEAD), k_map)),
        out_shape=(jax.ShapeDtypeStruct((S, QK_DIM), F32), jax.ShapeDtypeStruct((S, QK_DIM), F32),
                   jax.ShapeDtypeStruct((S, V_HEAD), F32)),
        scratch_shapes=[pltpu.VMEM((T, QK_DIM), F32), pltpu.VMEM((T, V_HEAD), F32)],
        compiler_params=_params(big=True),
    )(q, k, v, do, lse_row, delta_row)


HALO = 8


def _conv_tiles(S):
    tm = _rows(S, 256)
    tc = D_FF // 2
    return tm, tc, D_FF // tc


def _silu_parts(gate):
    sg = jax.nn.sigmoid(gate)
    return sg, gate * sg


def _convgate_fwd(up, cw, cb, *, name):
    S = up.shape[0]
    tm, tc, nc = _conv_tiles(S)
    hb = tm // HALO

    def body(v_ref, g_ref, hv_ref, hg_ref, wv_ref, wg_ref, bv_ref, bg_ref, o_ref, extv, extg):
        keep = (pl.program_id(0) > 0).astype(F32)

        def conv(ext, t_ref, h_ref, w_ref, b_ref):
            ext[0:HALO, :] = h_ref[...] * keep
            ext[HALO:HALO + tm, :] = t_ref[...]
            w = w_ref[...]
            return (w[0:1] * ext[HALO - 2:HALO - 2 + tm, :] + w[1:2] * ext[HALO - 1:HALO - 1 + tm, :]
                    + w[2:3] * ext[HALO:HALO + tm, :] + b_ref[...])

        val = conv(extv, v_ref, hv_ref, wv_ref, bv_ref)
        gate = conv(extg, g_ref, hg_ref, wg_ref, bg_ref)
        _, silu = _silu_parts(gate)
        o_ref[...] = (silu * val).astype(o_ref.dtype)

    prev = lambda i: jnp.maximum(i * hb - 1, 0)
    return pl.pallas_call(
        body, name=name, grid=(S // tm, nc),
        in_specs=[pl.BlockSpec((tm, tc), lambda i, j: (i, j)), pl.BlockSpec((tm, tc), lambda i, j: (i, j + nc)),
                  pl.BlockSpec((HALO, tc), lambda i, j: (prev(i), j)),
                  pl.BlockSpec((HALO, tc), lambda i, j: (prev(i), j + nc)),
                  pl.BlockSpec((3, tc), lambda i, j: (0, j)), pl.BlockSpec((3, tc), lambda i, j: (0, j + nc)),
                  pl.BlockSpec((1, tc), lambda i, j: (0, j)), pl.BlockSpec((1, tc), lambda i, j: (0, j + nc))],
        out_specs=pl.BlockSpec((tm, tc), lambda i, j: (i, j)),
        out_shape=jax.ShapeDtypeStruct((S, D_FF), MXU),
        scratch_shapes=[pltpu.VMEM((tm + HALO, tc), F32), pltpu.VMEM((tm + HALO, tc), F32)],
        compiler_params=_params(),
    )(up, up, up, up, cw, cw, cb, cb)


def _convgate_bwd(up, dact, cw, cb, *, name):
    S = up.shape[0]
    tm, tc, nc = _conv_tiles(S)
    hb = tm // HALO
    nr = S // tm
    R = tm + HALO

    def body(v_ref, g_ref, pv_ref, pg_ref, nv_ref, ng_ref, da_ref, dan_ref, wv_ref, wg_ref, bv_ref, bg_ref,
             duv_ref, dug_ref, dwv_ref, dwg_ref, dbv_ref, dbg_ref, extv, extg, dsv, dsg):
        i = pl.program_id(1)

        @pl.when(i == 0)
        def _():
            for r in (dwv_ref, dwg_ref, dbv_ref, dbg_ref):
                r[...] = jnp.zeros_like(r)

        keep_prev = (i > 0).astype(F32)
        keep_next = (i < nr - 1).astype(F32)

        def conv(ext, t_ref, p_ref, n_ref, w_ref, b_ref):
            ext[0:HALO, :] = p_ref[...] * keep_prev
            ext[HALO:HALO + tm, :] = t_ref[...]
            ext[HALO + tm:2 * HALO + tm, :] = n_ref[...]
            w = w_ref[...]
            return (w[0:1] * ext[HALO - 2:HALO - 2 + R, :] + w[1:2] * ext[HALO - 1:HALO - 1 + R, :]
                    + w[2:3] * ext[HALO:HALO + R, :] + b_ref[...])

        val = conv(extv, v_ref, pv_ref, nv_ref, wv_ref, bv_ref)
        gate = conv(extg, g_ref, pg_ref, ng_ref, wg_ref, bg_ref)
        dact_e = jnp.concatenate([da_ref[...], dan_ref[...] * keep_next], axis=0)
        sg, silu = _silu_parts(gate)
        dsv[...] = dact_e * silu
        dsg[...] = dact_e * val * (sg * (1.0 + gate * (1.0 - sg)))

        def back(ds, ext, w_ref, du_ref, dw_ref, db_ref):
            w = w_ref[...]
            du_ref[...] = (w[2:3] * ds[0:tm, :] + w[1:2] * ds[1:1 + tm, :] + w[0:1] * ds[2:2 + tm, :]).astype(du_ref.dtype)
            d0 = ds[0:tm, :]
            for kk in range(3):
                dw_ref[kk:kk + 1, :] += jnp.sum(d0 * ext[HALO - 2 + kk:HALO - 2 + kk + tm, :], axis=0, keepdims=True)
            db_ref[...] += jnp.sum(d0, axis=0, keepdims=True)

        back(dsv, extv, wv_ref, duv_ref, dwv_ref, dbv_ref)
        back(dsg, extg, wg_ref, dug_ref, dwg_ref, dbg_ref)

    prev = lambda i: jnp.maximum(i * hb - 1, 0)
    nxt = lambda i: jnp.minimum((i + 1) * hb, S // HALO - 1)
    tile_v = pl.BlockSpec((tm, tc), lambda j, i: (i, j))
    tile_g = pl.BlockSpec((tm, tc), lambda j, i: (i, j + nc))
    w_v = pl.BlockSpec((3, tc), lambda j, i: (0, j))
    w_g = pl.BlockSpec((3, tc), lambda j, i: (0, j + nc))
    b_v = pl.BlockSpec((1, tc), lambda j, i: (0, j))
    b_g = pl.BlockSpec((1, tc), lambda j, i: (0, j + nc))
    return pl.pallas_call(
        body, name=name, grid=(nc, nr),
        in_specs=[tile_v, tile_g,
                  pl.BlockSpec((HALO, tc), lambda j, i: (prev(i), j)), pl.BlockSpec((HALO, tc), lambda j, i: (prev(i), j + nc)),
                  pl.BlockSpec((HALO, tc), lambda j, i: (nxt(i), j)), pl.BlockSpec((HALO, tc), lambda j, i: (nxt(i), j + nc)),
                  tile_v, pl.BlockSpec((HALO, tc), lambda j, i: (nxt(i), j)), w_v, w_g, b_v, b_g],
        out_specs=(tile_v, tile_v, w_v, w_v, b_v, b_v),
        out_shape=(jax.ShapeDtypeStruct((S, D_FF), MXU), jax.ShapeDtypeStruct((S, D_FF), MXU),
                   jax.ShapeDtypeStruct((3, D_FF), F32), jax.ShapeDtypeStruct((3, D_FF), F32),
                   jax.ShapeDtypeStruct((1, D_FF), F32), jax.ShapeDtypeStruct((1, D_FF), F32)),
        scratch_shapes=[pltpu.VMEM((tm + 2 * HALO, tc), F32), pltpu.VMEM((tm + 2 * HALO, tc), F32),
                        pltpu.VMEM((R, tc), F32), pltpu.VMEM((R, tc), F32)],
        compiler_params=_params(),
    )(up, up, up, up, up, up, dact, dact, cw, cw, cb, cb)


def _glu_fwd(z, h, *, name):
    S = z.shape[0]
    tm = _rows(S, 512)

    def body(z_ref, h_ref, o_ref):
        o_ref[...] = h_ref[...] + z_ref[:, :D_MODEL] * jax.nn.sigmoid(z_ref[:, D_MODEL:])

    return pl.pallas_call(
        body, name=name, grid=(S // tm,), in_specs=[_row_spec(tm, 2 * D_MODEL), _row_spec(tm, D_MODEL)],
        out_specs=_row_spec(tm, D_MODEL), out_shape=jax.ShapeDtypeStruct((S, D_MODEL), F32),
        compiler_params=_params(),
    )(z, h)


def _glu_bwd(z, dm, *, name):
    S = z.shape[0]
    tm = _rows(S, 512)

    def body(z_ref, dm_ref, o_ref):
        sg = jax.nn.sigmoid(z_ref[:, D_MODEL:])
        dmv = dm_ref[...]
        o_ref[:, :D_MODEL] = (dmv * sg).astype(o_ref.dtype)
        o_ref[:, D_MODEL:] = (dmv * z_ref[:, :D_MODEL] * sg * (1.0 - sg)).astype(o_ref.dtype)

    return pl.pallas_call(
        body, name=name, grid=(S // tm,), in_specs=[_row_spec(tm, 2 * D_MODEL), _row_spec(tm, D_MODEL)],
        out_specs=_row_spec(tm, 2 * D_MODEL), out_shape=jax.ShapeDtypeStruct((S, 2 * D_MODEL), MXU),
        compiler_params=_params(),
    )(z, dm)


GELU_C = math.sqrt(2.0 / math.pi)
GELU_A = 0.044715


def _gelu(y):
    return 0.5 * y * (1.0 + jnp.tanh(GELU_C * (y + GELU_A * (y * y * y))))


def _gelu_bwd(y, dg, *, name):
    S = y.shape[0]
    tm = _rows(S, 512)

    def body(y_ref, dg_ref, o_ref):
        yv = y_ref[...]
        t = jnp.tanh(GELU_C * (yv + GELU_A * (yv * yv * yv)))
        d = 0.5 * (1.0 + t) + 0.5 * yv * (1.0 - t * t) * (GELU_C * (1.0 + 3.0 * GELU_A * (yv * yv)))
        o_ref[...] = dg_ref[...] * d

    return pl.pallas_call(
        body, name=name, grid=(S // tm,), in_specs=[_row_spec(tm, D_MODEL), _row_spec(tm, D_MODEL)],
        out_specs=_row_spec(tm, D_MODEL), out_shape=jax.ShapeDtypeStruct((S, D_MODEL), F32),
        compiler_params=_params(),
    )(y, dg)


SW = NSTATE // SBLK
ST2 = 2 * NSTATE


def _s5_fwd(u, wb, wc, abc, dskip, x0, *, full, name):
    S = u.shape[0]
    T = _rows(S, 256)
    nb = S // T
    nj = T // NSEG

    def body(u_ref, wb_ref, wc_ref, a_ref, d_ref, x0_ref, *rest):
        if full:
            xs_ref, y_ref, yg_ref, st = rest
        else:
            e_ref, xs_ref, st = rest
        i = pl.program_id(0)

        @pl.when(i == 0)
        def _():
            st[...] = x0_ref[...]

        uv = u_ref[...]
        ub = uv.astype(MXU)
        for kb in range(SBLK):
            r = jnp.dot(ub[:, kb * LANE:(kb + 1) * LANE], wb_ref[kb], preferred_element_type=F32)
            xs_ref[:, kb * SW:(kb + 1) * SW] = r[:, :SW]
            xs_ref[:, NSTATE + kb * SW:NSTATE + (kb + 1) * SW] = r[:, SW:]
        ar = a_ref[:, :NSTATE]
        ai = a_ref[:, NSTATE:]

        def step(j, c):
            xr, xi = c
            rows = pl.ds(pl.multiple_of(j * NSEG, NSEG), NSEG)
            br = xs_ref[rows, pl.ds(0, NSTATE)]
            bi = xs_ref[rows, pl.ds(NSTATE, NSTATE)]
            nr = ar * xr - ai * xi + br
            ni = ar * xi + ai * xr + bi
            xs_ref[rows, pl.ds(0, NSTATE)] = nr
            xs_ref[rows, pl.ds(NSTATE, NSTATE)] = ni
            return nr, ni

        xr, xi = lax.fori_loop(0, nj, step, (st[:, :NSTATE], st[:, NSTATE:]))
        st[:, :NSTATE] = xr
        st[:, NSTATE:] = xi
        if full:
            for kb in range(SBLK):
                yk = (jnp.dot(xs_ref[:, kb * SW:(kb + 1) * SW].astype(MXU), wc_ref[kb, :SW, :], preferred_element_type=F32)
                      + jnp.dot(xs_ref[:, NSTATE + kb * SW:NSTATE + (kb + 1) * SW].astype(MXU), wc_ref[kb, SW:, :],
                                preferred_element_type=F32))
                cols = slice(kb * LANE, (kb + 1) * LANE)
                yk = yk + d_ref[:, cols] * uv[:, cols]
                y_ref[:, cols] = yk
                yg_ref[:, cols] = _gelu(yk).astype(yg_ref.dtype)
        else:
            @pl.when(i == nb - 1)
            def _():
                e_ref[...] = st[...]

    in_specs = [_row_spec(T, D_MODEL), pl.BlockSpec((SBLK, LANE, 2 * SW), lambda i: (0, 0, 0)),
                pl.BlockSpec((SBLK, 2 * SW, LANE), lambda i: (0, 0, 0)), _const_spec(NSEG, ST2),
                _const_spec(1, D_MODEL), _const_spec(NSEG, ST2)]
    if full:
        out_specs = (_row_spec(T, ST2), _row_spec(T, D_MODEL), _row_spec(T, D_MODEL))
        out_shape = (jax.ShapeDtypeStruct((S, ST2), F32), jax.ShapeDtypeStruct((S, D_MODEL), F32),
                     jax.ShapeDtypeStruct((S, D_MODEL), MXU))
        scratch = [pltpu.VMEM((NSEG, ST2), F32)]
    else:
        out_specs = _const_spec(NSEG, ST2)
        out_shape = jax.ShapeDtypeStruct((NSEG, ST2), F32)
        scratch = [pltpu.VMEM((T, ST2), F32), pltpu.VMEM((NSEG, ST2), F32)]
    return pl.pallas_call(
        body, name=name, grid=(nb,), in_specs=in_specs, out_specs=out_specs, out_shape=out_shape,
        scratch_shapes=scratch, compiler_params=_params(big=True),
    )(u, wb, wc, abc, dskip, x0)


def _s5_bwd(dy, xs, u, wct, wbt, abc, dskip, x0, l0, *, full, name):
    S = dy.shape[0]
    T = _rows(S, 128)
    nb = S // T
    nj = T // NSEG
    blk = lambda i: nb - 1 - i

    def body(dy_ref, *rest):
        if full:
            (xs_ref, xh_ref, u_ref, wct_ref, wbt_ref, a_ref, d_ref, x0_ref, l0_ref,
             du_ref, da_ref, dwb_ref, dwc_ref, dd_ref, g_s, lam_s) = rest
        else:
            wct_ref, a_ref, l0_ref, f_ref, g_s, lam_s = rest
        i = pl.program_id(0)

        @pl.when(i == 0)
        def _():
            lam_s[...] = l0_ref[...]
            if full:
                for r in (da_ref, dwb_ref, dwc_ref, dd_ref):
                    r[...] = jnp.zeros_like(r)

        dyv = dy_ref[...]
        dyb = dyv.astype(MXU)
        for kb in range(SBLK):
            r = jnp.dot(dyb[:, kb * LANE:(kb + 1) * LANE], wct_ref[kb], preferred_element_type=F32)
            g_s[:, kb * SW:(kb + 1) * SW] = r[:, :SW]
            g_s[:, NSTATE + kb * SW:NSTATE + (kb + 1) * SW] = r[:, SW:]
        ar = a_ref[:, :NSTATE]
        ai = a_ref[:, NSTATE:]

        def advance(row, lam):
            lr, li = lam
            rows = pl.ds(row, NSEG)
            nr = g_s[rows, pl.ds(0, NSTATE)] + ar * lr + ai * li
            ni = g_s[rows, pl.ds(NSTATE, NSTATE)] - ai * lr + ar * li
            g_s[rows, pl.ds(0, NSTATE)] = nr
            g_s[rows, pl.ds(NSTATE, NSTATE)] = ni
            return nr, ni

        def accumulate(lam, xpr, xpi):
            nr, ni = lam
            da_ref[:, :NSTATE] += nr * xpr + ni * xpi
            da_ref[:, NSTATE:] += ni * xpr - nr * xpi

        def step(jj, lam):
            row = pl.multiple_of((nj - 1 - jj) * NSEG, NSEG)
            lam = advance(row, lam)
            if full:
                prow = pl.ds(pl.multiple_of(row - NSEG, NSEG), NSEG)
                accumulate(lam, xs_ref[prow, pl.ds(0, NSTATE)], xs_ref[prow, pl.ds(NSTATE, NSTATE)])
            return lam

        lam = lax.fori_loop(0, nj - 1, step, (lam_s[:, :NSTATE], lam_s[:, NSTATE:]))
        lam = advance(0, lam)
        if full:
            first = (blk(i) == 0)
            xp = jnp.where(first, x0_ref[...], xh_ref[...])
            accumulate(lam, xp[:, :NSTATE], xp[:, NSTATE:])
        lam_s[:, :NSTATE] = lam[0]
        lam_s[:, NSTATE:] = lam[1]
        if full:
            uv = u_ref[...]
            ub = uv.astype(MXU)
            dd_ref[...] += jnp.sum(dyv * uv, axis=0, keepdims=True)
            for kb in range(SBLK):
                cols = slice(kb * LANE, (kb + 1) * LANE)
                re = slice(kb * SW, (kb + 1) * SW)
                im = slice(NSTATE + kb * SW, NSTATE + (kb + 1) * SW)
                lr_b = g_s[:, re].astype(MXU)
                li_b = g_s[:, im].astype(MXU)
                duk = (jnp.dot(lr_b, wbt_ref[kb, :SW, :], preferred_element_type=F32)
                       + jnp.dot(li_b, wbt_ref[kb, SW:, :], preferred_element_type=F32))
                du_ref[:, cols] = duk + d_ref[:, cols] * dyv[:, cols]
                dwb_ref[kb, :, :SW] += _dot_tn(ub[:, cols], lr_b)
                dwb_ref[kb, :, SW:] += _dot_tn(ub[:, cols], li_b)
                dwc_ref[kb, :SW, :] += _dot_tn(xs_ref[:, re].astype(MXU), dyb[:, cols])
                dwc_ref[kb, SW:, :] += _dot_tn(xs_ref[:, im].astype(MXU), dyb[:, cols])
        else:
            @pl.when(i == nb - 1)
            def _():
                f_ref[...] = lam_s[...]

    rev = lambda c: pl.BlockSpec((T, c), lambda i: (blk(i), 0))
    w3 = lambda a, b: pl.BlockSpec((SBLK, a, b), lambda i: (0, 0, 0))
    if full:
        hb = T // NSEG
        in_specs = [rev(D_MODEL), rev(ST2),
                    pl.BlockSpec((NSEG, ST2), lambda i: (jnp.maximum(blk(i) * hb - 1, 0), 0)),
                    rev(D_MODEL), w3(LANE, 2 * SW), w3(2 * SW, LANE), _const_spec(NSEG, ST2),
                    _const_spec(1, D_MODEL), _const_spec(NSEG, ST2), _const_spec(NSEG, ST2)]
        ops = [dy, xs, xs, u, wct, wbt, abc, dskip, x0, l0]
        out_specs = (rev(D_MODEL), _const_spec(NSEG, ST2), w3(LANE, 2 * SW), w3(2 * SW, LANE),
                     _const_spec(1, D_MODEL))
        out_shape = (jax.ShapeDtypeStruct((S, D_MODEL), F32), jax.ShapeDtypeStruct((NSEG, ST2), F32),
                     jax.ShapeDtypeStruct((SBLK, LANE, 2 * SW), F32), jax.ShapeDtypeStruct((SBLK, 2 * SW, LANE), F32),
                     jax.ShapeDtypeStruct((1, D_MODEL), F32))
    else:
        in_specs = [rev(D_MODEL), w3(LANE, 2 * SW), _const_spec(NSEG, ST2), _const_spec(NSEG, ST2)]
        ops = [dy, wct, abc, l0]
        out_specs = _const_spec(NSEG, ST2)
        out_shape = jax.ShapeDtypeStruct((NSEG, ST2), F32)
    return pl.pallas_call(
        body, name=name, grid=(nb,), in_specs=in_specs, out_specs=out_specs, out_shape=out_shape,
        scratch_shapes=[pltpu.VMEM((T, ST2), F32), pltpu.VMEM((NSEG, ST2), F32)],
        compiler_params=_params(big=True),
    )(*ops)


def _s5_discretize(lr, li, log_dt, br, bi):
    dt = jnp.exp(log_dt)[:, None]
    mag = jnp.exp(lr * dt)
    ar = mag * jnp.cos(li * dt)
    ai = mag * jnp.sin(li * dt)
    den = lr * lr + li * li
    nr = ar - 1.0
    coef_r = (nr * lr + ai * li) / den
    coef_i = (ai * lr - nr * li) / den
    bbar_r = coef_r[..., None] * br - coef_i[..., None] * bi
    bbar_i = coef_r[..., None] * bi + coef_i[..., None] * br
    return ar, ai, bbar_r, bbar_i


def _blockdiag(m):
    gpb = SSM_GROUPS // SBLK
    a, b = m.shape[1:]
    mb = m.reshape(SBLK, gpb, a, b)
    eye = jnp.eye(gpb, dtype=m.dtype)
    return jnp.einsum('kgab,gh->kgahb', mb, eye).reshape(SBLK, gpb * a, gpb * b)


def _blockdiag_extract(w, a, b):
    gpb = SSM_GROUPS // SBLK
    w5 = w.reshape(SBLK, gpb, a, gpb, b)
    return jnp.einsum('kgahb,gh->kgab', w5, jnp.eye(gpb, dtype=w.dtype)).reshape(SSM_GROUPS, a, b)


def _cpow(ar, ai, n):
    rr, ri = jnp.ones_like(ar), jnp.zeros_like(ai)
    br, bi = ar, ai
    while n:
        if n & 1:
            rr, ri = rr * br - ri * bi, rr * bi + ri * br
        br, bi = br * br - bi * bi, 2.0 * br * bi
        n >>= 1
    return rr, ri


def _perm(a):
    s, c = a.shape
    return a.reshape(NSEG, s // NSEG, c).transpose(1, 0, 2).reshape(s, c)


def _unperm(a):
    s, c = a.shape
    return a.reshape(s // NSEG, NSEG, c).transpose(1, 0, 2).reshape(s, c)


def _other_chips(x, y):
    return [(1 - x, y), (x, 1 - y), (1 - x, 1 - y)]


def _span(chip, size, align):
    return pl.ds(pl.multiple_of(chip * size, align), size)


def _chip_exchange(srcs, out_shapes, pieces, *, name):
    ns, no, npc = len(srcs), len(out_shapes), len(pieces)

    def body(*refs):
        s_refs, o_refs = refs[:ns], refs[ns:ns + no]
        send_sems, recv_sems, local_sems = refs[ns + no:]
        x, y, c = lax.axis_index("x"), lax.axis_index("y"), lax.axis_index("c")
        me = 2 * x + y

        def remote(k, p, tx, ty, src_chip, dst_chip):
            si, oi, sv, dv = pieces[p]
            return pltpu.make_async_remote_copy(
                src_ref=sv(s_refs[si], src_chip), dst_ref=dv(o_refs[oi], dst_chip), send_sem=send_sems.at[k, p],
                recv_sem=recv_sems.at[k, p], device_id=(tx, ty, c), device_id_type=MESH)

        mine = [pltpu.make_async_copy(sv(s_refs[si], me), dv(o_refs[oi], me), local_sems.at[p])
                for p, (si, oi, sv, dv) in enumerate(pieces)]
        for cp in mine:
            cp.start()
        others = _other_chips(x, y)
        sends = [remote(k, p, tx, ty, 2 * tx + ty, me) for k, (tx, ty) in enumerate(others) for p in range(npc)]
        for cp in sends:
            cp.start()
        for k, (tx, ty) in enumerate(others):
            for p in range(npc):
                remote(k, p, tx, ty, me, 2 * tx + ty).wait_recv()
        for cp in sends:
            cp.wait_send()
        for cp in mine:
            cp.wait()

    hbm = pl.BlockSpec(memory_space=pl.ANY)
    return pl.pallas_call(
        body, name=name, in_specs=[hbm] * ns, out_specs=tuple([hbm] * no), out_shape=tuple(out_shapes),
        scratch_shapes=[pltpu.SemaphoreType.DMA((NCHIP - 1, npc)), pltpu.SemaphoreType.DMA((NCHIP - 1, npc)),
                        pltpu.SemaphoreType.DMA((npc,))],
    )(*srcs)


def _sibling_exchange(srcs, *, name):
    n = len(srcs)

    def body(*refs):
        s_refs, o_refs, send_sems, recv_sems = refs[:n], refs[n:2 * n], refs[2 * n], refs[2 * n + 1]
        x, y, c = lax.axis_index("x"), lax.axis_index("y"), lax.axis_index("c")
        cps = [pltpu.make_async_remote_copy(src_ref=s_refs[p], dst_ref=o_refs[p], send_sem=send_sems.at[p],
                                            recv_sem=recv_sems.at[p], device_id=(x, y, 1 - c), device_id_type=MESH)
               for p in range(n)]
        for cp in cps:
            cp.start()
        for cp in cps:
            cp.wait()

    hbm = pl.BlockSpec(memory_space=pl.ANY)
    return pl.pallas_call(
        body, name=name, in_specs=[hbm] * n, out_specs=tuple([hbm] * n),
        out_shape=tuple(jax.ShapeDtypeStruct(s.shape, s.dtype) for s in srcs),
        scratch_shapes=[pltpu.SemaphoreType.DMA((n,)), pltpu.SemaphoreType.DMA((n,))],
    )(*srcs)


def _row_tile(r, c, tile_bytes):
    best = 8
    for t in range(8, r + 1, 8):
        if r % t == 0 and t * c * 4 <= tile_bytes:
            best = t
    assert r % best == 0
    return best


def _sum_chips(r, *, name):
    _, R, W = r.shape
    tm = _row_tile(R, W, 2 * 1024 * 1024)

    def body(r_ref, o_ref):
        o_ref[...] = ((r_ref[0] + r_ref[1]) + r_ref[2]) + r_ref[3]

    return pl.pallas_call(
        body, name=name, grid=(R // tm,), in_specs=[pl.BlockSpec((NCHIP, tm, W), lambda i: (0, i, 0))],
        out_specs=_row_spec(tm, W), out_shape=jax.ShapeDtypeStruct((R, W), F32), compiler_params=_params(),
    )(r)


def _adamw(p_mine, p_sib, w, m, v, *, name):
    R, W = w.shape
    tm = _row_tile(R, W, 1024 * 1024)

    def body(a_ref, b_ref, w_ref, m_ref, v_ref, g_ref, d_ref, nm_ref, nv_ref):
        g = a_ref[...] + b_ref[...]
        mm = ADAM_B1 * m_ref[...] + (1.0 - ADAM_B1) * g
        vv = ADAM_B2 * v_ref[...] + (1.0 - ADAM_B2) * (g * g)
        m_hat = mm / (1.0 - ADAM_B1 ** ADAM_STEP)
        v_hat = vv / (1.0 - ADAM_B2 ** ADAM_STEP)
        g_ref[...] = g
        d_ref[...] = -ADAM_LR * (m_hat / (jnp.sqrt(v_hat) + ADAM_EPS) + ADAM_WD * w_ref[...])
        nm_ref[...] = mm
        nv_ref[...] = vv

    spec = _row_spec(tm, W)
    shp = jax.ShapeDtypeStruct((R, W), F32)
    return pl.pallas_call(
        body, name=name, grid=(R // tm,), in_specs=[spec] * 5, out_specs=(spec,) * 4, out_shape=(shp,) * 4,
        compiler_params=_params(),
    )(p_mine, p_sib, w, m, v)


def _pack_small(parts):
    flat = jnp.concatenate([p.reshape(-1) for p in parts])
    pad = (-flat.shape[0]) % (SMALL_ROWS * PACKW)
    return jnp.pad(flat, (0, pad)).reshape(-1, PACKW)


def _unpack_small(buf, shapes):
    flat, out, off = buf.reshape(-1), [], 0
    for shp in shapes:
        sz = math.prod(shp)
        out.append(flat[off:off + sz].reshape(shp))
        off += sz
    return out


def _shard(a, t, ax):
    sz = a.shape[ax] // NCHIP
    return lax.slice_in_dim(a, t * sz, (t + 1) * sz, axis=ax)


def _gather_weights(w):
    srcs, outs, pieces = [], [], []

    def add(name):
        local = w[name].astype(MXU)
        local = local[0] if local.shape[0] == 1 else local
        ax = SHARD_AXIS[name] - (1 if w[name].shape[0] == 1 else 0)
        full = local.shape[:ax] + (NCHIP * local.shape[ax],) + local.shape[ax + 1:]
        si, oi = len(srcs), len(outs)
        srcs.append(local)
        outs.append(jax.ShapeDtypeStruct(full, MXU))
        size = local.shape[ax]
        if local.ndim == 2:
            if ax == 0:
                pieces.append((si, oi, lambda r, t: r, lambda r, ch, size=size: r.at[_span(ch, size, 8), :]))
            else:
                pieces.append((si, oi, lambda r, t: r, lambda r, ch, size=size: r.at[:, _span(ch, size, LANE)]))
        else:
            for l in range(local.shape[0]):
                if ax == 1:
                    dv = lambda r, ch, l=l, size=size: r.at[l, _span(ch, size, 8), :]
                else:
                    dv = lambda r, ch, l=l, size=size: r.at[l, :, _span(ch, size, LANE)]
                pieces.append((si, oi, lambda r, t, l=l: r.at[l], dv))

    for name in GATHER_MXU:
        add(name)
    small = _pack_small([w[n] for n in GATHER_F32])
    srcs.append(small)
    outs.append(jax.ShapeDtypeStruct((NCHIP,) + small.shape, F32))
    pieces.append((len(srcs) - 1, len(outs) - 1, lambda r, t: r, lambda r, ch: r.at[ch]))
    got = _chip_exchange(srcs, outs, pieces, name="gather_weights")
    full = dict(zip(GATHER_MXU, got[:-1]))
    per_chip = [_unpack_small(got[-1][t], [w[n].shape for n in GATHER_F32]) for t in range(NCHIP)]
    for j, n in enumerate(GATHER_F32):
        full[n] = jnp.concatenate([per_chip[t][j] for t in range(NCHIP)], axis=SHARD_AXIS[n])
    return full


BIG = GATHER_MXU
SMALL = [n for n in WNAMES if n not in BIG]


def _as_rows(a):
    return a.reshape(-1, a.shape[-1])


def _reduce_and_update(grads, w, mom, var):
    srcs, outs, pieces = [], [], []
    for name in BIG:
        local = w[name]
        ax = SHARD_AXIS[name]
        size = local.shape[ax]
        oi = len(outs)
        outs.append(jax.ShapeDtypeStruct((NCHIP,) + local.shape, F32))
        layers = grads[name] if isinstance(grads[name], list) else [grads[name]]
        for l, g in enumerate(layers):
            si = len(srcs)
            srcs.append(g)
            if ax == 1:
                sv = lambda r, t, size=size: r.at[_span(t, size, 8), :]
            else:
                sv = lambda r, t, size=size: r.at[:, _span(t, size, LANE)]
            pieces.append((si, oi, sv, lambda r, ch, l=l: r.at[ch, l]))
    small = jnp.stack([_pack_small([_shard(grads[n], t, SHARD_AXIS[n]) if n in SHARD_AXIS else grads[n] for n in SMALL])
                       for t in range(NCHIP)])
    srcs.append(small)
    outs.append(jax.ShapeDtypeStruct(small.shape, F32))
    pieces.append((len(srcs) - 1, len(outs) - 1, lambda r, t: r.at[t], lambda r, ch: r.at[ch]))
    landed = _chip_exchange(srcs, outs, pieces, name="grad_exchange")
    partial = [_sum_chips(r.reshape(NCHIP, -1, r.shape[-1]), name="grad_sum_chips") for r in landed]
    sibling = _sibling_exchange(partial, name="grad_sibling")
    res = [dict(), dict(), dict(), dict()]
    for j, name in enumerate(BIG):
        outs4 = _adamw(partial[j], sibling[j], _as_rows(w[name]), _as_rows(mom[name]), _as_rows(var[name]), name="adamw")
        for d, o in zip(res, outs4):
            d[name] = o.reshape(w[name].shape)
    outs4 = _adamw(partial[-1], sibling[-1], *[_pack_small([t[n] for n in SMALL]) for t in (w, mom, var)], name="adamw")
    for d, o in zip(res, outs4):
        d.update(zip(SMALL, _unpack_small(o, [w[n].shape for n in SMALL])))
    return res


def kernel(x, positions, mla_w_a, mla_g_q, mla_g_kv, mla_w_uq, mla_w_ukv, mla_w_o, ssm_w_in, ssm_lambda_re, ssm_lambda_im, ssm_log_dt, ssm_b_re, ssm_b_im, ssm_c_re, ssm_c_im, ssm_d, ssm_w_glu, ffn_w_up, ffn_conv_w, ffn_conv_b, ffn_w_down, g_mix, g_ffn, g_final, loss_target, m_mla_w_a, m_mla_g_q, m_mla_g_kv, m_mla_w_uq, m_mla_w_ukv, m_mla_w_o, m_ssm_w_in, m_ssm_lambda_re, m_ssm_lambda_im, m_ssm_log_dt, m_ssm_b_re, m_ssm_b_im, m_ssm_c_re, m_ssm_c_im, m_ssm_d, m_ssm_w_glu, m_ffn_w_up, m_ffn_conv_w, m_ffn_conv_b, m_ffn_w_down, m_g_mix, m_g_ffn, m_g_final, v_mla_w_a, v_mla_g_q, v_mla_g_kv, v_mla_w_uq, v_mla_w_ukv, v_mla_w_o, v_ssm_w_in, v_ssm_lambda_re, v_ssm_lambda_im, v_ssm_log_dt, v_ssm_b_re, v_ssm_b_im, v_ssm_c_re, v_ssm_c_im, v_ssm_d, v_ssm_w_glu, v_ffn_w_up, v_ffn_conv_w, v_ffn_conv_b, v_ffn_w_down, v_g_mix, v_g_ffn, v_g_final):
    w = dict(zip(WNAMES, (mla_w_a, mla_g_q, mla_g_kv, mla_w_uq, mla_w_ukv, mla_w_o, ssm_w_in, ssm_lambda_re,
                          ssm_lambda_im, ssm_log_dt, ssm_b_re, ssm_b_im, ssm_c_re, ssm_c_im, ssm_d, ssm_w_glu,
                          ffn_w_up, ffn_conv_w, ffn_conv_b, ffn_w_down, g_mix, g_ffn, g_final)))
    mom = dict(zip(WNAMES, (m_mla_w_a, m_mla_g_q, m_mla_g_kv, m_mla_w_uq, m_mla_w_ukv, m_mla_w_o, m_ssm_w_in,
                            m_ssm_lambda_re, m_ssm_lambda_im, m_ssm_log_dt, m_ssm_b_re, m_ssm_b_im, m_ssm_c_re,
                            m_ssm_c_im, m_ssm_d, m_ssm_w_glu, m_ffn_w_up, m_ffn_conv_w, m_ffn_conv_b,
                            m_ffn_w_down, m_g_mix, m_g_ffn, m_g_final)))
    var = dict(zip(WNAMES, (v_mla_w_a, v_mla_g_q, v_mla_g_kv, v_mla_w_uq, v_mla_w_ukv, v_mla_w_o, v_ssm_w_in,
                            v_ssm_lambda_re, v_ssm_lambda_im, v_ssm_log_dt, v_ssm_b_re, v_ssm_b_im, v_ssm_c_re,
                            v_ssm_c_im, v_ssm_d, v_ssm_w_glu, v_ffn_w_up, v_ffn_conv_w, v_ffn_conv_b,
                            v_ffn_w_down, v_g_mix, v_g_ffn, v_g_final)))
    S = x.shape[1]
    D = D_MODEL
    x2 = x.reshape(S, D)
    tgt = loss_target.reshape(S, D)

    fw = _gather_weights(w)
    w_a = jnp.pad(fw['mla_w_a'], ((0, 0), (0, A_PAD - KR0 - QK_ROPE)))
    uq = fw['mla_w_uq'].reshape(Q_LORA, HEADS, QK_DIM)
    w_uq = jnp.concatenate([uq[:, :, :QK_NOPE].reshape(Q_LORA, HEADS * QK_NOPE),
                            jnp.pad(uq[:, :, QK_NOPE:], ((0, 0), (0, 0), (0, LANE - QK_ROPE))).reshape(Q_LORA, HEADS * LANE)],
                           axis=1)
    w_ukv = fw['mla_w_ukv']
    w_o = fw['mla_w_o']
    w_in = fw['ssm_w_in']
    w_glu = fw['ssm_w_glu']
    w_up = fw['ffn_w_up']
    w_down = fw['ffn_w_down']
    conv_w = fw['ffn_conv_w']
    dskip = fw['ssm_d']
    conv_b = w['ffn_conv_b']
    g_q, g_kv = w['mla_g_q'], w['mla_g_kv']
    gm, gf = w['g_mix'], w['g_ffn']
    gfin = w['g_final'].reshape(1, D)

    inv = 1.0 / (ROPE_THETA ** (jnp.arange(0, QK_ROPE, 2, dtype=F32) / QK_ROPE))
    ang = positions.reshape(S).astype(F32)[:, None] * inv
    cos, sin = jnp.cos(ang), jnp.sin(ang)
    zpad = jnp.zeros((S, LANE - QK_ROPE), F32)
    c128 = jnp.concatenate([cos, cos, zpad], axis=1)
    s128 = jnp.concatenate([-sin, sin, zpad], axis=1)

    hn0 = _rmsnorm_fwd(x2, gm[0:1], name="rms_mix0")
    a = _mm(hn0, w_a, name="mla_a")
    cqn, ckvn, kr = _mla_mid_fwd(a, g_q, g_kv, c128, s128, name="mla_mid_fwd")
    qfull = _mm(cqn, w_uq, name="mla_q")
    kv = _mm(ckvn, w_ukv, out_dtype=MXU, name="mla_kv")
    qs, ks, vs = _qk_prep(qfull, kv, kr, c128, s128, name="qk_prep")
    os_, lses = [], []
    for h in range(HEADS):
        o_h, lse_h = _attn_fwd(qs[h], ks[h], vs[h], name="attn_fwd")
        os_.append(o_h)
        lses.append(lse_h)
    o_cat = jnp.concatenate(os_, axis=1)
    h1 = _mm(o_cat, w_o, res=x2, name="mla_o")

    def ffn_fwd(h, l):
        hn = _rmsnorm_fwd(h, gf[l:l + 1], name="rms_ffn")
        up = _mm(hn, w_up[l], name="ffn_up")
        act = _convgate_fwd(up, conv_w[l], conv_b[l:l + 1], name="convgate_fwd")
        return _mm(act, w_down[l], res=h, name="ffn_down"), (hn, up, act)

    h2, saved0 = ffn_fwd(h1, 0)

    lam_re, lam_im, log_dt = w['ssm_lambda_re'][0], w['ssm_lambda_im'][0], w['ssm_log_dt'][0]
    (a_re, a_im, bbar_r, bbar_i), disc_vjp = jax.vjp(_s5_discretize, lam_re, lam_im, log_dt, w['ssm_b_re'][0],
                                                    w['ssm_b_im'][0])
    c_re, c_im = w['ssm_c_re'][0], w['ssm_c_im'][0]
    bt_r, bt_i = jnp.swapaxes(bbar_r, 1, 2), jnp.swapaxes(bbar_i, 1, 2)
    wb = jnp.concatenate([_blockdiag(bt_r), _blockdiag(bt_i)], axis=2).astype(MXU)
    wbt = jnp.concatenate([_blockdiag(bbar_r), _blockdiag(bbar_i)], axis=1).astype(MXU)
    ct_r, ct_i = jnp.swapaxes(c_re, 1, 2), jnp.swapaxes(c_im, 1, 2)
    wc = jnp.concatenate([_blockdiag(ct_r), _blockdiag(-ct_i)], axis=1).astype(MXU)
    wct = jnp.concatenate([_blockdiag(c_re), _blockdiag(-c_im)], axis=2).astype(MXU)
    af_r, af_i = a_re.reshape(NSTATE), a_im.reshape(NSTATE)
    abc = jnp.broadcast_to(jnp.concatenate([af_r, af_i])[None], (NSEG, ST2))
    seg = S // NSEG
    ap_r, ap_i = _cpow(af_r, af_i, seg)

    hn1 = _rmsnorm_fwd(h2, gm[1:2], name="rms_mix1")
    u = _mm(hn1, w_in, name="s5_in")
    u_p = _perm(u)
    zero_state = jnp.zeros((NSEG, ST2), F32)
    ends = _s5_fwd(u_p, wb, wc, abc, dskip, zero_state, full=False, name="s5_fwd_ends")
    inits, cr, ci = [], jnp.zeros((NSTATE,), F32), jnp.zeros((NSTATE,), F32)
    for r in range(NSEG):
        inits.append(jnp.concatenate([cr, ci]))
        er, ei = ends[r, :NSTATE], ends[r, NSTATE:]
        cr, ci = er + ap_r * cr - ap_i * ci, ei + ap_r * ci + ap_i * cr
    x0 = jnp.stack(inits)
    xs, y_p, yg_p = _s5_fwd(u_p, wb, wc, abc, dskip, x0, full=True, name="s5_fwd")
    yg = _unperm(yg_p)
    z = _mm(yg, w_glu, name="s5_glu")
    h3 = _glu_fwd(z, h2, name="glu_fwd")
    h4, saved1 = ffn_fwd(h3, 1)

    loss_l, dh4, dg_final = _loss_head(h4, gfin, tgt, name="loss_head")

    grads = {}

    def ffn_bwd(h_in, g, saved, l):
        hn, up, act = saved
        w_up_t = w_up[l].T
        dact = _mm(g, w_down[l].T, name="ffn_down_dx")
        dw_down = _mm(act, g, mode="tn", name="ffn_down_dw")
        duv, dug, dwv, dwg, dbv, dbg = _convgate_bwd(up, dact, conv_w[l], conv_b[l:l + 1], name="convgate_bwd")
        dw_up = jnp.concatenate([_mm(hn, duv, mode="tn", name="ffn_up_dw"), _mm(hn, dug, mode="tn", name="ffn_up_dw")],
                                axis=1)
        dhn = _mm(duv, w_up_t[:D_FF], name="ffn_up_dx")
        dhn = _mm(dug, w_up_t[D_FF:], res=dhn, name="ffn_up_dx_acc")
        dh, dg = _rmsnorm_bwd(h_in, gf[l:l + 1], dhn, g, name="rms_ffn_bwd")
        return dh, dict(w_up=dw_up, w_down=dw_down, conv_w=jnp.concatenate([dwv, dwg], axis=1),
                        conv_b=jnp.concatenate([dbv, dbg], axis=1)[0], g_ffn=dg[0])

    dh3, fg1 = ffn_bwd(h3, dh4, saved1, 1)

    dz = _glu_bwd(z, dh3, name="glu_bwd")
    grads['ssm_w_glu'] = _mm(yg, dz, mode="tn", name="s5_glu_dw")
    dyg = _mm(dz, w_glu.T, name="s5_glu_dx")
    dy_p = _gelu_bwd(y_p, _perm(dyg), name="gelu_bwd")
    firsts = _s5_bwd(dy_p, None, None, wct, None, abc, None, None, zero_state, full=False, name="s5_bwd_firsts")
    linits, cr, ci = [None] * NSEG, jnp.zeros((NSTATE,), F32), jnp.zeros((NSTATE,), F32)
    for r in reversed(range(NSEG)):
        linits[r] = jnp.concatenate([cr, ci])
        fr, fi = firsts[r, :NSTATE], firsts[r, NSTATE:]
        cr, ci = fr + ap_r * cr + ap_i * ci, fi + ap_r * ci - ap_i * cr
    l0 = jnp.stack(linits)
    du_p, dab, dwb, dwc, dd = _s5_bwd(dy_p, xs, u_p, wct, wbt, abc, dskip, x0, l0, full=True, name="s5_bwd")
    du = _unperm(du_p)
    grads['ssm_w_in'] = _mm(hn1, du, mode="tn", name="s5_in_dw")
    dhn1 = _mm(du, w_in.T, name="s5_in_dx")
    dh2, dg_mix1 = _rmsnorm_bwd(h2, gm[1:2], dhn1, dh3, name="rms_mix_bwd")
    da_sum = jnp.sum(dab, axis=0)
    dbt_r = _blockdiag_extract(dwb[:, :, :SW], SSM_GROUP, SSM_STATE)
    dbt_i = _blockdiag_extract(dwb[:, :, SW:], SSM_GROUP, SSM_STATE)
    dlr, dli, dlog_dt, dbr, dbi = disc_vjp((da_sum[:NSTATE].reshape(SSM_GROUPS, SSM_STATE),
                                            da_sum[NSTATE:].reshape(SSM_GROUPS, SSM_STATE),
                                            jnp.swapaxes(dbt_r, 1, 2), jnp.swapaxes(dbt_i, 1, 2)))
    dct_r = _blockdiag_extract(dwc[:, :SW, :], SSM_STATE, SSM_GROUP)
    dct_i = _blockdiag_extract(dwc[:, SW:, :], SSM_STATE, SSM_GROUP)
    grads['ssm_lambda_re'], grads['ssm_lambda_im'], grads['ssm_log_dt'] = dlr[None], dli[None], dlog_dt[None]
    grads['ssm_b_re'], grads['ssm_b_im'] = dbr[None], dbi[None]
    grads['ssm_c_re'] = jnp.swapaxes(dct_r, 1, 2)[None]
    grads['ssm_c_im'] = -jnp.swapaxes(dct_i, 1, 2)[None]
    grads['ssm_d'] = dd

    dh1, fg0 = ffn_bwd(h1, dh2, saved0, 0)
    grads['ffn_w_up'] = [fg0['w_up'], fg1['w_up']]
    grads['ffn_w_down'] = [fg0['w_down'], fg1['w_down']]
    grads['ffn_conv_w'] = jnp.stack([fg0['conv_w'], fg1['conv_w']])
    grads['ffn_conv_b'] = jnp.stack([fg0['conv_b'], fg1['conv_b']])
    grads['g_ffn'] = jnp.stack([fg0['g_ffn'], fg1['g_ffn']])

    do_cat = _mm(dh1, w_o.T, out_dtype=MXU, name="mla_o_dx")
    grads['mla_w_o'] = _mm(o_cat, dh1, mode="tn", name="mla_o_dw")
    dqs, dks, dvs = [], [], []
    for h in range(HEADS):
        do_h = do_cat[:, h * V_HEAD:(h + 1) * V_HEAD]
        delta = _attn_delta(do_h, os_[h], name="attn_delta")
        dq_h, dk_h, dv_h = _attn_bwd(qs[h], ks[h], vs[h], do_h, lses[h].reshape(1, S), delta.reshape(1, S),
                                     name="attn_bwd")
        dqs.append(dq_h)
        dks.append(dk_h)
        dvs.append(dv_h)
    dqfull, dkv, dkr = _qk_prep_bwd(dqs, dks, dvs, c128, s128, name="qk_prep_bwd")
    dw_uq_p = _mm(cqn, dqfull, mode="tn", name="mla_q_dw")
    dcqn = _mm(dqfull, w_uq.T, name="mla_q_dx")
    grads['mla_w_ukv'] = _mm(ckvn, dkv, mode="tn", name="mla_kv_dw")
    dckvn = _mm(dkv, w_ukv.T, name="mla_kv_dx")
    da, dgq, dgkv = _mla_mid_bwd(a, dcqn, dckvn, dkr, g_q, g_kv, c128, s128, name="mla_mid_bwd")
    grads['mla_w_a'] = _mm(hn0, da, mode="tn", name="mla_a_dw")[:, :KR0 + QK_ROPE]
    dhn0 = _mm(da, w_a.T, name="mla_a_dx")
    dx, dg_mix0 = _rmsnorm_bwd(x2, gm[0:1], dhn0, dh1, name="rms_mix_bwd")
    grads['mla_w_uq'] = jnp.concatenate(
        [dw_uq_p[:, :HEADS * QK_NOPE].reshape(Q_LORA, HEADS, QK_NOPE),
         dw_uq_p[:, HEADS * QK_NOPE:].reshape(Q_LORA, HEADS, LANE)[:, :, :QK_ROPE]], axis=2).reshape(Q_LORA, HEADS * QK_DIM)
    grads['mla_g_q'], grads['mla_g_kv'] = dgq, dgkv
    grads['g_mix'] = jnp.concatenate([dg_mix0, dg_mix1], axis=0)
    grads['g_final'] = dg_final[0]

    g_out, d_out, m_out, v_out = _reduce_and_update(grads, w, mom, var)

    loss = lax.psum(loss_l[0, 0], ("x", "y", "c"))
    return (loss, dx.reshape(1, S, D), *[g_out[n] for n in WNAMES], *[d_out[n] for n in WNAMES],
            *[m_out[n] for n in WNAMES], *[v_out[n] for n in WNAMES])
```

```python
import math

import jax
import jax.numpy as jnp
from jax import lax
from jax.experimental import pallas as pl
from jax.experimental.pallas import tpu as pltpu

F32 = jnp.float32
MXU = jnp.bfloat16

D_MODEL = 1024
CHUNK = 64
EPS = 1e-6
HEADS = 8
QK_NOPE = 128
QK_ROPE = 64
V_HEAD = 128
Q_LORA = 384
KV_LORA = 256
ROPE_THETA = 10000.0
QK_DIM = QK_NOPE + QK_ROPE
SSM_GROUP = 16
SSM_GROUPS = D_MODEL // SSM_GROUP
SSM_STATE = 64
NSTATE = SSM_GROUPS * SSM_STATE
D_FF = 2816
ATT_SCALE = QK_DIM ** -0.5
EXP2_SCALE = ATT_SCALE * math.log2(math.e)
NEG = -1e30
NSEG = 8
SBLK = 8

ADAM_LR = 0.001
ADAM_B1 = 0.9
ADAM_B2 = 0.999
ADAM_EPS = 1e-08
ADAM_WD = 0.01
ADAM_STEP = 10

LANE = 128
TN_MAX_COLS = 2816
TN_ACC_ELEMS = 1536 * 1024
VMEM_BIG = 56 * 1024 * 1024

WNAMES = ['mla_w_a', 'mla_g_q', 'mla_g_kv', 'mla_w_uq', 'mla_w_ukv', 'mla_w_o', 'ssm_w_in', 'ssm_lambda_re',
          'ssm_lambda_im', 'ssm_log_dt', 'ssm_b_re', 'ssm_b_im', 'ssm_c_re', 'ssm_c_im', 'ssm_d', 'ssm_w_glu',
          'ffn_w_up', 'ffn_conv_w', 'ffn_conv_b', 'ffn_w_down', 'g_mix', 'g_ffn', 'g_final']
FWD_NAMES = ['x', 'positions'] + WNAMES
SHARD_AXIS = {'mla_w_a': 1, 'mla_w_uq': 2, 'mla_w_ukv': 2, 'mla_w_o': 1, 'ssm_w_in': 1, 'ssm_d': 1,
              'ssm_w_glu': 2, 'ffn_w_up': 2, 'ffn_conv_w': 2, 'ffn_w_down': 1}
GATHER_MXU = ['mla_w_a', 'mla_w_uq', 'mla_w_ukv', 'mla_w_o', 'ssm_w_in', 'ssm_w_glu', 'ffn_w_up', 'ffn_w_down']
GATHER_F32 = ['ssm_d', 'ffn_conv_w']
NCHIP = 4
PACKW = 1024
SMALL_ROWS = 64
MESH = pl.DeviceIdType.MESH


def _tile(d, pref):
    t = min(pref, d) // LANE * LANE
    while t >= LANE:
        if d % t == 0:
            return t
        t -= LANE
    return d


def _rows(s, pref):
    t = min(s, pref)
    assert s % t == 0 and t % 8 == 0
    return t


def _params(big=False):
    if big:
        return pltpu.CompilerParams(vmem_limit_bytes=VMEM_BIG)
    return pltpu.CompilerParams(vmem_limit_bytes=40 * 1024 * 1024)


def _mm(a, b, *, name, mode="nn", out_dtype=F32, res=None, tm=None, tn=None, tk=None):
    if mode == "nn":
        M, K = a.shape
        tm = tm or _rows(M, 1024)
    else:
        K, M = a.shape
        tn = tn or _tile(b.shape[1], TN_MAX_COLS)
        tm = tm or _tile(M, max(LANE, TN_ACC_ELEMS // tn))
    N = b.shape[1]
    assert b.shape[0] == K
    tn = tn or _tile(N, 512)
    tk = tk or (_tile(K, 1408) if mode == "nn" else _rows(K, 512))
    nk = K // tk
    has_res = res is not None

    def body(a_ref, b_ref, *rest):
        if has_res:
            r_ref, o_ref, acc = rest
        else:
            o_ref, acc = rest
        k = pl.program_id(2)

        @pl.when(k == 0)
        def _():
            acc[...] = jnp.zeros_like(acc)

        av = a_ref[...].astype(MXU)
        bv = b_ref[...].astype(MXU)
        if mode == "nn":
            acc[...] += jnp.dot(av, bv, preferred_element_type=F32)
        else:
            acc[...] += lax.dot_general(av, bv, (((0,), (0,)), ((), ())), preferred_element_type=F32)

        @pl.when(k == nk - 1)
        def _():
            o = acc[...]
            if has_res:
                o = o + r_ref[...]
            o_ref[...] = o.astype(o_ref.dtype)

    if mode == "nn":
        a_spec = pl.BlockSpec((tm, tk), lambda i, j, k: (i, k))
    else:
        a_spec = pl.BlockSpec((tk, tm), lambda i, j, k: (k, i))
    in_specs = [a_spec, pl.BlockSpec((tk, tn), lambda i, j, k: (k, j))]
    ops = [a, b]
    if has_res:
        in_specs.append(pl.BlockSpec((tm, tn), lambda i, j, k: (i, j)))
        ops.append(res)
    return pl.pallas_call(
        body, name=name, grid=(M // tm, N // tn, nk), in_specs=in_specs,
        out_specs=pl.BlockSpec((tm, tn), lambda i, j, k: (i, j)),
        out_shape=jax.ShapeDtypeStruct((M, N), out_dtype),
        scratch_shapes=[pltpu.VMEM((tm, tn), F32)], compiler_params=_params(),
    )(*ops)


def _row_spec(tm, c):
    return pl.BlockSpec((tm, c), lambda i: (i, 0))


def _const_spec(r, c):
    return pl.BlockSpec((r, c), lambda i: (0, 0))


def _rms_parts(xv):
    r = lax.rsqrt(jnp.mean(xv * xv, axis=-1, keepdims=True) + EPS)
    return r, xv * r


def _rms_vjp(xv, gv, dyv):
    r, xhat = _rms_parts(xv)
    gy = dyv * gv
    dx = r * (gy - xhat * jnp.mean(gy * xhat, axis=-1, keepdims=True))
    return dx, dyv * xhat


def _rmsnorm_fwd(x, g, *, name):
    S, D = x.shape
    tm = _rows(S, 512)

    def body(x_ref, g_ref, o_ref):
        _, xhat = _rms_parts(x_ref[...])
        o_ref[...] = (xhat * g_ref[...]).astype(o_ref.dtype)

    return pl.pallas_call(
        body, name=name, grid=(S // tm,), in_specs=[_row_spec(tm, D), _const_spec(1, D)],
        out_specs=_row_spec(tm, D), out_shape=jax.ShapeDtypeStruct((S, D), MXU), compiler_params=_params(),
    )(x, g)


def _rmsnorm_bwd(x, g, dy, dres, *, name):
    S, D = x.shape
    tm = _rows(S, 512)

    def body(x_ref, g_ref, dy_ref, dr_ref, dx_ref, dg_ref):
        @pl.when(pl.program_id(0) == 0)
        def _():
            dg_ref[...] = jnp.zeros_like(dg_ref)

        dx, dgp = _rms_vjp(x_ref[...], g_ref[...], dy_ref[...])
        dx_ref[...] = dr_ref[...] + dx
        dg_ref[...] += jnp.sum(dgp, axis=0, keepdims=True)

    return pl.pallas_call(
        body, name=name, grid=(S // tm,),
        in_specs=[_row_spec(tm, D), _const_spec(1, D), _row_spec(tm, D), _row_spec(tm, D)],
        out_specs=(_row_spec(tm, D), _const_spec(1, D)),
        out_shape=(jax.ShapeDtypeStruct((S, D), F32), jax.ShapeDtypeStruct((1, D), F32)),
        compiler_params=_params(),
    )(x, g, dy, dres)


def _loss_head(h, g, tgt, *, name):
    S, D = h.shape
    tm = _rows(S, 512)

    def body(h_ref, g_ref, t_ref, l_ref, dh_ref, dg_ref):
        @pl.when(pl.program_id(0) == 0)
        def _():
            l_ref[...] = jnp.zeros_like(l_ref)
            dg_ref[...] = jnp.zeros_like(dg_ref)

        hv = h_ref[...]
        gv = g_ref[...]
        _, xhat = _rms_parts(hv)
        e = xhat * gv - t_ref[...]
        l_ref[...] += 0.5 * jnp.sum(jnp.mean(e * e, axis=-1, keepdims=True), axis=0, keepdims=True)
        dx, dgp = _rms_vjp(hv, gv, e * (1.0 / D))
        dh_ref[...] = dx
        dg_ref[...] += jnp.sum(dgp, axis=0, keepdims=True)

    return pl.pallas_call(
        body, name=name, grid=(S // tm,),
        in_specs=[_row_spec(tm, D), _const_spec(1, D), _row_spec(tm, D)],
        out_specs=(_const_spec(1, 1), _row_spec(tm, D), _const_spec(1, D)),
        out_shape=(jax.ShapeDtypeStruct((1, 1), F32), jax.ShapeDtypeStruct((S, D), F32),
                   jax.ShapeDtypeStruct((1, D), F32)),
        compiler_params=_params(),
    )(h, g, tgt)


def _swap_halves(g):
    lane = lax.broadcasted_iota(jnp.int32, g.shape, 1)
    return jnp.where(lane < QK_ROPE // 2, pltpu.roll(g, LANE - QK_ROPE // 2, axis=1),
                     pltpu.roll(g, QK_ROPE // 2, axis=1))


def _rope128(g, c128, s128):
    return g * c128 + _swap_halves(g) * s128


def _rope128_vjp(dy, c128, s128):
    lane = lax.broadcasted_iota(jnp.int32, dy.shape, 1)
    return jnp.where(lane < QK_ROPE, dy * c128 + _swap_halves(dy * s128), 0.0)


A_PAD = 768
KR0 = Q_LORA + KV_LORA


def _mla_mid_fwd(a, g_q, g_kv, c128, s128, *, name):
    S = a.shape[0]
    tm = _rows(S, 512)

    def body(a_ref, gq_ref, gkv_ref, c_ref, s_ref, cq_ref, ckv_ref, kr_ref):
        av = a_ref[...]
        _, qh = _rms_parts(av[:, :Q_LORA])
        cq_ref[...] = (qh * gq_ref[...]).astype(cq_ref.dtype)
        _, kh = _rms_parts(av[:, Q_LORA:KR0])
        ckv_ref[...] = (kh * gkv_ref[...]).astype(ckv_ref.dtype)
        kr = _rope128(av[:, KR0:A_PAD], c_ref[...], s_ref[...])
        kr_ref[...] = kr[:, :QK_ROPE].astype(kr_ref.dtype)

    return pl.pallas_call(
        body, name=name, grid=(S // tm,),
        in_specs=[_row_spec(tm, A_PAD), _const_spec(1, Q_LORA), _const_spec(1, KV_LORA), _row_spec(tm, LANE),
                  _row_spec(tm, LANE)],
        out_specs=(_row_spec(tm, Q_LORA), _row_spec(tm, KV_LORA), _row_spec(tm, QK_ROPE)),
        out_shape=(jax.ShapeDtypeStruct((S, Q_LORA), MXU), jax.ShapeDtypeStruct((S, KV_LORA), MXU),
                   jax.ShapeDtypeStruct((S, QK_ROPE), MXU)),
        compiler_params=_params(),
    )(a, g_q, g_kv, c128, s128)


def _mla_mid_bwd(a, dcq, dckv, dkr, g_q, g_kv, c128, s128, *, name):
    S = a.shape[0]
    tm = _rows(S, 512)

    def body(a_ref, dcq_ref, dckv_ref, dkr_ref, gq_ref, gkv_ref, c_ref, s_ref, da_ref, dgq_ref, dgkv_ref):
        @pl.when(pl.program_id(0) == 0)
        def _():
            dgq_ref[...] = jnp.zeros_like(dgq_ref)
            dgkv_ref[...] = jnp.zeros_like(dgkv_ref)

        av = a_ref[...]
        dx, dgp = _rms_vjp(av[:, :Q_LORA], gq_ref[...], dcq_ref[...])
        da_ref[:, :Q_LORA] = dx.astype(da_ref.dtype)
        dgq_ref[...] += jnp.sum(dgp, axis=0, keepdims=True)
        dx, dgp = _rms_vjp(av[:, Q_LORA:KR0], gkv_ref[...], dckv_ref[...])
        da_ref[:, Q_LORA:KR0] = dx.astype(da_ref.dtype)
        dgkv_ref[...] += jnp.sum(dgp, axis=0, keepdims=True)
        da_ref[:, KR0:A_PAD] = _rope128_vjp(dkr_ref[...], c_ref[...], s_ref[...]).astype(da_ref.dtype)

    return pl.pallas_call(
        body, name=name, grid=(S // tm,),
        in_specs=[_row_spec(tm, A_PAD), _row_spec(tm, Q_LORA), _row_spec(tm, KV_LORA), _row_spec(tm, LANE),
                  _const_spec(1, Q_LORA), _const_spec(1, KV_LORA), _row_spec(tm, LANE), _row_spec(tm, LANE)],
        out_specs=(_row_spec(tm, A_PAD), _const_spec(1, Q_LORA), _const_spec(1, KV_LORA)),
        out_shape=(jax.ShapeDtypeStruct((S, A_PAD), MXU), jax.ShapeDtypeStruct((1, Q_LORA), F32),
                   jax.ShapeDtypeStruct((1, KV_LORA), F32)),
        compiler_params=_params(),
    )(a, dcq, dckv, dkr, g_q, g_kv, c128, s128)


QF = 2 * HEADS * LANE
KVF = HEADS * (QK_NOPE + V_HEAD)
VX = 2 * V_HEAD


def _qk_prep(qfull, kv, kr, c128, s128, *, name):
    S = qfull.shape[0]
    tm = _rows(S, 256)

    def body(q_ref, kv_ref, kr_ref, c_ref, s_ref, *outs):
        qo, ko, vo = outs[:HEADS], outs[HEADS:2 * HEADS], outs[2 * HEADS:]
        cv, sv = c_ref[...], s_ref[...]
        krv = kr_ref[...]
        for h in range(HEADS):
            qo[h][:, :QK_NOPE] = q_ref[:, h * LANE:(h + 1) * LANE].astype(MXU)
            g = q_ref[:, (HEADS + h) * LANE:(HEADS + h + 1) * LANE]
            qo[h][:, QK_NOPE:] = _rope128(g, cv, sv)[:, :QK_ROPE].astype(MXU)
            ko[h][:, :QK_NOPE] = kv_ref[:, 2 * h * LANE:(2 * h + 1) * LANE]
            ko[h][:, QK_NOPE:] = krv
            vo[h][:, :V_HEAD] = kv_ref[:, (2 * h + 1) * LANE:(2 * h + 2) * LANE]
            vo[h][:, V_HEAD:] = jnp.ones((tm, VX - V_HEAD), MXU)

    shapes = ([jax.ShapeDtypeStruct((S, QK_DIM), MXU)] * (2 * HEADS)
              + [jax.ShapeDtypeStruct((S, VX), MXU)] * HEADS)
    specs = [_row_spec(tm, QK_DIM)] * (2 * HEADS) + [_row_spec(tm, VX)] * HEADS
    outs = pl.pallas_call(
        body, name=name, grid=(S // tm,),
        in_specs=[_row_spec(tm, QF), _row_spec(tm, KVF), _row_spec(tm, QK_ROPE), _row_spec(tm, LANE),
                  _row_spec(tm, LANE)],
        out_specs=tuple(specs), out_shape=tuple(shapes), compiler_params=_params(),
    )(qfull, kv, kr, c128, s128)
    return outs[:HEADS], outs[HEADS:2 * HEADS], outs[2 * HEADS:]


def _qk_prep_bwd(dqs, dks, dvs, c128, s128, *, name):
    S = dqs[0].shape[0]
    tm = _rows(S, 256)

    def body(*refs):
        dq = refs[:HEADS]
        dk = refs[HEADS:2 * HEADS]
        dv = refs[2 * HEADS:3 * HEADS]
        c_ref, s_ref, dqf_ref, dkv_ref, dkr_ref, tmp = refs[3 * HEADS:]
        cv, sv = c_ref[...], s_ref[...]
        tmp[...] = jnp.zeros_like(tmp)
        dkr_ref[...] = jnp.zeros_like(dkr_ref)
        for h in range(HEADS):
            dqf_ref[:, h * LANE:(h + 1) * LANE] = dq[h][:, :QK_NOPE].astype(MXU)
            tmp[:, :QK_ROPE] = dq[h][:, QK_NOPE:]
            dqf_ref[:, (HEADS + h) * LANE:(HEADS + h + 1) * LANE] = _rope128_vjp(tmp[...], cv, sv).astype(MXU)
            dkv_ref[:, 2 * h * LANE:(2 * h + 1) * LANE] = dk[h][:, :QK_NOPE].astype(MXU)
            dkv_ref[:, (2 * h + 1) * LANE:(2 * h + 2) * LANE] = dv[h][...].astype(MXU)
            dkr_ref[:, :QK_ROPE] += dk[h][:, QK_NOPE:]

    return pl.pallas_call(
        body, name=name, grid=(S // tm,),
        in_specs=[_row_spec(tm, QK_DIM)] * (2 * HEADS) + [_row_spec(tm, V_HEAD)] * HEADS
        + [_row_spec(tm, LANE), _row_spec(tm, LANE)],
        out_specs=(_row_spec(tm, QF), _row_spec(tm, KVF), _row_spec(tm, LANE)),
        out_shape=(jax.ShapeDtypeStruct((S, QF), MXU), jax.ShapeDtypeStruct((S, KVF), MXU),
                   jax.ShapeDtypeStruct((S, LANE), F32)),
        scratch_shapes=[pltpu.VMEM((tm, LANE), F32)], compiler_params=_params(),
    )(*dqs, *dks, *dvs, c128, s128)


def _dot_nt(a, b):
    return lax.dot_general(a, b, (((1,), (1,)), ((), ())), preferred_element_type=F32)


def _dot_tn(a, b):
    return lax.dot_general(a, b, (((0,), (0,)), ((), ())), preferred_element_type=F32)


def _attn_fwd(q, k, vx, *, name):
    S = q.shape[0]
    T = _rows(S, 1024)
    n = S // T
    cpt = T // CHUNK

    def body(q_ref, k_ref, v_ref, o_ref, lse_ref, s_buf, p_buf, a_buf, m_s, acc_s):
        i = pl.program_id(0)
        qc = lax.broadcasted_iota(jnp.int32, (T, T), 0) // CHUNK
        kc = lax.broadcasted_iota(jnp.int32, (T, T), 1) // CHUNK
        dchunk = kc - qc

        def tile_rows(b):
            return pl.ds(pl.multiple_of(jnp.clip(b, 0, n - 1) * T, T), T)

        def scores(b, slot):
            s = _dot_nt(q_ref[...], k_ref[tile_rows(b), :])
            s_buf[slot] = jnp.where(dchunk <= (i - b) * cpt, s, NEG)

        def softmax(slot):
            s = s_buf[slot]
            m_prev = m_s[...]
            m_new = jnp.maximum(m_prev, jnp.max(s, axis=1, keepdims=True))
            a_buf[slot] = jnp.exp2((m_prev - m_new) * EXP2_SCALE)
            p_buf[slot] = jnp.exp2((s - m_new) * EXP2_SCALE).astype(MXU)
            m_s[...] = m_new

        def pv(b, slot):
            acc_s[...] = a_buf[slot] * acc_s[...] + jnp.dot(p_buf[slot], v_ref[tile_rows(b), :],
                                                              preferred_element_type=F32)

        m_s[...] = jnp.full_like(m_s, NEG)
        acc_s[...] = jnp.zeros_like(acc_s)
        p_buf[1] = jnp.zeros((T, T), MXU)
        a_buf[1] = jnp.ones((T, 1), F32)
        scores(0, 0)

        def pair(u, carry):
            t = 2 * u
            scores(t + 1, 1)
            softmax(0)
            pv(t - 1, 1)
            scores(t + 2, 0)
            softmax(1)
            pv(t, 0)
            return carry

        npairs = (i + 2) // 2
        lax.fori_loop(0, npairs, pair, 0)
        pv(2 * npairs - 1, 1)
        acc = acc_s[...]
        l = acc[:, V_HEAD:V_HEAD + 1]
        o_ref[...] = (acc[:, :V_HEAD] / l).astype(o_ref.dtype)
        lse_ref[...] = m_s[...] * ATT_SCALE + jnp.log(l)

    return pl.pallas_call(
        body, name=name, grid=(n,),
        in_specs=[pl.BlockSpec((T, QK_DIM), lambda i: (i, 0)), pl.BlockSpec((S, QK_DIM), lambda i: (0, 0)),
                  pl.BlockSpec((S, VX), lambda i: (0, 0))],
        out_specs=(pl.BlockSpec((T, V_HEAD), lambda i: (i, 0)), pl.BlockSpec((T, 1), lambda i: (i, 0))),
        out_shape=(jax.ShapeDtypeStruct((S, V_HEAD), MXU), jax.ShapeDtypeStruct((S, 1), F32)),
        scratch_shapes=[pltpu.VMEM((2, T, T), F32), pltpu.VMEM((2, T, T), MXU), pltpu.VMEM((2, T, 1), F32),
                        pltpu.VMEM((T, 1), F32), pltpu.VMEM((T, VX), F32)],
        compiler_params=_params(big=True),
    )(q, k, vx)


def _attn_delta(do, o, *, name):
    S = do.shape[0]
    tm = _rows(S, 1024)

    def body(do_ref, o_ref, d_ref):
        d_ref[...] = jnp.sum(do_ref[...].astype(F32) * o_ref[...].astype(F32), axis=1, keepdims=True)

    return pl.pallas_call(
        body, name=name, grid=(S // tm,), in_specs=[_row_spec(tm, V_HEAD), _row_spec(tm, V_HEAD)],
        out_specs=_row_spec(tm, 1), out_shape=jax.ShapeDtypeStruct((S, 1), F32), compiler_params=_params(),
    )(do, o)


BWD_T = 512


def _attn_bwd(q, k, v, do, lse2, delta, *, name):
    S = q.shape[0]
    T = _rows(S, BWD_T)
    n = S // T
    cpt = T // CHUNK

    def body(q_hbm, k_ref, v_ref, do_hbm, lse_ref, dl_ref, dq_hbm, dk_ref, dv_ref,
             q_res, do_res, dq_s, s_buf, dp_buf, p_buf, ds_buf, dk_s, dv_s):
        j = pl.program_id(0)

        @pl.when(j == 0)
        def _():
            pltpu.sync_copy(q_hbm, q_res)
            pltpu.sync_copy(do_hbm, do_res)
            dq_s[...] = jnp.zeros_like(dq_s)

        kc = lax.broadcasted_iota(jnp.int32, (T, T), 0) // CHUNK
        qc = lax.broadcasted_iota(jnp.int32, (T, T), 1) // CHUNK
        dchunk = kc - qc

        def tile(t):
            return jnp.clip(j + t, 0, n - 1)

        def rows(t):
            return pl.ds(pl.multiple_of(tile(t) * T, T), T)

        def scores(t, slot):
            visible_up_to = jnp.where(j + t < n, t * cpt, -2 * cpt)
            s = _dot_nt(k_ref[...], q_res[rows(t), :])
            s_buf[slot] = jnp.where(dchunk <= visible_up_to, s, NEG)
            dp_buf[slot] = _dot_nt(v_ref[...], do_res[rows(t), :])

        def probs(t, slot):
            pt = jnp.exp2(s_buf[slot] * EXP2_SCALE - lse_ref[tile(t)])
            p_buf[slot] = pt.astype(MXU)
            ds_buf[slot] = (pt * (dp_buf[slot] - dl_ref[tile(t)]) * ATT_SCALE).astype(MXU)

        def grads(t, slot):
            r = rows(t)
            dv_s[...] += jnp.dot(p_buf[slot], do_res[r, :], preferred_element_type=F32)
            ds = ds_buf[slot]
            dk_s[...] += jnp.dot(ds, q_res[r, :], preferred_element_type=F32)
            dq_s[r, :] += _dot_tn(ds, k_ref[...])

        dk_s[...] = jnp.zeros_like(dk_s)
        dv_s[...] = jnp.zeros_like(dv_s)
        p_buf[1] = jnp.zeros((T, T), MXU)
        ds_buf[1] = jnp.zeros((T, T), MXU)
        scores(0, 0)

        def pair(u, carry):
            t = 2 * u
            scores(t + 1, 1)
            probs(t, 0)
            grads(t - 1, 1)
            scores(t + 2, 0)
            probs(t + 1, 1)
            grads(t, 0)
            return carry

        npairs = (n - j + 1) // 2
        lax.fori_loop(0, npairs, pair, 0)
        grads(2 * npairs - 1, 1)
        dk_ref[...] = dk_s[...]
        dv_ref[...] = dv_s[...]

        @pl.when(j == n - 1)
        def _():
            pltpu.sync_copy(dq_s, dq_hbm)

    hbm = pl.BlockSpec(memory_space=pl.ANY)
    k_map = lambda j: (j, 0)
    whole = pl.BlockSpec((n, 1, T), lambda j: (0, 0, 0))
    return pl.pallas_call(
        body, name=name, grid=(n,),
        in_specs=[hbm, pl.BlockSpec((T, QK_DIM), k_map), pl.BlockSpec((T, V_HEAD), k_map), hbm, whole, whole],
        out_specs=(hbm, pl.BlockSpec((T, QK_DIM), k_map), pl.BlockSpec((T, V_HEAD), k_map)),
        out_shape=(jax.ShapeDtypeStruct((S, QK_DIM), F32), jax.ShapeDtypeStruct((S, QK_DIM), F32),
                   jax.ShapeDtypeStruct((S, V_HEAD), F32)),
        scratch_shapes=[pltpu.VMEM((S, QK_DIM), MXU), pltpu.VMEM((S, V_HEAD), MXU), pltpu.VMEM((S, QK_DIM), F32),
                        pltpu.VMEM((2, T, T), F32), pltpu.VMEM((2, T, T), F32), pltpu.VMEM((2, T, T), MXU),
                        pltpu.VMEM((2, T, T), MXU), pltpu.VMEM((T, QK_DIM), F32), pltpu.VMEM((T, V_HEAD), F32)],
        compiler_params=_params(big=True),
    )(q, k, v, do, lse2, delta)


HALO = 8
CONV_RH = 64


def _conv_tiles(S):
    tm = _rows(S, 256)
    tc = D_FF // 2
    return tm, tc, D_FF // tc


def _silu_parts(gate):
    sg = jax.nn.sigmoid(gate)
    return sg, gate * sg


def _convgate_fwd(up, cw, cb, *, name):
    S = up.shape[0]
    tm, tc, nc = _conv_tiles(S)
    hb = tm // HALO

    def body(v_ref, g_ref, hv_ref, hg_ref, wv_ref, wg_ref, bv_ref, bg_ref, o_ref, extv, extg):
        keep = (pl.program_id(0) > 0).astype(F32)
        for ext, t_ref, h_ref in ((extv, v_ref, hv_ref), (extg, g_ref, hg_ref)):
            ext[0:HALO, :] = h_ref[...] * keep
            ext[HALO:HALO + tm, :] = t_ref[...]

        def chunk(cc, carry):
            cols = pl.ds(pl.multiple_of(cc * LANE, LANE), LANE)
            wv, wg, bv, bg = wv_ref[:, cols], wg_ref[:, cols], bv_ref[:, cols], bg_ref[:, cols]
            for r0 in range(0, tm, CONV_RH):
                def conv(ext, w, b):
                    return (w[0:1] * ext[pl.ds(HALO - 2 + r0, CONV_RH), cols] + w[1:2] * ext[pl.ds(HALO - 1 + r0, CONV_RH), cols]
                            + w[2:3] * ext[pl.ds(HALO + r0, CONV_RH), cols] + b)
                val = conv(extv, wv, bv)
                gate = conv(extg, wg, bg)
                o_ref[pl.ds(r0, CONV_RH), cols] = (_silu_parts(gate)[1] * val).astype(o_ref.dtype)
            return carry

        lax.fori_loop(0, tc // LANE, chunk, 0)

    prev = lambda i: jnp.maximum(i * hb - 1, 0)
    return pl.pallas_call(
        body, name=name, grid=(S // tm, nc),
        in_specs=[pl.BlockSpec((tm, tc), lambda i, j: (i, j)), pl.BlockSpec((tm, tc), lambda i, j: (i, j + nc)),
                  pl.BlockSpec((HALO, tc), lambda i, j: (prev(i), j)),
                  pl.BlockSpec((HALO, tc), lambda i, j: (prev(i), j + nc)),
                  pl.BlockSpec((3, tc), lambda i, j: (0, j)), pl.BlockSpec((3, tc), lambda i, j: (0, j + nc)),
                  pl.BlockSpec((1, tc), lambda i, j: (0, j)), pl.BlockSpec((1, tc), lambda i, j: (0, j + nc))],
        out_specs=pl.BlockSpec((tm, tc), lambda i, j: (i, j)),
        out_shape=jax.ShapeDtypeStruct((S, D_FF), MXU),
        scratch_shapes=[pltpu.VMEM((tm + HALO, tc), F32), pltpu.VMEM((tm + HALO, tc), F32)],
        compiler_params=_params(),
    )(up, up, up, up, cw, cw, cb, cb)


def _convgate_bwd(up, dact, cw, cb, *, name):
    S = up.shape[0]
    tm, tc, nc = _conv_tiles(S)
    hb = tm // HALO
    nr = S // tm
    R = tm + HALO

    def body(v_ref, g_ref, pv_ref, pg_ref, nv_ref, ng_ref, da_ref, dan_ref, wv_ref, wg_ref, bv_ref, bg_ref,
             duv_ref, dug_ref, dwv_ref, dwg_ref, dbv_ref, dbg_ref, extv, extg, dae, dsv, dsg):
        i = pl.program_id(1)

        @pl.when(i == 0)
        def _():
            for r in (dwv_ref, dwg_ref, dbv_ref, dbg_ref):
                r[...] = jnp.zeros_like(r)

        keep_prev = (i > 0).astype(F32)
        keep_next = (i < nr - 1).astype(F32)
        for ext, t_ref, p_ref, n_ref in ((extv, v_ref, pv_ref, nv_ref), (extg, g_ref, pg_ref, ng_ref)):
            ext[0:HALO, :] = p_ref[...] * keep_prev
            ext[HALO:HALO + tm, :] = t_ref[...]
            ext[HALO + tm:2 * HALO + tm, :] = n_ref[...]
        dae[0:tm, :] = da_ref[...]
        dae[tm:R, :] = dan_ref[...] * keep_next

        def chunk(cc, carry):
            cols = pl.ds(pl.multiple_of(cc * LANE, LANE), LANE)
            wv, wg, bv, bg = wv_ref[:, cols], wg_ref[:, cols], bv_ref[:, cols], bg_ref[:, cols]
            for r0 in range(0, R, CONV_RH):
                rh = min(CONV_RH, R - r0)
                def conv(ext, w, b):
                    return (w[0:1] * ext[pl.ds(HALO - 2 + r0, rh), cols] + w[1:2] * ext[pl.ds(HALO - 1 + r0, rh), cols]
                            + w[2:3] * ext[pl.ds(HALO + r0, rh), cols] + b)
                val = conv(extv, wv, bv)
                gate = conv(extg, wg, bg)
                d = dae[pl.ds(r0, rh), cols]
                sg, silu = _silu_parts(gate)
                dsv[pl.ds(r0, rh), cols] = d * silu
                dsg[pl.ds(r0, rh), cols] = d * val * (sg * (1.0 + gate * (1.0 - sg)))
            for ds, ext, w, du_ref, dw_ref, db_ref in ((dsv, extv, wv, duv_ref, dwv_ref, dbv_ref),
                                                      (dsg, extg, wg, dug_ref, dwg_ref, dbg_ref)):
                acc = [jnp.zeros((1, LANE), F32) for _ in range(4)]
                for r0 in range(0, tm, CONV_RH):
                    d0 = ds[pl.ds(r0, CONV_RH), cols]
                    du_ref[pl.ds(r0, CONV_RH), cols] = (w[2:3] * d0 + w[1:2] * ds[pl.ds(r0 + 1, CONV_RH), cols]
                                                        + w[0:1] * ds[pl.ds(r0 + 2, CONV_RH), cols]).astype(du_ref.dtype)
                    for kk in range(3):
                        acc[kk] = acc[kk] + jnp.sum(d0 * ext[pl.ds(HALO - 2 + kk + r0, CONV_RH), cols], axis=0, keepdims=True)
                    acc[3] = acc[3] + jnp.sum(d0, axis=0, keepdims=True)
                for kk in range(3):
                    dw_ref[kk:kk + 1, cols] += acc[kk]
                db_ref[:, cols] += acc[3]
            return carry

        lax.fori_loop(0, tc // LANE, chunk, 0)

    prev = lambda i: jnp.maximum(i * hb - 1, 0)
    nxt = lambda i: jnp.minimum((i + 1) * hb, S // HALO - 1)
    tile_v = pl.BlockSpec((tm, tc), lambda j, i: (i, j))
    tile_g = pl.BlockSpec((tm, tc), lambda j, i: (i, j + nc))
    w_v = pl.BlockSpec((3, tc), lambda j, i: (0, j))
    w_g = pl.BlockSpec((3, tc), lambda j, i: (0, j + nc))
    b_v = pl.BlockSpec((1, tc), lambda j, i: (0, j))
    b_g = pl.BlockSpec((1, tc), lambda j, i: (0, j + nc))
    return pl.pallas_call(
        body, name=name, grid=(nc, nr),
        in_specs=[tile_v, tile_g,
                  pl.BlockSpec((HALO, tc), lambda j, i: (prev(i), j)), pl.BlockSpec((HALO, tc), lambda j, i: (prev(i), j + nc)),
                  pl.BlockSpec((HALO, tc), lambda j, i: (nxt(i), j)), pl.BlockSpec((HALO, tc), lambda j, i: (nxt(i), j + nc)),
                  tile_v, pl.BlockSpec((HALO, tc), lambda j, i: (nxt(i), j)), w_v, w_g, b_v, b_g],
        out_specs=(tile_v, tile_v, w_v, w_v, b_v, b_v),
        out_shape=(jax.ShapeDtypeStruct((S, D_FF), MXU), jax.ShapeDtypeStruct((S, D_FF), MXU),
                   jax.ShapeDtypeStruct((3, D_FF), F32), jax.ShapeDtypeStruct((3, D_FF), F32),
                   jax.ShapeDtypeStruct((1, D_FF), F32), jax.ShapeDtypeStruct((1, D_FF), F32)),
        scratch_shapes=[pltpu.VMEM((tm + 2 * HALO, tc), F32), pltpu.VMEM((tm + 2 * HALO, tc), F32),
                        pltpu.VMEM((R, tc), F32), pltpu.VMEM((R, tc), F32), pltpu.VMEM((R, tc), F32)],
        compiler_params=_params(),
    )(up, up, up, up, up, up, dact, dact, cw, cw, cb, cb)


def _glu_fwd(z, h, *, name):
    S = z.shape[0]
    tm = _rows(S, 512)

    def body(z_ref, h_ref, o_ref):
        o_ref[...] = h_ref[...] + z_ref[:, :D_MODEL] * jax.nn.sigmoid(z_ref[:, D_MODEL:])

    return pl.pallas_call(
        body, name=name, grid=(S // tm,), in_specs=[_row_spec(tm, 2 * D_MODEL), _row_spec(tm, D_MODEL)],
        out_specs=_row_spec(tm, D_MODEL), out_shape=jax.ShapeDtypeStruct((S, D_MODEL), F32),
        compiler_params=_params(),
    )(z, h)


def _glu_bwd(z, dm, *, name):
    S = z.shape[0]
    tm = _rows(S, 512)

    def body(z_ref, dm_ref, o_ref):
        sg = jax.nn.sigmoid(z_ref[:, D_MODEL:])
        dmv = dm_ref[...]
        o_ref[:, :D_MODEL] = (dmv * sg).astype(o_ref.dtype)
        o_ref[:, D_MODEL:] = (dmv * z_ref[:, :D_MODEL] * sg * (1.0 - sg)).astype(o_ref.dtype)

    return pl.pallas_call(
        body, name=name, grid=(S // tm,), in_specs=[_row_spec(tm, 2 * D_MODEL), _row_spec(tm, D_MODEL)],
        out_specs=_row_spec(tm, 2 * D_MODEL), out_shape=jax.ShapeDtypeStruct((S, 2 * D_MODEL), MXU),
        compiler_params=_params(),
    )(z, dm)


GELU_C = math.sqrt(2.0 / math.pi)
GELU_A = 0.044715


def _gelu(y):
    return 0.5 * y * (1.0 + jnp.tanh(GELU_C * (y + GELU_A * (y * y * y))))


def _gelu_bwd(y, dg, *, name):
    S = y.shape[0]
    tm = _rows(S, 512)

    def body(y_ref, dg_ref, o_ref):
        yv = y_ref[...]
        t = jnp.tanh(GELU_C * (yv + GELU_A * (yv * yv * yv)))
        d = 0.5 * (1.0 + t) + 0.5 * yv * (1.0 - t * t) * (GELU_C * (1.0 + 3.0 * GELU_A * (yv * yv)))
        o_ref[...] = dg_ref[...] * d

    return pl.pallas_call(
        body, name=name, grid=(S // tm,), in_specs=[_row_spec(tm, D_MODEL), _row_spec(tm, D_MODEL)],
        out_specs=_row_spec(tm, D_MODEL), out_shape=jax.ShapeDtypeStruct((S, D_MODEL), F32),
        compiler_params=_params(),
    )(y, dg)


SW = NSTATE // SBLK
ST2 = 2 * NSTATE


def _s5_fwd(u, wb, wc, abc, dskip, x0, *, full, name):
    S = u.shape[0]
    T = _rows(S, 256)
    nb = S // T
    nj = T // NSEG

    def body(u_ref, wb_ref, wc_ref, a_ref, d_ref, x0_ref, *rest):
        if full:
            xs_ref, y_ref, yg_ref, st = rest
        else:
            e_ref, xs_ref, st = rest
        i = pl.program_id(0)

        @pl.when(i == 0)
        def _():
            st[...] = x0_ref[...]

        uv = u_ref[...]
        ub = uv.astype(MXU)
        for kb in range(SBLK):
            r = jnp.dot(ub[:, kb * LANE:(kb + 1) * LANE], wb_ref[kb], preferred_element_type=F32)
            xs_ref[:, kb * SW:(kb + 1) * SW] = r[:, :SW]
            xs_ref[:, NSTATE + kb * SW:NSTATE + (kb + 1) * SW] = r[:, SW:]
        ar = a_ref[:, :NSTATE]
        ai = a_ref[:, NSTATE:]

        def step(j, c):
            xr, xi = c
            rows = pl.ds(pl.multiple_of(j * NSEG, NSEG), NSEG)
            br = xs_ref[rows, pl.ds(0, NSTATE)]
            bi = xs_ref[rows, pl.ds(NSTATE, NSTATE)]
            nr = ar * xr - ai * xi + br
            ni = ar * xi + ai * xr + bi
            xs_ref[rows, pl.ds(0, NSTATE)] = nr
            xs_ref[rows, pl.ds(NSTATE, NSTATE)] = ni
            return nr, ni

        xr, xi = lax.fori_loop(0, nj, step, (st[:, :NSTATE], st[:, NSTATE:]))
        st[:, :NSTATE] = xr
        st[:, NSTATE:] = xi
        if full:
            for kb in range(SBLK):
                yk = (jnp.dot(xs_ref[:, kb * SW:(kb + 1) * SW].astype(MXU), wc_ref[kb, :SW, :], preferred_element_type=F32)
                      + jnp.dot(xs_ref[:, NSTATE + kb * SW:NSTATE + (kb + 1) * SW].astype(MXU), wc_ref[kb, SW:, :],
                                preferred_element_type=F32))
                cols = slice(kb * LANE, (kb + 1) * LANE)
                yk = yk + d_ref[:, cols] * uv[:, cols]
                y_ref[:, cols] = yk
                yg_ref[:, cols] = _gelu(yk).astype(yg_ref.dtype)
        else:
            @pl.when(i == nb - 1)
            def _():
                e_ref[...] = st[...]

    in_specs = [_row_spec(T, D_MODEL), pl.BlockSpec((SBLK, LANE, 2 * SW), lambda i: (0, 0, 0)),
                pl.BlockSpec((SBLK, 2 * SW, LANE), lambda i: (0, 0, 0)), _const_spec(NSEG, ST2),
                _const_spec(1, D_MODEL), _const_spec(NSEG, ST2)]
    if full:
        out_specs = (_row_spec(T, ST2), _row_spec(T, D_MODEL), _row_spec(T, D_MODEL))
        out_shape = (jax.ShapeDtypeStruct((S, ST2), F32), jax.ShapeDtypeStruct((S, D_MODEL), F32),
                     jax.ShapeDtypeStruct((S, D_MODEL), MXU))
        scratch = [pltpu.VMEM((NSEG, ST2), F32)]
    else:
        out_specs = _const_spec(NSEG, ST2)
        out_shape = jax.ShapeDtypeStruct((NSEG, ST2), F32)
        scratch = [pltpu.VMEM((T, ST2), F32), pltpu.VMEM((NSEG, ST2), F32)]
    return pl.pallas_call(
        body, name=name, grid=(nb,), in_specs=in_specs, out_specs=out_specs, out_shape=out_shape,
        scratch_shapes=scratch, compiler_params=_params(big=True),
    )(u, wb, wc, abc, dskip, x0)


def _s5_bwd(dy, xs, u, wct, wbt, abc, dskip, x0, l0, *, full, name):
    S = dy.shape[0]
    T = _rows(S, 128)
    nb = S // T
    nj = T // NSEG
    blk = lambda i: nb - 1 - i

    def body(dy_ref, *rest):
        if full:
            (xs_ref, xh_ref, u_ref, wct_ref, wbt_ref, a_ref, d_ref, x0_ref, l0_ref,
             du_ref, da_ref, dwb_ref, dwc_ref, dd_ref, g_s, lam_s) = rest
        else:
            wct_ref, a_ref, l0_ref, f_ref, g_s, lam_s = rest
        i = pl.program_id(0)

        @pl.when(i == 0)
        def _():
            lam_s[...] = l0_ref[...]
            if full:
                for r in (da_ref, dwb_ref, dwc_ref, dd_ref):
                    r[...] = jnp.zeros_like(r)

        dyv = dy_ref[...]
        dyb = dyv.astype(MXU)
        for kb in range(SBLK):
            r = jnp.dot(dyb[:, kb * LANE:(kb + 1) * LANE], wct_ref[kb], preferred_element_type=F32)
            g_s[:, kb * SW:(kb + 1) * SW] = r[:, :SW]
            g_s[:, NSTATE + kb * SW:NSTATE + (kb + 1) * SW] = r[:, SW:]
        ar = a_ref[:, :NSTATE]
        ai = a_ref[:, NSTATE:]

        def advance(row, lam):
            lr, li = lam
            rows = pl.ds(row, NSEG)
            nr = g_s[rows, pl.ds(0, NSTATE)] + ar * lr + ai * li
            ni = g_s[rows, pl.ds(NSTATE, NSTATE)] - ai * lr + ar * li
            g_s[rows, pl.ds(0, NSTATE)] = nr
            g_s[rows, pl.ds(NSTATE, NSTATE)] = ni
            return nr, ni

        def accumulate(lam, xpr, xpi):
            nr, ni = lam
            da_ref[:, :NSTATE] += nr * xpr + ni * xpi
            da_ref[:, NSTATE:] += ni * xpr - nr * xpi

        def step(jj, lam):
            row = pl.multiple_of((nj - 1 - jj) * NSEG, NSEG)
            lam = advance(row, lam)
            if full:
                prow = pl.ds(pl.multiple_of(row - NSEG, NSEG), NSEG)
                accumulate(lam, xs_ref[prow, pl.ds(0, NSTATE)], xs_ref[prow, pl.ds(NSTATE, NSTATE)])
            return lam

        lam = lax.fori_loop(0, nj - 1, step, (lam_s[:, :NSTATE], lam_s[:, NSTATE:]))
        lam = advance(0, lam)
        if full:
            first = (blk(i) == 0)
            xp = jnp.where(first, x0_ref[...], xh_ref[...])
            accumulate(lam, xp[:, :NSTATE], xp[:, NSTATE:])
        lam_s[:, :NSTATE] = lam[0]
        lam_s[:, NSTATE:] = lam[1]
        if full:
            uv = u_ref[...]
            ub = uv.astype(MXU)
            dd_ref[...] += jnp.sum(dyv * uv, axis=0, keepdims=True)
            for kb in range(SBLK):
                cols = slice(kb * LANE, (kb + 1) * LANE)
                re = slice(kb * SW, (kb + 1) * SW)
                im = slice(NSTATE + kb * SW, NSTATE + (kb + 1) * SW)
                lr_b = g_s[:, re].astype(MXU)
                li_b = g_s[:, im].astype(MXU)
                duk = (jnp.dot(lr_b, wbt_ref[kb, :SW, :], preferred_element_type=F32)
                       + jnp.dot(li_b, wbt_ref[kb, SW:, :], preferred_element_type=F32))
                du_ref[:, cols] = duk + d_ref[:, cols] * dyv[:, cols]
                dwb_ref[kb, :, :SW] += _dot_tn(ub[:, cols], lr_b)
                dwb_ref[kb, :, SW:] += _dot_tn(ub[:, cols], li_b)
                dwc_ref[kb, :SW, :] += _dot_tn(xs_ref[:, re].astype(MXU), dyb[:, cols])
                dwc_ref[kb, SW:, :] += _dot_tn(xs_ref[:, im].astype(MXU), dyb[:, cols])
        else:
            @pl.when(i == nb - 1)
            def _():
                f_ref[...] = lam_s[...]

    rev = lambda c: pl.BlockSpec((T, c), lambda i: (blk(i), 0))
    w3 = lambda a, b: pl.BlockSpec((SBLK, a, b), lambda i: (0, 0, 0))
    if full:
        hb = T // NSEG
        in_specs = [rev(D_MODEL), rev(ST2),
                    pl.BlockSpec((NSEG, ST2), lambda i: (jnp.maximum(blk(i) * hb - 1, 0), 0)),
                    rev(D_MODEL), w3(LANE, 2 * SW), w3(2 * SW, LANE), _const_spec(NSEG, ST2),
                    _const_spec(1, D_MODEL), _const_spec(NSEG, ST2), _const_spec(NSEG, ST2)]
        ops = [dy, xs, xs, u, wct, wbt, abc, dskip, x0, l0]
        out_specs = (rev(D_MODEL), _const_spec(NSEG, ST2), w3(LANE, 2 * SW), w3(2 * SW, LANE),
                     _const_spec(1, D_MODEL))
        out_shape = (jax.ShapeDtypeStruct((S, D_MODEL), F32), jax.ShapeDtypeStruct((NSEG, ST2), F32),
                     jax.ShapeDtypeStruct((SBLK, LANE, 2 * SW), F32), jax.ShapeDtypeStruct((SBLK, 2 * SW, LANE), F32),
                     jax.ShapeDtypeStruct((1, D_MODEL), F32))
    else:
        in_specs = [rev(D_MODEL), w3(LANE, 2 * SW), _const_spec(NSEG, ST2), _const_spec(NSEG, ST2)]
        ops = [dy, wct, abc, l0]
        out_specs = _const_spec(NSEG, ST2)
        out_shape = jax.ShapeDtypeStruct((NSEG, ST2), F32)
    return pl.pallas_call(
        body, name=name, grid=(nb,), in_specs=in_specs, out_specs=out_specs, out_shape=out_shape,
        scratch_shapes=[pltpu.VMEM((T, ST2), F32), pltpu.VMEM((NSEG, ST2), F32)],
        compiler_params=_params(big=True),
    )(*ops)


def _s5_discretize(lr, li, log_dt, br, bi):
    dt = jnp.exp(log_dt)[:, None]
    mag = jnp.exp(lr * dt)
    ar = mag * jnp.cos(li * dt)
    ai = mag * jnp.sin(li * dt)
    den = lr * lr + li * li
    nr = ar - 1.0
    coef_r = (nr * lr + ai * li) / den
    coef_i = (ai * lr - nr * li) / den
    bbar_r = coef_r[..., None] * br - coef_i[..., None] * bi
    bbar_i = coef_r[..., None] * bi + coef_i[..., None] * br
    return ar, ai, bbar_r, bbar_i


def _blockdiag(m):
    gpb = SSM_GROUPS // SBLK
    a, b = m.shape[1:]
    mb = m.reshape(SBLK, gpb, a, b)
    eye = jnp.eye(gpb, dtype=m.dtype)
    return jnp.einsum('kgab,gh->kgahb', mb, eye).reshape(SBLK, gpb * a, gpb * b)


def _blockdiag_extract(w, a, b):
    gpb = SSM_GROUPS // SBLK
    w5 = w.reshape(SBLK, gpb, a, gpb, b)
    return jnp.einsum('kgahb,gh->kgab', w5, jnp.eye(gpb, dtype=w.dtype)).reshape(SSM_GROUPS, a, b)


def _cpow(ar, ai, n):
    rr, ri = jnp.ones_like(ar), jnp.zeros_like(ai)
    br, bi = ar, ai
    while n:
        if n & 1:
            rr, ri = rr * br - ri * bi, rr * bi + ri * br
        br, bi = br * br - bi * bi, 2.0 * br * bi
        n >>= 1
    return rr, ri


def _perm(a):
    s, c = a.shape
    return a.reshape(NSEG, s // NSEG, c).transpose(1, 0, 2).reshape(s, c)


def _unperm(a):
    s, c = a.shape
    return a.reshape(s // NSEG, NSEG, c).transpose(1, 0, 2).reshape(s, c)


def _other_chips(x, y):
    return [(1 - x, y), (x, 1 - y), (1 - x, 1 - y)]


def _span(chip, size, align):
    return pl.ds(pl.multiple_of(chip * size, align), size)


def _chip_exchange(srcs, out_shapes, pieces, *, name):
    ns, no, npc = len(srcs), len(out_shapes), len(pieces)

    def body(*refs):
        s_refs, o_refs = refs[:ns], refs[ns:ns + no]
        send_sems, recv_sems, local_sems = refs[ns + no:]
        x, y, c = lax.axis_index("x"), lax.axis_index("y"), lax.axis_index("c")
        me = 2 * x + y

        def remote(k, p, tx, ty, src_chip, dst_chip):
            si, oi, sv, dv = pieces[p]
            return pltpu.make_async_remote_copy(
                src_ref=sv(s_refs[si], src_chip), dst_ref=dv(o_refs[oi], dst_chip), send_sem=send_sems.at[k, p],
                recv_sem=recv_sems.at[k, p], device_id=(tx, ty, c), device_id_type=MESH)

        mine = [pltpu.make_async_copy(sv(s_refs[si], me), dv(o_refs[oi], me), local_sems.at[p])
                for p, (si, oi, sv, dv) in enumerate(pieces)]
        for cp in mine:
            cp.start()
        others = _other_chips(x, y)
        sends = [remote(k, p, tx, ty, 2 * tx + ty, me) for k, (tx, ty) in enumerate(others) for p in range(npc)]
        for cp in sends:
            cp.start()
        for k, (tx, ty) in enumerate(others):
            for p in range(npc):
                remote(k, p, tx, ty, me, 2 * tx + ty).wait_recv()
        for cp in sends:
            cp.wait_send()
        for cp in mine:
            cp.wait()

    hbm = pl.BlockSpec(memory_space=pl.ANY)
    return pl.pallas_call(
        body, name=name, in_specs=[hbm] * ns, out_specs=tuple([hbm] * no), out_shape=tuple(out_shapes),
        scratch_shapes=[pltpu.SemaphoreType.DMA((NCHIP - 1, npc)), pltpu.SemaphoreType.DMA((NCHIP - 1, npc)),
                        pltpu.SemaphoreType.DMA((npc,))],
    )(*srcs)


def _sibling_exchange(srcs, *, name):
    n = len(srcs)

    def body(*refs):
        s_refs, o_refs, send_sems, recv_sems = refs[:n], refs[n:2 * n], refs[2 * n], refs[2 * n + 1]
        x, y, c = lax.axis_index("x"), lax.axis_index("y"), lax.axis_index("c")
        cps = [pltpu.make_async_remote_copy(src_ref=s_refs[p], dst_ref=o_refs[p], send_sem=send_sems.at[p],
                                            recv_sem=recv_sems.at[p], device_id=(x, y, 1 - c), device_id_type=MESH)
               for p in range(n)]
        for cp in cps:
            cp.start()
        for cp in cps:
            cp.wait()

    hbm = pl.BlockSpec(memory_space=pl.ANY)
    return pl.pallas_call(
        body, name=name, in_specs=[hbm] * n, out_specs=tuple([hbm] * n),
        out_shape=tuple(jax.ShapeDtypeStruct(s.shape, s.dtype) for s in srcs),
        scratch_shapes=[pltpu.SemaphoreType.DMA((n,)), pltpu.SemaphoreType.DMA((n,))],
    )(*srcs)


def _row_tile(r, c, tile_bytes):
    best = 16
    for t in range(16, r + 1, 16):
        if r % t == 0 and t * c * 4 <= tile_bytes:
            best = t
    assert r % best == 0
    return best


def _sum_chips(r, *, name):
    _, R, W = r.shape
    tm = _row_tile(R, W, 2 * 1024 * 1024)

    def body(r_ref, o_ref):
        o_ref[...] = ((r_ref[0].astype(F32) + r_ref[1].astype(F32)) + r_ref[2].astype(F32)) + r_ref[3].astype(F32)

    return pl.pallas_call(
        body, name=name, grid=(R // tm,), in_specs=[pl.BlockSpec((NCHIP, tm, W), lambda i: (0, i, 0))],
        out_specs=_row_spec(tm, W), out_shape=jax.ShapeDtypeStruct((R, W), F32), compiler_params=_params(),
    )(r)


def _adamw(p_mine, p_sib, w, m, v, *, name):
    R, W = w.shape
    tm = _row_tile(R, W, 1024 * 1024)

    def body(a_ref, b_ref, w_ref, m_ref, v_ref, g_ref, d_ref, nm_ref, nv_ref):
        g = a_ref[...] + b_ref[...]
        mm = ADAM_B1 * m_ref[...] + (1.0 - ADAM_B1) * g
        vv = ADAM_B2 * v_ref[...] + (1.0 - ADAM_B2) * (g * g)
        m_hat = mm / (1.0 - ADAM_B1 ** ADAM_STEP)
        v_hat = vv / (1.0 - ADAM_B2 ** ADAM_STEP)
        g_ref[...] = g
        d_ref[...] = -ADAM_LR * (m_hat / (jnp.sqrt(v_hat) + ADAM_EPS) + ADAM_WD * w_ref[...])
        nm_ref[...] = mm
        nv_ref[...] = vv

    spec = _row_spec(tm, W)
    shp = jax.ShapeDtypeStruct((R, W), F32)
    return pl.pallas_call(
        body, name=name, grid=(R // tm,), in_specs=[spec] * 5, out_specs=(spec,) * 4, out_shape=(shp,) * 4,
        compiler_params=_params(),
    )(p_mine, p_sib, w, m, v)


def _pack_small(parts):
    flat = jnp.concatenate([p.reshape(-1) for p in parts])
    pad = (-flat.shape[0]) % (SMALL_ROWS * PACKW)
    return jnp.pad(flat, (0, pad)).reshape(-1, PACKW)


def _unpack_small(buf, shapes):
    flat, out, off = buf.reshape(-1), [], 0
    for shp in shapes:
        sz = math.prod(shp)
        out.append(flat[off:off + sz].reshape(shp))
        off += sz
    return out


def _shard(a, t, ax):
    sz = a.shape[ax] // NCHIP
    return lax.slice_in_dim(a, t * sz, (t + 1) * sz, axis=ax)


def _gather_weights(w):
    srcs, outs, pieces = [], [], []

    def add(name):
        local = w[name].astype(MXU)
        local = local[0] if local.shape[0] == 1 else local
        ax = SHARD_AXIS[name] - (1 if w[name].shape[0] == 1 else 0)
        full = local.shape[:ax] + (NCHIP * local.shape[ax],) + local.shape[ax + 1:]
        si, oi = len(srcs), len(outs)
        srcs.append(local)
        outs.append(jax.ShapeDtypeStruct(full, MXU))
        size = local.shape[ax]
        if local.ndim == 2:
            if ax == 0:
                pieces.append((si, oi, lambda r, t: r, lambda r, ch, size=size: r.at[_span(ch, size, 8), :]))
            else:
                pieces.append((si, oi, lambda r, t: r, lambda r, ch, size=size: r.at[:, _span(ch, size, LANE)]))
        else:
            for l in range(local.shape[0]):
                if ax == 1:
                    dv = lambda r, ch, l=l, size=size: r.at[l, _span(ch, size, 8), :]
                else:
                    dv = lambda r, ch, l=l, size=size: r.at[l, :, _span(ch, size, LANE)]
                pieces.append((si, oi, lambda r, t, l=l: r.at[l], dv))

    for name in GATHER_MXU:
        add(name)
    small = _pack_small([w[n] for n in GATHER_F32])
    srcs.append(small)
    outs.append(jax.ShapeDtypeStruct((NCHIP,) + small.shape, F32))
    pieces.append((len(srcs) - 1, len(outs) - 1, lambda r, t: r, lambda r, ch: r.at[ch]))
    got = _chip_exchange(srcs, outs, pieces, name="gather_weights")
    full = dict(zip(GATHER_MXU, got[:-1]))
    per_chip = [_unpack_small(got[-1][t], [w[n].shape for n in GATHER_F32]) for t in range(NCHIP)]
    for j, n in enumerate(GATHER_F32):
        full[n] = jnp.concatenate([per_chip[t][j] for t in range(NCHIP)], axis=SHARD_AXIS[n])
    return full


BIG = GATHER_MXU
SMALL = [n for n in WNAMES if n not in BIG]


def _as_rows(a):
    return a.reshape(-1, a.shape[-1])


def _reduce_and_update(grads, w, mom, var):
    srcs, outs, pieces = [], [], []
    for name in BIG:
        local = w[name]
        ax = SHARD_AXIS[name]
        size = local.shape[ax]
        oi = len(outs)
        layers = grads[name] if isinstance(grads[name], list) else [grads[name]]
        outs.append(jax.ShapeDtypeStruct((NCHIP,) + local.shape, layers[0].dtype))
        for l, g in enumerate(layers):
            si = len(srcs)
            srcs.append(g)
            if ax == 1:
                sv = lambda r, t, size=size: r.at[_span(t, size, 8), :]
            else:
                sv = lambda r, t, size=size: r.at[:, _span(t, size, LANE)]
            pieces.append((si, oi, sv, lambda r, ch, l=l: r.at[ch, l]))
    small = jnp.stack([_pack_small([_shard(grads[n], t, SHARD_AXIS[n]) if n in SHARD_AXIS else grads[n] for n in SMALL])
                       for t in range(NCHIP)])
    srcs.append(small)
    outs.append(jax.ShapeDtypeStruct(small.shape, F32))
    pieces.append((len(srcs) - 1, len(outs) - 1, lambda r, t: r.at[t], lambda r, ch: r.at[ch]))
    landed = _chip_exchange(srcs, outs, pieces, name="grad_exchange")
    partial = [_sum_chips(r.reshape(NCHIP, -1, r.shape[-1]), name="grad_sum_chips") for r in landed]
    sibling = _sibling_exchange(partial, name="grad_sibling")
    res = [dict(), dict(), dict(), dict()]
    for j, name in enumerate(BIG):
        outs4 = _adamw(partial[j], sibling[j], _as_rows(w[name]), _as_rows(mom[name]), _as_rows(var[name]), name="adamw")
        for d, o in zip(res, outs4):
            d[name] = o.reshape(w[name].shape)
    outs4 = _adamw(partial[-1], sibling[-1], *[_pack_small([t[n] for n in SMALL]) for t in (w, mom, var)], name="adamw")
    for d, o in zip(res, outs4):
        d.update(zip(SMALL, _unpack_small(o, [w[n].shape for n in SMALL])))
    return res


def kernel(x, positions, mla_w_a, mla_g_q, mla_g_kv, mla_w_uq, mla_w_ukv, mla_w_o, ssm_w_in, ssm_lambda_re, ssm_lambda_im, ssm_log_dt, ssm_b_re, ssm_b_im, ssm_c_re, ssm_c_im, ssm_d, ssm_w_glu, ffn_w_up, ffn_conv_w, ffn_conv_b, ffn_w_down, g_mix, g_ffn, g_final, loss_target, m_mla_w_a, m_mla_g_q, m_mla_g_kv, m_mla_w_uq, m_mla_w_ukv, m_mla_w_o, m_ssm_w_in, m_ssm_lambda_re, m_ssm_lambda_im, m_ssm_log_dt, m_ssm_b_re, m_ssm_b_im, m_ssm_c_re, m_ssm_c_im, m_ssm_d, m_ssm_w_glu, m_ffn_w_up, m_ffn_conv_w, m_ffn_conv_b, m_ffn_w_down, m_g_mix, m_g_ffn, m_g_final, v_mla_w_a, v_mla_g_q, v_mla_g_kv, v_mla_w_uq, v_mla_w_ukv, v_mla_w_o, v_ssm_w_in, v_ssm_lambda_re, v_ssm_lambda_im, v_ssm_log_dt, v_ssm_b_re, v_ssm_b_im, v_ssm_c_re, v_ssm_c_im, v_ssm_d, v_ssm_w_glu, v_ffn_w_up, v_ffn_conv_w, v_ffn_conv_b, v_ffn_w_down, v_g_mix, v_g_ffn, v_g_final):
    w = dict(zip(WNAMES, (mla_w_a, mla_g_q, mla_g_kv, mla_w_uq, mla_w_ukv, mla_w_o, ssm_w_in, ssm_lambda_re,
                          ssm_lambda_im, ssm_log_dt, ssm_b_re, ssm_b_im, ssm_c_re, ssm_c_im, ssm_d, ssm_w_glu,
                          ffn_w_up, ffn_conv_w, ffn_conv_b, ffn_w_down, g_mix, g_ffn, g_final)))
    mom = dict(zip(WNAMES, (m_mla_w_a, m_mla_g_q, m_mla_g_kv, m_mla_w_uq, m_mla_w_ukv, m_mla_w_o, m_ssm_w_in,
                            m_ssm_lambda_re, m_ssm_lambda_im, m_ssm_log_dt, m_ssm_b_re, m_ssm_b_im, m_ssm_c_re,
                            m_ssm_c_im, m_ssm_d, m_ssm_w_glu, m_ffn_w_up, m_ffn_conv_w, m_ffn_conv_b,
                            m_ffn_w_down, m_g_mix, m_g_ffn, m_g_final)))
    var = dict(zip(WNAMES, (v_mla_w_a, v_mla_g_q, v_mla_g_kv, v_mla_w_uq, v_mla_w_ukv, v_mla_w_o, v_ssm_w_in,
                            v_ssm_lambda_re, v_ssm_lambda_im, v_ssm_log_dt, v_ssm_b_re, v_ssm_b_im, v_ssm_c_re,
                            v_ssm_c_im, v_ssm_d, v_ssm_w_glu, v_ffn_w_up, v_ffn_conv_w, v_ffn_conv_b,
                            v_ffn_w_down, v_g_mix, v_g_ffn, v_g_final)))
    S = x.shape[1]
    D = D_MODEL
    x2 = x.reshape(S, D)
    tgt = loss_target.reshape(S, D)

    fw = _gather_weights(w)
    w_a = jnp.pad(fw['mla_w_a'], ((0, 0), (0, A_PAD - KR0 - QK_ROPE)))
    uq = fw['mla_w_uq'].reshape(Q_LORA, HEADS, QK_DIM)
    w_uq = jnp.concatenate([uq[:, :, :QK_NOPE].reshape(Q_LORA, HEADS * QK_NOPE),
                            jnp.pad(uq[:, :, QK_NOPE:], ((0, 0), (0, 0), (0, LANE - QK_ROPE))).reshape(Q_LORA, HEADS * LANE)],
                           axis=1)
    w_ukv = fw['mla_w_ukv']
    w_o = fw['mla_w_o']
    w_in = fw['ssm_w_in']
    w_glu = fw['ssm_w_glu']
    w_up = fw['ffn_w_up']
    w_down = fw['ffn_w_down']
    conv_w = fw['ffn_conv_w']
    dskip = fw['ssm_d']
    conv_b = w['ffn_conv_b']
    g_q, g_kv = w['mla_g_q'], w['mla_g_kv']
    gm, gf = w['g_mix'], w['g_ffn']
    gfin = w['g_final'].reshape(1, D)

    inv = 1.0 / (ROPE_THETA ** (jnp.arange(0, QK_ROPE, 2, dtype=F32) / QK_ROPE))
    ang = positions.reshape(S).astype(F32)[:, None] * inv
    cos, sin = jnp.cos(ang), jnp.sin(ang)
    zpad = jnp.zeros((S, LANE - QK_ROPE), F32)
    c128 = jnp.concatenate([cos, cos, zpad], axis=1)
    s128 = jnp.concatenate([-sin, sin, zpad], axis=1)

    hn0 = _rmsnorm_fwd(x2, gm[0:1], name="rms_mix0")
    a = _mm(hn0, w_a, name="mla_a")
    cqn, ckvn, kr = _mla_mid_fwd(a, g_q, g_kv, c128, s128, name="mla_mid_fwd")
    qfull = _mm(cqn, w_uq, name="mla_q")
    kv = _mm(ckvn, w_ukv, out_dtype=MXU, name="mla_kv")
    qs, ks, vs = _qk_prep(qfull, kv, kr, c128, s128, name="qk_prep")
    os_, lses = [], []
    for h in range(HEADS):
        o_h, lse_h = _attn_fwd(qs[h], ks[h], vs[h], name="attn_fwd")
        os_.append(o_h)
        lses.append(lse_h)
    o_cat = jnp.concatenate(os_, axis=1)
    h1 = _mm(o_cat, w_o, res=x2, name="mla_o")

    def ffn_fwd(h, l):
        hn = _rmsnorm_fwd(h, gf[l:l + 1], name="rms_ffn")
        up = _mm(hn, w_up[l], name="ffn_up")
        act = _convgate_fwd(up, conv_w[l], conv_b[l:l + 1], name="convgate_fwd")
        return _mm(act, w_down[l], res=h, name="ffn_down"), (hn, up, act)

    h2, saved0 = ffn_fwd(h1, 0)

    lam_re, lam_im, log_dt = w['ssm_lambda_re'][0], w['ssm_lambda_im'][0], w['ssm_log_dt'][0]
    (a_re, a_im, bbar_r, bbar_i), disc_vjp = jax.vjp(_s5_discretize, lam_re, lam_im, log_dt, w['ssm_b_re'][0],
                                                    w['ssm_b_im'][0])
    c_re, c_im = w['ssm_c_re'][0], w['ssm_c_im'][0]
    bt_r, bt_i = jnp.swapaxes(bbar_r, 1, 2), jnp.swapaxes(bbar_i, 1, 2)
    wb = jnp.concatenate([_blockdiag(bt_r), _blockdiag(bt_i)], axis=2).astype(MXU)
    wbt = jnp.concatenate([_blockdiag(bbar_r), _blockdiag(bbar_i)], axis=1).astype(MXU)
    ct_r, ct_i = jnp.swapaxes(c_re, 1, 2), jnp.swapaxes(c_im, 1, 2)
    wc = jnp.concatenate([_blockdiag(ct_r), _blockdiag(-ct_i)], axis=1).astype(MXU)
    wct = jnp.concatenate([_blockdiag(c_re), _blockdiag(-c_im)], axis=2).astype(MXU)
    af_r, af_i = a_re.reshape(NSTATE), a_im.reshape(NSTATE)
    abc = jnp.broadcast_to(jnp.concatenate([af_r, af_i])[None], (NSEG, ST2))
    seg = S // NSEG
    ap_r, ap_i = _cpow(af_r, af_i, seg)

    hn1 = _rmsnorm_fwd(h2, gm[1:2], name="rms_mix1")
    u = _mm(hn1, w_in, name="s5_in")
    u_p = _perm(u)
    zero_state = jnp.zeros((NSEG, ST2), F32)
    ends = _s5_fwd(u_p, wb, wc, abc, dskip, zero_state, full=False, name="s5_fwd_ends")
    inits, cr, ci = [], jnp.zeros((NSTATE,), F32), jnp.zeros((NSTATE,), F32)
    for r in range(NSEG):
        inits.append(jnp.concatenate([cr, ci]))
        er, ei = ends[r, :NSTATE], ends[r, NSTATE:]
        cr, ci = er + ap_r * cr - ap_i * ci, ei + ap_r * ci + ap_i * cr
    x0 = jnp.stack(inits)
    xs, y_p, yg_p = _s5_fwd(u_p, wb, wc, abc, dskip, x0, full=True, name="s5_fwd")
    yg = _unperm(yg_p)
    z = _mm(yg, w_glu, name="s5_glu")
    h3 = _glu_fwd(z, h2, name="glu_fwd")
    h4, saved1 = ffn_fwd(h3, 1)

    loss_l, dh4, dg_final = _loss_head(h4, gfin, tgt, name="loss_head")

    grads = {}

    def ffn_bwd(h_in, g, saved, l):
        hn, up, act = saved
        w_up_t = w_up[l].T
        dact = _mm(g, w_down[l].T, name="ffn_down_dx")
        dw_down = _mm(act, g, mode="tn", out_dtype=MXU, name="ffn_down_dw")
        duv, dug, dwv, dwg, dbv, dbg = _convgate_bwd(up, dact, conv_w[l], conv_b[l:l + 1], name="convgate_bwd")
        dw_up = jnp.concatenate([_mm(hn, duv, mode="tn", out_dtype=MXU, name="ffn_up_dw"), _mm(hn, dug, mode="tn", out_dtype=MXU, name="ffn_up_dw")],
                                axis=1)
        dhn = _mm(duv, w_up_t[:D_FF], name="ffn_up_dx")
        dhn = _mm(dug, w_up_t[D_FF:], res=dhn, name="ffn_up_dx_acc")
        dh, dg = _rmsnorm_bwd(h_in, gf[l:l + 1], dhn, g, name="rms_ffn_bwd")
        return dh, dict(w_up=dw_up, w_down=dw_down, conv_w=jnp.concatenate([dwv, dwg], axis=1),
                        conv_b=jnp.concatenate([dbv, dbg], axis=1)[0], g_ffn=dg[0])

    dh3, fg1 = ffn_bwd(h3, dh4, saved1, 1)

    dz = _glu_bwd(z, dh3, name="glu_bwd")
    grads['ssm_w_glu'] = _mm(yg, dz, mode="tn", out_dtype=MXU, name="s5_glu_dw")
    dyg = _mm(dz, w_glu.T, name="s5_glu_dx")
    dy_p = _gelu_bwd(y_p, _perm(dyg), name="gelu_bwd")
    firsts = _s5_bwd(dy_p, None, None, wct, None, abc, None, None, zero_state, full=False, name="s5_bwd_firsts")
    linits, cr, ci = [None] * NSEG, jnp.zeros((NSTATE,), F32), jnp.zeros((NSTATE,), F32)
    for r in reversed(range(NSEG)):
        linits[r] = jnp.concatenate([cr, ci])
        fr, fi = firsts[r, :NSTATE], firsts[r, NSTATE:]
        cr, ci = fr + ap_r * cr + ap_i * ci, fi + ap_r * ci - ap_i * cr
    l0 = jnp.stack(linits)
    du_p, dab, dwb, dwc, dd = _s5_bwd(dy_p, xs, u_p, wct, wbt, abc, dskip, x0, l0, full=True, name="s5_bwd")
    du = _unperm(du_p)
    grads['ssm_w_in'] = _mm(hn1, du, mode="tn", out_dtype=MXU, name="s5_in_dw")
    dhn1 = _mm(du, w_in.T, name="s5_in_dx")
    dh2, dg_mix1 = _rmsnorm_bwd(h2, gm[1:2], dhn1, dh3, name="rms_mix_bwd")
    da_sum = jnp.sum(dab, axis=0)
    dbt_r = _blockdiag_extract(dwb[:, :, :SW], SSM_GROUP, SSM_STATE)
    dbt_i = _blockdiag_extract(dwb[:, :, SW:], SSM_GROUP, SSM_STATE)
    dlr, dli, dlog_dt, dbr, dbi = disc_vjp((da_sum[:NSTATE].reshape(SSM_GROUPS, SSM_STATE),
                                            da_sum[NSTATE:].reshape(SSM_GROUPS, SSM_STATE),
                                            jnp.swapaxes(dbt_r, 1, 2), jnp.swapaxes(dbt_i, 1, 2)))
    dct_r = _blockdiag_extract(dwc[:, :SW, :], SSM_STATE, SSM_GROUP)
    dct_i = _blockdiag_extract(dwc[:, SW:, :], SSM_STATE, SSM_GROUP)
    grads['ssm_lambda_re'], grads['ssm_lambda_im'], grads['ssm_log_dt'] = dlr[None], dli[None], dlog_dt[None]
    grads['ssm_b_re'], grads['ssm_b_im'] = dbr[None], dbi[None]
    grads['ssm_c_re'] = jnp.swapaxes(dct_r, 1, 2)[None]
    grads['ssm_c_im'] = -jnp.swapaxes(dct_i, 1, 2)[None]
    grads['ssm_d'] = dd

    dh1, fg0 = ffn_bwd(h1, dh2, saved0, 0)
    grads['ffn_w_up'] = [fg0['w_up'], fg1['w_up']]
    grads['ffn_w_down'] = [fg0['w_down'], fg1['w_down']]
    grads['ffn_conv_w'] = jnp.stack([fg0['conv_w'], fg1['conv_w']])
    grads['ffn_conv_b'] = jnp.stack([fg0['conv_b'], fg1['conv_b']])
    grads['g_ffn'] = jnp.stack([fg0['g_ffn'], fg1['g_ffn']])

    do_cat = _mm(dh1, w_o.T, out_dtype=MXU, name="mla_o_dx")
    grads['mla_w_o'] = _mm(o_cat, dh1, mode="tn", out_dtype=MXU, name="mla_o_dw")
    dqs, dks, dvs = [], [], []
    for h in range(HEADS):
        do_h = do_cat[:, h * V_HEAD:(h + 1) * V_HEAD]
        delta = _attn_delta(do_h, os_[h], name="attn_delta")
        tiles = (S // _rows(S, BWD_T), 1, _rows(S, BWD_T))
        dq_h, dk_h, dv_h = _attn_bwd(qs[h], ks[h], vs[h], do_h, (lses[h] * math.log2(math.e)).reshape(tiles),
                                     delta.reshape(tiles), name="attn_bwd")
        dqs.append(dq_h)
        dks.append(dk_h)
        dvs.append(dv_h)
    dqfull, dkv, dkr = _qk_prep_bwd(dqs, dks, dvs, c128, s128, name="qk_prep_bwd")
    dw_uq_p = _mm(cqn, dqfull, mode="tn", out_dtype=MXU, name="mla_q_dw")
    dcqn = _mm(dqfull, w_uq.T, name="mla_q_dx")
    grads['mla_w_ukv'] = _mm(ckvn, dkv, mode="tn", out_dtype=MXU, name="mla_kv_dw")
    dckvn = _mm(dkv, w_ukv.T, name="mla_kv_dx")
    da, dgq, dgkv = _mla_mid_bwd(a, dcqn, dckvn, dkr, g_q, g_kv, c128, s128, name="mla_mid_bwd")
    grads['mla_w_a'] = _mm(hn0, da, mode="tn", out_dtype=MXU, name="mla_a_dw")[:, :KR0 + QK_ROPE]
    dhn0 = _mm(da, w_a.T, name="mla_a_dx")
    dx, dg_mix0 = _rmsnorm_bwd(x2, gm[0:1], dhn0, dh1, name="rms_mix_bwd")
    grads['mla_w_uq'] = jnp.concatenate(
        [dw_uq_p[:, :HEADS * QK_NOPE].reshape(Q_LORA, HEADS, QK_NOPE),
         dw_uq_p[:, HEADS * QK_NOPE:].reshape(Q_LORA, HEADS, LANE)[:, :, :QK_ROPE]], axis=2).reshape(Q_LORA, HEADS * QK_DIM)
    grads['mla_g_q'], grads['mla_g_kv'] = dgq, dgkv
    grads['g_mix'] = jnp.concatenate([dg_mix0, dg_mix1], axis=0)
    grads['g_final'] = dg_final[0]

    g_out, d_out, m_out, v_out = _reduce_and_update(grads, w, mom, var)

    loss = lax.psum(loss_l[0, 0], ("x", "y", "c"))
    return (loss, dx.reshape(1, S, D), *[g_out[n] for n in WNAMES], *[d_out[n] for n in WNAMES],
            *[m_out[n] for n in WNAMES], *[v_out[n] for n in WNAMES])
```

```python
import math

import jax
import jax.numpy as jnp
from jax import lax
from jax.experimental import pallas as pl
from jax.experimental.pallas import tpu as pltpu

F32 = jnp.float32
MXU = jnp.bfloat16

D_MODEL = 1024
CHUNK = 64
EPS = 1e-6
HEADS = 8
QK_NOPE = 128
QK_ROPE = 64
V_HEAD = 128
Q_LORA = 384
KV_LORA = 256
ROPE_THETA = 10000.0
QK_DIM = QK_NOPE + QK_ROPE
SSM_GROUP = 16
SSM_GROUPS = D_MODEL // SSM_GROUP
SSM_STATE = 64
NSTATE = SSM_GROUPS * SSM_STATE
D_FF = 2816
ATT_SCALE = QK_DIM ** -0.5
EXP2_SCALE = ATT_SCALE * math.log2(math.e)
NEG = -1e30
NSEG = 8
SBLK = 8

ADAM_LR = 0.001
ADAM_B1 = 0.9
ADAM_B2 = 0.999
ADAM_EPS = 1e-08
ADAM_WD = 0.01
ADAM_STEP = 10

LANE = 128
TN_MAX_COLS = 2816
NN_MAX_COLS = 1408
TN_ACC_ELEMS = 1536 * 1024
VMEM_BIG = 56 * 1024 * 1024

WNAMES = ['mla_w_a', 'mla_g_q', 'mla_g_kv', 'mla_w_uq', 'mla_w_ukv', 'mla_w_o', 'ssm_w_in', 'ssm_lambda_re',
          'ssm_lambda_im', 'ssm_log_dt', 'ssm_b_re', 'ssm_b_im', 'ssm_c_re', 'ssm_c_im', 'ssm_d', 'ssm_w_glu',
          'ffn_w_up', 'ffn_conv_w', 'ffn_conv_b', 'ffn_w_down', 'g_mix', 'g_ffn', 'g_final']
FWD_NAMES = ['x', 'positions'] + WNAMES
SHARD_AXIS = {'mla_w_a': 1, 'mla_w_uq': 2, 'mla_w_ukv': 2, 'mla_w_o': 1, 'ssm_w_in': 1, 'ssm_d': 1,
              'ssm_w_glu': 2, 'ffn_w_up': 2, 'ffn_conv_w': 2, 'ffn_w_down': 1}
GATHER_F32 = ['ssm_d', 'ffn_conv_w']
NCHIP = 4
PACKW = 1024
SMALL_ROWS = 64
MESH = pl.DeviceIdType.MESH


def _tile(d, pref):
    t = min(pref, d) // LANE * LANE
    while t >= LANE:
        if d % t == 0:
            return t
        t -= LANE
    return d


def _rows(s, pref):
    t = min(s, pref)
    assert s % t == 0 and t % 8 == 0
    return t


def _params(big=False):
    if big:
        return pltpu.CompilerParams(vmem_limit_bytes=VMEM_BIG)
    return pltpu.CompilerParams(vmem_limit_bytes=40 * 1024 * 1024)


def _mm(a, b, *, name, mode="nn", out_dtype=F32, res=None, tm=None, tn=None, tk=None):
    if mode == "nn":
        M, K = a.shape
        tm = tm or _rows(M, 1024)
    else:
        K, M = a.shape
        tn = tn or _tile(b.shape[1], TN_MAX_COLS)
        tm = tm or _tile(M, max(LANE, TN_ACC_ELEMS // tn))
    N = b.shape[1]
    assert b.shape[0] == K
    tn = tn or _tile(N, NN_MAX_COLS if N % 512 else 512)
    tk = tk or (_tile(K, 1408) if mode == "nn" else _rows(K, 512))
    nk = K // tk
    has_res = res is not None

    def body(a_ref, b_ref, *rest):
        if has_res:
            r_ref, o_ref, acc = rest
        else:
            o_ref, acc = rest
        k = pl.program_id(2)

        @pl.when(k == 0)
        def _():
            acc[...] = jnp.zeros_like(acc)

        av = a_ref[...].astype(MXU)
        bv = b_ref[...].astype(MXU)
        if mode == "nn":
            acc[...] += jnp.dot(av, bv, preferred_element_type=F32)
        else:
            acc[...] += lax.dot_general(av, bv, (((0,), (0,)), ((), ())), preferred_element_type=F32)

        @pl.when(k == nk - 1)
        def _():
            o = acc[...]
            if has_res:
                o = o + r_ref[...]
            o_ref[...] = o.astype(o_ref.dtype)

    if mode == "nn":
        a_spec = pl.BlockSpec((tm, tk), lambda i, j, k: (i, k))
    else:
        a_spec = pl.BlockSpec((tk, tm), lambda i, j, k: (k, i))
    in_specs = [a_spec, pl.BlockSpec((tk, tn), lambda i, j, k: (k, j))]
    ops = [a, b]
    if has_res:
        in_specs.append(pl.BlockSpec((tm, tn), lambda i, j, k: (i, j)))
        ops.append(res)
    return pl.pallas_call(
        body, name=name, grid=(M // tm, N // tn, nk), in_specs=in_specs,
        out_specs=pl.BlockSpec((tm, tn), lambda i, j, k: (i, j)),
        out_shape=jax.ShapeDtypeStruct((M, N), out_dtype),
        scratch_shapes=[pltpu.VMEM((tm, tn), F32)], compiler_params=_params(),
    )(*ops)


def _row_spec(tm, c):
    return pl.BlockSpec((tm, c), lambda i: (i, 0))


def _const_spec(r, c):
    return pl.BlockSpec((r, c), lambda i: (0, 0))


def _rms_parts(xv):
    r = lax.rsqrt(jnp.mean(xv * xv, axis=-1, keepdims=True) + EPS)
    return r, xv * r


def _rms_vjp(xv, gv, dyv):
    r, xhat = _rms_parts(xv)
    gy = dyv * gv
    dx = r * (gy - xhat * jnp.mean(gy * xhat, axis=-1, keepdims=True))
    return dx, dyv * xhat


def _rmsnorm_fwd(x, g, *, name):
    S, D = x.shape
    tm = _rows(S, 512)

    def body(x_ref, g_ref, o_ref):
        _, xhat = _rms_parts(x_ref[...])
        o_ref[...] = (xhat * g_ref[...]).astype(o_ref.dtype)

    return pl.pallas_call(
        body, name=name, grid=(S // tm,), in_specs=[_row_spec(tm, D), _const_spec(1, D)],
        out_specs=_row_spec(tm, D), out_shape=jax.ShapeDtypeStruct((S, D), MXU), compiler_params=_params(),
    )(x, g)


def _rmsnorm_bwd(x, g, dy, dres, *, name):
    S, D = x.shape
    tm = _rows(S, 512)

    def body(x_ref, g_ref, dy_ref, dr_ref, dx_ref, dg_ref):
        @pl.when(pl.program_id(0) == 0)
        def _():
            dg_ref[...] = jnp.zeros_like(dg_ref)

        dx, dgp = _rms_vjp(x_ref[...], g_ref[...], dy_ref[...])
        dx_ref[...] = dr_ref[...] + dx
        dg_ref[...] += jnp.sum(dgp, axis=0, keepdims=True)

    return pl.pallas_call(
        body, name=name, grid=(S // tm,),
        in_specs=[_row_spec(tm, D), _const_spec(1, D), _row_spec(tm, D), _row_spec(tm, D)],
        out_specs=(_row_spec(tm, D), _const_spec(1, D)),
        out_shape=(jax.ShapeDtypeStruct((S, D), F32), jax.ShapeDtypeStruct((1, D), F32)),
        compiler_params=_params(),
    )(x, g, dy, dres)


def _loss_head(h, g, tgt, *, name):
    S, D = h.shape
    tm = _rows(S, 512)

    def body(h_ref, g_ref, t_ref, l_ref, dh_ref, dg_ref):
        @pl.when(pl.program_id(0) == 0)
        def _():
            l_ref[...] = jnp.zeros_like(l_ref)
            dg_ref[...] = jnp.zeros_like(dg_ref)

        hv = h_ref[...]
        gv = g_ref[...]
        _, xhat = _rms_parts(hv)
        e = xhat * gv - t_ref[...]
        l_ref[...] += 0.5 * jnp.sum(jnp.mean(e * e, axis=-1, keepdims=True), axis=0, keepdims=True)
        dx, dgp = _rms_vjp(hv, gv, e * (1.0 / D))
        dh_ref[...] = dx
        dg_ref[...] += jnp.sum(dgp, axis=0, keepdims=True)

    return pl.pallas_call(
        body, name=name, grid=(S // tm,),
        in_specs=[_row_spec(tm, D), _const_spec(1, D), _row_spec(tm, D)],
        out_specs=(_const_spec(1, 1), _row_spec(tm, D), _const_spec(1, D)),
        out_shape=(jax.ShapeDtypeStruct((1, 1), F32), jax.ShapeDtypeStruct((S, D), F32),
                   jax.ShapeDtypeStruct((1, D), F32)),
        compiler_params=_params(),
    )(h, g, tgt)


def _swap_halves(g):
    lane = lax.broadcasted_iota(jnp.int32, g.shape, 1)
    return jnp.where(lane < QK_ROPE // 2, pltpu.roll(g, LANE - QK_ROPE // 2, axis=1),
                     pltpu.roll(g, QK_ROPE // 2, axis=1))


def _rope128(g, c128, s128):
    return g * c128 + _swap_halves(g) * s128


def _rope128_vjp(dy, c128, s128):
    lane = lax.broadcasted_iota(jnp.int32, dy.shape, 1)
    return jnp.where(lane < QK_ROPE, dy * c128 + _swap_halves(dy * s128), 0.0)


A_PAD = 768
KR0 = Q_LORA + KV_LORA


def _mla_mid_fwd(a, g_q, g_kv, c128, s128, *, name):
    S = a.shape[0]
    tm = _rows(S, 512)

    def body(a_ref, gq_ref, gkv_ref, c_ref, s_ref, cq_ref, ckv_ref, kr_ref):
        av = a_ref[...]
        _, qh = _rms_parts(av[:, :Q_LORA])
        cq_ref[...] = (qh * gq_ref[...]).astype(cq_ref.dtype)
        _, kh = _rms_parts(av[:, Q_LORA:KR0])
        ckv_ref[...] = (kh * gkv_ref[...]).astype(ckv_ref.dtype)
        kr = _rope128(av[:, KR0:A_PAD], c_ref[...], s_ref[...])
        kr_ref[...] = kr[:, :QK_ROPE].astype(kr_ref.dtype)

    return pl.pallas_call(
        body, name=name, grid=(S // tm,),
        in_specs=[_row_spec(tm, A_PAD), _const_spec(1, Q_LORA), _const_spec(1, KV_LORA), _row_spec(tm, LANE),
                  _row_spec(tm, LANE)],
        out_specs=(_row_spec(tm, Q_LORA), _row_spec(tm, KV_LORA), _row_spec(tm, QK_ROPE)),
        out_shape=(jax.ShapeDtypeStruct((S, Q_LORA), MXU), jax.ShapeDtypeStruct((S, KV_LORA), MXU),
                   jax.ShapeDtypeStruct((S, QK_ROPE), MXU)),
        compiler_params=_params(),
    )(a, g_q, g_kv, c128, s128)


def _mla_mid_bwd(a, dcq, dckv, dkr, g_q, g_kv, c128, s128, *, name):
    S = a.shape[0]
    tm = _rows(S, 512)

    def body(a_ref, dcq_ref, dckv_ref, dkr_ref, gq_ref, gkv_ref, c_ref, s_ref, da_ref, dgq_ref, dgkv_ref):
        @pl.when(pl.program_id(0) == 0)
        def _():
            dgq_ref[...] = jnp.zeros_like(dgq_ref)
            dgkv_ref[...] = jnp.zeros_like(dgkv_ref)

        av = a_ref[...]
        dx, dgp = _rms_vjp(av[:, :Q_LORA], gq_ref[...], dcq_ref[...])
        da_ref[:, :Q_LORA] = dx.astype(da_ref.dtype)
        dgq_ref[...] += jnp.sum(dgp, axis=0, keepdims=True)
        dx, dgp = _rms_vjp(av[:, Q_LORA:KR0], gkv_ref[...], dckv_ref[...])
        da_ref[:, Q_LORA:KR0] = dx.astype(da_ref.dtype)
        dgkv_ref[...] += jnp.sum(dgp, axis=0, keepdims=True)
        da_ref[:, KR0:A_PAD] = _rope128_vjp(dkr_ref[...], c_ref[...], s_ref[...]).astype(da_ref.dtype)

    return pl.pallas_call(
        body, name=name, grid=(S // tm,),
        in_specs=[_row_spec(tm, A_PAD), _row_spec(tm, Q_LORA), _row_spec(tm, KV_LORA), _row_spec(tm, LANE),
                  _const_spec(1, Q_LORA), _const_spec(1, KV_LORA), _row_spec(tm, LANE), _row_spec(tm, LANE)],
        out_specs=(_row_spec(tm, A_PAD), _const_spec(1, Q_LORA), _const_spec(1, KV_LORA)),
        out_shape=(jax.ShapeDtypeStruct((S, A_PAD), MXU), jax.ShapeDtypeStruct((1, Q_LORA), F32),
                   jax.ShapeDtypeStruct((1, KV_LORA), F32)),
        compiler_params=_params(),
    )(a, dcq, dckv, dkr, g_q, g_kv, c128, s128)


QF = 2 * HEADS * LANE
KVF = HEADS * (QK_NOPE + V_HEAD)
VX = 2 * V_HEAD


def _qk_prep(qfull, kv, kr, c128, s128, *, name):
    S = qfull.shape[0]
    tm = _rows(S, 256)

    def body(q_ref, kv_ref, kr_ref, c_ref, s_ref, *outs):
        qo, ko, vo = outs[:HEADS], outs[HEADS:2 * HEADS], outs[2 * HEADS:]
        cv, sv = c_ref[...], s_ref[...]
        krv = kr_ref[...]
        for h in range(HEADS):
            qo[h][:, :QK_NOPE] = q_ref[:, h * LANE:(h + 1) * LANE].astype(MXU)
            g = q_ref[:, (HEADS + h) * LANE:(HEADS + h + 1) * LANE]
            qo[h][:, QK_NOPE:] = _rope128(g, cv, sv)[:, :QK_ROPE].astype(MXU)
            ko[h][:, :QK_NOPE] = kv_ref[:, 2 * h * LANE:(2 * h + 1) * LANE]
            ko[h][:, QK_NOPE:] = krv
            vo[h][:, :V_HEAD] = kv_ref[:, (2 * h + 1) * LANE:(2 * h + 2) * LANE]
            vo[h][:, V_HEAD:] = jnp.ones((tm, VX - V_HEAD), MXU)

    shapes = ([jax.ShapeDtypeStruct((S, QK_DIM), MXU)] * (2 * HEADS)
              + [jax.ShapeDtypeStruct((S, VX), MXU)] * HEADS)
    specs = [_row_spec(tm, QK_DIM)] * (2 * HEADS) + [_row_spec(tm, VX)] * HEADS
    outs = pl.pallas_call(
        body, name=name, grid=(S // tm,),
        in_specs=[_row_spec(tm, QF), _row_spec(tm, KVF), _row_spec(tm, QK_ROPE), _row_spec(tm, LANE),
                  _row_spec(tm, LANE)],
        out_specs=tuple(specs), out_shape=tuple(shapes), compiler_params=_params(),
    )(qfull, kv, kr, c128, s128)
    return outs[:HEADS], outs[HEADS:2 * HEADS], outs[2 * HEADS:]


def _qk_prep_bwd(dqs, dks, dvs, c128, s128, *, name):
    S = dqs[0].shape[0]
    tm = _rows(S, 256)

    def body(*refs):
        dq = refs[:HEADS]
        dk = refs[HEADS:2 * HEADS]
        dv = refs[2 * HEADS:3 * HEADS]
        c_ref, s_ref, dqf_ref, dkv_ref, dkr_ref, tmp = refs[3 * HEADS:]
        cv, sv = c_ref[...], s_ref[...]
        tmp[...] = jnp.zeros_like(tmp)
        dkr_ref[...] = jnp.zeros_like(dkr_ref)
        for h in range(HEADS):
            dqf_ref[:, h * LANE:(h + 1) * LANE] = dq[h][:, :QK_NOPE].astype(MXU)
            tmp[:, :QK_ROPE] = dq[h][:, QK_NOPE:]
            dqf_ref[:, (HEADS + h) * LANE:(HEADS + h + 1) * LANE] = _rope128_vjp(tmp[...], cv, sv).astype(MXU)
            dkv_ref[:, 2 * h * LANE:(2 * h + 1) * LANE] = dk[h][:, :QK_NOPE].astype(MXU)
            dkv_ref[:, (2 * h + 1) * LANE:(2 * h + 2) * LANE] = dv[h][...].astype(MXU)
            dkr_ref[:, :QK_ROPE] += dk[h][:, QK_NOPE:]

    return pl.pallas_call(
        body, name=name, grid=(S // tm,),
        in_specs=[_row_spec(tm, QK_DIM)] * (2 * HEADS) + [_row_spec(tm, V_HEAD)] * HEADS
        + [_row_spec(tm, LANE), _row_spec(tm, LANE)],
        out_specs=(_row_spec(tm, QF), _row_spec(tm, KVF), _row_spec(tm, LANE)),
        out_shape=(jax.ShapeDtypeStruct((S, QF), MXU), jax.ShapeDtypeStruct((S, KVF), MXU),
                   jax.ShapeDtypeStruct((S, LANE), F32)),
        scratch_shapes=[pltpu.VMEM((tm, LANE), F32)], compiler_params=_params(),
    )(*dqs, *dks, *dvs, c128, s128)


def _dot_nt(a, b):
    return lax.dot_general(a, b, (((1,), (1,)), ((), ())), preferred_element_type=F32)


def _dot_tn(a, b):
    return lax.dot_general(a, b, (((0,), (0,)), ((), ())), preferred_element_type=F32)


def _attn_fwd(q, k, vx, *, name, ride=None):
    S = q.shape[0]
    T = _rows(S, 1024)
    n = S // T
    cpt = T // CHUNK

    r_srcs, r_outs, r_pieces = ride or ((), (), ())
    ns, no = len(r_srcs), len(r_outs)

    def body(q_ref, k_ref, v_ref, *rest):
        o_ref, lse_ref = rest[ns:ns + 2]
        s_buf, p_buf, a_buf, m_s, acc_s = rest[ns + 2 + no:ns + 7 + no]
        i = pl.program_id(0)
        if ride:
            start, finish = _exchange_ops(r_pieces, rest[:ns], rest[ns + 2:ns + 2 + no], *rest[ns + 7 + no:])
            pl.when(i == 0)(start)
        qc = lax.broadcasted_iota(jnp.int32, (T, T), 0) // CHUNK
        kc = lax.broadcasted_iota(jnp.int32, (T, T), 1) // CHUNK
        dchunk = kc - qc

        def tile_rows(b):
            return pl.ds(pl.multiple_of(jnp.clip(b, 0, n - 1) * T, T), T)

        def scores(b, slot):
            s = _dot_nt(q_ref[...], k_ref[tile_rows(b), :])
            s_buf[slot] = jnp.where(dchunk <= (i - b) * cpt, s, NEG)

        def softmax(slot):
            s = s_buf[slot]
            m_prev = m_s[...]
            m_new = jnp.maximum(m_prev, jnp.max(s, axis=1, keepdims=True))
            a_buf[slot] = jnp.exp2((m_prev - m_new) * EXP2_SCALE)
            p_buf[slot] = jnp.exp2((s - m_new) * EXP2_SCALE).astype(MXU)
            m_s[...] = m_new

        def pv(b, slot):
            acc_s[...] = a_buf[slot] * acc_s[...] + jnp.dot(p_buf[slot], v_ref[tile_rows(b), :],
                                                              preferred_element_type=F32)

        m_s[...] = jnp.full_like(m_s, NEG)
        acc_s[...] = jnp.zeros_like(acc_s)
        p_buf[1] = jnp.zeros((T, T), MXU)
        a_buf[1] = jnp.ones((T, 1), F32)
        scores(0, 0)

        def pair(u, carry):
            t = 2 * u
            scores(t + 1, 1)
            softmax(0)
            pv(t - 1, 1)
            scores(t + 2, 0)
            softmax(1)
            pv(t, 0)
            return carry

        npairs = (i + 2) // 2
        lax.fori_loop(0, npairs, pair, 0)
        pv(2 * npairs - 1, 1)
        acc = acc_s[...]
        l = acc[:, V_HEAD:V_HEAD + 1]
        o_ref[...] = (acc[:, :V_HEAD] / l).astype(o_ref.dtype)
        lse_ref[...] = m_s[...] * ATT_SCALE + jnp.log(l)
        if ride:
            pl.when(i == n - 1)(finish)

    hbm = pl.BlockSpec(memory_space=pl.ANY)
    return pl.pallas_call(
        body, name=name, grid=(n,),
        in_specs=[pl.BlockSpec((T, QK_DIM), lambda i: (i, 0)), pl.BlockSpec((S, QK_DIM), lambda i: (0, 0)),
                  pl.BlockSpec((S, VX), lambda i: (0, 0))] + [hbm] * ns,
        out_specs=(pl.BlockSpec((T, V_HEAD), lambda i: (i, 0)), pl.BlockSpec((T, 1), lambda i: (i, 0))) + (hbm,) * no,
        out_shape=(jax.ShapeDtypeStruct((S, V_HEAD), MXU), jax.ShapeDtypeStruct((S, 1), F32)) + tuple(r_outs),
        scratch_shapes=[pltpu.VMEM((2, T, T), F32), pltpu.VMEM((2, T, T), MXU), pltpu.VMEM((2, T, 1), F32),
                        pltpu.VMEM((T, 1), F32), pltpu.VMEM((T, VX), F32)]
        + (_exchange_scratch(len(r_pieces)) if ride else []),
        compiler_params=_params(big=True),
    )(q, k, vx, *r_srcs)


def _attn_delta(do, o, *, name):
    S = do.shape[0]
    tm = _rows(S, 1024)

    def body(do_ref, o_ref, d_ref):
        d_ref[...] = jnp.sum(do_ref[...].astype(F32) * o_ref[...].astype(F32), axis=1, keepdims=True)

    return pl.pallas_call(
        body, name=name, grid=(S // tm,), in_specs=[_row_spec(tm, V_HEAD), _row_spec(tm, V_HEAD)],
        out_specs=_row_spec(tm, 1), out_shape=jax.ShapeDtypeStruct((S, 1), F32), compiler_params=_params(),
    )(do, o)


BWD_T = 512


def _attn_bwd(q, k, v, do, lse2, delta, *, name, ride=None):
    S = q.shape[0]
    T = _rows(S, BWD_T)
    n = S // T
    cpt = T // CHUNK

    r_srcs, r_outs, r_pieces = ride or ((), (), ())
    ns, no = len(r_srcs), len(r_outs)

    def body(q_hbm, k_ref, v_ref, do_hbm, lse_ref, dl_ref, *rest):
        dq_hbm, dk_ref, dv_ref = rest[ns:ns + 3]
        q_res, do_res, dq_s, s_buf, dp_buf, p_buf, ds_buf, dk_s, dv_s = rest[ns + 3 + no:ns + 12 + no]
        j = pl.program_id(0)
        if ride:
            start, finish = _exchange_ops(r_pieces, rest[:ns], rest[ns + 3:ns + 3 + no], *rest[ns + 12 + no:])
            pl.when(j == 0)(start)

        @pl.when(j == 0)
        def _():
            pltpu.sync_copy(q_hbm, q_res)
            pltpu.sync_copy(do_hbm, do_res)
            dq_s[...] = jnp.zeros_like(dq_s)

        kc = lax.broadcasted_iota(jnp.int32, (T, T), 0) // CHUNK
        qc = lax.broadcasted_iota(jnp.int32, (T, T), 1) // CHUNK
        dchunk = kc - qc

        def tile(t):
            return jnp.clip(j + t, 0, n - 1)

        def rows(t):
            return pl.ds(pl.multiple_of(tile(t) * T, T), T)

        def scores(t, slot):
            visible_up_to = jnp.where(j + t < n, t * cpt, -2 * cpt)
            s = _dot_nt(k_ref[...], q_res[rows(t), :])
            s_buf[slot] = jnp.where(dchunk <= visible_up_to, s, NEG)
            dp_buf[slot] = _dot_nt(v_ref[...], do_res[rows(t), :])

        def probs(t, slot):
            pt = jnp.exp2(s_buf[slot] * EXP2_SCALE - lse_ref[tile(t)])
            p_buf[slot] = pt.astype(MXU)
            ds_buf[slot] = (pt * (dp_buf[slot] - dl_ref[tile(t)]) * ATT_SCALE).astype(MXU)

        def grads(t, slot):
            r = rows(t)
            dv_s[...] += jnp.dot(p_buf[slot], do_res[r, :], preferred_element_type=F32)
            ds = ds_buf[slot]
            dk_s[...] += jnp.dot(ds, q_res[r, :], preferred_element_type=F32)
            dq_s[r, :] += _dot_tn(ds, k_ref[...])

        dk_s[...] = jnp.zeros_like(dk_s)
        dv_s[...] = jnp.zeros_like(dv_s)
        p_buf[1] = jnp.zeros((T, T), MXU)
        ds_buf[1] = jnp.zeros((T, T), MXU)
        scores(0, 0)

        def pair(u, carry):
            t = 2 * u
            scores(t + 1, 1)
            probs(t, 0)
            grads(t - 1, 1)
            scores(t + 2, 0)
            probs(t + 1, 1)
            grads(t, 0)
            return carry

        npairs = (n - j + 1) // 2
        lax.fori_loop(0, npairs, pair, 0)
        grads(2 * npairs - 1, 1)
        dk_ref[...] = dk_s[...]
        dv_ref[...] = dv_s[...]

        @pl.when(j == n - 1)
        def _():
            pltpu.sync_copy(dq_s, dq_hbm)
            if ride:
                finish()

    hbm = pl.BlockSpec(memory_space=pl.ANY)
    k_map = lambda j: (j, 0)
    whole = pl.BlockSpec((n, 1, T), lambda j: (0, 0, 0))
    return pl.pallas_call(
        body, name=name, grid=(n,),
        in_specs=[hbm, pl.BlockSpec((T, QK_DIM), k_map), pl.BlockSpec((T, V_HEAD), k_map), hbm, whole, whole]
        + [hbm] * ns,
        out_specs=(hbm, pl.BlockSpec((T, QK_DIM), k_map), pl.BlockSpec((T, V_HEAD), k_map)) + (hbm,) * no,
        out_shape=(jax.ShapeDtypeStruct((S, QK_DIM), F32), jax.ShapeDtypeStruct((S, QK_DIM), F32),
                   jax.ShapeDtypeStruct((S, V_HEAD), F32)) + tuple(r_outs),
        scratch_shapes=[pltpu.VMEM((S, QK_DIM), MXU), pltpu.VMEM((S, V_HEAD), MXU), pltpu.VMEM((S, QK_DIM), F32),
                        pltpu.VMEM((2, T, T), F32), pltpu.VMEM((2, T, T), F32), pltpu.VMEM((2, T, T), MXU),
                        pltpu.VMEM((2, T, T), MXU), pltpu.VMEM((T, QK_DIM), F32), pltpu.VMEM((T, V_HEAD), F32)]
        + (_exchange_scratch(len(r_pieces)) if ride else []),
        compiler_params=_params(big=True),
    )(q, k, v, do, lse2, delta, *r_srcs)


HALO = 8
CONV_RH = 64


def _conv_tiles(S):
    tm = _rows(S, 256)
    tc = D_FF // 2
    return tm, tc, D_FF // tc


def _silu_parts(gate):
    sg = jax.nn.sigmoid(gate)
    return sg, gate * sg


def _convgate_fwd(up, cw, cb, *, name):
    S = up.shape[0]
    tm, tc, nc = _conv_tiles(S)
    hb = tm // HALO

    def body(v_ref, g_ref, hv_ref, hg_ref, wv_ref, wg_ref, bv_ref, bg_ref, o_ref, extv, extg):
        keep = (pl.program_id(0) > 0).astype(F32)
        for ext, t_ref, h_ref in ((extv, v_ref, hv_ref), (extg, g_ref, hg_ref)):
            ext[0:HALO, :] = h_ref[...] * keep
            ext[HALO:HALO + tm, :] = t_ref[...]

        def chunk(cc, carry):
            cols = pl.ds(pl.multiple_of(cc * LANE, LANE), LANE)
            wv, wg, bv, bg = wv_ref[:, cols], wg_ref[:, cols], bv_ref[:, cols], bg_ref[:, cols]
            for r0 in range(0, tm, CONV_RH):
                def conv(ext, w, b):
                    return (w[0:1] * ext[pl.ds(HALO - 2 + r0, CONV_RH), cols] + w[1:2] * ext[pl.ds(HALO - 1 + r0, CONV_RH), cols]
                            + w[2:3] * ext[pl.ds(HALO + r0, CONV_RH), cols] + b)
                val = conv(extv, wv, bv)
                gate = conv(extg, wg, bg)
                o_ref[pl.ds(r0, CONV_RH), cols] = (_silu_parts(gate)[1] * val).astype(o_ref.dtype)
            return carry

        lax.fori_loop(0, tc // LANE, chunk, 0)

    prev = lambda i: jnp.maximum(i * hb - 1, 0)
    return pl.pallas_call(
        body, name=name, grid=(S // tm, nc),
        in_specs=[pl.BlockSpec((tm, tc), lambda i, j: (i, j)), pl.BlockSpec((tm, tc), lambda i, j: (i, j + nc)),
                  pl.BlockSpec((HALO, tc), lambda i, j: (prev(i), j)),
                  pl.BlockSpec((HALO, tc), lambda i, j: (prev(i), j + nc)),
                  pl.BlockSpec((3, tc), lambda i, j: (0, j)), pl.BlockSpec((3, tc), lambda i, j: (0, j + nc)),
                  pl.BlockSpec((1, tc), lambda i, j: (0, j)), pl.BlockSpec((1, tc), lambda i, j: (0, j + nc))],
        out_specs=pl.BlockSpec((tm, tc), lambda i, j: (i, j)),
        out_shape=jax.ShapeDtypeStruct((S, D_FF), MXU),
        scratch_shapes=[pltpu.VMEM((tm + HALO, tc), F32), pltpu.VMEM((tm + HALO, tc), F32)],
        compiler_params=_params(),
    )(up, up, up, up, cw, cw, cb, cb)


def _convgate_bwd(up, dact, cw, cb, *, name):
    S = up.shape[0]
    tm, tc, nc = _conv_tiles(S)
    hb = tm // HALO
    nr = S // tm
    R = tm + HALO

    def body(v_ref, g_ref, pv_ref, pg_ref, nv_ref, ng_ref, da_ref, dan_ref, wv_ref, wg_ref, bv_ref, bg_ref,
             duv_ref, dug_ref, dwv_ref, dwg_ref, dbv_ref, dbg_ref, extv, extg, dae, dsv, dsg):
        i = pl.program_id(1)

        @pl.when(i == 0)
        def _():
            for r in (dwv_ref, dwg_ref, dbv_ref, dbg_ref):
                r[...] = jnp.zeros_like(r)

        keep_prev = (i > 0).astype(F32)
        keep_next = (i < nr - 1).astype(F32)
        for ext, t_ref, p_ref, n_ref in ((extv, v_ref, pv_ref, nv_ref), (extg, g_ref, pg_ref, ng_ref)):
            ext[0:HALO, :] = p_ref[...] * keep_prev
            ext[HALO:HALO + tm, :] = t_ref[...]
            ext[HALO + tm:2 * HALO + tm, :] = n_ref[...]
        dae[0:tm, :] = da_ref[...]
        dae[tm:R, :] = dan_ref[...] * keep_next

        def chunk(cc, carry):
            cols = pl.ds(pl.multiple_of(cc * LANE, LANE), LANE)
            wv, wg, bv, bg = wv_ref[:, cols], wg_ref[:, cols], bv_ref[:, cols], bg_ref[:, cols]
            for r0 in range(0, R, CONV_RH):
                rh = min(CONV_RH, R - r0)
                def conv(ext, w, b):
                    return (w[0:1] * ext[pl.ds(HALO - 2 + r0, rh), cols] + w[1:2] * ext[pl.ds(HALO - 1 + r0, rh), cols]
                            + w[2:3] * ext[pl.ds(HALO + r0, rh), cols] + b)
                val = conv(extv, wv, bv)
                gate = conv(extg, wg, bg)
                d = dae[pl.ds(r0, rh), cols]
                sg, silu = _silu_parts(gate)
                dsv[pl.ds(r0, rh), cols] = d * silu
                dsg[pl.ds(r0, rh), cols] = d * val * (sg * (1.0 + gate * (1.0 - sg)))
            for ds, ext, w, du_ref, dw_ref, db_ref in ((dsv, extv, wv, duv_ref, dwv_ref, dbv_ref),
                                                      (dsg, extg, wg, dug_ref, dwg_ref, dbg_ref)):
                acc = [jnp.zeros((1, LANE), F32) for _ in range(4)]
                for r0 in range(0, tm, CONV_RH):
                    d0 = ds[pl.ds(r0, CONV_RH), cols]
                    du_ref[pl.ds(r0, CONV_RH), cols] = (w[2:3] * d0 + w[1:2] * ds[pl.ds(r0 + 1, CONV_RH), cols]
                                                        + w[0:1] * ds[pl.ds(r0 + 2, CONV_RH), cols]).astype(du_ref.dtype)
                    for kk in range(3):
                        acc[kk] = acc[kk] + jnp.sum(d0 * ext[pl.ds(HALO - 2 + kk + r0, CONV_RH), cols], axis=0, keepdims=True)
                    acc[3] = acc[3] + jnp.sum(d0, axis=0, keepdims=True)
                for kk in range(3):
                    dw_ref[kk:kk + 1, cols] += acc[kk]
                db_ref[:, cols] += acc[3]
            return carry

        lax.fori_loop(0, tc // LANE, chunk, 0)

    prev = lambda i: jnp.maximum(i * hb - 1, 0)
    nxt = lambda i: jnp.minimum((i + 1) * hb, S // HALO - 1)
    tile_v = pl.BlockSpec((tm, tc), lambda j, i: (i, j))
    tile_g = pl.BlockSpec((tm, tc), lambda j, i: (i, j + nc))
    w_v = pl.BlockSpec((3, tc), lambda j, i: (0, j))
    w_g = pl.BlockSpec((3, tc), lambda j, i: (0, j + nc))
    b_v = pl.BlockSpec((1, tc), lambda j, i: (0, j))
    b_g = pl.BlockSpec((1, tc), lambda j, i: (0, j + nc))
    return pl.pallas_call(
        body, name=name, grid=(nc, nr),
        in_specs=[tile_v, tile_g,
                  pl.BlockSpec((HALO, tc), lambda j, i: (prev(i), j)), pl.BlockSpec((HALO, tc), lambda j, i: (prev(i), j + nc)),
                  pl.BlockSpec((HALO, tc), lambda j, i: (nxt(i), j)), pl.BlockSpec((HALO, tc), lambda j, i: (nxt(i), j + nc)),
                  tile_v, pl.BlockSpec((HALO, tc), lambda j, i: (nxt(i), j)), w_v, w_g, b_v, b_g],
        out_specs=(tile_v, tile_v, w_v, w_v, b_v, b_v),
        out_shape=(jax.ShapeDtypeStruct((S, D_FF), MXU), jax.ShapeDtypeStruct((S, D_FF), MXU),
                   jax.ShapeDtypeStruct((3, D_FF), F32), jax.ShapeDtypeStruct((3, D_FF), F32),
                   jax.ShapeDtypeStruct((1, D_FF), F32), jax.ShapeDtypeStruct((1, D_FF), F32)),
        scratch_shapes=[pltpu.VMEM((tm + 2 * HALO, tc), F32), pltpu.VMEM((tm + 2 * HALO, tc), F32),
                        pltpu.VMEM((R, tc), F32), pltpu.VMEM((R, tc), F32), pltpu.VMEM((R, tc), F32)],
        compiler_params=_params(),
    )(up, up, up, up, up, up, dact, dact, cw, cw, cb, cb)


def _glu_fwd(z, h, *, name):
    S = z.shape[0]
    tm = _rows(S, 512)

    def body(z_ref, h_ref, o_ref):
        o_ref[...] = h_ref[...] + z_ref[:, :D_MODEL] * jax.nn.sigmoid(z_ref[:, D_MODEL:])

    return pl.pallas_call(
        body, name=name, grid=(S // tm,), in_specs=[_row_spec(tm, 2 * D_MODEL), _row_spec(tm, D_MODEL)],
        out_specs=_row_spec(tm, D_MODEL), out_shape=jax.ShapeDtypeStruct((S, D_MODEL), F32),
        compiler_params=_params(),
    )(z, h)


def _glu_bwd(z, dm, *, name):
    S = z.shape[0]
    tm = _rows(S, 512)

    def body(z_ref, dm_ref, o_ref):
        sg = jax.nn.sigmoid(z_ref[:, D_MODEL:])
        dmv = dm_ref[...]
        o_ref[:, :D_MODEL] = (dmv * sg).astype(o_ref.dtype)
        o_ref[:, D_MODEL:] = (dmv * z_ref[:, :D_MODEL] * sg * (1.0 - sg)).astype(o_ref.dtype)

    return pl.pallas_call(
        body, name=name, grid=(S // tm,), in_specs=[_row_spec(tm, 2 * D_MODEL), _row_spec(tm, D_MODEL)],
        out_specs=_row_spec(tm, 2 * D_MODEL), out_shape=jax.ShapeDtypeStruct((S, 2 * D_MODEL), MXU),
        compiler_params=_params(),
    )(z, dm)


GELU_C = math.sqrt(2.0 / math.pi)
GELU_A = 0.044715


def _gelu(y):
    return 0.5 * y * (1.0 + jnp.tanh(GELU_C * (y + GELU_A * (y * y * y))))


def _gelu_bwd(y, dg, *, name):
    S = y.shape[0]
    tm = _rows(S, 512)

    def body(y_ref, dg_ref, o_ref):
        yv = y_ref[...]
        t = jnp.tanh(GELU_C * (yv + GELU_A * (yv * yv * yv)))
        d = 0.5 * (1.0 + t) + 0.5 * yv * (1.0 - t * t) * (GELU_C * (1.0 + 3.0 * GELU_A * (yv * yv)))
        o_ref[...] = dg_ref[...] * d

    return pl.pallas_call(
        body, name=name, grid=(S // tm,), in_specs=[_row_spec(tm, D_MODEL), _row_spec(tm, D_MODEL)],
        out_specs=_row_spec(tm, D_MODEL), out_shape=jax.ShapeDtypeStruct((S, D_MODEL), F32),
        compiler_params=_params(),
    )(y, dg)


FWD_STRIPS = 4
BWD_STRIPS = 8
SW = NSTATE // SBLK
ST2 = 2 * NSTATE


def _s5_fwd(u, wb, wc, abc, dskip, x0, *, full, name):
    S = u.shape[0]
    T = _rows(S, 256)
    nb = S // T
    nj = T // NSEG

    def body(u_ref, wb_ref, wc_ref, a_ref, d_ref, x0_ref, *rest):
        if full:
            xs_ref, y_ref, yg_ref, st = rest
        else:
            e_ref, xs_ref, st = rest
        i = pl.program_id(0)

        @pl.when(i == 0)
        def _():
            st[...] = x0_ref[...]

        uv = u_ref[...]
        ub = uv.astype(MXU)
        for kb in range(SBLK):
            r = jnp.dot(ub[:, kb * LANE:(kb + 1) * LANE], wb_ref[kb], preferred_element_type=F32)
            xs_ref[:, kb * SW:(kb + 1) * SW] = r[:, :SW]
            xs_ref[:, NSTATE + kb * SW:NSTATE + (kb + 1) * SW] = r[:, SW:]
        for sp in range(FWD_STRIPS):
            w = NSTATE // FWD_STRIPS
            re, im = pl.ds(sp * w, w), pl.ds(NSTATE + sp * w, w)
            ar, ai = a_ref[:, re], a_ref[:, im]

            def step(j, c):
                xr, xi = c
                rows = pl.ds(pl.multiple_of(j * NSEG, NSEG), NSEG)
                nr = ar * xr - ai * xi + xs_ref[rows, re]
                ni = ar * xi + ai * xr + xs_ref[rows, im]
                xs_ref[rows, re] = nr
                xs_ref[rows, im] = ni
                return nr, ni

            xr, xi = lax.fori_loop(0, nj, step, (st[:, re], st[:, im]))
            st[:, re] = xr
            st[:, im] = xi
        if full:
            for kb in range(SBLK):
                yk = (jnp.dot(xs_ref[:, kb * SW:(kb + 1) * SW].astype(MXU), wc_ref[kb, :SW, :], preferred_element_type=F32)
                      + jnp.dot(xs_ref[:, NSTATE + kb * SW:NSTATE + (kb + 1) * SW].astype(MXU), wc_ref[kb, SW:, :],
                                preferred_element_type=F32))
                cols = slice(kb * LANE, (kb + 1) * LANE)
                yk = yk + d_ref[:, cols] * uv[:, cols]
                y_ref[:, cols] = yk
                yg_ref[:, cols] = _gelu(yk).astype(yg_ref.dtype)
        else:
            @pl.when(i == nb - 1)
            def _():
                e_ref[...] = st[...]

    in_specs = [_row_spec(T, D_MODEL), pl.BlockSpec((SBLK, LANE, 2 * SW), lambda i: (0, 0, 0)),
                pl.BlockSpec((SBLK, 2 * SW, LANE), lambda i: (0, 0, 0)), _const_spec(NSEG, ST2),
                _const_spec(1, D_MODEL), _const_spec(NSEG, ST2)]
    if full:
        out_specs = (_row_spec(T, ST2), _row_spec(T, D_MODEL), _row_spec(T, D_MODEL))
        out_shape = (jax.ShapeDtypeStruct((S, ST2), F32), jax.ShapeDtypeStruct((S, D_MODEL), F32),
                     jax.ShapeDtypeStruct((S, D_MODEL), MXU))
        scratch = [pltpu.VMEM((NSEG, ST2), F32)]
    else:
        out_specs = _const_spec(NSEG, ST2)
        out_shape = jax.ShapeDtypeStruct((NSEG, ST2), F32)
        scratch = [pltpu.VMEM((T, ST2), F32), pltpu.VMEM((NSEG, ST2), F32)]
    return pl.pallas_call(
        body, name=name, grid=(nb,), in_specs=in_specs, out_specs=out_specs, out_shape=out_shape,
        scratch_shapes=scratch, compiler_params=_params(big=True),
    )(u, wb, wc, abc, dskip, x0)


def _s5_bwd(dy, xs, u, wct, wbt, abc, dskip, x0, l0, *, full, name):
    S = dy.shape[0]
    T = _rows(S, 128)
    nb = S // T
    nj = T // NSEG
    blk = lambda i: nb - 1 - i

    def body(dy_ref, *rest):
        if full:
            (xs_ref, xh_ref, u_ref, wct_ref, wbt_ref, a_ref, d_ref, x0_ref, l0_ref,
             du_ref, da_ref, dwb_ref, dwc_ref, dd_ref, g_s, lam_s) = rest
        else:
            wct_ref, a_ref, l0_ref, f_ref, g_s, lam_s = rest
        i = pl.program_id(0)

        @pl.when(i == 0)
        def _():
            lam_s[...] = l0_ref[...]
            if full:
                for r in (da_ref, dwb_ref, dwc_ref, dd_ref):
                    r[...] = jnp.zeros_like(r)

        dyv = dy_ref[...]
        dyb = dyv.astype(MXU)
        for kb in range(SBLK):
            r = jnp.dot(dyb[:, kb * LANE:(kb + 1) * LANE], wct_ref[kb], preferred_element_type=F32)
            g_s[:, kb * SW:(kb + 1) * SW] = r[:, :SW]
            g_s[:, NSTATE + kb * SW:NSTATE + (kb + 1) * SW] = r[:, SW:]
        for sp in range(BWD_STRIPS):
            w = NSTATE // BWD_STRIPS
            re, im = pl.ds(sp * w, w), pl.ds(NSTATE + sp * w, w)
            ar, ai = a_ref[:, re], a_ref[:, im]

            def advance(row, lr, li):
                rows = pl.ds(row, NSEG)
                nr = g_s[rows, re] + ar * lr + ai * li
                ni = g_s[rows, im] - ai * lr + ar * li
                g_s[rows, re] = nr
                g_s[rows, im] = ni
                return nr, ni

            def step(jj, c):
                row = pl.multiple_of((nj - 1 - jj) * NSEG, NSEG)
                nr, ni = advance(row, c[0], c[1])
                if not full:
                    return nr, ni
                prow = pl.ds(pl.multiple_of(row - NSEG, NSEG), NSEG)
                xpr, xpi = xs_ref[prow, re], xs_ref[prow, im]
                return nr, ni, c[2] + (nr * xpr + ni * xpi), c[3] + (ni * xpr - nr * xpi)

            init = (lam_s[:, re], lam_s[:, im])
            if full:
                init = init + (jnp.zeros((NSEG, w), F32), jnp.zeros((NSEG, w), F32))
            c = lax.fori_loop(0, nj - 1, step, init)
            lr, li = advance(0, c[0], c[1])
            if full:
                first = (blk(i) == 0)
                xpr = jnp.where(first, x0_ref[:, re], xh_ref[:, re])
                xpi = jnp.where(first, x0_ref[:, im], xh_ref[:, im])
                da_ref[:, re] += c[2] + (lr * xpr + li * xpi)
                da_ref[:, im] += c[3] + (li * xpr - lr * xpi)
            lam_s[:, re] = lr
            lam_s[:, im] = li
        if full:
            uv = u_ref[...]
            ub = uv.astype(MXU)
            dd_ref[...] += jnp.sum(dyv * uv, axis=0, keepdims=True)
            for kb in range(SBLK):
                cols = slice(kb * LANE, (kb + 1) * LANE)
                re = slice(kb * SW, (kb + 1) * SW)
                im = slice(NSTATE + kb * SW, NSTATE + (kb + 1) * SW)
                lr_b = g_s[:, re].astype(MXU)
                li_b = g_s[:, im].astype(MXU)
                duk = (jnp.dot(lr_b, wbt_ref[kb, :SW, :], preferred_element_type=F32)
                       + jnp.dot(li_b, wbt_ref[kb, SW:, :], preferred_element_type=F32))
                du_ref[:, cols] = duk + d_ref[:, cols] * dyv[:, cols]
                dwb_ref[kb, :, :SW] += _dot_tn(ub[:, cols], lr_b)
                dwb_ref[kb, :, SW:] += _dot_tn(ub[:, cols], li_b)
                dwc_ref[kb, :SW, :] += _dot_tn(xs_ref[:, re].astype(MXU), dyb[:, cols])
                dwc_ref[kb, SW:, :] += _dot_tn(xs_ref[:, im].astype(MXU), dyb[:, cols])
        else:
            @pl.when(i == nb - 1)
            def _():
                f_ref[...] = lam_s[...]

    rev = lambda c: pl.BlockSpec((T, c), lambda i: (blk(i), 0))
    w3 = lambda a, b: pl.BlockSpec((SBLK, a, b), lambda i: (0, 0, 0))
    if full:
        hb = T // NSEG
        in_specs = [rev(D_MODEL), rev(ST2),
                    pl.BlockSpec((NSEG, ST2), lambda i: (jnp.maximum(blk(i) * hb - 1, 0), 0)),
                    rev(D_MODEL), w3(LANE, 2 * SW), w3(2 * SW, LANE), _const_spec(NSEG, ST2),
                    _const_spec(1, D_MODEL), _const_spec(NSEG, ST2), _const_spec(NSEG, ST2)]
        ops = [dy, xs, xs, u, wct, wbt, abc, dskip, x0, l0]
        out_specs = (rev(D_MODEL), _const_spec(NSEG, ST2), w3(LANE, 2 * SW), w3(2 * SW, LANE),
                     _const_spec(1, D_MODEL))
        out_shape = (jax.ShapeDtypeStruct((S, D_MODEL), F32), jax.ShapeDtypeStruct((NSEG, ST2), F32),
                     jax.ShapeDtypeStruct((SBLK, LANE, 2 * SW), F32), jax.ShapeDtypeStruct((SBLK, 2 * SW, LANE), F32),
                     jax.ShapeDtypeStruct((1, D_MODEL), F32))
    else:
        in_specs = [rev(D_MODEL), w3(LANE, 2 * SW), _const_spec(NSEG, ST2), _const_spec(NSEG, ST2)]
        ops = [dy, wct, abc, l0]
        out_specs = _const_spec(NSEG, ST2)
        out_shape = jax.ShapeDtypeStruct((NSEG, ST2), F32)
    return pl.pallas_call(
        body, name=name, grid=(nb,), in_specs=in_specs, out_specs=out_specs, out_shape=out_shape,
        scratch_shapes=[pltpu.VMEM((T, ST2), F32), pltpu.VMEM((NSEG, ST2), F32)],
        compiler_params=_params(big=True),
    )(*ops)


def _s5_discretize(lr, li, log_dt, br, bi):
    dt = jnp.exp(log_dt)[:, None]
    mag = jnp.exp(lr * dt)
    ar = mag * jnp.cos(li * dt)
    ai = mag * jnp.sin(li * dt)
    den = lr * lr + li * li
    nr = ar - 1.0
    coef_r = (nr * lr + ai * li) / den
    coef_i = (ai * lr - nr * li) / den
    bbar_r = coef_r[..., None] * br - coef_i[..., None] * bi
    bbar_i = coef_r[..., None] * bi + coef_i[..., None] * br
    return ar, ai, bbar_r, bbar_i


def _blockdiag(m):
    gpb = SSM_GROUPS // SBLK
    a, b = m.shape[1:]
    mb = m.reshape(SBLK, gpb, a, b)
    eye = jnp.eye(gpb, dtype=m.dtype)
    return jnp.einsum('kgab,gh->kgahb', mb, eye).reshape(SBLK, gpb * a, gpb * b)


def _blockdiag_extract(w, a, b):
    gpb = SSM_GROUPS // SBLK
    w5 = w.reshape(SBLK, gpb, a, gpb, b)
    return jnp.einsum('kgahb,gh->kgab', w5, jnp.eye(gpb, dtype=w.dtype)).reshape(SSM_GROUPS, a, b)


def _cpow(ar, ai, n):
    rr, ri = jnp.ones_like(ar), jnp.zeros_like(ai)
    br, bi = ar, ai
    while n:
        if n & 1:
            rr, ri = rr * br - ri * bi, rr * bi + ri * br
        br, bi = br * br - bi * bi, 2.0 * br * bi
        n >>= 1
    return rr, ri


def _perm(a):
    s, c = a.shape
    return a.reshape(NSEG, s // NSEG, c).transpose(1, 0, 2).reshape(s, c)


def _unperm(a):
    s, c = a.shape
    return a.reshape(s // NSEG, NSEG, c).transpose(1, 0, 2).reshape(s, c)


def _other_chips(x, y):
    return [(1 - x, y), (x, 1 - y), (1 - x, 1 - y)]


def _span(chip, size, align):
    return pl.ds(pl.multiple_of(chip * size, align), size)


def _exchange_ops(pieces, s_refs, o_refs, send_sems, recv_sems, local_sems):
    x, y, c = lax.axis_index("x"), lax.axis_index("y"), lax.axis_index("c")
    me = 2 * x + y
    others = _other_chips(x, y)
    npc = len(pieces)

    def remote(k, p, tx, ty, src_chip, dst_chip):
        si, oi, sv, dv = pieces[p]
        return pltpu.make_async_remote_copy(
            src_ref=sv(s_refs[si], src_chip), dst_ref=dv(o_refs[oi], dst_chip), send_sem=send_sems.at[k, p],
            recv_sem=recv_sems.at[k, p], device_id=(tx, ty, c), device_id_type=MESH)

    def local(p):
        si, oi, sv, dv = pieces[p]
        return pltpu.make_async_copy(sv(s_refs[si], me), dv(o_refs[oi], me), local_sems.at[p])

    def start():
        for p in range(npc):
            local(p).start()
        for k, (tx, ty) in enumerate(others):
            for p in range(npc):
                remote(k, p, tx, ty, 2 * tx + ty, me).start()

    def finish():
        for k, (tx, ty) in enumerate(others):
            for p in range(npc):
                remote(k, p, tx, ty, me, 2 * tx + ty).wait_recv()
        for k, (tx, ty) in enumerate(others):
            for p in range(npc):
                remote(k, p, tx, ty, 2 * tx + ty, me).wait_send()
        for p in range(npc):
            local(p).wait()

    return start, finish


def _exchange_scratch(npc):
    return [pltpu.SemaphoreType.DMA((NCHIP - 1, npc)), pltpu.SemaphoreType.DMA((NCHIP - 1, npc)),
            pltpu.SemaphoreType.DMA((npc,))]


def _chip_exchange(srcs, out_shapes, pieces, *, name):
    ns, no = len(srcs), len(out_shapes)

    def body(*refs):
        start, finish = _exchange_ops(pieces, refs[:ns], refs[ns:ns + no], *refs[ns + no:])
        start()
        finish()

    hbm = pl.BlockSpec(memory_space=pl.ANY)
    return pl.pallas_call(
        body, name=name, in_specs=[hbm] * ns, out_specs=tuple([hbm] * no), out_shape=tuple(out_shapes),
        scratch_shapes=_exchange_scratch(len(pieces)),
    )(*srcs)


def _sibling_exchange(srcs, *, name):
    n = len(srcs)

    def body(*refs):
        s_refs, o_refs, send_sems, recv_sems = refs[:n], refs[n:2 * n], refs[2 * n], refs[2 * n + 1]
        x, y, c = lax.axis_index("x"), lax.axis_index("y"), lax.axis_index("c")
        cps = [pltpu.make_async_remote_copy(src_ref=s_refs[p], dst_ref=o_refs[p], send_sem=send_sems.at[p],
                                            recv_sem=recv_sems.at[p], device_id=(x, y, 1 - c), device_id_type=MESH)
               for p in range(n)]
        for cp in cps:
            cp.start()
        for cp in cps:
            cp.wait()

    hbm = pl.BlockSpec(memory_space=pl.ANY)
    return pl.pallas_call(
        body, name=name, in_specs=[hbm] * n, out_specs=tuple([hbm] * n),
        out_shape=tuple(jax.ShapeDtypeStruct(s.shape, s.dtype) for s in srcs),
        scratch_shapes=[pltpu.SemaphoreType.DMA((n,)), pltpu.SemaphoreType.DMA((n,))],
    )(*srcs)


def _row_tile(r, c, tile_bytes):
    best = 16
    for t in range(16, r + 1, 16):
        if r % t == 0 and t * c * 4 <= tile_bytes:
            best = t
    assert r % best == 0
    return best


def _sum_chips(r, *, name):
    _, R, W = r.shape
    tm = _row_tile(R, W, 2 * 1024 * 1024)

    def body(r_ref, o_ref):
        o_ref[...] = ((r_ref[0].astype(F32) + r_ref[1].astype(F32)) + r_ref[2].astype(F32)) + r_ref[3].astype(F32)

    return pl.pallas_call(
        body, name=name, grid=(R // tm,), in_specs=[pl.BlockSpec((NCHIP, tm, W), lambda i: (0, i, 0))],
        out_specs=_row_spec(tm, W), out_shape=jax.ShapeDtypeStruct((R, W), F32), compiler_params=_params(),
    )(r)


def _adamw(p_mine, p_sib, w, m, v, *, name):
    R, W = w.shape
    tm = _row_tile(R, W, 1024 * 1024)

    def body(a_ref, b_ref, w_ref, m_ref, v_ref, g_ref, d_ref, nm_ref, nv_ref):
        g = a_ref[...] + b_ref[...]
        mm = ADAM_B1 * m_ref[...] + (1.0 - ADAM_B1) * g
        vv = ADAM_B2 * v_ref[...] + (1.0 - ADAM_B2) * (g * g)
        m_hat = mm / (1.0 - ADAM_B1 ** ADAM_STEP)
        v_hat = vv / (1.0 - ADAM_B2 ** ADAM_STEP)
        g_ref[...] = g
        d_ref[...] = -ADAM_LR * (m_hat / (jnp.sqrt(v_hat) + ADAM_EPS) + ADAM_WD * w_ref[...])
        nm_ref[...] = mm
        nv_ref[...] = vv

    spec = _row_spec(tm, W)
    shp = jax.ShapeDtypeStruct((R, W), F32)
    return pl.pallas_call(
        body, name=name, grid=(R // tm,), in_specs=[spec] * 5, out_specs=(spec,) * 4, out_shape=(shp,) * 4,
        compiler_params=_params(),
    )(p_mine, p_sib, w, m, v)


def _pack_small(parts):
    flat = jnp.concatenate([p.reshape(-1) for p in parts])
    pad = (-flat.shape[0]) % (SMALL_ROWS * PACKW)
    return jnp.pad(flat, (0, pad)).reshape(-1, PACKW)


def _unpack_small(buf, shapes):
    flat, out, off = buf.reshape(-1), [], 0
    for shp in shapes:
        sz = math.prod(shp)
        out.append(flat[off:off + sz].reshape(shp))
        off += sz
    return out


def _shard(a, t, ax):
    sz = a.shape[ax] // NCHIP
    return lax.slice_in_dim(a, t * sz, (t + 1) * sz, axis=ax)


MLA_W = ['mla_w_a', 'mla_w_uq', 'mla_w_ukv', 'mla_w_o']
REST_W = ['ssm_w_in', 'ssm_w_glu', 'ffn_w_up', 'ffn_w_down']
SMALL = [n for n in WNAMES if n not in MLA_W + REST_W]


def _gather_plan(w, names, with_small):
    srcs, outs, pieces = [], [], []
    for name in names:
        local = w[name].astype(MXU)
        local = local[0] if local.shape[0] == 1 else local
        ax = SHARD_AXIS[name] - (1 if w[name].shape[0] == 1 else 0)
        full = local.shape[:ax] + (NCHIP * local.shape[ax],) + local.shape[ax + 1:]
        si, oi = len(srcs), len(outs)
        srcs.append(local)
        outs.append(jax.ShapeDtypeStruct(full, MXU))
        size = local.shape[ax]
        if local.ndim == 2:
            if ax == 0:
                pieces.append((si, oi, lambda r, t: r, lambda r, ch, size=size: r.at[_span(ch, size, 8), :]))
            else:
                pieces.append((si, oi, lambda r, t: r, lambda r, ch, size=size: r.at[:, _span(ch, size, LANE)]))
        else:
            for l in range(local.shape[0]):
                if ax == 1:
                    dv = lambda r, ch, l=l, size=size: r.at[l, _span(ch, size, 8), :]
                else:
                    dv = lambda r, ch, l=l, size=size: r.at[l, :, _span(ch, size, LANE)]
                pieces.append((si, oi, lambda r, t, l=l: r.at[l], dv))
    if with_small:
        small = _pack_small([w[n] for n in GATHER_F32])
        srcs.append(small)
        outs.append(jax.ShapeDtypeStruct((NCHIP,) + small.shape, F32))
        pieces.append((len(srcs) - 1, len(outs) - 1, lambda r, t: r, lambda r, ch: r.at[ch]))
    return srcs, outs, pieces


def _gather_result(got, w, names, with_small):
    full = dict(zip(names, got[:len(names)]))
    if with_small:
        per_chip = [_unpack_small(got[-1][t], [w[n].shape for n in GATHER_F32]) for t in range(NCHIP)]
        for j, n in enumerate(GATHER_F32):
            full[n] = jnp.concatenate([per_chip[t][j] for t in range(NCHIP)], axis=SHARD_AXIS[n])
    return full


def _as_rows(a):
    return a.reshape(-1, a.shape[-1])


def _grad_plan(grads, w, names, with_small):
    srcs, outs, pieces = [], [], []
    for name in names:
        local = w[name]
        ax = SHARD_AXIS[name]
        size = local.shape[ax]
        oi = len(outs)
        layers = grads[name] if isinstance(grads[name], list) else [grads[name]]
        outs.append(jax.ShapeDtypeStruct((NCHIP,) + local.shape, layers[0].dtype))
        for l, g in enumerate(layers):
            si = len(srcs)
            srcs.append(g)
            if ax == 1:
                sv = lambda r, t, size=size: r.at[_span(t, size, 8), :]
            else:
                sv = lambda r, t, size=size: r.at[:, _span(t, size, LANE)]
            pieces.append((si, oi, sv, lambda r, ch, l=l: r.at[ch, l]))
    if with_small:
        small = jnp.stack([_pack_small([_shard(grads[n], t, SHARD_AXIS[n]) if n in SHARD_AXIS else grads[n]
                                        for n in SMALL]) for t in range(NCHIP)])
        srcs.append(small)
        outs.append(jax.ShapeDtypeStruct(small.shape, F32))
        pieces.append((len(srcs) - 1, len(outs) - 1, lambda r, t: r.at[t], lambda r, ch: r.at[ch]))
    return srcs, outs, pieces


def _reduce_and_update(names, landed, w, mom, var):
    partial = [_sum_chips(r.reshape(NCHIP, -1, r.shape[-1]), name="grad_sum_chips") for r in landed]
    sibling = _sibling_exchange(partial, name="grad_sibling")
    res = [dict(), dict(), dict(), dict()]
    for j, name in enumerate(names):
        outs4 = _adamw(partial[j], sibling[j], _as_rows(w[name]), _as_rows(mom[name]), _as_rows(var[name]), name="adamw")
        for d, o in zip(res, outs4):
            d[name] = o.reshape(w[name].shape)
    outs4 = _adamw(partial[-1], sibling[-1], *[_pack_small([t[n] for n in SMALL]) for t in (w, mom, var)], name="adamw")
    for d, o in zip(res, outs4):
        d.update(zip(SMALL, _unpack_small(o, [w[n].shape for n in SMALL])))
    return res


def kernel(x, positions, mla_w_a, mla_g_q, mla_g_kv, mla_w_uq, mla_w_ukv, mla_w_o, ssm_w_in, ssm_lambda_re, ssm_lambda_im, ssm_log_dt, ssm_b_re, ssm_b_im, ssm_c_re, ssm_c_im, ssm_d, ssm_w_glu, ffn_w_up, ffn_conv_w, ffn_conv_b, ffn_w_down, g_mix, g_ffn, g_final, loss_target, m_mla_w_a, m_mla_g_q, m_mla_g_kv, m_mla_w_uq, m_mla_w_ukv, m_mla_w_o, m_ssm_w_in, m_ssm_lambda_re, m_ssm_lambda_im, m_ssm_log_dt, m_ssm_b_re, m_ssm_b_im, m_ssm_c_re, m_ssm_c_im, m_ssm_d, m_ssm_w_glu, m_ffn_w_up, m_ffn_conv_w, m_ffn_conv_b, m_ffn_w_down, m_g_mix, m_g_ffn, m_g_final, v_mla_w_a, v_mla_g_q, v_mla_g_kv, v_mla_w_uq, v_mla_w_ukv, v_mla_w_o, v_ssm_w_in, v_ssm_lambda_re, v_ssm_lambda_im, v_ssm_log_dt, v_ssm_b_re, v_ssm_b_im, v_ssm_c_re, v_ssm_c_im, v_ssm_d, v_ssm_w_glu, v_ffn_w_up, v_ffn_conv_w, v_ffn_conv_b, v_ffn_w_down, v_g_mix, v_g_ffn, v_g_final):
    w = dict(zip(WNAMES, (mla_w_a, mla_g_q, mla_g_kv, mla_w_uq, mla_w_ukv, mla_w_o, ssm_w_in, ssm_lambda_re,
                          ssm_lambda_im, ssm_log_dt, ssm_b_re, ssm_b_im, ssm_c_re, ssm_c_im, ssm_d, ssm_w_glu,
                          ffn_w_up, ffn_conv_w, ffn_conv_b, ffn_w_down, g_mix, g_ffn, g_final)))
    mom = dict(zip(WNAMES, (m_mla_w_a, m_mla_g_q, m_mla_g_kv, m_mla_w_uq, m_mla_w_ukv, m_mla_w_o, m_ssm_w_in,
                            m_ssm_lambda_re, m_ssm_lambda_im, m_ssm_log_dt, m_ssm_b_re, m_ssm_b_im, m_ssm_c_re,
                            m_ssm_c_im, m_ssm_d, m_ssm_w_glu, m_ffn_w_up, m_ffn_conv_w, m_ffn_conv_b,
                            m_ffn_w_down, m_g_mix, m_g_ffn, m_g_final)))
    var = dict(zip(WNAMES, (v_mla_w_a, v_mla_g_q, v_mla_g_kv, v_mla_w_uq, v_mla_w_ukv, v_mla_w_o, v_ssm_w_in,
                            v_ssm_lambda_re, v_ssm_lambda_im, v_ssm_log_dt, v_ssm_b_re, v_ssm_b_im, v_ssm_c_re,
                            v_ssm_c_im, v_ssm_d, v_ssm_w_glu, v_ffn_w_up, v_ffn_conv_w, v_ffn_conv_b,
                            v_ffn_w_down, v_g_mix, v_g_ffn, v_g_final)))
    S = x.shape[1]
    D = D_MODEL
    x2 = x.reshape(S, D)
    tgt = loss_target.reshape(S, D)

    fw = _gather_result(_chip_exchange(*_gather_plan(w, MLA_W, False), name="gather_weights"), w, MLA_W, False)
    w_a = jnp.pad(fw['mla_w_a'], ((0, 0), (0, A_PAD - KR0 - QK_ROPE)))
    uq = fw['mla_w_uq'].reshape(Q_LORA, HEADS, QK_DIM)
    w_uq = jnp.concatenate([uq[:, :, :QK_NOPE].reshape(Q_LORA, HEADS * QK_NOPE),
                            jnp.pad(uq[:, :, QK_NOPE:], ((0, 0), (0, 0), (0, LANE - QK_ROPE))).reshape(Q_LORA, HEADS * LANE)],
                           axis=1)
    w_ukv = fw['mla_w_ukv']
    w_o = fw['mla_w_o']
    conv_b = w['ffn_conv_b']
    g_q, g_kv = w['mla_g_q'], w['mla_g_kv']
    gm, gf = w['g_mix'], w['g_ffn']
    gfin = w['g_final'].reshape(1, D)

    inv = 1.0 / (ROPE_THETA ** (jnp.arange(0, QK_ROPE, 2, dtype=F32) / QK_ROPE))
    ang = positions.reshape(S).astype(F32)[:, None] * inv
    cos, sin = jnp.cos(ang), jnp.sin(ang)
    zpad = jnp.zeros((S, LANE - QK_ROPE), F32)
    c128 = jnp.concatenate([cos, cos, zpad], axis=1)
    s128 = jnp.concatenate([-sin, sin, zpad], axis=1)

    hn0 = _rmsnorm_fwd(x2, gm[0:1], name="rms_mix0")
    a = _mm(hn0, w_a, name="mla_a")
    cqn, ckvn, kr = _mla_mid_fwd(a, g_q, g_kv, c128, s128, name="mla_mid_fwd")
    qfull = _mm(cqn, w_uq, name="mla_q")
    kv = _mm(ckvn, w_ukv, out_dtype=MXU, name="mla_kv")
    qs, ks, vs = _qk_prep(qfull, kv, kr, c128, s128, name="qk_prep")
    os_, lses = [], []
    for h in range(HEADS):
        if h == 0:
            o_h, lse_h, *got = _attn_fwd(qs[h], ks[h], vs[h], name="attn_fwd_gather", ride=_gather_plan(w, REST_W, True))
            fw.update(_gather_result(got, w, REST_W, True))
        else:
            o_h, lse_h = _attn_fwd(qs[h], ks[h], vs[h], name="attn_fwd")
        os_.append(o_h)
        lses.append(lse_h)
    w_in = fw['ssm_w_in']
    w_glu = fw['ssm_w_glu']
    w_up = fw['ffn_w_up']
    w_down = fw['ffn_w_down']
    conv_w = fw['ffn_conv_w']
    dskip = fw['ssm_d']
    o_cat = jnp.concatenate(os_, axis=1)
    h1 = _mm(o_cat, w_o, res=x2, name="mla_o")

    def ffn_fwd(h, l):
        hn = _rmsnorm_fwd(h, gf[l:l + 1], name="rms_ffn")
        up = _mm(hn, w_up[l], name="ffn_up")
        act = _convgate_fwd(up, conv_w[l], conv_b[l:l + 1], name="convgate_fwd")
        return _mm(act, w_down[l], res=h, name="ffn_down"), (hn, up, act)

    h2, saved0 = ffn_fwd(h1, 0)

    lam_re, lam_im, log_dt = w['ssm_lambda_re'][0], w['ssm_lambda_im'][0], w['ssm_log_dt'][0]
    (a_re, a_im, bbar_r, bbar_i), disc_vjp = jax.vjp(_s5_discretize, lam_re, lam_im, log_dt, w['ssm_b_re'][0],
                                                    w['ssm_b_im'][0])
    c_re, c_im = w['ssm_c_re'][0], w['ssm_c_im'][0]
    bt_r, bt_i = jnp.swapaxes(bbar_r, 1, 2), jnp.swapaxes(bbar_i, 1, 2)
    wb = jnp.concatenate([_blockdiag(bt_r), _blockdiag(bt_i)], axis=2).astype(MXU)
    wbt = jnp.concatenate([_blockdiag(bbar_r), _blockdiag(bbar_i)], axis=1).astype(MXU)
    ct_r, ct_i = jnp.swapaxes(c_re, 1, 2), jnp.swapaxes(c_im, 1, 2)
    wc = jnp.concatenate([_blockdiag(ct_r), _blockdiag(-ct_i)], axis=1).astype(MXU)
    wct = jnp.concatenate([_blockdiag(c_re), _blockdiag(-c_im)], axis=2).astype(MXU)
    af_r, af_i = a_re.reshape(NSTATE), a_im.reshape(NSTATE)
    abc = jnp.broadcast_to(jnp.concatenate([af_r, af_i])[None], (NSEG, ST2))
    seg = S // NSEG
    ap_r, ap_i = _cpow(af_r, af_i, seg)

    hn1 = _rmsnorm_fwd(h2, gm[1:2], name="rms_mix1")
    u = _mm(hn1, w_in, name="s5_in")
    u_p = _perm(u)
    zero_state = jnp.zeros((NSEG, ST2), F32)
    ends = _s5_fwd(u_p, wb, wc, abc, dskip, zero_state, full=False, name="s5_fwd_ends")
    inits, cr, ci = [], jnp.zeros((NSTATE,), F32), jnp.zeros((NSTATE,), F32)
    for r in range(NSEG):
        inits.append(jnp.concatenate([cr, ci]))
        er, ei = ends[r, :NSTATE], ends[r, NSTATE:]
        cr, ci = er + ap_r * cr - ap_i * ci, ei + ap_r * ci + ap_i * cr
    x0 = jnp.stack(inits)
    xs, y_p, yg_p = _s5_fwd(u_p, wb, wc, abc, dskip, x0, full=True, name="s5_fwd")
    yg = _unperm(yg_p)
    z = _mm(yg, w_glu, name="s5_glu")
    h3 = _glu_fwd(z, h2, name="glu_fwd")
    h4, saved1 = ffn_fwd(h3, 1)

    loss_l, dh4, dg_final = _loss_head(h4, gfin, tgt, name="loss_head")

    grads = {}

    def ffn_bwd(h_in, g, saved, l):
        hn, up, act = saved
        w_up_t = w_up[l].T
        dact = _mm(g, w_down[l].T, name="ffn_down_dx")
        dw_down = _mm(act, g, mode="tn", out_dtype=MXU, name="ffn_down_dw")
        duv, dug, dwv, dwg, dbv, dbg = _convgate_bwd(up, dact, conv_w[l], conv_b[l:l + 1], name="convgate_bwd")
        dw_up = jnp.concatenate([_mm(hn, duv, mode="tn", out_dtype=MXU, name="ffn_up_dw"), _mm(hn, dug, mode="tn", out_dtype=MXU, name="ffn_up_dw")],
                                axis=1)
        dhn = _mm(duv, w_up_t[:D_FF], name="ffn_up_dx")
        dhn = _mm(dug, w_up_t[D_FF:], res=dhn, name="ffn_up_dx_acc")
        dh, dg = _rmsnorm_bwd(h_in, gf[l:l + 1], dhn, g, name="rms_ffn_bwd")
        return dh, dict(w_up=dw_up, w_down=dw_down, conv_w=jnp.concatenate([dwv, dwg], axis=1),
                        conv_b=jnp.concatenate([dbv, dbg], axis=1)[0], g_ffn=dg[0])

    dh3, fg1 = ffn_bwd(h3, dh4, saved1, 1)

    dz = _glu_bwd(z, dh3, name="glu_bwd")
    grads['ssm_w_glu'] = _mm(yg, dz, mode="tn", out_dtype=MXU, name="s5_glu_dw")
    dyg = _mm(dz, w_glu.T, name="s5_glu_dx")
    dy_p = _gelu_bwd(y_p, _perm(dyg), name="gelu_bwd")
    firsts = _s5_bwd(dy_p, None, None, wct, None, abc, None, None, zero_state, full=False, name="s5_bwd_firsts")
    linits, cr, ci = [None] * NSEG, jnp.zeros((NSTATE,), F32), jnp.zeros((NSTATE,), F32)
    for r in reversed(range(NSEG)):
        linits[r] = jnp.concatenate([cr, ci])
        fr, fi = firsts[r, :NSTATE], firsts[r, NSTATE:]
        cr, ci = fr + ap_r * cr + ap_i * ci, fi + ap_r * ci - ap_i * cr
    l0 = jnp.stack(linits)
    du_p, dab, dwb, dwc, dd = _s5_bwd(dy_p, xs, u_p, wct, wbt, abc, dskip, x0, l0, full=True, name="s5_bwd")
    du = _unperm(du_p)
    grads['ssm_w_in'] = _mm(hn1, du, mode="tn", out_dtype=MXU, name="s5_in_dw")
    dhn1 = _mm(du, w_in.T, name="s5_in_dx")
    dh2, dg_mix1 = _rmsnorm_bwd(h2, gm[1:2], dhn1, dh3, name="rms_mix_bwd")
    da_sum = jnp.sum(dab, axis=0)
    dbt_r = _blockdiag_extract(dwb[:, :, :SW], SSM_GROUP, SSM_STATE)
    dbt_i = _blockdiag_extract(dwb[:, :, SW:], SSM_GROUP, SSM_STATE)
    dlr, dli, dlog_dt, dbr, dbi = disc_vjp((da_sum[:NSTATE].reshape(SSM_GROUPS, SSM_STATE),
                                            da_sum[NSTATE:].reshape(SSM_GROUPS, SSM_STATE),
                                            jnp.swapaxes(dbt_r, 1, 2), jnp.swapaxes(dbt_i, 1, 2)))
    dct_r = _blockdiag_extract(dwc[:, :SW, :], SSM_STATE, SSM_GROUP)
    dct_i = _blockdiag_extract(dwc[:, SW:, :], SSM_STATE, SSM_GROUP)
    grads['ssm_lambda_re'], grads['ssm_lambda_im'], grads['ssm_log_dt'] = dlr[None], dli[None], dlog_dt[None]
    grads['ssm_b_re'], grads['ssm_b_im'] = dbr[None], dbi[None]
    grads['ssm_c_re'] = jnp.swapaxes(dct_r, 1, 2)[None]
    grads['ssm_c_im'] = -jnp.swapaxes(dct_i, 1, 2)[None]
    grads['ssm_d'] = dd

    dh1, fg0 = ffn_bwd(h1, dh2, saved0, 0)
    grads['ffn_w_up'] = [fg0['w_up'], fg1['w_up']]
    grads['ffn_w_down'] = [fg0['w_down'], fg1['w_down']]
    grads['ffn_conv_w'] = jnp.stack([fg0['conv_w'], fg1['conv_w']])
    grads['ffn_conv_b'] = jnp.stack([fg0['conv_b'], fg1['conv_b']])
    grads['g_ffn'] = jnp.stack([fg0['g_ffn'], fg1['g_ffn']])

    do_cat = _mm(dh1, w_o.T, out_dtype=MXU, name="mla_o_dx")
    grads['mla_w_o'] = _mm(o_cat, dh1, mode="tn", out_dtype=MXU, name="mla_o_dw")
    dqs, dks, dvs = [], [], []
    for h in range(HEADS):
        do_h = do_cat[:, h * V_HEAD:(h + 1) * V_HEAD]
        delta = _attn_delta(do_h, os_[h], name="attn_delta")
        tiles = (S // _rows(S, BWD_T), 1, _rows(S, BWD_T))
        lse2 = (lses[h] * math.log2(math.e)).reshape(tiles)
        if h == 0:
            dq_h, dk_h, dv_h, *landed = _attn_bwd(qs[h], ks[h], vs[h], do_h, lse2, delta.reshape(tiles),
                                                  name="attn_bwd_exchange", ride=_grad_plan(grads, w, REST_W, False))
        else:
            dq_h, dk_h, dv_h = _attn_bwd(qs[h], ks[h], vs[h], do_h, lse2, delta.reshape(tiles), name="attn_bwd")
        dqs.append(dq_h)
        dks.append(dk_h)
        dvs.append(dv_h)
    dqfull, dkv, dkr = _qk_prep_bwd(dqs, dks, dvs, c128, s128, name="qk_prep_bwd")
    dw_uq_p = _mm(cqn, dqfull, mode="tn", out_dtype=MXU, name="mla_q_dw")
    dcqn = _mm(dqfull, w_uq.T, name="mla_q_dx")
    grads['mla_w_ukv'] = _mm(ckvn, dkv, mode="tn", out_dtype=MXU, name="mla_kv_dw")
    dckvn = _mm(dkv, w_ukv.T, name="mla_kv_dx")
    da, dgq, dgkv = _mla_mid_bwd(a, dcqn, dckvn, dkr, g_q, g_kv, c128, s128, name="mla_mid_bwd")
    grads['mla_w_a'] = _mm(hn0, da, mode="tn", out_dtype=MXU, name="mla_a_dw")[:, :KR0 + QK_ROPE]
    dhn0 = _mm(da, w_a.T, name="mla_a_dx")
    dx, dg_mix0 = _rmsnorm_bwd(x2, gm[0:1], dhn0, dh1, name="rms_mix_bwd")
    grads['mla_w_uq'] = jnp.concatenate(
        [dw_uq_p[:, :HEADS * QK_NOPE].reshape(Q_LORA, HEADS, QK_NOPE),
         dw_uq_p[:, HEADS * QK_NOPE:].reshape(Q_LORA, HEADS, LANE)[:, :, :QK_ROPE]], axis=2).reshape(Q_LORA, HEADS * QK_DIM)
    grads['mla_g_q'], grads['mla_g_kv'] = dgq, dgkv
    grads['g_mix'] = jnp.concatenate([dg_mix0, dg_mix1], axis=0)
    grads['g_final'] = dg_final[0]

    landed += _chip_exchange(*_grad_plan(grads, w, MLA_W, True), name="grad_exchange")
    g_out, d_out, m_out, v_out = _reduce_and_update(REST_W + MLA_W, landed, w, mom, var)

    loss = lax.psum(loss_l[0, 0], ("x", "y", "c"))
    return (loss, dx.reshape(1, S, D), *[g_out[n] for n in WNAMES], *[d_out[n] for n in WNAMES],
            *[m_out[n] for n in WNAMES], *[v_out[n] for n in WNAMES])
```

```python
import math

import jax
import jax.numpy as jnp
from jax import lax
from jax.experimental import pallas as pl
from jax.experimental.pallas import tpu as pltpu

F32 = jnp.float32
MXU = jnp.bfloat16

D_MODEL = 1024
CHUNK = 64
EPS = 1e-6
HEADS = 8
QK_NOPE = 128
QK_ROPE = 64
V_HEAD = 128
Q_LORA = 384
KV_LORA = 256
ROPE_THETA = 10000.0
QK_DIM = QK_NOPE + QK_ROPE
SSM_GROUP = 16
SSM_GROUPS = D_MODEL // SSM_GROUP
SSM_STATE = 64
NSTATE = SSM_GROUPS * SSM_STATE
D_FF = 2816
ATT_SCALE = QK_DIM ** -0.5
EXP2_SCALE = ATT_SCALE * math.log2(math.e)
NEG = -1e30
NSEG = 8
SBLK = 8

ADAM_LR = 0.001
ADAM_B1 = 0.9
ADAM_B2 = 0.999
ADAM_EPS = 1e-08
ADAM_WD = 0.01
ADAM_STEP = 10

LANE = 128
TN_MAX_COLS = 2816
NN_MAX_COLS = 1408
TN_ACC_ELEMS = 1536 * 1024
VMEM_BIG = 56 * 1024 * 1024

WNAMES = ['mla_w_a', 'mla_g_q', 'mla_g_kv', 'mla_w_uq', 'mla_w_ukv', 'mla_w_o', 'ssm_w_in', 'ssm_lambda_re',
          'ssm_lambda_im', 'ssm_log_dt', 'ssm_b_re', 'ssm_b_im', 'ssm_c_re', 'ssm_c_im', 'ssm_d', 'ssm_w_glu',
          'ffn_w_up', 'ffn_conv_w', 'ffn_conv_b', 'ffn_w_down', 'g_mix', 'g_ffn', 'g_final']
FWD_NAMES = ['x', 'positions'] + WNAMES
SHARD_AXIS = {'mla_w_a': 1, 'mla_w_uq': 2, 'mla_w_ukv': 2, 'mla_w_o': 1, 'ssm_w_in': 1, 'ssm_d': 1,
              'ssm_w_glu': 2, 'ffn_w_up': 2, 'ffn_conv_w': 2, 'ffn_w_down': 1}
GATHER_F32 = ['ssm_d', 'ffn_conv_w']
NCHIP = 4
PACKW = 1024
SMALL_ROWS = 64
MESH = pl.DeviceIdType.MESH


def _tile(d, pref):
    t = min(pref, d) // LANE * LANE
    while t >= LANE:
        if d % t == 0:
            return t
        t -= LANE
    return d


def _rows(s, pref):
    t = min(s, pref)
    assert s % t == 0 and t % 8 == 0
    return t


def _params(big=False):
    if big:
        return pltpu.CompilerParams(vmem_limit_bytes=VMEM_BIG)
    return pltpu.CompilerParams(vmem_limit_bytes=40 * 1024 * 1024)


def _mm(a, b, *, name, mode="nn", out_dtype=F32, res=None, tm=None, tn=None, tk=None):
    if mode == "nn":
        M, K = a.shape
        tm = tm or _rows(M, 1024)
    else:
        K, M = a.shape
        tn = tn or _tile(b.shape[1], TN_MAX_COLS)
        tm = tm or _tile(M, max(LANE, TN_ACC_ELEMS // tn))
    N = b.shape[1]
    assert b.shape[0] == K
    tn = tn or _tile(N, NN_MAX_COLS if N % 512 else 512)
    tk = tk or (_tile(K, 1408) if mode == "nn" else _rows(K, 512))
    nk = K // tk
    has_res = res is not None

    def body(a_ref, b_ref, *rest):
        if has_res:
            r_ref, o_ref, acc = rest
        else:
            o_ref, acc = rest
        k = pl.program_id(2)

        @pl.when(k == 0)
        def _():
            acc[...] = jnp.zeros_like(acc)

        av = a_ref[...].astype(MXU)
        bv = b_ref[...].astype(MXU)
        if mode == "nn":
            acc[...] += jnp.dot(av, bv, preferred_element_type=F32)
        else:
            acc[...] += lax.dot_general(av, bv, (((0,), (0,)), ((), ())), preferred_element_type=F32)

        @pl.when(k == nk - 1)
        def _():
            o = acc[...]
            if has_res:
                o = o + r_ref[...]
            o_ref[...] = o.astype(o_ref.dtype)

    if mode == "nn":
        a_spec = pl.BlockSpec((tm, tk), lambda i, j, k: (i, k))
    else:
        a_spec = pl.BlockSpec((tk, tm), lambda i, j, k: (k, i))
    in_specs = [a_spec, pl.BlockSpec((tk, tn), lambda i, j, k: (k, j))]
    ops = [a, b]
    if has_res:
        in_specs.append(pl.BlockSpec((tm, tn), lambda i, j, k: (i, j)))
        ops.append(res)
    return pl.pallas_call(
        body, name=name, grid=(M // tm, N // tn, nk), in_specs=in_specs,
        out_specs=pl.BlockSpec((tm, tn), lambda i, j, k: (i, j)),
        out_shape=jax.ShapeDtypeStruct((M, N), out_dtype),
        scratch_shapes=[pltpu.VMEM((tm, tn), F32)], compiler_params=_params(),
    )(*ops)


def _row_spec(tm, c):
    return pl.BlockSpec((tm, c), lambda i: (i, 0))


def _const_spec(r, c):
    return pl.BlockSpec((r, c), lambda i: (0, 0))


def _rms_parts(xv):
    r = lax.rsqrt(jnp.mean(xv * xv, axis=-1, keepdims=True) + EPS)
    return r, xv * r


def _rms_vjp(xv, gv, dyv):
    r, xhat = _rms_parts(xv)
    gy = dyv * gv
    dx = r * (gy - xhat * jnp.mean(gy * xhat, axis=-1, keepdims=True))
    return dx, dyv * xhat


def _rmsnorm_fwd(x, g, *, name):
    S, D = x.shape
    tm = _rows(S, 512)

    def body(x_ref, g_ref, o_ref):
        _, xhat = _rms_parts(x_ref[...])
        o_ref[...] = (xhat * g_ref[...]).astype(o_ref.dtype)

    return pl.pallas_call(
        body, name=name, grid=(S // tm,), in_specs=[_row_spec(tm, D), _const_spec(1, D)],
        out_specs=_row_spec(tm, D), out_shape=jax.ShapeDtypeStruct((S, D), MXU), compiler_params=_params(),
    )(x, g)


def _rmsnorm_bwd(x, g, dy, dres, *, name):
    S, D = x.shape
    tm = _rows(S, 512)

    def body(x_ref, g_ref, dy_ref, dr_ref, dx_ref, dg_ref):
        @pl.when(pl.program_id(0) == 0)
        def _():
            dg_ref[...] = jnp.zeros_like(dg_ref)

        dx, dgp = _rms_vjp(x_ref[...], g_ref[...], dy_ref[...])
        dx_ref[...] = dr_ref[...] + dx
        dg_ref[...] += jnp.sum(dgp, axis=0, keepdims=True)

    return pl.pallas_call(
        body, name=name, grid=(S // tm,),
        in_specs=[_row_spec(tm, D), _const_spec(1, D), _row_spec(tm, D), _row_spec(tm, D)],
        out_specs=(_row_spec(tm, D), _const_spec(1, D)),
        out_shape=(jax.ShapeDtypeStruct((S, D), F32), jax.ShapeDtypeStruct((1, D), F32)),
        compiler_params=_params(),
    )(x, g, dy, dres)


def _loss_head(h, g, tgt, *, name):
    S, D = h.shape
    tm = _rows(S, 512)

    def body(h_ref, g_ref, t_ref, l_ref, dh_ref, dg_ref):
        @pl.when(pl.program_id(0) == 0)
        def _():
            l_ref[...] = jnp.zeros_like(l_ref)
            dg_ref[...] = jnp.zeros_like(dg_ref)

        hv = h_ref[...]
        gv = g_ref[...]
        _, xhat = _rms_parts(hv)
        e = xhat * gv - t_ref[...]
        l_ref[...] += 0.5 * jnp.sum(jnp.mean(e * e, axis=-1, keepdims=True), axis=0, keepdims=True)
        dx, dgp = _rms_vjp(hv, gv, e * (1.0 / D))
        dh_ref[...] = dx
        dg_ref[...] += jnp.sum(dgp, axis=0, keepdims=True)

    return pl.pallas_call(
        body, name=name, grid=(S // tm,),
        in_specs=[_row_spec(tm, D), _const_spec(1, D), _row_spec(tm, D)],
        out_specs=(_const_spec(1, 1), _row_spec(tm, D), _const_spec(1, D)),
        out_shape=(jax.ShapeDtypeStruct((1, 1), F32), jax.ShapeDtypeStruct((S, D), F32),
                   jax.ShapeDtypeStruct((1, D), F32)),
        compiler_params=_params(),
    )(h, g, tgt)


def _swap_halves(g):
    lane = lax.broadcasted_iota(jnp.int32, g.shape, 1)
    return jnp.where(lane < QK_ROPE // 2, pltpu.roll(g, LANE - QK_ROPE // 2, axis=1),
                     pltpu.roll(g, QK_ROPE // 2, axis=1))


def _rope128(g, c128, s128):
    return g * c128 + _swap_halves(g) * s128


def _rope128_vjp(dy, c128, s128):
    lane = lax.broadcasted_iota(jnp.int32, dy.shape, 1)
    return jnp.where(lane < QK_ROPE, dy * c128 + _swap_halves(dy * s128), 0.0)


A_PAD = 768
KR0 = Q_LORA + KV_LORA


def _mla_mid_fwd(a, g_q, g_kv, c128, s128, *, name):
    S = a.shape[0]
    tm = _rows(S, 512)

    def body(a_ref, gq_ref, gkv_ref, c_ref, s_ref, cq_ref, ckv_ref, kr_ref):
        av = a_ref[...]
        _, qh = _rms_parts(av[:, :Q_LORA])
        cq_ref[...] = (qh * gq_ref[...]).astype(cq_ref.dtype)
        _, kh = _rms_parts(av[:, Q_LORA:KR0])
        ckv_ref[...] = (kh * gkv_ref[...]).astype(ckv_ref.dtype)
        kr = _rope128(av[:, KR0:A_PAD], c_ref[...], s_ref[...])
        kr_ref[...] = kr[:, :QK_ROPE].astype(kr_ref.dtype)

    return pl.pallas_call(
        body, name=name, grid=(S // tm,),
        in_specs=[_row_spec(tm, A_PAD), _const_spec(1, Q_LORA), _const_spec(1, KV_LORA), _row_spec(tm, LANE),
                  _row_spec(tm, LANE)],
        out_specs=(_row_spec(tm, Q_LORA), _row_spec(tm, KV_LORA), _row_spec(tm, QK_ROPE)),
        out_shape=(jax.ShapeDtypeStruct((S, Q_LORA), MXU), jax.ShapeDtypeStruct((S, KV_LORA), MXU),
                   jax.ShapeDtypeStruct((S, QK_ROPE), MXU)),
        compiler_params=_params(),
    )(a, g_q, g_kv, c128, s128)


def _mla_mid_bwd(a, dcq, dckv, dkr, g_q, g_kv, c128, s128, *, name):
    S = a.shape[0]
    tm = _rows(S, 512)

    def body(a_ref, dcq_ref, dckv_ref, dkr_ref, gq_ref, gkv_ref, c_ref, s_ref, da_ref, dgq_ref, dgkv_ref):
        @pl.when(pl.program_id(0) == 0)
        def _():
            dgq_ref[...] = jnp.zeros_like(dgq_ref)
            dgkv_ref[...] = jnp.zeros_like(dgkv_ref)

        av = a_ref[...]
        dx, dgp = _rms_vjp(av[:, :Q_LORA], gq_ref[...], dcq_ref[...])
        da_ref[:, :Q_LORA] = dx.astype(da_ref.dtype)
        dgq_ref[...] += jnp.sum(dgp, axis=0, keepdims=True)
        dx, dgp = _rms_vjp(av[:, Q_LORA:KR0], gkv_ref[...], dckv_ref[...])
        da_ref[:, Q_LORA:KR0] = dx.astype(da_ref.dtype)
        dgkv_ref[...] += jnp.sum(dgp, axis=0, keepdims=True)
        da_ref[:, KR0:A_PAD] = _rope128_vjp(dkr_ref[...], c_ref[...], s_ref[...]).astype(da_ref.dtype)

    return pl.pallas_call(
        body, name=name, grid=(S // tm,),
        in_specs=[_row_spec(tm, A_PAD), _row_spec(tm, Q_LORA), _row_spec(tm, KV_LORA), _row_spec(tm, LANE),
                  _const_spec(1, Q_LORA), _const_spec(1, KV_LORA), _row_spec(tm, LANE), _row_spec(tm, LANE)],
        out_specs=(_row_spec(tm, A_PAD), _const_spec(1, Q_LORA), _const_spec(1, KV_LORA)),
        out_shape=(jax.ShapeDtypeStruct((S, A_PAD), MXU), jax.ShapeDtypeStruct((1, Q_LORA), F32),
                   jax.ShapeDtypeStruct((1, KV_LORA), F32)),
        compiler_params=_params(),
    )(a, dcq, dckv, dkr, g_q, g_kv, c128, s128)


QF = 2 * HEADS * LANE
KVF = HEADS * (QK_NOPE + V_HEAD)
VX = 2 * V_HEAD


def _qk_prep(qfull, kv, kr, c128, s128, *, name):
    S = qfull.shape[0]
    tm = _rows(S, 256)

    def body(q_ref, kv_ref, kr_ref, c_ref, s_ref, *outs):
        qo, ko, vo = outs[:HEADS], outs[HEADS:2 * HEADS], outs[2 * HEADS:]
        cv, sv = c_ref[...], s_ref[...]
        krv = kr_ref[...]
        for h in range(HEADS):
            qo[h][:, :QK_NOPE] = q_ref[:, h * LANE:(h + 1) * LANE].astype(MXU)
            g = q_ref[:, (HEADS + h) * LANE:(HEADS + h + 1) * LANE]
            qo[h][:, QK_NOPE:] = _rope128(g, cv, sv)[:, :QK_ROPE].astype(MXU)
            ko[h][:, :QK_NOPE] = kv_ref[:, 2 * h * LANE:(2 * h + 1) * LANE]
            ko[h][:, QK_NOPE:] = krv
            vo[h][:, :V_HEAD] = kv_ref[:, (2 * h + 1) * LANE:(2 * h + 2) * LANE]
            vo[h][:, V_HEAD:] = jnp.ones((tm, VX - V_HEAD), MXU)

    shapes = ([jax.ShapeDtypeStruct((S, QK_DIM), MXU)] * (2 * HEADS)
              + [jax.ShapeDtypeStruct((S, VX), MXU)] * HEADS)
    specs = [_row_spec(tm, QK_DIM)] * (2 * HEADS) + [_row_spec(tm, VX)] * HEADS
    outs = pl.pallas_call(
        body, name=name, grid=(S // tm,),
        in_specs=[_row_spec(tm, QF), _row_spec(tm, KVF), _row_spec(tm, QK_ROPE), _row_spec(tm, LANE),
                  _row_spec(tm, LANE)],
        out_specs=tuple(specs), out_shape=tuple(shapes), compiler_params=_params(),
    )(qfull, kv, kr, c128, s128)
    return outs[:HEADS], outs[HEADS:2 * HEADS], outs[2 * HEADS:]


def _qk_prep_bwd(dqs, dks, dvs, c128, s128, *, name):
    S = dqs[0].shape[0]
    tm = _rows(S, 256)

    def body(*refs):
        dq = refs[:HEADS]
        dk = refs[HEADS:2 * HEADS]
        dv = refs[2 * HEADS:3 * HEADS]
        c_ref, s_ref, dqf_ref, dkv_ref, dkr_ref, tmp = refs[3 * HEADS:]
        cv, sv = c_ref[...], s_ref[...]
        tmp[...] = jnp.zeros_like(tmp)
        dkr_ref[...] = jnp.zeros_like(dkr_ref)
        for h in range(HEADS):
            dqf_ref[:, h * LANE:(h + 1) * LANE] = dq[h][:, :QK_NOPE].astype(MXU)
            tmp[:, :QK_ROPE] = dq[h][:, QK_NOPE:]
            dqf_ref[:, (HEADS + h) * LANE:(HEADS + h + 1) * LANE] = _rope128_vjp(tmp[...], cv, sv).astype(MXU)
            dkv_ref[:, 2 * h * LANE:(2 * h + 1) * LANE] = dk[h][:, :QK_NOPE].astype(MXU)
            dkv_ref[:, (2 * h + 1) * LANE:(2 * h + 2) * LANE] = dv[h][...].astype(MXU)
            dkr_ref[:, :QK_ROPE] += dk[h][:, QK_NOPE:]

    return pl.pallas_call(
        body, name=name, grid=(S // tm,),
        in_specs=[_row_spec(tm, QK_DIM)] * (2 * HEADS) + [_row_spec(tm, V_HEAD)] * HEADS
        + [_row_spec(tm, LANE), _row_spec(tm, LANE)],
        out_specs=(_row_spec(tm, QF), _row_spec(tm, KVF), _row_spec(tm, LANE)),
        out_shape=(jax.ShapeDtypeStruct((S, QF), MXU), jax.ShapeDtypeStruct((S, KVF), MXU),
                   jax.ShapeDtypeStruct((S, LANE), F32)),
        scratch_shapes=[pltpu.VMEM((tm, LANE), F32)], compiler_params=_params(),
    )(*dqs, *dks, *dvs, c128, s128)


def _dot_nt(a, b):
    return lax.dot_general(a, b, (((1,), (1,)), ((), ())), preferred_element_type=F32)


def _dot_tn(a, b):
    return lax.dot_general(a, b, (((0,), (0,)), ((), ())), preferred_element_type=F32)


def _attn_fwd(q, k, vx, *, name, ride=None):
    S = q.shape[0]
    T = _rows(S, 1024)
    n = S // T
    cpt = T // CHUNK

    r_srcs, r_outs, r_pieces = ride or ((), (), ())
    ns, no = len(r_srcs), len(r_outs)

    def body(q_ref, k_ref, v_ref, *rest):
        o_ref, lse_ref = rest[ns:ns + 2]
        s_buf, p_buf, a_buf, m_s, acc_s = rest[ns + 2 + no:ns + 7 + no]
        i = pl.program_id(0)
        if ride:
            start, finish = _exchange_ops(r_pieces, rest[:ns], rest[ns + 2:ns + 2 + no], *rest[ns + 7 + no:])
            pl.when(i == 0)(start)
        qc = lax.broadcasted_iota(jnp.int32, (T, T), 0) // CHUNK
        kc = lax.broadcasted_iota(jnp.int32, (T, T), 1) // CHUNK
        dchunk = kc - qc

        def tile_rows(b):
            return pl.ds(pl.multiple_of(jnp.clip(b, 0, n - 1) * T, T), T)

        def scores(b, slot):
            s = _dot_nt(q_ref[...], k_ref[tile_rows(b), :])
            s_buf[slot] = jnp.where(dchunk <= (i - b) * cpt, s, NEG)

        def softmax(slot):
            s = s_buf[slot]
            m_prev = m_s[...]
            m_new = jnp.maximum(m_prev, jnp.max(s, axis=1, keepdims=True))
            a_buf[slot] = jnp.exp2((m_prev - m_new) * EXP2_SCALE)
            p_buf[slot] = jnp.exp2((s - m_new) * EXP2_SCALE).astype(MXU)
            m_s[...] = m_new

        def pv(b, slot):
            acc_s[...] = a_buf[slot] * acc_s[...] + jnp.dot(p_buf[slot], v_ref[tile_rows(b), :],
                                                              preferred_element_type=F32)

        m_s[...] = jnp.full_like(m_s, NEG)
        acc_s[...] = jnp.zeros_like(acc_s)
        p_buf[1] = jnp.zeros((T, T), MXU)
        a_buf[1] = jnp.ones((T, 1), F32)
        scores(0, 0)

        def pair(u, carry):
            t = 2 * u
            scores(t + 1, 1)
            softmax(0)
            pv(t - 1, 1)
            scores(t + 2, 0)
            softmax(1)
            pv(t, 0)
            return carry

        npairs = (i + 2) // 2
        lax.fori_loop(0, npairs, pair, 0)
        pv(2 * npairs - 1, 1)
        acc = acc_s[...]
        l = acc[:, V_HEAD:V_HEAD + 1]
        o_ref[...] = (acc[:, :V_HEAD] / l).astype(o_ref.dtype)
        lse_ref[...] = m_s[...] * ATT_SCALE + jnp.log(l)
        if ride:
            pl.when(i == n - 1)(finish)

    hbm = pl.BlockSpec(memory_space=pl.ANY)
    return pl.pallas_call(
        body, name=name, grid=(n,),
        in_specs=[pl.BlockSpec((T, QK_DIM), lambda i: (i, 0)), pl.BlockSpec((S, QK_DIM), lambda i: (0, 0)),
                  pl.BlockSpec((S, VX), lambda i: (0, 0))] + [hbm] * ns,
        out_specs=(pl.BlockSpec((T, V_HEAD), lambda i: (i, 0)), pl.BlockSpec((T, 1), lambda i: (i, 0))) + (hbm,) * no,
        out_shape=(jax.ShapeDtypeStruct((S, V_HEAD), MXU), jax.ShapeDtypeStruct((S, 1), F32)) + tuple(r_outs),
        scratch_shapes=[pltpu.VMEM((2, T, T), F32), pltpu.VMEM((2, T, T), MXU), pltpu.VMEM((2, T, 1), F32),
                        pltpu.VMEM((T, 1), F32), pltpu.VMEM((T, VX), F32)]
        + (_exchange_scratch(len(r_pieces)) if ride else []),
        compiler_params=_params(big=True),
    )(q, k, vx, *r_srcs)


def _attn_delta(do, o, *, name):
    S = do.shape[0]
    tm = _rows(S, 1024)

    def body(do_ref, o_ref, d_ref):
        d_ref[...] = jnp.sum(do_ref[...].astype(F32) * o_ref[...].astype(F32), axis=1, keepdims=True)

    return pl.pallas_call(
        body, name=name, grid=(S // tm,), in_specs=[_row_spec(tm, V_HEAD), _row_spec(tm, V_HEAD)],
        out_specs=_row_spec(tm, 1), out_shape=jax.ShapeDtypeStruct((S, 1), F32), compiler_params=_params(),
    )(do, o)


BWD_T = 512


def _attn_bwd(q, k, v, do, lse2, delta, *, name, ride=None):
    S = q.shape[0]
    T = _rows(S, BWD_T)
    n = S // T
    cpt = T // CHUNK

    r_srcs, r_outs, r_pieces = ride or ((), (), ())
    ns, no = len(r_srcs), len(r_outs)

    def body(q_hbm, k_ref, v_ref, do_hbm, lse_ref, dl_ref, *rest):
        dq_hbm, dk_ref, dv_ref = rest[ns:ns + 3]
        q_res, do_res, dq_s, s_buf, dp_buf, p_buf, ds_buf, dk_s, dv_s = rest[ns + 3 + no:ns + 12 + no]
        j = pl.program_id(0)
        if ride:
            start, finish = _exchange_ops(r_pieces, rest[:ns], rest[ns + 3:ns + 3 + no], *rest[ns + 12 + no:])
            pl.when(j == 0)(start)

        @pl.when(j == 0)
        def _():
            pltpu.sync_copy(q_hbm, q_res)
            pltpu.sync_copy(do_hbm, do_res)
            dq_s[...] = jnp.zeros_like(dq_s)

        kc = lax.broadcasted_iota(jnp.int32, (T, T), 0) // CHUNK
        qc = lax.broadcasted_iota(jnp.int32, (T, T), 1) // CHUNK
        dchunk = kc - qc

        def tile(t):
            return jnp.clip(j + t, 0, n - 1)

        def rows(t):
            return pl.ds(pl.multiple_of(tile(t) * T, T), T)

        def scores(t, slot):
            visible_up_to = jnp.where(j + t < n, t * cpt, -2 * cpt)
            s = _dot_nt(k_ref[...], q_res[rows(t), :])
            s_buf[slot] = jnp.where(dchunk <= visible_up_to, s, NEG)
            dp_buf[slot] = _dot_nt(v_ref[...], do_res[rows(t), :])

        def probs(t, slot):
            pt = jnp.exp2(s_buf[slot] * EXP2_SCALE - lse_ref[tile(t)])
            p_buf[slot] = pt.astype(MXU)
            ds_buf[slot] = (pt * (dp_buf[slot] - dl_ref[tile(t)]) * ATT_SCALE).astype(MXU)

        def grads(t, slot):
            r = rows(t)
            dv_s[...] += jnp.dot(p_buf[slot], do_res[r, :], preferred_element_type=F32)
            ds = ds_buf[slot]
            dk_s[...] += jnp.dot(ds, q_res[r, :], preferred_element_type=F32)
            dq_s[r, :] += _dot_tn(ds, k_ref[...])

        dk_s[...] = jnp.zeros_like(dk_s)
        dv_s[...] = jnp.zeros_like(dv_s)
        p_buf[1] = jnp.zeros((T, T), MXU)
        ds_buf[1] = jnp.zeros((T, T), MXU)
        scores(0, 0)

        def pair(u, carry):
            t = 2 * u
            scores(t + 1, 1)
            probs(t, 0)
            grads(t - 1, 1)
            scores(t + 2, 0)
            probs(t + 1, 1)
            grads(t, 0)
            return carry

        npairs = (n - j + 1) // 2
        lax.fori_loop(0, npairs, pair, 0)
        grads(2 * npairs - 1, 1)
        dk_ref[...] = dk_s[...]
        dv_ref[...] = dv_s[...]

        @pl.when(j == n - 1)
        def _():
            pltpu.sync_copy(dq_s, dq_hbm)
            if ride:
                finish()

    hbm = pl.BlockSpec(memory_space=pl.ANY)
    k_map = lambda j: (j, 0)
    whole = pl.BlockSpec((n, 1, T), lambda j: (0, 0, 0))
    return pl.pallas_call(
        body, name=name, grid=(n,),
        in_specs=[hbm, pl.BlockSpec((T, QK_DIM), k_map), pl.BlockSpec((T, V_HEAD), k_map), hbm, whole, whole]
        + [hbm] * ns,
        out_specs=(hbm, pl.BlockSpec((T, QK_DIM), k_map), pl.BlockSpec((T, V_HEAD), k_map)) + (hbm,) * no,
        out_shape=(jax.ShapeDtypeStruct((S, QK_DIM), F32), jax.ShapeDtypeStruct((S, QK_DIM), F32),
                   jax.ShapeDtypeStruct((S, V_HEAD), F32)) + tuple(r_outs),
        scratch_shapes=[pltpu.VMEM((S, QK_DIM), MXU), pltpu.VMEM((S, V_HEAD), MXU), pltpu.VMEM((S, QK_DIM), F32),
                        pltpu.VMEM((2, T, T), F32), pltpu.VMEM((2, T, T), F32), pltpu.VMEM((2, T, T), MXU),
                        pltpu.VMEM((2, T, T), MXU), pltpu.VMEM((T, QK_DIM), F32), pltpu.VMEM((T, V_HEAD), F32)]
        + (_exchange_scratch(len(r_pieces)) if ride else []),
        compiler_params=_params(big=True),
    )(q, k, v, do, lse2, delta, *r_srcs)


HALO = 8
CONV_RH = 64


def _conv_tiles(S):
    tm = _rows(S, 256)
    tc = D_FF // 2
    return tm, tc, D_FF // tc


def _silu_parts(gate):
    sg = jax.nn.sigmoid(gate)
    return sg, gate * sg


def _convgate_fwd(up, cw, cb, *, name):
    S = up.shape[0]
    tm, tc, nc = _conv_tiles(S)
    hb = tm // HALO

    def body(v_ref, g_ref, hv_ref, hg_ref, wv_ref, wg_ref, bv_ref, bg_ref, o_ref, cv_ref, cg_ref):
        keep = (pl.program_id(0) > 0).astype(F32)

        def chunk(cc, carry):
            cols = pl.ds(pl.multiple_of(cc * LANE, LANE), LANE)
            wv, wg, bv, bg = wv_ref[:, cols], wg_ref[:, cols], bv_ref[:, cols], bg_ref[:, cols]
            for r0 in range(0, tm, CONV_RH):
                def conv(t_ref, h_ref, w, b):
                    if r0:
                        span = t_ref[pl.ds(r0 - HALO, CONV_RH + HALO), cols]
                    else:
                        span = jnp.concatenate([h_ref[:, cols] * keep, t_ref[pl.ds(0, CONV_RH), cols]], axis=0)
                    back2, back1 = [pltpu.roll(span, s, axis=0)[HALO:] for s in (2, 1)]
                    return w[0:1] * back2 + w[1:2] * back1 + w[2:3] * span[HALO:] + b
                val = conv(v_ref, hv_ref, wv, bv)
                gate = conv(g_ref, hg_ref, wg, bg)
                cv_ref[pl.ds(r0, CONV_RH), cols] = val
                cg_ref[pl.ds(r0, CONV_RH), cols] = gate
                o_ref[pl.ds(r0, CONV_RH), cols] = (_silu_parts(gate)[1] * val).astype(o_ref.dtype)
            return carry

        lax.fori_loop(0, tc // LANE, chunk, 0)

    prev = lambda i: jnp.maximum(i * hb - 1, 0)
    return pl.pallas_call(
        body, name=name, grid=(S // tm, nc),
        in_specs=[pl.BlockSpec((tm, tc), lambda i, j: (i, j)), pl.BlockSpec((tm, tc), lambda i, j: (i, j + nc)),
                  pl.BlockSpec((HALO, tc), lambda i, j: (prev(i), j)),
                  pl.BlockSpec((HALO, tc), lambda i, j: (prev(i), j + nc)),
                  pl.BlockSpec((3, tc), lambda i, j: (0, j)), pl.BlockSpec((3, tc), lambda i, j: (0, j + nc)),
                  pl.BlockSpec((1, tc), lambda i, j: (0, j)), pl.BlockSpec((1, tc), lambda i, j: (0, j + nc))],
        out_specs=(pl.BlockSpec((tm, tc), lambda i, j: (i, j)),) * 3,
        out_shape=(jax.ShapeDtypeStruct((S, D_FF), MXU), jax.ShapeDtypeStruct((S, D_FF), F32),
                   jax.ShapeDtypeStruct((S, D_FF), F32)),
        compiler_params=_params(),
    )(up, up, up, up, cw, cw, cb, cb)


def _convgate_bwd(up, cval, cgate, dact, cw, *, name):
    S = up.shape[0]
    tm, tc, nc = _conv_tiles(S)
    hb = tm // HALO
    nr = S // tm
    R = tm + HALO

    def body(uv_ref, ug_ref, cv_ref, cg_ref, ncv_ref, ncg_ref, da_ref, dan_ref, wv_ref, wg_ref,
             duv_ref, dug_ref, dwv_ref, dwg_ref, dbv_ref, dbg_ref, dsv, dsg):
        i = pl.program_id(1)

        @pl.when(i == 0)
        def _():
            for r in (dwv_ref, dwg_ref, dbv_ref, dbg_ref):
                r[...] = jnp.zeros_like(r)

        keep_next = (i < nr - 1).astype(F32)

        def chunk(cc, carry):
            cols = pl.ds(pl.multiple_of(cc * LANE, LANE), LANE)

            def d_conv(rows, val, gate, d):
                sg, silu = _silu_parts(gate)
                dsv[rows, cols] = d * silu
                dsg[rows, cols] = d * val * (sg * (1.0 + gate * (1.0 - sg)))

            for r0 in range(0, tm, CONV_RH):
                rows = pl.ds(r0, CONV_RH)
                d_conv(rows, cv_ref[rows, cols], cg_ref[rows, cols], da_ref[rows, cols])
            d_conv(pl.ds(tm, HALO), ncv_ref[:, cols], ncg_ref[:, cols], dan_ref[:, cols] * keep_next)
            for ds, u_ref, w_ref, du_ref, dw_ref, db_ref in ((dsv, uv_ref, wv_ref, duv_ref, dwv_ref, dbv_ref),
                                                            (dsg, ug_ref, wg_ref, dug_ref, dwg_ref, dbg_ref)):
                w = w_ref[:, cols]
                acc = [jnp.zeros((1, LANE), F32) for _ in range(4)]
                for r0 in range(0, tm, CONV_RH):
                    rows = pl.ds(r0, CONV_RH)
                    span = ds[pl.ds(r0, CONV_RH + HALO), cols]
                    d = [span[:CONV_RH]] + [pltpu.roll(span, CONV_RH + HALO - s, axis=0)[:CONV_RH] for s in (1, 2)]
                    du_ref[rows, cols] = (w[2:3] * d[0] + w[1:2] * d[1] + w[0:1] * d[2]).astype(du_ref.dtype)
                    u = u_ref[rows, cols]
                    for kk in range(3):
                        acc[kk] = acc[kk] + jnp.sum(d[2 - kk] * u, axis=0, keepdims=True)
                    acc[3] = acc[3] + jnp.sum(d[0], axis=0, keepdims=True)
                for kk in range(3):
                    dw_ref[kk:kk + 1, cols] += acc[kk]
                db_ref[:, cols] += acc[3]
            return carry

        lax.fori_loop(0, tc // LANE, chunk, 0)

    nxt = lambda i: jnp.minimum((i + 1) * hb, S // HALO - 1)
    tile_v = pl.BlockSpec((tm, tc), lambda j, i: (i, j))
    tile_g = pl.BlockSpec((tm, tc), lambda j, i: (i, j + nc))
    halo = pl.BlockSpec((HALO, tc), lambda j, i: (nxt(i), j))
    w_v = pl.BlockSpec((3, tc), lambda j, i: (0, j))
    w_g = pl.BlockSpec((3, tc), lambda j, i: (0, j + nc))
    b_v = pl.BlockSpec((1, tc), lambda j, i: (0, j))
    return pl.pallas_call(
        body, name=name, grid=(nc, nr),
        in_specs=[tile_v, tile_g, tile_v, tile_v, halo, halo, tile_v, halo, w_v, w_g],
        out_specs=(tile_v, tile_v, w_v, w_v, b_v, b_v),
        out_shape=(jax.ShapeDtypeStruct((S, D_FF), MXU), jax.ShapeDtypeStruct((S, D_FF), MXU),
                   jax.ShapeDtypeStruct((3, D_FF), F32), jax.ShapeDtypeStruct((3, D_FF), F32),
                   jax.ShapeDtypeStruct((1, D_FF), F32), jax.ShapeDtypeStruct((1, D_FF), F32)),
        scratch_shapes=[pltpu.VMEM((R, tc), F32), pltpu.VMEM((R, tc), F32)],
        compiler_params=_params(),
    )(up, up, cval, cgate, cval, cgate, dact, dact, cw, cw)


def _glu_fwd(z, h, *, name):
    S = z.shape[0]
    tm = _rows(S, 512)

    def body(z_ref, h_ref, o_ref):
        o_ref[...] = h_ref[...] + z_ref[:, :D_MODEL] * jax.nn.sigmoid(z_ref[:, D_MODEL:])

    return pl.pallas_call(
        body, name=name, grid=(S // tm,), in_specs=[_row_spec(tm, 2 * D_MODEL), _row_spec(tm, D_MODEL)],
        out_specs=_row_spec(tm, D_MODEL), out_shape=jax.ShapeDtypeStruct((S, D_MODEL), F32),
        compiler_params=_params(),
    )(z, h)


def _glu_bwd(z, dm, *, name):
    S = z.shape[0]
    tm = _rows(S, 512)

    def body(z_ref, dm_ref, o_ref):
        sg = jax.nn.sigmoid(z_ref[:, D_MODEL:])
        dmv = dm_ref[...]
        o_ref[:, :D_MODEL] = (dmv * sg).astype(o_ref.dtype)
        o_ref[:, D_MODEL:] = (dmv * z_ref[:, :D_MODEL] * sg * (1.0 - sg)).astype(o_ref.dtype)

    return pl.pallas_call(
        body, name=name, grid=(S // tm,), in_specs=[_row_spec(tm, 2 * D_MODEL), _row_spec(tm, D_MODEL)],
        out_specs=_row_spec(tm, 2 * D_MODEL), out_shape=jax.ShapeDtypeStruct((S, 2 * D_MODEL), MXU),
        compiler_params=_params(),
    )(z, dm)


GELU_C = math.sqrt(2.0 / math.pi)
GELU_A = 0.044715


def _gelu(y):
    return 0.5 * y * (1.0 + jnp.tanh(GELU_C * (y + GELU_A * (y * y * y))))


def _gelu_bwd(y, dg, *, name):
    S = y.shape[0]
    tm = _rows(S, 512)

    def body(y_ref, dg_ref, o_ref):
        yv = y_ref[...]
        t = jnp.tanh(GELU_C * (yv + GELU_A * (yv * yv * yv)))
        d = 0.5 * (1.0 + t) + 0.5 * yv * (1.0 - t * t) * (GELU_C * (1.0 + 3.0 * GELU_A * (yv * yv)))
        o_ref[...] = dg_ref[...] * d

    return pl.pallas_call(
        body, name=name, grid=(S // tm,), in_specs=[_row_spec(tm, D_MODEL), _row_spec(tm, D_MODEL)],
        out_specs=_row_spec(tm, D_MODEL), out_shape=jax.ShapeDtypeStruct((S, D_MODEL), F32),
        compiler_params=_params(),
    )(y, dg)


FWD_STRIPS = 4
BWD_STRIPS = 8
SW = NSTATE // SBLK
ST2 = 2 * NSTATE


def _s5_fwd(u, wb, wc, abc, dskip, x0, *, full, name):
    S = u.shape[0]
    T = _rows(S, 256)
    nb = S // T
    nj = T // NSEG

    def body(u_ref, wb_ref, wc_ref, a_ref, d_ref, x0_ref, *rest):
        if full:
            xs_ref, y_ref, yg_ref, st = rest
        else:
            e_ref, xs_ref, st = rest
        i = pl.program_id(0)

        @pl.when(i == 0)
        def _():
            st[...] = x0_ref[...]

        uv = u_ref[...]
        ub = uv.astype(MXU)
        for kb in range(SBLK):
            r = jnp.dot(ub[:, kb * LANE:(kb + 1) * LANE], wb_ref[kb], preferred_element_type=F32)
            xs_ref[:, kb * SW:(kb + 1) * SW] = r[:, :SW]
            xs_ref[:, NSTATE + kb * SW:NSTATE + (kb + 1) * SW] = r[:, SW:]
        for sp in range(FWD_STRIPS):
            w = NSTATE // FWD_STRIPS
            re, im = pl.ds(sp * w, w), pl.ds(NSTATE + sp * w, w)
            ar, ai = a_ref[:, re], a_ref[:, im]

            def step(j, c):
                xr, xi = c
                rows = pl.ds(pl.multiple_of(j * NSEG, NSEG), NSEG)
                nr = ar * xr - ai * xi + xs_ref[rows, re]
                ni = ar * xi + ai * xr + xs_ref[rows, im]
                xs_ref[rows, re] = nr
                xs_ref[rows, im] = ni
                return nr, ni

            xr, xi = lax.fori_loop(0, nj, step, (st[:, re], st[:, im]))
            st[:, re] = xr
            st[:, im] = xi
        if full:
            for kb in range(SBLK):
                yk = (jnp.dot(xs_ref[:, kb * SW:(kb + 1) * SW].astype(MXU), wc_ref[kb, :SW, :], preferred_element_type=F32)
                      + jnp.dot(xs_ref[:, NSTATE + kb * SW:NSTATE + (kb + 1) * SW].astype(MXU), wc_ref[kb, SW:, :],
                                preferred_element_type=F32))
                cols = slice(kb * LANE, (kb + 1) * LANE)
                yk = yk + d_ref[:, cols] * uv[:, cols]
                y_ref[:, cols] = yk
                yg_ref[:, cols] = _gelu(yk).astype(yg_ref.dtype)
        else:
            @pl.when(i == nb - 1)
            def _():
                e_ref[...] = st[...]

    in_specs = [_row_spec(T, D_MODEL), pl.BlockSpec((SBLK, LANE, 2 * SW), lambda i: (0, 0, 0)),
                pl.BlockSpec((SBLK, 2 * SW, LANE), lambda i: (0, 0, 0)), _const_spec(NSEG, ST2),
                _const_spec(1, D_MODEL), _const_spec(NSEG, ST2)]
    if full:
        out_specs = (_row_spec(T, ST2), _row_spec(T, D_MODEL), _row_spec(T, D_MODEL))
        out_shape = (jax.ShapeDtypeStruct((S, ST2), F32), jax.ShapeDtypeStruct((S, D_MODEL), F32),
                     jax.ShapeDtypeStruct((S, D_MODEL), MXU))
        scratch = [pltpu.VMEM((NSEG, ST2), F32)]
    else:
        out_specs = _const_spec(NSEG, ST2)
        out_shape = jax.ShapeDtypeStruct((NSEG, ST2), F32)
        scratch = [pltpu.VMEM((T, ST2), F32), pltpu.VMEM((NSEG, ST2), F32)]
    return pl.pallas_call(
        body, name=name, grid=(nb,), in_specs=in_specs, out_specs=out_specs, out_shape=out_shape,
        scratch_shapes=scratch, compiler_params=_params(big=True),
    )(u, wb, wc, abc, dskip, x0)


def _s5_bwd(dy, xs, u, wct, wbt, abc, dskip, x0, l0, *, full, name):
    S = dy.shape[0]
    T = _rows(S, 128)
    nb = S // T
    nj = T // NSEG
    blk = lambda i: nb - 1 - i

    def body(dy_ref, *rest):
        if full:
            (xs_ref, xh_ref, u_ref, wct_ref, wbt_ref, a_ref, d_ref, x0_ref, l0_ref,
             du_ref, da_ref, dwb_ref, dwc_ref, dd_ref, g_s, lam_s) = rest
        else:
            wct_ref, a_ref, l0_ref, f_ref, g_s, lam_s = rest
        i = pl.program_id(0)

        @pl.when(i == 0)
        def _():
            lam_s[...] = l0_ref[...]
            if full:
                for r in (da_ref, dwb_ref, dwc_ref, dd_ref):
                    r[...] = jnp.zeros_like(r)

        dyv = dy_ref[...]
        dyb = dyv.astype(MXU)
        for kb in range(SBLK):
            r = jnp.dot(dyb[:, kb * LANE:(kb + 1) * LANE], wct_ref[kb], preferred_element_type=F32)
            g_s[:, kb * SW:(kb + 1) * SW] = r[:, :SW]
            g_s[:, NSTATE + kb * SW:NSTATE + (kb + 1) * SW] = r[:, SW:]
        for sp in range(BWD_STRIPS):
            w = NSTATE // BWD_STRIPS
            re, im = pl.ds(sp * w, w), pl.ds(NSTATE + sp * w, w)
            ar, ai = a_ref[:, re], a_ref[:, im]

            def advance(row, lr, li):
                rows = pl.ds(row, NSEG)
                nr = g_s[rows, re] + ar * lr + ai * li
                ni = g_s[rows, im] - ai * lr + ar * li
                g_s[rows, re] = nr
                g_s[rows, im] = ni
                return nr, ni

            def step(jj, c):
                row = pl.multiple_of((nj - 1 - jj) * NSEG, NSEG)
                nr, ni = advance(row, c[0], c[1])
                if not full:
                    return nr, ni
                prow = pl.ds(pl.multiple_of(row - NSEG, NSEG), NSEG)
                xpr, xpi = xs_ref[prow, re], xs_ref[prow, im]
                return nr, ni, c[2] + (nr * xpr + ni * xpi), c[3] + (ni * xpr - nr * xpi)

            init = (lam_s[:, re], lam_s[:, im])
            if full:
                init = init + (jnp.zeros((NSEG, w), F32), jnp.zeros((NSEG, w), F32))
            c = lax.fori_loop(0, nj - 1, step, init)
            lr, li = advance(0, c[0], c[1])
            if full:
                first = (blk(i) == 0)
                xpr = jnp.where(first, x0_ref[:, re], xh_ref[:, re])
                xpi = jnp.where(first, x0_ref[:, im], xh_ref[:, im])
                da_ref[:, re] += c[2] + (lr * xpr + li * xpi)
                da_ref[:, im] += c[3] + (li * xpr - lr * xpi)
            lam_s[:, re] = lr
            lam_s[:, im] = li
        if full:
            uv = u_ref[...]
            ub = uv.astype(MXU)
            dd_ref[...] += jnp.sum(dyv * uv, axis=0, keepdims=True)
            for kb in range(SBLK):
                cols = slice(kb * LANE, (kb + 1) * LANE)
                re = slice(kb * SW, (kb + 1) * SW)
                im = slice(NSTATE + kb * SW, NSTATE + (kb + 1) * SW)
                lr_b = g_s[:, re].astype(MXU)
                li_b = g_s[:, im].astype(MXU)
                duk = (jnp.dot(lr_b, wbt_ref[kb, :SW, :], preferred_element_type=F32)
                       + jnp.dot(li_b, wbt_ref[kb, SW:, :], preferred_element_type=F32))
                du_ref[:, cols] = duk + d_ref[:, cols] * dyv[:, cols]
                dwb_ref[kb, :, :SW] += _dot_tn(ub[:, cols], lr_b)
                dwb_ref[kb, :, SW:] += _dot_tn(ub[:, cols], li_b)
                dwc_ref[kb, :SW, :] += _dot_tn(xs_ref[:, re].astype(MXU), dyb[:, cols])
                dwc_ref[kb, SW:, :] += _dot_tn(xs_ref[:, im].astype(MXU), dyb[:, cols])
        else:
            @pl.when(i == nb - 1)
            def _():
                f_ref[...] = lam_s[...]

    rev = lambda c: pl.BlockSpec((T, c), lambda i: (blk(i), 0))
    w3 = lambda a, b: pl.BlockSpec((SBLK, a, b), lambda i: (0, 0, 0))
    if full:
        hb = T // NSEG
        in_specs = [rev(D_MODEL), rev(ST2),
                    pl.BlockSpec((NSEG, ST2), lambda i: (jnp.maximum(blk(i) * hb - 1, 0), 0)),
                    rev(D_MODEL), w3(LANE, 2 * SW), w3(2 * SW, LANE), _const_spec(NSEG, ST2),
                    _const_spec(1, D_MODEL), _const_spec(NSEG, ST2), _const_spec(NSEG, ST2)]
        ops = [dy, xs, xs, u, wct, wbt, abc, dskip, x0, l0]
        out_specs = (rev(D_MODEL), _const_spec(NSEG, ST2), w3(LANE, 2 * SW), w3(2 * SW, LANE),
                     _const_spec(1, D_MODEL))
        out_shape = (jax.ShapeDtypeStruct((S, D_MODEL), F32), jax.ShapeDtypeStruct((NSEG, ST2), F32),
                     jax.ShapeDtypeStruct((SBLK, LANE, 2 * SW), F32), jax.ShapeDtypeStruct((SBLK, 2 * SW, LANE), F32),
                     jax.ShapeDtypeStruct((1, D_MODEL), F32))
    else:
        in_specs = [rev(D_MODEL), w3(LANE, 2 * SW), _const_spec(NSEG, ST2), _const_spec(NSEG, ST2)]
        ops = [dy, wct, abc, l0]
        out_specs = _const_spec(NSEG, ST2)
        out_shape = jax.ShapeDtypeStruct((NSEG, ST2), F32)
    return pl.pallas_call(
        body, name=name, grid=(nb,), in_specs=in_specs, out_specs=out_specs, out_shape=out_shape,
        scratch_shapes=[pltpu.VMEM((T, ST2), F32), pltpu.VMEM((NSEG, ST2), F32)],
        compiler_params=_params(big=True),
    )(*ops)


def _s5_discretize(lr, li, log_dt, br, bi):
    dt = jnp.exp(log_dt)[:, None]
    mag = jnp.exp(lr * dt)
    ar = mag * jnp.cos(li * dt)
    ai = mag * jnp.sin(li * dt)
    den = lr * lr + li * li
    nr = ar - 1.0
    coef_r = (nr * lr + ai * li) / den
    coef_i = (ai * lr - nr * li) / den
    bbar_r = coef_r[..., None] * br - coef_i[..., None] * bi
    bbar_i = coef_r[..., None] * bi + coef_i[..., None] * br
    return ar, ai, bbar_r, bbar_i


def _blockdiag(m):
    gpb = SSM_GROUPS // SBLK
    a, b = m.shape[1:]
    mb = m.reshape(SBLK, gpb, a, b)
    eye = jnp.eye(gpb, dtype=m.dtype)
    return jnp.einsum('kgab,gh->kgahb', mb, eye).reshape(SBLK, gpb * a, gpb * b)


def _blockdiag_extract(w, a, b):
    gpb = SSM_GROUPS // SBLK
    w5 = w.reshape(SBLK, gpb, a, gpb, b)
    return jnp.einsum('kgahb,gh->kgab', w5, jnp.eye(gpb, dtype=w.dtype)).reshape(SSM_GROUPS, a, b)


def _cpow(ar, ai, n):
    rr, ri = jnp.ones_like(ar), jnp.zeros_like(ai)
    br, bi = ar, ai
    while n:
        if n & 1:
            rr, ri = rr * br - ri * bi, rr * bi + ri * br
        br, bi = br * br - bi * bi, 2.0 * br * bi
        n >>= 1
    return rr, ri


def _perm(a):
    s, c = a.shape
    return a.reshape(NSEG, s // NSEG, c).transpose(1, 0, 2).reshape(s, c)


def _unperm(a):
    s, c = a.shape
    return a.reshape(s // NSEG, NSEG, c).transpose(1, 0, 2).reshape(s, c)


def _other_chips(x, y):
    return [(1 - x, y), (x, 1 - y), (1 - x, 1 - y)]


def _span(chip, size, align):
    return pl.ds(pl.multiple_of(chip * size, align), size)


def _exchange_ops(pieces, s_refs, o_refs, send_sems, recv_sems, local_sems):
    x, y, c = lax.axis_index("x"), lax.axis_index("y"), lax.axis_index("c")
    me = 2 * x + y
    others = _other_chips(x, y)
    npc = len(pieces)

    def remote(k, p, tx, ty, src_chip, dst_chip):
        si, oi, sv, dv = pieces[p]
        return pltpu.make_async_remote_copy(
            src_ref=sv(s_refs[si], src_chip), dst_ref=dv(o_refs[oi], dst_chip), send_sem=send_sems.at[k, p],
            recv_sem=recv_sems.at[k, p], device_id=(tx, ty, c), device_id_type=MESH)

    def local(p):
        si, oi, sv, dv = pieces[p]
        return pltpu.make_async_copy(sv(s_refs[si], me), dv(o_refs[oi], me), local_sems.at[p])

    def start():
        for p in range(npc):
            local(p).start()
        for k, (tx, ty) in enumerate(others):
            for p in range(npc):
                remote(k, p, tx, ty, 2 * tx + ty, me).start()

    def finish():
        for k, (tx, ty) in enumerate(others):
            for p in range(npc):
                remote(k, p, tx, ty, me, 2 * tx + ty).wait_recv()
        for k, (tx, ty) in enumerate(others):
            for p in range(npc):
                remote(k, p, tx, ty, 2 * tx + ty, me).wait_send()
        for p in range(npc):
            local(p).wait()

    return start, finish


def _exchange_scratch(npc):
    return [pltpu.SemaphoreType.DMA((NCHIP - 1, npc)), pltpu.SemaphoreType.DMA((NCHIP - 1, npc)),
            pltpu.SemaphoreType.DMA((npc,))]


def _chip_exchange(srcs, out_shapes, pieces, *, name):
    ns, no = len(srcs), len(out_shapes)

    def body(*refs):
        start, finish = _exchange_ops(pieces, refs[:ns], refs[ns:ns + no], *refs[ns + no:])
        start()
        finish()

    hbm = pl.BlockSpec(memory_space=pl.ANY)
    return pl.pallas_call(
        body, name=name, in_specs=[hbm] * ns, out_specs=tuple([hbm] * no), out_shape=tuple(out_shapes),
        scratch_shapes=_exchange_scratch(len(pieces)),
    )(*srcs)


def _sibling_exchange(srcs, *, name):
    n = len(srcs)

    def body(*refs):
        s_refs, o_refs, send_sems, recv_sems = refs[:n], refs[n:2 * n], refs[2 * n], refs[2 * n + 1]
        x, y, c = lax.axis_index("x"), lax.axis_index("y"), lax.axis_index("c")
        cps = [pltpu.make_async_remote_copy(src_ref=s_refs[p], dst_ref=o_refs[p], send_sem=send_sems.at[p],
                                            recv_sem=recv_sems.at[p], device_id=(x, y, 1 - c), device_id_type=MESH)
               for p in range(n)]
        for cp in cps:
            cp.start()
        for cp in cps:
            cp.wait()

    hbm = pl.BlockSpec(memory_space=pl.ANY)
    return pl.pallas_call(
        body, name=name, in_specs=[hbm] * n, out_specs=tuple([hbm] * n),
        out_shape=tuple(jax.ShapeDtypeStruct(s.shape, s.dtype) for s in srcs),
        scratch_shapes=[pltpu.SemaphoreType.DMA((n,)), pltpu.SemaphoreType.DMA((n,))],
    )(*srcs)


def _row_tile(r, c, tile_bytes):
    best = 16
    for t in range(16, r + 1, 16):
        if r % t == 0 and t * c * 4 <= tile_bytes:
            best = t
    assert r % best == 0
    return best


def _sum_chips(r, *, name):
    _, R, W = r.shape
    tm = _row_tile(R, W, 2 * 1024 * 1024)

    def body(r_ref, o_ref):
        o_ref[...] = ((r_ref[0].astype(F32) + r_ref[1].astype(F32)) + r_ref[2].astype(F32)) + r_ref[3].astype(F32)

    return pl.pallas_call(
        body, name=name, grid=(R // tm,), in_specs=[pl.BlockSpec((NCHIP, tm, W), lambda i: (0, i, 0))],
        out_specs=_row_spec(tm, W), out_shape=jax.ShapeDtypeStruct((R, W), F32), compiler_params=_params(),
    )(r)


def _adamw(p_mine, p_sib, w, m, v, *, name):
    R, W = w.shape
    tm = _row_tile(R, W, 1024 * 1024)

    def body(a_ref, b_ref, w_ref, m_ref, v_ref, g_ref, d_ref, nm_ref, nv_ref):
        g = a_ref[...] + b_ref[...]
        mm = ADAM_B1 * m_ref[...] + (1.0 - ADAM_B1) * g
        vv = ADAM_B2 * v_ref[...] + (1.0 - ADAM_B2) * (g * g)
        m_hat = mm / (1.0 - ADAM_B1 ** ADAM_STEP)
        v_hat = vv / (1.0 - ADAM_B2 ** ADAM_STEP)
        g_ref[...] = g
        d_ref[...] = -ADAM_LR * (m_hat / (jnp.sqrt(v_hat) + ADAM_EPS) + ADAM_WD * w_ref[...])
        nm_ref[...] = mm
        nv_ref[...] = vv

    spec = _row_spec(tm, W)
    shp = jax.ShapeDtypeStruct((R, W), F32)
    return pl.pallas_call(
        body, name=name, grid=(R // tm,), in_specs=[spec] * 5, out_specs=(spec,) * 4, out_shape=(shp,) * 4,
        compiler_params=_params(),
    )(p_mine, p_sib, w, m, v)


def _pack_small(parts):
    flat = jnp.concatenate([p.reshape(-1) for p in parts])
    pad = (-flat.shape[0]) % (SMALL_ROWS * PACKW)
    return jnp.pad(flat, (0, pad)).reshape(-1, PACKW)


def _unpack_small(buf, shapes):
    flat, out, off = buf.reshape(-1), [], 0
    for shp in shapes:
        sz = math.prod(shp)
        out.append(flat[off:off + sz].reshape(shp))
        off += sz
    return out


def _shard(a, t, ax):
    sz = a.shape[ax] // NCHIP
    return lax.slice_in_dim(a, t * sz, (t + 1) * sz, axis=ax)


MLA_W = ['mla_w_a', 'mla_w_uq', 'mla_w_ukv', 'mla_w_o']
REST_W = ['ssm_w_in', 'ssm_w_glu', 'ffn_w_up', 'ffn_w_down']
SMALL = [n for n in WNAMES if n not in MLA_W + REST_W]


def _gather_plan(w, names, with_small):
    srcs, outs, pieces = [], [], []
    for name in names:
        local = w[name].astype(MXU)
        local = local[0] if local.shape[0] == 1 else local
        ax = SHARD_AXIS[name] - (1 if w[name].shape[0] == 1 else 0)
        full = local.shape[:ax] + (NCHIP * local.shape[ax],) + local.shape[ax + 1:]
        si, oi = len(srcs), len(outs)
        srcs.append(local)
        outs.append(jax.ShapeDtypeStruct(full, MXU))
        size = local.shape[ax]
        if local.ndim == 2:
            if ax == 0:
                pieces.append((si, oi, lambda r, t: r, lambda r, ch, size=size: r.at[_span(ch, size, 8), :]))
            else:
                pieces.append((si, oi, lambda r, t: r, lambda r, ch, size=size: r.at[:, _span(ch, size, LANE)]))
        else:
            for l in range(local.shape[0]):
                if ax == 1:
                    dv = lambda r, ch, l=l, size=size: r.at[l, _span(ch, size, 8), :]
                else:
                    dv = lambda r, ch, l=l, size=size: r.at[l, :, _span(ch, size, LANE)]
                pieces.append((si, oi, lambda r, t, l=l: r.at[l], dv))
    if with_small:
        small = _pack_small([w[n] for n in GATHER_F32])
        srcs.append(small)
        outs.append(jax.ShapeDtypeStruct((NCHIP,) + small.shape, F32))
        pieces.append((len(srcs) - 1, len(outs) - 1, lambda r, t: r, lambda r, ch: r.at[ch]))
    return srcs, outs, pieces


def _gather_result(got, w, names, with_small):
    full = dict(zip(names, got[:len(names)]))
    if with_small:
        per_chip = [_unpack_small(got[-1][t], [w[n].shape for n in GATHER_F32]) for t in range(NCHIP)]
        for j, n in enumerate(GATHER_F32):
            full[n] = jnp.concatenate([per_chip[t][j] for t in range(NCHIP)], axis=SHARD_AXIS[n])
    return full


def _as_rows(a):
    return a.reshape(-1, a.shape[-1])


def _grad_plan(grads, w, names, with_small):
    srcs, outs, pieces = [], [], []
    for name in names:
        local = w[name]
        ax = SHARD_AXIS[name]
        size = local.shape[ax]
        oi = len(outs)
        layers = grads[name] if isinstance(grads[name], list) else [grads[name]]
        outs.append(jax.ShapeDtypeStruct((NCHIP,) + local.shape, layers[0].dtype))
        for l, g in enumerate(layers):
            si = len(srcs)
            srcs.append(g)
            if ax == 1:
                sv = lambda r, t, size=size: r.at[_span(t, size, 8), :]
            else:
                sv = lambda r, t, size=size: r.at[:, _span(t, size, LANE)]
            pieces.append((si, oi, sv, lambda r, ch, l=l: r.at[ch, l]))
    if with_small:
        small = jnp.stack([_pack_small([_shard(grads[n], t, SHARD_AXIS[n]) if n in SHARD_AXIS else grads[n]
                                        for n in SMALL]) for t in range(NCHIP)])
        srcs.append(small)
        outs.append(jax.ShapeDtypeStruct(small.shape, F32))
        pieces.append((len(srcs) - 1, len(outs) - 1, lambda r, t: r.at[t], lambda r, ch: r.at[ch]))
    return srcs, outs, pieces


def _reduce_and_update(names, landed, w, mom, var):
    partial = [_sum_chips(r.reshape(NCHIP, -1, r.shape[-1]), name="grad_sum_chips") for r in landed]
    sibling = _sibling_exchange(partial, name="grad_sibling")
    res = [dict(), dict(), dict(), dict()]
    for j, name in enumerate(names):
        outs4 = _adamw(partial[j], sibling[j], _as_rows(w[name]), _as_rows(mom[name]), _as_rows(var[name]), name="adamw")
        for d, o in zip(res, outs4):
            d[name] = o.reshape(w[name].shape)
    outs4 = _adamw(partial[-1], sibling[-1], *[_pack_small([t[n] for n in SMALL]) for t in (w, mom, var)], name="adamw")
    for d, o in zip(res, outs4):
        d.update(zip(SMALL, _unpack_small(o, [w[n].shape for n in SMALL])))
    return res


def kernel(x, positions, mla_w_a, mla_g_q, mla_g_kv, mla_w_uq, mla_w_ukv, mla_w_o, ssm_w_in, ssm_lambda_re, ssm_lambda_im, ssm_log_dt, ssm_b_re, ssm_b_im, ssm_c_re, ssm_c_im, ssm_d, ssm_w_glu, ffn_w_up, ffn_conv_w, ffn_conv_b, ffn_w_down, g_mix, g_ffn, g_final, loss_target, m_mla_w_a, m_mla_g_q, m_mla_g_kv, m_mla_w_uq, m_mla_w_ukv, m_mla_w_o, m_ssm_w_in, m_ssm_lambda_re, m_ssm_lambda_im, m_ssm_log_dt, m_ssm_b_re, m_ssm_b_im, m_ssm_c_re, m_ssm_c_im, m_ssm_d, m_ssm_w_glu, m_ffn_w_up, m_ffn_conv_w, m_ffn_conv_b, m_ffn_w_down, m_g_mix, m_g_ffn, m_g_final, v_mla_w_a, v_mla_g_q, v_mla_g_kv, v_mla_w_uq, v_mla_w_ukv, v_mla_w_o, v_ssm_w_in, v_ssm_lambda_re, v_ssm_lambda_im, v_ssm_log_dt, v_ssm_b_re, v_ssm_b_im, v_ssm_c_re, v_ssm_c_im, v_ssm_d, v_ssm_w_glu, v_ffn_w_up, v_ffn_conv_w, v_ffn_conv_b, v_ffn_w_down, v_g_mix, v_g_ffn, v_g_final):
    w = dict(zip(WNAMES, (mla_w_a, mla_g_q, mla_g_kv, mla_w_uq, mla_w_ukv, mla_w_o, ssm_w_in, ssm_lambda_re,
                          ssm_lambda_im, ssm_log_dt, ssm_b_re, ssm_b_im, ssm_c_re, ssm_c_im, ssm_d, ssm_w_glu,
                          ffn_w_up, ffn_conv_w, ffn_conv_b, ffn_w_down, g_mix, g_ffn, g_final)))
    mom = dict(zip(WNAMES, (m_mla_w_a, m_mla_g_q, m_mla_g_kv, m_mla_w_uq, m_mla_w_ukv, m_mla_w_o, m_ssm_w_in,
                            m_ssm_lambda_re, m_ssm_lambda_im, m_ssm_log_dt, m_ssm_b_re, m_ssm_b_im, m_ssm_c_re,
                            m_ssm_c_im, m_ssm_d, m_ssm_w_glu, m_ffn_w_up, m_ffn_conv_w, m_ffn_conv_b,
                            m_ffn_w_down, m_g_mix, m_g_ffn, m_g_final)))
    var = dict(zip(WNAMES, (v_mla_w_a, v_mla_g_q, v_mla_g_kv, v_mla_w_uq, v_mla_w_ukv, v_mla_w_o, v_ssm_w_in,
                            v_ssm_lambda_re, v_ssm_lambda_im, v_ssm_log_dt, v_ssm_b_re, v_ssm_b_im, v_ssm_c_re,
                            v_ssm_c_im, v_ssm_d, v_ssm_w_glu, v_ffn_w_up, v_ffn_conv_w, v_ffn_conv_b,
                            v_ffn_w_down, v_g_mix, v_g_ffn, v_g_final)))
    S = x.shape[1]
    D = D_MODEL
    x2 = x.reshape(S, D)
    tgt = loss_target.reshape(S, D)

    fw = _gather_result(_chip_exchange(*_gather_plan(w, MLA_W, False), name="gather_weights"), w, MLA_W, False)
    w_a = jnp.pad(fw['mla_w_a'], ((0, 0), (0, A_PAD - KR0 - QK_ROPE)))
    uq = fw['mla_w_uq'].reshape(Q_LORA, HEADS, QK_DIM)
    w_uq = jnp.concatenate([uq[:, :, :QK_NOPE].reshape(Q_LORA, HEADS * QK_NOPE),
                            jnp.pad(uq[:, :, QK_NOPE:], ((0, 0), (0, 0), (0, LANE - QK_ROPE))).reshape(Q_LORA, HEADS * LANE)],
                           axis=1)
    w_ukv = fw['mla_w_ukv']
    w_o = fw['mla_w_o']
    conv_b = w['ffn_conv_b']
    g_q, g_kv = w['mla_g_q'], w['mla_g_kv']
    gm, gf = w['g_mix'], w['g_ffn']
    gfin = w['g_final'].reshape(1, D)

    inv = 1.0 / (ROPE_THETA ** (jnp.arange(0, QK_ROPE, 2, dtype=F32) / QK_ROPE))
    ang = positions.reshape(S).astype(F32)[:, None] * inv
    cos, sin = jnp.cos(ang), jnp.sin(ang)
    zpad = jnp.zeros((S, LANE - QK_ROPE), F32)
    c128 = jnp.concatenate([cos, cos, zpad], axis=1)
    s128 = jnp.concatenate([-sin, sin, zpad], axis=1)

    hn0 = _rmsnorm_fwd(x2, gm[0:1], name="rms_mix0")
    a = _mm(hn0, w_a, name="mla_a")
    cqn, ckvn, kr = _mla_mid_fwd(a, g_q, g_kv, c128, s128, name="mla_mid_fwd")
    qfull = _mm(cqn, w_uq, name="mla_q")
    kv = _mm(ckvn, w_ukv, out_dtype=MXU, name="mla_kv")
    qs, ks, vs = _qk_prep(qfull, kv, kr, c128, s128, name="qk_prep")
    os_, lses = [], []
    rides = {0: (['ssm_w_in', 'ssm_w_glu', 'ffn_w_down'], True), 1: (['ffn_w_up'], False)}
    for h in range(HEADS):
        if h in rides:
            o_h, lse_h, *got = _attn_fwd(qs[h], ks[h], vs[h], name="attn_fwd_gather", ride=_gather_plan(w, *rides[h]))
            fw.update(_gather_result(got, w, *rides[h]))
        else:
            o_h, lse_h = _attn_fwd(qs[h], ks[h], vs[h], name="attn_fwd")
        os_.append(o_h)
        lses.append(lse_h)
    w_in = fw['ssm_w_in']
    w_glu = fw['ssm_w_glu']
    w_up = fw['ffn_w_up']
    w_down = fw['ffn_w_down']
    conv_w = fw['ffn_conv_w']
    dskip = fw['ssm_d']
    o_cat = jnp.concatenate(os_, axis=1)
    h1 = _mm(o_cat, w_o, res=x2, name="mla_o")

    def ffn_fwd(h, l):
        hn = _rmsnorm_fwd(h, gf[l:l + 1], name="rms_ffn")
        up = _mm(hn, w_up[l], name="ffn_up")
        act, cval, cgate = _convgate_fwd(up, conv_w[l], conv_b[l:l + 1], name="convgate_fwd")
        return _mm(act, w_down[l], res=h, name="ffn_down"), (hn, up, act, cval, cgate)

    h2, saved0 = ffn_fwd(h1, 0)

    lam_re, lam_im, log_dt = w['ssm_lambda_re'][0], w['ssm_lambda_im'][0], w['ssm_log_dt'][0]
    (a_re, a_im, bbar_r, bbar_i), disc_vjp = jax.vjp(_s5_discretize, lam_re, lam_im, log_dt, w['ssm_b_re'][0],
                                                    w['ssm_b_im'][0])
    c_re, c_im = w['ssm_c_re'][0], w['ssm_c_im'][0]
    bt_r, bt_i = jnp.swapaxes(bbar_r, 1, 2), jnp.swapaxes(bbar_i, 1, 2)
    wb = jnp.concatenate([_blockdiag(bt_r), _blockdiag(bt_i)], axis=2).astype(MXU)
    wbt = jnp.concatenate([_blockdiag(bbar_r), _blockdiag(bbar_i)], axis=1).astype(MXU)
    ct_r, ct_i = jnp.swapaxes(c_re, 1, 2), jnp.swapaxes(c_im, 1, 2)
    wc = jnp.concatenate([_blockdiag(ct_r), _blockdiag(-ct_i)], axis=1).astype(MXU)
    wct = jnp.concatenate([_blockdiag(c_re), _blockdiag(-c_im)], axis=2).astype(MXU)
    af_r, af_i = a_re.reshape(NSTATE), a_im.reshape(NSTATE)
    abc = jnp.broadcast_to(jnp.concatenate([af_r, af_i])[None], (NSEG, ST2))
    seg = S // NSEG
    ap_r, ap_i = _cpow(af_r, af_i, seg)

    hn1 = _rmsnorm_fwd(h2, gm[1:2], name="rms_mix1")
    u = _mm(hn1, w_in, name="s5_in")
    u_p = _perm(u)
    zero_state = jnp.zeros((NSEG, ST2), F32)
    ends = _s5_fwd(u_p, wb, wc, abc, dskip, zero_state, full=False, name="s5_fwd_ends")
    inits, cr, ci = [], jnp.zeros((NSTATE,), F32), jnp.zeros((NSTATE,), F32)
    for r in range(NSEG):
        inits.append(jnp.concatenate([cr, ci]))
        er, ei = ends[r, :NSTATE], ends[r, NSTATE:]
        cr, ci = er + ap_r * cr - ap_i * ci, ei + ap_r * ci + ap_i * cr
    x0 = jnp.stack(inits)
    xs, y_p, yg_p = _s5_fwd(u_p, wb, wc, abc, dskip, x0, full=True, name="s5_fwd")
    yg = _unperm(yg_p)
    z = _mm(yg, w_glu, name="s5_glu")
    h3 = _glu_fwd(z, h2, name="glu_fwd")
    h4, saved1 = ffn_fwd(h3, 1)

    loss_l, dh4, dg_final = _loss_head(h4, gfin, tgt, name="loss_head")

    grads = {}

    def ffn_bwd(h_in, g, saved, l):
        hn, up, act, cval, cgate = saved
        w_up_t = w_up[l].T
        dact = _mm(g, w_down[l].T, name="ffn_down_dx")
        dw_down = _mm(act, g, mode="tn", out_dtype=MXU, name="ffn_down_dw")
        duv, dug, dwv, dwg, dbv, dbg = _convgate_bwd(up, cval, cgate, dact, conv_w[l], name="convgate_bwd")
        dw_up = jnp.concatenate([_mm(hn, duv, mode="tn", out_dtype=MXU, name="ffn_up_dw"), _mm(hn, dug, mode="tn", out_dtype=MXU, name="ffn_up_dw")],
                                axis=1)
        dhn = _mm(duv, w_up_t[:D_FF], name="ffn_up_dx")
        dhn = _mm(dug, w_up_t[D_FF:], res=dhn, name="ffn_up_dx_acc")
        dh, dg = _rmsnorm_bwd(h_in, gf[l:l + 1], dhn, g, name="rms_ffn_bwd")
        return dh, dict(w_up=dw_up, w_down=dw_down, conv_w=jnp.concatenate([dwv, dwg], axis=1),
                        conv_b=jnp.concatenate([dbv, dbg], axis=1)[0], g_ffn=dg[0])

    dh3, fg1 = ffn_bwd(h3, dh4, saved1, 1)

    dz = _glu_bwd(z, dh3, name="glu_bwd")
    grads['ssm_w_glu'] = _mm(yg, dz, mode="tn", out_dtype=MXU, name="s5_glu_dw")
    dyg = _mm(dz, w_glu.T, name="s5_glu_dx")
    dy_p = _gelu_bwd(y_p, _perm(dyg), name="gelu_bwd")
    firsts = _s5_bwd(dy_p, None, None, wct, None, abc, None, None, zero_state, full=False, name="s5_bwd_firsts")
    linits, cr, ci = [None] * NSEG, jnp.zeros((NSTATE,), F32), jnp.zeros((NSTATE,), F32)
    for r in reversed(range(NSEG)):
        linits[r] = jnp.concatenate([cr, ci])
        fr, fi = firsts[r, :NSTATE], firsts[r, NSTATE:]
        cr, ci = fr + ap_r * cr + ap_i * ci, fi + ap_r * ci - ap_i * cr
    l0 = jnp.stack(linits)
    du_p, dab, dwb, dwc, dd = _s5_bwd(dy_p, xs, u_p, wct, wbt, abc, dskip, x0, l0, full=True, name="s5_bwd")
    du = _unperm(du_p)
    grads['ssm_w_in'] = _mm(hn1, du, mode="tn", out_dtype=MXU, name="s5_in_dw")
    dhn1 = _mm(du, w_in.T, name="s5_in_dx")
    dh2, dg_mix1 = _rmsnorm_bwd(h2, gm[1:2], dhn1, dh3, name="rms_mix_bwd")
    da_sum = jnp.sum(dab, axis=0)
    dbt_r = _blockdiag_extract(dwb[:, :, :SW], SSM_GROUP, SSM_STATE)
    dbt_i = _blockdiag_extract(dwb[:, :, SW:], SSM_GROUP, SSM_STATE)
    dlr, dli, dlog_dt, dbr, dbi = disc_vjp((da_sum[:NSTATE].reshape(SSM_GROUPS, SSM_STATE),
                                            da_sum[NSTATE:].reshape(SSM_GROUPS, SSM_STATE),
                                            jnp.swapaxes(dbt_r, 1, 2), jnp.swapaxes(dbt_i, 1, 2)))
    dct_r = _blockdiag_extract(dwc[:, :SW, :], SSM_STATE, SSM_GROUP)
    dct_i = _blockdiag_extract(dwc[:, SW:, :], SSM_STATE, SSM_GROUP)
    grads['ssm_lambda_re'], grads['ssm_lambda_im'], grads['ssm_log_dt'] = dlr[None], dli[None], dlog_dt[None]
    grads['ssm_b_re'], grads['ssm_b_im'] = dbr[None], dbi[None]
    grads['ssm_c_re'] = jnp.swapaxes(dct_r, 1, 2)[None]
    grads['ssm_c_im'] = -jnp.swapaxes(dct_i, 1, 2)[None]
    grads['ssm_d'] = dd

    dh1, fg0 = ffn_bwd(h1, dh2, saved0, 0)
    grads['ffn_w_up'] = [fg0['w_up'], fg1['w_up']]
    grads['ffn_w_down'] = [fg0['w_down'], fg1['w_down']]
    grads['ffn_conv_w'] = jnp.stack([fg0['conv_w'], fg1['conv_w']])
    grads['ffn_conv_b'] = jnp.stack([fg0['conv_b'], fg1['conv_b']])
    grads['g_ffn'] = jnp.stack([fg0['g_ffn'], fg1['g_ffn']])

    do_cat = _mm(dh1, w_o.T, out_dtype=MXU, name="mla_o_dx")
    grads['mla_w_o'] = _mm(o_cat, dh1, mode="tn", out_dtype=MXU, name="mla_o_dw")
    dqs, dks, dvs = [], [], []
    for h in range(HEADS):
        do_h = do_cat[:, h * V_HEAD:(h + 1) * V_HEAD]
        delta = _attn_delta(do_h, os_[h], name="attn_delta")
        tiles = (S // _rows(S, BWD_T), 1, _rows(S, BWD_T))
        lse2 = (lses[h] * math.log2(math.e)).reshape(tiles)
        if h == 0:
            dq_h, dk_h, dv_h, *landed = _attn_bwd(qs[h], ks[h], vs[h], do_h, lse2, delta.reshape(tiles),
                                                  name="attn_bwd_exchange", ride=_grad_plan(grads, w, REST_W, False))
        else:
            dq_h, dk_h, dv_h = _attn_bwd(qs[h], ks[h], vs[h], do_h, lse2, delta.reshape(tiles), name="attn_bwd")
        dqs.append(dq_h)
        dks.append(dk_h)
        dvs.append(dv_h)
    dqfull, dkv, dkr = _qk_prep_bwd(dqs, dks, dvs, c128, s128, name="qk_prep_bwd")
    dw_uq_p = _mm(cqn, dqfull, mode="tn", out_dtype=MXU, name="mla_q_dw")
    dcqn = _mm(dqfull, w_uq.T, name="mla_q_dx")
    grads['mla_w_ukv'] = _mm(ckvn, dkv, mode="tn", out_dtype=MXU, name="mla_kv_dw")
    dckvn = _mm(dkv, w_ukv.T, name="mla_kv_dx")
    da, dgq, dgkv = _mla_mid_bwd(a, dcqn, dckvn, dkr, g_q, g_kv, c128, s128, name="mla_mid_bwd")
    grads['mla_w_a'] = _mm(hn0, da, mode="tn", out_dtype=MXU, name="mla_a_dw")[:, :KR0 + QK_ROPE]
    dhn0 = _mm(da, w_a.T, name="mla_a_dx")
    dx, dg_mix0 = _rmsnorm_bwd(x2, gm[0:1], dhn0, dh1, name="rms_mix_bwd")
    grads['mla_w_uq'] = jnp.concatenate(
        [dw_uq_p[:, :HEADS * QK_NOPE].reshape(Q_LORA, HEADS, QK_NOPE),
         dw_uq_p[:, HEADS * QK_NOPE:].reshape(Q_LORA, HEADS, LANE)[:, :, :QK_ROPE]], axis=2).reshape(Q_LORA, HEADS * QK_DIM)
    grads['mla_g_q'], grads['mla_g_kv'] = dgq, dgkv
    grads['g_mix'] = jnp.concatenate([dg_mix0, dg_mix1], axis=0)
    grads['g_final'] = dg_final[0]

    landed += _chip_exchange(*_grad_plan(grads, w, MLA_W, True), name="grad_exchange")
    g_out, d_out, m_out, v_out = _reduce_and_update(REST_W + MLA_W, landed, w, mom, var)

    loss = lax.psum(loss_l[0, 0], ("x", "y", "c"))
    return (loss, dx.reshape(1, S, D), *[g_out[n] for n in WNAMES], *[d_out[n] for n in WNAMES],
            *[m_out[n] for n in WNAMES], *[v_out[n] for n in WNAMES])
```

```python
import math

import jax
import jax.numpy as jnp
from jax import lax
from jax.experimental import pallas as pl
from jax.experimental.pallas import tpu as pltpu

F32 = jnp.float32
MXU = jnp.bfloat16

D_MODEL = 1024
CHUNK = 64
EPS = 1e-6
HEADS = 8
QK_NOPE = 128
QK_ROPE = 64
V_HEAD = 128
Q_LORA = 384
KV_LORA = 256
ROPE_THETA = 10000.0
QK_DIM = QK_NOPE + QK_ROPE
SSM_GROUP = 16
SSM_GROUPS = D_MODEL // SSM_GROUP
SSM_STATE = 64
NSTATE = SSM_GROUPS * SSM_STATE
D_FF = 2816
ATT_SCALE = QK_DIM ** -0.5
EXP2_SCALE = ATT_SCALE * math.log2(math.e)
NEG = -1e30
NSEG = 8
SBLK = 8

ADAM_LR = 0.001
ADAM_B1 = 0.9
ADAM_B2 = 0.999
ADAM_EPS = 1e-08
ADAM_WD = 0.01
ADAM_STEP = 10

LANE = 128
TN_MAX_COLS = 2816
NN_MAX_COLS = 1408
NN_MAX_K = 2816
TN_ACC_ELEMS = 1536 * 1024
VMEM_BIG = 56 * 1024 * 1024

WNAMES = ['mla_w_a', 'mla_g_q', 'mla_g_kv', 'mla_w_uq', 'mla_w_ukv', 'mla_w_o', 'ssm_w_in', 'ssm_lambda_re',
          'ssm_lambda_im', 'ssm_log_dt', 'ssm_b_re', 'ssm_b_im', 'ssm_c_re', 'ssm_c_im', 'ssm_d', 'ssm_w_glu',
          'ffn_w_up', 'ffn_conv_w', 'ffn_conv_b', 'ffn_w_down', 'g_mix', 'g_ffn', 'g_final']
FWD_NAMES = ['x', 'positions'] + WNAMES
SHARD_AXIS = {'mla_w_a': 1, 'mla_w_uq': 2, 'mla_w_ukv': 2, 'mla_w_o': 1, 'ssm_w_in': 1, 'ssm_d': 1,
              'ssm_w_glu': 2, 'ffn_w_up': 2, 'ffn_conv_w': 2, 'ffn_w_down': 1}
GATHER_F32 = ['ssm_d', 'ffn_conv_w']
NCHIP = 4
PACKW = 1024
SMALL_ROWS = 64
MESH = pl.DeviceIdType.MESH


def _tile(d, pref):
    t = min(pref, d) // LANE * LANE
    while t >= LANE:
        if d % t == 0:
            return t
        t -= LANE
    return d


def _rows(s, pref):
    t = min(s, pref)
    assert s % t == 0 and t % 8 == 0
    return t


def _params(big=False):
    if big:
        return pltpu.CompilerParams(vmem_limit_bytes=VMEM_BIG)
    return pltpu.CompilerParams(vmem_limit_bytes=40 * 1024 * 1024)


def _mm(a, b, *, name, mode="nn", out_dtype=F32, res=None, tm=None, tn=None, tk=None):
    if mode == "nn":
        M, K = a.shape
        if b.shape[1] > 1024 and b.shape[1] % 512:
            tn = tn or b.shape[1]
            tm = tm or _rows(M, 512)
        tm = tm or _rows(M, 1024)
        tk = tk or _tile(K, NN_MAX_K if jnp.dtype(a.dtype).itemsize == 2 else NN_MAX_COLS)
    else:
        K, M = a.shape
        tn = tn or _tile(b.shape[1], TN_MAX_COLS)
        tm = tm or _tile(M, max(LANE, TN_ACC_ELEMS // tn))
    N = b.shape[1]
    assert b.shape[0] == K
    tn = tn or _tile(N, NN_MAX_COLS if N % 512 else 512)
    tk = tk or _rows(K, 512)
    nk = K // tk
    has_res = res is not None

    def body(a_ref, b_ref, *rest):
        if has_res:
            r_ref, o_ref, acc = rest
        else:
            o_ref, acc = rest
        k = pl.program_id(2)

        @pl.when(k == 0)
        def _():
            acc[...] = jnp.zeros_like(acc)

        av = a_ref[...].astype(MXU)
        bv = b_ref[...].astype(MXU)
        if mode == "nn":
            acc[...] += jnp.dot(av, bv, preferred_element_type=F32)
        else:
            acc[...] += lax.dot_general(av, bv, (((0,), (0,)), ((), ())), preferred_element_type=F32)

        @pl.when(k == nk - 1)
        def _():
            o = acc[...]
            if has_res:
                o = o + r_ref[...]
            o_ref[...] = o.astype(o_ref.dtype)

    if mode == "nn":
        a_spec = pl.BlockSpec((tm, tk), lambda i, j, k: (i, k))
    else:
        a_spec = pl.BlockSpec((tk, tm), lambda i, j, k: (k, i))
    in_specs = [a_spec, pl.BlockSpec((tk, tn), lambda i, j, k: (k, j))]
    ops = [a, b]
    if has_res:
        in_specs.append(pl.BlockSpec((tm, tn), lambda i, j, k: (i, j)))
        ops.append(res)
    return pl.pallas_call(
        body, name=name, grid=(M // tm, N // tn, nk), in_specs=in_specs,
        out_specs=pl.BlockSpec((tm, tn), lambda i, j, k: (i, j)),
        out_shape=jax.ShapeDtypeStruct((M, N), out_dtype),
        scratch_shapes=[pltpu.VMEM((tm, tn), F32)], compiler_params=_params(),
    )(*ops)


def _row_spec(tm, c):
    return pl.BlockSpec((tm, c), lambda i: (i, 0))


def _const_spec(r, c):
    return pl.BlockSpec((r, c), lambda i: (0, 0))


def _rms_parts(xv):
    r = lax.rsqrt(jnp.mean(xv * xv, axis=-1, keepdims=True) + EPS)
    return r, xv * r


def _rms_vjp(xv, gv, dyv):
    r, xhat = _rms_parts(xv)
    gy = dyv * gv
    dx = r * (gy - xhat * jnp.mean(gy * xhat, axis=-1, keepdims=True))
    return dx, dyv * xhat


def _rmsnorm_fwd(x, g, *, name):
    S, D = x.shape
    tm = _rows(S, 512)

    def body(x_ref, g_ref, o_ref):
        _, xhat = _rms_parts(x_ref[...])
        o_ref[...] = (xhat * g_ref[...]).astype(o_ref.dtype)

    return pl.pallas_call(
        body, name=name, grid=(S // tm,), in_specs=[_row_spec(tm, D), _const_spec(1, D)],
        out_specs=_row_spec(tm, D), out_shape=jax.ShapeDtypeStruct((S, D), MXU), compiler_params=_params(),
    )(x, g)


def _rmsnorm_bwd(x, g, dy, dres, *, name):
    S, D = x.shape
    tm = _rows(S, 512)

    def body(x_ref, g_ref, dy_ref, dr_ref, dx_ref, dg_ref):
        @pl.when(pl.program_id(0) == 0)
        def _():
            dg_ref[...] = jnp.zeros_like(dg_ref)

        dx, dgp = _rms_vjp(x_ref[...], g_ref[...], dy_ref[...])
        dx_ref[...] = dr_ref[...] + dx
        dg_ref[...] += jnp.sum(dgp, axis=0, keepdims=True)

    return pl.pallas_call(
        body, name=name, grid=(S // tm,),
        in_specs=[_row_spec(tm, D), _const_spec(1, D), _row_spec(tm, D), _row_spec(tm, D)],
        out_specs=(_row_spec(tm, D), _const_spec(1, D)),
        out_shape=(jax.ShapeDtypeStruct((S, D), F32), jax.ShapeDtypeStruct((1, D), F32)),
        compiler_params=_params(),
    )(x, g, dy, dres)


def _loss_head(h, g, tgt, *, name):
    S, D = h.shape
    tm = _rows(S, 512)

    def body(h_ref, g_ref, t_ref, l_ref, dh_ref, dg_ref):
        @pl.when(pl.program_id(0) == 0)
        def _():
            l_ref[...] = jnp.zeros_like(l_ref)
            dg_ref[...] = jnp.zeros_like(dg_ref)

        hv = h_ref[...]
        gv = g_ref[...]
        _, xhat = _rms_parts(hv)
        e = xhat * gv - t_ref[...]
        l_ref[...] += 0.5 * jnp.sum(jnp.mean(e * e, axis=-1, keepdims=True), axis=0, keepdims=True)
        dx, dgp = _rms_vjp(hv, gv, e * (1.0 / D))
        dh_ref[...] = dx
        dg_ref[...] += jnp.sum(dgp, axis=0, keepdims=True)

    return pl.pallas_call(
        body, name=name, grid=(S // tm,),
        in_specs=[_row_spec(tm, D), _const_spec(1, D), _row_spec(tm, D)],
        out_specs=(_const_spec(1, 1), _row_spec(tm, D), _const_spec(1, D)),
        out_shape=(jax.ShapeDtypeStruct((1, 1), F32), jax.ShapeDtypeStruct((S, D), F32),
                   jax.ShapeDtypeStruct((1, D), F32)),
        compiler_params=_params(),
    )(h, g, tgt)


def _swap_halves(g):
    lane = lax.broadcasted_iota(jnp.int32, g.shape, 1)
    return jnp.where(lane < QK_ROPE // 2, pltpu.roll(g, LANE - QK_ROPE // 2, axis=1),
                     pltpu.roll(g, QK_ROPE // 2, axis=1))


def _rope128(g, c128, s128):
    return g * c128 + _swap_halves(g) * s128


def _rope128_vjp(dy, c128, s128):
    lane = lax.broadcasted_iota(jnp.int32, dy.shape, 1)
    return jnp.where(lane < QK_ROPE, dy * c128 + _swap_halves(dy * s128), 0.0)


A_PAD = 768
KR0 = Q_LORA + KV_LORA


def _mla_mid_fwd(a, g_q, g_kv, c128, s128, *, name):
    S = a.shape[0]
    tm = _rows(S, 512)

    def body(a_ref, gq_ref, gkv_ref, c_ref, s_ref, cq_ref, ckv_ref, kr_ref):
        av = a_ref[...]
        _, qh = _rms_parts(av[:, :Q_LORA])
        cq_ref[...] = (qh * gq_ref[...]).astype(cq_ref.dtype)
        _, kh = _rms_parts(av[:, Q_LORA:KR0])
        ckv_ref[...] = (kh * gkv_ref[...]).astype(ckv_ref.dtype)
        kr = _rope128(av[:, KR0:A_PAD], c_ref[...], s_ref[...])
        kr_ref[...] = kr[:, :QK_ROPE].astype(kr_ref.dtype)

    return pl.pallas_call(
        body, name=name, grid=(S // tm,),
        in_specs=[_row_spec(tm, A_PAD), _const_spec(1, Q_LORA), _const_spec(1, KV_LORA), _row_spec(tm, LANE),
                  _row_spec(tm, LANE)],
        out_specs=(_row_spec(tm, Q_LORA), _row_spec(tm, KV_LORA), _row_spec(tm, QK_ROPE)),
        out_shape=(jax.ShapeDtypeStruct((S, Q_LORA), MXU), jax.ShapeDtypeStruct((S, KV_LORA), MXU),
                   jax.ShapeDtypeStruct((S, QK_ROPE), MXU)),
        compiler_params=_params(),
    )(a, g_q, g_kv, c128, s128)


def _mla_mid_bwd(a, dcq, dckv, dkr, g_q, g_kv, c128, s128, *, name):
    S = a.shape[0]
    tm = _rows(S, 512)

    def body(a_ref, dcq_ref, dckv_ref, dkr_ref, gq_ref, gkv_ref, c_ref, s_ref, da_ref, dgq_ref, dgkv_ref):
        @pl.when(pl.program_id(0) == 0)
        def _():
            dgq_ref[...] = jnp.zeros_like(dgq_ref)
            dgkv_ref[...] = jnp.zeros_like(dgkv_ref)

        av = a_ref[...]
        dx, dgp = _rms_vjp(av[:, :Q_LORA], gq_ref[...], dcq_ref[...])
        da_ref[:, :Q_LORA] = dx.astype(da_ref.dtype)
        dgq_ref[...] += jnp.sum(dgp, axis=0, keepdims=True)
        dx, dgp = _rms_vjp(av[:, Q_LORA:KR0], gkv_ref[...], dckv_ref[...])
        da_ref[:, Q_LORA:KR0] = dx.astype(da_ref.dtype)
        dgkv_ref[...] += jnp.sum(dgp, axis=0, keepdims=True)
        da_ref[:, KR0:A_PAD] = _rope128_vjp(dkr_ref[...], c_ref[...], s_ref[...]).astype(da_ref.dtype)

    return pl.pallas_call(
        body, name=name, grid=(S // tm,),
        in_specs=[_row_spec(tm, A_PAD), _row_spec(tm, Q_LORA), _row_spec(tm, KV_LORA), _row_spec(tm, LANE),
                  _const_spec(1, Q_LORA), _const_spec(1, KV_LORA), _row_spec(tm, LANE), _row_spec(tm, LANE)],
        out_specs=(_row_spec(tm, A_PAD), _const_spec(1, Q_LORA), _const_spec(1, KV_LORA)),
        out_shape=(jax.ShapeDtypeStruct((S, A_PAD), MXU), jax.ShapeDtypeStruct((1, Q_LORA), F32),
                   jax.ShapeDtypeStruct((1, KV_LORA), F32)),
        compiler_params=_params(),
    )(a, dcq, dckv, dkr, g_q, g_kv, c128, s128)


QF = 2 * HEADS * LANE
KVF = HEADS * (QK_NOPE + V_HEAD)
VX = 2 * V_HEAD


def _qk_prep(qfull, kv, kr, c128, s128, *, name):
    S = qfull.shape[0]
    tm = _rows(S, 256)

    def body(q_ref, kv_ref, kr_ref, c_ref, s_ref, *outs):
        qo, ko, vo = outs[:HEADS], outs[HEADS:2 * HEADS], outs[2 * HEADS:]
        cv, sv = c_ref[...], s_ref[...]
        krv = kr_ref[...]
        for h in range(HEADS):
            qo[h][:, :QK_NOPE] = q_ref[:, h * LANE:(h + 1) * LANE].astype(MXU)
            g = q_ref[:, (HEADS + h) * LANE:(HEADS + h + 1) * LANE]
            qo[h][:, QK_NOPE:] = _rope128(g, cv, sv)[:, :QK_ROPE].astype(MXU)
            ko[h][:, :QK_NOPE] = kv_ref[:, 2 * h * LANE:(2 * h + 1) * LANE]
            ko[h][:, QK_NOPE:] = krv
            vo[h][:, :V_HEAD] = kv_ref[:, (2 * h + 1) * LANE:(2 * h + 2) * LANE]
            vo[h][:, V_HEAD:] = jnp.ones((tm, VX - V_HEAD), MXU)

    shapes = ([jax.ShapeDtypeStruct((S, QK_DIM), MXU)] * (2 * HEADS)
              + [jax.ShapeDtypeStruct((S, VX), MXU)] * HEADS)
    specs = [_row_spec(tm, QK_DIM)] * (2 * HEADS) + [_row_spec(tm, VX)] * HEADS
    outs = pl.pallas_call(
        body, name=name, grid=(S // tm,),
        in_specs=[_row_spec(tm, QF), _row_spec(tm, KVF), _row_spec(tm, QK_ROPE), _row_spec(tm, LANE),
                  _row_spec(tm, LANE)],
        out_specs=tuple(specs), out_shape=tuple(shapes), compiler_params=_params(),
    )(qfull, kv, kr, c128, s128)
    return outs[:HEADS], outs[HEADS:2 * HEADS], outs[2 * HEADS:]


def _qk_prep_bwd(dqs, dks, dvs, c128, s128, *, name):
    S = dqs[0].shape[0]
    tm = _rows(S, 256)

    def body(*refs):
        dq = refs[:HEADS]
        dk = refs[HEADS:2 * HEADS]
        dv = refs[2 * HEADS:3 * HEADS]
        c_ref, s_ref, dqf_ref, dkv_ref, dkr_ref, tmp = refs[3 * HEADS:]
        cv, sv = c_ref[...], s_ref[...]
        tmp[...] = jnp.zeros_like(tmp)
        dkr_ref[...] = jnp.zeros_like(dkr_ref)
        for h in range(HEADS):
            dqf_ref[:, h * LANE:(h + 1) * LANE] = dq[h][:, :QK_NOPE].astype(MXU)
            tmp[:, :QK_ROPE] = dq[h][:, QK_NOPE:]
            dqf_ref[:, (HEADS + h) * LANE:(HEADS + h + 1) * LANE] = _rope128_vjp(tmp[...], cv, sv).astype(MXU)
            dkv_ref[:, 2 * h * LANE:(2 * h + 1) * LANE] = dk[h][:, :QK_NOPE].astype(MXU)
            dkv_ref[:, (2 * h + 1) * LANE:(2 * h + 2) * LANE] = dv[h][...].astype(MXU)
            dkr_ref[:, :QK_ROPE] += dk[h][:, QK_NOPE:]

    return pl.pallas_call(
        body, name=name, grid=(S // tm,),
        in_specs=[_row_spec(tm, QK_DIM)] * (2 * HEADS) + [_row_spec(tm, V_HEAD)] * HEADS
        + [_row_spec(tm, LANE), _row_spec(tm, LANE)],
        out_specs=(_row_spec(tm, QF), _row_spec(tm, KVF), _row_spec(tm, LANE)),
        out_shape=(jax.ShapeDtypeStruct((S, QF), MXU), jax.ShapeDtypeStruct((S, KVF), MXU),
                   jax.ShapeDtypeStruct((S, LANE), F32)),
        scratch_shapes=[pltpu.VMEM((tm, LANE), F32)], compiler_params=_params(),
    )(*dqs, *dks, *dvs, c128, s128)


def _dot_nt(a, b):
    return lax.dot_general(a, b, (((1,), (1,)), ((), ())), preferred_element_type=F32)


def _dot_tn(a, b):
    return lax.dot_general(a, b, (((0,), (0,)), ((), ())), preferred_element_type=F32)


def _attn_fwd(q, k, vx, *, name, ride=None):
    S = q.shape[0]
    T = _rows(S, 1024)
    n = S // T
    cpt = T // CHUNK

    r_srcs, r_outs, r_pieces = ride or ((), (), ())
    ns, no = len(r_srcs), len(r_outs)

    def body(q_ref, k_ref, v_ref, *rest):
        o_ref, lse_ref = rest[ns:ns + 2]
        s_buf, p_buf, a_buf, m_s, acc_s = rest[ns + 2 + no:ns + 7 + no]
        i = pl.program_id(0)
        if ride:
            start, finish = _exchange_ops(r_pieces, rest[:ns], rest[ns + 2:ns + 2 + no], *rest[ns + 7 + no:])
            pl.when(i == 0)(start)
        qc = lax.broadcasted_iota(jnp.int32, (T, T), 0) // CHUNK
        kc = lax.broadcasted_iota(jnp.int32, (T, T), 1) // CHUNK
        dchunk = kc - qc

        def tile_rows(b):
            return pl.ds(pl.multiple_of(jnp.clip(b, 0, n - 1) * T, T), T)

        def scores(b, slot):
            s = _dot_nt(q_ref[...], k_ref[tile_rows(b), :])
            s_buf[slot] = jnp.where(dchunk <= (i - b) * cpt, s, NEG)

        def softmax(slot):
            s = s_buf[slot]
            m_prev = m_s[...]
            m_new = jnp.maximum(m_prev, jnp.max(s, axis=1, keepdims=True))
            a_buf[slot] = jnp.exp2((m_prev - m_new) * EXP2_SCALE)
            p_buf[slot] = jnp.exp2((s - m_new) * EXP2_SCALE).astype(MXU)
            m_s[...] = m_new

        def pv(b, slot):
            acc_s[...] = a_buf[slot] * acc_s[...] + jnp.dot(p_buf[slot], v_ref[tile_rows(b), :],
                                                              preferred_element_type=F32)

        m_s[...] = jnp.full_like(m_s, NEG)
        acc_s[...] = jnp.zeros_like(acc_s)
        p_buf[1] = jnp.zeros((T, T), MXU)
        a_buf[1] = jnp.ones((T, 1), F32)
        scores(0, 0)

        def pair(u, carry):
            t = 2 * u
            scores(t + 1, 1)
            softmax(0)
            pv(t - 1, 1)
            scores(t + 2, 0)
            softmax(1)
            pv(t, 0)
            return carry

        npairs = (i + 2) // 2
        lax.fori_loop(0, npairs, pair, 0)
        pv(2 * npairs - 1, 1)
        acc = acc_s[...]
        l = acc[:, V_HEAD:V_HEAD + 1]
        o_ref[...] = (acc[:, :V_HEAD] / l).astype(o_ref.dtype)
        lse_ref[...] = m_s[...] * ATT_SCALE + jnp.log(l)
        if ride:
            pl.when(i == n - 1)(finish)

    hbm = pl.BlockSpec(memory_space=pl.ANY)
    return pl.pallas_call(
        body, name=name, grid=(n,),
        in_specs=[pl.BlockSpec((T, QK_DIM), lambda i: (i, 0)), pl.BlockSpec((S, QK_DIM), lambda i: (0, 0)),
                  pl.BlockSpec((S, VX), lambda i: (0, 0))] + [hbm] * ns,
        out_specs=(pl.BlockSpec((T, V_HEAD), lambda i: (i, 0)), pl.BlockSpec((T, 1), lambda i: (i, 0))) + (hbm,) * no,
        out_shape=(jax.ShapeDtypeStruct((S, V_HEAD), MXU), jax.ShapeDtypeStruct((S, 1), F32)) + tuple(r_outs),
        scratch_shapes=[pltpu.VMEM((2, T, T), F32), pltpu.VMEM((2, T, T), MXU), pltpu.VMEM((2, T, 1), F32),
                        pltpu.VMEM((T, 1), F32), pltpu.VMEM((T, VX), F32)]
        + (_exchange_scratch(len(r_pieces)) if ride else []),
        compiler_params=_params(big=True),
    )(q, k, vx, *r_srcs)


def _attn_delta(do, o, *, name):
    S = do.shape[0]
    tm = _rows(S, 1024)

    def body(do_ref, o_ref, d_ref):
        d_ref[...] = jnp.sum(do_ref[...].astype(F32) * o_ref[...].astype(F32), axis=1, keepdims=True)

    return pl.pallas_call(
        body, name=name, grid=(S // tm,), in_specs=[_row_spec(tm, V_HEAD), _row_spec(tm, V_HEAD)],
        out_specs=_row_spec(tm, 1), out_shape=jax.ShapeDtypeStruct((S, 1), F32), compiler_params=_params(),
    )(do, o)


BWD_T = 512


def _attn_bwd(q, k, v, do, lse2, delta, *, name, ride=None):
    S = q.shape[0]
    T = _rows(S, BWD_T)
    n = S // T
    cpt = T // CHUNK

    r_srcs, r_outs, r_pieces = ride or ((), (), ())
    ns, no = len(r_srcs), len(r_outs)

    def body(q_hbm, k_ref, v_ref, do_hbm, lse_ref, dl_ref, *rest):
        dq_hbm, dk_ref, dv_ref = rest[ns:ns + 3]
        q_res, do_res, dq_s, s_buf, dp_buf, p_buf, ds_buf, dk_s, dv_s = rest[ns + 3 + no:ns + 12 + no]
        j = pl.program_id(0)
        if ride:
            start, finish = _exchange_ops(r_pieces, rest[:ns], rest[ns + 3:ns + 3 + no], *rest[ns + 12 + no:])
            pl.when(j == 0)(start)

        @pl.when(j == 0)
        def _():
            pltpu.sync_copy(q_hbm, q_res)
            pltpu.sync_copy(do_hbm, do_res)
            dq_s[...] = jnp.zeros_like(dq_s)

        kc = lax.broadcasted_iota(jnp.int32, (T, T), 0) // CHUNK
        qc = lax.broadcasted_iota(jnp.int32, (T, T), 1) // CHUNK
        dchunk = kc - qc

        def tile(t):
            return jnp.clip(j + t, 0, n - 1)

        def rows(t):
            return pl.ds(pl.multiple_of(tile(t) * T, T), T)

        def scores(t, slot):
            visible_up_to = jnp.where(j + t < n, t * cpt, -2 * cpt)
            s = _dot_nt(k_ref[...], q_res[rows(t), :])
            s_buf[slot] = jnp.where(dchunk <= visible_up_to, s, NEG)
            dp_buf[slot] = _dot_nt(v_ref[...], do_res[rows(t), :])

        def probs(t, slot):
            pt = jnp.exp2(s_buf[slot] * EXP2_SCALE - lse_ref[tile(t)])
            p_buf[slot] = pt.astype(MXU)
            ds_buf[slot] = (pt * (dp_buf[slot] - dl_ref[tile(t)]) * ATT_SCALE).astype(MXU)

        def grads(t, slot):
            r = rows(t)
            dv_s[...] += jnp.dot(p_buf[slot], do_res[r, :], preferred_element_type=F32)
            ds = ds_buf[slot]
            dk_s[...] += jnp.dot(ds, q_res[r, :], preferred_element_type=F32)
            dq_s[r, :] += _dot_tn(ds, k_ref[...])

        dk_s[...] = jnp.zeros_like(dk_s)
        dv_s[...] = jnp.zeros_like(dv_s)
        p_buf[1] = jnp.zeros((T, T), MXU)
        ds_buf[1] = jnp.zeros((T, T), MXU)
        scores(0, 0)

        def pair(u, carry):
            t = 2 * u
            scores(t + 1, 1)
            probs(t, 0)
            grads(t - 1, 1)
            scores(t + 2, 0)
            probs(t + 1, 1)
            grads(t, 0)
            return carry

        npairs = (n - j + 1) // 2
        lax.fori_loop(0, npairs, pair, 0)
        grads(2 * npairs - 1, 1)
        dk_ref[...] = dk_s[...]
        dv_ref[...] = dv_s[...]

        @pl.when(j == n - 1)
        def _():
            pltpu.sync_copy(dq_s, dq_hbm)
            if ride:
                finish()

    hbm = pl.BlockSpec(memory_space=pl.ANY)
    k_map = lambda j: (j, 0)
    whole = pl.BlockSpec((n, 1, T), lambda j: (0, 0, 0))
    return pl.pallas_call(
        body, name=name, grid=(n,),
        in_specs=[hbm, pl.BlockSpec((T, QK_DIM), k_map), pl.BlockSpec((T, V_HEAD), k_map), hbm, whole, whole]
        + [hbm] * ns,
        out_specs=(hbm, pl.BlockSpec((T, QK_DIM), k_map), pl.BlockSpec((T, V_HEAD), k_map)) + (hbm,) * no,
        out_shape=(jax.ShapeDtypeStruct((S, QK_DIM), F32), jax.ShapeDtypeStruct((S, QK_DIM), F32),
                   jax.ShapeDtypeStruct((S, V_HEAD), F32)) + tuple(r_outs),
        scratch_shapes=[pltpu.VMEM((S, QK_DIM), MXU), pltpu.VMEM((S, V_HEAD), MXU), pltpu.VMEM((S, QK_DIM), F32),
                        pltpu.VMEM((2, T, T), F32), pltpu.VMEM((2, T, T), F32), pltpu.VMEM((2, T, T), MXU),
                        pltpu.VMEM((2, T, T), MXU), pltpu.VMEM((T, QK_DIM), F32), pltpu.VMEM((T, V_HEAD), F32)]
        + (_exchange_scratch(len(r_pieces)) if ride else []),
        compiler_params=_params(big=True),
    )(q, k, v, do, lse2, delta, *r_srcs)


HALO = 8
CONV_RH = 64


def _conv_tiles(S):
    tm = _rows(S, 256)
    tc = D_FF // 2
    return tm, tc, D_FF // tc


def _silu_parts(gate):
    sg = jax.nn.sigmoid(gate)
    return sg, gate * sg


def _convgate_fwd(up, cw, cb, *, name):
    S = up.shape[0]
    tm, tc, nc = _conv_tiles(S)
    hb = tm // HALO

    def body(v_ref, g_ref, hv_ref, hg_ref, wv_ref, wg_ref, bv_ref, bg_ref, o_ref, cv_ref, cg_ref):
        keep = (pl.program_id(0) > 0).astype(F32)

        def chunk(cc, carry):
            cols = pl.ds(pl.multiple_of(cc * LANE, LANE), LANE)
            wv, wg, bv, bg = wv_ref[:, cols], wg_ref[:, cols], bv_ref[:, cols], bg_ref[:, cols]
            for r0 in range(0, tm, CONV_RH):
                def conv(t_ref, h_ref, w, b):
                    if r0:
                        span = t_ref[pl.ds(r0 - HALO, CONV_RH + HALO), cols]
                    else:
                        span = jnp.concatenate([h_ref[:, cols] * keep, t_ref[pl.ds(0, CONV_RH), cols]], axis=0)
                    back2, back1 = [pltpu.roll(span, s, axis=0)[HALO:] for s in (2, 1)]
                    return w[0:1] * back2 + w[1:2] * back1 + w[2:3] * span[HALO:] + b
                val = conv(v_ref, hv_ref, wv, bv)
                gate = conv(g_ref, hg_ref, wg, bg)
                cv_ref[pl.ds(r0, CONV_RH), cols] = val
                cg_ref[pl.ds(r0, CONV_RH), cols] = gate
                o_ref[pl.ds(r0, CONV_RH), cols] = (_silu_parts(gate)[1] * val).astype(o_ref.dtype)
            return carry

        lax.fori_loop(0, tc // LANE, chunk, 0)

    prev = lambda i: jnp.maximum(i * hb - 1, 0)
    return pl.pallas_call(
        body, name=name, grid=(S // tm, nc),
        in_specs=[pl.BlockSpec((tm, tc), lambda i, j: (i, j)), pl.BlockSpec((tm, tc), lambda i, j: (i, j + nc)),
                  pl.BlockSpec((HALO, tc), lambda i, j: (prev(i), j)),
                  pl.BlockSpec((HALO, tc), lambda i, j: (prev(i), j + nc)),
                  pl.BlockSpec((3, tc), lambda i, j: (0, j)), pl.BlockSpec((3, tc), lambda i, j: (0, j + nc)),
                  pl.BlockSpec((1, tc), lambda i, j: (0, j)), pl.BlockSpec((1, tc), lambda i, j: (0, j + nc))],
        out_specs=(pl.BlockSpec((tm, tc), lambda i, j: (i, j)),) * 3,
        out_shape=(jax.ShapeDtypeStruct((S, D_FF), MXU), jax.ShapeDtypeStruct((S, D_FF), F32),
                   jax.ShapeDtypeStruct((S, D_FF), F32)),
        compiler_params=_params(),
    )(up, up, up, up, cw, cw, cb, cb)


def _convgate_bwd(up, cval, cgate, dact, cw, *, name):
    S = up.shape[0]
    tm, tc, nc = _conv_tiles(S)
    hb = tm // HALO
    nr = S // tm
    R = tm + HALO

    def body(uv_ref, ug_ref, cv_ref, cg_ref, ncv_ref, ncg_ref, da_ref, dan_ref, wv_ref, wg_ref,
             duv_ref, dug_ref, dwv_ref, dwg_ref, dbv_ref, dbg_ref, dsv, dsg):
        i = pl.program_id(1)

        @pl.when(i == 0)
        def _():
            for r in (dwv_ref, dwg_ref, dbv_ref, dbg_ref):
                r[...] = jnp.zeros_like(r)

        keep_next = (i < nr - 1).astype(F32)

        def chunk(cc, carry):
            cols = pl.ds(pl.multiple_of(cc * LANE, LANE), LANE)

            def d_conv(rows, val, gate, d):
                sg, silu = _silu_parts(gate)
                dsv[rows, cols] = d * silu
                dsg[rows, cols] = d * val * (sg * (1.0 + gate * (1.0 - sg)))

            for r0 in range(0, tm, CONV_RH):
                rows = pl.ds(r0, CONV_RH)
                d_conv(rows, cv_ref[rows, cols], cg_ref[rows, cols], da_ref[rows, cols])
            d_conv(pl.ds(tm, HALO), ncv_ref[:, cols], ncg_ref[:, cols], dan_ref[:, cols] * keep_next)
            for ds, u_ref, w_ref, du_ref, dw_ref, db_ref in ((dsv, uv_ref, wv_ref, duv_ref, dwv_ref, dbv_ref),
                                                            (dsg, ug_ref, wg_ref, dug_ref, dwg_ref, dbg_ref)):
                w = w_ref[:, cols]
                acc = [jnp.zeros((1, LANE), F32) for _ in range(4)]
                for r0 in range(0, tm, CONV_RH):
                    rows = pl.ds(r0, CONV_RH)
                    span = ds[pl.ds(r0, CONV_RH + HALO), cols]
                    d = [span[:CONV_RH]] + [pltpu.roll(span, CONV_RH + HALO - s, axis=0)[:CONV_RH] for s in (1, 2)]
                    du_ref[rows, cols] = (w[2:3] * d[0] + w[1:2] * d[1] + w[0:1] * d[2]).astype(du_ref.dtype)
                    u = u_ref[rows, cols]
                    for kk in range(3):
                        acc[kk] = acc[kk] + jnp.sum(d[2 - kk] * u, axis=0, keepdims=True)
                    acc[3] = acc[3] + jnp.sum(d[0], axis=0, keepdims=True)
                for kk in range(3):
                    dw_ref[kk:kk + 1, cols] += acc[kk]
                db_ref[:, cols] += acc[3]
            return carry

        lax.fori_loop(0, tc // LANE, chunk, 0)

    nxt = lambda i: jnp.minimum((i + 1) * hb, S // HALO - 1)
    tile_v = pl.BlockSpec((tm, tc), lambda j, i: (i, j))
    tile_g = pl.BlockSpec((tm, tc), lambda j, i: (i, j + nc))
    halo = pl.BlockSpec((HALO, tc), lambda j, i: (nxt(i), j))
    w_v = pl.BlockSpec((3, tc), lambda j, i: (0, j))
    w_g = pl.BlockSpec((3, tc), lambda j, i: (0, j + nc))
    b_v = pl.BlockSpec((1, tc), lambda j, i: (0, j))
    return pl.pallas_call(
        body, name=name, grid=(nc, nr),
        in_specs=[tile_v, tile_g, tile_v, tile_v, halo, halo, tile_v, halo, w_v, w_g],
        out_specs=(tile_v, tile_v, w_v, w_v, b_v, b_v),
        out_shape=(jax.ShapeDtypeStruct((S, D_FF), MXU), jax.ShapeDtypeStruct((S, D_FF), MXU),
                   jax.ShapeDtypeStruct((3, D_FF), F32), jax.ShapeDtypeStruct((3, D_FF), F32),
                   jax.ShapeDtypeStruct((1, D_FF), F32), jax.ShapeDtypeStruct((1, D_FF), F32)),
        scratch_shapes=[pltpu.VMEM((R, tc), F32), pltpu.VMEM((R, tc), F32)],
        compiler_params=_params(),
    )(up, up, cval, cgate, cval, cgate, dact, dact, cw, cw)


def _glu_fwd(z, h, *, name):
    S = z.shape[0]
    tm = _rows(S, 512)

    def body(z_ref, h_ref, o_ref):
        o_ref[...] = h_ref[...] + z_ref[:, :D_MODEL] * jax.nn.sigmoid(z_ref[:, D_MODEL:])

    return pl.pallas_call(
        body, name=name, grid=(S // tm,), in_specs=[_row_spec(tm, 2 * D_MODEL), _row_spec(tm, D_MODEL)],
        out_specs=_row_spec(tm, D_MODEL), out_shape=jax.ShapeDtypeStruct((S, D_MODEL), F32),
        compiler_params=_params(),
    )(z, h)


def _glu_bwd(z, dm, *, name):
    S = z.shape[0]
    tm = _rows(S, 512)

    def body(z_ref, dm_ref, o_ref):
        sg = jax.nn.sigmoid(z_ref[:, D_MODEL:])
        dmv = dm_ref[...]
        o_ref[:, :D_MODEL] = (dmv * sg).astype(o_ref.dtype)
        o_ref[:, D_MODEL:] = (dmv * z_ref[:, :D_MODEL] * sg * (1.0 - sg)).astype(o_ref.dtype)

    return pl.pallas_call(
        body, name=name, grid=(S // tm,), in_specs=[_row_spec(tm, 2 * D_MODEL), _row_spec(tm, D_MODEL)],
        out_specs=_row_spec(tm, 2 * D_MODEL), out_shape=jax.ShapeDtypeStruct((S, 2 * D_MODEL), MXU),
        compiler_params=_params(),
    )(z, dm)


GELU_C = math.sqrt(2.0 / math.pi)
GELU_A = 0.044715


def _gelu(y):
    return 0.5 * y * (1.0 + jnp.tanh(GELU_C * (y + GELU_A * (y * y * y))))


def _gelu_bwd(y, dg, *, name):
    S = y.shape[0]
    tm = _rows(S, 512)

    def body(y_ref, dg_ref, o_ref):
        yv = y_ref[...]
        t = jnp.tanh(GELU_C * (yv + GELU_A * (yv * yv * yv)))
        d = 0.5 * (1.0 + t) + 0.5 * yv * (1.0 - t * t) * (GELU_C * (1.0 + 3.0 * GELU_A * (yv * yv)))
        o_ref[...] = dg_ref[...] * d

    return pl.pallas_call(
        body, name=name, grid=(S // tm,), in_specs=[_row_spec(tm, D_MODEL), _row_spec(tm, D_MODEL)],
        out_specs=_row_spec(tm, D_MODEL), out_shape=jax.ShapeDtypeStruct((S, D_MODEL), F32),
        compiler_params=_params(),
    )(y, dg)


FWD_STRIPS = 4
BWD_STRIPS = 8
SW = NSTATE // SBLK
ST2 = 2 * NSTATE


def _s5_fwd(u, wb, wc, abc, dskip, x0, *, full, name):
    S = u.shape[0]
    T = _rows(S, 256)
    nb = S // T
    nj = T // NSEG

    def body(u_ref, wb_ref, wc_ref, a_ref, d_ref, x0_ref, *rest):
        if full:
            xs_ref, y_ref, yg_ref, st = rest
        else:
            e_ref, xs_ref, st = rest
        i = pl.program_id(0)

        @pl.when(i == 0)
        def _():
            st[...] = x0_ref[...]

        uv = u_ref[...]
        ub = uv.astype(MXU)
        for kb in range(SBLK):
            r = jnp.dot(ub[:, kb * LANE:(kb + 1) * LANE], wb_ref[kb], preferred_element_type=F32)
            xs_ref[:, kb * SW:(kb + 1) * SW] = r[:, :SW]
            xs_ref[:, NSTATE + kb * SW:NSTATE + (kb + 1) * SW] = r[:, SW:]
        for sp in range(FWD_STRIPS):
            w = NSTATE // FWD_STRIPS
            re, im = pl.ds(sp * w, w), pl.ds(NSTATE + sp * w, w)
            ar, ai = a_ref[:, re], a_ref[:, im]

            def step(j, c):
                xr, xi = c
                rows = pl.ds(pl.multiple_of(j * NSEG, NSEG), NSEG)
                nr = ar * xr - ai * xi + xs_ref[rows, re]
                ni = ar * xi + ai * xr + xs_ref[rows, im]
                xs_ref[rows, re] = nr
                xs_ref[rows, im] = ni
                return nr, ni

            xr, xi = lax.fori_loop(0, nj, step, (st[:, re], st[:, im]))
            st[:, re] = xr
            st[:, im] = xi
        if full:
            for kb in range(SBLK):
                yk = (jnp.dot(xs_ref[:, kb * SW:(kb + 1) * SW].astype(MXU), wc_ref[kb, :SW, :], preferred_element_type=F32)
                      + jnp.dot(xs_ref[:, NSTATE + kb * SW:NSTATE + (kb + 1) * SW].astype(MXU), wc_ref[kb, SW:, :],
                                preferred_element_type=F32))
                cols = slice(kb * LANE, (kb + 1) * LANE)
                yk = yk + d_ref[:, cols] * uv[:, cols]
                y_ref[:, cols] = yk
                yg_ref[:, cols] = _gelu(yk).astype(yg_ref.dtype)
        else:
            @pl.when(i == nb - 1)
            def _():
                e_ref[...] = st[...]

    in_specs = [_row_spec(T, D_MODEL), pl.BlockSpec((SBLK, LANE, 2 * SW), lambda i: (0, 0, 0)),
                pl.BlockSpec((SBLK, 2 * SW, LANE), lambda i: (0, 0, 0)), _const_spec(NSEG, ST2),
                _const_spec(1, D_MODEL), _const_spec(NSEG, ST2)]
    if full:
        out_specs = (_row_spec(T, ST2), _row_spec(T, D_MODEL), _row_spec(T, D_MODEL))
        out_shape = (jax.ShapeDtypeStruct((S, ST2), F32), jax.ShapeDtypeStruct((S, D_MODEL), F32),
                     jax.ShapeDtypeStruct((S, D_MODEL), MXU))
        scratch = [pltpu.VMEM((NSEG, ST2), F32)]
    else:
        out_specs = _const_spec(NSEG, ST2)
        out_shape = jax.ShapeDtypeStruct((NSEG, ST2), F32)
        scratch = [pltpu.VMEM((T, ST2), F32), pltpu.VMEM((NSEG, ST2), F32)]
    return pl.pallas_call(
        body, name=name, grid=(nb,), in_specs=in_specs, out_specs=out_specs, out_shape=out_shape,
        scratch_shapes=scratch, compiler_params=_params(big=True),
    )(u, wb, wc, abc, dskip, x0)


def _s5_bwd(dy, xs, u, wct, wbt, abc, dskip, x0, l0, *, full, name):
    S = dy.shape[0]
    T = _rows(S, 256)
    nb = S // T
    nj = T // NSEG
    blk = lambda i: nb - 1 - i

    def body(dy_ref, *rest):
        if full:
            (xs_ref, xh_ref, u_ref, wct_ref, wbt_ref, a_ref, d_ref, x0_ref, l0_ref,
             du_ref, da_ref, dwb_hbm, dwc_hbm, dd_ref, g_s, lam_s, dwb_ref, dwc_ref) = rest
        else:
            wct_ref, a_ref, l0_ref, f_ref, g_s, lam_s = rest
        i = pl.program_id(0)

        @pl.when(i == 0)
        def _():
            lam_s[...] = l0_ref[...]
            if full:
                for r in (da_ref, dwb_ref, dwc_ref, dd_ref):
                    r[...] = jnp.zeros_like(r)

        dyv = dy_ref[...]
        dyb = dyv.astype(MXU)
        for kb in range(SBLK):
            r = jnp.dot(dyb[:, kb * LANE:(kb + 1) * LANE], wct_ref[kb], preferred_element_type=F32)
            g_s[:, kb * SW:(kb + 1) * SW] = r[:, :SW]
            g_s[:, NSTATE + kb * SW:NSTATE + (kb + 1) * SW] = r[:, SW:]
        for sp in range(BWD_STRIPS):
            w = NSTATE // BWD_STRIPS
            re, im = pl.ds(sp * w, w), pl.ds(NSTATE + sp * w, w)
            ar, ai = a_ref[:, re], a_ref[:, im]

            def advance(row, lr, li):
                rows = pl.ds(row, NSEG)
                nr = g_s[rows, re] + ar * lr + ai * li
                ni = g_s[rows, im] - ai * lr + ar * li
                g_s[rows, re] = nr
                g_s[rows, im] = ni
                return nr, ni

            def step(jj, c):
                row = pl.multiple_of((nj - 1 - jj) * NSEG, NSEG)
                nr, ni = advance(row, c[0], c[1])
                if not full:
                    return nr, ni
                prow = pl.ds(pl.multiple_of(row - NSEG, NSEG), NSEG)
                xpr, xpi = xs_ref[prow, re], xs_ref[prow, im]
                return nr, ni, c[2] + (nr * xpr + ni * xpi), c[3] + (ni * xpr - nr * xpi)

            init = (lam_s[:, re], lam_s[:, im])
            if full:
                init = init + (jnp.zeros((NSEG, w), F32), jnp.zeros((NSEG, w), F32))
            c = lax.fori_loop(0, nj - 1, step, init)
            lr, li = advance(0, c[0], c[1])
            if full:
                first = (blk(i) == 0)
                xpr = jnp.where(first, x0_ref[:, re], xh_ref[:, re])
                xpi = jnp.where(first, x0_ref[:, im], xh_ref[:, im])
                da_ref[:, re] += c[2] + (lr * xpr + li * xpi)
                da_ref[:, im] += c[3] + (li * xpr - lr * xpi)
            lam_s[:, re] = lr
            lam_s[:, im] = li
        if full:
            uv = u_ref[...]
            ub = uv.astype(MXU)
            dd_ref[...] += jnp.sum(dyv * uv, axis=0, keepdims=True)
            for kb in range(SBLK):
                cols = slice(kb * LANE, (kb + 1) * LANE)
                re = slice(kb * SW, (kb + 1) * SW)
                im = slice(NSTATE + kb * SW, NSTATE + (kb + 1) * SW)
                lr_b = g_s[:, re].astype(MXU)
                li_b = g_s[:, im].astype(MXU)
                duk = (jnp.dot(lr_b, wbt_ref[kb, :SW, :], preferred_element_type=F32)
                       + jnp.dot(li_b, wbt_ref[kb, SW:, :], preferred_element_type=F32))
                du_ref[:, cols] = duk + d_ref[:, cols] * dyv[:, cols]
                dwb_ref[kb, :, :SW] += _dot_tn(ub[:, cols], lr_b)
                dwb_ref[kb, :, SW:] += _dot_tn(ub[:, cols], li_b)
                dwc_ref[kb, :SW, :] += _dot_tn(xs_ref[:, re].astype(MXU), dyb[:, cols])
                dwc_ref[kb, SW:, :] += _dot_tn(xs_ref[:, im].astype(MXU), dyb[:, cols])

            @pl.when(i == nb - 1)
            def _():
                pltpu.sync_copy(dwb_ref, dwb_hbm)
                pltpu.sync_copy(dwc_ref, dwc_hbm)
        else:
            @pl.when(i == nb - 1)
            def _():
                f_ref[...] = lam_s[...]

    rev = lambda c: pl.BlockSpec((T, c), lambda i: (blk(i), 0))
    w3 = lambda a, b: pl.BlockSpec((SBLK, a, b), lambda i: (0, 0, 0))
    if full:
        hb = T // NSEG
        in_specs = [rev(D_MODEL), rev(ST2),
                    pl.BlockSpec((NSEG, ST2), lambda i: (jnp.maximum(blk(i) * hb - 1, 0), 0)),
                    rev(D_MODEL), w3(LANE, 2 * SW), w3(2 * SW, LANE), _const_spec(NSEG, ST2),
                    _const_spec(1, D_MODEL), _const_spec(NSEG, ST2), _const_spec(NSEG, ST2)]
        ops = [dy, xs, xs, u, wct, wbt, abc, dskip, x0, l0]
        hbm = pl.BlockSpec(memory_space=pl.ANY)
        out_specs = (rev(D_MODEL), _const_spec(NSEG, ST2), hbm, hbm, _const_spec(1, D_MODEL))
        out_shape = (jax.ShapeDtypeStruct((S, D_MODEL), F32), jax.ShapeDtypeStruct((NSEG, ST2), F32),
                     jax.ShapeDtypeStruct((SBLK, LANE, 2 * SW), F32), jax.ShapeDtypeStruct((SBLK, 2 * SW, LANE), F32),
                     jax.ShapeDtypeStruct((1, D_MODEL), F32))
    else:
        in_specs = [rev(D_MODEL), w3(LANE, 2 * SW), _const_spec(NSEG, ST2), _const_spec(NSEG, ST2)]
        ops = [dy, wct, abc, l0]
        out_specs = _const_spec(NSEG, ST2)
        out_shape = jax.ShapeDtypeStruct((NSEG, ST2), F32)
    return pl.pallas_call(
        body, name=name, grid=(nb,), in_specs=in_specs, out_specs=out_specs, out_shape=out_shape,
        scratch_shapes=[pltpu.VMEM((T, ST2), F32), pltpu.VMEM((NSEG, ST2), F32)]
        + ([pltpu.VMEM((SBLK, LANE, 2 * SW), F32), pltpu.VMEM((SBLK, 2 * SW, LANE), F32)] if full else []),
        compiler_params=_params(big=True),
    )(*ops)


def _s5_discretize(lr, li, log_dt, br, bi):
    dt = jnp.exp(log_dt)[:, None]
    mag = jnp.exp(lr * dt)
    ar = mag * jnp.cos(li * dt)
    ai = mag * jnp.sin(li * dt)
    den = lr * lr + li * li
    nr = ar - 1.0
    coef_r = (nr * lr + ai * li) / den
    coef_i = (ai * lr - nr * li) / den
    bbar_r = coef_r[..., None] * br - coef_i[..., None] * bi
    bbar_i = coef_r[..., None] * bi + coef_i[..., None] * br
    return ar, ai, bbar_r, bbar_i


def _blockdiag(m):
    gpb = SSM_GROUPS // SBLK
    a, b = m.shape[1:]
    mb = m.reshape(SBLK, gpb, a, b)
    eye = jnp.eye(gpb, dtype=m.dtype)
    return jnp.einsum('kgab,gh->kgahb', mb, eye).reshape(SBLK, gpb * a, gpb * b)


def _blockdiag_extract(w, a, b):
    gpb = SSM_GROUPS // SBLK
    w5 = w.reshape(SBLK, gpb, a, gpb, b)
    return jnp.einsum('kgahb,gh->kgab', w5, jnp.eye(gpb, dtype=w.dtype)).reshape(SSM_GROUPS, a, b)


def _cpow(ar, ai, n):
    rr, ri = jnp.ones_like(ar), jnp.zeros_like(ai)
    br, bi = ar, ai
    while n:
        if n & 1:
            rr, ri = rr * br - ri * bi, rr * bi + ri * br
        br, bi = br * br - bi * bi, 2.0 * br * bi
        n >>= 1
    return rr, ri


def _perm(a):
    s, c = a.shape
    return a.reshape(NSEG, s // NSEG, c).transpose(1, 0, 2).reshape(s, c)


def _unperm(a):
    s, c = a.shape
    return a.reshape(s // NSEG, NSEG, c).transpose(1, 0, 2).reshape(s, c)


def _other_chips(x, y):
    return [(1 - x, y), (x, 1 - y), (1 - x, 1 - y)]


def _span(chip, size, align):
    return pl.ds(pl.multiple_of(chip * size, align), size)


def _exchange_ops(pieces, s_refs, o_refs, send_sems, recv_sems, local_sems):
    x, y, c = lax.axis_index("x"), lax.axis_index("y"), lax.axis_index("c")
    me = 2 * x + y
    others = _other_chips(x, y)
    npc = len(pieces)

    def remote(k, p, tx, ty, src_chip, dst_chip):
        si, oi, sv, dv = pieces[p]
        return pltpu.make_async_remote_copy(
            src_ref=sv(s_refs[si], src_chip), dst_ref=dv(o_refs[oi], dst_chip), send_sem=send_sems.at[k, p],
            recv_sem=recv_sems.at[k, p], device_id=(tx, ty, c), device_id_type=MESH)

    def local(p):
        si, oi, sv, dv = pieces[p]
        return pltpu.make_async_copy(sv(s_refs[si], me), dv(o_refs[oi], me), local_sems.at[p])

    def start():
        for p in range(npc):
            local(p).start()
        for k, (tx, ty) in enumerate(others):
            for p in range(npc):
                remote(k, p, tx, ty, 2 * tx + ty, me).start()

    def finish():
        for k, (tx, ty) in enumerate(others):
            for p in range(npc):
                remote(k, p, tx, ty, me, 2 * tx + ty).wait_recv()
        for k, (tx, ty) in enumerate(others):
            for p in range(npc):
                remote(k, p, tx, ty, 2 * tx + ty, me).wait_send()
        for p in range(npc):
            local(p).wait()

    return start, finish


def _exchange_scratch(npc):
    return [pltpu.SemaphoreType.DMA((NCHIP - 1, npc)), pltpu.SemaphoreType.DMA((NCHIP - 1, npc)),
            pltpu.SemaphoreType.DMA((npc,))]


def _chip_exchange(srcs, out_shapes, pieces, *, name):
    ns, no = len(srcs), len(out_shapes)

    def body(*refs):
        start, finish = _exchange_ops(pieces, refs[:ns], refs[ns:ns + no], *refs[ns + no:])
        start()
        finish()

    hbm = pl.BlockSpec(memory_space=pl.ANY)
    return pl.pallas_call(
        body, name=name, in_specs=[hbm] * ns, out_specs=tuple([hbm] * no), out_shape=tuple(out_shapes),
        scratch_shapes=_exchange_scratch(len(pieces)),
    )(*srcs)


def _sibling_exchange(srcs, *, name):
    n = len(srcs)

    def body(*refs):
        s_refs, o_refs, send_sems, recv_sems = refs[:n], refs[n:2 * n], refs[2 * n], refs[2 * n + 1]
        x, y, c = lax.axis_index("x"), lax.axis_index("y"), lax.axis_index("c")
        cps = [pltpu.make_async_remote_copy(src_ref=s_refs[p], dst_ref=o_refs[p], send_sem=send_sems.at[p],
                                            recv_sem=recv_sems.at[p], device_id=(x, y, 1 - c), device_id_type=MESH)
               for p in range(n)]
        for cp in cps:
            cp.start()
        for cp in cps:
            cp.wait()

    hbm = pl.BlockSpec(memory_space=pl.ANY)
    return pl.pallas_call(
        body, name=name, in_specs=[hbm] * n, out_specs=tuple([hbm] * n),
        out_shape=tuple(jax.ShapeDtypeStruct(s.shape, s.dtype) for s in srcs),
        scratch_shapes=[pltpu.SemaphoreType.DMA((n,)), pltpu.SemaphoreType.DMA((n,))],
    )(*srcs)


def _row_tile(r, c, tile_bytes):
    best = 16
    for t in range(16, r + 1, 16):
        if r % t == 0 and t * c * 4 <= tile_bytes:
            best = t
    assert r % best == 0
    return best


def _sum_chips(r, *, name):
    _, R, W = r.shape
    tm = _row_tile(R, W, 2 * 1024 * 1024)

    def body(r_ref, o_ref):
        o_ref[...] = ((r_ref[0].astype(F32) + r_ref[1].astype(F32)) + r_ref[2].astype(F32)) + r_ref[3].astype(F32)

    return pl.pallas_call(
        body, name=name, grid=(R // tm,), in_specs=[pl.BlockSpec((NCHIP, tm, W), lambda i: (0, i, 0))],
        out_specs=_row_spec(tm, W), out_shape=jax.ShapeDtypeStruct((R, W), F32), compiler_params=_params(),
    )(r)


def _adamw(p_mine, p_sib, w, m, v, *, name):
    R, W = w.shape
    tm = _row_tile(R, W, 1024 * 1024)

    def body(a_ref, b_ref, w_ref, m_ref, v_ref, g_ref, d_ref, nm_ref, nv_ref):
        g = a_ref[...] + b_ref[...]
        mm = ADAM_B1 * m_ref[...] + (1.0 - ADAM_B1) * g
        vv = ADAM_B2 * v_ref[...] + (1.0 - ADAM_B2) * (g * g)
        m_hat = mm / (1.0 - ADAM_B1 ** ADAM_STEP)
        v_hat = vv / (1.0 - ADAM_B2 ** ADAM_STEP)
        g_ref[...] = g
        d_ref[...] = -ADAM_LR * (m_hat / (jnp.sqrt(v_hat) + ADAM_EPS) + ADAM_WD * w_ref[...])
        nm_ref[...] = mm
        nv_ref[...] = vv

    spec = _row_spec(tm, W)
    shp = jax.ShapeDtypeStruct((R, W), F32)
    return pl.pallas_call(
        body, name=name, grid=(R // tm,), in_specs=[spec] * 5, out_specs=(spec,) * 4, out_shape=(shp,) * 4,
        compiler_params=_params(),
    )(p_mine, p_sib, w, m, v)


def _pack_small(parts):
    flat = jnp.concatenate([p.reshape(-1) for p in parts])
    pad = (-flat.shape[0]) % (SMALL_ROWS * PACKW)
    return jnp.pad(flat, (0, pad)).reshape(-1, PACKW)


def _unpack_small(buf, shapes):
    flat, out, off = buf.reshape(-1), [], 0
    for shp in shapes:
        sz = math.prod(shp)
        out.append(flat[off:off + sz].reshape(shp))
        off += sz
    return out


def _shard(a, t, ax):
    sz = a.shape[ax] // NCHIP
    return lax.slice_in_dim(a, t * sz, (t + 1) * sz, axis=ax)


MLA_W = ['mla_w_a', 'mla_w_uq', 'mla_w_ukv', 'mla_w_o']
REST_W = ['ssm_w_in', 'ssm_w_glu', 'ffn_w_up', 'ffn_w_down']
SMALL = [n for n in WNAMES if n not in MLA_W + REST_W]


def _gather_plan(w, names, with_small):
    srcs, outs, pieces = [], [], []
    for name in names:
        local = w[name].astype(MXU)
        local = local[0] if local.shape[0] == 1 else local
        ax = SHARD_AXIS[name] - (1 if w[name].shape[0] == 1 else 0)
        full = local.shape[:ax] + (NCHIP * local.shape[ax],) + local.shape[ax + 1:]
        si, oi = len(srcs), len(outs)
        srcs.append(local)
        outs.append(jax.ShapeDtypeStruct(full, MXU))
        size = local.shape[ax]
        if local.ndim == 2:
            if ax == 0:
                pieces.append((si, oi, lambda r, t: r, lambda r, ch, size=size: r.at[_span(ch, size, 8), :]))
            else:
                pieces.append((si, oi, lambda r, t: r, lambda r, ch, size=size: r.at[:, _span(ch, size, LANE)]))
        else:
            for l in range(local.shape[0]):
                if ax == 1:
                    dv = lambda r, ch, l=l, size=size: r.at[l, _span(ch, size, 8), :]
                else:
                    dv = lambda r, ch, l=l, size=size: r.at[l, :, _span(ch, size, LANE)]
                pieces.append((si, oi, lambda r, t, l=l: r.at[l], dv))
    if with_small:
        small = _pack_small([w[n] for n in GATHER_F32])
        srcs.append(small)
        outs.append(jax.ShapeDtypeStruct((NCHIP,) + small.shape, F32))
        pieces.append((len(srcs) - 1, len(outs) - 1, lambda r, t: r, lambda r, ch: r.at[ch]))
    return srcs, outs, pieces


def _gather_result(got, w, names, with_small):
    full = dict(zip(names, got[:len(names)]))
    if with_small:
        per_chip = [_unpack_small(got[-1][t], [w[n].shape for n in GATHER_F32]) for t in range(NCHIP)]
        for j, n in enumerate(GATHER_F32):
            full[n] = jnp.concatenate([per_chip[t][j] for t in range(NCHIP)], axis=SHARD_AXIS[n])
    return full


def _as_rows(a):
    return a.reshape(-1, a.shape[-1])


def _grad_plan(grads, w, names, with_small):
    srcs, outs, pieces = [], [], []
    for name in names:
        local = w[name]
        ax = SHARD_AXIS[name]
        size = local.shape[ax]
        oi = len(outs)
        layers = grads[name] if isinstance(grads[name], list) else [grads[name]]
        outs.append(jax.ShapeDtypeStruct((NCHIP,) + local.shape, layers[0].dtype))
        for l, g in enumerate(layers):
            si = len(srcs)
            srcs.append(g)
            if ax == 1:
                sv = lambda r, t, size=size: r.at[_span(t, size, 8), :]
            else:
                sv = lambda r, t, size=size: r.at[:, _span(t, size, LANE)]
            pieces.append((si, oi, sv, lambda r, ch, l=l: r.at[ch, l]))
    if with_small:
        small = jnp.stack([_pack_small([_shard(grads[n], t, SHARD_AXIS[n]) if n in SHARD_AXIS else grads[n]
                                        for n in SMALL]) for t in range(NCHIP)])
        srcs.append(small)
        outs.append(jax.ShapeDtypeStruct(small.shape, F32))
        pieces.append((len(srcs) - 1, len(outs) - 1, lambda r, t: r.at[t], lambda r, ch: r.at[ch]))
    return srcs, outs, pieces


def _reduce_and_update(names, landed, w, mom, var):
    partial = [_sum_chips(r.reshape(NCHIP, -1, r.shape[-1]), name="grad_sum_chips") for r in landed]
    sibling = _sibling_exchange(partial, name="grad_sibling")
    res = [dict(), dict(), dict(), dict()]
    for j, name in enumerate(names):
        outs4 = _adamw(partial[j], sibling[j], _as_rows(w[name]), _as_rows(mom[name]), _as_rows(var[name]), name="adamw")
        for d, o in zip(res, outs4):
            d[name] = o.reshape(w[name].shape)
    outs4 = _adamw(partial[-1], sibling[-1], *[_pack_small([t[n] for n in SMALL]) for t in (w, mom, var)], name="adamw")
    for d, o in zip(res, outs4):
        d.update(zip(SMALL, _unpack_small(o, [w[n].shape for n in SMALL])))
    return res


def kernel(x, positions, mla_w_a, mla_g_q, mla_g_kv, mla_w_uq, mla_w_ukv, mla_w_o, ssm_w_in, ssm_lambda_re, ssm_lambda_im, ssm_log_dt, ssm_b_re, ssm_b_im, ssm_c_re, ssm_c_im, ssm_d, ssm_w_glu, ffn_w_up, ffn_conv_w, ffn_conv_b, ffn_w_down, g_mix, g_ffn, g_final, loss_target, m_mla_w_a, m_mla_g_q, m_mla_g_kv, m_mla_w_uq, m_mla_w_ukv, m_mla_w_o, m_ssm_w_in, m_ssm_lambda_re, m_ssm_lambda_im, m_ssm_log_dt, m_ssm_b_re, m_ssm_b_im, m_ssm_c_re, m_ssm_c_im, m_ssm_d, m_ssm_w_glu, m_ffn_w_up, m_ffn_conv_w, m_ffn_conv_b, m_ffn_w_down, m_g_mix, m_g_ffn, m_g_final, v_mla_w_a, v_mla_g_q, v_mla_g_kv, v_mla_w_uq, v_mla_w_ukv, v_mla_w_o, v_ssm_w_in, v_ssm_lambda_re, v_ssm_lambda_im, v_ssm_log_dt, v_ssm_b_re, v_ssm_b_im, v_ssm_c_re, v_ssm_c_im, v_ssm_d, v_ssm_w_glu, v_ffn_w_up, v_ffn_conv_w, v_ffn_conv_b, v_ffn_w_down, v_g_mix, v_g_ffn, v_g_final):
    w = dict(zip(WNAMES, (mla_w_a, mla_g_q, mla_g_kv, mla_w_uq, mla_w_ukv, mla_w_o, ssm_w_in, ssm_lambda_re,
                          ssm_lambda_im, ssm_log_dt, ssm_b_re, ssm_b_im, ssm_c_re, ssm_c_im, ssm_d, ssm_w_glu,
                          ffn_w_up, ffn_conv_w, ffn_conv_b, ffn_w_down, g_mix, g_ffn, g_final)))
    mom = dict(zip(WNAMES, (m_mla_w_a, m_mla_g_q, m_mla_g_kv, m_mla_w_uq, m_mla_w_ukv, m_mla_w_o, m_ssm_w_in,
                            m_ssm_lambda_re, m_ssm_lambda_im, m_ssm_log_dt, m_ssm_b_re, m_ssm_b_im, m_ssm_c_re,
                            m_ssm_c_im, m_ssm_d, m_ssm_w_glu, m_ffn_w_up, m_ffn_conv_w, m_ffn_conv_b,
                            m_ffn_w_down, m_g_mix, m_g_ffn, m_g_final)))
    var = dict(zip(WNAMES, (v_mla_w_a, v_mla_g_q, v_mla_g_kv, v_mla_w_uq, v_mla_w_ukv, v_mla_w_o, v_ssm_w_in,
                            v_ssm_lambda_re, v_ssm_lambda_im, v_ssm_log_dt, v_ssm_b_re, v_ssm_b_im, v_ssm_c_re,
                            v_ssm_c_im, v_ssm_d, v_ssm_w_glu, v_ffn_w_up, v_ffn_conv_w, v_ffn_conv_b,
                            v_ffn_w_down, v_g_mix, v_g_ffn, v_g_final)))
    S = x.shape[1]
    D = D_MODEL
    x2 = x.reshape(S, D)
    tgt = loss_target.reshape(S, D)

    fw = _gather_result(_chip_exchange(*_gather_plan(w, MLA_W, False), name="gather_weights"), w, MLA_W, False)
    w_a = jnp.pad(fw['mla_w_a'], ((0, 0), (0, A_PAD - KR0 - QK_ROPE)))
    uq = fw['mla_w_uq'].reshape(Q_LORA, HEADS, QK_DIM)
    w_uq = jnp.concatenate([uq[:, :, :QK_NOPE].reshape(Q_LORA, HEADS * QK_NOPE),
                            jnp.pad(uq[:, :, QK_NOPE:], ((0, 0), (0, 0), (0, LANE - QK_ROPE))).reshape(Q_LORA, HEADS * LANE)],
                           axis=1)
    w_ukv = fw['mla_w_ukv']
    w_o = fw['mla_w_o']
    conv_b = w['ffn_conv_b']
    g_q, g_kv = w['mla_g_q'], w['mla_g_kv']
    gm, gf = w['g_mix'], w['g_ffn']
    gfin = w['g_final'].reshape(1, D)

    inv = 1.0 / (ROPE_THETA ** (jnp.arange(0, QK_ROPE, 2, dtype=F32) / QK_ROPE))
    ang = positions.reshape(S).astype(F32)[:, None] * inv
    cos, sin = jnp.cos(ang), jnp.sin(ang)
    zpad = jnp.zeros((S, LANE - QK_ROPE), F32)
    c128 = jnp.concatenate([cos, cos, zpad], axis=1)
    s128 = jnp.concatenate([-sin, sin, zpad], axis=1)

    hn0 = _rmsnorm_fwd(x2, gm[0:1], name="rms_mix0")
    a = _mm(hn0, w_a, name="mla_a")
    cqn, ckvn, kr = _mla_mid_fwd(a, g_q, g_kv, c128, s128, name="mla_mid_fwd")
    qfull = _mm(cqn, w_uq, name="mla_q")
    kv = _mm(ckvn, w_ukv, out_dtype=MXU, name="mla_kv")
    qs, ks, vs = _qk_prep(qfull, kv, kr, c128, s128, name="qk_prep")
    os_, lses = [], []
    rides = {0: (['ssm_w_in', 'ssm_w_glu', 'ffn_w_down'], True), 1: (['ffn_w_up'], False)}
    for h in range(HEADS):
        if h in rides:
            o_h, lse_h, *got = _attn_fwd(qs[h], ks[h], vs[h], name="attn_fwd_gather", ride=_gather_plan(w, *rides[h]))
            fw.update(_gather_result(got, w, *rides[h]))
        else:
            o_h, lse_h = _attn_fwd(qs[h], ks[h], vs[h], name="attn_fwd")
        os_.append(o_h)
        lses.append(lse_h)
    w_in = fw['ssm_w_in']
    w_glu = fw['ssm_w_glu']
    w_up = fw['ffn_w_up']
    w_down = fw['ffn_w_down']
    conv_w = fw['ffn_conv_w']
    dskip = fw['ssm_d']
    o_cat = jnp.concatenate(os_, axis=1)
    h1 = _mm(o_cat, w_o, res=x2, name="mla_o")

    def ffn_fwd(h, l):
        hn = _rmsnorm_fwd(h, gf[l:l + 1], name="rms_ffn")
        up = _mm(hn, w_up[l], name="ffn_up")
        act, cval, cgate = _convgate_fwd(up, conv_w[l], conv_b[l:l + 1], name="convgate_fwd")
        return _mm(act, w_down[l], res=h, name="ffn_down"), (hn, up, act, cval, cgate)

    h2, saved0 = ffn_fwd(h1, 0)

    lam_re, lam_im, log_dt = w['ssm_lambda_re'][0], w['ssm_lambda_im'][0], w['ssm_log_dt'][0]
    (a_re, a_im, bbar_r, bbar_i), disc_vjp = jax.vjp(_s5_discretize, lam_re, lam_im, log_dt, w['ssm_b_re'][0],
                                                    w['ssm_b_im'][0])
    c_re, c_im = w['ssm_c_re'][0], w['ssm_c_im'][0]
    bt_r, bt_i = jnp.swapaxes(bbar_r, 1, 2), jnp.swapaxes(bbar_i, 1, 2)
    wb = jnp.concatenate([_blockdiag(bt_r), _blockdiag(bt_i)], axis=2).astype(MXU)
    wbt = jnp.concatenate([_blockdiag(bbar_r), _blockdiag(bbar_i)], axis=1).astype(MXU)
    ct_r, ct_i = jnp.swapaxes(c_re, 1, 2), jnp.swapaxes(c_im, 1, 2)
    wc = jnp.concatenate([_blockdiag(ct_r), _blockdiag(-ct_i)], axis=1).astype(MXU)
    wct = jnp.concatenate([_blockdiag(c_re), _blockdiag(-c_im)], axis=2).astype(MXU)
    af_r, af_i = a_re.reshape(NSTATE), a_im.reshape(NSTATE)
    abc = jnp.broadcast_to(jnp.concatenate([af_r, af_i])[None], (NSEG, ST2))
    seg = S // NSEG
    ap_r, ap_i = _cpow(af_r, af_i, seg)

    hn1 = _rmsnorm_fwd(h2, gm[1:2], name="rms_mix1")
    u = _mm(hn1, w_in, name="s5_in")
    u_p = _perm(u)
    zero_state = jnp.zeros((NSEG, ST2), F32)
    ends = _s5_fwd(u_p, wb, wc, abc, dskip, zero_state, full=False, name="s5_fwd_ends")
    inits, cr, ci = [], jnp.zeros((NSTATE,), F32), jnp.zeros((NSTATE,), F32)
    for r in range(NSEG):
        inits.append(jnp.concatenate([cr, ci]))
        er, ei = ends[r, :NSTATE], ends[r, NSTATE:]
        cr, ci = er + ap_r * cr - ap_i * ci, ei + ap_r * ci + ap_i * cr
    x0 = jnp.stack(inits)
    xs, y_p, yg_p = _s5_fwd(u_p, wb, wc, abc, dskip, x0, full=True, name="s5_fwd")
    yg = _unperm(yg_p)
    z = _mm(yg, w_glu, name="s5_glu")
    h3 = _glu_fwd(z, h2, name="glu_fwd")
    h4, saved1 = ffn_fwd(h3, 1)

    loss_l, dh4, dg_final = _loss_head(h4, gfin, tgt, name="loss_head")

    grads = {}

    def ffn_bwd(h_in, g, saved, l):
        hn, up, act, cval, cgate = saved
        w_up_t = w_up[l].T
        dact = _mm(g, w_down[l].T, name="ffn_down_dx")
        dw_down = _mm(act, g, mode="tn", out_dtype=MXU, name="ffn_down_dw")
        duv, dug, dwv, dwg, dbv, dbg = _convgate_bwd(up, cval, cgate, dact, conv_w[l], name="convgate_bwd")
        dw_up = jnp.concatenate([_mm(hn, duv, mode="tn", out_dtype=MXU, name="ffn_up_dw"), _mm(hn, dug, mode="tn", out_dtype=MXU, name="ffn_up_dw")],
                                axis=1)
        dhn = _mm(duv, w_up_t[:D_FF], name="ffn_up_dx")
        dhn = _mm(dug, w_up_t[D_FF:], res=dhn, name="ffn_up_dx_acc")
        dh, dg = _rmsnorm_bwd(h_in, gf[l:l + 1], dhn, g, name="rms_ffn_bwd")
        return dh, dict(w_up=dw_up, w_down=dw_down, conv_w=jnp.concatenate([dwv, dwg], axis=1),
                        conv_b=jnp.concatenate([dbv, dbg], axis=1)[0], g_ffn=dg[0])

    dh3, fg1 = ffn_bwd(h3, dh4, saved1, 1)

    dz = _glu_bwd(z, dh3, name="glu_bwd")
    grads['ssm_w_glu'] = _mm(yg, dz, mode="tn", out_dtype=MXU, name="s5_glu_dw")
    dyg = _mm(dz, w_glu.T, name="s5_glu_dx")
    dy_p = _gelu_bwd(y_p, _perm(dyg), name="gelu_bwd")
    firsts = _s5_bwd(dy_p, None, None, wct, None, abc, None, None, zero_state, full=False, name="s5_bwd_firsts")
    linits, cr, ci = [None] * NSEG, jnp.zeros((NSTATE,), F32), jnp.zeros((NSTATE,), F32)
    for r in reversed(range(NSEG)):
        linits[r] = jnp.concatenate([cr, ci])
        fr, fi = firsts[r, :NSTATE], firsts[r, NSTATE:]
        cr, ci = fr + ap_r * cr + ap_i * ci, fi + ap_r * ci - ap_i * cr
    l0 = jnp.stack(linits)
    du_p, dab, dwb, dwc, dd = _s5_bwd(dy_p, xs, u_p, wct, wbt, abc, dskip, x0, l0, full=True, name="s5_bwd")
    du = _unperm(du_p)
    grads['ssm_w_in'] = _mm(hn1, du, mode="tn", out_dtype=MXU, name="s5_in_dw")
    dhn1 = _mm(du, w_in.T, name="s5_in_dx")
    dh2, dg_mix1 = _rmsnorm_bwd(h2, gm[1:2], dhn1, dh3, name="rms_mix_bwd")
    da_sum = jnp.sum(dab, axis=0)
    dbt_r = _blockdiag_extract(dwb[:, :, :SW], SSM_GROUP, SSM_STATE)
    dbt_i = _blockdiag_extract(dwb[:, :, SW:], SSM_GROUP, SSM_STATE)
    dlr, dli, dlog_dt, dbr, dbi = disc_vjp((da_sum[:NSTATE].reshape(SSM_GROUPS, SSM_STATE),
                                            da_sum[NSTATE:].reshape(SSM_GROUPS, SSM_STATE),
                                            jnp.swapaxes(dbt_r, 1, 2), jnp.swapaxes(dbt_i, 1, 2)))
    dct_r = _blockdiag_extract(dwc[:, :SW, :], SSM_STATE, SSM_GROUP)
    dct_i = _blockdiag_extract(dwc[:, SW:, :], SSM_STATE, SSM_GROUP)
    grads['ssm_lambda_re'], grads['ssm_lambda_im'], grads['ssm_log_dt'] = dlr[None], dli[None], dlog_dt[None]
    grads['ssm_b_re'], grads['ssm_b_im'] = dbr[None], dbi[None]
    grads['ssm_c_re'] = jnp.swapaxes(dct_r, 1, 2)[None]
    grads['ssm_c_im'] = -jnp.swapaxes(dct_i, 1, 2)[None]
    grads['ssm_d'] = dd

    dh1, fg0 = ffn_bwd(h1, dh2, saved0, 0)
    grads['ffn_w_up'] = [fg0['w_up'], fg1['w_up']]
    grads['ffn_w_down'] = [fg0['w_down'], fg1['w_down']]
    grads['ffn_conv_w'] = jnp.stack([fg0['conv_w'], fg1['conv_w']])
    grads['ffn_conv_b'] = jnp.stack([fg0['conv_b'], fg1['conv_b']])
    grads['g_ffn'] = jnp.stack([fg0['g_ffn'], fg1['g_ffn']])

    do_cat = _mm(dh1, w_o.T, out_dtype=MXU, name="mla_o_dx")
    grads['mla_w_o'] = _mm(o_cat, dh1, mode="tn", out_dtype=MXU, name="mla_o_dw")
    dqs, dks, dvs = [], [], []
    for h in range(HEADS):
        do_h = do_cat[:, h * V_HEAD:(h + 1) * V_HEAD]
        delta = _attn_delta(do_h, os_[h], name="attn_delta")
        tiles = (S // _rows(S, BWD_T), 1, _rows(S, BWD_T))
        lse2 = (lses[h] * math.log2(math.e)).reshape(tiles)
        if h == 0:
            dq_h, dk_h, dv_h, *landed = _attn_bwd(qs[h], ks[h], vs[h], do_h, lse2, delta.reshape(tiles),
                                                  name="attn_bwd_exchange", ride=_grad_plan(grads, w, REST_W, False))
        else:
            dq_h, dk_h, dv_h = _attn_bwd(qs[h], ks[h], vs[h], do_h, lse2, delta.reshape(tiles), name="attn_bwd")
        dqs.append(dq_h)
        dks.append(dk_h)
        dvs.append(dv_h)
    dqfull, dkv, dkr = _qk_prep_bwd(dqs, dks, dvs, c128, s128, name="qk_prep_bwd")
    dw_uq_p = _mm(cqn, dqfull, mode="tn", out_dtype=MXU, name="mla_q_dw")
    dcqn = _mm(dqfull, w_uq.T, name="mla_q_dx")
    grads['mla_w_ukv'] = _mm(ckvn, dkv, mode="tn", out_dtype=MXU, name="mla_kv_dw")
    dckvn = _mm(dkv, w_ukv.T, name="mla_kv_dx")
    da, dgq, dgkv = _mla_mid_bwd(a, dcqn, dckvn, dkr, g_q, g_kv, c128, s128, name="mla_mid_bwd")
    grads['mla_w_a'] = _mm(hn0, da, mode="tn", out_dtype=MXU, name="mla_a_dw")[:, :KR0 + QK_ROPE]
    dhn0 = _mm(da, w_a.T, name="mla_a_dx")
    dx, dg_mix0 = _rmsnorm_bwd(x2, gm[0:1], dhn0, dh1, name="rms_mix_bwd")
    grads['mla_w_uq'] = jnp.concatenate(
        [dw_uq_p[:, :HEADS * QK_NOPE].reshape(Q_LORA, HEADS, QK_NOPE),
         dw_uq_p[:, HEADS * QK_NOPE:].reshape(Q_LORA, HEADS, LANE)[:, :, :QK_ROPE]], axis=2).reshape(Q_LORA, HEADS * QK_DIM)
    grads['mla_g_q'], grads['mla_g_kv'] = dgq, dgkv
    grads['g_mix'] = jnp.concatenate([dg_mix0, dg_mix1], axis=0)
    grads['g_final'] = dg_final[0]

    landed += _chip_exchange(*_grad_plan(grads, w, MLA_W, True), name="grad_exchange")
    g_out, d_out, m_out, v_out = _reduce_and_update(REST_W + MLA_W, landed, w, mom, var)

    loss = lax.psum(loss_l[0, 0], ("x", "y", "c"))
    return (loss, dx.reshape(1, S, D), *[g_out[n] for n in WNAMES], *[d_out[n] for n in WNAMES],
            *[m_out[n] for n in WNAMES], *[v_out[n] for n in WNAMES])
```

```python
import math

import jax
import jax.numpy as jnp
from jax import lax
from jax.experimental import pallas as pl
from jax.experimental.pallas import tpu as pltpu

F32 = jnp.float32
MXU = jnp.bfloat16

D_MODEL = 1024
CHUNK = 64
EPS = 1e-6
HEADS = 8
QK_NOPE = 128
QK_ROPE = 64
V_HEAD = 128
Q_LORA = 384
KV_LORA = 256
ROPE_THETA = 10000.0
QK_DIM = QK_NOPE + QK_ROPE
SSM_GROUP = 16
SSM_GROUPS = D_MODEL // SSM_GROUP
SSM_STATE = 64
NSTATE = SSM_GROUPS * SSM_STATE
D_FF = 2816
ATT_SCALE = QK_DIM ** -0.5
EXP2_SCALE = ATT_SCALE * math.log2(math.e)
NEG = -1e30
NSEG = 8
SBLK = 8

ADAM_LR = 0.001
ADAM_B1 = 0.9
ADAM_B2 = 0.999
ADAM_EPS = 1e-08
ADAM_WD = 0.01
ADAM_STEP = 10

LANE = 128
TN_MAX_COLS = 2816
NN_MAX_COLS = 1408
NN_MAX_K = 2816
TN_ACC_ELEMS = 1536 * 1024
VMEM_BIG = 56 * 1024 * 1024

WNAMES = ['mla_w_a', 'mla_g_q', 'mla_g_kv', 'mla_w_uq', 'mla_w_ukv', 'mla_w_o', 'ssm_w_in', 'ssm_lambda_re',
          'ssm_lambda_im', 'ssm_log_dt', 'ssm_b_re', 'ssm_b_im', 'ssm_c_re', 'ssm_c_im', 'ssm_d', 'ssm_w_glu',
          'ffn_w_up', 'ffn_conv_w', 'ffn_conv_b', 'ffn_w_down', 'g_mix', 'g_ffn', 'g_final']
FWD_NAMES = ['x', 'positions'] + WNAMES
SHARD_AXIS = {'mla_w_a': 1, 'mla_w_uq': 2, 'mla_w_ukv': 2, 'mla_w_o': 1, 'ssm_w_in': 1, 'ssm_d': 1,
              'ssm_w_glu': 2, 'ffn_w_up': 2, 'ffn_conv_w': 2, 'ffn_w_down': 1}
GATHER_F32 = ['ssm_d', 'ffn_conv_w']
NCHIP = 4
PACKW = 1024
SMALL_ROWS = 64
MESH = pl.DeviceIdType.MESH


def _tile(d, pref):
    t = min(pref, d) // LANE * LANE
    while t >= LANE:
        if d % t == 0:
            return t
        t -= LANE
    return d


def _rows(s, pref):
    t = min(s, pref)
    assert s % t == 0 and t % 8 == 0
    return t


def _params(big=False):
    if big:
        return pltpu.CompilerParams(vmem_limit_bytes=VMEM_BIG)
    return pltpu.CompilerParams(vmem_limit_bytes=40 * 1024 * 1024)


def _mm(a, b, *, name, mode="nn", out_dtype=F32, res=None, tm=None, tn=None, tk=None):
    N = b.shape[0] if mode == "nt" else b.shape[1]
    if mode != "tn":
        M, K = a.shape
        if N > 1024 and N % 512:
            tn = tn or N
            tm = tm or _rows(M, 512)
        narrow = jnp.dtype(a.dtype).itemsize == 2
        tk = tk or _tile(K, NN_MAX_K if narrow else NN_MAX_COLS)
        tm = tm or _rows(M, 2048 if narrow and K <= 1024 else 1024)
    else:
        K, M = a.shape
        tn = tn or _tile(N, TN_MAX_COLS)
        tm = tm or _tile(M, max(LANE, TN_ACC_ELEMS // tn))
    assert b.shape[1 if mode == "nt" else 0] == K
    tn = tn or _tile(N, NN_MAX_COLS if N % 512 else 512)
    tk = tk or _rows(K, 512)
    nk = K // tk
    has_res = res is not None

    def body(a_ref, b_ref, *rest):
        if has_res:
            r_ref, o_ref, acc = rest
        else:
            o_ref, acc = rest
        k = pl.program_id(2)

        @pl.when(k == 0)
        def _():
            acc[...] = jnp.zeros_like(acc)

        av = a_ref[...].astype(MXU)
        bv = b_ref[...].astype(MXU)
        if mode == "nn":
            acc[...] += jnp.dot(av, bv, preferred_element_type=F32)
        elif mode == "nt":
            acc[...] += _dot_nt(av, bv)
        else:
            acc[...] += _dot_tn(av, bv)

        @pl.when(k == nk - 1)
        def _():
            o = acc[...]
            if has_res:
                o = o + r_ref[...]
            o_ref[...] = o.astype(o_ref.dtype)

    if mode == "tn":
        a_spec = pl.BlockSpec((tk, tm), lambda i, j, k: (k, i))
    else:
        a_spec = pl.BlockSpec((tm, tk), lambda i, j, k: (i, k))
    if mode == "nt":
        b_spec = pl.BlockSpec((tn, tk), lambda i, j, k: (j, k))
    else:
        b_spec = pl.BlockSpec((tk, tn), lambda i, j, k: (k, j))
    in_specs = [a_spec, b_spec]
    ops = [a, b]
    if has_res:
        in_specs.append(pl.BlockSpec((tm, tn), lambda i, j, k: (i, j)))
        ops.append(res)
    return pl.pallas_call(
        body, name=name, grid=(M // tm, N // tn, nk), in_specs=in_specs,
        out_specs=pl.BlockSpec((tm, tn), lambda i, j, k: (i, j)),
        out_shape=jax.ShapeDtypeStruct((M, N), out_dtype),
        scratch_shapes=[pltpu.VMEM((tm, tn), F32)], compiler_params=_params(),
    )(*ops)


def _row_spec(tm, c):
    return pl.BlockSpec((tm, c), lambda i: (i, 0))


def _const_spec(r, c):
    return pl.BlockSpec((r, c), lambda i: (0, 0))


def _rms_parts(xv):
    r = lax.rsqrt(jnp.mean(xv * xv, axis=-1, keepdims=True) + EPS)
    return r, xv * r


def _rms_vjp(xv, gv, dyv):
    r, xhat = _rms_parts(xv)
    gy = dyv * gv
    dx = r * (gy - xhat * jnp.mean(gy * xhat, axis=-1, keepdims=True))
    return dx, dyv * xhat


def _rmsnorm_fwd(x, g, *, name):
    S, D = x.shape
    tm = _rows(S, 512)

    def body(x_ref, g_ref, o_ref):
        _, xhat = _rms_parts(x_ref[...])
        o_ref[...] = (xhat * g_ref[...]).astype(o_ref.dtype)

    return pl.pallas_call(
        body, name=name, grid=(S // tm,), in_specs=[_row_spec(tm, D), _const_spec(1, D)],
        out_specs=_row_spec(tm, D), out_shape=jax.ShapeDtypeStruct((S, D), MXU), compiler_params=_params(),
    )(x, g)


def _rmsnorm_bwd(x, g, dy, dres, *, name):
    S, D = x.shape
    tm = _rows(S, 512)

    def body(x_ref, g_ref, dy_ref, dr_ref, dx_ref, dg_ref):
        @pl.when(pl.program_id(0) == 0)
        def _():
            dg_ref[...] = jnp.zeros_like(dg_ref)

        dx, dgp = _rms_vjp(x_ref[...], g_ref[...], dy_ref[...])
        dx_ref[...] = dr_ref[...] + dx
        dg_ref[...] += jnp.sum(dgp, axis=0, keepdims=True)

    return pl.pallas_call(
        body, name=name, grid=(S // tm,),
        in_specs=[_row_spec(tm, D), _const_spec(1, D), _row_spec(tm, D), _row_spec(tm, D)],
        out_specs=(_row_spec(tm, D), _const_spec(1, D)),
        out_shape=(jax.ShapeDtypeStruct((S, D), F32), jax.ShapeDtypeStruct((1, D), F32)),
        compiler_params=_params(),
    )(x, g, dy, dres)


def _loss_head(h, g, tgt, *, name):
    S, D = h.shape
    tm = _rows(S, 512)

    def body(h_ref, g_ref, t_ref, l_ref, dh_ref, dg_ref):
        @pl.when(pl.program_id(0) == 0)
        def _():
            l_ref[...] = jnp.zeros_like(l_ref)
            dg_ref[...] = jnp.zeros_like(dg_ref)

        hv = h_ref[...]
        gv = g_ref[...]
        _, xhat = _rms_parts(hv)
        e = xhat * gv - t_ref[...]
        l_ref[...] += 0.5 * jnp.sum(jnp.mean(e * e, axis=-1, keepdims=True), axis=0, keepdims=True)
        dx, dgp = _rms_vjp(hv, gv, e * (1.0 / D))
        dh_ref[...] = dx
        dg_ref[...] += jnp.sum(dgp, axis=0, keepdims=True)

    return pl.pallas_call(
        body, name=name, grid=(S // tm,),
        in_specs=[_row_spec(tm, D), _const_spec(1, D), _row_spec(tm, D)],
        out_specs=(_const_spec(1, 1), _row_spec(tm, D), _const_spec(1, D)),
        out_shape=(jax.ShapeDtypeStruct((1, 1), F32), jax.ShapeDtypeStruct((S, D), F32),
                   jax.ShapeDtypeStruct((1, D), F32)),
        compiler_params=_params(),
    )(h, g, tgt)


def _swap_halves(g):
    lane = lax.broadcasted_iota(jnp.int32, g.shape, 1)
    return jnp.where(lane < QK_ROPE // 2, pltpu.roll(g, LANE - QK_ROPE // 2, axis=1),
                     pltpu.roll(g, QK_ROPE // 2, axis=1))


def _rope128(g, c128, s128):
    return g * c128 + _swap_halves(g) * s128


def _rope128_vjp(dy, c128, s128):
    lane = lax.broadcasted_iota(jnp.int32, dy.shape, 1)
    return jnp.where(lane < QK_ROPE, dy * c128 + _swap_halves(dy * s128), 0.0)


A_PAD = 768
KR0 = Q_LORA + KV_LORA


def _mla_mid_fwd(a, g_q, g_kv, c128, s128, *, name):
    S = a.shape[0]
    tm = _rows(S, 512)

    def body(a_ref, gq_ref, gkv_ref, c_ref, s_ref, cq_ref, ckv_ref, kr_ref):
        av = a_ref[...]
        _, qh = _rms_parts(av[:, :Q_LORA])
        cq_ref[...] = (qh * gq_ref[...]).astype(cq_ref.dtype)
        _, kh = _rms_parts(av[:, Q_LORA:KR0])
        ckv_ref[...] = (kh * gkv_ref[...]).astype(ckv_ref.dtype)
        kr = _rope128(av[:, KR0:A_PAD], c_ref[...], s_ref[...])
        kr_ref[...] = kr[:, :QK_ROPE].astype(kr_ref.dtype)

    return pl.pallas_call(
        body, name=name, grid=(S // tm,),
        in_specs=[_row_spec(tm, A_PAD), _const_spec(1, Q_LORA), _const_spec(1, KV_LORA), _row_spec(tm, LANE),
                  _row_spec(tm, LANE)],
        out_specs=(_row_spec(tm, Q_LORA), _row_spec(tm, KV_LORA), _row_spec(tm, QK_ROPE)),
        out_shape=(jax.ShapeDtypeStruct((S, Q_LORA), MXU), jax.ShapeDtypeStruct((S, KV_LORA), MXU),
                   jax.ShapeDtypeStruct((S, QK_ROPE), MXU)),
        compiler_params=_params(),
    )(a, g_q, g_kv, c128, s128)


def _mla_mid_bwd(a, dcq, dckv, dkr, g_q, g_kv, c128, s128, *, name):
    S = a.shape[0]
    tm = _rows(S, 512)

    def body(a_ref, dcq_ref, dckv_ref, dkr_ref, gq_ref, gkv_ref, c_ref, s_ref, da_ref, dgq_ref, dgkv_ref):
        @pl.when(pl.program_id(0) == 0)
        def _():
            dgq_ref[...] = jnp.zeros_like(dgq_ref)
            dgkv_ref[...] = jnp.zeros_like(dgkv_ref)

        av = a_ref[...]
        dx, dgp = _rms_vjp(av[:, :Q_LORA], gq_ref[...], dcq_ref[...])
        da_ref[:, :Q_LORA] = dx.astype(da_ref.dtype)
        dgq_ref[...] += jnp.sum(dgp, axis=0, keepdims=True)
        dx, dgp = _rms_vjp(av[:, Q_LORA:KR0], gkv_ref[...], dckv_ref[...])
        da_ref[:, Q_LORA:KR0] = dx.astype(da_ref.dtype)
        dgkv_ref[...] += jnp.sum(dgp, axis=0, keepdims=True)
        da_ref[:, KR0:A_PAD] = _rope128_vjp(dkr_ref[...], c_ref[...], s_ref[...]).astype(da_ref.dtype)

    return pl.pallas_call(
        body, name=name, grid=(S // tm,),
        in_specs=[_row_spec(tm, A_PAD), _row_spec(tm, Q_LORA), _row_spec(tm, KV_LORA), _row_spec(tm, LANE),
                  _const_spec(1, Q_LORA), _const_spec(1, KV_LORA), _row_spec(tm, LANE), _row_spec(tm, LANE)],
        out_specs=(_row_spec(tm, A_PAD), _const_spec(1, Q_LORA), _const_spec(1, KV_LORA)),
        out_shape=(jax.ShapeDtypeStruct((S, A_PAD), MXU), jax.ShapeDtypeStruct((1, Q_LORA), F32),
                   jax.ShapeDtypeStruct((1, KV_LORA), F32)),
        compiler_params=_params(),
    )(a, dcq, dckv, dkr, g_q, g_kv, c128, s128)


QF = 2 * HEADS * LANE
KVF = HEADS * (QK_NOPE + V_HEAD)
VX = 2 * V_HEAD


def _qk_prep(qfull, kv, kr, c128, s128, *, name):
    S = qfull.shape[0]
    tm = _rows(S, 256)

    def body(q_ref, kv_ref, kr_ref, c_ref, s_ref, *outs):
        qo, ko, vo = outs[:HEADS], outs[HEADS:2 * HEADS], outs[2 * HEADS:]
        cv, sv = c_ref[...], s_ref[...]
        krv = kr_ref[...]
        for h in range(HEADS):
            qo[h][:, :QK_NOPE] = q_ref[:, h * LANE:(h + 1) * LANE].astype(MXU)
            g = q_ref[:, (HEADS + h) * LANE:(HEADS + h + 1) * LANE]
            qo[h][:, QK_NOPE:] = _rope128(g, cv, sv)[:, :QK_ROPE].astype(MXU)
            ko[h][:, :QK_NOPE] = kv_ref[:, 2 * h * LANE:(2 * h + 1) * LANE]
            ko[h][:, QK_NOPE:] = krv
            vo[h][:, :V_HEAD] = kv_ref[:, (2 * h + 1) * LANE:(2 * h + 2) * LANE]
            vo[h][:, V_HEAD:] = jnp.ones((tm, VX - V_HEAD), MXU)

    shapes = ([jax.ShapeDtypeStruct((S, QK_DIM), MXU)] * (2 * HEADS)
              + [jax.ShapeDtypeStruct((S, VX), MXU)] * HEADS)
    specs = [_row_spec(tm, QK_DIM)] * (2 * HEADS) + [_row_spec(tm, VX)] * HEADS
    outs = pl.pallas_call(
        body, name=name, grid=(S // tm,),
        in_specs=[_row_spec(tm, QF), _row_spec(tm, KVF), _row_spec(tm, QK_ROPE), _row_spec(tm, LANE),
                  _row_spec(tm, LANE)],
        out_specs=tuple(specs), out_shape=tuple(shapes), compiler_params=_params(),
    )(qfull, kv, kr, c128, s128)
    return outs[:HEADS], outs[HEADS:2 * HEADS], outs[2 * HEADS:]


def _qk_prep_bwd(dqs, dks, dvs, c128, s128, *, name):
    S = dqs[0].shape[0]
    tm = _rows(S, 256)

    def body(*refs):
        dq = refs[:HEADS]
        dk = refs[HEADS:2 * HEADS]
        dv = refs[2 * HEADS:3 * HEADS]
        c_ref, s_ref, dqf_ref, dkv_ref, dkr_ref, tmp = refs[3 * HEADS:]
        cv, sv = c_ref[...], s_ref[...]
        tmp[...] = jnp.zeros_like(tmp)
        dkr_ref[...] = jnp.zeros_like(dkr_ref)
        for h in range(HEADS):
            dqf_ref[:, h * LANE:(h + 1) * LANE] = dq[h][:, :QK_NOPE].astype(MXU)
            tmp[:, :QK_ROPE] = dq[h][:, QK_NOPE:]
            dqf_ref[:, (HEADS + h) * LANE:(HEADS + h + 1) * LANE] = _rope128_vjp(tmp[...], cv, sv).astype(MXU)
            dkv_ref[:, 2 * h * LANE:(2 * h + 1) * LANE] = dk[h][:, :QK_NOPE].astype(MXU)
            dkv_ref[:, (2 * h + 1) * LANE:(2 * h + 2) * LANE] = dv[h][...].astype(MXU)
            dkr_ref[:, :QK_ROPE] += dk[h][:, QK_NOPE:]

    return pl.pallas_call(
        body, name=name, grid=(S // tm,),
        in_specs=[_row_spec(tm, QK_DIM)] * (2 * HEADS) + [_row_spec(tm, V_HEAD)] * HEADS
        + [_row_spec(tm, LANE), _row_spec(tm, LANE)],
        out_specs=(_row_spec(tm, QF), _row_spec(tm, KVF), _row_spec(tm, LANE)),
        out_shape=(jax.ShapeDtypeStruct((S, QF), MXU), jax.ShapeDtypeStruct((S, KVF), MXU),
                   jax.ShapeDtypeStruct((S, LANE), F32)),
        scratch_shapes=[pltpu.VMEM((tm, LANE), F32)], compiler_params=_params(),
    )(*dqs, *dks, *dvs, c128, s128)


def _dot_nt(a, b):
    return lax.dot_general(a, b, (((1,), (1,)), ((), ())), preferred_element_type=F32)


def _dot_tn(a, b):
    return lax.dot_general(a, b, (((0,), (0,)), ((), ())), preferred_element_type=F32)


def _attn_fwd(q, k, vx, *, name, ride=None):
    S = q.shape[0]
    T = _rows(S, 1024)
    n = S // T
    cpt = T // CHUNK

    r_srcs, r_outs, r_pieces = ride or ((), (), ())
    ns, no = len(r_srcs), len(r_outs)

    def body(q_ref, k_ref, v_ref, *rest):
        o_ref, lse_ref = rest[ns:ns + 2]
        s_buf, p_buf, a_buf, m_s, acc_s = rest[ns + 2 + no:ns + 7 + no]
        i = pl.program_id(0)
        if ride:
            start, finish = _exchange_ops(r_pieces, rest[:ns], rest[ns + 2:ns + 2 + no], *rest[ns + 7 + no:])
            pl.when(i == 0)(start)
        qc = lax.broadcasted_iota(jnp.int32, (T, T), 0) // CHUNK
        kc = lax.broadcasted_iota(jnp.int32, (T, T), 1) // CHUNK
        dchunk = kc - qc

        def tile_rows(b):
            return pl.ds(pl.multiple_of(jnp.clip(b, 0, n - 1) * T, T), T)

        def scores(b, slot):
            s = _dot_nt(q_ref[...], k_ref[tile_rows(b), :])
            s_buf[slot] = jnp.where(dchunk <= (i - b) * cpt, s, NEG)

        def softmax(slot):
            s = s_buf[slot]
            m_prev = m_s[...]
            m_new = jnp.maximum(m_prev, jnp.max(s, axis=1, keepdims=True))
            a_buf[slot] = jnp.exp2((m_prev - m_new) * EXP2_SCALE)
            p_buf[slot] = jnp.exp2((s - m_new) * EXP2_SCALE).astype(MXU)
            m_s[...] = m_new

        def pv(b, slot):
            acc_s[...] = a_buf[slot] * acc_s[...] + jnp.dot(p_buf[slot], v_ref[tile_rows(b), :],
                                                              preferred_element_type=F32)

        m_s[...] = jnp.full_like(m_s, NEG)
        acc_s[...] = jnp.zeros_like(acc_s)
        p_buf[1] = jnp.zeros((T, T), MXU)
        a_buf[1] = jnp.ones((T, 1), F32)
        scores(0, 0)

        def pair(u, carry):
            t = 2 * u
            scores(t + 1, 1)
            softmax(0)
            pv(t - 1, 1)
            scores(t + 2, 0)
            softmax(1)
            pv(t, 0)
            return carry

        npairs = (i + 2) // 2
        lax.fori_loop(0, npairs, pair, 0)
        pv(2 * npairs - 1, 1)
        acc = acc_s[...]
        l = acc[:, V_HEAD:V_HEAD + 1]
        o_ref[...] = (acc[:, :V_HEAD] / l).astype(o_ref.dtype)
        lse_ref[...] = m_s[...] * ATT_SCALE + jnp.log(l)
        if ride:
            pl.when(i == n - 1)(finish)

    hbm = pl.BlockSpec(memory_space=pl.ANY)
    return pl.pallas_call(
        body, name=name, grid=(n,),
        in_specs=[pl.BlockSpec((T, QK_DIM), lambda i: (i, 0)), pl.BlockSpec((S, QK_DIM), lambda i: (0, 0)),
                  pl.BlockSpec((S, VX), lambda i: (0, 0))] + [hbm] * ns,
        out_specs=(pl.BlockSpec((T, V_HEAD), lambda i: (i, 0)), pl.BlockSpec((T, 1), lambda i: (i, 0))) + (hbm,) * no,
        out_shape=(jax.ShapeDtypeStruct((S, V_HEAD), MXU), jax.ShapeDtypeStruct((S, 1), F32)) + tuple(r_outs),
        scratch_shapes=[pltpu.VMEM((2, T, T), F32), pltpu.VMEM((2, T, T), MXU), pltpu.VMEM((2, T, 1), F32),
                        pltpu.VMEM((T, 1), F32), pltpu.VMEM((T, VX), F32)]
        + (_exchange_scratch(len(r_pieces)) if ride else []),
        compiler_params=_params(big=True),
    )(q, k, vx, *r_srcs)


def _attn_delta(do, o, *, name):
    S = do.shape[0]
    tm = _rows(S, 1024)

    def body(do_ref, o_ref, d_ref):
        d_ref[...] = jnp.sum(do_ref[...].astype(F32) * o_ref[...].astype(F32), axis=1, keepdims=True)

    return pl.pallas_call(
        body, name=name, grid=(S // tm,), in_specs=[_row_spec(tm, V_HEAD), _row_spec(tm, V_HEAD)],
        out_specs=_row_spec(tm, 1), out_shape=jax.ShapeDtypeStruct((S, 1), F32), compiler_params=_params(),
    )(do, o)


BWD_T = 512


def _attn_bwd(q, k, v, do, lse2, delta, *, name, ride=None):
    S = q.shape[0]
    T = _rows(S, BWD_T)
    n = S // T
    cpt = T // CHUNK

    r_srcs, r_outs, r_pieces = ride or ((), (), ())
    ns, no = len(r_srcs), len(r_outs)

    def body(q_hbm, k_ref, v_ref, do_hbm, lse_ref, dl_ref, *rest):
        dq_hbm, dk_ref, dv_ref = rest[ns:ns + 3]
        q_res, do_res, dq_s, s_buf, dp_buf, p_buf, ds_buf, dk_s, dv_s = rest[ns + 3 + no:ns + 12 + no]
        j = pl.program_id(0)
        if ride:
            start, finish = _exchange_ops(r_pieces, rest[:ns], rest[ns + 3:ns + 3 + no], *rest[ns + 12 + no:])
            pl.when(j == 0)(start)

        @pl.when(j == 0)
        def _():
            pltpu.sync_copy(q_hbm, q_res)
            pltpu.sync_copy(do_hbm, do_res)
            dq_s[...] = jnp.zeros_like(dq_s)

        kc = lax.broadcasted_iota(jnp.int32, (T, T), 0) // CHUNK
        qc = lax.broadcasted_iota(jnp.int32, (T, T), 1) // CHUNK
        dchunk = kc - qc

        def tile(t):
            return jnp.clip(j + t, 0, n - 1)

        def rows(t):
            return pl.ds(pl.multiple_of(tile(t) * T, T), T)

        def scores(t, slot):
            visible_up_to = jnp.where(j + t < n, t * cpt, -2 * cpt)
            s = _dot_nt(k_ref[...], q_res[rows(t), :])
            s_buf[slot] = jnp.where(dchunk <= visible_up_to, s, NEG)
            dp_buf[slot] = _dot_nt(v_ref[...], do_res[rows(t), :])

        def probs(t, slot):
            pt = jnp.exp2(s_buf[slot] * EXP2_SCALE - lse_ref[tile(t)])
            p_buf[slot] = pt.astype(MXU)
            ds_buf[slot] = (pt * (dp_buf[slot] - dl_ref[tile(t)]) * ATT_SCALE).astype(MXU)

        def grads(t, slot):
            r = rows(t)
            dv_s[...] += jnp.dot(p_buf[slot], do_res[r, :], preferred_element_type=F32)
            ds = ds_buf[slot]
            dk_s[...] += jnp.dot(ds, q_res[r, :], preferred_element_type=F32)
            dq_s[r, :] += _dot_tn(ds, k_ref[...])

        dk_s[...] = jnp.zeros_like(dk_s)
        dv_s[...] = jnp.zeros_like(dv_s)
        p_buf[1] = jnp.zeros((T, T), MXU)
        ds_buf[1] = jnp.zeros((T, T), MXU)
        scores(0, 0)

        def pair(u, carry):
            t = 2 * u
            scores(t + 1, 1)
            probs(t, 0)
            grads(t - 1, 1)
            scores(t + 2, 0)
            probs(t + 1, 1)
            grads(t, 0)
            return carry

        npairs = (n - j + 1) // 2
        lax.fori_loop(0, npairs, pair, 0)
        grads(2 * npairs - 1, 1)
        dk_ref[...] = dk_s[...]
        dv_ref[...] = dv_s[...]

        @pl.when(j == n - 1)
        def _():
            pltpu.sync_copy(dq_s, dq_hbm)
            if ride:
                finish()

    hbm = pl.BlockSpec(memory_space=pl.ANY)
    k_map = lambda j: (j, 0)
    whole = pl.BlockSpec((n, 1, T), lambda j: (0, 0, 0))
    return pl.pallas_call(
        body, name=name, grid=(n,),
        in_specs=[hbm, pl.BlockSpec((T, QK_DIM), k_map), pl.BlockSpec((T, V_HEAD), k_map), hbm, whole, whole]
        + [hbm] * ns,
        out_specs=(hbm, pl.BlockSpec((T, QK_DIM), k_map), pl.BlockSpec((T, V_HEAD), k_map)) + (hbm,) * no,
        out_shape=(jax.ShapeDtypeStruct((S, QK_DIM), F32), jax.ShapeDtypeStruct((S, QK_DIM), F32),
                   jax.ShapeDtypeStruct((S, V_HEAD), F32)) + tuple(r_outs),
        scratch_shapes=[pltpu.VMEM((S, QK_DIM), MXU), pltpu.VMEM((S, V_HEAD), MXU), pltpu.VMEM((S, QK_DIM), F32),
                        pltpu.VMEM((2, T, T), F32), pltpu.VMEM((2, T, T), F32), pltpu.VMEM((2, T, T), MXU),
                        pltpu.VMEM((2, T, T), MXU), pltpu.VMEM((T, QK_DIM), F32), pltpu.VMEM((T, V_HEAD), F32)]
        + (_exchange_scratch(len(r_pieces)) if ride else []),
        compiler_params=_params(big=True),
    )(q, k, v, do, lse2, delta, *r_srcs)


HALO = 8
CONV_RH = 64


def _conv_tiles(S):
    tm = _rows(S, 256)
    tc = D_FF // 2
    return tm, tc, D_FF // tc


def _silu_parts(gate):
    sg = jax.nn.sigmoid(gate)
    return sg, gate * sg


def _convgate_fwd(up, cw, cb, *, name):
    S = up.shape[0]
    tm, tc, nc = _conv_tiles(S)
    hb = tm // HALO

    def body(v_ref, g_ref, hv_ref, hg_ref, wv_ref, wg_ref, bv_ref, bg_ref, o_ref, cv_ref, cg_ref):
        keep = (pl.program_id(0) > 0).astype(F32)

        def chunk(cc, carry):
            cols = pl.ds(pl.multiple_of(cc * LANE, LANE), LANE)
            wv, wg, bv, bg = wv_ref[:, cols], wg_ref[:, cols], bv_ref[:, cols], bg_ref[:, cols]
            for r0 in range(0, tm, CONV_RH):
                def conv(t_ref, h_ref, w, b):
                    if r0:
                        span = t_ref[pl.ds(r0 - HALO, CONV_RH + HALO), cols]
                    else:
                        span = jnp.concatenate([h_ref[:, cols] * keep, t_ref[pl.ds(0, CONV_RH), cols]], axis=0)
                    back2, back1 = [pltpu.roll(span, s, axis=0)[HALO:] for s in (2, 1)]
                    return w[0:1] * back2 + w[1:2] * back1 + w[2:3] * span[HALO:] + b
                val = conv(v_ref, hv_ref, wv, bv)
                gate = conv(g_ref, hg_ref, wg, bg)
                cv_ref[pl.ds(r0, CONV_RH), cols] = val
                cg_ref[pl.ds(r0, CONV_RH), cols] = gate
                o_ref[pl.ds(r0, CONV_RH), cols] = (_silu_parts(gate)[1] * val).astype(o_ref.dtype)
            return carry

        lax.fori_loop(0, tc // LANE, chunk, 0)

    prev = lambda i: jnp.maximum(i * hb - 1, 0)
    return pl.pallas_call(
        body, name=name, grid=(S // tm, nc),
        in_specs=[pl.BlockSpec((tm, tc), lambda i, j: (i, j)), pl.BlockSpec((tm, tc), lambda i, j: (i, j + nc)),
                  pl.BlockSpec((HALO, tc), lambda i, j: (prev(i), j)),
                  pl.BlockSpec((HALO, tc), lambda i, j: (prev(i), j + nc)),
                  pl.BlockSpec((3, tc), lambda i, j: (0, j)), pl.BlockSpec((3, tc), lambda i, j: (0, j + nc)),
                  pl.BlockSpec((1, tc), lambda i, j: (0, j)), pl.BlockSpec((1, tc), lambda i, j: (0, j + nc))],
        out_specs=(pl.BlockSpec((tm, tc), lambda i, j: (i, j)),) * 3,
        out_shape=(jax.ShapeDtypeStruct((S, D_FF), MXU), jax.ShapeDtypeStruct((S, D_FF), F32),
                   jax.ShapeDtypeStruct((S, D_FF), F32)),
        compiler_params=_params(),
    )(up, up, up, up, cw, cw, cb, cb)


def _convgate_bwd(up, cval, cgate, dact, cw, *, name):
    S = up.shape[0]
    tm, tc, nc = _conv_tiles(S)
    hb = tm // HALO
    nr = S // tm
    R = tm + HALO

    def body(uv_ref, ug_ref, cv_ref, cg_ref, ncv_ref, ncg_ref, da_ref, dan_ref, wv_ref, wg_ref,
             duv_ref, dug_ref, dwv_ref, dwg_ref, dbv_ref, dbg_ref, dsv, dsg):
        i = pl.program_id(1)

        @pl.when(i == 0)
        def _():
            for r in (dwv_ref, dwg_ref, dbv_ref, dbg_ref):
                r[...] = jnp.zeros_like(r)

        keep_next = (i < nr - 1).astype(F32)

        def chunk(cc, carry):
            cols = pl.ds(pl.multiple_of(cc * LANE, LANE), LANE)

            def d_conv(rows, val, gate, d):
                sg, silu = _silu_parts(gate)
                dsv[rows, cols] = d * silu
                dsg[rows, cols] = d * val * (sg * (1.0 + gate * (1.0 - sg)))

            for r0 in range(0, tm, CONV_RH):
                rows = pl.ds(r0, CONV_RH)
                d_conv(rows, cv_ref[rows, cols], cg_ref[rows, cols], da_ref[rows, cols])
            d_conv(pl.ds(tm, HALO), ncv_ref[:, cols], ncg_ref[:, cols], dan_ref[:, cols] * keep_next)
            for ds, u_ref, w_ref, du_ref, dw_ref, db_ref in ((dsv, uv_ref, wv_ref, duv_ref, dwv_ref, dbv_ref),
                                                            (dsg, ug_ref, wg_ref, dug_ref, dwg_ref, dbg_ref)):
                w = w_ref[:, cols]
                acc = [jnp.zeros((1, LANE), F32) for _ in range(4)]
                for r0 in range(0, tm, CONV_RH):
                    rows = pl.ds(r0, CONV_RH)
                    span = ds[pl.ds(r0, CONV_RH + HALO), cols]
                    d = [span[:CONV_RH]] + [pltpu.roll(span, CONV_RH + HALO - s, axis=0)[:CONV_RH] for s in (1, 2)]
                    du_ref[rows, cols] = (w[2:3] * d[0] + w[1:2] * d[1] + w[0:1] * d[2]).astype(du_ref.dtype)
                    u = u_ref[rows, cols]
                    for kk in range(3):
                        acc[kk] = acc[kk] + jnp.sum(d[2 - kk] * u, axis=0, keepdims=True)
                    acc[3] = acc[3] + jnp.sum(d[0], axis=0, keepdims=True)
                for kk in range(3):
                    dw_ref[kk:kk + 1, cols] += acc[kk]
                db_ref[:, cols] += acc[3]
            return carry

        lax.fori_loop(0, tc // LANE, chunk, 0)

    nxt = lambda i: jnp.minimum((i + 1) * hb, S // HALO - 1)
    tile_v = pl.BlockSpec((tm, tc), lambda j, i: (i, j))
    tile_g = pl.BlockSpec((tm, tc), lambda j, i: (i, j + nc))
    halo = pl.BlockSpec((HALO, tc), lambda j, i: (nxt(i), j))
    w_v = pl.BlockSpec((3, tc), lambda j, i: (0, j))
    w_g = pl.BlockSpec((3, tc), lambda j, i: (0, j + nc))
    b_v = pl.BlockSpec((1, tc), lambda j, i: (0, j))
    return pl.pallas_call(
        body, name=name, grid=(nc, nr),
        in_specs=[tile_v, tile_g, tile_v, tile_v, halo, halo, tile_v, halo, w_v, w_g],
        out_specs=(tile_v, tile_v, w_v, w_v, b_v, b_v),
        out_shape=(jax.ShapeDtypeStruct((S, D_FF), MXU), jax.ShapeDtypeStruct((S, D_FF), MXU),
                   jax.ShapeDtypeStruct((3, D_FF), F32), jax.ShapeDtypeStruct((3, D_FF), F32),
                   jax.ShapeDtypeStruct((1, D_FF), F32), jax.ShapeDtypeStruct((1, D_FF), F32)),
        scratch_shapes=[pltpu.VMEM((R, tc), F32), pltpu.VMEM((R, tc), F32)],
        compiler_params=_params(),
    )(up, up, cval, cgate, cval, cgate, dact, dact, cw, cw)


def _glu_fwd(z, h, *, name):
    S = z.shape[0]
    tm = _rows(S, 512)

    def body(z_ref, h_ref, o_ref):
        o_ref[...] = h_ref[...] + z_ref[:, :D_MODEL] * jax.nn.sigmoid(z_ref[:, D_MODEL:])

    return pl.pallas_call(
        body, name=name, grid=(S // tm,), in_specs=[_row_spec(tm, 2 * D_MODEL), _row_spec(tm, D_MODEL)],
        out_specs=_row_spec(tm, D_MODEL), out_shape=jax.ShapeDtypeStruct((S, D_MODEL), F32),
        compiler_params=_params(),
    )(z, h)


def _glu_bwd(z, dm, *, name):
    S = z.shape[0]
    tm = _rows(S, 512)

    def body(z_ref, dm_ref, o_ref):
        sg = jax.nn.sigmoid(z_ref[:, D_MODEL:])
        dmv = dm_ref[...]
        o_ref[:, :D_MODEL] = (dmv * sg).astype(o_ref.dtype)
        o_ref[:, D_MODEL:] = (dmv * z_ref[:, :D_MODEL] * sg * (1.0 - sg)).astype(o_ref.dtype)

    return pl.pallas_call(
        body, name=name, grid=(S // tm,), in_specs=[_row_spec(tm, 2 * D_MODEL), _row_spec(tm, D_MODEL)],
        out_specs=_row_spec(tm, 2 * D_MODEL), out_shape=jax.ShapeDtypeStruct((S, 2 * D_MODEL), MXU),
        compiler_params=_params(),
    )(z, dm)


GELU_C = math.sqrt(2.0 / math.pi)
GELU_A = 0.044715


def _gelu(y):
    return 0.5 * y * (1.0 + jnp.tanh(GELU_C * (y + GELU_A * (y * y * y))))


def _gelu_bwd(y, dg, *, name):
    S = y.shape[0]
    tm = _rows(S, 512)

    def body(y_ref, dg_ref, o_ref):
        yv = y_ref[...]
        t = jnp.tanh(GELU_C * (yv + GELU_A * (yv * yv * yv)))
        d = 0.5 * (1.0 + t) + 0.5 * yv * (1.0 - t * t) * (GELU_C * (1.0 + 3.0 * GELU_A * (yv * yv)))
        o_ref[...] = dg_ref[...] * d

    return pl.pallas_call(
        body, name=name, grid=(S // tm,), in_specs=[_row_spec(tm, D_MODEL), _row_spec(tm, D_MODEL)],
        out_specs=_row_spec(tm, D_MODEL), out_shape=jax.ShapeDtypeStruct((S, D_MODEL), F32),
        compiler_params=_params(),
    )(y, dg)


FWD_STRIPS = 4
BWD_STRIPS = 8
SW = NSTATE // SBLK
ST2 = 2 * NSTATE


def _s5_fwd(u, wb, wc, abc, dskip, x0, *, full, name):
    S = u.shape[0]
    T = _rows(S, 256)
    nb = S // T
    nj = T // NSEG

    def body(u_ref, wb_ref, wc_ref, a_ref, d_ref, x0_ref, *rest):
        if full:
            xs_ref, y_ref, yg_ref, st = rest
        else:
            e_ref, xs_ref, st = rest
        i = pl.program_id(0)

        @pl.when(i == 0)
        def _():
            st[...] = x0_ref[...]

        uv = u_ref[...]
        ub = uv.astype(MXU)
        for kb in range(SBLK):
            r = jnp.dot(ub[:, kb * LANE:(kb + 1) * LANE], wb_ref[kb], preferred_element_type=F32)
            xs_ref[:, kb * SW:(kb + 1) * SW] = r[:, :SW]
            xs_ref[:, NSTATE + kb * SW:NSTATE + (kb + 1) * SW] = r[:, SW:]
        for sp in range(FWD_STRIPS):
            w = NSTATE // FWD_STRIPS
            re, im = pl.ds(sp * w, w), pl.ds(NSTATE + sp * w, w)
            ar, ai = a_ref[:, re], a_ref[:, im]

            def step(j, c):
                xr, xi = c
                rows = pl.ds(pl.multiple_of(j * NSEG, NSEG), NSEG)
                nr = ar * xr - ai * xi + xs_ref[rows, re]
                ni = ar * xi + ai * xr + xs_ref[rows, im]
                xs_ref[rows, re] = nr
                xs_ref[rows, im] = ni
                return nr, ni

            xr, xi = lax.fori_loop(0, nj, step, (st[:, re], st[:, im]))
            st[:, re] = xr
            st[:, im] = xi
        if full:
            for kb in range(SBLK):
                yk = (jnp.dot(xs_ref[:, kb * SW:(kb + 1) * SW].astype(MXU), wc_ref[kb, :SW, :], preferred_element_type=F32)
                      + jnp.dot(xs_ref[:, NSTATE + kb * SW:NSTATE + (kb + 1) * SW].astype(MXU), wc_ref[kb, SW:, :],
                                preferred_element_type=F32))
                cols = slice(kb * LANE, (kb + 1) * LANE)
                yk = yk + d_ref[:, cols] * uv[:, cols]
                y_ref[:, cols] = yk
                yg_ref[:, cols] = _gelu(yk).astype(yg_ref.dtype)
        else:
            @pl.when(i == nb - 1)
            def _():
                e_ref[...] = st[...]

    in_specs = [_row_spec(T, D_MODEL), pl.BlockSpec((SBLK, LANE, 2 * SW), lambda i: (0, 0, 0)),
                pl.BlockSpec((SBLK, 2 * SW, LANE), lambda i: (0, 0, 0)), _const_spec(NSEG, ST2),
                _const_spec(1, D_MODEL), _const_spec(NSEG, ST2)]
    if full:
        out_specs = (_row_spec(T, ST2), _row_spec(T, D_MODEL), _row_spec(T, D_MODEL))
        out_shape = (jax.ShapeDtypeStruct((S, ST2), F32), jax.ShapeDtypeStruct((S, D_MODEL), F32),
                     jax.ShapeDtypeStruct((S, D_MODEL), MXU))
        scratch = [pltpu.VMEM((NSEG, ST2), F32)]
    else:
        out_specs = _const_spec(NSEG, ST2)
        out_shape = jax.ShapeDtypeStruct((NSEG, ST2), F32)
        scratch = [pltpu.VMEM((T, ST2), F32), pltpu.VMEM((NSEG, ST2), F32)]
    return pl.pallas_call(
        body, name=name, grid=(nb,), in_specs=in_specs, out_specs=out_specs, out_shape=out_shape,
        scratch_shapes=scratch, compiler_params=_params(big=True),
    )(u, wb, wc, abc, dskip, x0)


def _s5_bwd(dy, xs, u, wct, wbt, abc, dskip, x0, l0, *, full, name):
    S = dy.shape[0]
    T = _rows(S, 256)
    nb = S // T
    nj = T // NSEG
    blk = lambda i: nb - 1 - i

    def body(dy_ref, *rest):
        if full:
            (xs_ref, xh_ref, u_ref, wct_ref, wbt_ref, a_ref, d_ref, x0_ref, l0_ref,
             du_ref, da_ref, dwb_hbm, dwc_hbm, dd_ref, g_s, lam_s, dwb_ref, dwc_ref) = rest
        else:
            wct_ref, a_ref, l0_ref, f_ref, g_s, lam_s = rest
        i = pl.program_id(0)

        @pl.when(i == 0)
        def _():
            lam_s[...] = l0_ref[...]
            if full:
                for r in (da_ref, dwb_ref, dwc_ref, dd_ref):
                    r[...] = jnp.zeros_like(r)

        dyv = dy_ref[...]
        dyb = dyv.astype(MXU)
        for kb in range(SBLK):
            r = jnp.dot(dyb[:, kb * LANE:(kb + 1) * LANE], wct_ref[kb], preferred_element_type=F32)
            g_s[:, kb * SW:(kb + 1) * SW] = r[:, :SW]
            g_s[:, NSTATE + kb * SW:NSTATE + (kb + 1) * SW] = r[:, SW:]
        for sp in range(BWD_STRIPS):
            w = NSTATE // BWD_STRIPS
            re, im = pl.ds(sp * w, w), pl.ds(NSTATE + sp * w, w)
            ar, ai = a_ref[:, re], a_ref[:, im]

            def advance(row, lr, li):
                rows = pl.ds(row, NSEG)
                nr = g_s[rows, re] + ar * lr + ai * li
                ni = g_s[rows, im] - ai * lr + ar * li
                g_s[rows, re] = nr
                g_s[rows, im] = ni
                return nr, ni

            def step(jj, c):
                row = pl.multiple_of((nj - 1 - jj) * NSEG, NSEG)
                nr, ni = advance(row, c[0], c[1])
                if not full:
                    return nr, ni
                prow = pl.ds(pl.multiple_of(row - NSEG, NSEG), NSEG)
                xpr, xpi = xs_ref[prow, re], xs_ref[prow, im]
                return nr, ni, c[2] + (nr * xpr + ni * xpi), c[3] + (ni * xpr - nr * xpi)

            init = (lam_s[:, re], lam_s[:, im])
            if full:
                init = init + (jnp.zeros((NSEG, w), F32), jnp.zeros((NSEG, w), F32))
            c = lax.fori_loop(0, nj - 1, step, init)
            lr, li = advance(0, c[0], c[1])
            if full:
                first = (blk(i) == 0)
                xpr = jnp.where(first, x0_ref[:, re], xh_ref[:, re])
                xpi = jnp.where(first, x0_ref[:, im], xh_ref[:, im])
                da_ref[:, re] += c[2] + (lr * xpr + li * xpi)
                da_ref[:, im] += c[3] + (li * xpr - lr * xpi)
            lam_s[:, re] = lr
            lam_s[:, im] = li
        if full:
            uv = u_ref[...]
            ub = uv.astype(MXU)
            dd_ref[...] += jnp.sum(dyv * uv, axis=0, keepdims=True)
            for kb in range(SBLK):
                cols = slice(kb * LANE, (kb + 1) * LANE)
                re = slice(kb * SW, (kb + 1) * SW)
                im = slice(NSTATE + kb * SW, NSTATE + (kb + 1) * SW)
                lr_b = g_s[:, re].astype(MXU)
                li_b = g_s[:, im].astype(MXU)
                duk = (jnp.dot(lr_b, wbt_ref[kb, :SW, :], preferred_element_type=F32)
                       + jnp.dot(li_b, wbt_ref[kb, SW:, :], preferred_element_type=F32))
                du_ref[:, cols] = duk + d_ref[:, cols] * dyv[:, cols]
                dwb_ref[kb, :, :SW] += _dot_tn(ub[:, cols], lr_b)
                dwb_ref[kb, :, SW:] += _dot_tn(ub[:, cols], li_b)
                dwc_ref[kb, :SW, :] += _dot_tn(xs_ref[:, re].astype(MXU), dyb[:, cols])
                dwc_ref[kb, SW:, :] += _dot_tn(xs_ref[:, im].astype(MXU), dyb[:, cols])

            @pl.when(i == nb - 1)
            def _():
                pltpu.sync_copy(dwb_ref, dwb_hbm)
                pltpu.sync_copy(dwc_ref, dwc_hbm)
        else:
            @pl.when(i == nb - 1)
            def _():
                f_ref[...] = lam_s[...]

    rev = lambda c: pl.BlockSpec((T, c), lambda i: (blk(i), 0))
    w3 = lambda a, b: pl.BlockSpec((SBLK, a, b), lambda i: (0, 0, 0))
    if full:
        hb = T // NSEG
        in_specs = [rev(D_MODEL), rev(ST2),
                    pl.BlockSpec((NSEG, ST2), lambda i: (jnp.maximum(blk(i) * hb - 1, 0), 0)),
                    rev(D_MODEL), w3(LANE, 2 * SW), w3(2 * SW, LANE), _const_spec(NSEG, ST2),
                    _const_spec(1, D_MODEL), _const_spec(NSEG, ST2), _const_spec(NSEG, ST2)]
        ops = [dy, xs, xs, u, wct, wbt, abc, dskip, x0, l0]
        hbm = pl.BlockSpec(memory_space=pl.ANY)
        out_specs = (rev(D_MODEL), _const_spec(NSEG, ST2), hbm, hbm, _const_spec(1, D_MODEL))
        out_shape = (jax.ShapeDtypeStruct((S, D_MODEL), F32), jax.ShapeDtypeStruct((NSEG, ST2), F32),
                     jax.ShapeDtypeStruct((SBLK, LANE, 2 * SW), F32), jax.ShapeDtypeStruct((SBLK, 2 * SW, LANE), F32),
                     jax.ShapeDtypeStruct((1, D_MODEL), F32))
    else:
        in_specs = [rev(D_MODEL), w3(LANE, 2 * SW), _const_spec(NSEG, ST2), _const_spec(NSEG, ST2)]
        ops = [dy, wct, abc, l0]
        out_specs = _const_spec(NSEG, ST2)
        out_shape = jax.ShapeDtypeStruct((NSEG, ST2), F32)
    return pl.pallas_call(
        body, name=name, grid=(nb,), in_specs=in_specs, out_specs=out_specs, out_shape=out_shape,
        scratch_shapes=[pltpu.VMEM((T, ST2), F32), pltpu.VMEM((NSEG, ST2), F32)]
        + ([pltpu.VMEM((SBLK, LANE, 2 * SW), F32), pltpu.VMEM((SBLK, 2 * SW, LANE), F32)] if full else []),
        compiler_params=_params(big=True),
    )(*ops)


def _s5_discretize(lr, li, log_dt, br, bi):
    dt = jnp.exp(log_dt)[:, None]
    mag = jnp.exp(lr * dt)
    ar = mag * jnp.cos(li * dt)
    ai = mag * jnp.sin(li * dt)
    den = lr * lr + li * li
    nr = ar - 1.0
    coef_r = (nr * lr + ai * li) / den
    coef_i = (ai * lr - nr * li) / den
    bbar_r = coef_r[..., None] * br - coef_i[..., None] * bi
    bbar_i = coef_r[..., None] * bi + coef_i[..., None] * br
    return ar, ai, bbar_r, bbar_i


def _blockdiag(m):
    gpb = SSM_GROUPS // SBLK
    a, b = m.shape[1:]
    mb = m.reshape(SBLK, gpb, a, b)
    eye = jnp.eye(gpb, dtype=m.dtype)
    return jnp.einsum('kgab,gh->kgahb', mb, eye).reshape(SBLK, gpb * a, gpb * b)


def _blockdiag_extract(w, a, b):
    gpb = SSM_GROUPS // SBLK
    w5 = w.reshape(SBLK, gpb, a, gpb, b)
    return jnp.einsum('kgahb,gh->kgab', w5, jnp.eye(gpb, dtype=w.dtype)).reshape(SSM_GROUPS, a, b)


def _cpow(ar, ai, n):
    rr, ri = jnp.ones_like(ar), jnp.zeros_like(ai)
    br, bi = ar, ai
    while n:
        if n & 1:
            rr, ri = rr * br - ri * bi, rr * bi + ri * br
        br, bi = br * br - bi * bi, 2.0 * br * bi
        n >>= 1
    return rr, ri


def _perm(a):
    s, c = a.shape
    return a.reshape(NSEG, s // NSEG, c).transpose(1, 0, 2).reshape(s, c)


def _unperm(a):
    s, c = a.shape
    return a.reshape(s // NSEG, NSEG, c).transpose(1, 0, 2).reshape(s, c)


def _other_chips(x, y):
    return [(1 - x, y), (x, 1 - y), (1 - x, 1 - y)]


def _span(chip, size, align):
    return pl.ds(pl.multiple_of(chip * size, align), size)


def _exchange_ops(pieces, s_refs, o_refs, send_sems, recv_sems, local_sems):
    x, y, c = lax.axis_index("x"), lax.axis_index("y"), lax.axis_index("c")
    me = 2 * x + y
    others = _other_chips(x, y)
    npc = len(pieces)

    def remote(k, p, tx, ty, src_chip, dst_chip):
        si, oi, sv, dv = pieces[p]
        return pltpu.make_async_remote_copy(
            src_ref=sv(s_refs[si], src_chip), dst_ref=dv(o_refs[oi], dst_chip), send_sem=send_sems.at[k, p],
            recv_sem=recv_sems.at[k, p], device_id=(tx, ty, c), device_id_type=MESH)

    def local(p):
        si, oi, sv, dv = pieces[p]
        return pltpu.make_async_copy(sv(s_refs[si], me), dv(o_refs[oi], me), local_sems.at[p])

    def start():
        for p in range(npc):
            local(p).start()
        for k, (tx, ty) in enumerate(others):
            for p in range(npc):
                remote(k, p, tx, ty, 2 * tx + ty, me).start()

    def finish():
        for k, (tx, ty) in enumerate(others):
            for p in range(npc):
                remote(k, p, tx, ty, me, 2 * tx + ty).wait_recv()
        for k, (tx, ty) in enumerate(others):
            for p in range(npc):
                remote(k, p, tx, ty, 2 * tx + ty, me).wait_send()
        for p in range(npc):
            local(p).wait()

    return start, finish


def _exchange_scratch(npc):
    return [pltpu.SemaphoreType.DMA((NCHIP - 1, npc)), pltpu.SemaphoreType.DMA((NCHIP - 1, npc)),
            pltpu.SemaphoreType.DMA((npc,))]


def _chip_exchange(srcs, out_shapes, pieces, *, name):
    ns, no = len(srcs), len(out_shapes)

    def body(*refs):
        start, finish = _exchange_ops(pieces, refs[:ns], refs[ns:ns + no], *refs[ns + no:])
        start()
        finish()

    hbm = pl.BlockSpec(memory_space=pl.ANY)
    return pl.pallas_call(
        body, name=name, in_specs=[hbm] * ns, out_specs=tuple([hbm] * no), out_shape=tuple(out_shapes),
        scratch_shapes=_exchange_scratch(len(pieces)),
    )(*srcs)


def _sibling_exchange(srcs, *, name):
    n = len(srcs)

    def body(*refs):
        s_refs, o_refs, send_sems, recv_sems = refs[:n], refs[n:2 * n], refs[2 * n], refs[2 * n + 1]
        x, y, c = lax.axis_index("x"), lax.axis_index("y"), lax.axis_index("c")
        cps = [pltpu.make_async_remote_copy(src_ref=s_refs[p], dst_ref=o_refs[p], send_sem=send_sems.at[p],
                                            recv_sem=recv_sems.at[p], device_id=(x, y, 1 - c), device_id_type=MESH)
               for p in range(n)]
        for cp in cps:
            cp.start()
        for cp in cps:
            cp.wait()

    hbm = pl.BlockSpec(memory_space=pl.ANY)
    return pl.pallas_call(
        body, name=name, in_specs=[hbm] * n, out_specs=tuple([hbm] * n),
        out_shape=tuple(jax.ShapeDtypeStruct(s.shape, s.dtype) for s in srcs),
        scratch_shapes=[pltpu.SemaphoreType.DMA((n,)), pltpu.SemaphoreType.DMA((n,))],
    )(*srcs)


def _row_tile(r, c, tile_bytes):
    best = 16
    for t in range(16, r + 1, 16):
        if r % t == 0 and t * c * 4 <= tile_bytes:
            best = t
    assert r % best == 0
    return best


def _sum_chips(r, *, name):
    _, R, W = r.shape
    tm = _row_tile(R, W, 2 * 1024 * 1024)

    def body(r_ref, o_ref):
        o_ref[...] = ((r_ref[0].astype(F32) + r_ref[1].astype(F32)) + r_ref[2].astype(F32)) + r_ref[3].astype(F32)

    return pl.pallas_call(
        body, name=name, grid=(R // tm,), in_specs=[pl.BlockSpec((NCHIP, tm, W), lambda i: (0, i, 0))],
        out_specs=_row_spec(tm, W), out_shape=jax.ShapeDtypeStruct((R, W), F32), compiler_params=_params(),
    )(r)


def _adamw(p_mine, p_sib, w, m, v, *, name):
    R, W = w.shape
    tm = _row_tile(R, W, 1024 * 1024)

    def body(a_ref, b_ref, w_ref, m_ref, v_ref, g_ref, d_ref, nm_ref, nv_ref):
        g = a_ref[...] + b_ref[...]
        mm = ADAM_B1 * m_ref[...] + (1.0 - ADAM_B1) * g
        vv = ADAM_B2 * v_ref[...] + (1.0 - ADAM_B2) * (g * g)
        m_hat = mm / (1.0 - ADAM_B1 ** ADAM_STEP)
        v_hat = vv / (1.0 - ADAM_B2 ** ADAM_STEP)
        g_ref[...] = g
        d_ref[...] = -ADAM_LR * (m_hat / (jnp.sqrt(v_hat) + ADAM_EPS) + ADAM_WD * w_ref[...])
        nm_ref[...] = mm
        nv_ref[...] = vv

    spec = _row_spec(tm, W)
    shp = jax.ShapeDtypeStruct((R, W), F32)
    return pl.pallas_call(
        body, name=name, grid=(R // tm,), in_specs=[spec] * 5, out_specs=(spec,) * 4, out_shape=(shp,) * 4,
        compiler_params=_params(),
    )(p_mine, p_sib, w, m, v)


def _pack_small(parts):
    flat = jnp.concatenate([p.reshape(-1) for p in parts])
    pad = (-flat.shape[0]) % (SMALL_ROWS * PACKW)
    return jnp.pad(flat, (0, pad)).reshape(-1, PACKW)


def _unpack_small(buf, shapes):
    flat, out, off = buf.reshape(-1), [], 0
    for shp in shapes:
        sz = math.prod(shp)
        out.append(flat[off:off + sz].reshape(shp))
        off += sz
    return out


def _shard(a, t, ax):
    sz = a.shape[ax] // NCHIP
    return lax.slice_in_dim(a, t * sz, (t + 1) * sz, axis=ax)


MLA_W = ['mla_w_a', 'mla_w_uq', 'mla_w_ukv', 'mla_w_o']
REST_W = ['ssm_w_in', 'ssm_w_glu', 'ffn_w_up', 'ffn_w_down']
SMALL = [n for n in WNAMES if n not in MLA_W + REST_W]


def _gather_plan(w, names, with_small):
    srcs, outs, pieces = [], [], []
    for name in names:
        local = w[name].astype(MXU)
        local = local[0] if local.shape[0] == 1 else local
        ax = SHARD_AXIS[name] - (1 if w[name].shape[0] == 1 else 0)
        full = local.shape[:ax] + (NCHIP * local.shape[ax],) + local.shape[ax + 1:]
        si, oi = len(srcs), len(outs)
        srcs.append(local)
        outs.append(jax.ShapeDtypeStruct(full, MXU))
        size = local.shape[ax]
        if local.ndim == 2:
            if ax == 0:
                pieces.append((si, oi, lambda r, t: r, lambda r, ch, size=size: r.at[_span(ch, size, 8), :]))
            else:
                pieces.append((si, oi, lambda r, t: r, lambda r, ch, size=size: r.at[:, _span(ch, size, LANE)]))
        else:
            for l in range(local.shape[0]):
                if ax == 1:
                    dv = lambda r, ch, l=l, size=size: r.at[l, _span(ch, size, 8), :]
                else:
                    dv = lambda r, ch, l=l, size=size: r.at[l, :, _span(ch, size, LANE)]
                pieces.append((si, oi, lambda r, t, l=l: r.at[l], dv))
    if with_small:
        small = _pack_small([w[n] for n in GATHER_F32])
        srcs.append(small)
        outs.append(jax.ShapeDtypeStruct((NCHIP,) + small.shape, F32))
        pieces.append((len(srcs) - 1, len(outs) - 1, lambda r, t: r, lambda r, ch: r.at[ch]))
    return srcs, outs, pieces


def _gather_result(got, w, names, with_small):
    full = dict(zip(names, got[:len(names)]))
    if with_small:
        per_chip = [_unpack_small(got[-1][t], [w[n].shape for n in GATHER_F32]) for t in range(NCHIP)]
        for j, n in enumerate(GATHER_F32):
            full[n] = jnp.concatenate([per_chip[t][j] for t in range(NCHIP)], axis=SHARD_AXIS[n])
    return full


def _as_rows(a):
    return a.reshape(-1, a.shape[-1])


def _grad_plan(grads, w, names, with_small):
    srcs, outs, pieces = [], [], []
    for name in names:
        local = w[name]
        ax = SHARD_AXIS[name]
        size = local.shape[ax]
        oi = len(outs)
        layers = grads[name] if isinstance(grads[name], list) else [grads[name]]
        outs.append(jax.ShapeDtypeStruct((NCHIP,) + local.shape, layers[0].dtype))
        for l, g in enumerate(layers):
            si = len(srcs)
            srcs.append(g)
            if ax == 1:
                sv = lambda r, t, size=size: r.at[_span(t, size, 8), :]
            else:
                sv = lambda r, t, size=size: r.at[:, _span(t, size, LANE)]
            pieces.append((si, oi, sv, lambda r, ch, l=l: r.at[ch, l]))
    if with_small:
        small = jnp.stack([_pack_small([_shard(grads[n], t, SHARD_AXIS[n]) if n in SHARD_AXIS else grads[n]
                                        for n in SMALL]) for t in range(NCHIP)])
        srcs.append(small)
        outs.append(jax.ShapeDtypeStruct(small.shape, F32))
        pieces.append((len(srcs) - 1, len(outs) - 1, lambda r, t: r.at[t], lambda r, ch: r.at[ch]))
    return srcs, outs, pieces


def _reduce_and_update(names, landed, w, mom, var):
    partial = [_sum_chips(r.reshape(NCHIP, -1, r.shape[-1]), name="grad_sum_chips") for r in landed]
    sibling = _sibling_exchange(partial, name="grad_sibling")
    res = [dict(), dict(), dict(), dict()]
    for j, name in enumerate(names):
        outs4 = _adamw(partial[j], sibling[j], _as_rows(w[name]), _as_rows(mom[name]), _as_rows(var[name]), name="adamw")
        for d, o in zip(res, outs4):
            d[name] = o.reshape(w[name].shape)
    outs4 = _adamw(partial[-1], sibling[-1], *[_pack_small([t[n] for n in SMALL]) for t in (w, mom, var)], name="adamw")
    for d, o in zip(res, outs4):
        d.update(zip(SMALL, _unpack_small(o, [w[n].shape for n in SMALL])))
    return res


def kernel(x, positions, mla_w_a, mla_g_q, mla_g_kv, mla_w_uq, mla_w_ukv, mla_w_o, ssm_w_in, ssm_lambda_re, ssm_lambda_im, ssm_log_dt, ssm_b_re, ssm_b_im, ssm_c_re, ssm_c_im, ssm_d, ssm_w_glu, ffn_w_up, ffn_conv_w, ffn_conv_b, ffn_w_down, g_mix, g_ffn, g_final, loss_target, m_mla_w_a, m_mla_g_q, m_mla_g_kv, m_mla_w_uq, m_mla_w_ukv, m_mla_w_o, m_ssm_w_in, m_ssm_lambda_re, m_ssm_lambda_im, m_ssm_log_dt, m_ssm_b_re, m_ssm_b_im, m_ssm_c_re, m_ssm_c_im, m_ssm_d, m_ssm_w_glu, m_ffn_w_up, m_ffn_conv_w, m_ffn_conv_b, m_ffn_w_down, m_g_mix, m_g_ffn, m_g_final, v_mla_w_a, v_mla_g_q, v_mla_g_kv, v_mla_w_uq, v_mla_w_ukv, v_mla_w_o, v_ssm_w_in, v_ssm_lambda_re, v_ssm_lambda_im, v_ssm_log_dt, v_ssm_b_re, v_ssm_b_im, v_ssm_c_re, v_ssm_c_im, v_ssm_d, v_ssm_w_glu, v_ffn_w_up, v_ffn_conv_w, v_ffn_conv_b, v_ffn_w_down, v_g_mix, v_g_ffn, v_g_final):
    w = dict(zip(WNAMES, (mla_w_a, mla_g_q, mla_g_kv, mla_w_uq, mla_w_ukv, mla_w_o, ssm_w_in, ssm_lambda_re,
                          ssm_lambda_im, ssm_log_dt, ssm_b_re, ssm_b_im, ssm_c_re, ssm_c_im, ssm_d, ssm_w_glu,
                          ffn_w_up, ffn_conv_w, ffn_conv_b, ffn_w_down, g_mix, g_ffn, g_final)))
    mom = dict(zip(WNAMES, (m_mla_w_a, m_mla_g_q, m_mla_g_kv, m_mla_w_uq, m_mla_w_ukv, m_mla_w_o, m_ssm_w_in,
                            m_ssm_lambda_re, m_ssm_lambda_im, m_ssm_log_dt, m_ssm_b_re, m_ssm_b_im, m_ssm_c_re,
                            m_ssm_c_im, m_ssm_d, m_ssm_w_glu, m_ffn_w_up, m_ffn_conv_w, m_ffn_conv_b,
                            m_ffn_w_down, m_g_mix, m_g_ffn, m_g_final)))
    var = dict(zip(WNAMES, (v_mla_w_a, v_mla_g_q, v_mla_g_kv, v_mla_w_uq, v_mla_w_ukv, v_mla_w_o, v_ssm_w_in,
                            v_ssm_lambda_re, v_ssm_lambda_im, v_ssm_log_dt, v_ssm_b_re, v_ssm_b_im, v_ssm_c_re,
                            v_ssm_c_im, v_ssm_d, v_ssm_w_glu, v_ffn_w_up, v_ffn_conv_w, v_ffn_conv_b,
                            v_ffn_w_down, v_g_mix, v_g_ffn, v_g_final)))
    S = x.shape[1]
    D = D_MODEL
    x2 = x.reshape(S, D)
    tgt = loss_target.reshape(S, D)

    fw = _gather_result(_chip_exchange(*_gather_plan(w, MLA_W, False), name="gather_weights"), w, MLA_W, False)
    w_a = jnp.pad(fw['mla_w_a'], ((0, 0), (0, A_PAD - KR0 - QK_ROPE)))
    uq = fw['mla_w_uq'].reshape(Q_LORA, HEADS, QK_DIM)
    w_uq = jnp.concatenate([uq[:, :, :QK_NOPE].reshape(Q_LORA, HEADS * QK_NOPE),
                            jnp.pad(uq[:, :, QK_NOPE:], ((0, 0), (0, 0), (0, LANE - QK_ROPE))).reshape(Q_LORA, HEADS * LANE)],
                           axis=1)
    w_ukv = fw['mla_w_ukv']
    w_o = fw['mla_w_o']
    conv_b = w['ffn_conv_b']
    g_q, g_kv = w['mla_g_q'], w['mla_g_kv']
    gm, gf = w['g_mix'], w['g_ffn']
    gfin = w['g_final'].reshape(1, D)

    inv = 1.0 / (ROPE_THETA ** (jnp.arange(0, QK_ROPE, 2, dtype=F32) / QK_ROPE))
    ang = positions.reshape(S).astype(F32)[:, None] * inv
    cos, sin = jnp.cos(ang), jnp.sin(ang)
    zpad = jnp.zeros((S, LANE - QK_ROPE), F32)
    c128 = jnp.concatenate([cos, cos, zpad], axis=1)
    s128 = jnp.concatenate([-sin, sin, zpad], axis=1)

    hn0 = _rmsnorm_fwd(x2, gm[0:1], name="rms_mix0")
    a = _mm(hn0, w_a, name="mla_a")
    cqn, ckvn, kr = _mla_mid_fwd(a, g_q, g_kv, c128, s128, name="mla_mid_fwd")
    qfull = _mm(cqn, w_uq, name="mla_q")
    kv = _mm(ckvn, w_ukv, out_dtype=MXU, name="mla_kv")
    qs, ks, vs = _qk_prep(qfull, kv, kr, c128, s128, name="qk_prep")
    os_, lses = [], []
    rides = {0: (['ssm_w_in', 'ssm_w_glu', 'ffn_w_down'], True), 1: (['ffn_w_up'], False)}
    for h in range(HEADS):
        if h in rides:
            o_h, lse_h, *got = _attn_fwd(qs[h], ks[h], vs[h], name="attn_fwd_gather", ride=_gather_plan(w, *rides[h]))
            fw.update(_gather_result(got, w, *rides[h]))
        else:
            o_h, lse_h = _attn_fwd(qs[h], ks[h], vs[h], name="attn_fwd")
        os_.append(o_h)
        lses.append(lse_h)
    w_in = fw['ssm_w_in']
    w_glu = fw['ssm_w_glu']
    w_up = fw['ffn_w_up']
    w_down = fw['ffn_w_down']
    conv_w = fw['ffn_conv_w']
    dskip = fw['ssm_d']
    o_cat = jnp.concatenate(os_, axis=1)
    h1 = _mm(o_cat, w_o, res=x2, name="mla_o")

    def ffn_fwd(h, l):
        hn = _rmsnorm_fwd(h, gf[l:l + 1], name="rms_ffn")
        up = _mm(hn, w_up[l], name="ffn_up")
        act, cval, cgate = _convgate_fwd(up, conv_w[l], conv_b[l:l + 1], name="convgate_fwd")
        return _mm(act, w_down[l], res=h, name="ffn_down"), (hn, up, act, cval, cgate)

    h2, saved0 = ffn_fwd(h1, 0)

    lam_re, lam_im, log_dt = w['ssm_lambda_re'][0], w['ssm_lambda_im'][0], w['ssm_log_dt'][0]
    (a_re, a_im, bbar_r, bbar_i), disc_vjp = jax.vjp(_s5_discretize, lam_re, lam_im, log_dt, w['ssm_b_re'][0],
                                                    w['ssm_b_im'][0])
    c_re, c_im = w['ssm_c_re'][0], w['ssm_c_im'][0]
    bt_r, bt_i = jnp.swapaxes(bbar_r, 1, 2), jnp.swapaxes(bbar_i, 1, 2)
    wb = jnp.concatenate([_blockdiag(bt_r), _blockdiag(bt_i)], axis=2).astype(MXU)
    wbt = jnp.concatenate([_blockdiag(bbar_r), _blockdiag(bbar_i)], axis=1).astype(MXU)
    ct_r, ct_i = jnp.swapaxes(c_re, 1, 2), jnp.swapaxes(c_im, 1, 2)
    wc = jnp.concatenate([_blockdiag(ct_r), _blockdiag(-ct_i)], axis=1).astype(MXU)
    wct = jnp.concatenate([_blockdiag(c_re), _blockdiag(-c_im)], axis=2).astype(MXU)
    af_r, af_i = a_re.reshape(NSTATE), a_im.reshape(NSTATE)
    abc = jnp.broadcast_to(jnp.concatenate([af_r, af_i])[None], (NSEG, ST2))
    seg = S // NSEG
    ap_r, ap_i = _cpow(af_r, af_i, seg)

    hn1 = _rmsnorm_fwd(h2, gm[1:2], name="rms_mix1")
    u = _mm(hn1, w_in, name="s5_in")
    u_p = _perm(u)
    zero_state = jnp.zeros((NSEG, ST2), F32)
    ends = _s5_fwd(u_p, wb, wc, abc, dskip, zero_state, full=False, name="s5_fwd_ends")
    inits, cr, ci = [], jnp.zeros((NSTATE,), F32), jnp.zeros((NSTATE,), F32)
    for r in range(NSEG):
        inits.append(jnp.concatenate([cr, ci]))
        er, ei = ends[r, :NSTATE], ends[r, NSTATE:]
        cr, ci = er + ap_r * cr - ap_i * ci, ei + ap_r * ci + ap_i * cr
    x0 = jnp.stack(inits)
    xs, y_p, yg_p = _s5_fwd(u_p, wb, wc, abc, dskip, x0, full=True, name="s5_fwd")
    yg = _unperm(yg_p)
    z = _mm(yg, w_glu, name="s5_glu")
    h3 = _glu_fwd(z, h2, name="glu_fwd")
    h4, saved1 = ffn_fwd(h3, 1)

    loss_l, dh4, dg_final = _loss_head(h4, gfin, tgt, name="loss_head")

    grads = {}

    def ffn_bwd(h_in, g, saved, l):
        hn, up, act, cval, cgate = saved
        dact = _mm(g, w_down[l], mode="nt", name="ffn_down_dx")
        dw_down = _mm(act, g, mode="tn", out_dtype=MXU, name="ffn_down_dw")
        duv, dug, dwv, dwg, dbv, dbg = _convgate_bwd(up, cval, cgate, dact, conv_w[l], name="convgate_bwd")
        dw_up = jnp.concatenate([_mm(hn, duv, mode="tn", out_dtype=MXU, name="ffn_up_dw"), _mm(hn, dug, mode="tn", out_dtype=MXU, name="ffn_up_dw")],
                                axis=1)
        dhn = _mm(duv, w_up[l][:, :D_FF], mode="nt", name="ffn_up_dx")
        dhn = _mm(dug, w_up[l][:, D_FF:], mode="nt", res=dhn, name="ffn_up_dx_acc")
        dh, dg = _rmsnorm_bwd(h_in, gf[l:l + 1], dhn, g, name="rms_ffn_bwd")
        return dh, dict(w_up=dw_up, w_down=dw_down, conv_w=jnp.concatenate([dwv, dwg], axis=1),
                        conv_b=jnp.concatenate([dbv, dbg], axis=1)[0], g_ffn=dg[0])

    dh3, fg1 = ffn_bwd(h3, dh4, saved1, 1)

    dz = _glu_bwd(z, dh3, name="glu_bwd")
    grads['ssm_w_glu'] = _mm(yg, dz, mode="tn", out_dtype=MXU, name="s5_glu_dw")
    dyg = _mm(dz, w_glu, mode="nt", name="s5_glu_dx")
    dy_p = _gelu_bwd(y_p, _perm(dyg), name="gelu_bwd")
    firsts = _s5_bwd(dy_p, None, None, wct, None, abc, None, None, zero_state, full=False, name="s5_bwd_firsts")
    linits, cr, ci = [None] * NSEG, jnp.zeros((NSTATE,), F32), jnp.zeros((NSTATE,), F32)
    for r in reversed(range(NSEG)):
        linits[r] = jnp.concatenate([cr, ci])
        fr, fi = firsts[r, :NSTATE], firsts[r, NSTATE:]
        cr, ci = fr + ap_r * cr + ap_i * ci, fi + ap_r * ci - ap_i * cr
    l0 = jnp.stack(linits)
    du_p, dab, dwb, dwc, dd = _s5_bwd(dy_p, xs, u_p, wct, wbt, abc, dskip, x0, l0, full=True, name="s5_bwd")
    du = _unperm(du_p)
    grads['ssm_w_in'] = _mm(hn1, du, mode="tn", out_dtype=MXU, name="s5_in_dw")
    dhn1 = _mm(du, w_in, mode="nt", name="s5_in_dx")
    dh2, dg_mix1 = _rmsnorm_bwd(h2, gm[1:2], dhn1, dh3, name="rms_mix_bwd")
    da_sum = jnp.sum(dab, axis=0)
    dbt_r = _blockdiag_extract(dwb[:, :, :SW], SSM_GROUP, SSM_STATE)
    dbt_i = _blockdiag_extract(dwb[:, :, SW:], SSM_GROUP, SSM_STATE)
    dlr, dli, dlog_dt, dbr, dbi = disc_vjp((da_sum[:NSTATE].reshape(SSM_GROUPS, SSM_STATE),
                                            da_sum[NSTATE:].reshape(SSM_GROUPS, SSM_STATE),
                                            jnp.swapaxes(dbt_r, 1, 2), jnp.swapaxes(dbt_i, 1, 2)))
    dct_r = _blockdiag_extract(dwc[:, :SW, :], SSM_STATE, SSM_GROUP)
    dct_i = _blockdiag_extract(dwc[:, SW:, :], SSM_STATE, SSM_GROUP)
    grads['ssm_lambda_re'], grads['ssm_lambda_im'], grads['ssm_log_dt'] = dlr[None], dli[None], dlog_dt[None]
    grads['ssm_b_re'], grads['ssm_b_im'] = dbr[None], dbi[None]
    grads['ssm_c_re'] = jnp.swapaxes(dct_r, 1, 2)[None]
    grads['ssm_c_im'] = -jnp.swapaxes(dct_i, 1, 2)[None]
    grads['ssm_d'] = dd

    dh1, fg0 = ffn_bwd(h1, dh2, saved0, 0)
    grads['ffn_w_up'] = [fg0['w_up'], fg1['w_up']]
    grads['ffn_w_down'] = [fg0['w_down'], fg1['w_down']]
    grads['ffn_conv_w'] = jnp.stack([fg0['conv_w'], fg1['conv_w']])
    grads['ffn_conv_b'] = jnp.stack([fg0['conv_b'], fg1['conv_b']])
    grads['g_ffn'] = jnp.stack([fg0['g_ffn'], fg1['g_ffn']])

    do_cat = _mm(dh1, w_o, mode="nt", out_dtype=MXU, name="mla_o_dx")
    grads['mla_w_o'] = _mm(o_cat, dh1, mode="tn", out_dtype=MXU, name="mla_o_dw")
    dqs, dks, dvs = [], [], []
    for h in range(HEADS):
        do_h = do_cat[:, h * V_HEAD:(h + 1) * V_HEAD]
        delta = _attn_delta(do_h, os_[h], name="attn_delta")
        tiles = (S // _rows(S, BWD_T), 1, _rows(S, BWD_T))
        lse2 = (lses[h] * math.log2(math.e)).reshape(tiles)
        if h == 0:
            dq_h, dk_h, dv_h, *landed = _attn_bwd(qs[h], ks[h], vs[h], do_h, lse2, delta.reshape(tiles),
                                                  name="attn_bwd_exchange", ride=_grad_plan(grads, w, REST_W, False))
        else:
            dq_h, dk_h, dv_h = _attn_bwd(qs[h], ks[h], vs[h], do_h, lse2, delta.reshape(tiles), name="attn_bwd")
        dqs.append(dq_h)
        dks.append(dk_h)
        dvs.append(dv_h)
    dqfull, dkv, dkr = _qk_prep_bwd(dqs, dks, dvs, c128, s128, name="qk_prep_bwd")
    dw_uq_p = _mm(cqn, dqfull, mode="tn", out_dtype=MXU, name="mla_q_dw")
    dcqn = _mm(dqfull, w_uq, mode="nt", name="mla_q_dx")
    grads['mla_w_ukv'] = _mm(ckvn, dkv, mode="tn", out_dtype=MXU, name="mla_kv_dw")
    dckvn = _mm(dkv, w_ukv, mode="nt", name="mla_kv_dx")
    da, dgq, dgkv = _mla_mid_bwd(a, dcqn, dckvn, dkr, g_q, g_kv, c128, s128, name="mla_mid_bwd")
    grads['mla_w_a'] = _mm(hn0, da, mode="tn", out_dtype=MXU, name="mla_a_dw")[:, :KR0 + QK_ROPE]
    dhn0 = _mm(da, w_a, mode="nt", name="mla_a_dx")
    dx, dg_mix0 = _rmsnorm_bwd(x2, gm[0:1], dhn0, dh1, name="rms_mix_bwd")
    grads['mla_w_uq'] = jnp.concatenate(
        [dw_uq_p[:, :HEADS * QK_NOPE].reshape(Q_LORA, HEADS, QK_NOPE),
         dw_uq_p[:, HEADS * QK_NOPE:].reshape(Q_LORA, HEADS, LANE)[:, :, :QK_ROPE]], axis=2).reshape(Q_LORA, HEADS * QK_DIM)
    grads['mla_g_q'], grads['mla_g_kv'] = dgq, dgkv
    grads['g_mix'] = jnp.concatenate([dg_mix0, dg_mix1], axis=0)
    grads['g_final'] = dg_final[0]

    landed += _chip_exchange(*_grad_plan(grads, w, MLA_W, True), name="grad_exchange")
    g_out, d_out, m_out, v_out = _reduce_and_update(REST_W + MLA_W, landed, w, mom, var)

    loss = lax.psum(loss_l[0, 0], ("x", "y", "c"))
    return (loss, dx.reshape(1, S, D), *[g_out[n] for n in WNAMES], *[d_out[n] for n in WNAMES],
            *[m_out[n] for n in WNAMES], *[v_out[n] for n in WNAMES])
```

```python
import math

import jax
import jax.numpy as jnp
from jax import lax
from jax.experimental import pallas as pl
from jax.experimental.pallas import tpu as pltpu

F32 = jnp.float32
MXU = jnp.bfloat16

D_MODEL = 1024
CHUNK = 64
EPS = 1e-6
HEADS = 8
QK_NOPE = 128
QK_ROPE = 64
V_HEAD = 128
Q_LORA = 384
KV_LORA = 256
ROPE_THETA = 10000.0
QK_DIM = QK_NOPE + QK_ROPE
SSM_GROUP = 16
SSM_GROUPS = D_MODEL // SSM_GROUP
SSM_STATE = 64
NSTATE = SSM_GROUPS * SSM_STATE
D_FF = 2816
ATT_SCALE = QK_DIM ** -0.5
EXP2_SCALE = ATT_SCALE * math.log2(math.e)
NEG = -1e30
NSEG = 8
SBLK = 8

ADAM_LR = 0.001
ADAM_B1 = 0.9
ADAM_B2 = 0.999
ADAM_EPS = 1e-08
ADAM_WD = 0.01
ADAM_STEP = 10

LANE = 128
TN_MAX_COLS = 2816
NN_MAX_COLS = 1408
NN_MAX_K = 2816
TN_ACC_ELEMS = 1536 * 1024
VMEM_BIG = 56 * 1024 * 1024

WNAMES = ['mla_w_a', 'mla_g_q', 'mla_g_kv', 'mla_w_uq', 'mla_w_ukv', 'mla_w_o', 'ssm_w_in', 'ssm_lambda_re',
          'ssm_lambda_im', 'ssm_log_dt', 'ssm_b_re', 'ssm_b_im', 'ssm_c_re', 'ssm_c_im', 'ssm_d', 'ssm_w_glu',
          'ffn_w_up', 'ffn_conv_w', 'ffn_conv_b', 'ffn_w_down', 'g_mix', 'g_ffn', 'g_final']
FWD_NAMES = ['x', 'positions'] + WNAMES
SHARD_AXIS = {'mla_w_a': 1, 'mla_w_uq': 2, 'mla_w_ukv': 2, 'mla_w_o': 1, 'ssm_w_in': 1, 'ssm_d': 1,
              'ssm_w_glu': 2, 'ffn_w_up': 2, 'ffn_conv_w': 2, 'ffn_w_down': 1}
GATHER_F32 = ['ssm_d', 'ffn_conv_w']
NCHIP = 4
PACKW = 1024
SMALL_ROWS = 64
MESH = pl.DeviceIdType.MESH


def _tile(d, pref):
    t = min(pref, d) // LANE * LANE
    while t >= LANE:
        if d % t == 0:
            return t
        t -= LANE
    return d


def _rows(s, pref):
    t = min(s, pref)
    assert s % t == 0 and t % 8 == 0
    return t


def _params(big=False):
    if big:
        return pltpu.CompilerParams(vmem_limit_bytes=VMEM_BIG)
    return pltpu.CompilerParams(vmem_limit_bytes=40 * 1024 * 1024)


def _mm(a, b, *, name, mode="nn", out_dtype=F32, res=None, tm=None, tn=None, tk=None):
    N = b.shape[0] if mode == "nt" else b.shape[1]
    if mode != "tn":
        M, K = a.shape
        if N > 1024 and N % 512:
            tn = tn or N
            tm = tm or _rows(M, 512)
        narrow = jnp.dtype(a.dtype).itemsize == 2
        tk = tk or _tile(K, NN_MAX_K if narrow else NN_MAX_COLS)
        tm = tm or _rows(M, 2048 if narrow and K <= 1024 else 1024)
    else:
        K, M = a.shape
        tn = tn or _tile(N, TN_MAX_COLS)
        tm = tm or _tile(M, max(LANE, TN_ACC_ELEMS // tn))
    assert b.shape[1 if mode == "nt" else 0] == K
    tn = tn or _tile(N, NN_MAX_COLS if N % 512 else 512)
    tk = tk or _rows(K, 512)
    nk = K // tk
    has_res = res is not None

    def body(a_ref, b_ref, *rest):
        if has_res:
            r_ref, o_ref, acc = rest
        else:
            o_ref, acc = rest
        k = pl.program_id(2)

        @pl.when(k == 0)
        def _():
            acc[...] = jnp.zeros_like(acc)

        av = a_ref[...].astype(MXU)
        bv = b_ref[...].astype(MXU)
        if mode == "nn":
            acc[...] += jnp.dot(av, bv, preferred_element_type=F32)
        elif mode == "nt":
            acc[...] += _dot_nt(av, bv)
        else:
            acc[...] += _dot_tn(av, bv)

        @pl.when(k == nk - 1)
        def _():
            o = acc[...]
            if has_res:
                o = o + r_ref[...]
            o_ref[...] = o.astype(o_ref.dtype)

    if mode == "tn":
        a_spec = pl.BlockSpec((tk, tm), lambda i, j, k: (k, i))
    else:
        a_spec = pl.BlockSpec((tm, tk), lambda i, j, k: (i, k))
    if mode == "nt":
        b_spec = pl.BlockSpec((tn, tk), lambda i, j, k: (j, k))
    else:
        b_spec = pl.BlockSpec((tk, tn), lambda i, j, k: (k, j))
    in_specs = [a_spec, b_spec]
    ops = [a, b]
    if has_res:
        in_specs.append(pl.BlockSpec((tm, tn), lambda i, j, k: (i, j)))
        ops.append(res)
    return pl.pallas_call(
        body, name=name, grid=(M // tm, N // tn, nk), in_specs=in_specs,
        out_specs=pl.BlockSpec((tm, tn), lambda i, j, k: (i, j)),
        out_shape=jax.ShapeDtypeStruct((M, N), out_dtype),
        scratch_shapes=[pltpu.VMEM((tm, tn), F32)], compiler_params=_params(),
    )(*ops)


def _row_spec(tm, c):
    return pl.BlockSpec((tm, c), lambda i: (i, 0))


def _const_spec(r, c):
    return pl.BlockSpec((r, c), lambda i: (0, 0))


def _rms_parts(xv):
    r = lax.rsqrt(jnp.mean(xv * xv, axis=-1, keepdims=True) + EPS)
    return r, xv * r


def _rms_vjp(xv, gv, dyv):
    r, xhat = _rms_parts(xv)
    gy = dyv * gv
    dx = r * (gy - xhat * jnp.mean(gy * xhat, axis=-1, keepdims=True))
    return dx, dyv * xhat


def _rmsnorm_fwd(x, g, *, name):
    S, D = x.shape
    tm = _rows(S, 512)

    def body(x_ref, g_ref, o_ref):
        _, xhat = _rms_parts(x_ref[...])
        o_ref[...] = (xhat * g_ref[...]).astype(o_ref.dtype)

    return pl.pallas_call(
        body, name=name, grid=(S // tm,), in_specs=[_row_spec(tm, D), _const_spec(1, D)],
        out_specs=_row_spec(tm, D), out_shape=jax.ShapeDtypeStruct((S, D), MXU), compiler_params=_params(),
    )(x, g)


def _rmsnorm_bwd(x, g, dy, dres, *, name):
    S, D = x.shape
    tm = _rows(S, 512)

    def body(x_ref, g_ref, dy_ref, dr_ref, dx_ref, dg_ref):
        @pl.when(pl.program_id(0) == 0)
        def _():
            dg_ref[...] = jnp.zeros_like(dg_ref)

        dx, dgp = _rms_vjp(x_ref[...], g_ref[...], dy_ref[...])
        dx_ref[...] = dr_ref[...] + dx
        dg_ref[...] += jnp.sum(dgp, axis=0, keepdims=True)

    return pl.pallas_call(
        body, name=name, grid=(S // tm,),
        in_specs=[_row_spec(tm, D), _const_spec(1, D), _row_spec(tm, D), _row_spec(tm, D)],
        out_specs=(_row_spec(tm, D), _const_spec(1, D)),
        out_shape=(jax.ShapeDtypeStruct((S, D), F32), jax.ShapeDtypeStruct((1, D), F32)),
        compiler_params=_params(),
    )(x, g, dy, dres)


def _loss_head(h, g, tgt, *, name):
    S, D = h.shape
    tm = _rows(S, 512)

    def body(h_ref, g_ref, t_ref, l_ref, dh_ref, dg_ref):
        @pl.when(pl.program_id(0) == 0)
        def _():
            l_ref[...] = jnp.zeros_like(l_ref)
            dg_ref[...] = jnp.zeros_like(dg_ref)

        hv = h_ref[...]
        gv = g_ref[...]
        _, xhat = _rms_parts(hv)
        e = xhat * gv - t_ref[...]
        l_ref[...] += 0.5 * jnp.sum(jnp.mean(e * e, axis=-1, keepdims=True), axis=0, keepdims=True)
        dx, dgp = _rms_vjp(hv, gv, e * (1.0 / D))
        dh_ref[...] = dx
        dg_ref[...] += jnp.sum(dgp, axis=0, keepdims=True)

    return pl.pallas_call(
        body, name=name, grid=(S // tm,),
        in_specs=[_row_spec(tm, D), _const_spec(1, D), _row_spec(tm, D)],
        out_specs=(_const_spec(1, 1), _row_spec(tm, D), _const_spec(1, D)),
        out_shape=(jax.ShapeDtypeStruct((1, 1), F32), jax.ShapeDtypeStruct((S, D), F32),
                   jax.ShapeDtypeStruct((1, D), F32)),
        compiler_params=_params(),
    )(h, g, tgt)


def _swap_halves(g):
    lane = lax.broadcasted_iota(jnp.int32, g.shape, 1)
    return jnp.where(lane < QK_ROPE // 2, pltpu.roll(g, LANE - QK_ROPE // 2, axis=1),
                     pltpu.roll(g, QK_ROPE // 2, axis=1))


def _rope128(g, c128, s128):
    return g * c128 + _swap_halves(g) * s128


def _rope128_vjp(dy, c128, s128):
    lane = lax.broadcasted_iota(jnp.int32, dy.shape, 1)
    return jnp.where(lane < QK_ROPE, dy * c128 + _swap_halves(dy * s128), 0.0)


A_PAD = 768
KR0 = Q_LORA + KV_LORA


def _mla_mid_fwd(a, g_q, g_kv, c128, s128, *, name):
    S = a.shape[0]
    tm = _rows(S, 512)

    def body(a_ref, gq_ref, gkv_ref, c_ref, s_ref, cq_ref, ckv_ref, kr_ref):
        av = a_ref[...]
        _, qh = _rms_parts(av[:, :Q_LORA])
        cq_ref[...] = (qh * gq_ref[...]).astype(cq_ref.dtype)
        _, kh = _rms_parts(av[:, Q_LORA:KR0])
        ckv_ref[...] = (kh * gkv_ref[...]).astype(ckv_ref.dtype)
        kr = _rope128(av[:, KR0:A_PAD], c_ref[...], s_ref[...])
        kr_ref[...] = kr[:, :QK_ROPE].astype(kr_ref.dtype)

    return pl.pallas_call(
        body, name=name, grid=(S // tm,),
        in_specs=[_row_spec(tm, A_PAD), _const_spec(1, Q_LORA), _const_spec(1, KV_LORA), _row_spec(tm, LANE),
                  _row_spec(tm, LANE)],
        out_specs=(_row_spec(tm, Q_LORA), _row_spec(tm, KV_LORA), _row_spec(tm, QK_ROPE)),
        out_shape=(jax.ShapeDtypeStruct((S, Q_LORA), MXU), jax.ShapeDtypeStruct((S, KV_LORA), MXU),
                   jax.ShapeDtypeStruct((S, QK_ROPE), MXU)),
        compiler_params=_params(),
    )(a, g_q, g_kv, c128, s128)


def _mla_mid_bwd(a, dcq, dckv, dkr, g_q, g_kv, c128, s128, *, name):
    S = a.shape[0]
    tm = _rows(S, 512)

    def body(a_ref, dcq_ref, dckv_ref, dkr_ref, gq_ref, gkv_ref, c_ref, s_ref, da_ref, dgq_ref, dgkv_ref):
        @pl.when(pl.program_id(0) == 0)
        def _():
            dgq_ref[...] = jnp.zeros_like(dgq_ref)
            dgkv_ref[...] = jnp.zeros_like(dgkv_ref)

        av = a_ref[...]
        dx, dgp = _rms_vjp(av[:, :Q_LORA], gq_ref[...], dcq_ref[...])
        da_ref[:, :Q_LORA] = dx.astype(da_ref.dtype)
        dgq_ref[...] += jnp.sum(dgp, axis=0, keepdims=True)
        dx, dgp = _rms_vjp(av[:, Q_LORA:KR0], gkv_ref[...], dckv_ref[...])
        da_ref[:, Q_LORA:KR0] = dx.astype(da_ref.dtype)
        dgkv_ref[...] += jnp.sum(dgp, axis=0, keepdims=True)
        da_ref[:, KR0:A_PAD] = _rope128_vjp(dkr_ref[...], c_ref[...], s_ref[...]).astype(da_ref.dtype)

    return pl.pallas_call(
        body, name=name, grid=(S // tm,),
        in_specs=[_row_spec(tm, A_PAD), _row_spec(tm, Q_LORA), _row_spec(tm, KV_LORA), _row_spec(tm, LANE),
                  _const_spec(1, Q_LORA), _const_spec(1, KV_LORA), _row_spec(tm, LANE), _row_spec(tm, LANE)],
        out_specs=(_row_spec(tm, A_PAD), _const_spec(1, Q_LORA), _const_spec(1, KV_LORA)),
        out_shape=(jax.ShapeDtypeStruct((S, A_PAD), MXU), jax.ShapeDtypeStruct((1, Q_LORA), F32),
                   jax.ShapeDtypeStruct((1, KV_LORA), F32)),
        compiler_params=_params(),
    )(a, dcq, dckv, dkr, g_q, g_kv, c128, s128)


QF = 2 * HEADS * LANE
KVF = HEADS * (QK_NOPE + V_HEAD)
VX = 2 * V_HEAD


def _qk_prep(qfull, kv, kr, c128, s128, *, name):
    S = qfull.shape[0]
    tm = _rows(S, 256)

    def body(q_ref, kv_ref, kr_ref, c_ref, s_ref, *outs):
        qo, ko, vo = outs[:HEADS], outs[HEADS:2 * HEADS], outs[2 * HEADS:]
        cv, sv = c_ref[...], s_ref[...]
        krv = kr_ref[...]
        for h in range(HEADS):
            qo[h][:, :QK_NOPE] = q_ref[:, h * LANE:(h + 1) * LANE].astype(MXU)
            g = q_ref[:, (HEADS + h) * LANE:(HEADS + h + 1) * LANE]
            qo[h][:, QK_NOPE:] = _rope128(g, cv, sv)[:, :QK_ROPE].astype(MXU)
            ko[h][:, :QK_NOPE] = kv_ref[:, 2 * h * LANE:(2 * h + 1) * LANE]
            ko[h][:, QK_NOPE:] = krv
            vo[h][:, :V_HEAD] = kv_ref[:, (2 * h + 1) * LANE:(2 * h + 2) * LANE]
            vo[h][:, V_HEAD:] = jnp.ones((tm, VX - V_HEAD), MXU)

    shapes = ([jax.ShapeDtypeStruct((S, QK_DIM), MXU)] * (2 * HEADS)
              + [jax.ShapeDtypeStruct((S, VX), MXU)] * HEADS)
    specs = [_row_spec(tm, QK_DIM)] * (2 * HEADS) + [_row_spec(tm, VX)] * HEADS
    outs = pl.pallas_call(
        body, name=name, grid=(S // tm,),
        in_specs=[_row_spec(tm, QF), _row_spec(tm, KVF), _row_spec(tm, QK_ROPE), _row_spec(tm, LANE),
                  _row_spec(tm, LANE)],
        out_specs=tuple(specs), out_shape=tuple(shapes), compiler_params=_params(),
    )(qfull, kv, kr, c128, s128)
    return outs[:HEADS], outs[HEADS:2 * HEADS], outs[2 * HEADS:]


def _qk_prep_bwd(dqs, dks, dvs, c128, s128, *, name):
    S = dqs[0].shape[0]
    tm = _rows(S, 256)

    def body(*refs):
        dq = refs[:HEADS]
        dk = refs[HEADS:2 * HEADS]
        dv = refs[2 * HEADS:3 * HEADS]
        c_ref, s_ref, dqf_ref, dkv_ref, dkr_ref, tmp = refs[3 * HEADS:]
        cv, sv = c_ref[...], s_ref[...]
        tmp[...] = jnp.zeros_like(tmp)
        dkr_ref[...] = jnp.zeros_like(dkr_ref)
        for h in range(HEADS):
            dqf_ref[:, h * LANE:(h + 1) * LANE] = dq[h][:, :QK_NOPE].astype(MXU)
            tmp[:, :QK_ROPE] = dq[h][:, QK_NOPE:]
            dqf_ref[:, (HEADS + h) * LANE:(HEADS + h + 1) * LANE] = _rope128_vjp(tmp[...], cv, sv).astype(MXU)
            dkv_ref[:, 2 * h * LANE:(2 * h + 1) * LANE] = dk[h][:, :QK_NOPE].astype(MXU)
            dkv_ref[:, (2 * h + 1) * LANE:(2 * h + 2) * LANE] = dv[h][...].astype(MXU)
            dkr_ref[:, :QK_ROPE] += dk[h][:, QK_NOPE:]

    return pl.pallas_call(
        body, name=name, grid=(S // tm,),
        in_specs=[_row_spec(tm, QK_DIM)] * (2 * HEADS) + [_row_spec(tm, V_HEAD)] * HEADS
        + [_row_spec(tm, LANE), _row_spec(tm, LANE)],
        out_specs=(_row_spec(tm, QF), _row_spec(tm, KVF), _row_spec(tm, LANE)),
        out_shape=(jax.ShapeDtypeStruct((S, QF), MXU), jax.ShapeDtypeStruct((S, KVF), MXU),
                   jax.ShapeDtypeStruct((S, LANE), F32)),
        scratch_shapes=[pltpu.VMEM((tm, LANE), F32)], compiler_params=_params(),
    )(*dqs, *dks, *dvs, c128, s128)


def _dot_nt(a, b):
    return lax.dot_general(a, b, (((1,), (1,)), ((), ())), preferred_element_type=F32)


def _dot_tn(a, b):
    return lax.dot_general(a, b, (((0,), (0,)), ((), ())), preferred_element_type=F32)


def _attn_fwd(q, k, vx, *, name, ride=None):
    S = q.shape[0]
    T = _rows(S, 1024)
    n = S // T
    cpt = T // CHUNK

    r_srcs, r_outs, r_pieces = ride or ((), (), ())
    ns, no = len(r_srcs), len(r_outs)

    def body(q_ref, k_ref, v_ref, *rest):
        o_ref, lse_ref = rest[ns:ns + 2]
        s_buf, p_buf, a_buf, m_s, acc_s = rest[ns + 2 + no:ns + 7 + no]
        i = pl.program_id(0)
        if ride:
            start, finish = _exchange_ops(r_pieces, rest[:ns], rest[ns + 2:ns + 2 + no], *rest[ns + 7 + no:])
            pl.when(i == 0)(start)
        qc = lax.broadcasted_iota(jnp.int32, (T, T), 0) // CHUNK
        kc = lax.broadcasted_iota(jnp.int32, (T, T), 1) // CHUNK
        dchunk = kc - qc

        def tile_rows(b):
            return pl.ds(pl.multiple_of(jnp.clip(b, 0, n - 1) * T, T), T)

        def scores(b, slot):
            s = _dot_nt(q_ref[...], k_ref[tile_rows(b), :])
            s_buf[slot] = jnp.where(dchunk <= (i - b) * cpt, s, NEG)

        def softmax(slot):
            s = s_buf[slot]
            m_prev = m_s[...]
            m_new = jnp.maximum(m_prev, jnp.max(s, axis=1, keepdims=True))
            a_buf[slot] = jnp.exp2((m_prev - m_new) * EXP2_SCALE)
            p_buf[slot] = jnp.exp2((s - m_new) * EXP2_SCALE).astype(MXU)
            m_s[...] = m_new

        def pv(b, slot):
            acc_s[...] = a_buf[slot] * acc_s[...] + jnp.dot(p_buf[slot], v_ref[tile_rows(b), :],
                                                              preferred_element_type=F32)

        m_s[...] = jnp.full_like(m_s, NEG)
        acc_s[...] = jnp.zeros_like(acc_s)
        p_buf[1] = jnp.zeros((T, T), MXU)
        a_buf[1] = jnp.ones((T, 1), F32)
        scores(0, 0)

        def pair(u, carry):
            t = 2 * u
            scores(t + 1, 1)
            softmax(0)
            pv(t - 1, 1)
            scores(t + 2, 0)
            softmax(1)
            pv(t, 0)
            return carry

        npairs = (i + 2) // 2
        lax.fori_loop(0, npairs, pair, 0)
        pv(2 * npairs - 1, 1)
        acc = acc_s[...]
        l = acc[:, V_HEAD:V_HEAD + 1]
        o_ref[...] = (acc[:, :V_HEAD] / l).astype(o_ref.dtype)
        lse_ref[...] = m_s[...] * ATT_SCALE + jnp.log(l)
        if ride:
            pl.when(i == n - 1)(finish)

    hbm = pl.BlockSpec(memory_space=pl.ANY)
    return pl.pallas_call(
        body, name=name, grid=(n,),
        in_specs=[pl.BlockSpec((T, QK_DIM), lambda i: (i, 0)), pl.BlockSpec((S, QK_DIM), lambda i: (0, 0)),
                  pl.BlockSpec((S, VX), lambda i: (0, 0))] + [hbm] * ns,
        out_specs=(pl.BlockSpec((T, V_HEAD), lambda i: (i, 0)), pl.BlockSpec((T, 1), lambda i: (i, 0))) + (hbm,) * no,
        out_shape=(jax.ShapeDtypeStruct((S, V_HEAD), MXU), jax.ShapeDtypeStruct((S, 1), F32)) + tuple(r_outs),
        scratch_shapes=[pltpu.VMEM((2, T, T), F32), pltpu.VMEM((2, T, T), MXU), pltpu.VMEM((2, T, 1), F32),
                        pltpu.VMEM((T, 1), F32), pltpu.VMEM((T, VX), F32)]
        + (_exchange_scratch(len(r_pieces)) if ride else []),
        compiler_params=_params(big=True),
    )(q, k, vx, *r_srcs)


def _attn_delta(do, o, *, name):
    S = do.shape[0]
    tm = _rows(S, 1024)

    def body(do_ref, o_ref, d_ref):
        d_ref[...] = jnp.sum(do_ref[...].astype(F32) * o_ref[...].astype(F32), axis=1, keepdims=True)

    return pl.pallas_call(
        body, name=name, grid=(S // tm,), in_specs=[_row_spec(tm, V_HEAD), _row_spec(tm, V_HEAD)],
        out_specs=_row_spec(tm, 1), out_shape=jax.ShapeDtypeStruct((S, 1), F32), compiler_params=_params(),
    )(do, o)


BWD_T = 512


def _attn_bwd(q, k, v, do, lse2, delta, *, name, ride=None):
    S = q.shape[0]
    T = _rows(S, BWD_T)
    n = S // T
    cpt = T // CHUNK

    r_srcs, r_outs, r_pieces = ride or ((), (), ())
    ns, no = len(r_srcs), len(r_outs)

    def body(q_hbm, k_ref, v_ref, do_hbm, lse_ref, dl_ref, *rest):
        dq_hbm, dk_ref, dv_ref = rest[ns:ns + 3]
        q_res, do_res, dq_s, s_buf, dp_buf, p_buf, ds_buf, dk_s, dv_s = rest[ns + 3 + no:ns + 12 + no]
        j = pl.program_id(0)
        if ride:
            start, finish = _exchange_ops(r_pieces, rest[:ns], rest[ns + 3:ns + 3 + no], *rest[ns + 12 + no:])
            pl.when(j == 0)(start)

        @pl.when(j == 0)
        def _():
            pltpu.sync_copy(q_hbm, q_res)
            pltpu.sync_copy(do_hbm, do_res)
            dq_s[...] = jnp.zeros_like(dq_s)

        kc = lax.broadcasted_iota(jnp.int32, (T, T), 0) // CHUNK
        qc = lax.broadcasted_iota(jnp.int32, (T, T), 1) // CHUNK
        dchunk = kc - qc

        def tile(t):
            return jnp.clip(j + t, 0, n - 1)

        def rows(t):
            return pl.ds(pl.multiple_of(tile(t) * T, T), T)

        def scores(t, slot):
            visible_up_to = jnp.where(j + t < n, t * cpt, -2 * cpt)
            s = _dot_nt(k_ref[...], q_res[rows(t), :])
            s_buf[slot] = jnp.where(dchunk <= visible_up_to, s, NEG)
            dp_buf[slot] = _dot_nt(v_ref[...], do_res[rows(t), :])

        def probs(t, slot):
            pt = jnp.exp2(s_buf[slot] * EXP2_SCALE - lse_ref[tile(t)])
            p_buf[slot] = pt.astype(MXU)
            ds_buf[slot] = (pt * (dp_buf[slot] - dl_ref[tile(t)]) * ATT_SCALE).astype(MXU)

        def grads(t, slot):
            r = rows(t)
            dv_s[...] += jnp.dot(p_buf[slot], do_res[r, :], preferred_element_type=F32)
            ds = ds_buf[slot]
            dk_s[...] += jnp.dot(ds, q_res[r, :], preferred_element_type=F32)
            dq_s[r, :] += _dot_tn(ds, k_ref[...])

        dk_s[...] = jnp.zeros_like(dk_s)
        dv_s[...] = jnp.zeros_like(dv_s)
        p_buf[1] = jnp.zeros((T, T), MXU)
        ds_buf[1] = jnp.zeros((T, T), MXU)
        scores(0, 0)

        def pair(u, carry):
            t = 2 * u
            scores(t + 1, 1)
            probs(t, 0)
            grads(t - 1, 1)
            scores(t + 2, 0)
            probs(t + 1, 1)
            grads(t, 0)
            return carry

        npairs = (n - j + 1) // 2
        lax.fori_loop(0, npairs, pair, 0)
        grads(2 * npairs - 1, 1)
        dk_ref[...] = dk_s[...]
        dv_ref[...] = dv_s[...]

        @pl.when(j == n - 1)
        def _():
            pltpu.sync_copy(dq_s, dq_hbm)
            if ride:
                finish()

    hbm = pl.BlockSpec(memory_space=pl.ANY)
    k_map = lambda j: (j, 0)
    whole = pl.BlockSpec((n, 1, T), lambda j: (0, 0, 0))
    return pl.pallas_call(
        body, name=name, grid=(n,),
        in_specs=[hbm, pl.BlockSpec((T, QK_DIM), k_map), pl.BlockSpec((T, V_HEAD), k_map), hbm, whole, whole]
        + [hbm] * ns,
        out_specs=(hbm, pl.BlockSpec((T, QK_DIM), k_map), pl.BlockSpec((T, V_HEAD), k_map)) + (hbm,) * no,
        out_shape=(jax.ShapeDtypeStruct((S, QK_DIM), F32), jax.ShapeDtypeStruct((S, QK_DIM), F32),
                   jax.ShapeDtypeStruct((S, V_HEAD), F32)) + tuple(r_outs),
        scratch_shapes=[pltpu.VMEM((S, QK_DIM), MXU), pltpu.VMEM((S, V_HEAD), MXU), pltpu.VMEM((S, QK_DIM), F32),
                        pltpu.VMEM((2, T, T), F32), pltpu.VMEM((2, T, T), F32), pltpu.VMEM((2, T, T), MXU),
                        pltpu.VMEM((2, T, T), MXU), pltpu.VMEM((T, QK_DIM), F32), pltpu.VMEM((T, V_HEAD), F32)]
        + (_exchange_scratch(len(r_pieces)) if ride else []),
        compiler_params=_params(big=True),
    )(q, k, v, do, lse2, delta, *r_srcs)


HALO = 8
CONV_RH = 64


def _conv_tiles(S):
    tm = _rows(S, 256)
    tc = D_FF // 2
    return tm, tc, D_FF // tc


def _silu_parts(gate):
    sg = jax.nn.sigmoid(gate)
    return sg, gate * sg


def _convgate_fwd(up, cw, cb, *, name):
    S = up.shape[0]
    tm, tc, nc = _conv_tiles(S)
    hb = tm // HALO

    def body(v_ref, g_ref, hv_ref, hg_ref, wv_ref, wg_ref, bv_ref, bg_ref, o_ref, cv_ref, cg_ref):
        keep = (pl.program_id(0) > 0).astype(F32)

        def chunk(cc, carry):
            cols = pl.ds(pl.multiple_of(cc * LANE, LANE), LANE)
            wv, wg, bv, bg = wv_ref[:, cols], wg_ref[:, cols], bv_ref[:, cols], bg_ref[:, cols]
            for r0 in range(0, tm, CONV_RH):
                def conv(t_ref, h_ref, w, b):
                    if r0:
                        span = t_ref[pl.ds(r0 - HALO, CONV_RH + HALO), cols]
                    else:
                        span = jnp.concatenate([h_ref[:, cols] * keep, t_ref[pl.ds(0, CONV_RH), cols]], axis=0)
                    back2, back1 = [pltpu.roll(span, s, axis=0)[HALO:] for s in (2, 1)]
                    return w[0:1] * back2 + w[1:2] * back1 + w[2:3] * span[HALO:] + b
                val = conv(v_ref, hv_ref, wv, bv)
                gate = conv(g_ref, hg_ref, wg, bg)
                cv_ref[pl.ds(r0, CONV_RH), cols] = val.astype(cv_ref.dtype)
                cg_ref[pl.ds(r0, CONV_RH), cols] = gate.astype(cg_ref.dtype)
                o_ref[pl.ds(r0, CONV_RH), cols] = (_silu_parts(gate)[1] * val).astype(o_ref.dtype)
            return carry

        lax.fori_loop(0, tc // LANE, chunk, 0)

    prev = lambda i: jnp.maximum(i * hb - 1, 0)
    return pl.pallas_call(
        body, name=name, grid=(S // tm, nc),
        in_specs=[pl.BlockSpec((tm, tc), lambda i, j: (i, j)), pl.BlockSpec((tm, tc), lambda i, j: (i, j + nc)),
                  pl.BlockSpec((HALO, tc), lambda i, j: (prev(i), j)),
                  pl.BlockSpec((HALO, tc), lambda i, j: (prev(i), j + nc)),
                  pl.BlockSpec((3, tc), lambda i, j: (0, j)), pl.BlockSpec((3, tc), lambda i, j: (0, j + nc)),
                  pl.BlockSpec((1, tc), lambda i, j: (0, j)), pl.BlockSpec((1, tc), lambda i, j: (0, j + nc))],
        out_specs=(pl.BlockSpec((tm, tc), lambda i, j: (i, j)),) * 3,
        out_shape=(jax.ShapeDtypeStruct((S, D_FF), MXU),) * 3,
        compiler_params=_params(),
    )(up, up, up, up, cw, cw, cb, cb)


def _convgate_bwd(up, cval, cgate, dact, cw, *, name):
    S = up.shape[0]
    tm, tc, nc = _conv_tiles(S)
    hb = tm // HALO
    nr = S // tm
    R = tm + HALO

    def body(uv_ref, ug_ref, cv_ref, cg_ref, ncv_ref, ncg_ref, da_ref, dan_ref, wv_ref, wg_ref,
             duv_ref, dug_ref, dwv_ref, dwg_ref, dbv_ref, dbg_ref, dsv, dsg):
        i = pl.program_id(1)

        @pl.when(i == 0)
        def _():
            for r in (dwv_ref, dwg_ref, dbv_ref, dbg_ref):
                r[...] = jnp.zeros_like(r)

        keep_next = (i < nr - 1).astype(F32)

        def chunk(cc, carry):
            cols = pl.ds(pl.multiple_of(cc * LANE, LANE), LANE)

            def d_conv(rows, val, gate, d):
                sg, silu = _silu_parts(gate)
                dsv[rows, cols] = d * silu
                dsg[rows, cols] = d * val * (sg * (1.0 + gate * (1.0 - sg)))

            for r0 in range(0, tm, CONV_RH):
                rows = pl.ds(r0, CONV_RH)
                d_conv(rows, cv_ref[rows, cols].astype(F32), cg_ref[rows, cols].astype(F32), da_ref[rows, cols])
            d_conv(pl.ds(tm, HALO), ncv_ref[0:HALO, cols].astype(F32), ncg_ref[0:HALO, cols].astype(F32),
                   dan_ref[:, cols] * keep_next)
            for ds, u_ref, w_ref, du_ref, dw_ref, db_ref in ((dsv, uv_ref, wv_ref, duv_ref, dwv_ref, dbv_ref),
                                                            (dsg, ug_ref, wg_ref, dug_ref, dwg_ref, dbg_ref)):
                w = w_ref[:, cols]
                acc = [jnp.zeros((1, LANE), F32) for _ in range(4)]
                for r0 in range(0, tm, CONV_RH):
                    rows = pl.ds(r0, CONV_RH)
                    span = ds[pl.ds(r0, CONV_RH + HALO), cols]
                    d = [span[:CONV_RH]] + [pltpu.roll(span, CONV_RH + HALO - s, axis=0)[:CONV_RH] for s in (1, 2)]
                    du_ref[rows, cols] = (w[2:3] * d[0] + w[1:2] * d[1] + w[0:1] * d[2]).astype(du_ref.dtype)
                    u = u_ref[rows, cols]
                    for kk in range(3):
                        acc[kk] = acc[kk] + jnp.sum(d[2 - kk] * u, axis=0, keepdims=True)
                    acc[3] = acc[3] + jnp.sum(d[0], axis=0, keepdims=True)
                for kk in range(3):
                    dw_ref[kk:kk + 1, cols] += acc[kk]
                db_ref[:, cols] += acc[3]
            return carry

        lax.fori_loop(0, tc // LANE, chunk, 0)

    nxt = lambda i: jnp.minimum((i + 1) * hb, S // HALO - 1)
    tile_v = pl.BlockSpec((tm, tc), lambda j, i: (i, j))
    tile_g = pl.BlockSpec((tm, tc), lambda j, i: (i, j + nc))
    halo = pl.BlockSpec((HALO, tc), lambda j, i: (nxt(i), j))
    halo16 = pl.BlockSpec((2 * HALO, tc), lambda j, i: (jnp.minimum((i + 1) * (hb // 2), S // (2 * HALO) - 1), j))
    w_v = pl.BlockSpec((3, tc), lambda j, i: (0, j))
    w_g = pl.BlockSpec((3, tc), lambda j, i: (0, j + nc))
    b_v = pl.BlockSpec((1, tc), lambda j, i: (0, j))
    return pl.pallas_call(
        body, name=name, grid=(nc, nr),
        in_specs=[tile_v, tile_g, tile_v, tile_v, halo16, halo16, tile_v, halo, w_v, w_g],
        out_specs=(tile_v, tile_v, w_v, w_v, b_v, b_v),
        out_shape=(jax.ShapeDtypeStruct((S, D_FF), MXU), jax.ShapeDtypeStruct((S, D_FF), MXU),
                   jax.ShapeDtypeStruct((3, D_FF), F32), jax.ShapeDtypeStruct((3, D_FF), F32),
                   jax.ShapeDtypeStruct((1, D_FF), F32), jax.ShapeDtypeStruct((1, D_FF), F32)),
        scratch_shapes=[pltpu.VMEM((R, tc), F32), pltpu.VMEM((R, tc), F32)],
        compiler_params=_params(),
    )(up, up, cval, cgate, cval, cgate, dact, dact, cw, cw)


def _glu_fwd(z, h, *, name):
    S = z.shape[0]
    tm = _rows(S, 512)

    def body(z_ref, h_ref, o_ref):
        o_ref[...] = h_ref[...] + z_ref[:, :D_MODEL] * jax.nn.sigmoid(z_ref[:, D_MODEL:])

    return pl.pallas_call(
        body, name=name, grid=(S // tm,), in_specs=[_row_spec(tm, 2 * D_MODEL), _row_spec(tm, D_MODEL)],
        out_specs=_row_spec(tm, D_MODEL), out_shape=jax.ShapeDtypeStruct((S, D_MODEL), F32),
        compiler_params=_params(),
    )(z, h)


def _glu_bwd(z, dm, *, name):
    S = z.shape[0]
    tm = _rows(S, 512)

    def body(z_ref, dm_ref, o_ref):
        sg = jax.nn.sigmoid(z_ref[:, D_MODEL:])
        dmv = dm_ref[...]
        o_ref[:, :D_MODEL] = (dmv * sg).astype(o_ref.dtype)
        o_ref[:, D_MODEL:] = (dmv * z_ref[:, :D_MODEL] * sg * (1.0 - sg)).astype(o_ref.dtype)

    return pl.pallas_call(
        body, name=name, grid=(S // tm,), in_specs=[_row_spec(tm, 2 * D_MODEL), _row_spec(tm, D_MODEL)],
        out_specs=_row_spec(tm, 2 * D_MODEL), out_shape=jax.ShapeDtypeStruct((S, 2 * D_MODEL), MXU),
        compiler_params=_params(),
    )(z, dm)


GELU_C = math.sqrt(2.0 / math.pi)
GELU_A = 0.044715


def _gelu(y):
    return 0.5 * y * (1.0 + jnp.tanh(GELU_C * (y + GELU_A * (y * y * y))))


def _gelu_bwd(y, dg, *, name):
    S = y.shape[0]
    tm = _rows(S, 512)

    def body(y_ref, dg_ref, o_ref):
        yv = y_ref[...]
        t = jnp.tanh(GELU_C * (yv + GELU_A * (yv * yv * yv)))
        d = 0.5 * (1.0 + t) + 0.5 * yv * (1.0 - t * t) * (GELU_C * (1.0 + 3.0 * GELU_A * (yv * yv)))
        o_ref[...] = dg_ref[...] * d

    return pl.pallas_call(
        body, name=name, grid=(S // tm,), in_specs=[_row_spec(tm, D_MODEL), _row_spec(tm, D_MODEL)],
        out_specs=_row_spec(tm, D_MODEL), out_shape=jax.ShapeDtypeStruct((S, D_MODEL), F32),
        compiler_params=_params(),
    )(y, dg)


FWD_STRIPS = 4
BWD_STRIPS = 8
SW = NSTATE // SBLK
ST2 = 2 * NSTATE


def _s5_fwd(u, wb, wc, abc, dskip, x0, *, full, name):
    S = u.shape[0]
    T = _rows(S, 256 if full else 512)
    nb = S // T
    nj = T // NSEG

    def body(u_ref, wb_ref, wc_ref, a_ref, d_ref, x0_ref, *rest):
        if full:
            xs_ref, y_ref, yg_ref, st = rest
        else:
            e_ref, xs_ref, st = rest
        i = pl.program_id(0)

        @pl.when(i == 0)
        def _():
            st[...] = x0_ref[...]

        uv = u_ref[...]
        ub = uv.astype(MXU)
        for kb in range(SBLK):
            r = jnp.dot(ub[:, kb * LANE:(kb + 1) * LANE], wb_ref[kb], preferred_element_type=F32)
            xs_ref[:, kb * SW:(kb + 1) * SW] = r[:, :SW]
            xs_ref[:, NSTATE + kb * SW:NSTATE + (kb + 1) * SW] = r[:, SW:]
        for sp in range(FWD_STRIPS):
            w = NSTATE // FWD_STRIPS
            re, im = pl.ds(sp * w, w), pl.ds(NSTATE + sp * w, w)
            ar, ai = a_ref[:, re], a_ref[:, im]

            def step(j, c):
                xr, xi = c
                rows = pl.ds(pl.multiple_of(j * NSEG, NSEG), NSEG)
                nr = ar * xr - ai * xi + xs_ref[rows, re]
                ni = ar * xi + ai * xr + xs_ref[rows, im]
                xs_ref[rows, re] = nr
                xs_ref[rows, im] = ni
                return nr, ni

            xr, xi = lax.fori_loop(0, nj, step, (st[:, re], st[:, im]))
            st[:, re] = xr
            st[:, im] = xi
        if full:
            for kb in range(SBLK):
                yk = (jnp.dot(xs_ref[:, kb * SW:(kb + 1) * SW].astype(MXU), wc_ref[kb, :SW, :], preferred_element_type=F32)
                      + jnp.dot(xs_ref[:, NSTATE + kb * SW:NSTATE + (kb + 1) * SW].astype(MXU), wc_ref[kb, SW:, :],
                                preferred_element_type=F32))
                cols = slice(kb * LANE, (kb + 1) * LANE)
                yk = yk + d_ref[:, cols] * uv[:, cols]
                y_ref[:, cols] = yk
                yg_ref[:, cols] = _gelu(yk).astype(yg_ref.dtype)
        else:
            @pl.when(i == nb - 1)
            def _():
                e_ref[...] = st[...]

    in_specs = [_row_spec(T, D_MODEL), pl.BlockSpec((SBLK, LANE, 2 * SW), lambda i: (0, 0, 0)),
                pl.BlockSpec((SBLK, 2 * SW, LANE), lambda i: (0, 0, 0)), _const_spec(NSEG, ST2),
                _const_spec(1, D_MODEL), _const_spec(NSEG, ST2)]
    if full:
        out_specs = (_row_spec(T, ST2), _row_spec(T, D_MODEL), _row_spec(T, D_MODEL))
        out_shape = (jax.ShapeDtypeStruct((S, ST2), F32), jax.ShapeDtypeStruct((S, D_MODEL), F32),
                     jax.ShapeDtypeStruct((S, D_MODEL), MXU))
        scratch = [pltpu.VMEM((NSEG, ST2), F32)]
    else:
        out_specs = _const_spec(NSEG, ST2)
        out_shape = jax.ShapeDtypeStruct((NSEG, ST2), F32)
        scratch = [pltpu.VMEM((T, ST2), F32), pltpu.VMEM((NSEG, ST2), F32)]
    return pl.pallas_call(
        body, name=name, grid=(nb,), in_specs=in_specs, out_specs=out_specs, out_shape=out_shape,
        scratch_shapes=scratch, compiler_params=_params(big=True),
    )(u, wb, wc, abc, dskip, x0)


def _s5_bwd(dy, xs, u, wct, wbt, abc, dskip, x0, l0, *, full, name):
    S = dy.shape[0]
    T = _rows(S, 256 if full else 512)
    nb = S // T
    nj = T // NSEG
    blk = lambda i: nb - 1 - i

    def body(dy_ref, *rest):
        if full:
            (xs_ref, xh_ref, u_ref, wct_ref, wbt_ref, a_ref, d_ref, x0_ref, l0_ref,
             du_ref, da_ref, dwb_hbm, dwc_hbm, dd_ref, g_s, lam_s, dwb_ref, dwc_ref) = rest
        else:
            wct_ref, a_ref, l0_ref, f_ref, g_s, lam_s = rest
        i = pl.program_id(0)

        @pl.when(i == 0)
        def _():
            lam_s[...] = l0_ref[...]
            if full:
                for r in (da_ref, dwb_ref, dwc_ref, dd_ref):
                    r[...] = jnp.zeros_like(r)

        dyv = dy_ref[...]
        dyb = dyv.astype(MXU)
        for kb in range(SBLK):
            r = jnp.dot(dyb[:, kb * LANE:(kb + 1) * LANE], wct_ref[kb], preferred_element_type=F32)
            g_s[:, kb * SW:(kb + 1) * SW] = r[:, :SW]
            g_s[:, NSTATE + kb * SW:NSTATE + (kb + 1) * SW] = r[:, SW:]
        for sp in range(BWD_STRIPS):
            w = NSTATE // BWD_STRIPS
            re, im = pl.ds(sp * w, w), pl.ds(NSTATE + sp * w, w)
            ar, ai = a_ref[:, re], a_ref[:, im]

            def advance(row, lr, li):
                rows = pl.ds(row, NSEG)
                nr = g_s[rows, re] + ar * lr + ai * li
                ni = g_s[rows, im] - ai * lr + ar * li
                g_s[rows, re] = nr
                g_s[rows, im] = ni
                return nr, ni

            def step(jj, c):
                row = pl.multiple_of((nj - 1 - jj) * NSEG, NSEG)
                nr, ni = advance(row, c[0], c[1])
                if not full:
                    return nr, ni
                prow = pl.ds(pl.multiple_of(row - NSEG, NSEG), NSEG)
                xpr, xpi = xs_ref[prow, re], xs_ref[prow, im]
                return nr, ni, c[2] + (nr * xpr + ni * xpi), c[3] + (ni * xpr - nr * xpi)

            init = (lam_s[:, re], lam_s[:, im])
            if full:
                init = init + (jnp.zeros((NSEG, w), F32), jnp.zeros((NSEG, w), F32))
            c = lax.fori_loop(0, nj - 1, step, init)
            lr, li = advance(0, c[0], c[1])
            if full:
                first = (blk(i) == 0)
                xpr = jnp.where(first, x0_ref[:, re], xh_ref[:, re])
                xpi = jnp.where(first, x0_ref[:, im], xh_ref[:, im])
                da_ref[:, re] += c[2] + (lr * xpr + li * xpi)
                da_ref[:, im] += c[3] + (li * xpr - lr * xpi)
            lam_s[:, re] = lr
            lam_s[:, im] = li
        if full:
            uv = u_ref[...]
            ub = uv.astype(MXU)
            dd_ref[...] += jnp.sum(dyv * uv, axis=0, keepdims=True)
            for kb in range(SBLK):
                cols = slice(kb * LANE, (kb + 1) * LANE)
                re = slice(kb * SW, (kb + 1) * SW)
                im = slice(NSTATE + kb * SW, NSTATE + (kb + 1) * SW)
                lr_b = g_s[:, re].astype(MXU)
                li_b = g_s[:, im].astype(MXU)
                duk = (jnp.dot(lr_b, wbt_ref[kb, :SW, :], preferred_element_type=F32)
                       + jnp.dot(li_b, wbt_ref[kb, SW:, :], preferred_element_type=F32))
                du_ref[:, cols] = duk + d_ref[:, cols] * dyv[:, cols]
                dwb_ref[kb, :, :SW] += _dot_tn(ub[:, cols], lr_b)
                dwb_ref[kb, :, SW:] += _dot_tn(ub[:, cols], li_b)
                dwc_ref[kb, :SW, :] += _dot_tn(xs_ref[:, re].astype(MXU), dyb[:, cols])
                dwc_ref[kb, SW:, :] += _dot_tn(xs_ref[:, im].astype(MXU), dyb[:, cols])

            @pl.when(i == nb - 1)
            def _():
                pltpu.sync_copy(dwb_ref, dwb_hbm)
                pltpu.sync_copy(dwc_ref, dwc_hbm)
        else:
            @pl.when(i == nb - 1)
            def _():
                f_ref[...] = lam_s[...]

    rev = lambda c: pl.BlockSpec((T, c), lambda i: (blk(i), 0))
    w3 = lambda a, b: pl.BlockSpec((SBLK, a, b), lambda i: (0, 0, 0))
    if full:
        hb = T // NSEG
        in_specs = [rev(D_MODEL), rev(ST2),
                    pl.BlockSpec((NSEG, ST2), lambda i: (jnp.maximum(blk(i) * hb - 1, 0), 0)),
                    rev(D_MODEL), w3(LANE, 2 * SW), w3(2 * SW, LANE), _const_spec(NSEG, ST2),
                    _const_spec(1, D_MODEL), _const_spec(NSEG, ST2), _const_spec(NSEG, ST2)]
        ops = [dy, xs, xs, u, wct, wbt, abc, dskip, x0, l0]
        hbm = pl.BlockSpec(memory_space=pl.ANY)
        out_specs = (rev(D_MODEL), _const_spec(NSEG, ST2), hbm, hbm, _const_spec(1, D_MODEL))
        out_shape = (jax.ShapeDtypeStruct((S, D_MODEL), F32), jax.ShapeDtypeStruct((NSEG, ST2), F32),
                     jax.ShapeDtypeStruct((SBLK, LANE, 2 * SW), F32), jax.ShapeDtypeStruct((SBLK, 2 * SW, LANE), F32),
                     jax.ShapeDtypeStruct((1, D_MODEL), F32))
    else:
        in_specs = [rev(D_MODEL), w3(LANE, 2 * SW), _const_spec(NSEG, ST2), _const_spec(NSEG, ST2)]
        ops = [dy, wct, abc, l0]
        out_specs = _const_spec(NSEG, ST2)
        out_shape = jax.ShapeDtypeStruct((NSEG, ST2), F32)
    return pl.pallas_call(
        body, name=name, grid=(nb,), in_specs=in_specs, out_specs=out_specs, out_shape=out_shape,
        scratch_shapes=[pltpu.VMEM((T, ST2), F32), pltpu.VMEM((NSEG, ST2), F32)]
        + ([pltpu.VMEM((SBLK, LANE, 2 * SW), F32), pltpu.VMEM((SBLK, 2 * SW, LANE), F32)] if full else []),
        compiler_params=_params(big=True),
    )(*ops)


def _s5_discretize(lr, li, log_dt, br, bi):
    dt = jnp.exp(log_dt)[:, None]
    mag = jnp.exp(lr * dt)
    ar = mag * jnp.cos(li * dt)
    ai = mag * jnp.sin(li * dt)
    den = lr * lr + li * li
    nr = ar - 1.0
    coef_r = (nr * lr + ai * li) / den
    coef_i = (ai * lr - nr * li) / den
    bbar_r = coef_r[..., None] * br - coef_i[..., None] * bi
    bbar_i = coef_r[..., None] * bi + coef_i[..., None] * br
    return ar, ai, bbar_r, bbar_i


def _blockdiag(m):
    gpb = SSM_GROUPS // SBLK
    a, b = m.shape[1:]
    mb = m.reshape(SBLK, gpb, a, b)
    eye = jnp.eye(gpb, dtype=m.dtype)
    return jnp.einsum('kgab,gh->kgahb', mb, eye).reshape(SBLK, gpb * a, gpb * b)


def _blockdiag_extract(w, a, b):
    gpb = SSM_GROUPS // SBLK
    w5 = w.reshape(SBLK, gpb, a, gpb, b)
    return jnp.einsum('kgahb,gh->kgab', w5, jnp.eye(gpb, dtype=w.dtype)).reshape(SSM_GROUPS, a, b)


def _cpow(ar, ai, n):
    rr, ri = jnp.ones_like(ar), jnp.zeros_like(ai)
    br, bi = ar, ai
    while n:
        if n & 1:
            rr, ri = rr * br - ri * bi, rr * bi + ri * br
        br, bi = br * br - bi * bi, 2.0 * br * bi
        n >>= 1
    return rr, ri


def _perm(a):
    s, c = a.shape
    return a.reshape(NSEG, s // NSEG, c).transpose(1, 0, 2).reshape(s, c)


def _unperm(a):
    s, c = a.shape
    return a.reshape(s // NSEG, NSEG, c).transpose(1, 0, 2).reshape(s, c)


def _other_chips(x, y):
    return [(1 - x, y), (x, 1 - y), (1 - x, 1 - y)]


def _span(chip, size, align):
    return pl.ds(pl.multiple_of(chip * size, align), size)


def _exchange_ops(pieces, s_refs, o_refs, send_sems, recv_sems, local_sems):
    x, y, c = lax.axis_index("x"), lax.axis_index("y"), lax.axis_index("c")
    me = 2 * x + y
    others = _other_chips(x, y)
    npc = len(pieces)

    def remote(k, p, tx, ty, src_chip, dst_chip):
        si, oi, sv, dv = pieces[p]
        return pltpu.make_async_remote_copy(
            src_ref=sv(s_refs[si], src_chip), dst_ref=dv(o_refs[oi], dst_chip), send_sem=send_sems.at[k, p],
            recv_sem=recv_sems.at[k, p], device_id=(tx, ty, c), device_id_type=MESH)

    def local(p):
        si, oi, sv, dv = pieces[p]
        return pltpu.make_async_copy(sv(s_refs[si], me), dv(o_refs[oi], me), local_sems.at[p])

    def start():
        for p in range(npc):
            local(p).start()
        for k, (tx, ty) in enumerate(others):
            for p in range(npc):
                remote(k, p, tx, ty, 2 * tx + ty, me).start()

    def finish():
        for k, (tx, ty) in enumerate(others):
            for p in range(npc):
                remote(k, p, tx, ty, me, 2 * tx + ty).wait_recv()
        for k, (tx, ty) in enumerate(others):
            for p in range(npc):
                remote(k, p, tx, ty, 2 * tx + ty, me).wait_send()
        for p in range(npc):
            local(p).wait()

    return start, finish


def _exchange_scratch(npc):
    return [pltpu.SemaphoreType.DMA((NCHIP - 1, npc)), pltpu.SemaphoreType.DMA((NCHIP - 1, npc)),
            pltpu.SemaphoreType.DMA((npc,))]


def _chip_exchange(srcs, out_shapes, pieces, *, name):
    ns, no = len(srcs), len(out_shapes)

    def body(*refs):
        start, finish = _exchange_ops(pieces, refs[:ns], refs[ns:ns + no], *refs[ns + no:])
        start()
        finish()

    hbm = pl.BlockSpec(memory_space=pl.ANY)
    return pl.pallas_call(
        body, name=name, in_specs=[hbm] * ns, out_specs=tuple([hbm] * no), out_shape=tuple(out_shapes),
        scratch_shapes=_exchange_scratch(len(pieces)),
    )(*srcs)


def _sibling_exchange(srcs, *, name):
    n = len(srcs)

    def body(*refs):
        s_refs, o_refs, send_sems, recv_sems = refs[:n], refs[n:2 * n], refs[2 * n], refs[2 * n + 1]
        x, y, c = lax.axis_index("x"), lax.axis_index("y"), lax.axis_index("c")
        cps = [pltpu.make_async_remote_copy(src_ref=s_refs[p], dst_ref=o_refs[p], send_sem=send_sems.at[p],
                                            recv_sem=recv_sems.at[p], device_id=(x, y, 1 - c), device_id_type=MESH)
               for p in range(n)]
        for cp in cps:
            cp.start()
        for cp in cps:
            cp.wait()

    hbm = pl.BlockSpec(memory_space=pl.ANY)
    return pl.pallas_call(
        body, name=name, in_specs=[hbm] * n, out_specs=tuple([hbm] * n),
        out_shape=tuple(jax.ShapeDtypeStruct(s.shape, s.dtype) for s in srcs),
        scratch_shapes=[pltpu.SemaphoreType.DMA((n,)), pltpu.SemaphoreType.DMA((n,))],
    )(*srcs)


def _row_tile(r, c, tile_bytes):
    best = 16
    for t in range(16, r + 1, 16):
        if r % t == 0 and t * c * 4 <= tile_bytes:
            best = t
    assert r % best == 0
    return best


def _sum_chips(r, *, name):
    _, R, W = r.shape
    tm = _row_tile(R, W, 2 * 1024 * 1024)

    def body(r_ref, o_ref):
        o_ref[...] = ((r_ref[0].astype(F32) + r_ref[1].astype(F32)) + r_ref[2].astype(F32)) + r_ref[3].astype(F32)

    return pl.pallas_call(
        body, name=name, grid=(R // tm,), in_specs=[pl.BlockSpec((NCHIP, tm, W), lambda i: (0, i, 0))],
        out_specs=_row_spec(tm, W), out_shape=jax.ShapeDtypeStruct((R, W), F32), compiler_params=_params(),
    )(r)


def _adamw(p_mine, p_sib, w, m, v, *, name):
    R, W = w.shape
    tm = _row_tile(R, W, 1024 * 1024)

    def body(a_ref, b_ref, w_ref, m_ref, v_ref, g_ref, d_ref, nm_ref, nv_ref):
        g = a_ref[...] + b_ref[...]
        mm = ADAM_B1 * m_ref[...] + (1.0 - ADAM_B1) * g
        vv = ADAM_B2 * v_ref[...] + (1.0 - ADAM_B2) * (g * g)
        m_hat = mm / (1.0 - ADAM_B1 ** ADAM_STEP)
        v_hat = vv / (1.0 - ADAM_B2 ** ADAM_STEP)
        g_ref[...] = g
        d_ref[...] = -ADAM_LR * (m_hat / (jnp.sqrt(v_hat) + ADAM_EPS) + ADAM_WD * w_ref[...])
        nm_ref[...] = mm
        nv_ref[...] = vv

    spec = _row_spec(tm, W)
    shp = jax.ShapeDtypeStruct((R, W), F32)
    return pl.pallas_call(
        body, name=name, grid=(R // tm,), in_specs=[spec] * 5, out_specs=(spec,) * 4, out_shape=(shp,) * 4,
        compiler_params=_params(),
    )(p_mine, p_sib, w, m, v)


def _pack_small(parts):
    flat = jnp.concatenate([p.reshape(-1) for p in parts])
    pad = (-flat.shape[0]) % (SMALL_ROWS * PACKW)
    return jnp.pad(flat, (0, pad)).reshape(-1, PACKW)


def _unpack_small(buf, shapes):
    flat, out, off = buf.reshape(-1), [], 0
    for shp in shapes:
        sz = math.prod(shp)
        out.append(flat[off:off + sz].reshape(shp))
        off += sz
    return out


def _shard(a, t, ax):
    sz = a.shape[ax] // NCHIP
    return lax.slice_in_dim(a, t * sz, (t + 1) * sz, axis=ax)


MLA_W = ['mla_w_a', 'mla_w_uq', 'mla_w_ukv', 'mla_w_o']
REST_W = ['ssm_w_in', 'ssm_w_glu', 'ffn_w_up', 'ffn_w_down']
SMALL = [n for n in WNAMES if n not in MLA_W + REST_W]


def _gather_plan(w, names, with_small):
    srcs, outs, pieces = [], [], []
    for name in names:
        local = w[name].astype(MXU)
        local = local[0] if local.shape[0] == 1 else local
        ax = SHARD_AXIS[name] - (1 if w[name].shape[0] == 1 else 0)
        full = local.shape[:ax] + (NCHIP * local.shape[ax],) + local.shape[ax + 1:]
        si, oi = len(srcs), len(outs)
        srcs.append(local)
        outs.append(jax.ShapeDtypeStruct(full, MXU))
        size = local.shape[ax]
        if local.ndim == 2:
            if ax == 0:
                pieces.append((si, oi, lambda r, t: r, lambda r, ch, size=size: r.at[_span(ch, size, 8), :]))
            else:
                pieces.append((si, oi, lambda r, t: r, lambda r, ch, size=size: r.at[:, _span(ch, size, LANE)]))
        else:
            for l in range(local.shape[0]):
                if ax == 1:
                    dv = lambda r, ch, l=l, size=size: r.at[l, _span(ch, size, 8), :]
                else:
                    dv = lambda r, ch, l=l, size=size: r.at[l, :, _span(ch, size, LANE)]
                pieces.append((si, oi, lambda r, t, l=l: r.at[l], dv))
    if with_small:
        small = _pack_small([w[n] for n in GATHER_F32])
        srcs.append(small)
        outs.append(jax.ShapeDtypeStruct((NCHIP,) + small.shape, F32))
        pieces.append((len(srcs) - 1, len(outs) - 1, lambda r, t: r, lambda r, ch: r.at[ch]))
    return srcs, outs, pieces


def _gather_result(got, w, names, with_small):
    full = dict(zip(names, got[:len(names)]))
    if with_small:
        per_chip = [_unpack_small(got[-1][t], [w[n].shape for n in GATHER_F32]) for t in range(NCHIP)]
        for j, n in enumerate(GATHER_F32):
            full[n] = jnp.concatenate([per_chip[t][j] for t in range(NCHIP)], axis=SHARD_AXIS[n])
    return full


def _as_rows(a):
    return a.reshape(-1, a.shape[-1])


def _grad_plan(grads, w, names, with_small):
    srcs, outs, pieces = [], [], []
    for name in names:
        local = w[name]
        ax = SHARD_AXIS[name]
        size = local.shape[ax]
        oi = len(outs)
        layers = grads[name] if isinstance(grads[name], list) else [grads[name]]
        outs.append(jax.ShapeDtypeStruct((NCHIP,) + local.shape, layers[0].dtype))
        for l, g in enumerate(layers):
            si = len(srcs)
            srcs.append(g)
            if ax == 1:
                sv = lambda r, t, size=size: r.at[_span(t, size, 8), :]
            else:
                sv = lambda r, t, size=size: r.at[:, _span(t, size, LANE)]
            pieces.append((si, oi, sv, lambda r, ch, l=l: r.at[ch, l]))
    if with_small:
        small = jnp.stack([_pack_small([_shard(grads[n], t, SHARD_AXIS[n]) if n in SHARD_AXIS else grads[n]
                                        for n in SMALL]) for t in range(NCHIP)])
        srcs.append(small)
        outs.append(jax.ShapeDtypeStruct(small.shape, F32))
        pieces.append((len(srcs) - 1, len(outs) - 1, lambda r, t: r.at[t], lambda r, ch: r.at[ch]))
    return srcs, outs, pieces


def _reduce_and_update(names, landed, w, mom, var):
    partial = [_sum_chips(r.reshape(NCHIP, -1, r.shape[-1]), name="grad_sum_chips") for r in landed]
    sibling = _sibling_exchange(partial, name="grad_sibling")
    res = [dict(), dict(), dict(), dict()]
    for j, name in enumerate(names):
        outs4 = _adamw(partial[j], sibling[j], _as_rows(w[name]), _as_rows(mom[name]), _as_rows(var[name]), name="adamw")
        for d, o in zip(res, outs4):
            d[name] = o.reshape(w[name].shape)
    outs4 = _adamw(partial[-1], sibling[-1], *[_pack_small([t[n] for n in SMALL]) for t in (w, mom, var)], name="adamw")
    for d, o in zip(res, outs4):
        d.update(zip(SMALL, _unpack_small(o, [w[n].shape for n in SMALL])))
    return res


def kernel(x, positions, mla_w_a, mla_g_q, mla_g_kv, mla_w_uq, mla_w_ukv, mla_w_o, ssm_w_in, ssm_lambda_re, ssm_lambda_im, ssm_log_dt, ssm_b_re, ssm_b_im, ssm_c_re, ssm_c_im, ssm_d, ssm_w_glu, ffn_w_up, ffn_conv_w, ffn_conv_b, ffn_w_down, g_mix, g_ffn, g_final, loss_target, m_mla_w_a, m_mla_g_q, m_mla_g_kv, m_mla_w_uq, m_mla_w_ukv, m_mla_w_o, m_ssm_w_in, m_ssm_lambda_re, m_ssm_lambda_im, m_ssm_log_dt, m_ssm_b_re, m_ssm_b_im, m_ssm_c_re, m_ssm_c_im, m_ssm_d, m_ssm_w_glu, m_ffn_w_up, m_ffn_conv_w, m_ffn_conv_b, m_ffn_w_down, m_g_mix, m_g_ffn, m_g_final, v_mla_w_a, v_mla_g_q, v_mla_g_kv, v_mla_w_uq, v_mla_w_ukv, v_mla_w_o, v_ssm_w_in, v_ssm_lambda_re, v_ssm_lambda_im, v_ssm_log_dt, v_ssm_b_re, v_ssm_b_im, v_ssm_c_re, v_ssm_c_im, v_ssm_d, v_ssm_w_glu, v_ffn_w_up, v_ffn_conv_w, v_ffn_conv_b, v_ffn_w_down, v_g_mix, v_g_ffn, v_g_final):
    w = dict(zip(WNAMES, (mla_w_a, mla_g_q, mla_g_kv, mla_w_uq, mla_w_ukv, mla_w_o, ssm_w_in, ssm_lambda_re,
                          ssm_lambda_im, ssm_log_dt, ssm_b_re, ssm_b_im, ssm_c_re, ssm_c_im, ssm_d, ssm_w_glu,
                          ffn_w_up, ffn_conv_w, ffn_conv_b, ffn_w_down, g_mix, g_ffn, g_final)))
    mom = dict(zip(WNAMES, (m_mla_w_a, m_mla_g_q, m_mla_g_kv, m_mla_w_uq, m_mla_w_ukv, m_mla_w_o, m_ssm_w_in,
                            m_ssm_lambda_re, m_ssm_lambda_im, m_ssm_log_dt, m_ssm_b_re, m_ssm_b_im, m_ssm_c_re,
                            m_ssm_c_im, m_ssm_d, m_ssm_w_glu, m_ffn_w_up, m_ffn_conv_w, m_ffn_conv_b,
                            m_ffn_w_down, m_g_mix, m_g_ffn, m_g_final)))
    var = dict(zip(WNAMES, (v_mla_w_a, v_mla_g_q, v_mla_g_kv, v_mla_w_uq, v_mla_w_ukv, v_mla_w_o, v_ssm_w_in,
                            v_ssm_lambda_re, v_ssm_lambda_im, v_ssm_log_dt, v_ssm_b_re, v_ssm_b_im, v_ssm_c_re,
                            v_ssm_c_im, v_ssm_d, v_ssm_w_glu, v_ffn_w_up, v_ffn_conv_w, v_ffn_conv_b,
                            v_ffn_w_down, v_g_mix, v_g_ffn, v_g_final)))
    S = x.shape[1]
    D = D_MODEL
    x2 = x.reshape(S, D)
    tgt = loss_target.reshape(S, D)

    fw = _gather_result(_chip_exchange(*_gather_plan(w, MLA_W, False), name="gather_weights"), w, MLA_W, False)
    w_a = jnp.pad(fw['mla_w_a'], ((0, 0), (0, A_PAD - KR0 - QK_ROPE)))
    uq = fw['mla_w_uq'].reshape(Q_LORA, HEADS, QK_DIM)
    w_uq = jnp.concatenate([uq[:, :, :QK_NOPE].reshape(Q_LORA, HEADS * QK_NOPE),
                            jnp.pad(uq[:, :, QK_NOPE:], ((0, 0), (0, 0), (0, LANE - QK_ROPE))).reshape(Q_LORA, HEADS * LANE)],
                           axis=1)
    w_ukv = fw['mla_w_ukv']
    w_o = fw['mla_w_o']
    conv_b = w['ffn_conv_b']
    g_q, g_kv = w['mla_g_q'], w['mla_g_kv']
    gm, gf = w['g_mix'], w['g_ffn']
    gfin = w['g_final'].reshape(1, D)

    inv = 1.0 / (ROPE_THETA ** (jnp.arange(0, QK_ROPE, 2, dtype=F32) / QK_ROPE))
    ang = positions.reshape(S).astype(F32)[:, None] * inv
    cos, sin = jnp.cos(ang), jnp.sin(ang)
    zpad = jnp.zeros((S, LANE - QK_ROPE), F32)
    c128 = jnp.concatenate([cos, cos, zpad], axis=1)
    s128 = jnp.concatenate([-sin, sin, zpad], axis=1)

    hn0 = _rmsnorm_fwd(x2, gm[0:1], name="rms_mix0")
    a = _mm(hn0, w_a, name="mla_a")
    cqn, ckvn, kr = _mla_mid_fwd(a, g_q, g_kv, c128, s128, name="mla_mid_fwd")
    qfull = _mm(cqn, w_uq, name="mla_q")
    kv = _mm(ckvn, w_ukv, out_dtype=MXU, name="mla_kv")
    qs, ks, vs = _qk_prep(qfull, kv, kr, c128, s128, name="qk_prep")
    os_, lses = [], []
    rides = {0: (['ssm_w_in', 'ssm_w_glu', 'ffn_w_down'], True), 1: (['ffn_w_up'], False)}
    for h in range(HEADS):
        if h in rides:
            o_h, lse_h, *got = _attn_fwd(qs[h], ks[h], vs[h], name="attn_fwd_gather", ride=_gather_plan(w, *rides[h]))
            fw.update(_gather_result(got, w, *rides[h]))
        else:
            o_h, lse_h = _attn_fwd(qs[h], ks[h], vs[h], name="attn_fwd")
        os_.append(o_h)
        lses.append(lse_h)
    w_in = fw['ssm_w_in']
    w_glu = fw['ssm_w_glu']
    w_up = fw['ffn_w_up']
    w_down = fw['ffn_w_down']
    conv_w = fw['ffn_conv_w']
    dskip = fw['ssm_d']
    o_cat = jnp.concatenate(os_, axis=1)
    h1 = _mm(o_cat, w_o, res=x2, name="mla_o")

    def ffn_fwd(h, l):
        hn = _rmsnorm_fwd(h, gf[l:l + 1], name="rms_ffn")
        up = _mm(hn, w_up[l], name="ffn_up")
        act, cval, cgate = _convgate_fwd(up, conv_w[l], conv_b[l:l + 1], name="convgate_fwd")
        return _mm(act, w_down[l], res=h, name="ffn_down"), (hn, up, act, cval, cgate)

    h2, saved0 = ffn_fwd(h1, 0)

    lam_re, lam_im, log_dt = w['ssm_lambda_re'][0], w['ssm_lambda_im'][0], w['ssm_log_dt'][0]
    (a_re, a_im, bbar_r, bbar_i), disc_vjp = jax.vjp(_s5_discretize, lam_re, lam_im, log_dt, w['ssm_b_re'][0],
                                                    w['ssm_b_im'][0])
    c_re, c_im = w['ssm_c_re'][0], w['ssm_c_im'][0]
    bt_r, bt_i = jnp.swapaxes(bbar_r, 1, 2), jnp.swapaxes(bbar_i, 1, 2)
    wb = jnp.concatenate([_blockdiag(bt_r), _blockdiag(bt_i)], axis=2).astype(MXU)
    wbt = jnp.concatenate([_blockdiag(bbar_r), _blockdiag(bbar_i)], axis=1).astype(MXU)
    ct_r, ct_i = jnp.swapaxes(c_re, 1, 2), jnp.swapaxes(c_im, 1, 2)
    wc = jnp.concatenate([_blockdiag(ct_r), _blockdiag(-ct_i)], axis=1).astype(MXU)
    wct = jnp.concatenate([_blockdiag(c_re), _blockdiag(-c_im)], axis=2).astype(MXU)
    af_r, af_i = a_re.reshape(NSTATE), a_im.reshape(NSTATE)
    abc = jnp.broadcast_to(jnp.concatenate([af_r, af_i])[None], (NSEG, ST2))
    seg = S // NSEG
    ap_r, ap_i = _cpow(af_r, af_i, seg)

    hn1 = _rmsnorm_fwd(h2, gm[1:2], name="rms_mix1")
    u = _mm(hn1, w_in, name="s5_in")
    u_p = _perm(u)
    zero_state = jnp.zeros((NSEG, ST2), F32)
    ends = _s5_fwd(u_p, wb, wc, abc, dskip, zero_state, full=False, name="s5_fwd_ends")
    inits, cr, ci = [], jnp.zeros((NSTATE,), F32), jnp.zeros((NSTATE,), F32)
    for r in range(NSEG):
        inits.append(jnp.concatenate([cr, ci]))
        er, ei = ends[r, :NSTATE], ends[r, NSTATE:]
        cr, ci = er + ap_r * cr - ap_i * ci, ei + ap_r * ci + ap_i * cr
    x0 = jnp.stack(inits)
    xs, y_p, yg_p = _s5_fwd(u_p, wb, wc, abc, dskip, x0, full=True, name="s5_fwd")
    yg = _unperm(yg_p)
    z = _mm(yg, w_glu, name="s5_glu")
    h3 = _glu_fwd(z, h2, name="glu_fwd")
    h4, saved1 = ffn_fwd(h3, 1)

    loss_l, dh4, dg_final = _loss_head(h4, gfin, tgt, name="loss_head")

    grads = {}

    def ffn_bwd(h_in, g, saved, l):
        hn, up, act, cval, cgate = saved
        dact = _mm(g, w_down[l], mode="nt", name="ffn_down_dx")
        dw_down = _mm(act, g, mode="tn", out_dtype=MXU, name="ffn_down_dw")
        duv, dug, dwv, dwg, dbv, dbg = _convgate_bwd(up, cval, cgate, dact, conv_w[l], name="convgate_bwd")
        dw_up = jnp.concatenate([_mm(hn, duv, mode="tn", out_dtype=MXU, name="ffn_up_dw"), _mm(hn, dug, mode="tn", out_dtype=MXU, name="ffn_up_dw")],
                                axis=1)
        dhn = _mm(duv, w_up[l][:, :D_FF], mode="nt", name="ffn_up_dx")
        dhn = _mm(dug, w_up[l][:, D_FF:], mode="nt", res=dhn, name="ffn_up_dx_acc")
        dh, dg = _rmsnorm_bwd(h_in, gf[l:l + 1], dhn, g, name="rms_ffn_bwd")
        return dh, dict(w_up=dw_up, w_down=dw_down, conv_w=jnp.concatenate([dwv, dwg], axis=1),
                        conv_b=jnp.concatenate([dbv, dbg], axis=1)[0], g_ffn=dg[0])

    dh3, fg1 = ffn_bwd(h3, dh4, saved1, 1)

    dz = _glu_bwd(z, dh3, name="glu_bwd")
    grads['ssm_w_glu'] = _mm(yg, dz, mode="tn", out_dtype=MXU, name="s5_glu_dw")
    dyg = _mm(dz, w_glu, mode="nt", name="s5_glu_dx")
    dy_p = _gelu_bwd(y_p, _perm(dyg), name="gelu_bwd")
    firsts = _s5_bwd(dy_p, None, None, wct, None, abc, None, None, zero_state, full=False, name="s5_bwd_firsts")
    linits, cr, ci = [None] * NSEG, jnp.zeros((NSTATE,), F32), jnp.zeros((NSTATE,), F32)
    for r in reversed(range(NSEG)):
        linits[r] = jnp.concatenate([cr, ci])
        fr, fi = firsts[r, :NSTATE], firsts[r, NSTATE:]
        cr, ci = fr + ap_r * cr + ap_i * ci, fi + ap_r * ci - ap_i * cr
    l0 = jnp.stack(linits)
    du_p, dab, dwb, dwc, dd = _s5_bwd(dy_p, xs, u_p, wct, wbt, abc, dskip, x0, l0, full=True, name="s5_bwd")
    du = _unperm(du_p)
    grads['ssm_w_in'] = _mm(hn1, du, mode="tn", out_dtype=MXU, name="s5_in_dw")
    dhn1 = _mm(du, w_in, mode="nt", name="s5_in_dx")
    dh2, dg_mix1 = _rmsnorm_bwd(h2, gm[1:2], dhn1, dh3, name="rms_mix_bwd")
    da_sum = jnp.sum(dab, axis=0)
    dbt_r = _blockdiag_extract(dwb[:, :, :SW], SSM_GROUP, SSM_STATE)
    dbt_i = _blockdiag_extract(dwb[:, :, SW:], SSM_GROUP, SSM_STATE)
    dlr, dli, dlog_dt, dbr, dbi = disc_vjp((da_sum[:NSTATE].reshape(SSM_GROUPS, SSM_STATE),
                                            da_sum[NSTATE:].reshape(SSM_GROUPS, SSM_STATE),
                                            jnp.swapaxes(dbt_r, 1, 2), jnp.swapaxes(dbt_i, 1, 2)))
    dct_r = _blockdiag_extract(dwc[:, :SW, :], SSM_STATE, SSM_GROUP)
    dct_i = _blockdiag_extract(dwc[:, SW:, :], SSM_STATE, SSM_GROUP)
    grads['ssm_lambda_re'], grads['ssm_lambda_im'], grads['ssm_log_dt'] = dlr[None], dli[None], dlog_dt[None]
    grads['ssm_b_re'], grads['ssm_b_im'] = dbr[None], dbi[None]
    grads['ssm_c_re'] = jnp.swapaxes(dct_r, 1, 2)[None]
    grads['ssm_c_im'] = -jnp.swapaxes(dct_i, 1, 2)[None]
    grads['ssm_d'] = dd

    dh1, fg0 = ffn_bwd(h1, dh2, saved0, 0)
    grads['ffn_w_up'] = [fg0['w_up'], fg1['w_up']]
    grads['ffn_w_down'] = [fg0['w_down'], fg1['w_down']]
    grads['ffn_conv_w'] = jnp.stack([fg0['conv_w'], fg1['conv_w']])
    grads['ffn_conv_b'] = jnp.stack([fg0['conv_b'], fg1['conv_b']])
    grads['g_ffn'] = jnp.stack([fg0['g_ffn'], fg1['g_ffn']])

    do_cat = _mm(dh1, w_o, mode="nt", out_dtype=MXU, name="mla_o_dx")
    grads['mla_w_o'] = _mm(o_cat, dh1, mode="tn", out_dtype=MXU, name="mla_o_dw")
    dqs, dks, dvs = [], [], []
    for h in range(HEADS):
        do_h = do_cat[:, h * V_HEAD:(h + 1) * V_HEAD]
        delta = _attn_delta(do_h, os_[h], name="attn_delta")
        tiles = (S // _rows(S, BWD_T), 1, _rows(S, BWD_T))
        lse2 = (lses[h] * math.log2(math.e)).reshape(tiles)
        if h == 0:
            dq_h, dk_h, dv_h, *landed = _attn_bwd(qs[h], ks[h], vs[h], do_h, lse2, delta.reshape(tiles),
                                                  name="attn_bwd_exchange", ride=_grad_plan(grads, w, REST_W, False))
        else:
            dq_h, dk_h, dv_h = _attn_bwd(qs[h], ks[h], vs[h], do_h, lse2, delta.reshape(tiles), name="attn_bwd")
        dqs.append(dq_h)
        dks.append(dk_h)
        dvs.append(dv_h)
    dqfull, dkv, dkr = _qk_prep_bwd(dqs, dks, dvs, c128, s128, name="qk_prep_bwd")
    dw_uq_p = _mm(cqn, dqfull, mode="tn", out_dtype=MXU, name="mla_q_dw")
    dcqn = _mm(dqfull, w_uq, mode="nt", name="mla_q_dx")
    grads['mla_w_ukv'] = _mm(ckvn, dkv, mode="tn", out_dtype=MXU, name="mla_kv_dw")
    dckvn = _mm(dkv, w_ukv, mode="nt", name="mla_kv_dx")
    da, dgq, dgkv = _mla_mid_bwd(a, dcqn, dckvn, dkr, g_q, g_kv, c128, s128, name="mla_mid_bwd")
    grads['mla_w_a'] = _mm(hn0, da, mode="tn", out_dtype=MXU, name="mla_a_dw")[:, :KR0 + QK_ROPE]
    dhn0 = _mm(da, w_a, mode="nt", name="mla_a_dx")
    dx, dg_mix0 = _rmsnorm_bwd(x2, gm[0:1], dhn0, dh1, name="rms_mix_bwd")
    grads['mla_w_uq'] = jnp.concatenate(
        [dw_uq_p[:, :HEADS * QK_NOPE].reshape(Q_LORA, HEADS, QK_NOPE),
         dw_uq_p[:, HEADS * QK_NOPE:].reshape(Q_LORA, HEADS, LANE)[:, :, :QK_ROPE]], axis=2).reshape(Q_LORA, HEADS * QK_DIM)
    grads['mla_g_q'], grads['mla_g_kv'] = dgq, dgkv
    grads['g_mix'] = jnp.concatenate([dg_mix0, dg_mix1], axis=0)
    grads['g_final'] = dg_final[0]

    landed += _chip_exchange(*_grad_plan(grads, w, MLA_W, True), name="grad_exchange")
    g_out, d_out, m_out, v_out = _reduce_and_update(REST_W + MLA_W, landed, w, mom, var)

    loss = lax.psum(loss_l[0, 0], ("x", "y", "c"))
    return (loss, dx.reshape(1, S, D), *[g_out[n] for n in WNAMES], *[d_out[n] for n in WNAMES],
            *[m_out[n] for n in WNAMES], *[v_out[n] for n in WNAMES])
```

```python
import math

import jax
import jax.numpy as jnp
from jax import lax
from jax.experimental import pallas as pl
from jax.experimental.pallas import tpu as pltpu

F32 = jnp.float32
MXU = jnp.bfloat16

D_MODEL = 1024
CHUNK = 64
EPS = 1e-6
HEADS = 8
QK_NOPE = 128
QK_ROPE = 64
V_HEAD = 128
Q_LORA = 384
KV_LORA = 256
ROPE_THETA = 10000.0
QK_DIM = QK_NOPE + QK_ROPE
SSM_GROUP = 16
SSM_GROUPS = D_MODEL // SSM_GROUP
SSM_STATE = 64
NSTATE = SSM_GROUPS * SSM_STATE
D_FF = 2816
ATT_SCALE = QK_DIM ** -0.5
EXP2_SCALE = ATT_SCALE * math.log2(math.e)
NEG = -1e30
NSEG = 8
SBLK = 8

ADAM_LR = 0.001
ADAM_B1 = 0.9
ADAM_B2 = 0.999
ADAM_EPS = 1e-08
ADAM_WD = 0.01
ADAM_STEP = 10

LANE = 128
TN_MAX_COLS = 2816
NN_MAX_COLS = 1408
NN_MAX_K = 2816
TN_ACC_ELEMS = 1536 * 1024
VMEM_BIG = 56 * 1024 * 1024

WNAMES = ['mla_w_a', 'mla_g_q', 'mla_g_kv', 'mla_w_uq', 'mla_w_ukv', 'mla_w_o', 'ssm_w_in', 'ssm_lambda_re',
          'ssm_lambda_im', 'ssm_log_dt', 'ssm_b_re', 'ssm_b_im', 'ssm_c_re', 'ssm_c_im', 'ssm_d', 'ssm_w_glu',
          'ffn_w_up', 'ffn_conv_w', 'ffn_conv_b', 'ffn_w_down', 'g_mix', 'g_ffn', 'g_final']
SHARD_AXIS = {'mla_w_a': 1, 'mla_w_uq': 2, 'mla_w_ukv': 2, 'mla_w_o': 1, 'ssm_w_in': 1, 'ssm_d': 1,
              'ssm_w_glu': 2, 'ffn_w_up': 2, 'ffn_conv_w': 2, 'ffn_w_down': 1}
GATHER_F32 = ['ssm_d', 'ffn_conv_w']
NCHIP = 4
PACKW = 1024
SMALL_ROWS = 64
MESH = pl.DeviceIdType.MESH


def _tile(d, pref):
    t = min(pref, d) // LANE * LANE
    while t >= LANE:
        if d % t == 0:
            return t
        t -= LANE
    return d


def _rows(s, pref):
    t = min(s, pref)
    assert s % t == 0 and t % 8 == 0
    return t


def _params(big=False):
    if big:
        return pltpu.CompilerParams(vmem_limit_bytes=VMEM_BIG)
    return pltpu.CompilerParams(vmem_limit_bytes=40 * 1024 * 1024)


def _mm(a, b, *, name, mode="nn", out_dtype=F32, res=None, tm=None, tn=None, tk=None):
    N = b.shape[0] if mode == "nt" else b.shape[1]
    if mode != "tn":
        M, K = a.shape
        if N > 1024 and N % 512:
            tn = tn or N
            tm = tm or _rows(M, 512)
        narrow = jnp.dtype(a.dtype).itemsize == 2
        tk = tk or _tile(K, NN_MAX_K if narrow else NN_MAX_COLS)
        tm = tm or _rows(M, 2048 if narrow and K <= 1024 else 1024)
    else:
        K, M = a.shape
        tn = tn or _tile(N, TN_MAX_COLS)
        tm = tm or _tile(M, max(LANE, TN_ACC_ELEMS // tn))
    assert b.shape[1 if mode == "nt" else 0] == K
    tn = tn or _tile(N, NN_MAX_COLS if N % 512 else 512)
    tk = tk or _rows(K, 512)
    nk = K // tk
    has_res = res is not None

    def body(a_ref, b_ref, *rest):
        if has_res:
            r_ref, o_ref, acc = rest
        else:
            o_ref, acc = rest
        k = pl.program_id(2)

        @pl.when(k == 0)
        def _():
            acc[...] = jnp.zeros_like(acc)

        av = a_ref[...].astype(MXU)
        bv = b_ref[...].astype(MXU)
        if mode == "nn":
            acc[...] += jnp.dot(av, bv, preferred_element_type=F32)
        elif mode == "nt":
            acc[...] += _dot_nt(av, bv)
        else:
            acc[...] += _dot_tn(av, bv)

        @pl.when(k == nk - 1)
        def _():
            o = acc[...]
            if has_res:
                o = o + r_ref[...]
            o_ref[...] = o.astype(o_ref.dtype)

    if mode == "tn":
        a_spec = pl.BlockSpec((tk, tm), lambda i, j, k: (k, i))
    else:
        a_spec = pl.BlockSpec((tm, tk), lambda i, j, k: (i, k))
    if mode == "nt":
        b_spec = pl.BlockSpec((tn, tk), lambda i, j, k: (j, k))
    else:
        b_spec = pl.BlockSpec((tk, tn), lambda i, j, k: (k, j))
    in_specs = [a_spec, b_spec]
    ops = [a, b]
    if has_res:
        in_specs.append(pl.BlockSpec((tm, tn), lambda i, j, k: (i, j)))
        ops.append(res)
    return pl.pallas_call(
        body, name=name, grid=(M // tm, N // tn, nk), in_specs=in_specs,
        out_specs=pl.BlockSpec((tm, tn), lambda i, j, k: (i, j)),
        out_shape=jax.ShapeDtypeStruct((M, N), out_dtype),
        scratch_shapes=[pltpu.VMEM((tm, tn), F32)], compiler_params=_params(),
    )(*ops)


def _row_spec(tm, c):
    return pl.BlockSpec((tm, c), lambda i: (i, 0))


def _const_spec(r, c):
    return pl.BlockSpec((r, c), lambda i: (0, 0))


def _rms_parts(xv):
    r = lax.rsqrt(jnp.mean(xv * xv, axis=-1, keepdims=True) + EPS)
    return r, xv * r


def _rms_vjp(xv, gv, dyv):
    r, xhat = _rms_parts(xv)
    gy = dyv * gv
    dx = r * (gy - xhat * jnp.mean(gy * xhat, axis=-1, keepdims=True))
    return dx, dyv * xhat


def _rmsnorm_fwd(x, g, *, name):
    S, D = x.shape
    tm = _rows(S, 512)

    def body(x_ref, g_ref, o_ref):
        _, xhat = _rms_parts(x_ref[...])
        o_ref[...] = (xhat * g_ref[...]).astype(o_ref.dtype)

    return pl.pallas_call(
        body, name=name, grid=(S // tm,), in_specs=[_row_spec(tm, D), _const_spec(1, D)],
        out_specs=_row_spec(tm, D), out_shape=jax.ShapeDtypeStruct((S, D), MXU), compiler_params=_params(),
    )(x, g)


def _rmsnorm_bwd(x, g, dy, dres, *, name):
    S, D = x.shape
    tm = _rows(S, 512)

    def body(x_ref, g_ref, dy_ref, dr_ref, dx_ref, dg_ref):
        @pl.when(pl.program_id(0) == 0)
        def _():
            dg_ref[...] = jnp.zeros_like(dg_ref)

        dx, dgp = _rms_vjp(x_ref[...], g_ref[...], dy_ref[...])
        dx_ref[...] = dr_ref[...] + dx
        dg_ref[...] += jnp.sum(dgp, axis=0, keepdims=True)

    return pl.pallas_call(
        body, name=name, grid=(S // tm,),
        in_specs=[_row_spec(tm, D), _const_spec(1, D), _row_spec(tm, D), _row_spec(tm, D)],
        out_specs=(_row_spec(tm, D), _const_spec(1, D)),
        out_shape=(jax.ShapeDtypeStruct((S, D), F32), jax.ShapeDtypeStruct((1, D), F32)),
        compiler_params=_params(),
    )(x, g, dy, dres)


def _loss_head(h, g, tgt, *, name):
    S, D = h.shape
    tm = _rows(S, 512)

    def body(h_ref, g_ref, t_ref, l_ref, dh_ref, dg_ref):
        @pl.when(pl.program_id(0) == 0)
        def _():
            l_ref[...] = jnp.zeros_like(l_ref)
            dg_ref[...] = jnp.zeros_like(dg_ref)

        hv = h_ref[...]
        gv = g_ref[...]
        _, xhat = _rms_parts(hv)
        e = xhat * gv - t_ref[...]
        l_ref[...] += 0.5 * jnp.sum(jnp.mean(e * e, axis=-1, keepdims=True), axis=0, keepdims=True)
        dx, dgp = _rms_vjp(hv, gv, e * (1.0 / D))
        dh_ref[...] = dx
        dg_ref[...] += jnp.sum(dgp, axis=0, keepdims=True)

    return pl.pallas_call(
        body, name=name, grid=(S // tm,),
        in_specs=[_row_spec(tm, D), _const_spec(1, D), _row_spec(tm, D)],
        out_specs=(_const_spec(1, 1), _row_spec(tm, D), _const_spec(1, D)),
        out_shape=(jax.ShapeDtypeStruct((1, 1), F32), jax.ShapeDtypeStruct((S, D), F32),
                   jax.ShapeDtypeStruct((1, D), F32)),
        compiler_params=_params(),
    )(h, g, tgt)


def _swap_halves(g):
    lane = lax.broadcasted_iota(jnp.int32, g.shape, 1)
    return jnp.where(lane < QK_ROPE // 2, pltpu.roll(g, LANE - QK_ROPE // 2, axis=1),
                     pltpu.roll(g, QK_ROPE // 2, axis=1))


def _rope128(g, c128, s128):
    return g * c128 + _swap_halves(g) * s128


def _rope128_vjp(dy, c128, s128):
    lane = lax.broadcasted_iota(jnp.int32, dy.shape, 1)
    return jnp.where(lane < QK_ROPE, dy * c128 + _swap_halves(dy * s128), 0.0)


A_PAD = 768
KR0 = Q_LORA + KV_LORA


def _mla_mid_fwd(a, g_q, g_kv, c128, s128, *, name):
    S = a.shape[0]
    tm = _rows(S, 512)

    def body(a_ref, gq_ref, gkv_ref, c_ref, s_ref, cq_ref, ckv_ref, kr_ref):
        av = a_ref[...]
        _, qh = _rms_parts(av[:, :Q_LORA])
        cq_ref[...] = (qh * gq_ref[...]).astype(cq_ref.dtype)
        _, kh = _rms_parts(av[:, Q_LORA:KR0])
        ckv_ref[...] = (kh * gkv_ref[...]).astype(ckv_ref.dtype)
        kr = _rope128(av[:, KR0:A_PAD], c_ref[...], s_ref[...])
        kr_ref[...] = kr[:, :QK_ROPE].astype(kr_ref.dtype)

    return pl.pallas_call(
        body, name=name, grid=(S // tm,),
        in_specs=[_row_spec(tm, A_PAD), _const_spec(1, Q_LORA), _const_spec(1, KV_LORA), _row_spec(tm, LANE),
                  _row_spec(tm, LANE)],
        out_specs=(_row_spec(tm, Q_LORA), _row_spec(tm, KV_LORA), _row_spec(tm, QK_ROPE)),
        out_shape=(jax.ShapeDtypeStruct((S, Q_LORA), MXU), jax.ShapeDtypeStruct((S, KV_LORA), MXU),
                   jax.ShapeDtypeStruct((S, QK_ROPE), MXU)),
        compiler_params=_params(),
    )(a, g_q, g_kv, c128, s128)


def _mla_mid_bwd(a, dcq, dckv, dkr, g_q, g_kv, c128, s128, *, name):
    S = a.shape[0]
    tm = _rows(S, 512)

    def body(a_ref, dcq_ref, dckv_ref, dkr_ref, gq_ref, gkv_ref, c_ref, s_ref, da_ref, dgq_ref, dgkv_ref):
        @pl.when(pl.program_id(0) == 0)
        def _():
            dgq_ref[...] = jnp.zeros_like(dgq_ref)
            dgkv_ref[...] = jnp.zeros_like(dgkv_ref)

        av = a_ref[...]
        dx, dgp = _rms_vjp(av[:, :Q_LORA], gq_ref[...], dcq_ref[...])
        da_ref[:, :Q_LORA] = dx.astype(da_ref.dtype)
        dgq_ref[...] += jnp.sum(dgp, axis=0, keepdims=True)
        dx, dgp = _rms_vjp(av[:, Q_LORA:KR0], gkv_ref[...], dckv_ref[...])
        da_ref[:, Q_LORA:KR0] = dx.astype(da_ref.dtype)
        dgkv_ref[...] += jnp.sum(dgp, axis=0, keepdims=True)
        da_ref[:, KR0:A_PAD] = _rope128_vjp(dkr_ref[...], c_ref[...], s_ref[...]).astype(da_ref.dtype)

    return pl.pallas_call(
        body, name=name, grid=(S // tm,),
        in_specs=[_row_spec(tm, A_PAD), _row_spec(tm, Q_LORA), _row_spec(tm, KV_LORA), _row_spec(tm, LANE),
                  _const_spec(1, Q_LORA), _const_spec(1, KV_LORA), _row_spec(tm, LANE), _row_spec(tm, LANE)],
        out_specs=(_row_spec(tm, A_PAD), _const_spec(1, Q_LORA), _const_spec(1, KV_LORA)),
        out_shape=(jax.ShapeDtypeStruct((S, A_PAD), MXU), jax.ShapeDtypeStruct((1, Q_LORA), F32),
                   jax.ShapeDtypeStruct((1, KV_LORA), F32)),
        compiler_params=_params(),
    )(a, dcq, dckv, dkr, g_q, g_kv, c128, s128)


QF = 2 * HEADS * LANE
KVF = HEADS * (QK_NOPE + V_HEAD)
VX = 2 * V_HEAD


def _qk_prep(qfull, kv, kr, c128, s128, *, name):
    S = qfull.shape[0]
    tm = _rows(S, 256)

    def body(q_ref, kv_ref, kr_ref, c_ref, s_ref, *outs):
        qo, ko, vo = outs[:HEADS], outs[HEADS:2 * HEADS], outs[2 * HEADS:]
        cv, sv = c_ref[...], s_ref[...]
        krv = kr_ref[...]
        for h in range(HEADS):
            qo[h][:, :QK_NOPE] = q_ref[:, h * LANE:(h + 1) * LANE].astype(MXU)
            g = q_ref[:, (HEADS + h) * LANE:(HEADS + h + 1) * LANE]
            qo[h][:, QK_NOPE:] = _rope128(g, cv, sv)[:, :QK_ROPE].astype(MXU)
            ko[h][:, :QK_NOPE] = kv_ref[:, 2 * h * LANE:(2 * h + 1) * LANE]
            ko[h][:, QK_NOPE:] = krv
            vo[h][:, :V_HEAD] = kv_ref[:, (2 * h + 1) * LANE:(2 * h + 2) * LANE]
            vo[h][:, V_HEAD:] = jnp.ones((tm, VX - V_HEAD), MXU)

    shapes = ([jax.ShapeDtypeStruct((S, QK_DIM), MXU)] * (2 * HEADS)
              + [jax.ShapeDtypeStruct((S, VX), MXU)] * HEADS)
    specs = [_row_spec(tm, QK_DIM)] * (2 * HEADS) + [_row_spec(tm, VX)] * HEADS
    outs = pl.pallas_call(
        body, name=name, grid=(S // tm,),
        in_specs=[_row_spec(tm, QF), _row_spec(tm, KVF), _row_spec(tm, QK_ROPE), _row_spec(tm, LANE),
                  _row_spec(tm, LANE)],
        out_specs=tuple(specs), out_shape=tuple(shapes), compiler_params=_params(),
    )(qfull, kv, kr, c128, s128)
    return outs[:HEADS], outs[HEADS:2 * HEADS], outs[2 * HEADS:]


def _qk_prep_bwd(dqs, dks, dvs, c128, s128, *, name):
    S = dqs[0].shape[0]
    tm = _rows(S, 256)

    def body(*refs):
        dq = refs[:HEADS]
        dk = refs[HEADS:2 * HEADS]
        dv = refs[2 * HEADS:3 * HEADS]
        c_ref, s_ref, dqf_ref, dkv_ref, dkr_ref, tmp = refs[3 * HEADS:]
        cv, sv = c_ref[...], s_ref[...]
        tmp[...] = jnp.zeros_like(tmp)
        dkr_ref[...] = jnp.zeros_like(dkr_ref)
        for h in range(HEADS):
            dqf_ref[:, h * LANE:(h + 1) * LANE] = dq[h][:, :QK_NOPE].astype(MXU)
            tmp[:, :QK_ROPE] = dq[h][:, QK_NOPE:]
            dqf_ref[:, (HEADS + h) * LANE:(HEADS + h + 1) * LANE] = _rope128_vjp(tmp[...], cv, sv).astype(MXU)
            dkv_ref[:, 2 * h * LANE:(2 * h + 1) * LANE] = dk[h][:, :QK_NOPE].astype(MXU)
            dkv_ref[:, (2 * h + 1) * LANE:(2 * h + 2) * LANE] = dv[h][...].astype(MXU)
            dkr_ref[:, :QK_ROPE] += dk[h][:, QK_NOPE:]

    return pl.pallas_call(
        body, name=name, grid=(S // tm,),
        in_specs=[_row_spec(tm, QK_DIM)] * (2 * HEADS) + [_row_spec(tm, V_HEAD)] * HEADS
        + [_row_spec(tm, LANE), _row_spec(tm, LANE)],
        out_specs=(_row_spec(tm, QF), _row_spec(tm, KVF), _row_spec(tm, LANE)),
        out_shape=(jax.ShapeDtypeStruct((S, QF), MXU), jax.ShapeDtypeStruct((S, KVF), MXU),
                   jax.ShapeDtypeStruct((S, LANE), F32)),
        scratch_shapes=[pltpu.VMEM((tm, LANE), F32)], compiler_params=_params(),
    )(*dqs, *dks, *dvs, c128, s128)


def _dot_nt(a, b):
    return lax.dot_general(a, b, (((1,), (1,)), ((), ())), preferred_element_type=F32)


def _dot_tn(a, b):
    return lax.dot_general(a, b, (((0,), (0,)), ((), ())), preferred_element_type=F32)


def _attn_fwd(q, k, vx, *, name, ride=None):
    S = q.shape[0]
    T = _rows(S, 1024)
    n = S // T
    cpt = T // CHUNK

    r_srcs, r_outs, r_pieces = ride or ((), (), ())
    ns, no = len(r_srcs), len(r_outs)

    def body(q_ref, k_ref, v_ref, *rest):
        o_ref, lse_ref = rest[ns:ns + 2]
        s_buf, p_buf, a_buf, m_s, acc_s = rest[ns + 2 + no:ns + 7 + no]
        i = pl.program_id(0)
        if ride:
            start, finish = _exchange_ops(r_pieces, rest[:ns], rest[ns + 2:ns + 2 + no], *rest[ns + 7 + no:])
            pl.when(i == 0)(start)
        qc = lax.broadcasted_iota(jnp.int32, (T, T), 0) // CHUNK
        kc = lax.broadcasted_iota(jnp.int32, (T, T), 1) // CHUNK
        dchunk = kc - qc

        def tile_rows(b):
            return pl.ds(pl.multiple_of(jnp.clip(b, 0, n - 1) * T, T), T)

        def scores(b, slot):
            s = _dot_nt(q_ref[...], k_ref[tile_rows(b), :])
            s_buf[slot] = jnp.where(dchunk <= (i - b) * cpt, s, NEG)

        def softmax(slot):
            s = s_buf[slot]
            m_prev = m_s[...]
            m_new = jnp.maximum(m_prev, jnp.max(s, axis=1, keepdims=True))
            a_buf[slot] = jnp.exp2((m_prev - m_new) * EXP2_SCALE)
            p_buf[slot] = jnp.exp2((s - m_new) * EXP2_SCALE).astype(MXU)
            m_s[...] = m_new

        def pv(b, slot):
            acc_s[...] = a_buf[slot] * acc_s[...] + jnp.dot(p_buf[slot], v_ref[tile_rows(b), :],
                                                              preferred_element_type=F32)

        m_s[...] = jnp.full_like(m_s, NEG)
        acc_s[...] = jnp.zeros_like(acc_s)
        p_buf[1] = jnp.zeros((T, T), MXU)
        a_buf[1] = jnp.ones((T, 1), F32)
        scores(0, 0)

        def pair(u, carry):
            t = 2 * u
            scores(t + 1, 1)
            softmax(0)
            pv(t - 1, 1)
            scores(t + 2, 0)
            softmax(1)
            pv(t, 0)
            return carry

        npairs = (i + 2) // 2
        lax.fori_loop(0, npairs, pair, 0)
        pv(2 * npairs - 1, 1)
        acc = acc_s[...]
        l = acc[:, V_HEAD:V_HEAD + 1]
        o_ref[...] = (acc[:, :V_HEAD] / l).astype(o_ref.dtype)
        lse_ref[...] = m_s[...] * ATT_SCALE + jnp.log(l)
        if ride:
            pl.when(i == n - 1)(finish)

    hbm = pl.BlockSpec(memory_space=pl.ANY)
    return pl.pallas_call(
        body, name=name, grid=(n,),
        in_specs=[pl.BlockSpec((T, QK_DIM), lambda i: (i, 0)), pl.BlockSpec((S, QK_DIM), lambda i: (0, 0)),
                  pl.BlockSpec((S, VX), lambda i: (0, 0))] + [hbm] * ns,
        out_specs=(pl.BlockSpec((T, V_HEAD), lambda i: (i, 0)), pl.BlockSpec((T, 1), lambda i: (i, 0))) + (hbm,) * no,
        out_shape=(jax.ShapeDtypeStruct((S, V_HEAD), MXU), jax.ShapeDtypeStruct((S, 1), F32)) + tuple(r_outs),
        scratch_shapes=[pltpu.VMEM((2, T, T), F32), pltpu.VMEM((2, T, T), MXU), pltpu.VMEM((2, T, 1), F32),
                        pltpu.VMEM((T, 1), F32), pltpu.VMEM((T, VX), F32)]
        + (_exchange_scratch(len(r_pieces)) if ride else []),
        compiler_params=_params(big=True),
    )(q, k, vx, *r_srcs)


def _attn_delta(do, o, *, name):
    S = do.shape[0]
    tm = _rows(S, 512)

    def body(do_ref, o_ref, d_ref):
        for h in range(HEADS):
            cols = slice(h * V_HEAD, (h + 1) * V_HEAD)
            d_ref[:, h:h + 1] = jnp.sum(do_ref[:, cols].astype(F32) * o_ref[:, cols].astype(F32), axis=1,
                                        keepdims=True)

    width = HEADS * V_HEAD
    return pl.pallas_call(
        body, name=name, grid=(S // tm,), in_specs=[_row_spec(tm, width), _row_spec(tm, width)],
        out_specs=_row_spec(tm, HEADS), out_shape=jax.ShapeDtypeStruct((S, HEADS), F32), compiler_params=_params(),
    )(do, o)


BWD_T = 512


def _attn_bwd(q, k, v, do, lse2, delta, *, name, ride=None):
    S = q.shape[0]
    T = _rows(S, BWD_T)
    n = S // T
    cpt = T // CHUNK

    r_srcs, r_outs, r_pieces = ride or ((), (), ())
    ns, no = len(r_srcs), len(r_outs)

    def body(q_hbm, k_ref, v_ref, do_hbm, lse_ref, dl_ref, *rest):
        dq_hbm, dk_ref, dv_ref = rest[ns:ns + 3]
        q_res, do_res, dq_s, s_buf, dp_buf, p_buf, ds_buf, dk_s, dv_s = rest[ns + 3 + no:ns + 12 + no]
        j = pl.program_id(0)
        if ride:
            start, finish = _exchange_ops(r_pieces, rest[:ns], rest[ns + 3:ns + 3 + no], *rest[ns + 12 + no:])
            pl.when(j == 0)(start)

        @pl.when(j == 0)
        def _():
            pltpu.sync_copy(q_hbm, q_res)
            pltpu.sync_copy(do_hbm, do_res)
            dq_s[...] = jnp.zeros_like(dq_s)

        kc = lax.broadcasted_iota(jnp.int32, (T, T), 0) // CHUNK
        qc = lax.broadcasted_iota(jnp.int32, (T, T), 1) // CHUNK
        dchunk = kc - qc

        def tile(t):
            return jnp.clip(j + t, 0, n - 1)

        def rows(t):
            return pl.ds(pl.multiple_of(tile(t) * T, T), T)

        def scores(t, slot):
            visible_up_to = jnp.where(j + t < n, t * cpt, -2 * cpt)
            s = _dot_nt(k_ref[...], q_res[rows(t), :])
            s_buf[slot] = jnp.where(dchunk <= visible_up_to, s, NEG)
            dp_buf[slot] = _dot_nt(v_ref[...], do_res[rows(t), :])

        def probs(t, slot):
            pt = jnp.exp2(s_buf[slot] * EXP2_SCALE - lse_ref[tile(t)])
            p_buf[slot] = pt.astype(MXU)
            ds_buf[slot] = (pt * (dp_buf[slot] - dl_ref[tile(t)]) * ATT_SCALE).astype(MXU)

        def grads(t, slot):
            r = rows(t)
            dv_s[...] += jnp.dot(p_buf[slot], do_res[r, :], preferred_element_type=F32)
            ds = ds_buf[slot]
            dk_s[...] += jnp.dot(ds, q_res[r, :], preferred_element_type=F32)
            dq_s[r, :] += _dot_tn(ds, k_ref[...])

        dk_s[...] = jnp.zeros_like(dk_s)
        dv_s[...] = jnp.zeros_like(dv_s)
        p_buf[1] = jnp.zeros((T, T), MXU)
        ds_buf[1] = jnp.zeros((T, T), MXU)
        scores(0, 0)

        def pair(u, carry):
            t = 2 * u
            scores(t + 1, 1)
            probs(t, 0)
            grads(t - 1, 1)
            scores(t + 2, 0)
            probs(t + 1, 1)
            grads(t, 0)
            return carry

        npairs = (n - j + 1) // 2
        lax.fori_loop(0, npairs, pair, 0)
        grads(2 * npairs - 1, 1)
        dk_ref[...] = dk_s[...]
        dv_ref[...] = dv_s[...]

        @pl.when(j == n - 1)
        def _():
            pltpu.sync_copy(dq_s, dq_hbm)
            if ride:
                finish()

    hbm = pl.BlockSpec(memory_space=pl.ANY)
    k_map = lambda j: (j, 0)
    whole = pl.BlockSpec((n, 1, T), lambda j: (0, 0, 0))
    return pl.pallas_call(
        body, name=name, grid=(n,),
        in_specs=[hbm, pl.BlockSpec((T, QK_DIM), k_map), pl.BlockSpec((T, V_HEAD), k_map), hbm, whole, whole]
        + [hbm] * ns,
        out_specs=(hbm, pl.BlockSpec((T, QK_DIM), k_map), pl.BlockSpec((T, V_HEAD), k_map)) + (hbm,) * no,
        out_shape=(jax.ShapeDtypeStruct((S, QK_DIM), F32), jax.ShapeDtypeStruct((S, QK_DIM), F32),
                   jax.ShapeDtypeStruct((S, V_HEAD), F32)) + tuple(r_outs),
        scratch_shapes=[pltpu.VMEM((S, QK_DIM), MXU), pltpu.VMEM((S, V_HEAD), MXU), pltpu.VMEM((S, QK_DIM), F32),
                        pltpu.VMEM((2, T, T), F32), pltpu.VMEM((2, T, T), F32), pltpu.VMEM((2, T, T), MXU),
                        pltpu.VMEM((2, T, T), MXU), pltpu.VMEM((T, QK_DIM), F32), pltpu.VMEM((T, V_HEAD), F32)]
        + (_exchange_scratch(len(r_pieces)) if ride else []),
        compiler_params=_params(big=True),
    )(q, k, v, do, lse2, delta, *r_srcs)


HALO = 8
CONV_RH = 64


def _conv_tiles(S):
    tm = _rows(S, 256)
    tc = D_FF // 2
    return tm, tc, D_FF // tc


def _silu_parts(gate):
    sg = jax.nn.sigmoid(gate)
    return sg, gate * sg


def _convgate_fwd(up, cw, cb, *, name):
    S = up.shape[0]
    tm, tc, nc = _conv_tiles(S)
    hb = tm // HALO

    def body(v_ref, g_ref, hv_ref, hg_ref, wv_ref, wg_ref, bv_ref, bg_ref, o_ref, cv_ref, cg_ref):
        keep = (pl.program_id(0) > 0).astype(F32)

        def chunk(cc, carry):
            cols = pl.ds(pl.multiple_of(cc * LANE, LANE), LANE)
            wv, wg, bv, bg = wv_ref[:, cols], wg_ref[:, cols], bv_ref[:, cols], bg_ref[:, cols]
            for r0 in range(0, tm, CONV_RH):
                def conv(t_ref, h_ref, w, b):
                    if r0:
                        span = t_ref[pl.ds(r0 - HALO, CONV_RH + HALO), cols]
                    else:
                        span = jnp.concatenate([h_ref[:, cols] * keep, t_ref[pl.ds(0, CONV_RH), cols]], axis=0)
                    back2, back1 = [pltpu.roll(span, s, axis=0)[HALO:] for s in (2, 1)]
                    return w[0:1] * back2 + w[1:2] * back1 + w[2:3] * span[HALO:] + b
                val = conv(v_ref, hv_ref, wv, bv)
                gate = conv(g_ref, hg_ref, wg, bg)
                cv_ref[pl.ds(r0, CONV_RH), cols] = val.astype(cv_ref.dtype)
                cg_ref[pl.ds(r0, CONV_RH), cols] = gate.astype(cg_ref.dtype)
                o_ref[pl.ds(r0, CONV_RH), cols] = (_silu_parts(gate)[1] * val).astype(o_ref.dtype)
            return carry

        lax.fori_loop(0, tc // LANE, chunk, 0)

    prev = lambda i: jnp.maximum(i * hb - 1, 0)
    return pl.pallas_call(
        body, name=name, grid=(S // tm, nc),
        in_specs=[pl.BlockSpec((tm, tc), lambda i, j: (i, j)), pl.BlockSpec((tm, tc), lambda i, j: (i, j + nc)),
                  pl.BlockSpec((HALO, tc), lambda i, j: (prev(i), j)),
                  pl.BlockSpec((HALO, tc), lambda i, j: (prev(i), j + nc)),
                  pl.BlockSpec((3, tc), lambda i, j: (0, j)), pl.BlockSpec((3, tc), lambda i, j: (0, j + nc)),
                  pl.BlockSpec((1, tc), lambda i, j: (0, j)), pl.BlockSpec((1, tc), lambda i, j: (0, j + nc))],
        out_specs=(pl.BlockSpec((tm, tc), lambda i, j: (i, j)),) * 3,
        out_shape=(jax.ShapeDtypeStruct((S, D_FF), MXU),) * 3,
        compiler_params=_params(),
    )(up, up, up, up, cw, cw, cb, cb)


def _convgate_bwd(up, cval, cgate, dact, cw, *, name):
    S = up.shape[0]
    tm, tc, nc = _conv_tiles(S)
    hb = tm // HALO
    nr = S // tm
    R = tm + HALO

    def body(uv_ref, ug_ref, cv_ref, cg_ref, ncv_ref, ncg_ref, da_ref, dan_ref, wv_ref, wg_ref,
             duv_ref, dug_ref, dwv_ref, dwg_ref, dbv_ref, dbg_ref, dsv, dsg):
        i = pl.program_id(1)

        @pl.when(i == 0)
        def _():
            for r in (dwv_ref, dwg_ref, dbv_ref, dbg_ref):
                r[...] = jnp.zeros_like(r)

        keep_next = (i < nr - 1).astype(F32)

        def chunk(cc, carry):
            cols = pl.ds(pl.multiple_of(cc * LANE, LANE), LANE)

            def d_conv(rows, val, gate, d):
                sg, silu = _silu_parts(gate)
                dsv[rows, cols] = d * silu
                dsg[rows, cols] = d * val * (sg * (1.0 + gate * (1.0 - sg)))

            for r0 in range(0, tm, CONV_RH):
                rows = pl.ds(r0, CONV_RH)
                d_conv(rows, cv_ref[rows, cols].astype(F32), cg_ref[rows, cols].astype(F32), da_ref[rows, cols])
            d_conv(pl.ds(tm, HALO), ncv_ref[0:HALO, cols].astype(F32), ncg_ref[0:HALO, cols].astype(F32),
                   dan_ref[:, cols] * keep_next)
            for ds, u_ref, w_ref, du_ref, dw_ref, db_ref in ((dsv, uv_ref, wv_ref, duv_ref, dwv_ref, dbv_ref),
                                                            (dsg, ug_ref, wg_ref, dug_ref, dwg_ref, dbg_ref)):
                w = w_ref[:, cols]
                acc = [jnp.zeros((1, LANE), F32) for _ in range(4)]
                for r0 in range(0, tm, CONV_RH):
                    rows = pl.ds(r0, CONV_RH)
                    span = ds[pl.ds(r0, CONV_RH + HALO), cols]
                    d = [span[:CONV_RH]] + [pltpu.roll(span, CONV_RH + HALO - s, axis=0)[:CONV_RH] for s in (1, 2)]
                    du_ref[rows, cols] = (w[2:3] * d[0] + w[1:2] * d[1] + w[0:1] * d[2]).astype(du_ref.dtype)
                    u = u_ref[rows, cols]
                    for kk in range(3):
                        acc[kk] = acc[kk] + jnp.sum(d[2 - kk] * u, axis=0, keepdims=True)
                    acc[3] = acc[3] + jnp.sum(d[0], axis=0, keepdims=True)
                for kk in range(3):
                    dw_ref[kk:kk + 1, cols] += acc[kk]
                db_ref[:, cols] += acc[3]
            return carry

        lax.fori_loop(0, tc // LANE, chunk, 0)

    nxt = lambda i: jnp.minimum((i + 1) * hb, S // HALO - 1)
    tile_v = pl.BlockSpec((tm, tc), lambda j, i: (i, j))
    tile_g = pl.BlockSpec((tm, tc), lambda j, i: (i, j + nc))
    halo = pl.BlockSpec((HALO, tc), lambda j, i: (nxt(i), j))
    halo16 = pl.BlockSpec((2 * HALO, tc), lambda j, i: (jnp.minimum((i + 1) * (hb // 2), S // (2 * HALO) - 1), j))
    w_v = pl.BlockSpec((3, tc), lambda j, i: (0, j))
    w_g = pl.BlockSpec((3, tc), lambda j, i: (0, j + nc))
    b_v = pl.BlockSpec((1, tc), lambda j, i: (0, j))
    return pl.pallas_call(
        body, name=name, grid=(nc, nr),
        in_specs=[tile_v, tile_g, tile_v, tile_v, halo16, halo16, tile_v, halo, w_v, w_g],
        out_specs=(tile_v, tile_v, w_v, w_v, b_v, b_v),
        out_shape=(jax.ShapeDtypeStruct((S, D_FF), MXU), jax.ShapeDtypeStruct((S, D_FF), MXU),
                   jax.ShapeDtypeStruct((3, D_FF), F32), jax.ShapeDtypeStruct((3, D_FF), F32),
                   jax.ShapeDtypeStruct((1, D_FF), F32), jax.ShapeDtypeStruct((1, D_FF), F32)),
        scratch_shapes=[pltpu.VMEM((R, tc), F32), pltpu.VMEM((R, tc), F32)],
        compiler_params=_params(),
    )(up, up, cval, cgate, cval, cgate, dact, dact, cw, cw)


def _glu_fwd(z, h, *, name):
    S = z.shape[0]
    tm = _rows(S, 512)

    def body(z_ref, h_ref, o_ref):
        o_ref[...] = h_ref[...] + z_ref[:, :D_MODEL] * jax.nn.sigmoid(z_ref[:, D_MODEL:])

    return pl.pallas_call(
        body, name=name, grid=(S // tm,), in_specs=[_row_spec(tm, 2 * D_MODEL), _row_spec(tm, D_MODEL)],
        out_specs=_row_spec(tm, D_MODEL), out_shape=jax.ShapeDtypeStruct((S, D_MODEL), F32),
        compiler_params=_params(),
    )(z, h)


def _glu_bwd(z, dm, *, name):
    S = z.shape[0]
    tm = _rows(S, 512)

    def body(z_ref, dm_ref, o_ref):
        sg = jax.nn.sigmoid(z_ref[:, D_MODEL:])
        dmv = dm_ref[...]
        o_ref[:, :D_MODEL] = (dmv * sg).astype(o_ref.dtype)
        o_ref[:, D_MODEL:] = (dmv * z_ref[:, :D_MODEL] * sg * (1.0 - sg)).astype(o_ref.dtype)

    return pl.pallas_call(
        body, name=name, grid=(S // tm,), in_specs=[_row_spec(tm, 2 * D_MODEL), _row_spec(tm, D_MODEL)],
        out_specs=_row_spec(tm, 2 * D_MODEL), out_shape=jax.ShapeDtypeStruct((S, 2 * D_MODEL), MXU),
        compiler_params=_params(),
    )(z, dm)


GELU_C = math.sqrt(2.0 / math.pi)
GELU_A = 0.044715


def _gelu(y):
    return 0.5 * y * (1.0 + jnp.tanh(GELU_C * (y + GELU_A * (y * y * y))))


def _gelu_bwd(y, dg, *, name):
    S = y.shape[0]
    tm = _rows(S, 512)

    def body(y_ref, dg_ref, o_ref):
        yv = y_ref[...]
        t = jnp.tanh(GELU_C * (yv + GELU_A * (yv * yv * yv)))
        d = 0.5 * (1.0 + t) + 0.5 * yv * (1.0 - t * t) * (GELU_C * (1.0 + 3.0 * GELU_A * (yv * yv)))
        o_ref[...] = dg_ref[...] * d

    return pl.pallas_call(
        body, name=name, grid=(S // tm,), in_specs=[_row_spec(tm, D_MODEL), _row_spec(tm, D_MODEL)],
        out_specs=_row_spec(tm, D_MODEL), out_shape=jax.ShapeDtypeStruct((S, D_MODEL), F32),
        compiler_params=_params(),
    )(y, dg)


FWD_STRIPS = 4
BWD_STRIPS = 8
SW = NSTATE // SBLK
ST2 = 2 * NSTATE


def _s5_fwd(u, wb, wc, abc, dskip, x0, *, full, name):
    S = u.shape[0]
    T = _rows(S, 256 if full else 512)
    nb = S // T
    nj = T // NSEG

    def body(u_ref, wb_ref, wc_ref, a_ref, d_ref, x0_ref, *rest):
        if full:
            xs_ref, y_ref, yg_ref, st = rest
        else:
            e_ref, xs_ref, st = rest
        i = pl.program_id(0)

        @pl.when(i == 0)
        def _():
            st[...] = x0_ref[...]

        uv = u_ref[...]
        ub = uv.astype(MXU)
        for kb in range(SBLK):
            r = jnp.dot(ub[:, kb * LANE:(kb + 1) * LANE], wb_ref[kb], preferred_element_type=F32)
            xs_ref[:, kb * SW:(kb + 1) * SW] = r[:, :SW]
            xs_ref[:, NSTATE + kb * SW:NSTATE + (kb + 1) * SW] = r[:, SW:]
        for sp in range(FWD_STRIPS):
            w = NSTATE // FWD_STRIPS
            re, im = pl.ds(sp * w, w), pl.ds(NSTATE + sp * w, w)
            ar, ai = a_ref[:, re], a_ref[:, im]

            def step(j, c):
                xr, xi = c
                rows = pl.ds(pl.multiple_of(j * NSEG, NSEG), NSEG)
                nr = ar * xr - ai * xi + xs_ref[rows, re]
                ni = ar * xi + ai * xr + xs_ref[rows, im]
                xs_ref[rows, re] = nr
                xs_ref[rows, im] = ni
                return nr, ni

            xr, xi = lax.fori_loop(0, nj, step, (st[:, re], st[:, im]))
            st[:, re] = xr
            st[:, im] = xi
        if full:
            for kb in range(SBLK):
                yk = (jnp.dot(xs_ref[:, kb * SW:(kb + 1) * SW].astype(MXU), wc_ref[kb, :SW, :], preferred_element_type=F32)
                      + jnp.dot(xs_ref[:, NSTATE + kb * SW:NSTATE + (kb + 1) * SW].astype(MXU), wc_ref[kb, SW:, :],
                                preferred_element_type=F32))
                cols = slice(kb * LANE, (kb + 1) * LANE)
                yk = yk + d_ref[:, cols] * uv[:, cols]
                y_ref[:, cols] = yk
                yg_ref[:, cols] = _gelu(yk).astype(yg_ref.dtype)
        else:
            @pl.when(i == nb - 1)
            def _():
                e_ref[...] = st[...]

    in_specs = [_row_spec(T, D_MODEL), pl.BlockSpec((SBLK, LANE, 2 * SW), lambda i: (0, 0, 0)),
                pl.BlockSpec((SBLK, 2 * SW, LANE), lambda i: (0, 0, 0)), _const_spec(NSEG, ST2),
                _const_spec(1, D_MODEL), _const_spec(NSEG, ST2)]
    if full:
        out_specs = (_row_spec(T, ST2), _row_spec(T, D_MODEL), _row_spec(T, D_MODEL))
        out_shape = (jax.ShapeDtypeStruct((S, ST2), F32), jax.ShapeDtypeStruct((S, D_MODEL), F32),
                     jax.ShapeDtypeStruct((S, D_MODEL), MXU))
        scratch = [pltpu.VMEM((NSEG, ST2), F32)]
    else:
        out_specs = _const_spec(NSEG, ST2)
        out_shape = jax.ShapeDtypeStruct((NSEG, ST2), F32)
        scratch = [pltpu.VMEM((T, ST2), F32), pltpu.VMEM((NSEG, ST2), F32)]
    return pl.pallas_call(
        body, name=name, grid=(nb,), in_specs=in_specs, out_specs=out_specs, out_shape=out_shape,
        scratch_shapes=scratch, compiler_params=_params(big=True),
    )(u, wb, wc, abc, dskip, x0)


def _s5_bwd(dy, xs, u, wct, wbt, abc, dskip, x0, l0, *, full, name):
    S = dy.shape[0]
    T = _rows(S, 256 if full else 512)
    nb = S // T
    nj = T // NSEG
    blk = lambda i: nb - 1 - i

    def body(dy_ref, *rest):
        if full:
            (xs_ref, xh_ref, u_ref, wct_ref, wbt_ref, a_ref, d_ref, x0_ref, l0_ref,
             du_ref, da_ref, dwb_hbm, dwc_hbm, dd_ref, g_s, lam_s, dwb_ref, dwc_ref) = rest
        else:
            wct_ref, a_ref, l0_ref, f_ref, g_s, lam_s = rest
        i = pl.program_id(0)

        @pl.when(i == 0)
        def _():
            lam_s[...] = l0_ref[...]
            if full:
                for r in (da_ref, dwb_ref, dwc_ref, dd_ref):
                    r[...] = jnp.zeros_like(r)

        dyv = dy_ref[...]
        dyb = dyv.astype(MXU)
        for kb in range(SBLK):
            r = jnp.dot(dyb[:, kb * LANE:(kb + 1) * LANE], wct_ref[kb], preferred_element_type=F32)
            g_s[:, kb * SW:(kb + 1) * SW] = r[:, :SW]
            g_s[:, NSTATE + kb * SW:NSTATE + (kb + 1) * SW] = r[:, SW:]
        for sp in range(BWD_STRIPS):
            w = NSTATE // BWD_STRIPS
            re, im = pl.ds(sp * w, w), pl.ds(NSTATE + sp * w, w)
            ar, ai = a_ref[:, re], a_ref[:, im]

            def advance(row, lr, li):
                rows = pl.ds(row, NSEG)
                nr = g_s[rows, re] + ar * lr + ai * li
                ni = g_s[rows, im] - ai * lr + ar * li
                g_s[rows, re] = nr
                g_s[rows, im] = ni
                return nr, ni

            def step(jj, c):
                row = pl.multiple_of((nj - 1 - jj) * NSEG, NSEG)
                nr, ni = advance(row, c[0], c[1])
                if not full:
                    return nr, ni
                prow = pl.ds(pl.multiple_of(row - NSEG, NSEG), NSEG)
                xpr, xpi = xs_ref[prow, re], xs_ref[prow, im]
                return nr, ni, c[2] + (nr * xpr + ni * xpi), c[3] + (ni * xpr - nr * xpi)

            init = (lam_s[:, re], lam_s[:, im])
            if full:
                init = init + (jnp.zeros((NSEG, w), F32), jnp.zeros((NSEG, w), F32))
            c = lax.fori_loop(0, nj - 1, step, init)
            lr, li = advance(0, c[0], c[1])
            if full:
                first = (blk(i) == 0)
                xpr = jnp.where(first, x0_ref[:, re], xh_ref[:, re])
                xpi = jnp.where(first, x0_ref[:, im], xh_ref[:, im])
                da_ref[:, re] += c[2] + (lr * xpr + li * xpi)
                da_ref[:, im] += c[3] + (li * xpr - lr * xpi)
            lam_s[:, re] = lr
            lam_s[:, im] = li
        if full:
            uv = u_ref[...]
            ub = uv.astype(MXU)
            dd_ref[...] += jnp.sum(dyv * uv, axis=0, keepdims=True)
            for kb in range(SBLK):
                cols = slice(kb * LANE, (kb + 1) * LANE)
                re = slice(kb * SW, (kb + 1) * SW)
                im = slice(NSTATE + kb * SW, NSTATE + (kb + 1) * SW)
                lr_b = g_s[:, re].astype(MXU)
                li_b = g_s[:, im].astype(MXU)
                duk = (jnp.dot(lr_b, wbt_ref[kb, :SW, :], preferred_element_type=F32)
                       + jnp.dot(li_b, wbt_ref[kb, SW:, :], preferred_element_type=F32))
                du_ref[:, cols] = duk + d_ref[:, cols] * dyv[:, cols]
                dwb_ref[kb, :, :SW] += _dot_tn(ub[:, cols], lr_b)
                dwb_ref[kb, :, SW:] += _dot_tn(ub[:, cols], li_b)
                dwc_ref[kb, :SW, :] += _dot_tn(xs_ref[:, re].astype(MXU), dyb[:, cols])
                dwc_ref[kb, SW:, :] += _dot_tn(xs_ref[:, im].astype(MXU), dyb[:, cols])

            @pl.when(i == nb - 1)
            def _():
                pltpu.sync_copy(dwb_ref, dwb_hbm)
                pltpu.sync_copy(dwc_ref, dwc_hbm)
        else:
            @pl.when(i == nb - 1)
            def _():
                f_ref[...] = lam_s[...]

    rev = lambda c: pl.BlockSpec((T, c), lambda i: (blk(i), 0))
    w3 = lambda a, b: pl.BlockSpec((SBLK, a, b), lambda i: (0, 0, 0))
    if full:
        hb = T // NSEG
        in_specs = [rev(D_MODEL), rev(ST2),
                    pl.BlockSpec((NSEG, ST2), lambda i: (jnp.maximum(blk(i) * hb - 1, 0), 0)),
                    rev(D_MODEL), w3(LANE, 2 * SW), w3(2 * SW, LANE), _const_spec(NSEG, ST2),
                    _const_spec(1, D_MODEL), _const_spec(NSEG, ST2), _const_spec(NSEG, ST2)]
        ops = [dy, xs, xs, u, wct, wbt, abc, dskip, x0, l0]
        hbm = pl.BlockSpec(memory_space=pl.ANY)
        out_specs = (rev(D_MODEL), _const_spec(NSEG, ST2), hbm, hbm, _const_spec(1, D_MODEL))
        out_shape = (jax.ShapeDtypeStruct((S, D_MODEL), F32), jax.ShapeDtypeStruct((NSEG, ST2), F32),
                     jax.ShapeDtypeStruct((SBLK, LANE, 2 * SW), F32), jax.ShapeDtypeStruct((SBLK, 2 * SW, LANE), F32),
                     jax.ShapeDtypeStruct((1, D_MODEL), F32))
    else:
        in_specs = [rev(D_MODEL), w3(LANE, 2 * SW), _const_spec(NSEG, ST2), _const_spec(NSEG, ST2)]
        ops = [dy, wct, abc, l0]
        out_specs = _const_spec(NSEG, ST2)
        out_shape = jax.ShapeDtypeStruct((NSEG, ST2), F32)
    return pl.pallas_call(
        body, name=name, grid=(nb,), in_specs=in_specs, out_specs=out_specs, out_shape=out_shape,
        scratch_shapes=[pltpu.VMEM((T, ST2), F32), pltpu.VMEM((NSEG, ST2), F32)]
        + ([pltpu.VMEM((SBLK, LANE, 2 * SW), F32), pltpu.VMEM((SBLK, 2 * SW, LANE), F32)] if full else []),
        compiler_params=_params(big=True),
    )(*ops)


def _s5_discretize(lr, li, log_dt, br, bi):
    dt = jnp.exp(log_dt)[:, None]
    mag = jnp.exp(lr * dt)
    ar = mag * jnp.cos(li * dt)
    ai = mag * jnp.sin(li * dt)
    den = lr * lr + li * li
    nr = ar - 1.0
    coef_r = (nr * lr + ai * li) / den
    coef_i = (ai * lr - nr * li) / den
    bbar_r = coef_r[..., None] * br - coef_i[..., None] * bi
    bbar_i = coef_r[..., None] * bi + coef_i[..., None] * br
    return ar, ai, bbar_r, bbar_i


def _blockdiag(m):
    gpb = SSM_GROUPS // SBLK
    a, b = m.shape[1:]
    mb = m.reshape(SBLK, gpb, a, b)
    eye = jnp.eye(gpb, dtype=m.dtype)
    return jnp.einsum('kgab,gh->kgahb', mb, eye).reshape(SBLK, gpb * a, gpb * b)


def _blockdiag_extract(w, a, b):
    gpb = SSM_GROUPS // SBLK
    w5 = w.reshape(SBLK, gpb, a, gpb, b)
    return jnp.einsum('kgahb,gh->kgab', w5, jnp.eye(gpb, dtype=w.dtype)).reshape(SSM_GROUPS, a, b)


def _cpow(ar, ai, n):
    rr, ri = jnp.ones_like(ar), jnp.zeros_like(ai)
    br, bi = ar, ai
    while n:
        if n & 1:
            rr, ri = rr * br - ri * bi, rr * bi + ri * br
        br, bi = br * br - bi * bi, 2.0 * br * bi
        n >>= 1
    return rr, ri


def _perm(a):
    s, c = a.shape
    return a.reshape(NSEG, s // NSEG, c).transpose(1, 0, 2).reshape(s, c)


def _unperm(a):
    s, c = a.shape
    return a.reshape(s // NSEG, NSEG, c).transpose(1, 0, 2).reshape(s, c)


def _other_chips(x, y):
    return [(1 - x, y), (x, 1 - y), (1 - x, 1 - y)]


def _span(chip, size, align):
    return pl.ds(pl.multiple_of(chip * size, align), size)


def _exchange_ops(pieces, s_refs, o_refs, send_sems, recv_sems, local_sems):
    x, y, c = lax.axis_index("x"), lax.axis_index("y"), lax.axis_index("c")
    me = 2 * x + y
    others = _other_chips(x, y)
    npc = len(pieces)

    def remote(k, p, tx, ty, src_chip, dst_chip):
        si, oi, sv, dv = pieces[p]
        return pltpu.make_async_remote_copy(
            src_ref=sv(s_refs[si], src_chip), dst_ref=dv(o_refs[oi], dst_chip), send_sem=send_sems.at[k, p],
            recv_sem=recv_sems.at[k, p], device_id=(tx, ty, c), device_id_type=MESH)

    def local(p):
        si, oi, sv, dv = pieces[p]
        return pltpu.make_async_copy(sv(s_refs[si], me), dv(o_refs[oi], me), local_sems.at[p])

    def start():
        for p in range(npc):
            local(p).start()
        for k, (tx, ty) in enumerate(others):
            for p in range(npc):
                remote(k, p, tx, ty, 2 * tx + ty, me).start()

    def finish():
        for k, (tx, ty) in enumerate(others):
            for p in range(npc):
                remote(k, p, tx, ty, me, 2 * tx + ty).wait_recv()
        for k, (tx, ty) in enumerate(others):
            for p in range(npc):
                remote(k, p, tx, ty, 2 * tx + ty, me).wait_send()
        for p in range(npc):
            local(p).wait()

    return start, finish


def _exchange_scratch(npc):
    return [pltpu.SemaphoreType.DMA((NCHIP - 1, npc)), pltpu.SemaphoreType.DMA((NCHIP - 1, npc)),
            pltpu.SemaphoreType.DMA((npc,))]


def _chip_exchange(srcs, out_shapes, pieces, *, name):
    ns, no = len(srcs), len(out_shapes)

    def body(*refs):
        start, finish = _exchange_ops(pieces, refs[:ns], refs[ns:ns + no], *refs[ns + no:])
        start()
        finish()

    hbm = pl.BlockSpec(memory_space=pl.ANY)
    return pl.pallas_call(
        body, name=name, in_specs=[hbm] * ns, out_specs=tuple([hbm] * no), out_shape=tuple(out_shapes),
        scratch_shapes=_exchange_scratch(len(pieces)),
    )(*srcs)


def _sibling_exchange(srcs, *, name):
    n = len(srcs)

    def body(*refs):
        s_refs, o_refs, send_sems, recv_sems = refs[:n], refs[n:2 * n], refs[2 * n], refs[2 * n + 1]
        x, y, c = lax.axis_index("x"), lax.axis_index("y"), lax.axis_index("c")
        cps = [pltpu.make_async_remote_copy(src_ref=s_refs[p], dst_ref=o_refs[p], send_sem=send_sems.at[p],
                                            recv_sem=recv_sems.at[p], device_id=(x, y, 1 - c), device_id_type=MESH)
               for p in range(n)]
        for cp in cps:
            cp.start()
        for cp in cps:
            cp.wait()

    hbm = pl.BlockSpec(memory_space=pl.ANY)
    return pl.pallas_call(
        body, name=name, in_specs=[hbm] * n, out_specs=tuple([hbm] * n),
        out_shape=tuple(jax.ShapeDtypeStruct(s.shape, s.dtype) for s in srcs),
        scratch_shapes=[pltpu.SemaphoreType.DMA((n,)), pltpu.SemaphoreType.DMA((n,))],
    )(*srcs)


def _row_tile(r, c, tile_bytes):
    best = 16
    for t in range(16, r + 1, 16):
        if r % t == 0 and t * c * 4 <= tile_bytes:
            best = t
    assert r % best == 0
    return best


def _sum_chips(r, *, name):
    _, R, W = r.shape
    tm = _row_tile(R, W, 2 * 1024 * 1024)

    def body(r_ref, o_ref):
        o_ref[...] = ((r_ref[0].astype(F32) + r_ref[1].astype(F32)) + r_ref[2].astype(F32)) + r_ref[3].astype(F32)

    return pl.pallas_call(
        body, name=name, grid=(R // tm,), in_specs=[pl.BlockSpec((NCHIP, tm, W), lambda i: (0, i, 0))],
        out_specs=_row_spec(tm, W), out_shape=jax.ShapeDtypeStruct((R, W), F32), compiler_params=_params(),
    )(r)


def _adamw(p_mine, p_sib, w, m, v, *, name):
    R, W = w.shape
    tm = _row_tile(R, W, 1024 * 1024)

    def body(a_ref, b_ref, w_ref, m_ref, v_ref, g_ref, d_ref, nm_ref, nv_ref):
        g = a_ref[...] + b_ref[...]
        mm = ADAM_B1 * m_ref[...] + (1.0 - ADAM_B1) * g
        vv = ADAM_B2 * v_ref[...] + (1.0 - ADAM_B2) * (g * g)
        m_hat = mm / (1.0 - ADAM_B1 ** ADAM_STEP)
        v_hat = vv / (1.0 - ADAM_B2 ** ADAM_STEP)
        g_ref[...] = g
        d_ref[...] = -ADAM_LR * (m_hat / (jnp.sqrt(v_hat) + ADAM_EPS) + ADAM_WD * w_ref[...])
        nm_ref[...] = mm
        nv_ref[...] = vv

    spec = _row_spec(tm, W)
    shp = jax.ShapeDtypeStruct((R, W), F32)
    return pl.pallas_call(
        body, name=name, grid=(R // tm,), in_specs=[spec] * 5, out_specs=(spec,) * 4, out_shape=(shp,) * 4,
        compiler_params=_params(),
    )(p_mine, p_sib, w, m, v)


def _pack_small(parts):
    flat = jnp.concatenate([p.reshape(-1) for p in parts])
    pad = (-flat.shape[0]) % (SMALL_ROWS * PACKW)
    return jnp.pad(flat, (0, pad)).reshape(-1, PACKW)


def _unpack_small(buf, shapes):
    flat, out, off = buf.reshape(-1), [], 0
    for shp in shapes:
        sz = math.prod(shp)
        out.append(flat[off:off + sz].reshape(shp))
        off += sz
    return out


def _shard(a, t, ax):
    sz = a.shape[ax] // NCHIP
    return lax.slice_in_dim(a, t * sz, (t + 1) * sz, axis=ax)


MLA_W = ['mla_w_a', 'mla_w_uq', 'mla_w_ukv', 'mla_w_o']
REST_W = ['ssm_w_in', 'ssm_w_glu', 'ffn_w_up', 'ffn_w_down']
SMALL = [n for n in WNAMES if n not in MLA_W + REST_W]


def _gather_plan(w, names, with_small):
    srcs, outs, pieces = [], [], []
    for name in names:
        local = w[name].astype(MXU)
        local = local[0] if local.shape[0] == 1 else local
        ax = SHARD_AXIS[name] - (1 if w[name].shape[0] == 1 else 0)
        full = local.shape[:ax] + (NCHIP * local.shape[ax],) + local.shape[ax + 1:]
        si, oi = len(srcs), len(outs)
        srcs.append(local)
        outs.append(jax.ShapeDtypeStruct(full, MXU))
        size = local.shape[ax]
        if local.ndim == 2:
            if ax == 0:
                pieces.append((si, oi, lambda r, t: r, lambda r, ch, size=size: r.at[_span(ch, size, 8), :]))
            else:
                pieces.append((si, oi, lambda r, t: r, lambda r, ch, size=size: r.at[:, _span(ch, size, LANE)]))
        else:
            for l in range(local.shape[0]):
                if ax == 1:
                    dv = lambda r, ch, l=l, size=size: r.at[l, _span(ch, size, 8), :]
                else:
                    dv = lambda r, ch, l=l, size=size: r.at[l, :, _span(ch, size, LANE)]
                pieces.append((si, oi, lambda r, t, l=l: r.at[l], dv))
    if with_small:
        small = _pack_small([w[n] for n in GATHER_F32])
        srcs.append(small)
        outs.append(jax.ShapeDtypeStruct((NCHIP,) + small.shape, F32))
        pieces.append((len(srcs) - 1, len(outs) - 1, lambda r, t: r, lambda r, ch: r.at[ch]))
    return srcs, outs, pieces


def _gather_result(got, w, names, with_small):
    full = dict(zip(names, got[:len(names)]))
    if with_small:
        per_chip = [_unpack_small(got[-1][t], [w[n].shape for n in GATHER_F32]) for t in range(NCHIP)]
        for j, n in enumerate(GATHER_F32):
            full[n] = jnp.concatenate([per_chip[t][j] for t in range(NCHIP)], axis=SHARD_AXIS[n])
    return full


def _as_rows(a):
    return a.reshape(-1, a.shape[-1])


def _grad_plan(grads, w, names, with_small):
    srcs, outs, pieces = [], [], []
    for name in names:
        local = w[name]
        ax = SHARD_AXIS[name]
        size = local.shape[ax]
        oi = len(outs)
        layers = grads[name] if isinstance(grads[name], list) else [grads[name]]
        outs.append(jax.ShapeDtypeStruct((NCHIP,) + local.shape, layers[0].dtype))
        for l, g in enumerate(layers):
            si = len(srcs)
            srcs.append(g)
            if ax == 1:
                sv = lambda r, t, size=size: r.at[_span(t, size, 8), :]
            else:
                sv = lambda r, t, size=size: r.at[:, _span(t, size, LANE)]
            pieces.append((si, oi, sv, lambda r, ch, l=l: r.at[ch, l]))
    if with_small:
        small = jnp.stack([_pack_small([_shard(grads[n], t, SHARD_AXIS[n]) if n in SHARD_AXIS else grads[n]
                                        for n in SMALL]) for t in range(NCHIP)])
        srcs.append(small)
        outs.append(jax.ShapeDtypeStruct(small.shape, F32))
        pieces.append((len(srcs) - 1, len(outs) - 1, lambda r, t: r.at[t], lambda r, ch: r.at[ch]))
    return srcs, outs, pieces


def _reduce_and_update(names, landed, w, mom, var):
    partial = [_sum_chips(r.reshape(NCHIP, -1, r.shape[-1]), name="grad_sum_chips") for r in landed]
    sibling = _sibling_exchange(partial, name="grad_sibling")
    res = [dict(), dict(), dict(), dict()]
    for j, name in enumerate(names):
        outs4 = _adamw(partial[j], sibling[j], _as_rows(w[name]), _as_rows(mom[name]), _as_rows(var[name]), name="adamw")
        for d, o in zip(res, outs4):
            d[name] = o.reshape(w[name].shape)
    outs4 = _adamw(partial[-1], sibling[-1], *[_pack_small([t[n] for n in SMALL]) for t in (w, mom, var)], name="adamw")
    for d, o in zip(res, outs4):
        d.update(zip(SMALL, _unpack_small(o, [w[n].shape for n in SMALL])))
    return res


def kernel(x, positions, mla_w_a, mla_g_q, mla_g_kv, mla_w_uq, mla_w_ukv, mla_w_o, ssm_w_in, ssm_lambda_re, ssm_lambda_im, ssm_log_dt, ssm_b_re, ssm_b_im, ssm_c_re, ssm_c_im, ssm_d, ssm_w_glu, ffn_w_up, ffn_conv_w, ffn_conv_b, ffn_w_down, g_mix, g_ffn, g_final, loss_target, m_mla_w_a, m_mla_g_q, m_mla_g_kv, m_mla_w_uq, m_mla_w_ukv, m_mla_w_o, m_ssm_w_in, m_ssm_lambda_re, m_ssm_lambda_im, m_ssm_log_dt, m_ssm_b_re, m_ssm_b_im, m_ssm_c_re, m_ssm_c_im, m_ssm_d, m_ssm_w_glu, m_ffn_w_up, m_ffn_conv_w, m_ffn_conv_b, m_ffn_w_down, m_g_mix, m_g_ffn, m_g_final, v_mla_w_a, v_mla_g_q, v_mla_g_kv, v_mla_w_uq, v_mla_w_ukv, v_mla_w_o, v_ssm_w_in, v_ssm_lambda_re, v_ssm_lambda_im, v_ssm_log_dt, v_ssm_b_re, v_ssm_b_im, v_ssm_c_re, v_ssm_c_im, v_ssm_d, v_ssm_w_glu, v_ffn_w_up, v_ffn_conv_w, v_ffn_conv_b, v_ffn_w_down, v_g_mix, v_g_ffn, v_g_final):
    w = dict(zip(WNAMES, (mla_w_a, mla_g_q, mla_g_kv, mla_w_uq, mla_w_ukv, mla_w_o, ssm_w_in, ssm_lambda_re,
                          ssm_lambda_im, ssm_log_dt, ssm_b_re, ssm_b_im, ssm_c_re, ssm_c_im, ssm_d, ssm_w_glu,
                          ffn_w_up, ffn_conv_w, ffn_conv_b, ffn_w_down, g_mix, g_ffn, g_final)))
    mom = dict(zip(WNAMES, (m_mla_w_a, m_mla_g_q, m_mla_g_kv, m_mla_w_uq, m_mla_w_ukv, m_mla_w_o, m_ssm_w_in,
                            m_ssm_lambda_re, m_ssm_lambda_im, m_ssm_log_dt, m_ssm_b_re, m_ssm_b_im, m_ssm_c_re,
                            m_ssm_c_im, m_ssm_d, m_ssm_w_glu, m_ffn_w_up, m_ffn_conv_w, m_ffn_conv_b,
                            m_ffn_w_down, m_g_mix, m_g_ffn, m_g_final)))
    var = dict(zip(WNAMES, (v_mla_w_a, v_mla_g_q, v_mla_g_kv, v_mla_w_uq, v_mla_w_ukv, v_mla_w_o, v_ssm_w_in,
                            v_ssm_lambda_re, v_ssm_lambda_im, v_ssm_log_dt, v_ssm_b_re, v_ssm_b_im, v_ssm_c_re,
                            v_ssm_c_im, v_ssm_d, v_ssm_w_glu, v_ffn_w_up, v_ffn_conv_w, v_ffn_conv_b,
                            v_ffn_w_down, v_g_mix, v_g_ffn, v_g_final)))
    S = x.shape[1]
    D = D_MODEL
    x2 = x.reshape(S, D)
    tgt = loss_target.reshape(S, D)

    fw = _gather_result(_chip_exchange(*_gather_plan(w, MLA_W, False), name="gather_weights"), w, MLA_W, False)
    w_a = jnp.pad(fw['mla_w_a'], ((0, 0), (0, A_PAD - KR0 - QK_ROPE)))
    uq = fw['mla_w_uq'].reshape(Q_LORA, HEADS, QK_DIM)
    w_uq = jnp.concatenate([uq[:, :, :QK_NOPE].reshape(Q_LORA, HEADS * QK_NOPE),
                            jnp.pad(uq[:, :, QK_NOPE:], ((0, 0), (0, 0), (0, LANE - QK_ROPE))).reshape(Q_LORA, HEADS * LANE)],
                           axis=1)
    w_ukv = fw['mla_w_ukv']
    w_o = fw['mla_w_o']
    conv_b = w['ffn_conv_b']
    g_q, g_kv = w['mla_g_q'], w['mla_g_kv']
    gm, gf = w['g_mix'], w['g_ffn']
    gfin = w['g_final'].reshape(1, D)

    inv = 1.0 / (ROPE_THETA ** (jnp.arange(0, QK_ROPE, 2, dtype=F32) / QK_ROPE))
    ang = positions.reshape(S).astype(F32)[:, None] * inv
    cos, sin = jnp.cos(ang), jnp.sin(ang)
    zpad = jnp.zeros((S, LANE - QK_ROPE), F32)
    c128 = jnp.concatenate([cos, cos, zpad], axis=1)
    s128 = jnp.concatenate([-sin, sin, zpad], axis=1)

    hn0 = _rmsnorm_fwd(x2, gm[0:1], name="rms_mix0")
    a = _mm(hn0, w_a, name="mla_a")
    cqn, ckvn, kr = _mla_mid_fwd(a, g_q, g_kv, c128, s128, name="mla_mid_fwd")
    qfull = _mm(cqn, w_uq, name="mla_q")
    kv = _mm(ckvn, w_ukv, out_dtype=MXU, name="mla_kv")
    qs, ks, vs = _qk_prep(qfull, kv, kr, c128, s128, name="qk_prep")
    os_, lses = [], []
    rides = {0: (['ssm_w_in', 'ssm_w_glu', 'ffn_w_down'], True), 1: (['ffn_w_up'], False)}
    for h in range(HEADS):
        if h in rides:
            o_h, lse_h, *got = _attn_fwd(qs[h], ks[h], vs[h], name="attn_fwd_gather", ride=_gather_plan(w, *rides[h]))
            fw.update(_gather_result(got, w, *rides[h]))
        else:
            o_h, lse_h = _attn_fwd(qs[h], ks[h], vs[h], name="attn_fwd")
        os_.append(o_h)
        lses.append(lse_h)
    w_in = fw['ssm_w_in']
    w_glu = fw['ssm_w_glu']
    w_up = fw['ffn_w_up']
    w_down = fw['ffn_w_down']
    conv_w = fw['ffn_conv_w']
    dskip = fw['ssm_d']
    o_cat = jnp.concatenate(os_, axis=1)
    h1 = _mm(o_cat, w_o, res=x2, name="mla_o")

    def ffn_fwd(h, l):
        hn = _rmsnorm_fwd(h, gf[l:l + 1], name="rms_ffn")
        up = _mm(hn, w_up[l], name="ffn_up")
        act, cval, cgate = _convgate_fwd(up, conv_w[l], conv_b[l:l + 1], name="convgate_fwd")
        return _mm(act, w_down[l], res=h, name="ffn_down"), (hn, up, act, cval, cgate)

    h2, saved0 = ffn_fwd(h1, 0)

    lam_re, lam_im, log_dt = w['ssm_lambda_re'][0], w['ssm_lambda_im'][0], w['ssm_log_dt'][0]
    (a_re, a_im, bbar_r, bbar_i), disc_vjp = jax.vjp(_s5_discretize, lam_re, lam_im, log_dt, w['ssm_b_re'][0],
                                                    w['ssm_b_im'][0])
    c_re, c_im = w['ssm_c_re'][0], w['ssm_c_im'][0]
    bt_r, bt_i = jnp.swapaxes(bbar_r, 1, 2), jnp.swapaxes(bbar_i, 1, 2)
    wb = jnp.concatenate([_blockdiag(bt_r), _blockdiag(bt_i)], axis=2).astype(MXU)
    wbt = jnp.concatenate([_blockdiag(bbar_r), _blockdiag(bbar_i)], axis=1).astype(MXU)
    ct_r, ct_i = jnp.swapaxes(c_re, 1, 2), jnp.swapaxes(c_im, 1, 2)
    wc = jnp.concatenate([_blockdiag(ct_r), _blockdiag(-ct_i)], axis=1).astype(MXU)
    wct = jnp.concatenate([_blockdiag(c_re), _blockdiag(-c_im)], axis=2).astype(MXU)
    af_r, af_i = a_re.reshape(NSTATE), a_im.reshape(NSTATE)
    abc = jnp.broadcast_to(jnp.concatenate([af_r, af_i])[None], (NSEG, ST2))
    seg = S // NSEG
    ap_r, ap_i = _cpow(af_r, af_i, seg)

    hn1 = _rmsnorm_fwd(h2, gm[1:2], name="rms_mix1")
    u = _mm(hn1, w_in, name="s5_in")
    u_p = _perm(u)
    zero_state = jnp.zeros((NSEG, ST2), F32)
    ends = _s5_fwd(u_p, wb, wc, abc, dskip, zero_state, full=False, name="s5_fwd_ends")
    inits, cr, ci = [], jnp.zeros((NSTATE,), F32), jnp.zeros((NSTATE,), F32)
    for r in range(NSEG):
        inits.append(jnp.concatenate([cr, ci]))
        er, ei = ends[r, :NSTATE], ends[r, NSTATE:]
        cr, ci = er + ap_r * cr - ap_i * ci, ei + ap_r * ci + ap_i * cr
    x0 = jnp.stack(inits)
    xs, y_p, yg_p = _s5_fwd(u_p, wb, wc, abc, dskip, x0, full=True, name="s5_fwd")
    yg = _unperm(yg_p)
    z = _mm(yg, w_glu, name="s5_glu")
    h3 = _glu_fwd(z, h2, name="glu_fwd")
    h4, saved1 = ffn_fwd(h3, 1)

    loss_l, dh4, dg_final = _loss_head(h4, gfin, tgt, name="loss_head")

    grads = {}

    def ffn_bwd(h_in, g, saved, l):
        hn, up, act, cval, cgate = saved
        dact = _mm(g, w_down[l], mode="nt", name="ffn_down_dx")
        dw_down = _mm(act, g, mode="tn", out_dtype=MXU, name="ffn_down_dw")
        duv, dug, dwv, dwg, dbv, dbg = _convgate_bwd(up, cval, cgate, dact, conv_w[l], name="convgate_bwd")
        dw_up = jnp.concatenate([_mm(hn, duv, mode="tn", out_dtype=MXU, name="ffn_up_dw"), _mm(hn, dug, mode="tn", out_dtype=MXU, name="ffn_up_dw")],
                                axis=1)
        dhn = _mm(duv, w_up[l][:, :D_FF], mode="nt", name="ffn_up_dx")
        dhn = _mm(dug, w_up[l][:, D_FF:], mode="nt", res=dhn, name="ffn_up_dx_acc")
        dh, dg = _rmsnorm_bwd(h_in, gf[l:l + 1], dhn, g, name="rms_ffn_bwd")
        return dh, dict(w_up=dw_up, w_down=dw_down, conv_w=jnp.concatenate([dwv, dwg], axis=1),
                        conv_b=jnp.concatenate([dbv, dbg], axis=1)[0], g_ffn=dg[0])

    dh3, fg1 = ffn_bwd(h3, dh4, saved1, 1)

    dz = _glu_bwd(z, dh3, name="glu_bwd")
    grads['ssm_w_glu'] = _mm(yg, dz, mode="tn", out_dtype=MXU, name="s5_glu_dw")
    dyg = _mm(dz, w_glu, mode="nt", name="s5_glu_dx")
    dy_p = _gelu_bwd(y_p, _perm(dyg), name="gelu_bwd")
    firsts = _s5_bwd(dy_p, None, None, wct, None, abc, None, None, zero_state, full=False, name="s5_bwd_firsts")
    linits, cr, ci = [None] * NSEG, jnp.zeros((NSTATE,), F32), jnp.zeros((NSTATE,), F32)
    for r in reversed(range(NSEG)):
        linits[r] = jnp.concatenate([cr, ci])
        fr, fi = firsts[r, :NSTATE], firsts[r, NSTATE:]
        cr, ci = fr + ap_r * cr + ap_i * ci, fi + ap_r * ci - ap_i * cr
    l0 = jnp.stack(linits)
    du_p, dab, dwb, dwc, dd = _s5_bwd(dy_p, xs, u_p, wct, wbt, abc, dskip, x0, l0, full=True, name="s5_bwd")
    du = _unperm(du_p)
    grads['ssm_w_in'] = _mm(hn1, du, mode="tn", out_dtype=MXU, name="s5_in_dw")
    dhn1 = _mm(du, w_in, mode="nt", name="s5_in_dx")
    dh2, dg_mix1 = _rmsnorm_bwd(h2, gm[1:2], dhn1, dh3, name="rms_mix_bwd")
    da_sum = jnp.sum(dab, axis=0)
    dbt_r = _blockdiag_extract(dwb[:, :, :SW], SSM_GROUP, SSM_STATE)
    dbt_i = _blockdiag_extract(dwb[:, :, SW:], SSM_GROUP, SSM_STATE)
    dlr, dli, dlog_dt, dbr, dbi = disc_vjp((da_sum[:NSTATE].reshape(SSM_GROUPS, SSM_STATE),
                                            da_sum[NSTATE:].reshape(SSM_GROUPS, SSM_STATE),
                                            jnp.swapaxes(dbt_r, 1, 2), jnp.swapaxes(dbt_i, 1, 2)))
    dct_r = _blockdiag_extract(dwc[:, :SW, :], SSM_STATE, SSM_GROUP)
    dct_i = _blockdiag_extract(dwc[:, SW:, :], SSM_STATE, SSM_GROUP)
    grads['ssm_lambda_re'], grads['ssm_lambda_im'], grads['ssm_log_dt'] = dlr[None], dli[None], dlog_dt[None]
    grads['ssm_b_re'], grads['ssm_b_im'] = dbr[None], dbi[None]
    grads['ssm_c_re'] = jnp.swapaxes(dct_r, 1, 2)[None]
    grads['ssm_c_im'] = -jnp.swapaxes(dct_i, 1, 2)[None]
    grads['ssm_d'] = dd

    dh1, fg0 = ffn_bwd(h1, dh2, saved0, 0)
    grads['ffn_w_up'] = [fg0['w_up'], fg1['w_up']]
    grads['ffn_w_down'] = [fg0['w_down'], fg1['w_down']]
    grads['ffn_conv_w'] = jnp.stack([fg0['conv_w'], fg1['conv_w']])
    grads['ffn_conv_b'] = jnp.stack([fg0['conv_b'], fg1['conv_b']])
    grads['g_ffn'] = jnp.stack([fg0['g_ffn'], fg1['g_ffn']])

    do_cat = _mm(dh1, w_o, mode="nt", out_dtype=MXU, name="mla_o_dx")
    grads['mla_w_o'] = _mm(o_cat, dh1, mode="tn", out_dtype=MXU, name="mla_o_dw")
    dqs, dks, dvs = [], [], []
    deltas = _attn_delta(do_cat, o_cat, name="attn_delta")
    for h in range(HEADS):
        do_h = do_cat[:, h * V_HEAD:(h + 1) * V_HEAD]
        delta = deltas[:, h]
        tiles = (S // _rows(S, BWD_T), 1, _rows(S, BWD_T))
        lse2 = (lses[h] * math.log2(math.e)).reshape(tiles)
        if h == 0:
            dq_h, dk_h, dv_h, *landed = _attn_bwd(qs[h], ks[h], vs[h], do_h, lse2, delta.reshape(tiles),
                                                  name="attn_bwd_exchange", ride=_grad_plan(grads, w, REST_W, False))
        else:
            dq_h, dk_h, dv_h = _attn_bwd(qs[h], ks[h], vs[h], do_h, lse2, delta.reshape(tiles), name="attn_bwd")
        dqs.append(dq_h)
        dks.append(dk_h)
        dvs.append(dv_h)
    dqfull, dkv, dkr = _qk_prep_bwd(dqs, dks, dvs, c128, s128, name="qk_prep_bwd")
    dw_uq_p = _mm(cqn, dqfull, mode="tn", out_dtype=MXU, name="mla_q_dw")
    dcqn = _mm(dqfull, w_uq, mode="nt", name="mla_q_dx")
    grads['mla_w_ukv'] = _mm(ckvn, dkv, mode="tn", out_dtype=MXU, name="mla_kv_dw")
    dckvn = _mm(dkv, w_ukv, mode="nt", name="mla_kv_dx")
    da, dgq, dgkv = _mla_mid_bwd(a, dcqn, dckvn, dkr, g_q, g_kv, c128, s128, name="mla_mid_bwd")
    grads['mla_w_a'] = _mm(hn0, da, mode="tn", out_dtype=MXU, name="mla_a_dw")[:, :KR0 + QK_ROPE]
    dhn0 = _mm(da, w_a, mode="nt", name="mla_a_dx")
    dx, dg_mix0 = _rmsnorm_bwd(x2, gm[0:1], dhn0, dh1, name="rms_mix_bwd")
    grads['mla_w_uq'] = jnp.concatenate(
        [dw_uq_p[:, :HEADS * QK_NOPE].reshape(Q_LORA, HEADS, QK_NOPE),
         dw_uq_p[:, HEADS * QK_NOPE:].reshape(Q_LORA, HEADS, LANE)[:, :, :QK_ROPE]], axis=2).reshape(Q_LORA, HEADS * QK_DIM)
    grads['mla_g_q'], grads['mla_g_kv'] = dgq, dgkv
    grads['g_mix'] = jnp.concatenate([dg_mix0, dg_mix1], axis=0)
    grads['g_final'] = dg_final[0]

    landed += _chip_exchange(*_grad_plan(grads, w, MLA_W, True), name="grad_exchange")
    g_out, d_out, m_out, v_out = _reduce_and_update(REST_W + MLA_W, landed, w, mom, var)

    loss = lax.psum(loss_l[0, 0], ("x", "y", "c"))
    return (loss, dx.reshape(1, S, D), *[g_out[n] for n in WNAMES], *[d_out[n] for n in WNAMES],
            *[m_out[n] for n in WNAMES], *[v_out[n] for n in WNAMES])
```

```python
import math

import jax
import jax.numpy as jnp
from jax import lax
from jax.experimental import pallas as pl
from jax.experimental.pallas import tpu as pltpu

F32 = jnp.float32
MXU = jnp.bfloat16

D_MODEL = 1024
CHUNK = 64
EPS = 1e-6
HEADS = 8
QK_NOPE = 128
QK_ROPE = 64
V_HEAD = 128
Q_LORA = 384
KV_LORA = 256
ROPE_THETA = 10000.0
QK_DIM = QK_NOPE + QK_ROPE
SSM_GROUP = 16
SSM_GROUPS = D_MODEL // SSM_GROUP
SSM_STATE = 64
NSTATE = SSM_GROUPS * SSM_STATE
D_FF = 2816
ATT_SCALE = QK_DIM ** -0.5
EXP2_SCALE = ATT_SCALE * math.log2(math.e)
NEG = -1e30
NSEG = 8
SBLK = 8

ADAM_LR = 0.001
ADAM_B1 = 0.9
ADAM_B2 = 0.999
ADAM_EPS = 1e-08
ADAM_WD = 0.01
ADAM_STEP = 10

LANE = 128
TN_MAX_COLS = 2816
NN_MAX_COLS = 1408
NN_MAX_K = 2816
TN_ACC_ELEMS = 1536 * 1024
TN_ROWS = 1024
VMEM_BIG = 56 * 1024 * 1024

WNAMES = ['mla_w_a', 'mla_g_q', 'mla_g_kv', 'mla_w_uq', 'mla_w_ukv', 'mla_w_o', 'ssm_w_in', 'ssm_lambda_re',
          'ssm_lambda_im', 'ssm_log_dt', 'ssm_b_re', 'ssm_b_im', 'ssm_c_re', 'ssm_c_im', 'ssm_d', 'ssm_w_glu',
          'ffn_w_up', 'ffn_conv_w', 'ffn_conv_b', 'ffn_w_down', 'g_mix', 'g_ffn', 'g_final']
SHARD_AXIS = {'mla_w_a': 1, 'mla_w_uq': 2, 'mla_w_ukv': 2, 'mla_w_o': 1, 'ssm_w_in': 1, 'ssm_d': 1,
              'ssm_w_glu': 2, 'ffn_w_up': 2, 'ffn_conv_w': 2, 'ffn_w_down': 1}
GATHER_F32 = ['ssm_d', 'ffn_conv_w']
NCHIP = 4
PACKW = 1024
SMALL_ROWS = 64
MESH = pl.DeviceIdType.MESH


def _tile(d, pref):
    t = min(pref, d) // LANE * LANE
    while t >= LANE:
        if d % t == 0:
            return t
        t -= LANE
    return d


def _rows(s, pref):
    t = min(s, pref)
    assert s % t == 0 and t % 8 == 0
    return t


def _params(big=False):
    if big:
        return pltpu.CompilerParams(vmem_limit_bytes=VMEM_BIG)
    return pltpu.CompilerParams(vmem_limit_bytes=40 * 1024 * 1024)


def _mm(a, b, *, name, mode="nn", out_dtype=F32, res=None, tm=None, tn=None, tk=None):
    N = b.shape[0] if mode == "nt" else b.shape[1]
    if mode != "tn":
        M, K = a.shape
        if N > 1024 and N % 512:
            tn = tn or N
            tm = tm or _rows(M, 512)
        narrow = jnp.dtype(a.dtype).itemsize == 2
        tk = tk or _tile(K, NN_MAX_K if narrow else NN_MAX_COLS)
        tm = tm or _rows(M, 2048 if narrow and K <= 1024 else 1024)
    else:
        K, M = a.shape
        tn = tn or _tile(N, TN_MAX_COLS)
        tm = tm or _tile(M, max(LANE, TN_ACC_ELEMS // tn))
    assert b.shape[1 if mode == "nt" else 0] == K
    tn = tn or _tile(N, NN_MAX_COLS if N % 512 else 512)
    tk = tk or _rows(K, TN_ROWS)
    nk = K // tk
    has_res = res is not None

    def body(a_ref, b_ref, *rest):
        if has_res:
            r_ref, o_ref, acc = rest
        else:
            o_ref, acc = rest
        k = pl.program_id(2)

        @pl.when(k == 0)
        def _():
            acc[...] = jnp.zeros_like(acc)

        av = a_ref[...].astype(MXU)
        bv = b_ref[...].astype(MXU)
        if mode == "nn":
            acc[...] += jnp.dot(av, bv, preferred_element_type=F32)
        elif mode == "nt":
            acc[...] += _dot_nt(av, bv)
        else:
            acc[...] += _dot_tn(av, bv)

        @pl.when(k == nk - 1)
        def _():
            o = acc[...]
            if has_res:
                o = o + r_ref[...]
            o_ref[...] = o.astype(o_ref.dtype)

    if mode == "tn":
        a_spec = pl.BlockSpec((tk, tm), lambda i, j, k: (k, i))
    else:
        a_spec = pl.BlockSpec((tm, tk), lambda i, j, k: (i, k))
    if mode == "nt":
        b_spec = pl.BlockSpec((tn, tk), lambda i, j, k: (j, k))
    else:
        b_spec = pl.BlockSpec((tk, tn), lambda i, j, k: (k, j))
    in_specs = [a_spec, b_spec]
    ops = [a, b]
    if has_res:
        in_specs.append(pl.BlockSpec((tm, tn), lambda i, j, k: (i, j)))
        ops.append(res)
    return pl.pallas_call(
        body, name=name, grid=(M // tm, N // tn, nk), in_specs=in_specs,
        out_specs=pl.BlockSpec((tm, tn), lambda i, j, k: (i, j)),
        out_shape=jax.ShapeDtypeStruct((M, N), out_dtype),
        scratch_shapes=[pltpu.VMEM((tm, tn), F32)], compiler_params=_params(),
    )(*ops)


def _row_spec(tm, c):
    return pl.BlockSpec((tm, c), lambda i: (i, 0))


def _const_spec(r, c):
    return pl.BlockSpec((r, c), lambda i: (0, 0))


def _rms_parts(xv):
    r = lax.rsqrt(jnp.mean(xv * xv, axis=-1, keepdims=True) + EPS)
    return r, xv * r


def _rms_vjp(xv, gv, dyv):
    r, xhat = _rms_parts(xv)
    gy = dyv * gv
    dx = r * (gy - xhat * jnp.mean(gy * xhat, axis=-1, keepdims=True))
    return dx, dyv * xhat


def _rmsnorm_fwd(x, g, *, name):
    S, D = x.shape
    tm = _rows(S, 512)

    def body(x_ref, g_ref, o_ref):
        _, xhat = _rms_parts(x_ref[...])
        o_ref[...] = (xhat * g_ref[...]).astype(o_ref.dtype)

    return pl.pallas_call(
        body, name=name, grid=(S // tm,), in_specs=[_row_spec(tm, D), _const_spec(1, D)],
        out_specs=_row_spec(tm, D), out_shape=jax.ShapeDtypeStruct((S, D), MXU), compiler_params=_params(),
    )(x, g)


def _rmsnorm_bwd(x, g, dy, dres, *, name):
    S, D = x.shape
    tm = _rows(S, 512)

    def body(x_ref, g_ref, dy_ref, dr_ref, dx_ref, dg_ref):
        @pl.when(pl.program_id(0) == 0)
        def _():
            dg_ref[...] = jnp.zeros_like(dg_ref)

        dx, dgp = _rms_vjp(x_ref[...], g_ref[...], dy_ref[...])
        dx_ref[...] = dr_ref[...] + dx
        dg_ref[...] += jnp.sum(dgp, axis=0, keepdims=True)

    return pl.pallas_call(
        body, name=name, grid=(S // tm,),
        in_specs=[_row_spec(tm, D), _const_spec(1, D), _row_spec(tm, D), _row_spec(tm, D)],
        out_specs=(_row_spec(tm, D), _const_spec(1, D)),
        out_shape=(jax.ShapeDtypeStruct((S, D), F32), jax.ShapeDtypeStruct((1, D), F32)),
        compiler_params=_params(),
    )(x, g, dy, dres)


def _loss_head(h, g, tgt, *, name):
    S, D = h.shape
    tm = _rows(S, 512)

    def body(h_ref, g_ref, t_ref, l_ref, dh_ref, dg_ref):
        @pl.when(pl.program_id(0) == 0)
        def _():
            l_ref[...] = jnp.zeros_like(l_ref)
            dg_ref[...] = jnp.zeros_like(dg_ref)

        hv = h_ref[...]
        gv = g_ref[...]
        _, xhat = _rms_parts(hv)
        e = xhat * gv - t_ref[...]
        l_ref[...] += 0.5 * jnp.sum(jnp.mean(e * e, axis=-1, keepdims=True), axis=0, keepdims=True)
        dx, dgp = _rms_vjp(hv, gv, e * (1.0 / D))
        dh_ref[...] = dx
        dg_ref[...] += jnp.sum(dgp, axis=0, keepdims=True)

    return pl.pallas_call(
        body, name=name, grid=(S // tm,),
        in_specs=[_row_spec(tm, D), _const_spec(1, D), _row_spec(tm, D)],
        out_specs=(_const_spec(1, 1), _row_spec(tm, D), _const_spec(1, D)),
        out_shape=(jax.ShapeDtypeStruct((1, 1), F32), jax.ShapeDtypeStruct((S, D), F32),
                   jax.ShapeDtypeStruct((1, D), F32)),
        compiler_params=_params(),
    )(h, g, tgt)


def _swap_halves(g):
    lane = lax.broadcasted_iota(jnp.int32, g.shape, 1)
    return jnp.where(lane < QK_ROPE // 2, pltpu.roll(g, LANE - QK_ROPE // 2, axis=1),
                     pltpu.roll(g, QK_ROPE // 2, axis=1))


def _rope128(g, c128, s128):
    return g * c128 + _swap_halves(g) * s128


def _rope128_vjp(dy, c128, s128):
    lane = lax.broadcasted_iota(jnp.int32, dy.shape, 1)
    return jnp.where(lane < QK_ROPE, dy * c128 + _swap_halves(dy * s128), 0.0)


A_PAD = 768
KR0 = Q_LORA + KV_LORA


def _mla_mid_fwd(a, g_q, g_kv, c128, s128, *, name):
    S = a.shape[0]
    tm = _rows(S, 512)

    def body(a_ref, gq_ref, gkv_ref, c_ref, s_ref, cq_ref, ckv_ref, kr_ref):
        av = a_ref[...]
        _, qh = _rms_parts(av[:, :Q_LORA])
        cq_ref[...] = (qh * gq_ref[...]).astype(cq_ref.dtype)
        _, kh = _rms_parts(av[:, Q_LORA:KR0])
        ckv_ref[...] = (kh * gkv_ref[...]).astype(ckv_ref.dtype)
        kr = _rope128(av[:, KR0:A_PAD], c_ref[...], s_ref[...])
        kr_ref[...] = kr[:, :QK_ROPE].astype(kr_ref.dtype)

    return pl.pallas_call(
        body, name=name, grid=(S // tm,),
        in_specs=[_row_spec(tm, A_PAD), _const_spec(1, Q_LORA), _const_spec(1, KV_LORA), _row_spec(tm, LANE),
                  _row_spec(tm, LANE)],
        out_specs=(_row_spec(tm, Q_LORA), _row_spec(tm, KV_LORA), _row_spec(tm, QK_ROPE)),
        out_shape=(jax.ShapeDtypeStruct((S, Q_LORA), MXU), jax.ShapeDtypeStruct((S, KV_LORA), MXU),
                   jax.ShapeDtypeStruct((S, QK_ROPE), MXU)),
        compiler_params=_params(),
    )(a, g_q, g_kv, c128, s128)


def _mla_mid_bwd(a, dcq, dckv, dkr, g_q, g_kv, c128, s128, *, name):
    S = a.shape[0]
    tm = _rows(S, 512)

    def body(a_ref, dcq_ref, dckv_ref, dkr_ref, gq_ref, gkv_ref, c_ref, s_ref, da_ref, dgq_ref, dgkv_ref):
        @pl.when(pl.program_id(0) == 0)
        def _():
            dgq_ref[...] = jnp.zeros_like(dgq_ref)
            dgkv_ref[...] = jnp.zeros_like(dgkv_ref)

        av = a_ref[...]
        dx, dgp = _rms_vjp(av[:, :Q_LORA], gq_ref[...], dcq_ref[...])
        da_ref[:, :Q_LORA] = dx.astype(da_ref.dtype)
        dgq_ref[...] += jnp.sum(dgp, axis=0, keepdims=True)
        dx, dgp = _rms_vjp(av[:, Q_LORA:KR0], gkv_ref[...], dckv_ref[...])
        da_ref[:, Q_LORA:KR0] = dx.astype(da_ref.dtype)
        dgkv_ref[...] += jnp.sum(dgp, axis=0, keepdims=True)
        da_ref[:, KR0:A_PAD] = _rope128_vjp(dkr_ref[...], c_ref[...], s_ref[...]).astype(da_ref.dtype)

    return pl.pallas_call(
        body, name=name, grid=(S // tm,),
        in_specs=[_row_spec(tm, A_PAD), _row_spec(tm, Q_LORA), _row_spec(tm, KV_LORA), _row_spec(tm, LANE),
                  _const_spec(1, Q_LORA), _const_spec(1, KV_LORA), _row_spec(tm, LANE), _row_spec(tm, LANE)],
        out_specs=(_row_spec(tm, A_PAD), _const_spec(1, Q_LORA), _const_spec(1, KV_LORA)),
        out_shape=(jax.ShapeDtypeStruct((S, A_PAD), MXU), jax.ShapeDtypeStruct((1, Q_LORA), F32),
                   jax.ShapeDtypeStruct((1, KV_LORA), F32)),
        compiler_params=_params(),
    )(a, dcq, dckv, dkr, g_q, g_kv, c128, s128)


QF = 2 * HEADS * LANE
KVF = HEADS * (QK_NOPE + V_HEAD)
VX = 2 * V_HEAD


def _qk_prep(qfull, kv, kr, c128, s128, *, name):
    S = qfull.shape[0]
    tm = _rows(S, 256)

    def body(q_ref, kv_ref, kr_ref, c_ref, s_ref, *outs):
        qo, ko, vo = outs[:HEADS], outs[HEADS:2 * HEADS], outs[2 * HEADS:]
        cv, sv = c_ref[...], s_ref[...]
        krv = kr_ref[...]
        for h in range(HEADS):
            qo[h][:, :QK_NOPE] = q_ref[:, h * LANE:(h + 1) * LANE].astype(MXU)
            g = q_ref[:, (HEADS + h) * LANE:(HEADS + h + 1) * LANE]
            qo[h][:, QK_NOPE:] = _rope128(g, cv, sv)[:, :QK_ROPE].astype(MXU)
            ko[h][:, :QK_NOPE] = kv_ref[:, 2 * h * LANE:(2 * h + 1) * LANE]
            ko[h][:, QK_NOPE:] = krv
            vo[h][:, :V_HEAD] = kv_ref[:, (2 * h + 1) * LANE:(2 * h + 2) * LANE]
            vo[h][:, V_HEAD:] = jnp.ones((tm, VX - V_HEAD), MXU)

    shapes = ([jax.ShapeDtypeStruct((S, QK_DIM), MXU)] * (2 * HEADS)
              + [jax.ShapeDtypeStruct((S, VX), MXU)] * HEADS)
    specs = [_row_spec(tm, QK_DIM)] * (2 * HEADS) + [_row_spec(tm, VX)] * HEADS
    outs = pl.pallas_call(
        body, name=name, grid=(S // tm,),
        in_specs=[_row_spec(tm, QF), _row_spec(tm, KVF), _row_spec(tm, QK_ROPE), _row_spec(tm, LANE),
                  _row_spec(tm, LANE)],
        out_specs=tuple(specs), out_shape=tuple(shapes), compiler_params=_params(),
    )(qfull, kv, kr, c128, s128)
    return outs[:HEADS], outs[HEADS:2 * HEADS], outs[2 * HEADS:]


def _qk_prep_bwd(dqs, dks, dvs, c128, s128, *, name):
    S = dqs[0].shape[0]
    tm = _rows(S, 256)

    def body(*refs):
        dq = refs[:HEADS]
        dk = refs[HEADS:2 * HEADS]
        dv = refs[2 * HEADS:3 * HEADS]
        c_ref, s_ref, dqf_ref, dkv_ref, dkr_ref, tmp = refs[3 * HEADS:]
        cv, sv = c_ref[...], s_ref[...]
        tmp[...] = jnp.zeros_like(tmp)
        dkr_ref[...] = jnp.zeros_like(dkr_ref)
        for h in range(HEADS):
            dqf_ref[:, h * LANE:(h + 1) * LANE] = dq[h][:, :QK_NOPE].astype(MXU)
            tmp[:, :QK_ROPE] = dq[h][:, QK_NOPE:]
            dqf_ref[:, (HEADS + h) * LANE:(HEADS + h + 1) * LANE] = _rope128_vjp(tmp[...], cv, sv).astype(MXU)
            dkv_ref[:, 2 * h * LANE:(2 * h + 1) * LANE] = dk[h][:, :QK_NOPE].astype(MXU)
            dkv_ref[:, (2 * h + 1) * LANE:(2 * h + 2) * LANE] = dv[h][...].astype(MXU)
            dkr_ref[:, :QK_ROPE] += dk[h][:, QK_NOPE:]

    return pl.pallas_call(
        body, name=name, grid=(S // tm,),
        in_specs=[_row_spec(tm, QK_DIM)] * (2 * HEADS) + [_row_spec(tm, V_HEAD)] * HEADS
        + [_row_spec(tm, LANE), _row_spec(tm, LANE)],
        out_specs=(_row_spec(tm, QF), _row_spec(tm, KVF), _row_spec(tm, LANE)),
        out_shape=(jax.ShapeDtypeStruct((S, QF), MXU), jax.ShapeDtypeStruct((S, KVF), MXU),
                   jax.ShapeDtypeStruct((S, LANE), F32)),
        scratch_shapes=[pltpu.VMEM((tm, LANE), F32)], compiler_params=_params(),
    )(*dqs, *dks, *dvs, c128, s128)


def _dot_nt(a, b):
    return lax.dot_general(a, b, (((1,), (1,)), ((), ())), preferred_element_type=F32)


def _dot_tn(a, b):
    return lax.dot_general(a, b, (((0,), (0,)), ((), ())), preferred_element_type=F32)


def _attn_fwd(q, k, vx, *, name, ride=None):
    S = q.shape[0]
    T = _rows(S, 1024)
    n = S // T
    cpt = T // CHUNK

    r_srcs, r_outs, r_pieces = ride or ((), (), ())
    ns, no = len(r_srcs), len(r_outs)

    def body(q_ref, k_ref, v_ref, *rest):
        o_ref, lse_ref = rest[ns:ns + 2]
        s_buf, p_buf, a_buf, m_s, acc_s = rest[ns + 2 + no:ns + 7 + no]
        i = pl.program_id(0)
        if ride:
            start, finish = _exchange_ops(r_pieces, rest[:ns], rest[ns + 2:ns + 2 + no], *rest[ns + 7 + no:])
            pl.when(i == 0)(start)
        qc = lax.broadcasted_iota(jnp.int32, (T, T), 0) // CHUNK
        kc = lax.broadcasted_iota(jnp.int32, (T, T), 1) // CHUNK
        dchunk = kc - qc

        def tile_rows(b):
            return pl.ds(pl.multiple_of(jnp.clip(b, 0, n - 1) * T, T), T)

        def scores(b, slot):
            s = _dot_nt(q_ref[...], k_ref[tile_rows(b), :])
            s_buf[slot] = jnp.where(dchunk <= (i - b) * cpt, s, NEG)

        def softmax(slot):
            s = s_buf[slot]
            m_prev = m_s[...]
            m_new = jnp.maximum(m_prev, jnp.max(s, axis=1, keepdims=True))
            a_buf[slot] = jnp.exp2((m_prev - m_new) * EXP2_SCALE)
            p_buf[slot] = jnp.exp2((s - m_new) * EXP2_SCALE).astype(MXU)
            m_s[...] = m_new

        def pv(b, slot):
            acc_s[...] = a_buf[slot] * acc_s[...] + jnp.dot(p_buf[slot], v_ref[tile_rows(b), :],
                                                              preferred_element_type=F32)

        m_s[...] = jnp.full_like(m_s, NEG)
        acc_s[...] = jnp.zeros_like(acc_s)
        p_buf[1] = jnp.zeros((T, T), MXU)
        a_buf[1] = jnp.ones((T, 1), F32)
        scores(0, 0)

        def pair(u, carry):
            t = 2 * u
            scores(t + 1, 1)
            softmax(0)
            pv(t - 1, 1)
            scores(t + 2, 0)
            softmax(1)
            pv(t, 0)
            return carry

        npairs = (i + 2) // 2
        lax.fori_loop(0, npairs, pair, 0)
        pv(2 * npairs - 1, 1)
        acc = acc_s[...]
        l = acc[:, V_HEAD:V_HEAD + 1]
        o_ref[...] = (acc[:, :V_HEAD] / l).astype(o_ref.dtype)
        lse_ref[...] = m_s[...] * ATT_SCALE + jnp.log(l)
        if ride:
            pl.when(i == n - 1)(finish)

    hbm = pl.BlockSpec(memory_space=pl.ANY)
    return pl.pallas_call(
        body, name=name, grid=(n,),
        in_specs=[pl.BlockSpec((T, QK_DIM), lambda i: (i, 0)), pl.BlockSpec((S, QK_DIM), lambda i: (0, 0)),
                  pl.BlockSpec((S, VX), lambda i: (0, 0))] + [hbm] * ns,
        out_specs=(pl.BlockSpec((T, V_HEAD), lambda i: (i, 0)), pl.BlockSpec((T, 1), lambda i: (i, 0))) + (hbm,) * no,
        out_shape=(jax.ShapeDtypeStruct((S, V_HEAD), MXU), jax.ShapeDtypeStruct((S, 1), F32)) + tuple(r_outs),
        scratch_shapes=[pltpu.VMEM((2, T, T), F32), pltpu.VMEM((2, T, T), MXU), pltpu.VMEM((2, T, 1), F32),
                        pltpu.VMEM((T, 1), F32), pltpu.VMEM((T, VX), F32)]
        + (_exchange_scratch(len(r_pieces)) if ride else []),
        compiler_params=_params(big=True),
    )(q, k, vx, *r_srcs)


def _attn_delta(do, o, *, name):
    S = do.shape[0]
    tm = _rows(S, 512)

    def body(do_ref, o_ref, d_ref):
        for h in range(HEADS):
            cols = slice(h * V_HEAD, (h + 1) * V_HEAD)
            d_ref[:, h:h + 1] = jnp.sum(do_ref[:, cols].astype(F32) * o_ref[:, cols].astype(F32), axis=1,
                                        keepdims=True)

    width = HEADS * V_HEAD
    return pl.pallas_call(
        body, name=name, grid=(S // tm,), in_specs=[_row_spec(tm, width), _row_spec(tm, width)],
        out_specs=_row_spec(tm, HEADS), out_shape=jax.ShapeDtypeStruct((S, HEADS), F32), compiler_params=_params(),
    )(do, o)


BWD_T = 512


def _attn_bwd(q, k, v, do, lse2, delta, *, name, ride=None):
    S = q.shape[0]
    T = _rows(S, BWD_T)
    n = S // T
    cpt = T // CHUNK

    r_srcs, r_outs, r_pieces = ride or ((), (), ())
    ns, no = len(r_srcs), len(r_outs)

    def body(q_hbm, k_ref, v_ref, do_hbm, lse_ref, dl_ref, *rest):
        dq_hbm, dk_ref, dv_ref = rest[ns:ns + 3]
        q_res, do_res, dq_s, s_buf, dp_buf, p_buf, ds_buf, dk_s, dv_s = rest[ns + 3 + no:ns + 12 + no]
        j = pl.program_id(0)
        if ride:
            start, finish = _exchange_ops(r_pieces, rest[:ns], rest[ns + 3:ns + 3 + no], *rest[ns + 12 + no:])
            pl.when(j == 0)(start)

        @pl.when(j == 0)
        def _():
            pltpu.sync_copy(q_hbm, q_res)
            pltpu.sync_copy(do_hbm, do_res)
            dq_s[...] = jnp.zeros_like(dq_s)

        kc = lax.broadcasted_iota(jnp.int32, (T, T), 0) // CHUNK
        qc = lax.broadcasted_iota(jnp.int32, (T, T), 1) // CHUNK
        dchunk = kc - qc

        def tile(t):
            return jnp.clip(j + t, 0, n - 1)

        def rows(t):
            return pl.ds(pl.multiple_of(tile(t) * T, T), T)

        def scores(t, slot):
            visible_up_to = jnp.where(j + t < n, t * cpt, -2 * cpt)
            s = _dot_nt(k_ref[...], q_res[rows(t), :])
            s_buf[slot] = jnp.where(dchunk <= visible_up_to, s, NEG)
            dp_buf[slot] = _dot_nt(v_ref[...], do_res[rows(t), :])

        def probs(t, slot):
            pt = jnp.exp2(s_buf[slot] * EXP2_SCALE - lse_ref[tile(t)])
            p_buf[slot] = pt.astype(MXU)
            ds_buf[slot] = (pt * (dp_buf[slot] - dl_ref[tile(t)]) * ATT_SCALE).astype(MXU)

        def grads(t, slot):
            r = rows(t)
            dv_s[...] += jnp.dot(p_buf[slot], do_res[r, :], preferred_element_type=F32)
            ds = ds_buf[slot]
            dk_s[...] += jnp.dot(ds, q_res[r, :], preferred_element_type=F32)
            dq_s[r, :] += _dot_tn(ds, k_ref[...])

        dk_s[...] = jnp.zeros_like(dk_s)
        dv_s[...] = jnp.zeros_like(dv_s)
        p_buf[1] = jnp.zeros((T, T), MXU)
        ds_buf[1] = jnp.zeros((T, T), MXU)
        scores(0, 0)

        def pair(u, carry):
            t = 2 * u
            scores(t + 1, 1)
            probs(t, 0)
            grads(t - 1, 1)
            scores(t + 2, 0)
            probs(t + 1, 1)
            grads(t, 0)
            return carry

        npairs = (n - j + 1) // 2
        lax.fori_loop(0, npairs, pair, 0)
        grads(2 * npairs - 1, 1)
        dk_ref[...] = dk_s[...]
        dv_ref[...] = dv_s[...]

        @pl.when(j == n - 1)
        def _():
            pltpu.sync_copy(dq_s, dq_hbm)
            if ride:
                finish()

    hbm = pl.BlockSpec(memory_space=pl.ANY)
    k_map = lambda j: (j, 0)
    whole = pl.BlockSpec((n, 1, T), lambda j: (0, 0, 0))
    return pl.pallas_call(
        body, name=name, grid=(n,),
        in_specs=[hbm, pl.BlockSpec((T, QK_DIM), k_map), pl.BlockSpec((T, V_HEAD), k_map), hbm, whole, whole]
        + [hbm] * ns,
        out_specs=(hbm, pl.BlockSpec((T, QK_DIM), k_map), pl.BlockSpec((T, V_HEAD), k_map)) + (hbm,) * no,
        out_shape=(jax.ShapeDtypeStruct((S, QK_DIM), F32), jax.ShapeDtypeStruct((S, QK_DIM), F32),
                   jax.ShapeDtypeStruct((S, V_HEAD), F32)) + tuple(r_outs),
        scratch_shapes=[pltpu.VMEM((S, QK_DIM), MXU), pltpu.VMEM((S, V_HEAD), MXU), pltpu.VMEM((S, QK_DIM), F32),
                        pltpu.VMEM((2, T, T), F32), pltpu.VMEM((2, T, T), F32), pltpu.VMEM((2, T, T), MXU),
                        pltpu.VMEM((2, T, T), MXU), pltpu.VMEM((T, QK_DIM), F32), pltpu.VMEM((T, V_HEAD), F32)]
        + (_exchange_scratch(len(r_pieces)) if ride else []),
        compiler_params=_params(big=True),
    )(q, k, v, do, lse2, delta, *r_srcs)


HALO = 8
CONV_RH = 64


def _conv_tiles(S):
    tm = _rows(S, 256)
    tc = D_FF // 2
    return tm, tc, D_FF // tc


def _silu_parts(gate):
    sg = jax.nn.sigmoid(gate)
    return sg, gate * sg


def _convgate_fwd(up, cw, cb, *, name):
    S = up.shape[0]
    tm, tc, nc = _conv_tiles(S)
    hb = tm // HALO

    def body(v_ref, g_ref, hv_ref, hg_ref, wv_ref, wg_ref, bv_ref, bg_ref, o_ref, cv_ref, cg_ref):
        keep = (pl.program_id(0) > 0).astype(F32)

        def chunk(cc, carry):
            cols = pl.ds(pl.multiple_of(cc * LANE, LANE), LANE)
            wv, wg, bv, bg = wv_ref[:, cols], wg_ref[:, cols], bv_ref[:, cols], bg_ref[:, cols]
            for r0 in range(0, tm, CONV_RH):
                def conv(t_ref, h_ref, w, b):
                    if r0:
                        span = t_ref[pl.ds(r0 - HALO, CONV_RH + HALO), cols]
                    else:
                        span = jnp.concatenate([h_ref[:, cols] * keep, t_ref[pl.ds(0, CONV_RH), cols]], axis=0)
                    back2, back1 = [pltpu.roll(span, s, axis=0)[HALO:] for s in (2, 1)]
                    return w[0:1] * back2 + w[1:2] * back1 + w[2:3] * span[HALO:] + b
                val = conv(v_ref, hv_ref, wv, bv)
                gate = conv(g_ref, hg_ref, wg, bg)
                cv_ref[pl.ds(r0, CONV_RH), cols] = val.astype(cv_ref.dtype)
                cg_ref[pl.ds(r0, CONV_RH), cols] = gate.astype(cg_ref.dtype)
                o_ref[pl.ds(r0, CONV_RH), cols] = (_silu_parts(gate)[1] * val).astype(o_ref.dtype)
            return carry

        lax.fori_loop(0, tc // LANE, chunk, 0)

    prev = lambda i: jnp.maximum(i * hb - 1, 0)
    return pl.pallas_call(
        body, name=name, grid=(S // tm, nc),
        in_specs=[pl.BlockSpec((tm, tc), lambda i, j: (i, j)), pl.BlockSpec((tm, tc), lambda i, j: (i, j + nc)),
                  pl.BlockSpec((HALO, tc), lambda i, j: (prev(i), j)),
                  pl.BlockSpec((HALO, tc), lambda i, j: (prev(i), j + nc)),
                  pl.BlockSpec((3, tc), lambda i, j: (0, j)), pl.BlockSpec((3, tc), lambda i, j: (0, j + nc)),
                  pl.BlockSpec((1, tc), lambda i, j: (0, j)), pl.BlockSpec((1, tc), lambda i, j: (0, j + nc))],
        out_specs=(pl.BlockSpec((tm, tc), lambda i, j: (i, j)),) * 3,
        out_shape=(jax.ShapeDtypeStruct((S, D_FF), MXU),) * 3,
        compiler_params=_params(),
    )(up, up, up, up, cw, cw, cb, cb)


def _convgate_bwd(up, cval, cgate, dact, cw, *, name):
    S = up.shape[0]
    tm, tc, nc = _conv_tiles(S)
    hb = tm // HALO
    nr = S // tm
    R = tm + HALO

    def body(uv_ref, ug_ref, cv_ref, cg_ref, ncv_ref, ncg_ref, da_ref, dan_ref, wv_ref, wg_ref,
             duv_ref, dug_ref, dwv_ref, dwg_ref, dbv_ref, dbg_ref, dsv, dsg):
        i = pl.program_id(1)

        @pl.when(i == 0)
        def _():
            for r in (dwv_ref, dwg_ref, dbv_ref, dbg_ref):
                r[...] = jnp.zeros_like(r)

        keep_next = (i < nr - 1).astype(F32)

        def chunk(cc, carry):
            cols = pl.ds(pl.multiple_of(cc * LANE, LANE), LANE)

            def d_conv(rows, val, gate, d):
                sg, silu = _silu_parts(gate)
                dsv[rows, cols] = d * silu
                dsg[rows, cols] = d * val * (sg * (1.0 + gate * (1.0 - sg)))

            for r0 in range(0, tm, CONV_RH):
                rows = pl.ds(r0, CONV_RH)
                d_conv(rows, cv_ref[rows, cols].astype(F32), cg_ref[rows, cols].astype(F32), da_ref[rows, cols])
            d_conv(pl.ds(tm, HALO), ncv_ref[0:HALO, cols].astype(F32), ncg_ref[0:HALO, cols].astype(F32),
                   dan_ref[:, cols] * keep_next)
            for ds, u_ref, w_ref, du_ref, dw_ref, db_ref in ((dsv, uv_ref, wv_ref, duv_ref, dwv_ref, dbv_ref),
                                                            (dsg, ug_ref, wg_ref, dug_ref, dwg_ref, dbg_ref)):
                w = w_ref[:, cols]
                acc = [jnp.zeros((1, LANE), F32) for _ in range(4)]
                for r0 in range(0, tm, CONV_RH):
                    rows = pl.ds(r0, CONV_RH)
                    span = ds[pl.ds(r0, CONV_RH + HALO), cols]
                    d = [span[:CONV_RH]] + [pltpu.roll(span, CONV_RH + HALO - s, axis=0)[:CONV_RH] for s in (1, 2)]
                    du_ref[rows, cols] = (w[2:3] * d[0] + w[1:2] * d[1] + w[0:1] * d[2]).astype(du_ref.dtype)
                    u = u_ref[rows, cols]
                    for kk in range(3):
                        acc[kk] = acc[kk] + jnp.sum(d[2 - kk] * u, axis=0, keepdims=True)
                    acc[3] = acc[3] + jnp.sum(d[0], axis=0, keepdims=True)
                for kk in range(3):
                    dw_ref[kk:kk + 1, cols] += acc[kk]
                db_ref[:, cols] += acc[3]
            return carry

        lax.fori_loop(0, tc // LANE, chunk, 0)

    nxt = lambda i: jnp.minimum((i + 1) * hb, S // HALO - 1)
    tile_v = pl.BlockSpec((tm, tc), lambda j, i: (i, j))
    tile_g = pl.BlockSpec((tm, tc), lambda j, i: (i, j + nc))
    halo = pl.BlockSpec((HALO, tc), lambda j, i: (nxt(i), j))
    halo16 = pl.BlockSpec((2 * HALO, tc), lambda j, i: (jnp.minimum((i + 1) * (hb // 2), S // (2 * HALO) - 1), j))
    w_v = pl.BlockSpec((3, tc), lambda j, i: (0, j))
    w_g = pl.BlockSpec((3, tc), lambda j, i: (0, j + nc))
    b_v = pl.BlockSpec((1, tc), lambda j, i: (0, j))
    return pl.pallas_call(
        body, name=name, grid=(nc, nr),
        in_specs=[tile_v, tile_g, tile_v, tile_v, halo16, halo16, tile_v, halo, w_v, w_g],
        out_specs=(tile_v, tile_v, w_v, w_v, b_v, b_v),
        out_shape=(jax.ShapeDtypeStruct((S, D_FF), MXU), jax.ShapeDtypeStruct((S, D_FF), MXU),
                   jax.ShapeDtypeStruct((3, D_FF), F32), jax.ShapeDtypeStruct((3, D_FF), F32),
                   jax.ShapeDtypeStruct((1, D_FF), F32), jax.ShapeDtypeStruct((1, D_FF), F32)),
        scratch_shapes=[pltpu.VMEM((R, tc), F32), pltpu.VMEM((R, tc), F32)],
        compiler_params=_params(),
    )(up, up, cval, cgate, cval, cgate, dact, dact, cw, cw)


def _glu_fwd(z, h, *, name):
    S = z.shape[0]
    tm = _rows(S, 512)

    def body(z_ref, h_ref, o_ref):
        o_ref[...] = h_ref[...] + z_ref[:, :D_MODEL] * jax.nn.sigmoid(z_ref[:, D_MODEL:])

    return pl.pallas_call(
        body, name=name, grid=(S // tm,), in_specs=[_row_spec(tm, 2 * D_MODEL), _row_spec(tm, D_MODEL)],
        out_specs=_row_spec(tm, D_MODEL), out_shape=jax.ShapeDtypeStruct((S, D_MODEL), F32),
        compiler_params=_params(),
    )(z, h)


def _glu_bwd(z, dm, *, name):
    S = z.shape[0]
    tm = _rows(S, 512)

    def body(z_ref, dm_ref, o_ref):
        sg = jax.nn.sigmoid(z_ref[:, D_MODEL:])
        dmv = dm_ref[...]
        o_ref[:, :D_MODEL] = (dmv * sg).astype(o_ref.dtype)
        o_ref[:, D_MODEL:] = (dmv * z_ref[:, :D_MODEL] * sg * (1.0 - sg)).astype(o_ref.dtype)

    return pl.pallas_call(
        body, name=name, grid=(S // tm,), in_specs=[_row_spec(tm, 2 * D_MODEL), _row_spec(tm, D_MODEL)],
        out_specs=_row_spec(tm, 2 * D_MODEL), out_shape=jax.ShapeDtypeStruct((S, 2 * D_MODEL), MXU),
        compiler_params=_params(),
    )(z, dm)


GELU_C = math.sqrt(2.0 / math.pi)
GELU_A = 0.044715


def _gelu(y):
    return 0.5 * y * (1.0 + jnp.tanh(GELU_C * (y + GELU_A * (y * y * y))))


def _gelu_bwd(y, dg, *, name):
    S = y.shape[0]
    tm = _rows(S, 512)

    def body(y_ref, dg_ref, o_ref):
        yv = y_ref[...]
        t = jnp.tanh(GELU_C * (yv + GELU_A * (yv * yv * yv)))
        d = 0.5 * (1.0 + t) + 0.5 * yv * (1.0 - t * t) * (GELU_C * (1.0 + 3.0 * GELU_A * (yv * yv)))
        o_ref[...] = dg_ref[...] * d

    return pl.pallas_call(
        body, name=name, grid=(S // tm,), in_specs=[_row_spec(tm, D_MODEL), _row_spec(tm, D_MODEL)],
        out_specs=_row_spec(tm, D_MODEL), out_shape=jax.ShapeDtypeStruct((S, D_MODEL), F32),
        compiler_params=_params(),
    )(y, dg)


FWD_STRIPS = 4
BWD_STRIPS = 8
SW = NSTATE // SBLK
ST2 = 2 * NSTATE


def _s5_fwd(u, wb, wc, abc, dskip, x0, *, full, name):
    S = u.shape[0]
    T = _rows(S, 256 if full else 512)
    nb = S // T
    nj = T // NSEG

    def body(u_ref, wb_ref, wc_ref, a_ref, d_ref, x0_ref, *rest):
        if full:
            xs_ref, y_ref, yg_ref, st = rest
        else:
            e_ref, xs_ref, st = rest
        i = pl.program_id(0)

        @pl.when(i == 0)
        def _():
            st[...] = x0_ref[...]

        uv = u_ref[...]
        ub = uv.astype(MXU)
        for kb in range(SBLK):
            r = jnp.dot(ub[:, kb * LANE:(kb + 1) * LANE], wb_ref[kb], preferred_element_type=F32)
            xs_ref[:, kb * SW:(kb + 1) * SW] = r[:, :SW]
            xs_ref[:, NSTATE + kb * SW:NSTATE + (kb + 1) * SW] = r[:, SW:]
        for sp in range(FWD_STRIPS):
            w = NSTATE // FWD_STRIPS
            re, im = pl.ds(sp * w, w), pl.ds(NSTATE + sp * w, w)
            ar, ai = a_ref[:, re], a_ref[:, im]

            def step(j, c):
                xr, xi = c
                rows = pl.ds(pl.multiple_of(j * NSEG, NSEG), NSEG)
                nr = ar * xr - ai * xi + xs_ref[rows, re]
                ni = ar * xi + ai * xr + xs_ref[rows, im]
                xs_ref[rows, re] = nr
                xs_ref[rows, im] = ni
                return nr, ni

            xr, xi = lax.fori_loop(0, nj, step, (st[:, re], st[:, im]))
            st[:, re] = xr
            st[:, im] = xi
        if full:
            for kb in range(SBLK):
                yk = (jnp.dot(xs_ref[:, kb * SW:(kb + 1) * SW].astype(MXU), wc_ref[kb, :SW, :], preferred_element_type=F32)
                      + jnp.dot(xs_ref[:, NSTATE + kb * SW:NSTATE + (kb + 1) * SW].astype(MXU), wc_ref[kb, SW:, :],
                                preferred_element_type=F32))
                cols = slice(kb * LANE, (kb + 1) * LANE)
                yk = yk + d_ref[:, cols] * uv[:, cols]
                y_ref[:, cols] = yk
                yg_ref[:, cols] = _gelu(yk).astype(yg_ref.dtype)
        else:
            @pl.when(i == nb - 1)
            def _():
                e_ref[...] = st[...]

    in_specs = [_row_spec(T, D_MODEL), pl.BlockSpec((SBLK, LANE, 2 * SW), lambda i: (0, 0, 0)),
                pl.BlockSpec((SBLK, 2 * SW, LANE), lambda i: (0, 0, 0)), _const_spec(NSEG, ST2),
                _const_spec(1, D_MODEL), _const_spec(NSEG, ST2)]
    if full:
        out_specs = (_row_spec(T, ST2), _row_spec(T, D_MODEL), _row_spec(T, D_MODEL))
        out_shape = (jax.ShapeDtypeStruct((S, ST2), F32), jax.ShapeDtypeStruct((S, D_MODEL), F32),
                     jax.ShapeDtypeStruct((S, D_MODEL), MXU))
        scratch = [pltpu.VMEM((NSEG, ST2), F32)]
    else:
        out_specs = _const_spec(NSEG, ST2)
        out_shape = jax.ShapeDtypeStruct((NSEG, ST2), F32)
        scratch = [pltpu.VMEM((T, ST2), F32), pltpu.VMEM((NSEG, ST2), F32)]
    return pl.pallas_call(
        body, name=name, grid=(nb,), in_specs=in_specs, out_specs=out_specs, out_shape=out_shape,
        scratch_shapes=scratch, compiler_params=_params(big=True),
    )(u, wb, wc, abc, dskip, x0)


def _s5_bwd(dy, xs, u, wct, wbt, abc, dskip, x0, l0, *, full, name):
    S = dy.shape[0]
    T = _rows(S, 256 if full else 512)
    nb = S // T
    nj = T // NSEG
    blk = lambda i: nb - 1 - i

    def body(dy_ref, *rest):
        if full:
            (xs_ref, xh_ref, u_ref, wct_ref, wbt_ref, a_ref, d_ref, x0_ref, l0_ref,
             du_ref, da_ref, dwb_hbm, dwc_hbm, dd_ref, g_s, lam_s, dwb_ref, dwc_ref) = rest
        else:
            wct_ref, a_ref, l0_ref, f_ref, g_s, lam_s = rest
        i = pl.program_id(0)

        @pl.when(i == 0)
        def _():
            lam_s[...] = l0_ref[...]
            if full:
                for r in (da_ref, dwb_ref, dwc_ref, dd_ref):
                    r[...] = jnp.zeros_like(r)

        dyv = dy_ref[...]
        dyb = dyv.astype(MXU)
        for kb in range(SBLK):
            r = jnp.dot(dyb[:, kb * LANE:(kb + 1) * LANE], wct_ref[kb], preferred_element_type=F32)
            g_s[:, kb * SW:(kb + 1) * SW] = r[:, :SW]
            g_s[:, NSTATE + kb * SW:NSTATE + (kb + 1) * SW] = r[:, SW:]
        for sp in range(BWD_STRIPS):
            w = NSTATE // BWD_STRIPS
            re, im = pl.ds(sp * w, w), pl.ds(NSTATE + sp * w, w)
            ar, ai = a_ref[:, re], a_ref[:, im]

            def advance(row, lr, li):
                rows = pl.ds(row, NSEG)
                nr = g_s[rows, re] + ar * lr + ai * li
                ni = g_s[rows, im] - ai * lr + ar * li
                g_s[rows, re] = nr
                g_s[rows, im] = ni
                return nr, ni

            def step(jj, c):
                row = pl.multiple_of((nj - 1 - jj) * NSEG, NSEG)
                nr, ni = advance(row, c[0], c[1])
                if not full:
                    return nr, ni
                prow = pl.ds(pl.multiple_of(row - NSEG, NSEG), NSEG)
                xpr, xpi = xs_ref[prow, re], xs_ref[prow, im]
                return nr, ni, c[2] + (nr * xpr + ni * xpi), c[3] + (ni * xpr - nr * xpi)

            init = (lam_s[:, re], lam_s[:, im])
            if full:
                init = init + (jnp.zeros((NSEG, w), F32), jnp.zeros((NSEG, w), F32))
            c = lax.fori_loop(0, nj - 1, step, init)
            lr, li = advance(0, c[0], c[1])
            if full:
                first = (blk(i) == 0)
                xpr = jnp.where(first, x0_ref[:, re], xh_ref[:, re])
                xpi = jnp.where(first, x0_ref[:, im], xh_ref[:, im])
                da_ref[:, re] += c[2] + (lr * xpr + li * xpi)
                da_ref[:, im] += c[3] + (li * xpr - lr * xpi)
            lam_s[:, re] = lr
            lam_s[:, im] = li
        if full:
            uv = u_ref[...]
            ub = uv.astype(MXU)
            dd_ref[...] += jnp.sum(dyv * uv, axis=0, keepdims=True)
            for kb in range(SBLK):
                cols = slice(kb * LANE, (kb + 1) * LANE)
                re = slice(kb * SW, (kb + 1) * SW)
                im = slice(NSTATE + kb * SW, NSTATE + (kb + 1) * SW)
                lr_b = g_s[:, re].astype(MXU)
                li_b = g_s[:, im].astype(MXU)
                duk = (jnp.dot(lr_b, wbt_ref[kb, :SW, :], preferred_element_type=F32)
                       + jnp.dot(li_b, wbt_ref[kb, SW:, :], preferred_element_type=F32))
                du_ref[:, cols] = duk + d_ref[:, cols] * dyv[:, cols]
                dwb_ref[kb, :, :SW] += _dot_tn(ub[:, cols], lr_b)
                dwb_ref[kb, :, SW:] += _dot_tn(ub[:, cols], li_b)
                dwc_ref[kb, :SW, :] += _dot_tn(xs_ref[:, re].astype(MXU), dyb[:, cols])
                dwc_ref[kb, SW:, :] += _dot_tn(xs_ref[:, im].astype(MXU), dyb[:, cols])

            @pl.when(i == nb - 1)
            def _():
                pltpu.sync_copy(dwb_ref, dwb_hbm)
                pltpu.sync_copy(dwc_ref, dwc_hbm)
        else:
            @pl.when(i == nb - 1)
            def _():
                f_ref[...] = lam_s[...]

    rev = lambda c: pl.BlockSpec((T, c), lambda i: (blk(i), 0))
    w3 = lambda a, b: pl.BlockSpec((SBLK, a, b), lambda i: (0, 0, 0))
    if full:
        hb = T // NSEG
        in_specs = [rev(D_MODEL), rev(ST2),
                    pl.BlockSpec((NSEG, ST2), lambda i: (jnp.maximum(blk(i) * hb - 1, 0), 0)),
                    rev(D_MODEL), w3(LANE, 2 * SW), w3(2 * SW, LANE), _const_spec(NSEG, ST2),
                    _const_spec(1, D_MODEL), _const_spec(NSEG, ST2), _const_spec(NSEG, ST2)]
        ops = [dy, xs, xs, u, wct, wbt, abc, dskip, x0, l0]
        hbm = pl.BlockSpec(memory_space=pl.ANY)
        out_specs = (rev(D_MODEL), _const_spec(NSEG, ST2), hbm, hbm, _const_spec(1, D_MODEL))
        out_shape = (jax.ShapeDtypeStruct((S, D_MODEL), F32), jax.ShapeDtypeStruct((NSEG, ST2), F32),
                     jax.ShapeDtypeStruct((SBLK, LANE, 2 * SW), F32), jax.ShapeDtypeStruct((SBLK, 2 * SW, LANE), F32),
                     jax.ShapeDtypeStruct((1, D_MODEL), F32))
    else:
        in_specs = [rev(D_MODEL), w3(LANE, 2 * SW), _const_spec(NSEG, ST2), _const_spec(NSEG, ST2)]
        ops = [dy, wct, abc, l0]
        out_specs = _const_spec(NSEG, ST2)
        out_shape = jax.ShapeDtypeStruct((NSEG, ST2), F32)
    return pl.pallas_call(
        body, name=name, grid=(nb,), in_specs=in_specs, out_specs=out_specs, out_shape=out_shape,
        scratch_shapes=[pltpu.VMEM((T, ST2), F32), pltpu.VMEM((NSEG, ST2), F32)]
        + ([pltpu.VMEM((SBLK, LANE, 2 * SW), F32), pltpu.VMEM((SBLK, 2 * SW, LANE), F32)] if full else []),
        compiler_params=_params(big=True),
    )(*ops)


def _s5_discretize(lr, li, log_dt, br, bi):
    dt = jnp.exp(log_dt)[:, None]
    mag = jnp.exp(lr * dt)
    ar = mag * jnp.cos(li * dt)
    ai = mag * jnp.sin(li * dt)
    den = lr * lr + li * li
    nr = ar - 1.0
    coef_r = (nr * lr + ai * li) / den
    coef_i = (ai * lr - nr * li) / den
    bbar_r = coef_r[..., None] * br - coef_i[..., None] * bi
    bbar_i = coef_r[..., None] * bi + coef_i[..., None] * br
    return ar, ai, bbar_r, bbar_i


def _blockdiag(m):
    gpb = SSM_GROUPS // SBLK
    a, b = m.shape[1:]
    mb = m.reshape(SBLK, gpb, a, b)
    eye = jnp.eye(gpb, dtype=m.dtype)
    return jnp.einsum('kgab,gh->kgahb', mb, eye).reshape(SBLK, gpb * a, gpb * b)


def _blockdiag_extract(w, a, b):
    gpb = SSM_GROUPS // SBLK
    w5 = w.reshape(SBLK, gpb, a, gpb, b)
    return jnp.einsum('kgahb,gh->kgab', w5, jnp.eye(gpb, dtype=w.dtype)).reshape(SSM_GROUPS, a, b)


def _cpow(ar, ai, n):
    rr, ri = jnp.ones_like(ar), jnp.zeros_like(ai)
    br, bi = ar, ai
    while n:
        if n & 1:
            rr, ri = rr * br - ri * bi, rr * bi + ri * br
        br, bi = br * br - bi * bi, 2.0 * br * bi
        n >>= 1
    return rr, ri


def _perm(a):
    s, c = a.shape
    return a.reshape(NSEG, s // NSEG, c).transpose(1, 0, 2).reshape(s, c)


def _unperm(a):
    s, c = a.shape
    return a.reshape(s // NSEG, NSEG, c).transpose(1, 0, 2).reshape(s, c)


def _other_chips(x, y):
    return [(1 - x, y), (x, 1 - y), (1 - x, 1 - y)]


def _span(chip, size, align):
    return pl.ds(pl.multiple_of(chip * size, align), size)


def _exchange_ops(pieces, s_refs, o_refs, send_sems, recv_sems, local_sems):
    x, y, c = lax.axis_index("x"), lax.axis_index("y"), lax.axis_index("c")
    me = 2 * x + y
    others = _other_chips(x, y)
    npc = len(pieces)

    def remote(k, p, tx, ty, src_chip, dst_chip):
        si, oi, sv, dv = pieces[p]
        return pltpu.make_async_remote_copy(
            src_ref=sv(s_refs[si], src_chip), dst_ref=dv(o_refs[oi], dst_chip), send_sem=send_sems.at[k, p],
            recv_sem=recv_sems.at[k, p], device_id=(tx, ty, c), device_id_type=MESH)

    def local(p):
        si, oi, sv, dv = pieces[p]
        return pltpu.make_async_copy(sv(s_refs[si], me), dv(o_refs[oi], me), local_sems.at[p])

    def start():
        for p in range(npc):
            local(p).start()
        for k, (tx, ty) in enumerate(others):
            for p in range(npc):
                remote(k, p, tx, ty, 2 * tx + ty, me).start()

    def finish():
        for k, (tx, ty) in enumerate(others):
            for p in range(npc):
                remote(k, p, tx, ty, me, 2 * tx + ty).wait_recv()
        for k, (tx, ty) in enumerate(others):
            for p in range(npc):
                remote(k, p, tx, ty, 2 * tx + ty, me).wait_send()
        for p in range(npc):
            local(p).wait()

    return start, finish


def _exchange_scratch(npc):
    return [pltpu.SemaphoreType.DMA((NCHIP - 1, npc)), pltpu.SemaphoreType.DMA((NCHIP - 1, npc)),
            pltpu.SemaphoreType.DMA((npc,))]


def _chip_exchange(srcs, out_shapes, pieces, *, name):
    ns, no = len(srcs), len(out_shapes)

    def body(*refs):
        start, finish = _exchange_ops(pieces, refs[:ns], refs[ns:ns + no], *refs[ns + no:])
        start()
        finish()

    hbm = pl.BlockSpec(memory_space=pl.ANY)
    return pl.pallas_call(
        body, name=name, in_specs=[hbm] * ns, out_specs=tuple([hbm] * no), out_shape=tuple(out_shapes),
        scratch_shapes=_exchange_scratch(len(pieces)),
    )(*srcs)


def _sibling_exchange(srcs, *, name):
    n = len(srcs)

    def body(*refs):
        s_refs, o_refs, send_sems, recv_sems = refs[:n], refs[n:2 * n], refs[2 * n], refs[2 * n + 1]
        x, y, c = lax.axis_index("x"), lax.axis_index("y"), lax.axis_index("c")
        cps = [pltpu.make_async_remote_copy(src_ref=s_refs[p], dst_ref=o_refs[p], send_sem=send_sems.at[p],
                                            recv_sem=recv_sems.at[p], device_id=(x, y, 1 - c), device_id_type=MESH)
               for p in range(n)]
        for cp in cps:
            cp.start()
        for cp in cps:
            cp.wait()

    hbm = pl.BlockSpec(memory_space=pl.ANY)
    return pl.pallas_call(
        body, name=name, in_specs=[hbm] * n, out_specs=tuple([hbm] * n),
        out_shape=tuple(jax.ShapeDtypeStruct(s.shape, s.dtype) for s in srcs),
        scratch_shapes=[pltpu.SemaphoreType.DMA((n,)), pltpu.SemaphoreType.DMA((n,))],
    )(*srcs)


def _row_tile(r, c, tile_bytes):
    best = 16
    for t in range(16, r + 1, 16):
        if r % t == 0 and t * c * 4 <= tile_bytes:
            best = t
    assert r % best == 0
    return best


def _sum_chips(r, *, name):
    _, R, W = r.shape
    tm = _row_tile(R, W, 2 * 1024 * 1024)

    def body(r_ref, o_ref):
        o_ref[...] = ((r_ref[0].astype(F32) + r_ref[1].astype(F32)) + r_ref[2].astype(F32)) + r_ref[3].astype(F32)

    return pl.pallas_call(
        body, name=name, grid=(R // tm,), in_specs=[pl.BlockSpec((NCHIP, tm, W), lambda i: (0, i, 0))],
        out_specs=_row_spec(tm, W), out_shape=jax.ShapeDtypeStruct((R, W), F32), compiler_params=_params(),
    )(r)


def _adamw(p_mine, p_sib, w, m, v, *, name):
    R, W = w.shape
    tm = _row_tile(R, W, 1024 * 1024)

    def body(a_ref, b_ref, w_ref, m_ref, v_ref, g_ref, d_ref, nm_ref, nv_ref):
        g = a_ref[...] + b_ref[...]
        mm = ADAM_B1 * m_ref[...] + (1.0 - ADAM_B1) * g
        vv = ADAM_B2 * v_ref[...] + (1.0 - ADAM_B2) * (g * g)
        m_hat = mm / (1.0 - ADAM_B1 ** ADAM_STEP)
        v_hat = vv / (1.0 - ADAM_B2 ** ADAM_STEP)
        g_ref[...] = g
        d_ref[...] = -ADAM_LR * (m_hat / (jnp.sqrt(v_hat) + ADAM_EPS) + ADAM_WD * w_ref[...])
        nm_ref[...] = mm
        nv_ref[...] = vv

    spec = _row_spec(tm, W)
    shp = jax.ShapeDtypeStruct((R, W), F32)
    return pl.pallas_call(
        body, name=name, grid=(R // tm,), in_specs=[spec] * 5, out_specs=(spec,) * 4, out_shape=(shp,) * 4,
        compiler_params=_params(),
    )(p_mine, p_sib, w, m, v)


def _pack_small(parts):
    flat = jnp.concatenate([p.reshape(-1) for p in parts])
    pad = (-flat.shape[0]) % (SMALL_ROWS * PACKW)
    return jnp.pad(flat, (0, pad)).reshape(-1, PACKW)


def _unpack_small(buf, shapes):
    flat, out, off = buf.reshape(-1), [], 0
    for shp in shapes:
        sz = math.prod(shp)
        out.append(flat[off:off + sz].reshape(shp))
        off += sz
    return out


def _shard(a, t, ax):
    sz = a.shape[ax] // NCHIP
    return lax.slice_in_dim(a, t * sz, (t + 1) * sz, axis=ax)


MLA_W = ['mla_w_a', 'mla_w_uq', 'mla_w_ukv', 'mla_w_o']
REST_W = ['ssm_w_in', 'ssm_w_glu', 'ffn_w_up', 'ffn_w_down']
SMALL = [n for n in WNAMES if n not in MLA_W + REST_W]


def _gather_plan(w, names, with_small):
    srcs, outs, pieces = [], [], []
    for name in names:
        local = w[name].astype(MXU)
        local = local[0] if local.shape[0] == 1 else local
        ax = SHARD_AXIS[name] - (1 if w[name].shape[0] == 1 else 0)
        full = local.shape[:ax] + (NCHIP * local.shape[ax],) + local.shape[ax + 1:]
        si, oi = len(srcs), len(outs)
        srcs.append(local)
        outs.append(jax.ShapeDtypeStruct(full, MXU))
        size = local.shape[ax]
        if local.ndim == 2:
            if ax == 0:
                pieces.append((si, oi, lambda r, t: r, lambda r, ch, size=size: r.at[_span(ch, size, 8), :]))
            else:
                pieces.append((si, oi, lambda r, t: r, lambda r, ch, size=size: r.at[:, _span(ch, size, LANE)]))
        else:
            for l in range(local.shape[0]):
                if ax == 1:
                    dv = lambda r, ch, l=l, size=size: r.at[l, _span(ch, size, 8), :]
                else:
                    dv = lambda r, ch, l=l, size=size: r.at[l, :, _span(ch, size, LANE)]
                pieces.append((si, oi, lambda r, t, l=l: r.at[l], dv))
    if with_small:
        small = _pack_small([w[n] for n in GATHER_F32])
        srcs.append(small)
        outs.append(jax.ShapeDtypeStruct((NCHIP,) + small.shape, F32))
        pieces.append((len(srcs) - 1, len(outs) - 1, lambda r, t: r, lambda r, ch: r.at[ch]))
    return srcs, outs, pieces


def _gather_result(got, w, names, with_small):
    full = dict(zip(names, got[:len(names)]))
    if with_small:
        per_chip = [_unpack_small(got[-1][t], [w[n].shape for n in GATHER_F32]) for t in range(NCHIP)]
        for j, n in enumerate(GATHER_F32):
            full[n] = jnp.concatenate([per_chip[t][j] for t in range(NCHIP)], axis=SHARD_AXIS[n])
    return full


def _as_rows(a):
    return a.reshape(-1, a.shape[-1])


def _grad_plan(grads, w, names, with_small):
    srcs, outs, pieces = [], [], []
    for name in names:
        local = w[name]
        ax = SHARD_AXIS[name]
        size = local.shape[ax]
        oi = len(outs)
        layers = grads[name] if isinstance(grads[name], list) else [grads[name]]
        outs.append(jax.ShapeDtypeStruct((NCHIP,) + local.shape, layers[0].dtype))
        for l, g in enumerate(layers):
            si = len(srcs)
            srcs.append(g)
            if ax == 1:
                sv = lambda r, t, size=size: r.at[_span(t, size, 8), :]
            else:
                sv = lambda r, t, size=size: r.at[:, _span(t, size, LANE)]
            pieces.append((si, oi, sv, lambda r, ch, l=l: r.at[ch, l]))
    if with_small:
        small = jnp.stack([_pack_small([_shard(grads[n], t, SHARD_AXIS[n]) if n in SHARD_AXIS else grads[n]
                                        for n in SMALL]) for t in range(NCHIP)])
        srcs.append(small)
        outs.append(jax.ShapeDtypeStruct(small.shape, F32))
        pieces.append((len(srcs) - 1, len(outs) - 1, lambda r, t: r.at[t], lambda r, ch: r.at[ch]))
    return srcs, outs, pieces


def _reduce_and_update(names, landed, w, mom, var):
    partial = [_sum_chips(r.reshape(NCHIP, -1, r.shape[-1]), name="grad_sum_chips") for r in landed]
    sibling = _sibling_exchange(partial, name="grad_sibling")
    res = [dict(), dict(), dict(), dict()]
    for j, name in enumerate(names):
        outs4 = _adamw(partial[j], sibling[j], _as_rows(w[name]), _as_rows(mom[name]), _as_rows(var[name]), name="adamw")
        for d, o in zip(res, outs4):
            d[name] = o.reshape(w[name].shape)
    outs4 = _adamw(partial[-1], sibling[-1], *[_pack_small([t[n] for n in SMALL]) for t in (w, mom, var)], name="adamw")
    for d, o in zip(res, outs4):
        d.update(zip(SMALL, _unpack_small(o, [w[n].shape for n in SMALL])))
    return res


def kernel(x, positions, mla_w_a, mla_g_q, mla_g_kv, mla_w_uq, mla_w_ukv, mla_w_o, ssm_w_in, ssm_lambda_re, ssm_lambda_im, ssm_log_dt, ssm_b_re, ssm_b_im, ssm_c_re, ssm_c_im, ssm_d, ssm_w_glu, ffn_w_up, ffn_conv_w, ffn_conv_b, ffn_w_down, g_mix, g_ffn, g_final, loss_target, m_mla_w_a, m_mla_g_q, m_mla_g_kv, m_mla_w_uq, m_mla_w_ukv, m_mla_w_o, m_ssm_w_in, m_ssm_lambda_re, m_ssm_lambda_im, m_ssm_log_dt, m_ssm_b_re, m_ssm_b_im, m_ssm_c_re, m_ssm_c_im, m_ssm_d, m_ssm_w_glu, m_ffn_w_up, m_ffn_conv_w, m_ffn_conv_b, m_ffn_w_down, m_g_mix, m_g_ffn, m_g_final, v_mla_w_a, v_mla_g_q, v_mla_g_kv, v_mla_w_uq, v_mla_w_ukv, v_mla_w_o, v_ssm_w_in, v_ssm_lambda_re, v_ssm_lambda_im, v_ssm_log_dt, v_ssm_b_re, v_ssm_b_im, v_ssm_c_re, v_ssm_c_im, v_ssm_d, v_ssm_w_glu, v_ffn_w_up, v_ffn_conv_w, v_ffn_conv_b, v_ffn_w_down, v_g_mix, v_g_ffn, v_g_final):
    w = dict(zip(WNAMES, (mla_w_a, mla_g_q, mla_g_kv, mla_w_uq, mla_w_ukv, mla_w_o, ssm_w_in, ssm_lambda_re,
                          ssm_lambda_im, ssm_log_dt, ssm_b_re, ssm_b_im, ssm_c_re, ssm_c_im, ssm_d, ssm_w_glu,
                          ffn_w_up, ffn_conv_w, ffn_conv_b, ffn_w_down, g_mix, g_ffn, g_final)))
    mom = dict(zip(WNAMES, (m_mla_w_a, m_mla_g_q, m_mla_g_kv, m_mla_w_uq, m_mla_w_ukv, m_mla_w_o, m_ssm_w_in,
                            m_ssm_lambda_re, m_ssm_lambda_im, m_ssm_log_dt, m_ssm_b_re, m_ssm_b_im, m_ssm_c_re,
                            m_ssm_c_im, m_ssm_d, m_ssm_w_glu, m_ffn_w_up, m_ffn_conv_w, m_ffn_conv_b,
                            m_ffn_w_down, m_g_mix, m_g_ffn, m_g_final)))
    var = dict(zip(WNAMES, (v_mla_w_a, v_mla_g_q, v_mla_g_kv, v_mla_w_uq, v_mla_w_ukv, v_mla_w_o, v_ssm_w_in,
                            v_ssm_lambda_re, v_ssm_lambda_im, v_ssm_log_dt, v_ssm_b_re, v_ssm_b_im, v_ssm_c_re,
                            v_ssm_c_im, v_ssm_d, v_ssm_w_glu, v_ffn_w_up, v_ffn_conv_w, v_ffn_conv_b,
                            v_ffn_w_down, v_g_mix, v_g_ffn, v_g_final)))
    S = x.shape[1]
    D = D_MODEL
    x2 = x.reshape(S, D)
    tgt = loss_target.reshape(S, D)

    fw = _gather_result(_chip_exchange(*_gather_plan(w, MLA_W, False), name="gather_weights"), w, MLA_W, False)
    w_a = jnp.pad(fw['mla_w_a'], ((0, 0), (0, A_PAD - KR0 - QK_ROPE)))
    uq = fw['mla_w_uq'].reshape(Q_LORA, HEADS, QK_DIM)
    w_uq = jnp.concatenate([uq[:, :, :QK_NOPE].reshape(Q_LORA, HEADS * QK_NOPE),
                            jnp.pad(uq[:, :, QK_NOPE:], ((0, 0), (0, 0), (0, LANE - QK_ROPE))).reshape(Q_LORA, HEADS * LANE)],
                           axis=1)
    w_ukv = fw['mla_w_ukv']
    w_o = fw['mla_w_o']
    conv_b = w['ffn_conv_b']
    g_q, g_kv = w['mla_g_q'], w['mla_g_kv']
    gm, gf = w['g_mix'], w['g_ffn']
    gfin = w['g_final'].reshape(1, D)

    inv = 1.0 / (ROPE_THETA ** (jnp.arange(0, QK_ROPE, 2, dtype=F32) / QK_ROPE))
    ang = positions.reshape(S).astype(F32)[:, None] * inv
    cos, sin = jnp.cos(ang), jnp.sin(ang)
    zpad = jnp.zeros((S, LANE - QK_ROPE), F32)
    c128 = jnp.concatenate([cos, cos, zpad], axis=1)
    s128 = jnp.concatenate([-sin, sin, zpad], axis=1)

    hn0 = _rmsnorm_fwd(x2, gm[0:1], name="rms_mix0")
    a = _mm(hn0, w_a, name="mla_a")
    cqn, ckvn, kr = _mla_mid_fwd(a, g_q, g_kv, c128, s128, name="mla_mid_fwd")
    qfull = _mm(cqn, w_uq, name="mla_q")
    kv = _mm(ckvn, w_ukv, out_dtype=MXU, name="mla_kv")
    qs, ks, vs = _qk_prep(qfull, kv, kr, c128, s128, name="qk_prep")
    os_, lses = [], []
    rides = {0: (['ssm_w_in', 'ssm_w_glu', 'ffn_w_down'], True), 1: (['ffn_w_up'], False)}
    for h in range(HEADS):
        if h in rides:
            o_h, lse_h, *got = _attn_fwd(qs[h], ks[h], vs[h], name="attn_fwd_gather", ride=_gather_plan(w, *rides[h]))
            fw.update(_gather_result(got, w, *rides[h]))
        else:
            o_h, lse_h = _attn_fwd(qs[h], ks[h], vs[h], name="attn_fwd")
        os_.append(o_h)
        lses.append(lse_h)
    w_in = fw['ssm_w_in']
    w_glu = fw['ssm_w_glu']
    w_up = fw['ffn_w_up']
    w_down = fw['ffn_w_down']
    conv_w = fw['ffn_conv_w']
    dskip = fw['ssm_d']
    o_cat = jnp.concatenate(os_, axis=1)
    h1 = _mm(o_cat, w_o, res=x2, name="mla_o")

    def ffn_fwd(h, l):
        hn = _rmsnorm_fwd(h, gf[l:l + 1], name="rms_ffn")
        up = _mm(hn, w_up[l], name="ffn_up")
        act, cval, cgate = _convgate_fwd(up, conv_w[l], conv_b[l:l + 1], name="convgate_fwd")
        return _mm(act, w_down[l], res=h, name="ffn_down"), (hn, up, act, cval, cgate)

    h2, saved0 = ffn_fwd(h1, 0)

    lam_re, lam_im, log_dt = w['ssm_lambda_re'][0], w['ssm_lambda_im'][0], w['ssm_log_dt'][0]
    (a_re, a_im, bbar_r, bbar_i), disc_vjp = jax.vjp(_s5_discretize, lam_re, lam_im, log_dt, w['ssm_b_re'][0],
                                                    w['ssm_b_im'][0])
    c_re, c_im = w['ssm_c_re'][0], w['ssm_c_im'][0]
    bt_r, bt_i = jnp.swapaxes(bbar_r, 1, 2), jnp.swapaxes(bbar_i, 1, 2)
    wb = jnp.concatenate([_blockdiag(bt_r), _blockdiag(bt_i)], axis=2).astype(MXU)
    wbt = jnp.concatenate([_blockdiag(bbar_r), _blockdiag(bbar_i)], axis=1).astype(MXU)
    ct_r, ct_i = jnp.swapaxes(c_re, 1, 2), jnp.swapaxes(c_im, 1, 2)
    wc = jnp.concatenate([_blockdiag(ct_r), _blockdiag(-ct_i)], axis=1).astype(MXU)
    wct = jnp.concatenate([_blockdiag(c_re), _blockdiag(-c_im)], axis=2).astype(MXU)
    af_r, af_i = a_re.reshape(NSTATE), a_im.reshape(NSTATE)
    abc = jnp.broadcast_to(jnp.concatenate([af_r, af_i])[None], (NSEG, ST2))
    seg = S // NSEG
    ap_r, ap_i = _cpow(af_r, af_i, seg)

    hn1 = _rmsnorm_fwd(h2, gm[1:2], name="rms_mix1")
    u = _mm(hn1, w_in, name="s5_in")
    u_p = _perm(u)
    zero_state = jnp.zeros((NSEG, ST2), F32)
    ends = _s5_fwd(u_p, wb, wc, abc, dskip, zero_state, full=False, name="s5_fwd_ends")
    inits, cr, ci = [], jnp.zeros((NSTATE,), F32), jnp.zeros((NSTATE,), F32)
    for r in range(NSEG):
        inits.append(jnp.concatenate([cr, ci]))
        er, ei = ends[r, :NSTATE], ends[r, NSTATE:]
        cr, ci = er + ap_r * cr - ap_i * ci, ei + ap_r * ci + ap_i * cr
    x0 = jnp.stack(inits)
    xs, y_p, yg_p = _s5_fwd(u_p, wb, wc, abc, dskip, x0, full=True, name="s5_fwd")
    yg = _unperm(yg_p)
    z = _mm(yg, w_glu, name="s5_glu")
    h3 = _glu_fwd(z, h2, name="glu_fwd")
    h4, saved1 = ffn_fwd(h3, 1)

    loss_l, dh4, dg_final = _loss_head(h4, gfin, tgt, name="loss_head")

    grads = {}

    def ffn_bwd(h_in, g, saved, l):
        hn, up, act, cval, cgate = saved
        dact = _mm(g, w_down[l], mode="nt", name="ffn_down_dx")
        dw_down = _mm(act, g, mode="tn", out_dtype=MXU, name="ffn_down_dw")
        duv, dug, dwv, dwg, dbv, dbg = _convgate_bwd(up, cval, cgate, dact, conv_w[l], name="convgate_bwd")
        dw_up = jnp.concatenate([_mm(hn, duv, mode="tn", out_dtype=MXU, name="ffn_up_dw"), _mm(hn, dug, mode="tn", out_dtype=MXU, name="ffn_up_dw")],
                                axis=1)
        dhn = _mm(duv, w_up[l][:, :D_FF], mode="nt", name="ffn_up_dx")
        dhn = _mm(dug, w_up[l][:, D_FF:], mode="nt", res=dhn, name="ffn_up_dx_acc")
        dh, dg = _rmsnorm_bwd(h_in, gf[l:l + 1], dhn, g, name="rms_ffn_bwd")
        return dh, dict(w_up=dw_up, w_down=dw_down, conv_w=jnp.concatenate([dwv, dwg], axis=1),
                        conv_b=jnp.concatenate([dbv, dbg], axis=1)[0], g_ffn=dg[0])

    dh3, fg1 = ffn_bwd(h3, dh4, saved1, 1)

    dz = _glu_bwd(z, dh3, name="glu_bwd")
    grads['ssm_w_glu'] = _mm(yg, dz, mode="tn", out_dtype=MXU, name="s5_glu_dw")
    dyg = _mm(dz, w_glu, mode="nt", name="s5_glu_dx")
    dy_p = _gelu_bwd(y_p, _perm(dyg), name="gelu_bwd")
    firsts = _s5_bwd(dy_p, None, None, wct, None, abc, None, None, zero_state, full=False, name="s5_bwd_firsts")
    linits, cr, ci = [None] * NSEG, jnp.zeros((NSTATE,), F32), jnp.zeros((NSTATE,), F32)
    for r in reversed(range(NSEG)):
        linits[r] = jnp.concatenate([cr, ci])
        fr, fi = firsts[r, :NSTATE], firsts[r, NSTATE:]
        cr, ci = fr + ap_r * cr + ap_i * ci, fi + ap_r * ci - ap_i * cr
    l0 = jnp.stack(linits)
    du_p, dab, dwb, dwc, dd = _s5_bwd(dy_p, xs, u_p, wct, wbt, abc, dskip, x0, l0, full=True, name="s5_bwd")
    du = _unperm(du_p)
    grads['ssm_w_in'] = _mm(hn1, du, mode="tn", out_dtype=MXU, name="s5_in_dw")
    dhn1 = _mm(du, w_in, mode="nt", name="s5_in_dx")
    dh2, dg_mix1 = _rmsnorm_bwd(h2, gm[1:2], dhn1, dh3, name="rms_mix_bwd")
    da_sum = jnp.sum(dab, axis=0)
    dbt_r = _blockdiag_extract(dwb[:, :, :SW], SSM_GROUP, SSM_STATE)
    dbt_i = _blockdiag_extract(dwb[:, :, SW:], SSM_GROUP, SSM_STATE)
    dlr, dli, dlog_dt, dbr, dbi = disc_vjp((da_sum[:NSTATE].reshape(SSM_GROUPS, SSM_STATE),
                                            da_sum[NSTATE:].reshape(SSM_GROUPS, SSM_STATE),
                                            jnp.swapaxes(dbt_r, 1, 2), jnp.swapaxes(dbt_i, 1, 2)))
    dct_r = _blockdiag_extract(dwc[:, :SW, :], SSM_STATE, SSM_GROUP)
    dct_i = _blockdiag_extract(dwc[:, SW:, :], SSM_STATE, SSM_GROUP)
    grads['ssm_lambda_re'], grads['ssm_lambda_im'], grads['ssm_log_dt'] = dlr[None], dli[None], dlog_dt[None]
    grads['ssm_b_re'], grads['ssm_b_im'] = dbr[None], dbi[None]
    grads['ssm_c_re'] = jnp.swapaxes(dct_r, 1, 2)[None]
    grads['ssm_c_im'] = -jnp.swapaxes(dct_i, 1, 2)[None]
    grads['ssm_d'] = dd

    dh1, fg0 = ffn_bwd(h1, dh2, saved0, 0)
    grads['ffn_w_up'] = [fg0['w_up'], fg1['w_up']]
    grads['ffn_w_down'] = [fg0['w_down'], fg1['w_down']]
    grads['ffn_conv_w'] = jnp.stack([fg0['conv_w'], fg1['conv_w']])
    grads['ffn_conv_b'] = jnp.stack([fg0['conv_b'], fg1['conv_b']])
    grads['g_ffn'] = jnp.stack([fg0['g_ffn'], fg1['g_ffn']])

    do_cat = _mm(dh1, w_o, mode="nt", out_dtype=MXU, name="mla_o_dx")
    grads['mla_w_o'] = _mm(o_cat, dh1, mode="tn", out_dtype=MXU, name="mla_o_dw")
    dqs, dks, dvs = [], [], []
    deltas = _attn_delta(do_cat, o_cat, name="attn_delta")
    for h in range(HEADS):
        do_h = do_cat[:, h * V_HEAD:(h + 1) * V_HEAD]
        delta = deltas[:, h]
        tiles = (S // _rows(S, BWD_T), 1, _rows(S, BWD_T))
        lse2 = (lses[h] * math.log2(math.e)).reshape(tiles)
        if h == 0:
            dq_h, dk_h, dv_h, *landed = _attn_bwd(qs[h], ks[h], vs[h], do_h, lse2, delta.reshape(tiles),
                                                  name="attn_bwd_exchange", ride=_grad_plan(grads, w, REST_W, False))
        else:
            dq_h, dk_h, dv_h = _attn_bwd(qs[h], ks[h], vs[h], do_h, lse2, delta.reshape(tiles), name="attn_bwd")
        dqs.append(dq_h)
        dks.append(dk_h)
        dvs.append(dv_h)
    dqfull, dkv, dkr = _qk_prep_bwd(dqs, dks, dvs, c128, s128, name="qk_prep_bwd")
    dw_uq_p = _mm(cqn, dqfull, mode="tn", out_dtype=MXU, name="mla_q_dw")
    dcqn = _mm(dqfull, w_uq, mode="nt", name="mla_q_dx")
    grads['mla_w_ukv'] = _mm(ckvn, dkv, mode="tn", out_dtype=MXU, name="mla_kv_dw")
    dckvn = _mm(dkv, w_ukv, mode="nt", name="mla_kv_dx")
    da, dgq, dgkv = _mla_mid_bwd(a, dcqn, dckvn, dkr, g_q, g_kv, c128, s128, name="mla_mid_bwd")
    grads['mla_w_a'] = _mm(hn0, da, mode="tn", out_dtype=MXU, name="mla_a_dw")[:, :KR0 + QK_ROPE]
    dhn0 = _mm(da, w_a, mode="nt", name="mla_a_dx")
    dx, dg_mix0 = _rmsnorm_bwd(x2, gm[0:1], dhn0, dh1, name="rms_mix_bwd")
    grads['mla_w_uq'] = jnp.concatenate(
        [dw_uq_p[:, :HEADS * QK_NOPE].reshape(Q_LORA, HEADS, QK_NOPE),
         dw_uq_p[:, HEADS * QK_NOPE:].reshape(Q_LORA, HEADS, LANE)[:, :, :QK_ROPE]], axis=2).reshape(Q_LORA, HEADS * QK_DIM)
    grads['mla_g_q'], grads['mla_g_kv'] = dgq, dgkv
    grads['g_mix'] = jnp.concatenate([dg_mix0, dg_mix1], axis=0)
    grads['g_final'] = dg_final[0]

    landed += _chip_exchange(*_grad_plan(grads, w, MLA_W, True), name="grad_exchange")
    g_out, d_out, m_out, v_out = _reduce_and_update(REST_W + MLA_W, landed, w, mom, var)

    loss = lax.psum(loss_l[0, 0], ("x", "y", "c"))
    return (loss, dx.reshape(1, S, D), *[g_out[n] for n in WNAMES], *[d_out[n] for n in WNAMES],
            *[m_out[n] for n in WNAMES], *[v_out[n] for n in WNAMES])
```

```python
import math

import jax
import jax.numpy as jnp
from jax import lax
from jax.experimental import pallas as pl
from jax.experimental.pallas import tpu as pltpu

F32 = jnp.float32
MXU = jnp.bfloat16

D_MODEL = 1024
CHUNK = 64
EPS = 1e-6
HEADS = 8
QK_NOPE = 128
QK_ROPE = 64
V_HEAD = 128
Q_LORA = 384
KV_LORA = 256
ROPE_THETA = 10000.0
QK_DIM = QK_NOPE + QK_ROPE
SSM_GROUP = 16
SSM_GROUPS = D_MODEL // SSM_GROUP
SSM_STATE = 64
NSTATE = SSM_GROUPS * SSM_STATE
D_FF = 2816
ATT_SCALE = QK_DIM ** -0.5
EXP2_SCALE = ATT_SCALE * math.log2(math.e)
NEG = -1e30
NSEG = 8
SBLK = 8

ADAM_LR = 0.001
ADAM_B1 = 0.9
ADAM_B2 = 0.999
ADAM_EPS = 1e-08
ADAM_WD = 0.01
ADAM_STEP = 10

LANE = 128
TN_MAX_COLS = 2816
NN_MAX_COLS = 1408
NN_MAX_K = 2816
TN_ACC_ELEMS = 1536 * 1024
TN_ROWS = 1024
VMEM_BIG = 56 * 1024 * 1024

WNAMES = ['mla_w_a', 'mla_g_q', 'mla_g_kv', 'mla_w_uq', 'mla_w_ukv', 'mla_w_o', 'ssm_w_in', 'ssm_lambda_re',
          'ssm_lambda_im', 'ssm_log_dt', 'ssm_b_re', 'ssm_b_im', 'ssm_c_re', 'ssm_c_im', 'ssm_d', 'ssm_w_glu',
          'ffn_w_up', 'ffn_conv_w', 'ffn_conv_b', 'ffn_w_down', 'g_mix', 'g_ffn', 'g_final']
SHARD_AXIS = {'mla_w_a': 1, 'mla_w_uq': 2, 'mla_w_ukv': 2, 'mla_w_o': 1, 'ssm_w_in': 1, 'ssm_d': 1,
              'ssm_w_glu': 2, 'ffn_w_up': 2, 'ffn_conv_w': 2, 'ffn_w_down': 1}
GATHER_F32 = ['ssm_d', 'ffn_conv_w']
NCHIP = 4
PACKW = 1024
SMALL_ROWS = 64
MESH = pl.DeviceIdType.MESH


def _tile(d, pref):
    t = min(pref, d) // LANE * LANE
    while t >= LANE:
        if d % t == 0:
            return t
        t -= LANE
    return d


def _rows(s, pref):
    t = min(s, pref)
    assert s % t == 0 and t % 8 == 0
    return t


def _params(big=False):
    if big:
        return pltpu.CompilerParams(vmem_limit_bytes=VMEM_BIG)
    return pltpu.CompilerParams(vmem_limit_bytes=40 * 1024 * 1024)


def _mm(a, b, *, name, mode="nn", out_dtype=F32, res=None, pair=None, tm=None, tn=None, tk=None):
    N = b.shape[0] if mode == "nt" else b.shape[1]
    if mode != "tn":
        M, K = a.shape
        if N > 1024 and N % 512:
            tn = tn or N
            tm = tm or _rows(M, 512)
        narrow = jnp.dtype(a.dtype).itemsize == 2
        tk = tk or _tile(K, NN_MAX_K if narrow else NN_MAX_COLS)
        tm = tm or _rows(M, 2048 if narrow and K <= 1024 else 1024)
    else:
        K, M = a.shape
        tn = tn or _tile(N, TN_MAX_COLS)
        tm = tm or _tile(M, max(LANE, TN_ACC_ELEMS // tn))
    assert b.shape[1 if mode == "nt" else 0] == K
    tn = tn or _tile(N, NN_MAX_COLS if N % 512 else 512)
    tk = tk or _rows(K, TN_ROWS)
    nk = K // tk
    has_res = res is not None

    def body(a_ref, b_ref, *rest):
        lhs_rhs = [(a_ref, b_ref)]
        if pair is not None:
            lhs_rhs.append(rest[:2])
            rest = rest[2:]
        if has_res:
            r_ref, o_ref, acc = rest
        else:
            o_ref, acc = rest
        k = pl.program_id(2)

        @pl.when(k == 0)
        def _():
            acc[...] = jnp.zeros_like(acc)

        for l_ref, r_ref2 in lhs_rhs:
            av = l_ref[...].astype(MXU)
            bv = r_ref2[...].astype(MXU)
            if mode == "nn":
                acc[...] += jnp.dot(av, bv, preferred_element_type=F32)
            elif mode == "nt":
                acc[...] += _dot_nt(av, bv)
            else:
                acc[...] += _dot_tn(av, bv)

        @pl.when(k == nk - 1)
        def _():
            o = acc[...]
            if has_res:
                o = o + r_ref[...]
            o_ref[...] = o.astype(o_ref.dtype)

    if mode == "tn":
        a_spec = pl.BlockSpec((tk, tm), lambda i, j, k: (k, i))
    else:
        a_spec = pl.BlockSpec((tm, tk), lambda i, j, k: (i, k))
    if mode == "nt":
        b_spec = pl.BlockSpec((tn, tk), lambda i, j, k: (j, k))
    else:
        b_spec = pl.BlockSpec((tk, tn), lambda i, j, k: (k, j))
    in_specs = [a_spec, b_spec]
    ops = [a, b]
    if pair is not None:
        in_specs += [a_spec, b_spec]
        ops += list(pair)
    if has_res:
        in_specs.append(pl.BlockSpec((tm, tn), lambda i, j, k: (i, j)))
        ops.append(res)
    return pl.pallas_call(
        body, name=name, grid=(M // tm, N // tn, nk), in_specs=in_specs,
        out_specs=pl.BlockSpec((tm, tn), lambda i, j, k: (i, j)),
        out_shape=jax.ShapeDtypeStruct((M, N), out_dtype),
        scratch_shapes=[pltpu.VMEM((tm, tn), F32)], compiler_params=_params(),
    )(*ops)


def _row_spec(tm, c):
    return pl.BlockSpec((tm, c), lambda i: (i, 0))


def _const_spec(r, c):
    return pl.BlockSpec((r, c), lambda i: (0, 0))


def _rms_parts(xv):
    r = lax.rsqrt(jnp.mean(xv * xv, axis=-1, keepdims=True) + EPS)
    return r, xv * r


def _rms_vjp(xv, gv, dyv):
    r, xhat = _rms_parts(xv)
    gy = dyv * gv
    dx = r * (gy - xhat * jnp.mean(gy * xhat, axis=-1, keepdims=True))
    return dx, dyv * xhat


def _rmsnorm_fwd(x, g, *, name):
    S, D = x.shape
    tm = _rows(S, 512)

    def body(x_ref, g_ref, o_ref):
        _, xhat = _rms_parts(x_ref[...])
        o_ref[...] = (xhat * g_ref[...]).astype(o_ref.dtype)

    return pl.pallas_call(
        body, name=name, grid=(S // tm,), in_specs=[_row_spec(tm, D), _const_spec(1, D)],
        out_specs=_row_spec(tm, D), out_shape=jax.ShapeDtypeStruct((S, D), MXU), compiler_params=_params(),
    )(x, g)


def _rmsnorm_bwd(x, g, dy, dres, *, name):
    S, D = x.shape
    tm = _rows(S, 512)

    def body(x_ref, g_ref, dy_ref, dr_ref, dx_ref, dg_ref):
        @pl.when(pl.program_id(0) == 0)
        def _():
            dg_ref[...] = jnp.zeros_like(dg_ref)

        dx, dgp = _rms_vjp(x_ref[...], g_ref[...], dy_ref[...])
        dx_ref[...] = dr_ref[...] + dx
        dg_ref[...] += jnp.sum(dgp, axis=0, keepdims=True)

    return pl.pallas_call(
        body, name=name, grid=(S // tm,),
        in_specs=[_row_spec(tm, D), _const_spec(1, D), _row_spec(tm, D), _row_spec(tm, D)],
        out_specs=(_row_spec(tm, D), _const_spec(1, D)),
        out_shape=(jax.ShapeDtypeStruct((S, D), F32), jax.ShapeDtypeStruct((1, D), F32)),
        compiler_params=_params(),
    )(x, g, dy, dres)


def _loss_head(h, g, tgt, *, name):
    S, D = h.shape
    tm = _rows(S, 512)

    def body(h_ref, g_ref, t_ref, l_ref, dh_ref, dg_ref):
        @pl.when(pl.program_id(0) == 0)
        def _():
            l_ref[...] = jnp.zeros_like(l_ref)
            dg_ref[...] = jnp.zeros_like(dg_ref)

        hv = h_ref[...]
        gv = g_ref[...]
        _, xhat = _rms_parts(hv)
        e = xhat * gv - t_ref[...]
        l_ref[...] += 0.5 * jnp.sum(jnp.mean(e * e, axis=-1, keepdims=True), axis=0, keepdims=True)
        dx, dgp = _rms_vjp(hv, gv, e * (1.0 / D))
        dh_ref[...] = dx
        dg_ref[...] += jnp.sum(dgp, axis=0, keepdims=True)

    return pl.pallas_call(
        body, name=name, grid=(S // tm,),
        in_specs=[_row_spec(tm, D), _const_spec(1, D), _row_spec(tm, D)],
        out_specs=(_const_spec(1, 1), _row_spec(tm, D), _const_spec(1, D)),
        out_shape=(jax.ShapeDtypeStruct((1, 1), F32), jax.ShapeDtypeStruct((S, D), F32),
                   jax.ShapeDtypeStruct((1, D), F32)),
        compiler_params=_params(),
    )(h, g, tgt)


def _swap_halves(g):
    lane = lax.broadcasted_iota(jnp.int32, g.shape, 1)
    return jnp.where(lane < QK_ROPE // 2, pltpu.roll(g, LANE - QK_ROPE // 2, axis=1),
                     pltpu.roll(g, QK_ROPE // 2, axis=1))


def _rope128(g, c128, s128):
    return g * c128 + _swap_halves(g) * s128


def _rope128_vjp(dy, c128, s128):
    lane = lax.broadcasted_iota(jnp.int32, dy.shape, 1)
    return jnp.where(lane < QK_ROPE, dy * c128 + _swap_halves(dy * s128), 0.0)


A_PAD = 768
KR0 = Q_LORA + KV_LORA


def _mla_mid_fwd(a, g_q, g_kv, c128, s128, *, name):
    S = a.shape[0]
    tm = _rows(S, 512)

    def body(a_ref, gq_ref, gkv_ref, c_ref, s_ref, cq_ref, ckv_ref, kr_ref):
        av = a_ref[...]
        _, qh = _rms_parts(av[:, :Q_LORA])
        cq_ref[...] = (qh * gq_ref[...]).astype(cq_ref.dtype)
        _, kh = _rms_parts(av[:, Q_LORA:KR0])
        ckv_ref[...] = (kh * gkv_ref[...]).astype(ckv_ref.dtype)
        kr = _rope128(av[:, KR0:A_PAD], c_ref[...], s_ref[...])
        kr_ref[...] = kr[:, :QK_ROPE].astype(kr_ref.dtype)

    return pl.pallas_call(
        body, name=name, grid=(S // tm,),
        in_specs=[_row_spec(tm, A_PAD), _const_spec(1, Q_LORA), _const_spec(1, KV_LORA), _row_spec(tm, LANE),
                  _row_spec(tm, LANE)],
        out_specs=(_row_spec(tm, Q_LORA), _row_spec(tm, KV_LORA), _row_spec(tm, QK_ROPE)),
        out_shape=(jax.ShapeDtypeStruct((S, Q_LORA), MXU), jax.ShapeDtypeStruct((S, KV_LORA), MXU),
                   jax.ShapeDtypeStruct((S, QK_ROPE), MXU)),
        compiler_params=_params(),
    )(a, g_q, g_kv, c128, s128)


def _mla_mid_bwd(a, dcq, dckv, dkr, g_q, g_kv, c128, s128, *, name):
    S = a.shape[0]
    tm = _rows(S, 512)

    def body(a_ref, dcq_ref, dckv_ref, dkr_ref, gq_ref, gkv_ref, c_ref, s_ref, da_ref, dgq_ref, dgkv_ref):
        @pl.when(pl.program_id(0) == 0)
        def _():
            dgq_ref[...] = jnp.zeros_like(dgq_ref)
            dgkv_ref[...] = jnp.zeros_like(dgkv_ref)

        av = a_ref[...]
        dx, dgp = _rms_vjp(av[:, :Q_LORA], gq_ref[...], dcq_ref[...])
        da_ref[:, :Q_LORA] = dx.astype(da_ref.dtype)
        dgq_ref[...] += jnp.sum(dgp, axis=0, keepdims=True)
        dx, dgp = _rms_vjp(av[:, Q_LORA:KR0], gkv_ref[...], dckv_ref[...])
        da_ref[:, Q_LORA:KR0] = dx.astype(da_ref.dtype)
        dgkv_ref[...] += jnp.sum(dgp, axis=0, keepdims=True)
        da_ref[:, KR0:A_PAD] = _rope128_vjp(dkr_ref[...], c_ref[...], s_ref[...]).astype(da_ref.dtype)

    return pl.pallas_call(
        body, name=name, grid=(S // tm,),
        in_specs=[_row_spec(tm, A_PAD), _row_spec(tm, Q_LORA), _row_spec(tm, KV_LORA), _row_spec(tm, LANE),
                  _const_spec(1, Q_LORA), _const_spec(1, KV_LORA), _row_spec(tm, LANE), _row_spec(tm, LANE)],
        out_specs=(_row_spec(tm, A_PAD), _const_spec(1, Q_LORA), _const_spec(1, KV_LORA)),
        out_shape=(jax.ShapeDtypeStruct((S, A_PAD), MXU), jax.ShapeDtypeStruct((1, Q_LORA), F32),
                   jax.ShapeDtypeStruct((1, KV_LORA), F32)),
        compiler_params=_params(),
    )(a, dcq, dckv, dkr, g_q, g_kv, c128, s128)


QF = 2 * HEADS * LANE
KVF = HEADS * (QK_NOPE + V_HEAD)
VX = 2 * V_HEAD


def _qk_prep(qfull, kv, kr, c128, s128, *, name):
    S = qfull.shape[0]
    tm = _rows(S, 256)

    def body(q_ref, kv_ref, kr_ref, c_ref, s_ref, *outs):
        qo, ko, vo = outs[:HEADS], outs[HEADS:2 * HEADS], outs[2 * HEADS:]
        cv, sv = c_ref[...], s_ref[...]
        krv = kr_ref[...]
        for h in range(HEADS):
            qo[h][:, :QK_NOPE] = q_ref[:, h * LANE:(h + 1) * LANE].astype(MXU)
            g = q_ref[:, (HEADS + h) * LANE:(HEADS + h + 1) * LANE]
            qo[h][:, QK_NOPE:] = _rope128(g, cv, sv)[:, :QK_ROPE].astype(MXU)
            ko[h][:, :QK_NOPE] = kv_ref[:, 2 * h * LANE:(2 * h + 1) * LANE]
            ko[h][:, QK_NOPE:] = krv
            vo[h][:, :V_HEAD] = kv_ref[:, (2 * h + 1) * LANE:(2 * h + 2) * LANE]
            vo[h][:, V_HEAD:] = jnp.ones((tm, VX - V_HEAD), MXU)

    shapes = ([jax.ShapeDtypeStruct((S, QK_DIM), MXU)] * (2 * HEADS)
              + [jax.ShapeDtypeStruct((S, VX), MXU)] * HEADS)
    specs = [_row_spec(tm, QK_DIM)] * (2 * HEADS) + [_row_spec(tm, VX)] * HEADS
    outs = pl.pallas_call(
        body, name=name, grid=(S // tm,),
        in_specs=[_row_spec(tm, QF), _row_spec(tm, KVF), _row_spec(tm, QK_ROPE), _row_spec(tm, LANE),
                  _row_spec(tm, LANE)],
        out_specs=tuple(specs), out_shape=tuple(shapes), compiler_params=_params(),
    )(qfull, kv, kr, c128, s128)
    return outs[:HEADS], outs[HEADS:2 * HEADS], outs[2 * HEADS:]


def _qk_prep_bwd(dqs, dks, dvs, c128, s128, *, name):
    S = dqs[0].shape[0]
    tm = _rows(S, 256)

    def body(*refs):
        dq = refs[:HEADS]
        dk = refs[HEADS:2 * HEADS]
        dv = refs[2 * HEADS:3 * HEADS]
        c_ref, s_ref, dqf_ref, dkv_ref, dkr_ref, tmp = refs[3 * HEADS:]
        cv, sv = c_ref[...], s_ref[...]
        tmp[...] = jnp.zeros_like(tmp)
        dkr_ref[...] = jnp.zeros_like(dkr_ref)
        for h in range(HEADS):
            dqf_ref[:, h * LANE:(h + 1) * LANE] = dq[h][:, :QK_NOPE].astype(MXU)
            tmp[:, :QK_ROPE] = dq[h][:, QK_NOPE:]
            dqf_ref[:, (HEADS + h) * LANE:(HEADS + h + 1) * LANE] = _rope128_vjp(tmp[...], cv, sv).astype(MXU)
            dkv_ref[:, 2 * h * LANE:(2 * h + 1) * LANE] = dk[h][:, :QK_NOPE].astype(MXU)
            dkv_ref[:, (2 * h + 1) * LANE:(2 * h + 2) * LANE] = dv[h][...].astype(MXU)
            dkr_ref[:, :QK_ROPE] += dk[h][:, QK_NOPE:]

    return pl.pallas_call(
        body, name=name, grid=(S // tm,),
        in_specs=[_row_spec(tm, QK_DIM)] * (2 * HEADS) + [_row_spec(tm, V_HEAD)] * HEADS
        + [_row_spec(tm, LANE), _row_spec(tm, LANE)],
        out_specs=(_row_spec(tm, QF), _row_spec(tm, KVF), _row_spec(tm, LANE)),
        out_shape=(jax.ShapeDtypeStruct((S, QF), MXU), jax.ShapeDtypeStruct((S, KVF), MXU),
                   jax.ShapeDtypeStruct((S, LANE), F32)),
        scratch_shapes=[pltpu.VMEM((tm, LANE), F32)], compiler_params=_params(),
    )(*dqs, *dks, *dvs, c128, s128)


def _dot_nt(a, b):
    return lax.dot_general(a, b, (((1,), (1,)), ((), ())), preferred_element_type=F32)


def _dot_tn(a, b):
    return lax.dot_general(a, b, (((0,), (0,)), ((), ())), preferred_element_type=F32)


def _attn_fwd(q, k, vx, *, name, ride=None):
    S = q.shape[0]
    T = _rows(S, 1024)
    n = S // T
    cpt = T // CHUNK

    r_srcs, r_outs, r_pieces = ride or ((), (), ())
    ns, no = len(r_srcs), len(r_outs)

    def body(q_ref, k_ref, v_ref, *rest):
        o_ref, lse_ref = rest[ns:ns + 2]
        s_buf, p_buf, a_buf, m_s, acc_s = rest[ns + 2 + no:ns + 7 + no]
        i = pl.program_id(0)
        if ride:
            start, finish = _exchange_ops(r_pieces, rest[:ns], rest[ns + 2:ns + 2 + no], *rest[ns + 7 + no:])
            pl.when(i == 0)(start)
        qc = lax.broadcasted_iota(jnp.int32, (T, T), 0) // CHUNK
        kc = lax.broadcasted_iota(jnp.int32, (T, T), 1) // CHUNK
        dchunk = kc - qc

        def tile_rows(b):
            return pl.ds(pl.multiple_of(jnp.clip(b, 0, n - 1) * T, T), T)

        def scores(b, slot):
            s = _dot_nt(q_ref[...], k_ref[tile_rows(b), :])
            s_buf[slot] = jnp.where(dchunk <= (i - b) * cpt, s, NEG)

        def softmax(slot):
            s = s_buf[slot]
            m_prev = m_s[...]
            m_new = jnp.maximum(m_prev, jnp.max(s, axis=1, keepdims=True))
            a_buf[slot] = jnp.exp2((m_prev - m_new) * EXP2_SCALE)
            p_buf[slot] = jnp.exp2((s - m_new) * EXP2_SCALE).astype(MXU)
            m_s[...] = m_new

        def pv(b, slot):
            acc_s[...] = a_buf[slot] * acc_s[...] + jnp.dot(p_buf[slot], v_ref[tile_rows(b), :],
                                                              preferred_element_type=F32)

        m_s[...] = jnp.full_like(m_s, NEG)
        acc_s[...] = jnp.zeros_like(acc_s)
        p_buf[1] = jnp.zeros((T, T), MXU)
        a_buf[1] = jnp.ones((T, 1), F32)
        scores(0, 0)

        def pair(u, carry):
            t = 2 * u
            scores(t + 1, 1)
            softmax(0)
            pv(t - 1, 1)
            scores(t + 2, 0)
            softmax(1)
            pv(t, 0)
            return carry

        npairs = (i + 2) // 2
        lax.fori_loop(0, npairs, pair, 0)
        pv(2 * npairs - 1, 1)
        acc = acc_s[...]
        l = acc[:, V_HEAD:V_HEAD + 1]
        o_ref[...] = (acc[:, :V_HEAD] / l).astype(o_ref.dtype)
        lse_ref[...] = m_s[...] * ATT_SCALE + jnp.log(l)
        if ride:
            pl.when(i == n - 1)(finish)

    hbm = pl.BlockSpec(memory_space=pl.ANY)
    return pl.pallas_call(
        body, name=name, grid=(n,),
        in_specs=[pl.BlockSpec((T, QK_DIM), lambda i: (i, 0)), pl.BlockSpec((S, QK_DIM), lambda i: (0, 0)),
                  pl.BlockSpec((S, VX), lambda i: (0, 0))] + [hbm] * ns,
        out_specs=(pl.BlockSpec((T, V_HEAD), lambda i: (i, 0)), pl.BlockSpec((T, 1), lambda i: (i, 0))) + (hbm,) * no,
        out_shape=(jax.ShapeDtypeStruct((S, V_HEAD), MXU), jax.ShapeDtypeStruct((S, 1), F32)) + tuple(r_outs),
        scratch_shapes=[pltpu.VMEM((2, T, T), F32), pltpu.VMEM((2, T, T), MXU), pltpu.VMEM((2, T, 1), F32),
                        pltpu.VMEM((T, 1), F32), pltpu.VMEM((T, VX), F32)]
        + (_exchange_scratch(len(r_pieces)) if ride else []),
        compiler_params=_params(big=True),
    )(q, k, vx, *r_srcs)


def _attn_delta(do, o, *, name):
    S = do.shape[0]
    tm = _rows(S, 512)

    def body(do_ref, o_ref, d_ref):
        for h in range(HEADS):
            cols = slice(h * V_HEAD, (h + 1) * V_HEAD)
            d_ref[:, h:h + 1] = jnp.sum(do_ref[:, cols].astype(F32) * o_ref[:, cols].astype(F32), axis=1,
                                        keepdims=True)

    width = HEADS * V_HEAD
    return pl.pallas_call(
        body, name=name, grid=(S // tm,), in_specs=[_row_spec(tm, width), _row_spec(tm, width)],
        out_specs=_row_spec(tm, HEADS), out_shape=jax.ShapeDtypeStruct((S, HEADS), F32), compiler_params=_params(),
    )(do, o)


BWD_T = 512


def _attn_bwd(q, k, v, do, lse2, delta, *, name, ride=None):
    S = q.shape[0]
    T = _rows(S, BWD_T)
    n = S // T
    cpt = T // CHUNK

    r_srcs, r_outs, r_pieces = ride or ((), (), ())
    ns, no = len(r_srcs), len(r_outs)

    def body(q_hbm, k_ref, v_ref, do_hbm, lse_ref, dl_ref, *rest):
        dq_hbm, dk_ref, dv_ref = rest[ns:ns + 3]
        q_res, do_res, dq_s, s_buf, dp_buf, p_buf, ds_buf, dk_s, dv_s = rest[ns + 3 + no:ns + 12 + no]
        j = pl.program_id(0)
        if ride:
            start, finish = _exchange_ops(r_pieces, rest[:ns], rest[ns + 3:ns + 3 + no], *rest[ns + 12 + no:])
            pl.when(j == 0)(start)

        @pl.when(j == 0)
        def _():
            pltpu.sync_copy(q_hbm, q_res)
            pltpu.sync_copy(do_hbm, do_res)
            dq_s[...] = jnp.zeros_like(dq_s)

        kc = lax.broadcasted_iota(jnp.int32, (T, T), 0) // CHUNK
        qc = lax.broadcasted_iota(jnp.int32, (T, T), 1) // CHUNK
        dchunk = kc - qc

        def tile(t):
            return jnp.clip(j + t, 0, n - 1)

        def rows(t):
            return pl.ds(pl.multiple_of(tile(t) * T, T), T)

        def scores(t, slot):
            visible_up_to = jnp.where(j + t < n, t * cpt, -2 * cpt)
            s = _dot_nt(k_ref[...], q_res[rows(t), :])
            s_buf[slot] = jnp.where(dchunk <= visible_up_to, s, NEG)
            dp_buf[slot] = _dot_nt(v_ref[...], do_res[rows(t), :])

        def probs(t, slot):
            pt = jnp.exp2(s_buf[slot] * EXP2_SCALE - lse_ref[tile(t)])
            p_buf[slot] = pt.astype(MXU)
            ds_buf[slot] = (pt * (dp_buf[slot] - dl_ref[tile(t)]) * ATT_SCALE).astype(MXU)

        def grads(t, slot):
            r = rows(t)
            dv_s[...] += jnp.dot(p_buf[slot], do_res[r, :], preferred_element_type=F32)
            ds = ds_buf[slot]
            dk_s[...] += jnp.dot(ds, q_res[r, :], preferred_element_type=F32)
            dq_s[r, :] += _dot_tn(ds, k_ref[...])

        dk_s[...] = jnp.zeros_like(dk_s)
        dv_s[...] = jnp.zeros_like(dv_s)
        p_buf[1] = jnp.zeros((T, T), MXU)
        ds_buf[1] = jnp.zeros((T, T), MXU)
        scores(0, 0)

        def pair(u, carry):
            t = 2 * u
            scores(t + 1, 1)
            probs(t, 0)
            grads(t - 1, 1)
            scores(t + 2, 0)
            probs(t + 1, 1)
            grads(t, 0)
            return carry

        npairs = (n - j + 1) // 2
        lax.fori_loop(0, npairs, pair, 0)
        grads(2 * npairs - 1, 1)
        dk_ref[...] = dk_s[...]
        dv_ref[...] = dv_s[...]

        @pl.when(j == n - 1)
        def _():
            pltpu.sync_copy(dq_s, dq_hbm)
            if ride:
                finish()

    hbm = pl.BlockSpec(memory_space=pl.ANY)
    k_map = lambda j: (j, 0)
    whole = pl.BlockSpec((n, 1, T), lambda j: (0, 0, 0))
    return pl.pallas_call(
        body, name=name, grid=(n,),
        in_specs=[hbm, pl.BlockSpec((T, QK_DIM), k_map), pl.BlockSpec((T, V_HEAD), k_map), hbm, whole, whole]
        + [hbm] * ns,
        out_specs=(hbm, pl.BlockSpec((T, QK_DIM), k_map), pl.BlockSpec((T, V_HEAD), k_map)) + (hbm,) * no,
        out_shape=(jax.ShapeDtypeStruct((S, QK_DIM), F32), jax.ShapeDtypeStruct((S, QK_DIM), F32),
                   jax.ShapeDtypeStruct((S, V_HEAD), F32)) + tuple(r_outs),
        scratch_shapes=[pltpu.VMEM((S, QK_DIM), MXU), pltpu.VMEM((S, V_HEAD), MXU), pltpu.VMEM((S, QK_DIM), F32),
                        pltpu.VMEM((2, T, T), F32), pltpu.VMEM((2, T, T), F32), pltpu.VMEM((2, T, T), MXU),
                        pltpu.VMEM((2, T, T), MXU), pltpu.VMEM((T, QK_DIM), F32), pltpu.VMEM((T, V_HEAD), F32)]
        + (_exchange_scratch(len(r_pieces)) if ride else []),
        compiler_params=_params(big=True),
    )(q, k, v, do, lse2, delta, *r_srcs)


HALO = 8
CONV_RH = 64


def _conv_tiles(S):
    tm = _rows(S, 256)
    tc = D_FF // 2
    return tm, tc, D_FF // tc


def _silu_parts(gate):
    sg = jax.nn.sigmoid(gate)
    return sg, gate * sg


def _convgate_fwd(up, cw, cb, *, name):
    S = up.shape[0]
    tm, tc, nc = _conv_tiles(S)
    hb = tm // HALO

    def body(v_ref, g_ref, hv_ref, hg_ref, wv_ref, wg_ref, bv_ref, bg_ref, o_ref, cv_ref, cg_ref):
        keep = (pl.program_id(0) > 0).astype(F32)

        def chunk(cc, carry):
            cols = pl.ds(pl.multiple_of(cc * LANE, LANE), LANE)
            wv, wg, bv, bg = wv_ref[:, cols], wg_ref[:, cols], bv_ref[:, cols], bg_ref[:, cols]
            for r0 in range(0, tm, CONV_RH):
                def conv(t_ref, h_ref, w, b):
                    if r0:
                        span = t_ref[pl.ds(r0 - HALO, CONV_RH + HALO), cols]
                    else:
                        span = jnp.concatenate([h_ref[:, cols] * keep, t_ref[pl.ds(0, CONV_RH), cols]], axis=0)
                    back2, back1 = [pltpu.roll(span, s, axis=0)[HALO:] for s in (2, 1)]
                    return w[0:1] * back2 + w[1:2] * back1 + w[2:3] * span[HALO:] + b
                val = conv(v_ref, hv_ref, wv, bv)
                gate = conv(g_ref, hg_ref, wg, bg)
                cv_ref[pl.ds(r0, CONV_RH), cols] = val.astype(cv_ref.dtype)
                cg_ref[pl.ds(r0, CONV_RH), cols] = gate.astype(cg_ref.dtype)
                o_ref[pl.ds(r0, CONV_RH), cols] = (_silu_parts(gate)[1] * val).astype(o_ref.dtype)
            return carry

        lax.fori_loop(0, tc // LANE, chunk, 0)

    prev = lambda i: jnp.maximum(i * hb - 1, 0)
    return pl.pallas_call(
        body, name=name, grid=(S // tm, nc),
        in_specs=[pl.BlockSpec((tm, tc), lambda i, j: (i, j)), pl.BlockSpec((tm, tc), lambda i, j: (i, j + nc)),
                  pl.BlockSpec((HALO, tc), lambda i, j: (prev(i), j)),
                  pl.BlockSpec((HALO, tc), lambda i, j: (prev(i), j + nc)),
                  pl.BlockSpec((3, tc), lambda i, j: (0, j)), pl.BlockSpec((3, tc), lambda i, j: (0, j + nc)),
                  pl.BlockSpec((1, tc), lambda i, j: (0, j)), pl.BlockSpec((1, tc), lambda i, j: (0, j + nc))],
        out_specs=(pl.BlockSpec((tm, tc), lambda i, j: (i, j)),) * 3,
        out_shape=(jax.ShapeDtypeStruct((S, D_FF), MXU),) * 3,
        compiler_params=_params(),
    )(up, up, up, up, cw, cw, cb, cb)


def _convgate_bwd(up, cval, cgate, dact, cw, *, name):
    S = up.shape[0]
    tm, tc, nc = _conv_tiles(S)
    hb = tm // HALO
    nr = S // tm
    R = tm + HALO

    def body(uv_ref, ug_ref, cv_ref, cg_ref, ncv_ref, ncg_ref, da_ref, dan_ref, wv_ref, wg_ref,
             duv_ref, dug_ref, dwv_ref, dwg_ref, dbv_ref, dbg_ref, dsv, dsg):
        i = pl.program_id(1)

        @pl.when(i == 0)
        def _():
            for r in (dwv_ref, dwg_ref, dbv_ref, dbg_ref):
                r[...] = jnp.zeros_like(r)

        keep_next = (i < nr - 1).astype(F32)

        def chunk(cc, carry):
            cols = pl.ds(pl.multiple_of(cc * LANE, LANE), LANE)

            def d_conv(rows, val, gate, d):
                sg, silu = _silu_parts(gate)
                dsv[rows, cols] = d * silu
                dsg[rows, cols] = d * val * (sg * (1.0 + gate * (1.0 - sg)))

            for r0 in range(0, tm, CONV_RH):
                rows = pl.ds(r0, CONV_RH)
                d_conv(rows, cv_ref[rows, cols].astype(F32), cg_ref[rows, cols].astype(F32), da_ref[rows, cols])
            d_conv(pl.ds(tm, HALO), ncv_ref[0:HALO, cols].astype(F32), ncg_ref[0:HALO, cols].astype(F32),
                   dan_ref[:, cols] * keep_next)
            for ds, u_ref, w_ref, du_ref, dw_ref, db_ref in ((dsv, uv_ref, wv_ref, duv_ref, dwv_ref, dbv_ref),
                                                            (dsg, ug_ref, wg_ref, dug_ref, dwg_ref, dbg_ref)):
                w = w_ref[:, cols]
                acc = [jnp.zeros((1, LANE), F32) for _ in range(4)]
                for r0 in range(0, tm, CONV_RH):
                    rows = pl.ds(r0, CONV_RH)
                    span = ds[pl.ds(r0, CONV_RH + HALO), cols]
                    d = [span[:CONV_RH]] + [pltpu.roll(span, CONV_RH + HALO - s, axis=0)[:CONV_RH] for s in (1, 2)]
                    du_ref[rows, cols] = (w[2:3] * d[0] + w[1:2] * d[1] + w[0:1] * d[2]).astype(du_ref.dtype)
                    u = u_ref[rows, cols]
                    for kk in range(3):
                        acc[kk] = acc[kk] + jnp.sum(d[2 - kk] * u, axis=0, keepdims=True)
                    acc[3] = acc[3] + jnp.sum(d[0], axis=0, keepdims=True)
                for kk in range(3):
                    dw_ref[kk:kk + 1, cols] += acc[kk]
                db_ref[:, cols] += acc[3]
            return carry

        lax.fori_loop(0, tc // LANE, chunk, 0)

    nxt = lambda i: jnp.minimum((i + 1) * hb, S // HALO - 1)
    tile_v = pl.BlockSpec((tm, tc), lambda j, i: (i, j))
    tile_g = pl.BlockSpec((tm, tc), lambda j, i: (i, j + nc))
    halo = pl.BlockSpec((HALO, tc), lambda j, i: (nxt(i), j))
    halo16 = pl.BlockSpec((2 * HALO, tc), lambda j, i: (jnp.minimum((i + 1) * (hb // 2), S // (2 * HALO) - 1), j))
    w_v = pl.BlockSpec((3, tc), lambda j, i: (0, j))
    w_g = pl.BlockSpec((3, tc), lambda j, i: (0, j + nc))
    b_v = pl.BlockSpec((1, tc), lambda j, i: (0, j))
    return pl.pallas_call(
        body, name=name, grid=(nc, nr),
        in_specs=[tile_v, tile_g, tile_v, tile_v, halo16, halo16, tile_v, halo, w_v, w_g],
        out_specs=(tile_v, tile_v, w_v, w_v, b_v, b_v),
        out_shape=(jax.ShapeDtypeStruct((S, D_FF), MXU), jax.ShapeDtypeStruct((S, D_FF), MXU),
                   jax.ShapeDtypeStruct((3, D_FF), F32), jax.ShapeDtypeStruct((3, D_FF), F32),
                   jax.ShapeDtypeStruct((1, D_FF), F32), jax.ShapeDtypeStruct((1, D_FF), F32)),
        scratch_shapes=[pltpu.VMEM((R, tc), F32), pltpu.VMEM((R, tc), F32)],
        compiler_params=_params(),
    )(up, up, cval, cgate, cval, cgate, dact, dact, cw, cw)


def _glu_fwd(z, h, *, name):
    S = z.shape[0]
    tm = _rows(S, 512)

    def body(z_ref, h_ref, o_ref):
        o_ref[...] = h_ref[...] + z_ref[:, :D_MODEL] * jax.nn.sigmoid(z_ref[:, D_MODEL:])

    return pl.pallas_call(
        body, name=name, grid=(S // tm,), in_specs=[_row_spec(tm, 2 * D_MODEL), _row_spec(tm, D_MODEL)],
        out_specs=_row_spec(tm, D_MODEL), out_shape=jax.ShapeDtypeStruct((S, D_MODEL), F32),
        compiler_params=_params(),
    )(z, h)


def _glu_bwd(z, dm, *, name):
    S = z.shape[0]
    tm = _rows(S, 512)

    def body(z_ref, dm_ref, o_ref):
        sg = jax.nn.sigmoid(z_ref[:, D_MODEL:])
        dmv = dm_ref[...]
        o_ref[:, :D_MODEL] = (dmv * sg).astype(o_ref.dtype)
        o_ref[:, D_MODEL:] = (dmv * z_ref[:, :D_MODEL] * sg * (1.0 - sg)).astype(o_ref.dtype)

    return pl.pallas_call(
        body, name=name, grid=(S // tm,), in_specs=[_row_spec(tm, 2 * D_MODEL), _row_spec(tm, D_MODEL)],
        out_specs=_row_spec(tm, 2 * D_MODEL), out_shape=jax.ShapeDtypeStruct((S, 2 * D_MODEL), MXU),
        compiler_params=_params(),
    )(z, dm)


GELU_C = math.sqrt(2.0 / math.pi)
GELU_A = 0.044715


def _gelu(y):
    return 0.5 * y * (1.0 + jnp.tanh(GELU_C * (y + GELU_A * (y * y * y))))


def _gelu_bwd(y, dg, *, name):
    S = y.shape[0]
    tm = _rows(S, 512)

    def body(y_ref, dg_ref, o_ref):
        yv = y_ref[...]
        t = jnp.tanh(GELU_C * (yv + GELU_A * (yv * yv * yv)))
        d = 0.5 * (1.0 + t) + 0.5 * yv * (1.0 - t * t) * (GELU_C * (1.0 + 3.0 * GELU_A * (yv * yv)))
        o_ref[...] = dg_ref[...] * d

    return pl.pallas_call(
        body, name=name, grid=(S // tm,), in_specs=[_row_spec(tm, D_MODEL), _row_spec(tm, D_MODEL)],
        out_specs=_row_spec(tm, D_MODEL), out_shape=jax.ShapeDtypeStruct((S, D_MODEL), F32),
        compiler_params=_params(),
    )(y, dg)


FWD_STRIPS = 4
BWD_STRIPS = 8
SW = NSTATE // SBLK
ST2 = 2 * NSTATE


def _s5_fwd(u, wb, wc, abc, dskip, x0, *, full, name):
    S = u.shape[0]
    T = _rows(S, 256 if full else 512)
    nb = S // T
    nj = T // NSEG

    def body(u_ref, wb_ref, wc_ref, a_ref, d_ref, x0_ref, *rest):
        if full:
            xs_ref, y_ref, yg_ref, st = rest
        else:
            e_ref, xs_ref, st = rest
        i = pl.program_id(0)

        @pl.when(i == 0)
        def _():
            st[...] = x0_ref[...]

        uv = u_ref[...]
        ub = uv.astype(MXU)
        for kb in range(SBLK):
            r = jnp.dot(ub[:, kb * LANE:(kb + 1) * LANE], wb_ref[kb], preferred_element_type=F32)
            xs_ref[:, kb * SW:(kb + 1) * SW] = r[:, :SW]
            xs_ref[:, NSTATE + kb * SW:NSTATE + (kb + 1) * SW] = r[:, SW:]
        for sp in range(FWD_STRIPS):
            w = NSTATE // FWD_STRIPS
            re, im = pl.ds(sp * w, w), pl.ds(NSTATE + sp * w, w)
            ar, ai = a_ref[:, re], a_ref[:, im]

            def step(j, c):
                xr, xi = c
                rows = pl.ds(pl.multiple_of(j * NSEG, NSEG), NSEG)
                nr = ar * xr - ai * xi + xs_ref[rows, re]
                ni = ar * xi + ai * xr + xs_ref[rows, im]
                xs_ref[rows, re] = nr
                xs_ref[rows, im] = ni
                return nr, ni

            xr, xi = lax.fori_loop(0, nj, step, (st[:, re], st[:, im]))
            st[:, re] = xr
            st[:, im] = xi
        if full:
            for kb in range(SBLK):
                yk = (jnp.dot(xs_ref[:, kb * SW:(kb + 1) * SW].astype(MXU), wc_ref[kb, :SW, :], preferred_element_type=F32)
                      + jnp.dot(xs_ref[:, NSTATE + kb * SW:NSTATE + (kb + 1) * SW].astype(MXU), wc_ref[kb, SW:, :],
                                preferred_element_type=F32))
                cols = slice(kb * LANE, (kb + 1) * LANE)
                yk = yk + d_ref[:, cols] * uv[:, cols]
                y_ref[:, cols] = yk
                yg_ref[:, cols] = _gelu(yk).astype(yg_ref.dtype)
        else:
            @pl.when(i == nb - 1)
            def _():
                e_ref[...] = st[...]

    in_specs = [_row_spec(T, D_MODEL), pl.BlockSpec((SBLK, LANE, 2 * SW), lambda i: (0, 0, 0)),
                pl.BlockSpec((SBLK, 2 * SW, LANE), lambda i: (0, 0, 0)), _const_spec(NSEG, ST2),
                _const_spec(1, D_MODEL), _const_spec(NSEG, ST2)]
    if full:
        out_specs = (_row_spec(T, ST2), _row_spec(T, D_MODEL), _row_spec(T, D_MODEL))
        out_shape = (jax.ShapeDtypeStruct((S, ST2), F32), jax.ShapeDtypeStruct((S, D_MODEL), F32),
                     jax.ShapeDtypeStruct((S, D_MODEL), MXU))
        scratch = [pltpu.VMEM((NSEG, ST2), F32)]
    else:
        out_specs = _const_spec(NSEG, ST2)
        out_shape = jax.ShapeDtypeStruct((NSEG, ST2), F32)
        scratch = [pltpu.VMEM((T, ST2), F32), pltpu.VMEM((NSEG, ST2), F32)]
    return pl.pallas_call(
        body, name=name, grid=(nb,), in_specs=in_specs, out_specs=out_specs, out_shape=out_shape,
        scratch_shapes=scratch, compiler_params=_params(big=True),
    )(u, wb, wc, abc, dskip, x0)


def _s5_bwd(dy, xs, u, wct, wbt, abc, dskip, x0, l0, *, full, name):
    S = dy.shape[0]
    T = _rows(S, 256 if full else 512)
    nb = S // T
    nj = T // NSEG
    blk = lambda i: nb - 1 - i

    def body(dy_ref, *rest):
        if full:
            (xs_ref, xh_ref, u_ref, wct_ref, wbt_ref, a_ref, d_ref, x0_ref, l0_ref,
             du_ref, da_ref, dwb_hbm, dwc_hbm, dd_ref, g_s, lam_s, dwb_ref, dwc_ref) = rest
        else:
            wct_ref, a_ref, l0_ref, f_ref, g_s, lam_s = rest
        i = pl.program_id(0)

        @pl.when(i == 0)
        def _():
            lam_s[...] = l0_ref[...]
            if full:
                for r in (da_ref, dwb_ref, dwc_ref, dd_ref):
                    r[...] = jnp.zeros_like(r)

        dyv = dy_ref[...]
        dyb = dyv.astype(MXU)
        for kb in range(SBLK):
            r = jnp.dot(dyb[:, kb * LANE:(kb + 1) * LANE], wct_ref[kb], preferred_element_type=F32)
            g_s[:, kb * SW:(kb + 1) * SW] = r[:, :SW]
            g_s[:, NSTATE + kb * SW:NSTATE + (kb + 1) * SW] = r[:, SW:]
        for sp in range(BWD_STRIPS):
            w = NSTATE // BWD_STRIPS
            re, im = pl.ds(sp * w, w), pl.ds(NSTATE + sp * w, w)
            ar, ai = a_ref[:, re], a_ref[:, im]

            def advance(row, lr, li):
                rows = pl.ds(row, NSEG)
                nr = g_s[rows, re] + ar * lr + ai * li
                ni = g_s[rows, im] - ai * lr + ar * li
                g_s[rows, re] = nr
                g_s[rows, im] = ni
                return nr, ni

            def step(jj, c):
                row = pl.multiple_of((nj - 1 - jj) * NSEG, NSEG)
                nr, ni = advance(row, c[0], c[1])
                if not full:
                    return nr, ni
                prow = pl.ds(pl.multiple_of(row - NSEG, NSEG), NSEG)
                xpr, xpi = xs_ref[prow, re], xs_ref[prow, im]
                return nr, ni, c[2] + (nr * xpr + ni * xpi), c[3] + (ni * xpr - nr * xpi)

            init = (lam_s[:, re], lam_s[:, im])
            if full:
                init = init + (jnp.zeros((NSEG, w), F32), jnp.zeros((NSEG, w), F32))
            c = lax.fori_loop(0, nj - 1, step, init)
            lr, li = advance(0, c[0], c[1])
            if full:
                first = (blk(i) == 0)
                xpr = jnp.where(first, x0_ref[:, re], xh_ref[:, re])
                xpi = jnp.where(first, x0_ref[:, im], xh_ref[:, im])
                da_ref[:, re] += c[2] + (lr * xpr + li * xpi)
                da_ref[:, im] += c[3] + (li * xpr - lr * xpi)
            lam_s[:, re] = lr
            lam_s[:, im] = li
        if full:
            uv = u_ref[...]
            ub = uv.astype(MXU)
            dd_ref[...] += jnp.sum(dyv * uv, axis=0, keepdims=True)
            for kb in range(SBLK):
                cols = slice(kb * LANE, (kb + 1) * LANE)
                re = slice(kb * SW, (kb + 1) * SW)
                im = slice(NSTATE + kb * SW, NSTATE + (kb + 1) * SW)
                lr_b = g_s[:, re].astype(MXU)
                li_b = g_s[:, im].astype(MXU)
                duk = (jnp.dot(lr_b, wbt_ref[kb, :SW, :], preferred_element_type=F32)
                       + jnp.dot(li_b, wbt_ref[kb, SW:, :], preferred_element_type=F32))
                du_ref[:, cols] = duk + d_ref[:, cols] * dyv[:, cols]
                dwb_ref[kb, :, :SW] += _dot_tn(ub[:, cols], lr_b)
                dwb_ref[kb, :, SW:] += _dot_tn(ub[:, cols], li_b)
                dwc_ref[kb, :SW, :] += _dot_tn(xs_ref[:, re].astype(MXU), dyb[:, cols])
                dwc_ref[kb, SW:, :] += _dot_tn(xs_ref[:, im].astype(MXU), dyb[:, cols])

            @pl.when(i == nb - 1)
            def _():
                pltpu.sync_copy(dwb_ref, dwb_hbm)
                pltpu.sync_copy(dwc_ref, dwc_hbm)
        else:
            @pl.when(i == nb - 1)
            def _():
                f_ref[...] = lam_s[...]

    rev = lambda c: pl.BlockSpec((T, c), lambda i: (blk(i), 0))
    w3 = lambda a, b: pl.BlockSpec((SBLK, a, b), lambda i: (0, 0, 0))
    if full:
        hb = T // NSEG
        in_specs = [rev(D_MODEL), rev(ST2),
                    pl.BlockSpec((NSEG, ST2), lambda i: (jnp.maximum(blk(i) * hb - 1, 0), 0)),
                    rev(D_MODEL), w3(LANE, 2 * SW), w3(2 * SW, LANE), _const_spec(NSEG, ST2),
                    _const_spec(1, D_MODEL), _const_spec(NSEG, ST2), _const_spec(NSEG, ST2)]
        ops = [dy, xs, xs, u, wct, wbt, abc, dskip, x0, l0]
        hbm = pl.BlockSpec(memory_space=pl.ANY)
        out_specs = (rev(D_MODEL), _const_spec(NSEG, ST2), hbm, hbm, _const_spec(1, D_MODEL))
        out_shape = (jax.ShapeDtypeStruct((S, D_MODEL), F32), jax.ShapeDtypeStruct((NSEG, ST2), F32),
                     jax.ShapeDtypeStruct((SBLK, LANE, 2 * SW), F32), jax.ShapeDtypeStruct((SBLK, 2 * SW, LANE), F32),
                     jax.ShapeDtypeStruct((1, D_MODEL), F32))
    else:
        in_specs = [rev(D_MODEL), w3(LANE, 2 * SW), _const_spec(NSEG, ST2), _const_spec(NSEG, ST2)]
        ops = [dy, wct, abc, l0]
        out_specs = _const_spec(NSEG, ST2)
        out_shape = jax.ShapeDtypeStruct((NSEG, ST2), F32)
    return pl.pallas_call(
        body, name=name, grid=(nb,), in_specs=in_specs, out_specs=out_specs, out_shape=out_shape,
        scratch_shapes=[pltpu.VMEM((T, ST2), F32), pltpu.VMEM((NSEG, ST2), F32)]
        + ([pltpu.VMEM((SBLK, LANE, 2 * SW), F32), pltpu.VMEM((SBLK, 2 * SW, LANE), F32)] if full else []),
        compiler_params=_params(big=True),
    )(*ops)


def _s5_discretize(lr, li, log_dt, br, bi):
    dt = jnp.exp(log_dt)[:, None]
    mag = jnp.exp(lr * dt)
    ar = mag * jnp.cos(li * dt)
    ai = mag * jnp.sin(li * dt)
    den = lr * lr + li * li
    nr = ar - 1.0
    coef_r = (nr * lr + ai * li) / den
    coef_i = (ai * lr - nr * li) / den
    bbar_r = coef_r[..., None] * br - coef_i[..., None] * bi
    bbar_i = coef_r[..., None] * bi + coef_i[..., None] * br
    return ar, ai, bbar_r, bbar_i


def _blockdiag(m):
    gpb = SSM_GROUPS // SBLK
    a, b = m.shape[1:]
    mb = m.reshape(SBLK, gpb, a, b)
    eye = jnp.eye(gpb, dtype=m.dtype)
    return jnp.einsum('kgab,gh->kgahb', mb, eye).reshape(SBLK, gpb * a, gpb * b)


def _blockdiag_extract(w, a, b):
    gpb = SSM_GROUPS // SBLK
    w5 = w.reshape(SBLK, gpb, a, gpb, b)
    return jnp.einsum('kgahb,gh->kgab', w5, jnp.eye(gpb, dtype=w.dtype)).reshape(SSM_GROUPS, a, b)


def _cpow(ar, ai, n):
    rr, ri = jnp.ones_like(ar), jnp.zeros_like(ai)
    br, bi = ar, ai
    while n:
        if n & 1:
            rr, ri = rr * br - ri * bi, rr * bi + ri * br
        br, bi = br * br - bi * bi, 2.0 * br * bi
        n >>= 1
    return rr, ri


def _perm(a):
    s, c = a.shape
    return a.reshape(NSEG, s // NSEG, c).transpose(1, 0, 2).reshape(s, c)


def _unperm(a):
    s, c = a.shape
    return a.reshape(s // NSEG, NSEG, c).transpose(1, 0, 2).reshape(s, c)


def _other_chips(x, y):
    return [(1 - x, y), (x, 1 - y), (1 - x, 1 - y)]


def _span(chip, size, align):
    return pl.ds(pl.multiple_of(chip * size, align), size)


def _exchange_ops(pieces, s_refs, o_refs, send_sems, recv_sems, local_sems):
    x, y, c = lax.axis_index("x"), lax.axis_index("y"), lax.axis_index("c")
    me = 2 * x + y
    others = _other_chips(x, y)
    npc = len(pieces)

    def remote(k, p, tx, ty, src_chip, dst_chip):
        si, oi, sv, dv = pieces[p]
        return pltpu.make_async_remote_copy(
            src_ref=sv(s_refs[si], src_chip), dst_ref=dv(o_refs[oi], dst_chip), send_sem=send_sems.at[k, p],
            recv_sem=recv_sems.at[k, p], device_id=(tx, ty, c), device_id_type=MESH)

    def local(p):
        si, oi, sv, dv = pieces[p]
        return pltpu.make_async_copy(sv(s_refs[si], me), dv(o_refs[oi], me), local_sems.at[p])

    def start():
        for p in range(npc):
            local(p).start()
        for k, (tx, ty) in enumerate(others):
            for p in range(npc):
                remote(k, p, tx, ty, 2 * tx + ty, me).start()

    def finish():
        for k, (tx, ty) in enumerate(others):
            for p in range(npc):
                remote(k, p, tx, ty, me, 2 * tx + ty).wait_recv()
        for k, (tx, ty) in enumerate(others):
            for p in range(npc):
                remote(k, p, tx, ty, 2 * tx + ty, me).wait_send()
        for p in range(npc):
            local(p).wait()

    return start, finish


def _exchange_scratch(npc):
    return [pltpu.SemaphoreType.DMA((NCHIP - 1, npc)), pltpu.SemaphoreType.DMA((NCHIP - 1, npc)),
            pltpu.SemaphoreType.DMA((npc,))]


def _chip_exchange(srcs, out_shapes, pieces, *, name):
    ns, no = len(srcs), len(out_shapes)

    def body(*refs):
        start, finish = _exchange_ops(pieces, refs[:ns], refs[ns:ns + no], *refs[ns + no:])
        start()
        finish()

    hbm = pl.BlockSpec(memory_space=pl.ANY)
    return pl.pallas_call(
        body, name=name, in_specs=[hbm] * ns, out_specs=tuple([hbm] * no), out_shape=tuple(out_shapes),
        scratch_shapes=_exchange_scratch(len(pieces)),
    )(*srcs)


def _sibling_exchange(srcs, *, name):
    n = len(srcs)

    def body(*refs):
        s_refs, o_refs, send_sems, recv_sems = refs[:n], refs[n:2 * n], refs[2 * n], refs[2 * n + 1]
        x, y, c = lax.axis_index("x"), lax.axis_index("y"), lax.axis_index("c")
        cps = [pltpu.make_async_remote_copy(src_ref=s_refs[p], dst_ref=o_refs[p], send_sem=send_sems.at[p],
                                            recv_sem=recv_sems.at[p], device_id=(x, y, 1 - c), device_id_type=MESH)
               for p in range(n)]
        for cp in cps:
            cp.start()
        for cp in cps:
            cp.wait()

    hbm = pl.BlockSpec(memory_space=pl.ANY)
    return pl.pallas_call(
        body, name=name, in_specs=[hbm] * n, out_specs=tuple([hbm] * n),
        out_shape=tuple(jax.ShapeDtypeStruct(s.shape, s.dtype) for s in srcs),
        scratch_shapes=[pltpu.SemaphoreType.DMA((n,)), pltpu.SemaphoreType.DMA((n,))],
    )(*srcs)


def _row_tile(r, c, tile_bytes):
    best = 16
    for t in range(16, r + 1, 16):
        if r % t == 0 and t * c * 4 <= tile_bytes:
            best = t
    assert r % best == 0
    return best


def _sum_chips(r, *, name):
    _, R, W = r.shape
    tm = _row_tile(R, W, 2 * 1024 * 1024)

    def body(r_ref, o_ref):
        o_ref[...] = ((r_ref[0].astype(F32) + r_ref[1].astype(F32)) + r_ref[2].astype(F32)) + r_ref[3].astype(F32)

    return pl.pallas_call(
        body, name=name, grid=(R // tm,), in_specs=[pl.BlockSpec((NCHIP, tm, W), lambda i: (0, i, 0))],
        out_specs=_row_spec(tm, W), out_shape=jax.ShapeDtypeStruct((R, W), F32), compiler_params=_params(),
    )(r)


def _adamw(p_mine, p_sib, w, m, v, *, name):
    R, W = w.shape
    tm = _row_tile(R, W, 1024 * 1024)

    def body(a_ref, b_ref, w_ref, m_ref, v_ref, g_ref, d_ref, nm_ref, nv_ref):
        g = a_ref[...] + b_ref[...]
        mm = ADAM_B1 * m_ref[...] + (1.0 - ADAM_B1) * g
        vv = ADAM_B2 * v_ref[...] + (1.0 - ADAM_B2) * (g * g)
        m_hat = mm / (1.0 - ADAM_B1 ** ADAM_STEP)
        v_hat = vv / (1.0 - ADAM_B2 ** ADAM_STEP)
        g_ref[...] = g
        d_ref[...] = -ADAM_LR * (m_hat / (jnp.sqrt(v_hat) + ADAM_EPS) + ADAM_WD * w_ref[...])
        nm_ref[...] = mm
        nv_ref[...] = vv

    spec = _row_spec(tm, W)
    shp = jax.ShapeDtypeStruct((R, W), F32)
    return pl.pallas_call(
        body, name=name, grid=(R // tm,), in_specs=[spec] * 5, out_specs=(spec,) * 4, out_shape=(shp,) * 4,
        compiler_params=_params(),
    )(p_mine, p_sib, w, m, v)


def _pack_small(parts):
    flat = jnp.concatenate([p.reshape(-1) for p in parts])
    pad = (-flat.shape[0]) % (SMALL_ROWS * PACKW)
    return jnp.pad(flat, (0, pad)).reshape(-1, PACKW)


def _unpack_small(buf, shapes):
    flat, out, off = buf.reshape(-1), [], 0
    for shp in shapes:
        sz = math.prod(shp)
        out.append(flat[off:off + sz].reshape(shp))
        off += sz
    return out


def _shard(a, t, ax):
    sz = a.shape[ax] // NCHIP
    return lax.slice_in_dim(a, t * sz, (t + 1) * sz, axis=ax)


MLA_W = ['mla_w_a', 'mla_w_uq', 'mla_w_ukv', 'mla_w_o']
REST_W = ['ssm_w_in', 'ssm_w_glu', 'ffn_w_up', 'ffn_w_down']
SMALL = [n for n in WNAMES if n not in MLA_W + REST_W]


def _gather_plan(w, names, with_small):
    srcs, outs, pieces = [], [], []
    for name in names:
        local = w[name].astype(MXU)
        local = local[0] if local.shape[0] == 1 else local
        ax = SHARD_AXIS[name] - (1 if w[name].shape[0] == 1 else 0)
        full = local.shape[:ax] + (NCHIP * local.shape[ax],) + local.shape[ax + 1:]
        si, oi = len(srcs), len(outs)
        srcs.append(local)
        outs.append(jax.ShapeDtypeStruct(full, MXU))
        size = local.shape[ax]
        if local.ndim == 2:
            if ax == 0:
                pieces.append((si, oi, lambda r, t: r, lambda r, ch, size=size: r.at[_span(ch, size, 8), :]))
            else:
                pieces.append((si, oi, lambda r, t: r, lambda r, ch, size=size: r.at[:, _span(ch, size, LANE)]))
        else:
            for l in range(local.shape[0]):
                if ax == 1:
                    dv = lambda r, ch, l=l, size=size: r.at[l, _span(ch, size, 8), :]
                else:
                    dv = lambda r, ch, l=l, size=size: r.at[l, :, _span(ch, size, LANE)]
                pieces.append((si, oi, lambda r, t, l=l: r.at[l], dv))
    if with_small:
        small = _pack_small([w[n] for n in GATHER_F32])
        srcs.append(small)
        outs.append(jax.ShapeDtypeStruct((NCHIP,) + small.shape, F32))
        pieces.append((len(srcs) - 1, len(outs) - 1, lambda r, t: r, lambda r, ch: r.at[ch]))
    return srcs, outs, pieces


def _gather_result(got, w, names, with_small):
    full = dict(zip(names, got[:len(names)]))
    if with_small:
        per_chip = [_unpack_small(got[-1][t], [w[n].shape for n in GATHER_F32]) for t in range(NCHIP)]
        for j, n in enumerate(GATHER_F32):
            full[n] = jnp.concatenate([per_chip[t][j] for t in range(NCHIP)], axis=SHARD_AXIS[n])
    return full


def _as_rows(a):
    return a.reshape(-1, a.shape[-1])


def _grad_plan(grads, w, names, with_small):
    srcs, outs, pieces = [], [], []
    for name in names:
        local = w[name]
        ax = SHARD_AXIS[name]
        size = local.shape[ax]
        oi = len(outs)
        layers = grads[name] if isinstance(grads[name], list) else [grads[name]]
        outs.append(jax.ShapeDtypeStruct((NCHIP,) + local.shape, layers[0].dtype))
        for l, g in enumerate(layers):
            si = len(srcs)
            srcs.append(g)
            if ax == 1:
                sv = lambda r, t, size=size: r.at[_span(t, size, 8), :]
            else:
                sv = lambda r, t, size=size: r.at[:, _span(t, size, LANE)]
            pieces.append((si, oi, sv, lambda r, ch, l=l: r.at[ch, l]))
    if with_small:
        small = jnp.stack([_pack_small([_shard(grads[n], t, SHARD_AXIS[n]) if n in SHARD_AXIS else grads[n]
                                        for n in SMALL]) for t in range(NCHIP)])
        srcs.append(small)
        outs.append(jax.ShapeDtypeStruct(small.shape, F32))
        pieces.append((len(srcs) - 1, len(outs) - 1, lambda r, t: r.at[t], lambda r, ch: r.at[ch]))
    return srcs, outs, pieces


def _reduce_and_update(names, landed, w, mom, var):
    partial = [_sum_chips(r.reshape(NCHIP, -1, r.shape[-1]), name="grad_sum_chips") for r in landed]
    sibling = _sibling_exchange(partial, name="grad_sibling")
    res = [dict(), dict(), dict(), dict()]
    for j, name in enumerate(names):
        outs4 = _adamw(partial[j], sibling[j], _as_rows(w[name]), _as_rows(mom[name]), _as_rows(var[name]), name="adamw")
        for d, o in zip(res, outs4):
            d[name] = o.reshape(w[name].shape)
    outs4 = _adamw(partial[-1], sibling[-1], *[_pack_small([t[n] for n in SMALL]) for t in (w, mom, var)], name="adamw")
    for d, o in zip(res, outs4):
        d.update(zip(SMALL, _unpack_small(o, [w[n].shape for n in SMALL])))
    return res


def kernel(x, positions, mla_w_a, mla_g_q, mla_g_kv, mla_w_uq, mla_w_ukv, mla_w_o, ssm_w_in, ssm_lambda_re, ssm_lambda_im, ssm_log_dt, ssm_b_re, ssm_b_im, ssm_c_re, ssm_c_im, ssm_d, ssm_w_glu, ffn_w_up, ffn_conv_w, ffn_conv_b, ffn_w_down, g_mix, g_ffn, g_final, loss_target, m_mla_w_a, m_mla_g_q, m_mla_g_kv, m_mla_w_uq, m_mla_w_ukv, m_mla_w_o, m_ssm_w_in, m_ssm_lambda_re, m_ssm_lambda_im, m_ssm_log_dt, m_ssm_b_re, m_ssm_b_im, m_ssm_c_re, m_ssm_c_im, m_ssm_d, m_ssm_w_glu, m_ffn_w_up, m_ffn_conv_w, m_ffn_conv_b, m_ffn_w_down, m_g_mix, m_g_ffn, m_g_final, v_mla_w_a, v_mla_g_q, v_mla_g_kv, v_mla_w_uq, v_mla_w_ukv, v_mla_w_o, v_ssm_w_in, v_ssm_lambda_re, v_ssm_lambda_im, v_ssm_log_dt, v_ssm_b_re, v_ssm_b_im, v_ssm_c_re, v_ssm_c_im, v_ssm_d, v_ssm_w_glu, v_ffn_w_up, v_ffn_conv_w, v_ffn_conv_b, v_ffn_w_down, v_g_mix, v_g_ffn, v_g_final):
    w = dict(zip(WNAMES, (mla_w_a, mla_g_q, mla_g_kv, mla_w_uq, mla_w_ukv, mla_w_o, ssm_w_in, ssm_lambda_re,
                          ssm_lambda_im, ssm_log_dt, ssm_b_re, ssm_b_im, ssm_c_re, ssm_c_im, ssm_d, ssm_w_glu,
                          ffn_w_up, ffn_conv_w, ffn_conv_b, ffn_w_down, g_mix, g_ffn, g_final)))
    mom = dict(zip(WNAMES, (m_mla_w_a, m_mla_g_q, m_mla_g_kv, m_mla_w_uq, m_mla_w_ukv, m_mla_w_o, m_ssm_w_in,
                            m_ssm_lambda_re, m_ssm_lambda_im, m_ssm_log_dt, m_ssm_b_re, m_ssm_b_im, m_ssm_c_re,
                            m_ssm_c_im, m_ssm_d, m_ssm_w_glu, m_ffn_w_up, m_ffn_conv_w, m_ffn_conv_b,
                            m_ffn_w_down, m_g_mix, m_g_ffn, m_g_final)))
    var = dict(zip(WNAMES, (v_mla_w_a, v_mla_g_q, v_mla_g_kv, v_mla_w_uq, v_mla_w_ukv, v_mla_w_o, v_ssm_w_in,
                            v_ssm_lambda_re, v_ssm_lambda_im, v_ssm_log_dt, v_ssm_b_re, v_ssm_b_im, v_ssm_c_re,
                            v_ssm_c_im, v_ssm_d, v_ssm_w_glu, v_ffn_w_up, v_ffn_conv_w, v_ffn_conv_b,
                            v_ffn_w_down, v_g_mix, v_g_ffn, v_g_final)))
    S = x.shape[1]
    D = D_MODEL
    x2 = x.reshape(S, D)
    tgt = loss_target.reshape(S, D)

    fw = _gather_result(_chip_exchange(*_gather_plan(w, MLA_W, False), name="gather_weights"), w, MLA_W, False)
    w_a = jnp.pad(fw['mla_w_a'], ((0, 0), (0, A_PAD - KR0 - QK_ROPE)))
    uq = fw['mla_w_uq'].reshape(Q_LORA, HEADS, QK_DIM)
    w_uq = jnp.concatenate([uq[:, :, :QK_NOPE].reshape(Q_LORA, HEADS * QK_NOPE),
                            jnp.pad(uq[:, :, QK_NOPE:], ((0, 0), (0, 0), (0, LANE - QK_ROPE))).reshape(Q_LORA, HEADS * LANE)],
                           axis=1)
    w_ukv = fw['mla_w_ukv']
    w_o = fw['mla_w_o']
    conv_b = w['ffn_conv_b']
    g_q, g_kv = w['mla_g_q'], w['mla_g_kv']
    gm, gf = w['g_mix'], w['g_ffn']
    gfin = w['g_final'].reshape(1, D)

    inv = 1.0 / (ROPE_THETA ** (jnp.arange(0, QK_ROPE, 2, dtype=F32) / QK_ROPE))
    ang = positions.reshape(S).astype(F32)[:, None] * inv
    cos, sin = jnp.cos(ang), jnp.sin(ang)
    zpad = jnp.zeros((S, LANE - QK_ROPE), F32)
    c128 = jnp.concatenate([cos, cos, zpad], axis=1)
    s128 = jnp.concatenate([-sin, sin, zpad], axis=1)

    hn0 = _rmsnorm_fwd(x2, gm[0:1], name="rms_mix0")
    a = _mm(hn0, w_a, name="mla_a")
    cqn, ckvn, kr = _mla_mid_fwd(a, g_q, g_kv, c128, s128, name="mla_mid_fwd")
    qfull = _mm(cqn, w_uq, name="mla_q")
    kv = _mm(ckvn, w_ukv, out_dtype=MXU, name="mla_kv")
    qs, ks, vs = _qk_prep(qfull, kv, kr, c128, s128, name="qk_prep")
    os_, lses = [], []
    rides = {0: (['ssm_w_in', 'ssm_w_glu', 'ffn_w_down'], True), 1: (['ffn_w_up'], False)}
    for h in range(HEADS):
        if h in rides:
            o_h, lse_h, *got = _attn_fwd(qs[h], ks[h], vs[h], name="attn_fwd_gather", ride=_gather_plan(w, *rides[h]))
            fw.update(_gather_result(got, w, *rides[h]))
        else:
            o_h, lse_h = _attn_fwd(qs[h], ks[h], vs[h], name="attn_fwd")
        os_.append(o_h)
        lses.append(lse_h)
    w_in = fw['ssm_w_in']
    w_glu = fw['ssm_w_glu']
    w_up = fw['ffn_w_up']
    w_down = fw['ffn_w_down']
    conv_w = fw['ffn_conv_w']
    dskip = fw['ssm_d']
    o_cat = jnp.concatenate(os_, axis=1)
    h1 = _mm(o_cat, w_o, res=x2, name="mla_o")

    def ffn_fwd(h, l):
        hn = _rmsnorm_fwd(h, gf[l:l + 1], name="rms_ffn")
        up = _mm(hn, w_up[l], name="ffn_up")
        act, cval, cgate = _convgate_fwd(up, conv_w[l], conv_b[l:l + 1], name="convgate_fwd")
        return _mm(act, w_down[l], res=h, name="ffn_down"), (hn, up, act, cval, cgate)

    h2, saved0 = ffn_fwd(h1, 0)

    lam_re, lam_im, log_dt = w['ssm_lambda_re'][0], w['ssm_lambda_im'][0], w['ssm_log_dt'][0]
    (a_re, a_im, bbar_r, bbar_i), disc_vjp = jax.vjp(_s5_discretize, lam_re, lam_im, log_dt, w['ssm_b_re'][0],
                                                    w['ssm_b_im'][0])
    c_re, c_im = w['ssm_c_re'][0], w['ssm_c_im'][0]
    bt_r, bt_i = jnp.swapaxes(bbar_r, 1, 2), jnp.swapaxes(bbar_i, 1, 2)
    wb = jnp.concatenate([_blockdiag(bt_r), _blockdiag(bt_i)], axis=2).astype(MXU)
    wbt = jnp.concatenate([_blockdiag(bbar_r), _blockdiag(bbar_i)], axis=1).astype(MXU)
    ct_r, ct_i = jnp.swapaxes(c_re, 1, 2), jnp.swapaxes(c_im, 1, 2)
    wc = jnp.concatenate([_blockdiag(ct_r), _blockdiag(-ct_i)], axis=1).astype(MXU)
    wct = jnp.concatenate([_blockdiag(c_re), _blockdiag(-c_im)], axis=2).astype(MXU)
    af_r, af_i = a_re.reshape(NSTATE), a_im.reshape(NSTATE)
    abc = jnp.broadcast_to(jnp.concatenate([af_r, af_i])[None], (NSEG, ST2))
    seg = S // NSEG
    ap_r, ap_i = _cpow(af_r, af_i, seg)

    hn1 = _rmsnorm_fwd(h2, gm[1:2], name="rms_mix1")
    u = _mm(hn1, w_in, name="s5_in")
    u_p = _perm(u)
    zero_state = jnp.zeros((NSEG, ST2), F32)
    ends = _s5_fwd(u_p, wb, wc, abc, dskip, zero_state, full=False, name="s5_fwd_ends")
    inits, cr, ci = [], jnp.zeros((NSTATE,), F32), jnp.zeros((NSTATE,), F32)
    for r in range(NSEG):
        inits.append(jnp.concatenate([cr, ci]))
        er, ei = ends[r, :NSTATE], ends[r, NSTATE:]
        cr, ci = er + ap_r * cr - ap_i * ci, ei + ap_r * ci + ap_i * cr
    x0 = jnp.stack(inits)
    xs, y_p, yg_p = _s5_fwd(u_p, wb, wc, abc, dskip, x0, full=True, name="s5_fwd")
    yg = _unperm(yg_p)
    z = _mm(yg, w_glu, name="s5_glu")
    h3 = _glu_fwd(z, h2, name="glu_fwd")
    h4, saved1 = ffn_fwd(h3, 1)

    loss_l, dh4, dg_final = _loss_head(h4, gfin, tgt, name="loss_head")

    grads = {}

    def ffn_bwd(h_in, g, saved, l):
        hn, up, act, cval, cgate = saved
        dact = _mm(g, w_down[l], mode="nt", name="ffn_down_dx")
        dw_down = _mm(act, g, mode="tn", out_dtype=MXU, name="ffn_down_dw")
        duv, dug, dwv, dwg, dbv, dbg = _convgate_bwd(up, cval, cgate, dact, conv_w[l], name="convgate_bwd")
        dw_up = jnp.concatenate([_mm(hn, duv, mode="tn", out_dtype=MXU, name="ffn_up_dw"), _mm(hn, dug, mode="tn", out_dtype=MXU, name="ffn_up_dw")],
                                axis=1)
        dhn = _mm(duv, w_up[l][:, :D_FF], mode="nt", pair=(dug, w_up[l][:, D_FF:]), tm=_rows(S, 512), name="ffn_up_dx")
        dh, dg = _rmsnorm_bwd(h_in, gf[l:l + 1], dhn, g, name="rms_ffn_bwd")
        return dh, dict(w_up=dw_up, w_down=dw_down, conv_w=jnp.concatenate([dwv, dwg], axis=1),
                        conv_b=jnp.concatenate([dbv, dbg], axis=1)[0], g_ffn=dg[0])

    dh3, fg1 = ffn_bwd(h3, dh4, saved1, 1)

    dz = _glu_bwd(z, dh3, name="glu_bwd")
    grads['ssm_w_glu'] = _mm(yg, dz, mode="tn", out_dtype=MXU, name="s5_glu_dw")
    dyg = _mm(dz, w_glu, mode="nt", name="s5_glu_dx")
    dy_p = _gelu_bwd(y_p, _perm(dyg), name="gelu_bwd")
    firsts = _s5_bwd(dy_p, None, None, wct, None, abc, None, None, zero_state, full=False, name="s5_bwd_firsts")
    linits, cr, ci = [None] * NSEG, jnp.zeros((NSTATE,), F32), jnp.zeros((NSTATE,), F32)
    for r in reversed(range(NSEG)):
        linits[r] = jnp.concatenate([cr, ci])
        fr, fi = firsts[r, :NSTATE], firsts[r, NSTATE:]
        cr, ci = fr + ap_r * cr + ap_i * ci, fi + ap_r * ci - ap_i * cr
    l0 = jnp.stack(linits)
    du_p, dab, dwb, dwc, dd = _s5_bwd(dy_p, xs, u_p, wct, wbt, abc, dskip, x0, l0, full=True, name="s5_bwd")
    du = _unperm(du_p)
    grads['ssm_w_in'] = _mm(hn1, du, mode="tn", out_dtype=MXU, name="s5_in_dw")
    dhn1 = _mm(du, w_in, mode="nt", name="s5_in_dx")
    dh2, dg_mix1 = _rmsnorm_bwd(h2, gm[1:2], dhn1, dh3, name="rms_mix_bwd")
    da_sum = jnp.sum(dab, axis=0)
    dbt_r = _blockdiag_extract(dwb[:, :, :SW], SSM_GROUP, SSM_STATE)
    dbt_i = _blockdiag_extract(dwb[:, :, SW:], SSM_GROUP, SSM_STATE)
    dlr, dli, dlog_dt, dbr, dbi = disc_vjp((da_sum[:NSTATE].reshape(SSM_GROUPS, SSM_STATE),
                                            da_sum[NSTATE:].reshape(SSM_GROUPS, SSM_STATE),
                                            jnp.swapaxes(dbt_r, 1, 2), jnp.swapaxes(dbt_i, 1, 2)))
    dct_r = _blockdiag_extract(dwc[:, :SW, :], SSM_STATE, SSM_GROUP)
    dct_i = _blockdiag_extract(dwc[:, SW:, :], SSM_STATE, SSM_GROUP)
    grads['ssm_lambda_re'], grads['ssm_lambda_im'], grads['ssm_log_dt'] = dlr[None], dli[None], dlog_dt[None]
    grads['ssm_b_re'], grads['ssm_b_im'] = dbr[None], dbi[None]
    grads['ssm_c_re'] = jnp.swapaxes(dct_r, 1, 2)[None]
    grads['ssm_c_im'] = -jnp.swapaxes(dct_i, 1, 2)[None]
    grads['ssm_d'] = dd

    dh1, fg0 = ffn_bwd(h1, dh2, saved0, 0)
    grads['ffn_w_up'] = [fg0['w_up'], fg1['w_up']]
    grads['ffn_w_down'] = [fg0['w_down'], fg1['w_down']]
    grads['ffn_conv_w'] = jnp.stack([fg0['conv_w'], fg1['conv_w']])
    grads['ffn_conv_b'] = jnp.stack([fg0['conv_b'], fg1['conv_b']])
    grads['g_ffn'] = jnp.stack([fg0['g_ffn'], fg1['g_ffn']])

    do_cat = _mm(dh1, w_o, mode="nt", out_dtype=MXU, name="mla_o_dx")
    grads['mla_w_o'] = _mm(o_cat, dh1, mode="tn", out_dtype=MXU, name="mla_o_dw")
    dqs, dks, dvs = [], [], []
    deltas = _attn_delta(do_cat, o_cat, name="attn_delta")
    for h in range(HEADS):
        do_h = do_cat[:, h * V_HEAD:(h + 1) * V_HEAD]
        delta = deltas[:, h]
        tiles = (S // _rows(S, BWD_T), 1, _rows(S, BWD_T))
        lse2 = (lses[h] * math.log2(math.e)).reshape(tiles)
        if h == 0:
            dq_h, dk_h, dv_h, *landed = _attn_bwd(qs[h], ks[h], vs[h], do_h, lse2, delta.reshape(tiles),
                                                  name="attn_bwd_exchange", ride=_grad_plan(grads, w, REST_W, False))
        else:
            dq_h, dk_h, dv_h = _attn_bwd(qs[h], ks[h], vs[h], do_h, lse2, delta.reshape(tiles), name="attn_bwd")
        dqs.append(dq_h)
        dks.append(dk_h)
        dvs.append(dv_h)
    dqfull, dkv, dkr = _qk_prep_bwd(dqs, dks, dvs, c128, s128, name="qk_prep_bwd")
    dw_uq_p = _mm(cqn, dqfull, mode="tn", out_dtype=MXU, name="mla_q_dw")
    dcqn = _mm(dqfull, w_uq, mode="nt", name="mla_q_dx")
    grads['mla_w_ukv'] = _mm(ckvn, dkv, mode="tn", out_dtype=MXU, name="mla_kv_dw")
    dckvn = _mm(dkv, w_ukv, mode="nt", name="mla_kv_dx")
    da, dgq, dgkv = _mla_mid_bwd(a, dcqn, dckvn, dkr, g_q, g_kv, c128, s128, name="mla_mid_bwd")
    grads['mla_w_a'] = _mm(hn0, da, mode="tn", out_dtype=MXU, name="mla_a_dw")[:, :KR0 + QK_ROPE]
    dhn0 = _mm(da, w_a, mode="nt", name="mla_a_dx")
    dx, dg_mix0 = _rmsnorm_bwd(x2, gm[0:1], dhn0, dh1, name="rms_mix_bwd")
    grads['mla_w_uq'] = jnp.concatenate(
        [dw_uq_p[:, :HEADS * QK_NOPE].reshape(Q_LORA, HEADS, QK_NOPE),
         dw_uq_p[:, HEADS * QK_NOPE:].reshape(Q_LORA, HEADS, LANE)[:, :, :QK_ROPE]], axis=2).reshape(Q_LORA, HEADS * QK_DIM)
    grads['mla_g_q'], grads['mla_g_kv'] = dgq, dgkv
    grads['g_mix'] = jnp.concatenate([dg_mix0, dg_mix1], axis=0)
    grads['g_final'] = dg_final[0]

    landed += _chip_exchange(*_grad_plan(grads, w, MLA_W, True), name="grad_exchange")
    g_out, d_out, m_out, v_out = _reduce_and_update(REST_W + MLA_W, landed, w, mom, var)

    loss = lax.psum(loss_l[0, 0], ("x", "y", "c"))
    return (loss, dx.reshape(1, S, D), *[g_out[n] for n in WNAMES], *[d_out[n] for n in WNAMES],
            *[m_out[n] for n in WNAMES], *[v_out[n] for n in WNAMES])
```
